```python
import jax, jax.numpy as jnp
from jax import lax
import numpy as np

D_MODEL = 1024
BATCH = 8
SEQ = 2048
DEPTH = 1

CTX_LEN = 256
GRID_W = 64

GDN_HEADS = 8
GDN_DK = 128
GDN_DV = 128
GDN_W = GDN_HEADS * GDN_DV
SHORT_CONV = 5
CHUNK = 64
CONF_W = D_MODEL
CONF_KERNEL = 31
N_EXPERTS = 32
TOP_K = 4
D_FF = D_MODEL
SWIGLU_LIMIT = 7.0
SWIGLU_ALPHA = 1.702
MOE_BLOCK = 128
EPS = 1e-6
POS_BASE = 10000.0
F32 = jnp.float32

V_OFF = GDN_HEADS * GDN_DK
BETA_OFF = V_OFF + GDN_W
ALPHA_OFF = BETA_OFF + 2 * GDN_HEADS
STATE_COLS = ALPHA_OFF + 2 * GDN_HEADS
Q_OFF = STATE_COLS
Z_OFF = Q_OFF + GDN_HEADS * GDN_DK
GLU_OFF = Z_OFF + GDN_W
GATE_OFF = GLU_OFF + 2 * CONF_W
N_IN = GATE_OFF + 2 * D_MODEL

kernel_name = 'hybrid_gdn_conformer_moe_dit'


def rms_norm(x, g):
    xf = x.astype(F32)
    y = xf * lax.rsqrt(jnp.mean(xf * xf, axis=-1, keepdims=True) + EPS)
    return (y * g.astype(F32)).astype(x.dtype)


def layer_norm(x, g, b):
    xf = x.astype(F32)
    mu = jnp.mean(xf, axis=-1, keepdims=True)
    var = jnp.mean(jnp.square(xf - mu), axis=-1, keepdims=True)
    y = (xf - mu) * lax.rsqrt(var + EPS) * g.astype(F32) + b.astype(F32)
    return y.astype(x.dtype)


def l2_normalize(x):
    return x * lax.rsqrt(jnp.sum(x * x, axis=-1, keepdims=True) + EPS)


def depthwise_conv(x, w):
    width = w.shape[0]
    return lax.conv_general_dilated(
        x, w[:, None, :].astype(x.dtype), window_strides=(1,),
        padding=[(width // 2, width // 2)],
        dimension_numbers=('NWC', 'WIO', 'NWC'),
        feature_group_count=x.shape[-1])


def grid_pos_embedding(rows, dtype):
    row = jnp.repeat(jnp.arange(rows, dtype=F32), GRID_W)
    col = jnp.tile(jnp.arange(GRID_W, dtype=F32), rows)
    quarter = D_MODEL // 4
    omega = POS_BASE ** (-jnp.arange(quarter, dtype=F32) / quarter)
    def emb(p):
        ang = p[:, None] * omega[None, :]
        return jnp.concatenate([jnp.sin(ang), jnp.cos(ang)], axis=-1)
    return jnp.concatenate([emb(row), emb(col)], axis=-1).astype(dtype)


def modulation(cond, w_mod, b_mod):
    m = jax.nn.silu(cond) @ w_mod + b_mod
    return jnp.split(m[..., None, :], 6, axis=-1)


def gdn_key_value(p_state, conv_kv, a_log, dt_bias):
    b_, l_, _ = p_state.shape
    kv = jax.nn.silu(depthwise_conv(p_state[..., :BETA_OFF], conv_kv)).astype(F32)
    k = l2_normalize(kv[..., :V_OFF].reshape(b_, l_, GDN_HEADS, GDN_DK))
    v = kv[..., V_OFF:].reshape(b_, l_, GDN_HEADS, GDN_DV)
    beta = jax.nn.sigmoid(p_state[..., BETA_OFF:ALPHA_OFF].astype(F32)).reshape(b_, l_, 2, GDN_HEADS)
    alpha = p_state[..., ALPHA_OFF:STATE_COLS].astype(F32).reshape(b_, l_, 2, GDN_HEADS)
    g = -jnp.exp(a_log.astype(F32)) * jax.nn.softplus(alpha + dt_bias.astype(F32))
    return k, v, beta, g


def gdn_query(p_q, conv_q):
    b_, l_, _ = p_q.shape
    q = jax.nn.silu(depthwise_conv(p_q, conv_q)).astype(F32).reshape(b_, l_, GDN_HEADS, GDN_DK)
    return l2_normalize(q) * GDN_DK ** -0.5


def chunk_gated_delta(k, v, beta, g, s0, q=None):
    b_, l_, h_, _ = k.shape
    dv = v.shape[-1]
    n = l_ // CHUNK
    def to_chunks(t):
        return t.reshape(b_, n, CHUNK, h_, -1).transpose(1, 0, 3, 2, 4)
    kc, vc = to_chunks(k), to_chunks(v)
    bc = to_chunks(beta[..., None])[..., 0]
    gc = jnp.cumsum(to_chunks(g[..., None])[..., 0], axis=-1)
    idx = jnp.arange(CHUNK)
    lower_incl = idx[:, None] >= idx[None, :]
    strict = idx[:, None] > idx[None, :]
    decay = jnp.exp(jnp.where(lower_incl, gc[..., :, None] - gc[..., None, :], -jnp.inf))
    k_beta = kc * bc[..., None]
    a_strict = jnp.where(strict, jnp.einsum('nbhik,nbhjk->nbhij', k_beta, kc) * decay, 0.0)
    rhs = jnp.concatenate([vc * bc[..., None], k_beta * jnp.exp(gc)[..., None]], axis=-1)
    sol = lax.linalg.triangular_solve(a_strict, rhs, left_side=True, lower=True, unit_diagonal=True)
    value, k_cum = sol[..., :dv], sol[..., dv:]
    k_tail = kc * jnp.exp(gc[..., -1:] - gc)[..., None]
    chunk_decay = jnp.exp(gc[..., -1])
    xs = (k_tail, value, k_cum, chunk_decay)
    if q is not None:
        qc = to_chunks(q)
        attn = jnp.einsum('nbhik,nbhjk->nbhij', qc, kc) * decay
        xs = xs + (qc, gc, attn)

    def step(state, xs_c):
        k_t, val, kcum, dec = xs_c[:4]
        v_new = val - jnp.einsum('bhck,bhkv->bhcv', kcum, state)
        new_state = state * dec[..., None, None] + jnp.einsum('bhck,bhcv->bhkv', k_t, v_new)
        if q is None:
            return new_state, None
        q_t, g_t, att = xs_c[4:]
        o = (jnp.einsum('bhck,bhkv->bhcv', q_t * jnp.exp(g_t)[..., None], state)
             + jnp.einsum('bhij,bhjv->bhiv', att, v_new))
        return new_state, o

    s_final, o = lax.scan(step, s0, xs)
    if q is None:
        return None, s_final
    return o.transpose(1, 0, 3, 2, 4).reshape(b_, l_, h_, dv), s_final


def gdn_bidirectional(k, v, beta, g, s0_f, s0_b, q=None):
    flip = lambda t: jnp.flip(t, axis=1)
    o_f, s_f = chunk_gated_delta(k, v, beta[:, :, 0], g[:, :, 0], s0_f, q)
    o_b, s_b = chunk_gated_delta(flip(k), flip(v), flip(beta[:, :, 1]), flip(g[:, :, 1]), s0_b,
                                 None if q is None else flip(q))
    o = None if q is None else o_f + flip(o_b)
    return o, s_f, s_b


def mixer(h, mp, s0_f, s0_b):
    (w_in, conv_kv, conv_q, a_log, dt_bias, gdn_norm_g, w_proj_a,
     conf_dw, conf_dw_b, conf_ln_g, conf_ln_b, w_proj_b, w_out) = mp
    b_, l_, _ = h.shape
    proj = h @ w_in
    k, v, beta, g = gdn_key_value(proj[..., :STATE_COLS], conv_kv, a_log, dt_bias)
    q = gdn_query(proj[..., Q_OFF:Z_OFF], conv_q)
    o, s_f, s_b = gdn_bidirectional(k, v, beta, g, s0_f, s0_b, q)
    z = proj[..., Z_OFF:GLU_OFF].astype(F32).reshape(b_, l_, GDN_HEADS, GDN_DV)
    o = (rms_norm(o, gdn_norm_g) * jax.nn.silu(z)).reshape(b_, l_, GDN_W).astype(h.dtype)
    y_a = o @ w_proj_a
    glu = proj[..., GLU_OFF:GATE_OFF]
    u = glu[..., :CONF_W] * jax.nn.sigmoid(glu[..., CONF_W:])
    u = depthwise_conv(u, conf_dw) + conf_dw_b
    u = jax.nn.silu(layer_norm(u, conf_ln_g, conf_ln_b))
    y_b = u @ w_proj_b
    gates = jax.nn.sigmoid(proj[..., GATE_OFF:])
    merged = gates[..., :D_MODEL] * y_a + gates[..., D_MODEL:] * y_b
    return merged @ w_out, s_f, s_b


def moe(h, w_router, b_router, w_gate_up, b_gate_up, w_down, b_down):
    b_, l_, d = h.shape
    n_tok = b_ * l_
    xt = h.reshape(n_tok, d)
    logits = (xt @ w_router).astype(F32) + b_router.astype(F32)
    top_logit, top_idx = lax.top_k(logits, TOP_K)
    top_w = jax.nn.softmax(top_logit, axis=-1)
    n_assign = n_tok * TOP_K
    flat_e = top_idx.reshape(-1)
    order = jnp.argsort(flat_e)
    sorted_e = flat_e[order]
    sorted_tok = (order // TOP_K).astype(jnp.int32)
    sorted_w = top_w.reshape(-1)[order]
    counts = jnp.zeros((N_EXPERTS,), jnp.int32).at[flat_e].add(1)
    padded = (counts + MOE_BLOCK - 1) // MOE_BLOCK * MOE_BLOCK
    start = jnp.cumsum(counts) - counts
    padded_end = jnp.cumsum(padded)
    padded_start = padded_end - padded
    dest = padded_start[sorted_e] + jnp.arange(n_assign, dtype=jnp.int32) - start[sorted_e]
    n_rows = -(-n_assign // MOE_BLOCK) * MOE_BLOCK + N_EXPERTS * MOE_BLOCK
    n_blocks = n_rows // MOE_BLOCK
    row_tok = jnp.full((n_rows,), n_tok, jnp.int32).at[dest].set(sorted_tok)
    row_w = jnp.zeros((n_rows,), F32).at[dest].set(sorted_w)
    block_expert = jnp.minimum(
        jnp.searchsorted(padded_end, jnp.arange(n_blocks, dtype=jnp.int32) * MOE_BLOCK, side='right'),
        N_EXPERTS - 1)
    x_pad = jnp.concatenate([xt, jnp.zeros((1, d), xt.dtype)], axis=0)
    xb = x_pad[row_tok].reshape(n_blocks, MOE_BLOCK, d)

    def expert_block(args):
        xe, e = args
        gu = xe @ w_gate_up[e] + b_gate_up[e]
        gl = jnp.minimum(gu[..., :D_FF], SWIGLU_LIMIT)
        lin = jnp.clip(gu[..., D_FF:], -SWIGLU_LIMIT, SWIGLU_LIMIT)
        act = gl * jax.nn.sigmoid(SWIGLU_ALPHA * gl) * (lin + 1)
        return act @ w_down[e] + b_down[e]

    yb = lax.map(expert_block, (xb, block_expert))
    y = jnp.zeros((n_tok + 1, d), F32).at[row_tok].add(yb.reshape(n_rows, d).astype(F32) * row_w[:, None])
    return y[:n_tok].reshape(b_, l_, d).astype(h.dtype)


def block(x, mod6, norms, mp, ffn_p, s0_f, s0_b):
    shift_m, scale_m, gate_m, shift_f, scale_f, gate_f = mod6
    g_pre_m, g_post_m, g_pre_f, g_post_f = norms
    h = rms_norm(x, g_pre_m) * (1 + scale_m) + shift_m
    y, s_f, s_b = mixer(h, mp, s0_f, s0_b)
    x = x + gate_m * rms_norm(y, g_post_m)
    h = rms_norm(x, g_pre_f) * (1 + scale_f) + shift_f
    x = x + gate_f * rms_norm(moe(h, *ffn_p), g_post_f)
    return x, s_f, s_b


def context_states(ctx, mod6, g_pre_m, mp):
    shift_m, scale_m = mod6[0], mod6[1]
    h = rms_norm(ctx, g_pre_m) * (1 + scale_m) + shift_m
    w_in, conv_kv, a_log, dt_bias = mp[0], mp[1], mp[3], mp[4]
    k, v, beta, g = gdn_key_value(h @ w_in[:, :STATE_COLS], conv_kv, a_log, dt_bias)
    zero = jnp.zeros((ctx.shape[0], GDN_HEADS, GDN_DK, GDN_DV), F32)
    _, s_f, s_b = gdn_bidirectional(k, v, beta, g, zero, zero)
    return s_f, s_b


def setup_inputs(seed: int = 0) -> dict:
    key = jax.random.key(seed)
    ks = jax.random.split(key, 32)
    def nrm(k, shape, scale):
        return jax.random.normal(k, shape, F32) * scale
    def gain(k, shape):
        return 1.0 + 0.05 * jax.random.normal(k, shape, F32)
    dt = jnp.exp(jax.random.uniform(ks[10], (DEPTH, 2, GDN_HEADS), F32, np.log(1e-3), np.log(1e-1)))
    return {
        'x': nrm(ks[0], (BATCH, SEQ, D_MODEL), 1.0),
        'c': nrm(ks[1], (BATCH, D_MODEL), 1.0),
        'ctx': nrm(ks[2], (BATCH, CTX_LEN, D_MODEL), 1.0),
        'c_ctx': nrm(ks[3], (D_MODEL,), 1.0),
        'w_mod': nrm(ks[4], (DEPTH, D_MODEL, 6 * D_MODEL), 0.5 * D_MODEL ** -0.5),
        'b_mod': nrm(ks[5], (DEPTH, 6 * D_MODEL), 0.01),
        'g_pre_mix': gain(ks[6], (DEPTH, D_MODEL)),
        'g_post_mix': gain(ks[7], (DEPTH, D_MODEL)),
        'g_pre_ffn': gain(ks[8], (DEPTH, D_MODEL)),
        'g_post_ffn': gain(ks[9], (DEPTH, D_MODEL)),
        'w_in': nrm(ks[11], (DEPTH, D_MODEL, N_IN), D_MODEL ** -0.5),
        'conv_kv': nrm(ks[12], (DEPTH, SHORT_CONV, 2 * GDN_W), SHORT_CONV ** -0.5),
        'conv_q': nrm(ks[13], (DEPTH, SHORT_CONV, GDN_HEADS * GDN_DK), SHORT_CONV ** -0.5),
        'a_log': jnp.log(jax.random.uniform(ks[14], (DEPTH, 2, GDN_HEADS), F32, 1.0, 16.0)),
        'dt_bias': dt + jnp.log(-jnp.expm1(-dt)),
        'gdn_norm_g': gain(ks[15], (DEPTH, GDN_DV)),
        'w_proj_a': nrm(ks[16], (DEPTH, GDN_W, D_MODEL), GDN_W ** -0.5),
        'conf_dw': nrm(ks[17], (DEPTH, CONF_KERNEL, CONF_W), CONF_KERNEL ** -0.5),
        'conf_dw_b': nrm(ks[18], (DEPTH, CONF_W), 0.01),
        'conf_ln_g': gain(ks[19], (DEPTH, CONF_W)),
        'conf_ln_b': nrm(ks[20], (DEPTH, CONF_W), 0.01),
        'w_proj_b': nrm(ks[21], (DEPTH, CONF_W, D_MODEL), CONF_W ** -0.5),
        'w_out': nrm(ks[22], (DEPTH, D_MODEL, D_MODEL), D_MODEL ** -0.5),
        'w_router': nrm(ks[23], (DEPTH, D_MODEL, N_EXPERTS), D_MODEL ** -0.5),
        'b_router': nrm(ks[24], (DEPTH, N_EXPERTS), 0.01),
        'w_gate_up': nrm(ks[25], (DEPTH, N_EXPERTS, D_MODEL, 2 * D_FF), D_MODEL ** -0.5),
        'b_gate_up': nrm(ks[26], (DEPTH, N_EXPERTS, 2 * D_FF), 0.01),
        'w_down': nrm(ks[27], (DEPTH, N_EXPERTS, D_FF, D_MODEL), D_FF ** -0.5),
        'b_down': nrm(ks[28], (DEPTH, N_EXPERTS, D_MODEL), 0.01),
    }


def reference(x, c, ctx, c_ctx, w_mod, b_mod, g_pre_mix, g_post_mix, g_pre_ffn, g_post_ffn,
              w_in, conv_kv, conv_q, a_log, dt_bias, gdn_norm_g, w_proj_a,
              conf_dw, conf_dw_b, conf_ln_g, conf_ln_b, w_proj_b, w_out,
              w_router, b_router, w_gate_up, b_gate_up, w_down, b_down):
    rows = x.shape[1] // GRID_W
    x = x + grid_pos_embedding(rows, x.dtype)[None]
    for layer in range(DEPTH):
        mod_lat = modulation(c, w_mod[layer], b_mod[layer])
        mod_ctx = modulation(c_ctx, w_mod[layer], b_mod[layer])
        norms = (g_pre_mix[layer], g_post_mix[layer], g_pre_ffn[layer], g_post_ffn[layer])
        mp = (w_in[layer], conv_kv[layer], conv_q[layer], a_log[layer], dt_bias[layer],
              gdn_norm_g[layer], w_proj_a[layer], conf_dw[layer], conf_dw_b[layer],
              conf_ln_g[layer], conf_ln_b[layer], w_proj_b[layer], w_out[layer])
        ffn_p = (w_router[layer], b_router[layer], w_gate_up[layer], b_gate_up[layer],
                 w_down[layer], b_down[layer])
        if layer + 1 < DEPTH:
            zero = jnp.zeros((ctx.shape[0], GDN_HEADS, GDN_DK, GDN_DV), F32)
            ctx_next, s_f, s_b = block(ctx, mod_ctx, norms, mp, ffn_p, zero, zero)
        else:
            s_f, s_b = context_states(ctx, mod_ctx, norms[0], mp)
            ctx_next = ctx
        x, _, _ = block(x, mod_lat, norms, mp, ffn_p, s_f, s_b)
        ctx = ctx_next
    return x
```

```python
import functools

import jax
import jax.numpy as jnp
from jax import lax
from jax.experimental import pallas as pl
from jax.experimental.pallas import tpu as pltpu

F32 = jnp.float32
BF16 = jnp.bfloat16

D_MODEL = 1024
GRID_W = 64
HEADS = 8
DK = 128
DV = 128
SHORT_CONV = 5
CHUNK = 64
CONF_KERNEL = 31
N_EXPERTS = 32
TOP_K = 4
SWIGLU_LIMIT = 7.0
SWIGLU_ALPHA = 1.702
MOE_BLOCK = 128
EPS = 1e-6
POS_BASE = 10000.0

LANES = 128
SUBLANES = 8
VMEM_LIMIT = 56 * 1024 * 1024

COL_K, COL_V, COL_Q, COL_Z, COL_GLU_A, COL_GLU_G, COL_GATE_A, COL_GATE_B = range(8)


def _params(*sem):
    return pltpu.CompilerParams(dimension_semantics=sem, vmem_limit_bytes=VMEM_LIMIT)


def _mod_kernel(c_ref, w_ref, b_ref, o_ref):
    c = c_ref[...]
    s = c * jax.nn.sigmoid(c)
    o_ref[...] = jnp.dot(s, w_ref[...], preferred_element_type=F32,
                         precision=lax.Precision.HIGHEST) + b_ref[...]


def modulation(cond, w_mod, b_mod):
    r, d = cond.shape
    n = w_mod.shape[1]
    tn = 1024
    return pl.pallas_call(
        _mod_kernel,
        grid=(n // tn,),
        in_specs=[pl.BlockSpec((r, d), lambda j: (0, 0)),
                  pl.BlockSpec((d, tn), lambda j: (0, j)),
                  pl.BlockSpec((1, tn), lambda j: (0, j))],
        out_specs=pl.BlockSpec((r, tn), lambda j: (0, j)),
        out_shape=jax.ShapeDtypeStruct((r, n), F32),
        compiler_params=_params("arbitrary"),
        name="modulation",
    )(cond, w_mod, b_mod.reshape(1, n))


def _inproj_kernel(*refs, has_pos):
    if has_pos:
        x_ref, pos_ref, sh_ref, sc_ref, g_ref, w_ref, wba_ref, o_ref, ba_ref, h_scr = refs
    else:
        x_ref, sh_ref, sc_ref, g_ref, w_ref, wba_ref, o_ref, ba_ref, h_scr = refs

    @pl.when(pl.program_id(1) == 0)
    def _():
        x = x_ref[...]
        if has_pos:
            x = x + pos_ref[...]
        y = x * lax.rsqrt(jnp.mean(x * x, axis=-1, keepdims=True) + EPS) * g_ref[...]
        h = (y * (1.0 + sc_ref[0]) + sh_ref[0]).astype(BF16)
        h_scr[...] = h
        ba_ref[...] = jnp.dot(h, wba_ref[...], preferred_element_type=F32)

    o_ref[...] = jnp.dot(h_scr[...], w_ref[...], preferred_element_type=F32).astype(o_ref.dtype)


def input_projection(x, pos, mod, mod_row_of_tile, g_pre, w_main, w_ba, *, tm, tn):
    n, d = x.shape
    w = w_main.shape[1]
    has_pos = pos is not None
    in_specs = [pl.BlockSpec((tm, d), lambda i, j: (i, 0))]
    args = [x]
    if has_pos:
        pos_tiles = pos.shape[0] // tm
        in_specs.append(pl.BlockSpec((tm, d), lambda i, j: (i % pos_tiles, 0)))
        args.append(pos)
    in_specs += [
        pl.BlockSpec((1, 1, d), lambda i, j: (mod_row_of_tile(i), 0, 0)),
        pl.BlockSpec((1, 1, d), lambda i, j: (mod_row_of_tile(i), 0, 1)),
        pl.BlockSpec((1, d), lambda i, j: (0, 0)),
        pl.BlockSpec((d, tn), lambda i, j: (0, j)),
        pl.BlockSpec((d, LANES), lambda i, j: (0, 0)),
    ]
    args += [mod, mod, g_pre.reshape(1, d), w_main, w_ba]
    return pl.pallas_call(
        functools.partial(_inproj_kernel, has_pos=has_pos),
        grid=(n // tm, w // tn),
        in_specs=in_specs,
        out_specs=[pl.BlockSpec((tm, tn), lambda i, j: (i, j)),
                   pl.BlockSpec((tm, LANES), lambda i, j: (i, 0))],
        out_shape=[jax.ShapeDtypeStruct((n, w), BF16), jax.ShapeDtypeStruct((n, LANES), F32)],
        scratch_shapes=[pltpu.VMEM((tm, d), BF16)],
        compiler_params=_params("arbitrary", "arbitrary"),
        name="input_projection_pos" if has_pos else "input_projection_ctx",
    )(*args)


def _silu(x):
    return x * jax.nn.sigmoid(x)


def _softplus(x):
    return jnp.maximum(x, 0.0) + jnp.log(1.0 + jnp.exp(-jnp.abs(x)))


def _short_conv(src_ref, w_ref, pad_scr, rows):
    half = SHORT_CONV // 2
    pad_scr[0:SUBLANES, :] = jnp.zeros((SUBLANES, LANES), F32)
    pad_scr[SUBLANES:SUBLANES + rows, :] = src_ref[...].astype(F32)
    pad_scr[SUBLANES + rows:2 * SUBLANES + rows, :] = jnp.zeros((SUBLANES, LANES), F32)
    acc = None
    for j in range(SHORT_CONV):
        off = SUBLANES + j - half
        term = pad_scr[off:off + rows, :] * w_ref[j:j + 1, :]
        acc = term if acc is None else acc + term
    return _silu(acc)


def _l2n(x):
    return x * lax.rsqrt(jnp.sum(x * x, axis=-1, keepdims=True) + EPS)


def _gdn_kernel(k_ref, v_ref, q_ref, z_ref, ba_ref, kc_ref, vc_ref, bac_ref,
                wk_ref, wv_ref, wq_ref, ab_ref, gn_ref, o_ref,
                kf, vf, qf, gates, pad_scr, lhs, val, att, ktt, dec, osc, *, seq, ctx):
    h = pl.program_id(1)
    total = ctx + seq
    nc = total // CHUNK
    ncc = ctx // CHUNK

    kf[0:ctx, :] = _l2n(_short_conv(kc_ref, wk_ref, pad_scr, ctx))
    vf[0:ctx, :] = _short_conv(vc_ref, wv_ref, pad_scr, ctx)
    qf[0:ctx, :] = jnp.zeros((ctx, LANES), F32)
    kf[ctx:total, :] = _l2n(_short_conv(k_ref, wk_ref, pad_scr, seq))
    vf[ctx:total, :] = _short_conv(v_ref, wv_ref, pad_scr, seq)
    qf[ctx:total, :] = _l2n(_short_conv(q_ref, wq_ref, pad_scr, seq)) * (DK ** -0.5)

    def gate_cols(src_ref, lo, rows):
        x = src_ref[...]
        lane = lax.broadcasted_iota(jnp.int32, (rows, LANES), 1)
        beta = jax.nn.sigmoid(x)
        g = -jnp.exp(ab_ref[0:1, :]) * _softplus(x + ab_ref[1:2, :])
        for slot, (arr, base) in enumerate(((beta, 0), (beta, HEADS), (g, 2 * HEADS), (g, 3 * HEADS))):
            col = jnp.sum(jnp.where(lane == base + h, arr, 0.0), axis=-1, keepdims=True)
            gates[slot, lo:lo + rows, :] = jnp.broadcast_to(col, (rows, LANES))

    gate_cols(bac_ref, 0, ctx)
    gate_cols(ba_ref, ctx, seq)

    ri = lax.broadcasted_iota(jnp.int32, (CHUNK, CHUNK), 0)
    ci = lax.broadcasted_iota(jnp.int32, (CHUNK, CHUNK), 1)
    eye = jnp.where(ri == ci, 1.0, 0.0)
    tri_f = jnp.where(ri >= ci, 1.0, 0.0)
    tri_b = jnp.where(ri <= ci, 1.0, 0.0)
    level_masks = []
    for lg in range(CHUNK.bit_length() - 1):
        same_parent = (ri >> (lg + 1)) == (ci >> (lg + 1))
        level_masks.append(jnp.where(same_parent & ((ri >> lg) != (ci >> lg)), 1.0, 0.0))

    def chunk_body(c, carry):
        r0 = pl.multiple_of(c * CHUNK, CHUNK)
        k = kf[pl.ds(r0, CHUNK), :]
        v = vf[pl.ds(r0, CHUNK), :]
        q = qf[pl.ds(r0, CHUNK), :]
        kb = k.astype(BF16)
        kq = jnp.concatenate([kb, q.astype(BF16)], axis=0)
        kk_qk = lax.dot_general(kq, kb, (((1,), (1,)), ((), ())), preferred_element_type=F32)
        kk = kk_qk[:CHUNK]
        qk = kk_qk[CHUNK:]
        for d in range(2):
            beta = gates[d, pl.ds(r0, CHUNK), :]
            g = gates[2 + d, pl.ds(r0, CHUNK), :]
            tri = tri_f if d == 0 else tri_b
            gc = jnp.dot(tri, g, preferred_element_type=F32, precision=lax.Precision.HIGHEST)
            gc_row = jnp.transpose(gc)[:CHUNK, :]
            gc_col = gc[:, :CHUNK]
            incl = (ri >= ci) if d == 0 else (ri <= ci)
            decay = jnp.where(incl, jnp.exp(gc_col - gc_row), 0.0)
            a = jnp.where(ri == ci, 0.0, beta[:, :CHUNK] * kk * decay)
            t = eye - a * level_masks[0]
            for m in level_masks[1:]:
                cs = (a * m).astype(BF16)
                tb = t.astype(BF16)
                tc = jnp.dot(tb, cs, preferred_element_type=F32)
                t = t - jnp.dot(tc.astype(BF16), tb, preferred_element_type=F32)
            eg = jnp.exp(gc)
            rhs = jnp.concatenate([v * beta, k * beta * eg], axis=1).astype(BF16)
            sol = jnp.dot(t.astype(BF16), rhs, preferred_element_type=F32)
            g_end = gc[CHUNK - 1:CHUNK, :] if d == 0 else gc[0:1, :]
            k_tail = k * jnp.exp(g_end - gc)
            l0 = pl.multiple_of(c * 2 * CHUNK, 2 * CHUNK)
            lhs[d, pl.ds(l0, CHUNK), :] = sol[:, DV:].astype(BF16)
            lhs[d, pl.ds(l0 + CHUNK, CHUNK), :] = (q * eg).astype(BF16)
            val[d, pl.ds(r0, CHUNK), :] = sol[:, :DV]
            att[d, pl.ds(r0, CHUNK), :] = (qk * decay).astype(BF16)
            ktt[d, pl.ds(l0, 2 * CHUNK), :] = jnp.transpose(k_tail)[:, :CHUNK].astype(BF16)
            dec[d, pl.ds(pl.multiple_of(c * SUBLANES, SUBLANES), SUBLANES), :] = jnp.broadcast_to(
                jnp.exp(g_end), (SUBLANES, LANES))
        return carry

    lax.fori_loop(0, nc, chunk_body, 0)

    def step(state, d, c):
        r0 = pl.multiple_of(c * CHUNK, CHUNK)
        l0 = pl.multiple_of(c * 2 * CHUNK, 2 * CHUNK)
        r = jnp.dot(lhs[d, pl.ds(l0, 2 * CHUNK), :], state.astype(BF16), preferred_element_type=F32)
        v_new = (val[d, pl.ds(r0, CHUNK), :] - r[:CHUNK]).astype(BF16)
        o = r[CHUNK:] + jnp.dot(att[d, pl.ds(r0, CHUNK), :], v_new, preferred_element_type=F32)
        osc[d, pl.ds(r0, CHUNK), :] = o
        decay = dec[d, pl.ds(pl.multiple_of(c * SUBLANES, SUBLANES), SUBLANES), :][0:1, :]
        return state * decay + jnp.dot(ktt[d, pl.ds(l0, 2 * CHUNK), :], v_new, preferred_element_type=F32)

    def scan_body(t, carry):
        s_f, s_b = carry
        c_b = jnp.where(t < ncc, ncc - 1 - t, nc + ncc - 1 - t)
        return step(s_f, 0, t), step(s_b, 1, c_b)

    zero = jnp.zeros((DK, DV), F32)
    lax.fori_loop(0, nc, scan_body, (zero, zero))

    o = osc[0, ctx:total, :] + osc[1, ctx:total, :]
    o = o * lax.rsqrt(jnp.mean(o * o, axis=-1, keepdims=True) + EPS) * gn_ref[...]
    o_ref[...] = (o * _silu(z_ref[...].astype(F32))).astype(o_ref.dtype)


def gated_deltanet(proj, ba, proj_ctx, ba_ctx, conv_kv, conv_q, a_log, dt_bias, gn, *, batch, seq, ctx):
    total = seq + ctx
    nc = total // CHUNK
    taps = jnp.zeros((SUBLANES, conv_kv.shape[1]), F32).at[:SHORT_CONV].set(conv_kv)
    taps_q = jnp.zeros((SUBLANES, conv_q.shape[1]), F32).at[:SHORT_CONV].set(conv_q)
    ab = jnp.zeros((SUBLANES, LANES), F32)
    ab = ab.at[0, 2 * HEADS:4 * HEADS].set(a_log.reshape(-1)).at[1, 2 * HEADS:4 * HEADS].set(dt_bias.reshape(-1))
    col = lambda base: (lambda b, h: (b, base * HEADS + h))
    return pl.pallas_call(
        functools.partial(_gdn_kernel, seq=seq, ctx=ctx),
        grid=(batch, HEADS),
        in_specs=[
            pl.BlockSpec((seq, LANES), col(COL_K)),
            pl.BlockSpec((seq, LANES), col(COL_V)),
            pl.BlockSpec((seq, LANES), col(COL_Q)),
            pl.BlockSpec((seq, LANES), col(COL_Z)),
            pl.BlockSpec((seq, LANES), lambda b, h: (b, 0)),
            pl.BlockSpec((ctx, LANES), col(COL_K)),
            pl.BlockSpec((ctx, LANES), col(COL_V)),
            pl.BlockSpec((ctx, LANES), lambda b, h: (b, 0)),
            pl.BlockSpec((SUBLANES, LANES), lambda b, h: (0, h)),
            pl.BlockSpec((SUBLANES, LANES), lambda b, h: (0, HEADS + h)),
            pl.BlockSpec((SUBLANES, LANES), lambda b, h: (0, h)),
            pl.BlockSpec((SUBLANES, LANES), lambda b, h: (0, 0)),
            pl.BlockSpec((1, LANES), lambda b, h: (0, 0)),
        ],
        out_specs=pl.BlockSpec((seq, LANES), lambda b, h: (b, h)),
        out_shape=jax.ShapeDtypeStruct((batch * seq, HEADS * DV), BF16),
        scratch_shapes=[
            pltpu.VMEM((total, LANES), F32),
            pltpu.VMEM((total, LANES), F32),
            pltpu.VMEM((total, LANES), F32),
            pltpu.VMEM((4, total, LANES), F32),
            pltpu.VMEM((seq + 2 * SUBLANES, LANES), F32),
            pltpu.VMEM((2, nc * 2 * CHUNK, LANES), BF16),
            pltpu.VMEM((2, total, LANES), F32),
            pltpu.VMEM((2, total, CHUNK), BF16),
            pltpu.VMEM((2, nc * 2 * CHUNK, CHUNK), BF16),
            pltpu.VMEM((2, nc * SUBLANES, LANES), F32),
            pltpu.VMEM((2, total, LANES), F32),
        ],
        compiler_params=_params("arbitrary", "arbitrary"),
        name="gated_deltanet",
    )(proj, proj, proj, proj, ba, proj_ctx, proj_ctx, ba_ctx, taps, taps, taps_q, ab, gn.reshape(1, DV))


CONF_ROWS = 128
CONF_HALO = 16


def _conf_kernel(a_ref, g_ref, w_ref, b_ref, o_ref, pad_scr, *, seq):
    cb = a_ref.shape[1]
    u = a_ref[...].astype(F32) * jax.nn.sigmoid(g_ref[...].astype(F32))
    pad_scr[0:CONF_HALO, :] = jnp.zeros((CONF_HALO, cb), F32)
    pad_scr[CONF_HALO:CONF_HALO + seq, :] = u
    pad_scr[CONF_HALO + seq:2 * CONF_HALO + seq, :] = jnp.zeros((CONF_HALO, cb), F32)
    win = CONF_ROWS + 2 * CONF_HALO
    first = CONF_HALO - CONF_KERNEL // 2

    def body(i, carry):
        r0 = pl.multiple_of(i * CONF_ROWS, CONF_ROWS)
        w = pad_scr[pl.ds(r0, win), :]
        acc = jnp.zeros((CONF_ROWS, cb), F32) + b_ref[...]
        for sub in range(SUBLANES):
            shifted = w if sub == 0 else pltpu.roll(w, win - sub, axis=0)
            for j in range(CONF_KERNEL):
                off = first + j
                if off % SUBLANES == sub:
                    base = off - sub
                    acc = acc + shifted[base:base + CONF_ROWS, :] * w_ref[j:j + 1, :]
        o_ref[pl.ds(r0, CONF_ROWS), :] = acc.astype(o_ref.dtype)
        return carry

    lax.fori_loop(0, seq // CONF_ROWS, body, 0)


def conformer_conv(proj, conf_dw, conf_dw_b, *, batch, seq, cb=256):
    c = conf_dw.shape[1]
    nb = c // cb
    taps = jnp.zeros((32, c), F32).at[:CONF_KERNEL].set(conf_dw)
    return pl.pallas_call(
        functools.partial(_conf_kernel, seq=seq),
        grid=(batch, nb),
        in_specs=[pl.BlockSpec((seq, cb), lambda b, j: (b, COL_GLU_A * nb + j)),
                  pl.BlockSpec((seq, cb), lambda b, j: (b, COL_GLU_G * nb + j)),
                  pl.BlockSpec((32, cb), lambda b, j: (0, j)),
                  pl.BlockSpec((1, cb), lambda b, j: (0, j))],
        out_specs=pl.BlockSpec((seq, cb), lambda b, j: (b, j)),
        out_shape=jax.ShapeDtypeStruct((batch * seq, c), BF16),
        scratch_shapes=[pltpu.VMEM((seq + 2 * CONF_HALO, cb), F32)],
        compiler_params=_params("arbitrary", "arbitrary"),
        name="conformer_conv",
    )(proj, proj, taps, conf_dw_b.reshape(1, c))


def _mixer_out_kernel(og_ref, uc_ref, ga_ref, gb_ref, x_ref, pos_ref, gm_ref, shf_ref, scf_ref,
                      gpost_ref, gpre_ref, lng_ref, lnb_ref, wa_ref, wb_ref, wo_ref, wr_ref, br_ref,
                      x2_ref, h2_ref, ti_ref, tw_ref):
    uc = uc_ref[...].astype(F32)
    mu = jnp.mean(uc, axis=-1, keepdims=True)
    var = jnp.mean(jnp.square(uc - mu), axis=-1, keepdims=True)
    u = _silu((uc - mu) * lax.rsqrt(var + EPS) * lng_ref[...] + lnb_ref[...]).astype(BF16)
    ya = jnp.dot(og_ref[...], wa_ref[...], preferred_element_type=F32)
    yb = jnp.dot(u, wb_ref[...], preferred_element_type=F32)
    merged = (jax.nn.sigmoid(ga_ref[...].astype(F32)) * ya
              + jax.nn.sigmoid(gb_ref[...].astype(F32)) * yb).astype(BF16)
    y = jnp.dot(merged, wo_ref[...], preferred_element_type=F32)
    yn = y * lax.rsqrt(jnp.mean(y * y, axis=-1, keepdims=True) + EPS) * gpost_ref[...]
    x2 = x_ref[...] + pos_ref[...] + gm_ref[0] * yn
    x2_ref[...] = x2
    hn = x2 * lax.rsqrt(jnp.mean(x2 * x2, axis=-1, keepdims=True) + EPS) * gpre_ref[...]
    h2 = (hn * (1.0 + scf_ref[0]) + shf_ref[0]).astype(BF16)
    h2_ref[...] = h2
    logits = jnp.dot(h2, wr_ref[...], preferred_element_type=F32) + br_ref[...]
    rows = logits.shape[0]
    lane = lax.broadcasted_iota(jnp.int32, (rows, LANES), 1).astype(F32)
    live = jnp.where(lane < N_EXPERTS, logits, -jnp.inf)
    top_v, top_i = [], []
    for _ in range(TOP_K):
        m = jnp.max(live, axis=-1, keepdims=True)
        idx = jnp.min(jnp.where(live == m, lane, float(LANES)), axis=-1, keepdims=True)
        top_v.append(m)
        top_i.append(idx)
        live = jnp.where(lane == idx, -jnp.inf, live)
    ex = [jnp.exp(v - top_v[0]) for v in top_v]
    denom = ex[0] + ex[1] + ex[2] + ex[3]
    ti = jnp.zeros((rows, LANES), F32)
    tw = jnp.zeros((rows, LANES), F32)
    for k in range(TOP_K):
        ti = jnp.where(lane == k, top_i[k], ti)
        tw = jnp.where(lane == k, ex[k] / denom, tw)
    ti_ref[...] = ti.astype(jnp.int32)
    tw_ref[...] = tw


def mixer_output(og, uc, proj, x, pos, mod, mod_row_of_tile, g_post, g_pre_f, ln_g, ln_b,
                 w_a, w_b, w_o, w_r, b_r, *, tm):
    n, d = x.shape
    pos_tiles = pos.shape[0] // tm
    row = lambda i: (i, 0)
    fixed = lambda i: (0, 0)
    modspec = lambda k: pl.BlockSpec((1, 1, d), lambda i: (mod_row_of_tile(i), 0, k))
    vec = pl.BlockSpec((1, d), fixed)
    mat = pl.BlockSpec((d, d), fixed)
    return pl.pallas_call(
        _mixer_out_kernel,
        grid=(n // tm,),
        in_specs=[pl.BlockSpec((tm, d), row), pl.BlockSpec((tm, d), row),
                  pl.BlockSpec((tm, d), lambda i: (i, COL_GATE_A)),
                  pl.BlockSpec((tm, d), lambda i: (i, COL_GATE_B)),
                  pl.BlockSpec((tm, d), row),
                  pl.BlockSpec((tm, d), lambda i: (i % pos_tiles, 0)),
                  modspec(2), modspec(3), modspec(4),
                  vec, vec, vec, vec, mat, mat, mat,
                  pl.BlockSpec((d, LANES), fixed), pl.BlockSpec((1, LANES), fixed)],
        out_specs=[pl.BlockSpec((tm, d), row), pl.BlockSpec((tm, d), row),
                   pl.BlockSpec((tm, LANES), row), pl.BlockSpec((tm, LANES), row)],
        out_shape=[jax.ShapeDtypeStruct((n, d), F32), jax.ShapeDtypeStruct((n, d), BF16),
                   jax.ShapeDtypeStruct((n, LANES), jnp.int32), jax.ShapeDtypeStruct((n, LANES), F32)],
        compiler_params=_params("arbitrary"),
        name="mixer_output",
    )(og, uc, proj, proj, x, pos, mod, mod, mod,
      g_post.reshape(1, d), g_pre_f.reshape(1, d), ln_g.reshape(1, d), ln_b.reshape(1, d),
      w_a, w_b, w_o, w_r, b_r)


def _expert_kernel(be_ref, nb_ref, x_ref, wgu_ref, bgu_ref, wd_ref, bd_ref, rw_ref, o_ref):
    @pl.when(pl.program_id(0) < nb_ref[0])
    def _():
        gu = jnp.dot(x_ref[...], wgu_ref[0].astype(BF16), preferred_element_type=F32) + bgu_ref[0]
        dff = gu.shape[1] // 2
        gl = jnp.minimum(gu[:, :dff], SWIGLU_LIMIT)
        lin = jnp.clip(gu[:, dff:], -SWIGLU_LIMIT, SWIGLU_LIMIT)
        act = (gl * jax.nn.sigmoid(SWIGLU_ALPHA * gl) * (lin + 1.0)).astype(BF16)
        y = jnp.dot(act, wd_ref[0].astype(BF16), preferred_element_type=F32) + bd_ref[0]
        o_ref[...] = y * rw_ref[...]

    @pl.when(pl.program_id(0) >= nb_ref[0])
    def _():
        o_ref[...] = jnp.zeros_like(o_ref)


def expert_blocks(xb, block_expert, n_used, w_gate_up, b_gate_up, w_down, b_down, row_w):
    n_rows, d = xb.shape
    n_blocks = n_rows // MOE_BLOCK
    e, _, f2 = w_gate_up.shape
    grid_spec = pltpu.PrefetchScalarGridSpec(
        num_scalar_prefetch=2,
        grid=(n_blocks,),
        in_specs=[pl.BlockSpec((MOE_BLOCK, d), lambda i, be, nb: (i, 0)),
                  pl.BlockSpec((1, d, f2), lambda i, be, nb: (be[i], 0, 0)),
                  pl.BlockSpec((1, 1, f2), lambda i, be, nb: (be[i], 0, 0)),
                  pl.BlockSpec((1, f2 // 2, d), lambda i, be, nb: (be[i], 0, 0)),
                  pl.BlockSpec((1, 1, d), lambda i, be, nb: (be[i], 0, 0)),
                  pl.BlockSpec((MOE_BLOCK, 1), lambda i, be, nb: (i, 0))],
        out_specs=pl.BlockSpec((MOE_BLOCK, d), lambda i, be, nb: (i, 0)),
    )
    return pl.pallas_call(
        _expert_kernel,
        grid_spec=grid_spec,
        out_shape=jax.ShapeDtypeStruct((n_rows, d), F32),
        compiler_params=_params("arbitrary"),
        name="expert_blocks",
    )(block_expert, n_used, xb, w_gate_up, b_gate_up.reshape(e, 1, f2), w_down,
      b_down.reshape(e, 1, d), row_w.reshape(n_rows, 1))


def _final_kernel(x2_ref, y_ref, gf_ref, g_ref, o_ref):
    y = y_ref[...]
    yn = y * lax.rsqrt(jnp.mean(y * y, axis=-1, keepdims=True) + EPS) * g_ref[...]
    o_ref[...] = x2_ref[...] + gf_ref[0] * yn


def final_residual(x2, y, mod, mod_row_of_tile, g_post_f, *, tm):
    n, d = x2.shape
    row = lambda i: (i, 0)
    return pl.pallas_call(
        _final_kernel,
        grid=(n // tm,),
        in_specs=[pl.BlockSpec((tm, d), row), pl.BlockSpec((tm, d), row),
                  pl.BlockSpec((1, 1, d), lambda i: (mod_row_of_tile(i), 0, 5)),
                  pl.BlockSpec((1, d), lambda i: (0, 0))],
        out_specs=pl.BlockSpec((tm, d), row),
        out_shape=jax.ShapeDtypeStruct((n, d), F32),
        compiler_params=_params("arbitrary"),
        name="final_residual",
    )(x2, y, mod, g_post_f.reshape(1, d))


def _grid_pos_embedding(rows, d):
    row = jnp.repeat(jnp.arange(rows, dtype=F32), GRID_W)
    col = jnp.tile(jnp.arange(GRID_W, dtype=F32), rows)
    quarter = d // 4
    omega = POS_BASE ** (-jnp.arange(quarter, dtype=F32) / quarter)

    def emb(p):
        ang = p[:, None] * omega[None, :]
        return jnp.concatenate([jnp.sin(ang), jnp.cos(ang)], axis=-1)

    return jnp.concatenate([emb(row), emb(col)], axis=-1)


def _dispatch_plan(top_i, top_w, n_tok):
    n_assign = n_tok * TOP_K
    flat_e = top_i.reshape(-1)
    order = jnp.argsort(flat_e)
    sorted_e = flat_e[order]
    sorted_tok = (order // TOP_K).astype(jnp.int32)
    sorted_w = top_w.reshape(-1)[order]
    counts = jnp.zeros((N_EXPERTS,), jnp.int32).at[flat_e].add(1)
    padded = (counts + MOE_BLOCK - 1) // MOE_BLOCK * MOE_BLOCK
    start = jnp.cumsum(counts) - counts
    padded_end = jnp.cumsum(padded)
    padded_start = padded_end - padded
    dest = padded_start[sorted_e] + jnp.arange(n_assign, dtype=jnp.int32) - start[sorted_e]
    n_rows = -(-n_assign // MOE_BLOCK) * MOE_BLOCK + N_EXPERTS * MOE_BLOCK
    n_blocks = n_rows // MOE_BLOCK
    row_tok = jnp.full((n_rows,), n_tok, jnp.int32).at[dest].set(sorted_tok)
    row_w = jnp.zeros((n_rows,), F32).at[dest].set(sorted_w)
    block_expert = jnp.minimum(
        jnp.searchsorted(padded_end, jnp.arange(n_blocks, dtype=jnp.int32) * MOE_BLOCK, side='right'),
        N_EXPERTS - 1).astype(jnp.int32)
    n_used = (padded_end[-1] // MOE_BLOCK).astype(jnp.int32).reshape(1)
    return row_tok, row_w, block_expert, n_used


def kernel(x, c, ctx, c_ctx, w_mod, b_mod, g_pre_mix, g_post_mix, g_pre_ffn, g_post_ffn, w_in, conv_kv,
           conv_q, a_log, dt_bias, gdn_norm_g, w_proj_a, conf_dw, conf_dw_b, conf_ln_g, conf_ln_b,
           w_proj_b, w_out, w_router, b_router, w_gate_up, b_gate_up, w_down, b_down):
    batch, seq, d = x.shape
    ctx_len = ctx.shape[1]
    n_tok = batch * seq
    gw = HEADS * DV
    v_off, beta_off = HEADS * DK, HEADS * DK + gw
    state_cols = beta_off + 4 * HEADS
    q_off = state_cols
    z_off = q_off + HEADS * DK

    mod_rows = 2 * SUBLANES
    cond = jnp.zeros((mod_rows, d), F32).at[:batch].set(c).at[batch].set(c_ctx)
    mod = modulation(cond, w_mod[0], b_mod[0]).reshape(mod_rows, 1, 6 * d)

    w = w_in[0]
    w_main = jnp.concatenate([w[:, :beta_off], w[:, q_off:]], axis=1).astype(BF16)
    w_ba = jnp.zeros((d, LANES), F32).at[:, :4 * HEADS].set(w[:, beta_off:state_cols]).astype(BF16)

    pos = _grid_pos_embedding(seq // GRID_W, d)
    x_flat = x.reshape(n_tok, d)
    tm = 1024
    tiles_per_seq = seq // tm
    proj, ba = input_projection(x_flat, pos, mod, lambda i: i // tiles_per_seq, g_pre_mix[0],
                                w_main, w_ba, tm=tm, tn=1024)
    proj_ctx, ba_ctx = input_projection(ctx.reshape(batch * ctx_len, d), None, mod, lambda i: batch,
                                        g_pre_mix[0], w_main[:, :beta_off], w_ba, tm=ctx_len, tn=1024)

    og = gated_deltanet(proj, ba, proj_ctx, ba_ctx, conv_kv[0], conv_q[0], a_log[0], dt_bias[0],
                        gdn_norm_g[0], batch=batch, seq=seq, ctx=ctx_len)
    uc = conformer_conv(proj, conf_dw[0], conf_dw_b[0], batch=batch, seq=seq)

    tm2 = 256
    w_r = jnp.zeros((d, LANES), F32).at[:, :N_EXPERTS].set(w_router[0]).astype(BF16)
    b_r = jnp.zeros((1, LANES), F32).at[0, :N_EXPERTS].set(b_router[0])
    x2, h2, top_i, top_w = mixer_output(
        og, uc, proj, x_flat, pos, mod, lambda i: i // (seq // tm2), g_post_mix[0], g_pre_ffn[0],
        conf_ln_g[0], conf_ln_b[0], w_proj_a[0].astype(BF16), w_proj_b[0].astype(BF16),
        w_out[0].astype(BF16), w_r, b_r, tm=tm2)

    row_tok, row_w, block_expert, n_used = _dispatch_plan(top_i[:, :TOP_K], top_w[:, :TOP_K], n_tok)
    h2_pad = jnp.concatenate([h2, jnp.zeros((1, d), h2.dtype)], axis=0)
    xb = h2_pad[row_tok]
    yb = expert_blocks(xb, block_expert, n_used, w_gate_up[0], b_gate_up[0], w_down[0], b_down[0], row_w)
    y = jnp.zeros((n_tok + 1, d), F32).at[row_tok].add(yb)[:n_tok]

    out = final_residual(x2, y, mod, lambda i: i // (seq // tm2), g_post_ffn[0], tm=tm2)
    return out.reshape(batch, seq, d)
```

```python
import functools

import jax
import jax.numpy as jnp
from jax import lax
from jax.experimental import pallas as pl
from jax.experimental.pallas import tpu as pltpu

F32 = jnp.float32
BF16 = jnp.bfloat16

D_MODEL = 1024
GRID_W = 64
HEADS = 8
DK = 128
DV = 128
SHORT_CONV = 5
CHUNK = 64
CONF_KERNEL = 31
N_EXPERTS = 32
TOP_K = 4
SWIGLU_LIMIT = 7.0
SWIGLU_ALPHA = 1.702
MOE_BLOCK = 128
EPS = 1e-6
POS_BASE = 10000.0

LANES = 128
SUBLANES = 8
VMEM_LIMIT = 56 * 1024 * 1024

COL_K, COL_V, COL_Q, COL_Z, COL_GLU_A, COL_GLU_G, COL_GATE_A, COL_GATE_B = range(8)


def _params(*sem):
    return pltpu.CompilerParams(dimension_semantics=sem, vmem_limit_bytes=VMEM_LIMIT)


def _mod_kernel(c_ref, w_ref, b_ref, o_ref):
    c = c_ref[...]
    s = c * jax.nn.sigmoid(c)
    o_ref[...] = jnp.dot(s, w_ref[...], preferred_element_type=F32,
                         precision=lax.Precision.HIGHEST) + b_ref[...]


def modulation(cond, w_mod, b_mod):
    r, d = cond.shape
    n = w_mod.shape[1]
    tn = 1024
    return pl.pallas_call(
        _mod_kernel,
        grid=(n // tn,),
        in_specs=[pl.BlockSpec((r, d), lambda j: (0, 0)),
                  pl.BlockSpec((d, tn), lambda j: (0, j)),
                  pl.BlockSpec((1, tn), lambda j: (0, j))],
        out_specs=pl.BlockSpec((r, tn), lambda j: (0, j)),
        out_shape=jax.ShapeDtypeStruct((r, n), F32),
        compiler_params=_params("arbitrary"),
        name="modulation",
    )(cond, w_mod, b_mod.reshape(1, n))


def _inproj_kernel(*refs, has_pos):
    if has_pos:
        x_ref, pos_ref, sh_ref, sc_ref, g_ref, w_ref, wba_ref, o_ref, ba_ref, h_scr = refs
    else:
        x_ref, sh_ref, sc_ref, g_ref, w_ref, wba_ref, o_ref, ba_ref, h_scr = refs

    @pl.when(pl.program_id(1) == 0)
    def _():
        x = x_ref[...]
        if has_pos:
            x = x + pos_ref[...]
        y = x * lax.rsqrt(jnp.mean(x * x, axis=-1, keepdims=True) + EPS) * g_ref[...]
        h = (y * (1.0 + sc_ref[0]) + sh_ref[0]).astype(BF16)
        h_scr[...] = h
        ba_ref[...] = jnp.dot(h, wba_ref[...], preferred_element_type=F32)

    o_ref[...] = jnp.dot(h_scr[...], w_ref[...], preferred_element_type=F32).astype(o_ref.dtype)


def input_projection(x, pos, mod, mod_row_of_tile, g_pre, w_main, w_ba, *, tm, tn):
    n, d = x.shape
    w = w_main.shape[1]
    has_pos = pos is not None
    in_specs = [pl.BlockSpec((tm, d), lambda i, j: (i, 0))]
    args = [x]
    if has_pos:
        pos_tiles = pos.shape[0] // tm
        in_specs.append(pl.BlockSpec((tm, d), lambda i, j: (i % pos_tiles, 0)))
        args.append(pos)
    in_specs += [
        pl.BlockSpec((1, 1, d), lambda i, j: (mod_row_of_tile(i), 0, 0)),
        pl.BlockSpec((1, 1, d), lambda i, j: (mod_row_of_tile(i), 0, 1)),
        pl.BlockSpec((1, d), lambda i, j: (0, 0)),
        pl.BlockSpec((d, tn), lambda i, j: (0, j)),
        pl.BlockSpec((d, LANES), lambda i, j: (0, 0)),
    ]
    args += [mod, mod, g_pre.reshape(1, d), w_main, w_ba]
    return pl.pallas_call(
        functools.partial(_inproj_kernel, has_pos=has_pos),
        grid=(n // tm, w // tn),
        in_specs=in_specs,
        out_specs=[pl.BlockSpec((tm, tn), lambda i, j: (i, j)),
                   pl.BlockSpec((tm, LANES), lambda i, j: (i, 0))],
        out_shape=[jax.ShapeDtypeStruct((n, w), BF16), jax.ShapeDtypeStruct((n, LANES), F32)],
        scratch_shapes=[pltpu.VMEM((tm, d), BF16)],
        compiler_params=_params("arbitrary", "arbitrary"),
        name="input_projection_pos" if has_pos else "input_projection_ctx",
    )(*args)


def _silu(x):
    return x * jax.nn.sigmoid(x)


def _softplus(x):
    return jnp.maximum(x, 0.0) + jnp.log(1.0 + jnp.exp(-jnp.abs(x)))


def _short_conv(src, taps, pad_scr, rows):
    half = SHORT_CONV // 2
    pad_scr[0:SUBLANES, :] = jnp.zeros((SUBLANES, LANES), F32)
    pad_scr[SUBLANES:SUBLANES + rows, :] = src.astype(F32)
    pad_scr[SUBLANES + rows:2 * SUBLANES + rows, :] = jnp.zeros((SUBLANES, LANES), F32)
    acc = None
    for j in range(SHORT_CONV):
        off = SUBLANES + j - half
        term = pad_scr[off:off + rows, :] * taps[j:j + 1, :]
        acc = term if acc is None else acc + term
    return _silu(acc)


def _l2n(x):
    return x * lax.rsqrt(jnp.sum(x * x, axis=-1, keepdims=True) + EPS)


def _gdn_kernel(k_ref, v_ref, q_ref, z_ref, ba_ref, kc_ref, vc_ref, bac_ref,
                wk_ref, wv_ref, wq_ref, ab_ref, gn_ref, o_ref,
                kf, vf, qf, gates, pad_scr, lhs, val, att, ktt, dec, osc, *, seq, ctx, unroll, group):
    total = ctx + seq
    nc = total // CHUNK
    ncc = ctx // CHUNK
    two = 2 * CHUNK
    lg_chunk = CHUNK.bit_length() - 1

    ri = lax.broadcasted_iota(jnp.int32, (two, two), 0)
    ci = lax.broadcasted_iota(jnp.int32, (two, two), 1)
    same_dir = (ri >> lg_chunk) == (ci >> lg_chunk)
    incl = same_dir & (((ri < CHUNK) & (ri >= ci)) | ((ri >= CHUNK) & (ri <= ci)))
    strict = incl & (ri != ci)
    eye = jnp.where(ri == ci, 1.0, 0.0)
    tri = jnp.where(incl, 1.0, 0.0).astype(BF16)
    level_masks = []
    for lg in range(lg_chunk):
        same_parent = (ri >> (lg + 1)) == (ci >> (lg + 1))
        level_masks.append(jnp.where(same_parent & ((ri >> lg) != (ci >> lg)), 1.0, 0.0))
    top_rows = lax.broadcasted_iota(jnp.int32, (two, LANES), 0) < CHUNK

    def bwd_chunk(t):
        return jnp.where(t < ncc, ncc - 1 - t, nc + ncc - 1 - t)

    def stacked(ref, rf, rb):
        return jnp.concatenate([ref[pl.ds(rf, CHUNK), :], ref[pl.ds(rb, CHUNK), :]], axis=0)

    def chunk_load(t):
        rf = pl.multiple_of(t * CHUNK, CHUNK)
        rb = pl.multiple_of(bwd_chunk(t) * CHUNK, CHUNK)
        beta = jnp.concatenate([gates[0, pl.ds(rf, CHUNK), :], gates[1, pl.ds(rb, CHUNK), :]], axis=0)
        g = jnp.concatenate([gates[2, pl.ds(rf, CHUNK), :], gates[3, pl.ds(rb, CHUNK), :]], axis=0)
        return stacked(kf, rf, rb), stacked(vf, rf, rb), stacked(qf, rf, rb), beta, g

    def chunk_triangle(k, q, beta, g):
        kb = k.astype(BF16)
        kq = jnp.concatenate([kb, q.astype(BF16)], axis=0)
        kk_qk = lax.dot_general(kq, kb, (((1,), (1,)), ((), ())), preferred_element_type=F32)
        g_hi = g.astype(BF16)
        r1 = g - g_hi.astype(F32)
        g_mid = r1.astype(BF16)
        g_lo = (r1 - g_mid.astype(F32)).astype(BF16)
        gc3 = jnp.dot(tri, jnp.concatenate([g_hi, g_mid, g_lo], axis=1), preferred_element_type=F32)
        gc = gc3[:, :LANES] + gc3[:, LANES:2 * LANES] + gc3[:, 2 * LANES:]
        gc_row = jnp.transpose(gc)
        decay = jnp.where(incl, jnp.exp(gc - gc_row), 0.0)
        a = jnp.where(strict, beta * kk_qk[:two] * decay, 0.0)
        return a, (kk_qk[two:] * decay).astype(BF16), gc

    def chunk_finish(hh, t, k, v, q, beta, gc, attn, tmat):
        eg = jnp.exp(gc)
        rhs = jnp.concatenate([v * beta, k * beta * eg], axis=1).astype(BF16)
        sol = jnp.dot(tmat.astype(BF16), rhs, preferred_element_type=F32)
        g_end = jnp.where(top_rows, jnp.broadcast_to(gc[CHUNK - 1:CHUNK, :], (two, LANES)),
                          jnp.broadcast_to(gc[CHUNK:CHUNK + 1, :], (two, LANES)))
        k_tail = k * jnp.exp(g_end - gc)
        qg = (q * eg).astype(BF16)
        kcum = sol[:, DV:].astype(BF16)
        r0 = pl.multiple_of(t * two, two)
        lhs[hh, 0, pl.ds(r0, two), :] = jnp.concatenate([kcum[:CHUNK], qg[:CHUNK]], axis=0)
        lhs[hh, 1, pl.ds(r0, two), :] = jnp.concatenate([kcum[CHUNK:], qg[CHUNK:]], axis=0)
        val[hh, pl.ds(r0, two), :] = sol[:, :DV]
        att[hh, pl.ds(r0, two), :] = attn
        ktt[hh, pl.ds(r0, two), :] = jnp.transpose(k_tail).astype(BF16)
        d0 = pl.multiple_of(t * SUBLANES, SUBLANES)
        e_end = jnp.exp(g_end)
        dec[hh, 0, pl.ds(d0, SUBLANES), :] = e_end[:SUBLANES]
        dec[hh, 1, pl.ds(d0, SUBLANES), :] = e_end[CHUNK:CHUNK + SUBLANES]

    for hh in range(group):
        head = pl.program_id(1) * group + hh
        cols = slice(hh * LANES, (hh + 1) * LANES)

        kf[0:ctx, :] = _l2n(_short_conv(kc_ref[:, cols], wk_ref[:, cols], pad_scr, ctx))
        vf[0:ctx, :] = _short_conv(vc_ref[:, cols], wv_ref[:, cols], pad_scr, ctx)
        qf[0:ctx, :] = jnp.zeros((ctx, LANES), F32)
        kf[ctx:total, :] = _l2n(_short_conv(k_ref[:, cols], wk_ref[:, cols], pad_scr, seq))
        vf[ctx:total, :] = _short_conv(v_ref[:, cols], wv_ref[:, cols], pad_scr, seq)
        qf[ctx:total, :] = _l2n(_short_conv(q_ref[:, cols], wq_ref[:, cols], pad_scr, seq)) * (DK ** -0.5)

        def gate_cols(src_ref, lo, rows):
            x = src_ref[...]
            lane = lax.broadcasted_iota(jnp.int32, (rows, LANES), 1)
            beta = jax.nn.sigmoid(x)
            g = -jnp.exp(ab_ref[0:1, :]) * _softplus(x + ab_ref[1:2, :])
            for slot, (arr, base) in enumerate(((beta, 0), (beta, HEADS), (g, 2 * HEADS), (g, 3 * HEADS))):
                col = jnp.sum(jnp.where(lane == base + head, arr, 0.0), axis=-1, keepdims=True)
                gates[slot, lo:lo + rows, :] = jnp.broadcast_to(col, (rows, LANES))

        gate_cols(bac_ref, 0, ctx)
        gate_cols(ba_ref, ctx, seq)

        def chunk_body(i, carry, hh=hh):
            steps = [i * unroll + u for u in range(unroll)]
            loaded = [chunk_load(t) for t in steps]
            tris = [chunk_triangle(k, q, beta, g) for (k, v, q, beta, g) in loaded]
            tmats = [eye - a * level_masks[0] for (a, _, _) in tris]
            for m in level_masks[1:]:
                nxt = []
                for (a, _, _), tmat in zip(tris, tmats):
                    cs = (a * m).astype(BF16)
                    tb = tmat.astype(BF16)
                    tc = jnp.dot(tb, cs, preferred_element_type=F32)
                    nxt.append(tmat - jnp.dot(tc.astype(BF16), tb, preferred_element_type=F32))
                tmats = nxt
            for t, (k, v, q, beta, g), (a, attn, gc), tmat in zip(steps, loaded, tris, tmats):
                chunk_finish(hh, t, k, v, q, beta, gc, attn, tmat)
            return carry

        lax.fori_loop(0, nc // unroll, chunk_body, 0)

    def scan_body(t, carry):
        rf = pl.multiple_of(t * CHUNK, CHUNK)
        rb = pl.multiple_of(bwd_chunk(t) * CHUNK, CHUNK)
        r0 = pl.multiple_of(t * two, two)
        d0 = pl.multiple_of(t * SUBLANES, SUBLANES)
        rs = [(jnp.dot(lhs[hh, 0, pl.ds(r0, two), :], carry[2 * hh].astype(BF16), preferred_element_type=F32),
               jnp.dot(lhs[hh, 1, pl.ds(r0, two), :], carry[2 * hh + 1].astype(BF16), preferred_element_type=F32))
              for hh in range(group)]
        v_news = [val[hh, pl.ds(r0, two), :] - jnp.concatenate([r_f[:CHUNK], r_b[:CHUNK]], axis=0)
                  for hh, (r_f, r_b) in enumerate(rs)]
        outs, new_states = [], []
        for hh, ((r_f, r_b), v_new) in enumerate(zip(rs, v_news)):
            zeros = jnp.zeros_like(v_new)
            v_bd = jnp.concatenate([jnp.where(top_rows, v_new, zeros), jnp.where(top_rows, zeros, v_new)],
                                   axis=1).astype(BF16)
            upd = jnp.dot(ktt[hh, pl.ds(r0, two), :], v_bd, preferred_element_type=F32)
            new_states.append(carry[2 * hh] * dec[hh, 0, pl.ds(d0, SUBLANES), :][0:1, :] + upd[:, :DV])
            new_states.append(carry[2 * hh + 1] * dec[hh, 1, pl.ds(d0, SUBLANES), :][0:1, :] + upd[:, DV:])
            outs.append(jnp.concatenate([r_f[CHUNK:], r_b[CHUNK:]], axis=0)
                        + jnp.dot(att[hh, pl.ds(r0, two), :], v_new.astype(BF16), preferred_element_type=F32))
        for hh, o in enumerate(outs):
            osc[hh, 0, pl.ds(rf, CHUNK), :] = o[:CHUNK]
            osc[hh, 1, pl.ds(rb, CHUNK), :] = o[CHUNK:]
        return tuple(new_states)

    zero = jnp.zeros((DK, DV), F32)
    lax.fori_loop(0, nc, scan_body, (zero,) * (2 * group))

    for hh in range(group):
        cols = slice(hh * LANES, (hh + 1) * LANES)
        o = osc[hh, 0, ctx:total, :] + osc[hh, 1, ctx:total, :]
        o = o * lax.rsqrt(jnp.mean(o * o, axis=-1, keepdims=True) + EPS) * gn_ref[...]
        o_ref[:, cols] = (o * _silu(z_ref[:, cols].astype(F32))).astype(o_ref.dtype)


def gated_deltanet(proj, ba, proj_ctx, ba_ctx, conv_kv, conv_q, a_log, dt_bias, gn, *, batch, seq, ctx,
                   unroll=12, group=2):
    total = seq + ctx
    nc = total // CHUNK
    gl = group * LANES
    ng = HEADS // group
    taps = jnp.zeros((SUBLANES, conv_kv.shape[1]), F32).at[:SHORT_CONV].set(conv_kv)
    taps_q = jnp.zeros((SUBLANES, conv_q.shape[1]), F32).at[:SHORT_CONV].set(conv_q)
    ab = jnp.zeros((SUBLANES, LANES), F32)
    ab = ab.at[0, 2 * HEADS:4 * HEADS].set(a_log.reshape(-1)).at[1, 2 * HEADS:4 * HEADS].set(dt_bias.reshape(-1))
    col = lambda base: (lambda b, h: (b, base * ng + h))
    return pl.pallas_call(
        functools.partial(_gdn_kernel, seq=seq, ctx=ctx, unroll=unroll, group=group),
        grid=(batch, ng),
        in_specs=[
            pl.BlockSpec((seq, gl), col(COL_K)),
            pl.BlockSpec((seq, gl), col(COL_V)),
            pl.BlockSpec((seq, gl), col(COL_Q)),
            pl.BlockSpec((seq, gl), col(COL_Z)),
            pl.BlockSpec((seq, LANES), lambda b, h: (b, 0)),
            pl.BlockSpec((ctx, gl), col(COL_K)),
            pl.BlockSpec((ctx, gl), col(COL_V)),
            pl.BlockSpec((ctx, LANES), lambda b, h: (b, 0)),
            pl.BlockSpec((SUBLANES, gl), lambda b, h: (0, h)),
            pl.BlockSpec((SUBLANES, gl), lambda b, h: (0, ng + h)),
            pl.BlockSpec((SUBLANES, gl), lambda b, h: (0, h)),
            pl.BlockSpec((SUBLANES, LANES), lambda b, h: (0, 0)),
            pl.BlockSpec((1, LANES), lambda b, h: (0, 0)),
        ],
        out_specs=pl.BlockSpec((seq, gl), lambda b, h: (b, h)),
        out_shape=jax.ShapeDtypeStruct((batch * seq, HEADS * DV), BF16),
        scratch_shapes=[
            pltpu.VMEM((total, LANES), F32),
            pltpu.VMEM((total, LANES), F32),
            pltpu.VMEM((total, LANES), F32),
            pltpu.VMEM((4, total, LANES), F32),
            pltpu.VMEM((seq + 2 * SUBLANES, LANES), F32),
            pltpu.VMEM((group, 2, nc * 2 * CHUNK, LANES), BF16),
            pltpu.VMEM((group, nc * 2 * CHUNK, LANES), F32),
            pltpu.VMEM((group, nc * 2 * CHUNK, LANES), BF16),
            pltpu.VMEM((group, nc * 2 * CHUNK, LANES), BF16),
            pltpu.VMEM((group, 2, nc * SUBLANES, LANES), F32),
            pltpu.VMEM((group, 2, total, LANES), F32),
        ],
        compiler_params=_params("arbitrary", "arbitrary"),
        name="gated_deltanet",
    )(proj, proj, proj, proj, ba, proj_ctx, proj_ctx, ba_ctx, taps, taps, taps_q, ab, gn.reshape(1, DV))


CONF_ROWS = 128
CONF_HALO = 16


def _conf_kernel(a_ref, g_ref, w_ref, b_ref, o_ref, pad_scr, *, seq):
    cb = a_ref.shape[1]
    u = a_ref[...].astype(F32) * jax.nn.sigmoid(g_ref[...].astype(F32))
    pad_scr[0:CONF_HALO, :] = jnp.zeros((CONF_HALO, cb), F32)
    pad_scr[CONF_HALO:CONF_HALO + seq, :] = u
    pad_scr[CONF_HALO + seq:2 * CONF_HALO + seq, :] = jnp.zeros((CONF_HALO, cb), F32)
    win = CONF_ROWS + 2 * CONF_HALO
    first = CONF_HALO - CONF_KERNEL // 2

    def body(i, carry):
        r0 = pl.multiple_of(i * CONF_ROWS, CONF_ROWS)
        w = pad_scr[pl.ds(r0, win), :]
        acc = jnp.zeros((CONF_ROWS, cb), F32) + b_ref[...]
        for sub in range(SUBLANES):
            shifted = w if sub == 0 else pltpu.roll(w, win - sub, axis=0)
            for j in range(CONF_KERNEL):
                off = first + j
                if off % SUBLANES == sub:
                    base = off - sub
                    acc = acc + shifted[base:base + CONF_ROWS, :] * w_ref[j:j + 1, :]
        o_ref[pl.ds(r0, CONF_ROWS), :] = acc.astype(o_ref.dtype)
        return carry

    lax.fori_loop(0, seq // CONF_ROWS, body, 0)


def conformer_conv(proj, conf_dw, conf_dw_b, *, batch, seq, cb=256):
    c = conf_dw.shape[1]
    nb = c // cb
    taps = jnp.zeros((32, c), F32).at[:CONF_KERNEL].set(conf_dw)
    return pl.pallas_call(
        functools.partial(_conf_kernel, seq=seq),
        grid=(batch, nb),
        in_specs=[pl.BlockSpec((seq, cb), lambda b, j: (b, COL_GLU_A * nb + j)),
                  pl.BlockSpec((seq, cb), lambda b, j: (b, COL_GLU_G * nb + j)),
                  pl.BlockSpec((32, cb), lambda b, j: (0, j)),
                  pl.BlockSpec((1, cb), lambda b, j: (0, j))],
        out_specs=pl.BlockSpec((seq, cb), lambda b, j: (b, j)),
        out_shape=jax.ShapeDtypeStruct((batch * seq, c), BF16),
        scratch_shapes=[pltpu.VMEM((seq + 2 * CONF_HALO, cb), F32)],
        compiler_params=_params("arbitrary", "arbitrary"),
        name="conformer_conv",
    )(proj, proj, taps, conf_dw_b.reshape(1, c))


def _mixer_out_kernel(og_ref, uc_ref, ga_ref, gb_ref, x_ref, pos_ref, gm_ref, shf_ref, scf_ref,
                      gpost_ref, gpre_ref, lng_ref, lnb_ref, wa_ref, wb_ref, wo_ref, wr_ref, br_ref,
                      x2_ref, h2_ref, ti_ref, tw_ref):
    uc = uc_ref[...].astype(F32)
    mu = jnp.mean(uc, axis=-1, keepdims=True)
    var = jnp.mean(jnp.square(uc - mu), axis=-1, keepdims=True)
    u = _silu((uc - mu) * lax.rsqrt(var + EPS) * lng_ref[...] + lnb_ref[...]).astype(BF16)
    ya = jnp.dot(og_ref[...], wa_ref[...], preferred_element_type=F32)
    yb = jnp.dot(u, wb_ref[...], preferred_element_type=F32)
    merged = (jax.nn.sigmoid(ga_ref[...].astype(F32)) * ya
              + jax.nn.sigmoid(gb_ref[...].astype(F32)) * yb).astype(BF16)
    y = jnp.dot(merged, wo_ref[...], preferred_element_type=F32)
    yn = y * lax.rsqrt(jnp.mean(y * y, axis=-1, keepdims=True) + EPS) * gpost_ref[...]
    x2 = x_ref[...] + pos_ref[...] + gm_ref[0] * yn
    x2_ref[...] = x2
    hn = x2 * lax.rsqrt(jnp.mean(x2 * x2, axis=-1, keepdims=True) + EPS) * gpre_ref[...]
    h2 = (hn * (1.0 + scf_ref[0]) + shf_ref[0]).astype(BF16)
    h2_ref[...] = h2
    logits = jnp.dot(h2, wr_ref[...], preferred_element_type=F32) + br_ref[...]
    rows = logits.shape[0]
    lane = lax.broadcasted_iota(jnp.int32, (rows, LANES), 1).astype(F32)
    live = jnp.where(lane < N_EXPERTS, logits, -jnp.inf)
    top_v, top_i = [], []
    for _ in range(TOP_K):
        m = jnp.max(live, axis=-1, keepdims=True)
        idx = jnp.min(jnp.where(live == m, lane, float(LANES)), axis=-1, keepdims=True)
        top_v.append(m)
        top_i.append(idx)
        live = jnp.where(lane == idx, -jnp.inf, live)
    ex = [jnp.exp(v - top_v[0]) for v in top_v]
    denom = ex[0] + ex[1] + ex[2] + ex[3]
    ti = jnp.zeros((rows, LANES), F32)
    tw = jnp.zeros((rows, LANES), F32)
    for k in range(TOP_K):
        ti = jnp.where(lane == k, top_i[k], ti)
        tw = jnp.where(lane == k, ex[k] / denom, tw)
    ti_ref[...] = ti.astype(jnp.int32)
    tw_ref[...] = tw


def mixer_output(og, uc, proj, x, pos, mod, mod_row_of_tile, g_post, g_pre_f, ln_g, ln_b,
                 w_a, w_b, w_o, w_r, b_r, *, tm):
    n, d = x.shape
    pos_tiles = pos.shape[0] // tm
    row = lambda i: (i, 0)
    fixed = lambda i: (0, 0)
    modspec = lambda k: pl.BlockSpec((1, 1, d), lambda i: (mod_row_of_tile(i), 0, k))
    vec = pl.BlockSpec((1, d), fixed)
    mat = pl.BlockSpec((d, d), fixed)
    return pl.pallas_call(
        _mixer_out_kernel,
        grid=(n // tm,),
        in_specs=[pl.BlockSpec((tm, d), row), pl.BlockSpec((tm, d), row),
                  pl.BlockSpec((tm, d), lambda i: (i, COL_GATE_A)),
                  pl.BlockSpec((tm, d), lambda i: (i, COL_GATE_B)),
                  pl.BlockSpec((tm, d), row),
                  pl.BlockSpec((tm, d), lambda i: (i % pos_tiles, 0)),
                  modspec(2), modspec(3), modspec(4),
                  vec, vec, vec, vec, mat, mat, mat,
                  pl.BlockSpec((d, LANES), fixed), pl.BlockSpec((1, LANES), fixed)],
        out_specs=[pl.BlockSpec((tm, d), row), pl.BlockSpec((tm, d), row),
                   pl.BlockSpec((tm, LANES), row), pl.BlockSpec((tm, LANES), row)],
        out_shape=[jax.ShapeDtypeStruct((n, d), F32), jax.ShapeDtypeStruct((n, d), BF16),
                   jax.ShapeDtypeStruct((n, LANES), jnp.int32), jax.ShapeDtypeStruct((n, LANES), F32)],
        compiler_params=_params("arbitrary"),
        name="mixer_output",
    )(og, uc, proj, proj, x, pos, mod, mod, mod,
      g_post.reshape(1, d), g_pre_f.reshape(1, d), ln_g.reshape(1, d), ln_b.reshape(1, d),
      w_a, w_b, w_o, w_r, b_r)


def _expert_kernel(be_ref, nb_ref, x_ref, wgu_ref, bgu_ref, wd_ref, bd_ref, rw_ref, o_ref, wgu_bf, wd_bf):
    i = pl.program_id(0)

    @pl.when((i == 0) | (be_ref[i] != be_ref[jnp.maximum(i - 1, 0)]))
    def _():
        wgu_bf[...] = wgu_ref[0].astype(BF16)
        wd_bf[...] = wd_ref[0].astype(BF16)

    @pl.when(i < nb_ref[0])
    def _():
        gu = jnp.dot(x_ref[...], wgu_bf[...], preferred_element_type=F32) + bgu_ref[0]
        dff = gu.shape[1] // 2
        gl = jnp.minimum(gu[:, :dff], SWIGLU_LIMIT)
        lin = jnp.clip(gu[:, dff:], -SWIGLU_LIMIT, SWIGLU_LIMIT)
        act = (gl * jax.nn.sigmoid(SWIGLU_ALPHA * gl) * (lin + 1.0)).astype(BF16)
        y = jnp.dot(act, wd_bf[...], preferred_element_type=F32) + bd_ref[0]
        o_ref[...] = y * rw_ref[...]

    @pl.when(i >= nb_ref[0])
    def _():
        o_ref[...] = jnp.zeros_like(o_ref)


def expert_blocks(xb, block_expert, n_used, w_gate_up, b_gate_up, w_down, b_down, row_w):
    n_rows, d = xb.shape
    n_blocks = n_rows // MOE_BLOCK
    e, _, f2 = w_gate_up.shape
    grid_spec = pltpu.PrefetchScalarGridSpec(
        num_scalar_prefetch=2,
        grid=(n_blocks,),
        in_specs=[pl.BlockSpec((MOE_BLOCK, d), lambda i, be, nb: (i, 0)),
                  pl.BlockSpec((1, d, f2), lambda i, be, nb: (be[i], 0, 0)),
                  pl.BlockSpec((1, 1, f2), lambda i, be, nb: (be[i], 0, 0)),
                  pl.BlockSpec((1, f2 // 2, d), lambda i, be, nb: (be[i], 0, 0)),
                  pl.BlockSpec((1, 1, d), lambda i, be, nb: (be[i], 0, 0)),
                  pl.BlockSpec((MOE_BLOCK, 1), lambda i, be, nb: (i, 0))],
        out_specs=pl.BlockSpec((MOE_BLOCK, d), lambda i, be, nb: (i, 0)),
        scratch_shapes=[pltpu.VMEM((d, f2), BF16), pltpu.VMEM((f2 // 2, d), BF16)],
    )
    return pl.pallas_call(
        _expert_kernel,
        grid_spec=grid_spec,
        out_shape=jax.ShapeDtypeStruct((n_rows, d), F32),
        compiler_params=_params("arbitrary"),
        name="expert_blocks",
    )(block_expert, n_used, xb, w_gate_up, b_gate_up.reshape(e, 1, f2), w_down,
      b_down.reshape(e, 1, d), row_w.reshape(n_rows, 1))


def _final_kernel(x2_ref, y_ref, gf_ref, g_ref, o_ref):
    y = y_ref[...]
    yn = y * lax.rsqrt(jnp.mean(y * y, axis=-1, keepdims=True) + EPS) * g_ref[...]
    o_ref[...] = x2_ref[...] + gf_ref[0] * yn


def final_residual(x2, y, mod, mod_row_of_tile, g_post_f, *, tm):
    n, d = x2.shape
    row = lambda i: (i, 0)
    return pl.pallas_call(
        _final_kernel,
        grid=(n // tm,),
        in_specs=[pl.BlockSpec((tm, d), row), pl.BlockSpec((tm, d), row),
                  pl.BlockSpec((1, 1, d), lambda i: (mod_row_of_tile(i), 0, 5)),
                  pl.BlockSpec((1, d), lambda i: (0, 0))],
        out_specs=pl.BlockSpec((tm, d), row),
        out_shape=jax.ShapeDtypeStruct((n, d), F32),
        compiler_params=_params("arbitrary"),
        name="final_residual",
    )(x2, y, mod, g_post_f.reshape(1, d))


def _grid_pos_embedding(rows, d):
    row = jnp.repeat(jnp.arange(rows, dtype=F32), GRID_W)
    col = jnp.tile(jnp.arange(GRID_W, dtype=F32), rows)
    quarter = d // 4
    omega = POS_BASE ** (-jnp.arange(quarter, dtype=F32) / quarter)

    def emb(p):
        ang = p[:, None] * omega[None, :]
        return jnp.concatenate([jnp.sin(ang), jnp.cos(ang)], axis=-1)

    return jnp.concatenate([emb(row), emb(col)], axis=-1)


def _dispatch_plan(top_i, top_w, n_tok):
    n_assign = n_tok * TOP_K
    flat_e = top_i.reshape(-1)
    order = jnp.argsort(flat_e)
    sorted_e = flat_e[order]
    sorted_tok = (order // TOP_K).astype(jnp.int32)
    sorted_w = top_w.reshape(-1)[order]
    counts = jnp.zeros((N_EXPERTS,), jnp.int32).at[flat_e].add(1)
    padded = (counts + MOE_BLOCK - 1) // MOE_BLOCK * MOE_BLOCK
    start = jnp.cumsum(counts) - counts
    padded_end = jnp.cumsum(padded)
    padded_start = padded_end - padded
    dest = padded_start[sorted_e] + jnp.arange(n_assign, dtype=jnp.int32) - start[sorted_e]
    n_rows = -(-n_assign // MOE_BLOCK) * MOE_BLOCK + N_EXPERTS * MOE_BLOCK
    n_blocks = n_rows // MOE_BLOCK
    row_tok = jnp.full((n_rows,), n_tok, jnp.int32).at[dest].set(sorted_tok)
    row_w = jnp.zeros((n_rows,), F32).at[dest].set(sorted_w)
    block_expert = jnp.minimum(
        jnp.searchsorted(padded_end, jnp.arange(n_blocks, dtype=jnp.int32) * MOE_BLOCK, side='right'),
        N_EXPERTS - 1).astype(jnp.int32)
    n_used = (padded_end[-1] // MOE_BLOCK).astype(jnp.int32).reshape(1)
    return row_tok, row_w, block_expert, n_used


def kernel(x, c, ctx, c_ctx, w_mod, b_mod, g_pre_mix, g_post_mix, g_pre_ffn, g_post_ffn, w_in, conv_kv,
           conv_q, a_log, dt_bias, gdn_norm_g, w_proj_a, conf_dw, conf_dw_b, conf_ln_g, conf_ln_b,
           w_proj_b, w_out, w_router, b_router, w_gate_up, b_gate_up, w_down, b_down):
    batch, seq, d = x.shape
    ctx_len = ctx.shape[1]
    n_tok = batch * seq
    gw = HEADS * DV
    beta_off = HEADS * DK + gw
    state_cols = beta_off + 4 * HEADS
    q_off = state_cols

    mod_rows = 2 * SUBLANES
    cond = jnp.zeros((mod_rows, d), F32).at[:batch].set(c).at[batch].set(c_ctx)
    mod = modulation(cond, w_mod[0], b_mod[0]).reshape(mod_rows, 1, 6 * d)

    w = w_in[0]
    w_main = jnp.concatenate([w[:, :beta_off], w[:, q_off:]], axis=1).astype(BF16)
    w_ba = jnp.zeros((d, LANES), F32).at[:, :4 * HEADS].set(w[:, beta_off:state_cols]).astype(BF16)

    pos = _grid_pos_embedding(seq // GRID_W, d)
    x_flat = x.reshape(n_tok, d)
    tm = 1024
    tiles_per_seq = seq // tm
    proj, ba = input_projection(x_flat, pos, mod, lambda i: i // tiles_per_seq, g_pre_mix[0],
                                w_main, w_ba, tm=tm, tn=1024)
    proj_ctx, ba_ctx = input_projection(ctx.reshape(batch * ctx_len, d), None, mod, lambda i: batch,
                                        g_pre_mix[0], w_main[:, :beta_off], w_ba, tm=ctx_len, tn=1024)

    og = gated_deltanet(proj, ba, proj_ctx, ba_ctx, conv_kv[0], conv_q[0], a_log[0], dt_bias[0],
                        gdn_norm_g[0], batch=batch, seq=seq, ctx=ctx_len)
    uc = conformer_conv(proj, conf_dw[0], conf_dw_b[0], batch=batch, seq=seq)

    tm2 = 256
    w_r = jnp.zeros((d, LANES), F32).at[:, :N_EXPERTS].set(w_router[0]).astype(BF16)
    b_r = jnp.zeros((1, LANES), F32).at[0, :N_EXPERTS].set(b_router[0])
    x2, h2, top_i, top_w = mixer_output(
        og, uc, proj, x_flat, pos, mod, lambda i: i // (seq // tm2), g_post_mix[0], g_pre_ffn[0],
        conf_ln_g[0], conf_ln_b[0], w_proj_a[0].astype(BF16), w_proj_b[0].astype(BF16),
        w_out[0].astype(BF16), w_r, b_r, tm=tm2)

    row_tok, row_w, block_expert, n_used = _dispatch_plan(top_i[:, :TOP_K], top_w[:, :TOP_K], n_tok)
    h2_pad = jnp.concatenate([h2, jnp.zeros((1, d), h2.dtype)], axis=0)
    xb = h2_pad[row_tok]
    yb = expert_blocks(xb, block_expert, n_used, w_gate_up[0], b_gate_up[0], w_down[0], b_down[0], row_w)
    y = jnp.zeros((n_tok + 1, d), F32).at[row_tok].add(yb)[:n_tok]

    out = final_residual(x2, y, mod, lambda i: i // (seq // tm2), g_post_ffn[0], tm=tm2)
    return out.reshape(batch, seq, d)
```

```python
import functools

import jax
import jax.numpy as jnp
from jax import lax
from jax.experimental import pallas as pl
from jax.experimental.pallas import tpu as pltpu

F32 = jnp.float32
BF16 = jnp.bfloat16

D_MODEL = 1024
GRID_W = 64
HEADS = 8
DK = 128
DV = 128
SHORT_CONV = 5
CHUNK = 64
CONF_KERNEL = 31
N_EXPERTS = 32
TOP_K = 4
SWIGLU_LIMIT = 7.0
SWIGLU_ALPHA = 1.702
MOE_BLOCK = 128
EPS = 1e-6
POS_BASE = 10000.0

LANES = 128
SUBLANES = 8
VMEM_LIMIT = 56 * 1024 * 1024

COL_K, COL_V, COL_Q, COL_Z, COL_GLU_A, COL_GLU_G, COL_GATE_A, COL_GATE_B = range(8)


def _params(*sem):
    return pltpu.CompilerParams(dimension_semantics=sem, vmem_limit_bytes=VMEM_LIMIT)


def _mod_kernel(c_ref, w_ref, b_ref, o_ref):
    c = c_ref[...]
    s = c * jax.nn.sigmoid(c)
    o_ref[...] = jnp.dot(s, w_ref[...], preferred_element_type=F32,
                         precision=lax.Precision.HIGHEST) + b_ref[...]


def modulation(cond, w_mod, b_mod):
    r, d = cond.shape
    n = w_mod.shape[1]
    tn = 1024
    return pl.pallas_call(
        _mod_kernel,
        grid=(n // tn,),
        in_specs=[pl.BlockSpec((r, d), lambda j: (0, 0)),
                  pl.BlockSpec((d, tn), lambda j: (0, j)),
                  pl.BlockSpec((1, tn), lambda j: (0, j))],
        out_specs=pl.BlockSpec((r, tn), lambda j: (0, j)),
        out_shape=jax.ShapeDtypeStruct((r, n), F32),
        compiler_params=_params("arbitrary"),
        name="modulation",
    )(cond, w_mod, b_mod.reshape(1, n))


def _inproj_kernel(*refs, has_pos):
    if has_pos:
        x_ref, pos_ref, sh_ref, sc_ref, g_ref, w_ref, wba_ref, o_ref, ba_ref, h_scr = refs
    else:
        x_ref, sh_ref, sc_ref, g_ref, w_ref, wba_ref, o_ref, ba_ref, h_scr = refs

    @pl.when(pl.program_id(1) == 0)
    def _():
        x = x_ref[...]
        if has_pos:
            x = x + pos_ref[...]
        y = x * lax.rsqrt(jnp.mean(x * x, axis=-1, keepdims=True) + EPS) * g_ref[...]
        h = (y * (1.0 + sc_ref[0]) + sh_ref[0]).astype(BF16)
        h_scr[...] = h
        ba_ref[...] = jnp.dot(h, wba_ref[...], preferred_element_type=F32)

    o_ref[...] = jnp.dot(h_scr[...], w_ref[...], preferred_element_type=F32).astype(o_ref.dtype)


def input_projection(x, pos, mod, mod_row_of_tile, g_pre, w_main, w_ba, *, tm, tn):
    n, d = x.shape
    w = w_main.shape[1]
    has_pos = pos is not None
    in_specs = [pl.BlockSpec((tm, d), lambda i, j: (i, 0))]
    args = [x]
    if has_pos:
        pos_tiles = pos.shape[0] // tm
        in_specs.append(pl.BlockSpec((tm, d), lambda i, j: (i % pos_tiles, 0)))
        args.append(pos)
    in_specs += [
        pl.BlockSpec((1, 1, d), lambda i, j: (mod_row_of_tile(i), 0, 0)),
        pl.BlockSpec((1, 1, d), lambda i, j: (mod_row_of_tile(i), 0, 1)),
        pl.BlockSpec((1, d), lambda i, j: (0, 0)),
        pl.BlockSpec((d, tn), lambda i, j: (0, j)),
        pl.BlockSpec((d, LANES), lambda i, j: (0, 0)),
    ]
    args += [mod, mod, g_pre.reshape(1, d), w_main, w_ba]
    return pl.pallas_call(
        functools.partial(_inproj_kernel, has_pos=has_pos),
        grid=(n // tm, w // tn),
        in_specs=in_specs,
        out_specs=[pl.BlockSpec((tm, tn), lambda i, j: (i, j)),
                   pl.BlockSpec((tm, LANES), lambda i, j: (i, 0))],
        out_shape=[jax.ShapeDtypeStruct((n, w), BF16), jax.ShapeDtypeStruct((n, LANES), F32)],
        scratch_shapes=[pltpu.VMEM((tm, d), BF16)],
        compiler_params=_params("arbitrary", "arbitrary"),
        name="input_projection_pos" if has_pos else "input_projection_ctx",
    )(*args)


def _silu(x):
    return x * jax.nn.sigmoid(x)


def _softplus(x):
    return jnp.maximum(x, 0.0) + jnp.log(1.0 + jnp.exp(-jnp.abs(x)))


def _short_conv(src, taps, pad_scr, rows):
    half = SHORT_CONV // 2
    pad_scr[0:SUBLANES, :] = jnp.zeros((SUBLANES, LANES), F32)
    pad_scr[SUBLANES:SUBLANES + rows, :] = src.astype(F32)
    pad_scr[SUBLANES + rows:2 * SUBLANES + rows, :] = jnp.zeros((SUBLANES, LANES), F32)
    acc = None
    for j in range(SHORT_CONV):
        off = SUBLANES + j - half
        term = pad_scr[off:off + rows, :] * taps[j:j + 1, :]
        acc = term if acc is None else acc + term
    return _silu(acc)


def _l2n(x):
    return x * lax.rsqrt(jnp.sum(x * x, axis=-1, keepdims=True) + EPS)


def _gdn_kernel(k_ref, v_ref, q_ref, z_ref, ba_ref, kc_ref, vc_ref, bac_ref,
                wk_ref, wv_ref, wq_ref, ab_ref, gn_ref, o_ref,
                kf, vf, qf, gates, pad_scr, lhs, val, att, ktt, dec, osc, *, seq, ctx, unroll, group):
    total = ctx + seq
    nc = total // CHUNK
    ncc = ctx // CHUNK
    two = 2 * CHUNK
    lg_chunk = CHUNK.bit_length() - 1

    ri = lax.broadcasted_iota(jnp.int32, (two, two), 0)
    ci = lax.broadcasted_iota(jnp.int32, (two, two), 1)
    same_dir = (ri >> lg_chunk) == (ci >> lg_chunk)
    incl = same_dir & (((ri < CHUNK) & (ri >= ci)) | ((ri >= CHUNK) & (ri <= ci)))
    strict = incl & (ri != ci)
    eye = jnp.where(ri == ci, 1.0, 0.0)
    tri = jnp.where(incl, 1.0, 0.0).astype(BF16)
    level_masks = []
    for lg in range(lg_chunk):
        same_parent = (ri >> (lg + 1)) == (ci >> (lg + 1))
        level_masks.append(jnp.where(same_parent & ((ri >> lg) != (ci >> lg)), 1.0, 0.0))
    top_rows = lax.broadcasted_iota(jnp.int32, (two, LANES), 0) < CHUNK

    def bwd_chunk(t):
        return jnp.where(t < ncc, ncc - 1 - t, nc + ncc - 1 - t)

    def stacked(ref, rf, rb):
        return jnp.concatenate([ref[pl.ds(rf, CHUNK), :], ref[pl.ds(rb, CHUNK), :]], axis=0)

    def chunk_load(t):
        rf = pl.multiple_of(t * CHUNK, CHUNK)
        rb = pl.multiple_of(bwd_chunk(t) * CHUNK, CHUNK)
        beta = jnp.concatenate([gates[0, pl.ds(rf, CHUNK), :], gates[1, pl.ds(rb, CHUNK), :]], axis=0)
        g = jnp.concatenate([gates[2, pl.ds(rf, CHUNK), :], gates[3, pl.ds(rb, CHUNK), :]], axis=0)
        return stacked(kf, rf, rb), stacked(vf, rf, rb), stacked(qf, rf, rb), beta, g

    def chunk_triangle(k, q, beta, g):
        kb = k.astype(BF16)
        kq = jnp.concatenate([kb, q.astype(BF16)], axis=0)
        kk_qk = lax.dot_general(kq, kb, (((1,), (1,)), ((), ())), preferred_element_type=F32)
        g_hi = g.astype(BF16)
        r1 = g - g_hi.astype(F32)
        g_mid = r1.astype(BF16)
        g_lo = (r1 - g_mid.astype(F32)).astype(BF16)
        gc3 = jnp.dot(tri, jnp.concatenate([g_hi, g_mid, g_lo], axis=1), preferred_element_type=F32)
        gc = gc3[:, :LANES] + gc3[:, LANES:2 * LANES] + gc3[:, 2 * LANES:]
        gc_row = jnp.transpose(gc)
        decay = jnp.where(incl, jnp.exp(gc - gc_row), 0.0)
        a = jnp.where(strict, beta * kk_qk[:two] * decay, 0.0)
        return a, (kk_qk[two:] * decay).astype(BF16), gc

    def chunk_finish(hh, t, k, v, q, beta, gc, attn, tmat):
        eg = jnp.exp(gc)
        rhs = jnp.concatenate([v * beta, k * beta * eg], axis=1).astype(BF16)
        sol = jnp.dot(tmat.astype(BF16), rhs, preferred_element_type=F32)
        g_end = jnp.where(top_rows, jnp.broadcast_to(gc[CHUNK - 1:CHUNK, :], (two, LANES)),
                          jnp.broadcast_to(gc[CHUNK:CHUNK + 1, :], (two, LANES)))
        k_tail = k * jnp.exp(g_end - gc)
        qg = (q * eg).astype(BF16)
        kcum = sol[:, DV:].astype(BF16)
        r0 = pl.multiple_of(t * two, two)
        lhs[hh, 0, pl.ds(r0, two), :] = jnp.concatenate([kcum[:CHUNK], qg[:CHUNK]], axis=0)
        lhs[hh, 1, pl.ds(r0, two), :] = jnp.concatenate([kcum[CHUNK:], qg[CHUNK:]], axis=0)
        val[hh, pl.ds(r0, two), :] = sol[:, :DV]
        att[hh, pl.ds(r0, two), :] = attn
        ktt[hh, pl.ds(r0, two), :] = jnp.transpose(k_tail).astype(BF16)
        d0 = pl.multiple_of(t * SUBLANES, SUBLANES)
        e_end = jnp.exp(g_end)
        dec[hh, 0, pl.ds(d0, SUBLANES), :] = e_end[:SUBLANES]
        dec[hh, 1, pl.ds(d0, SUBLANES), :] = e_end[CHUNK:CHUNK + SUBLANES]

    for hh in range(group):
        head = pl.program_id(1) * group + hh
        cols = slice(hh * LANES, (hh + 1) * LANES)

        kf[0:ctx, :] = _l2n(_short_conv(kc_ref[:, cols], wk_ref[:, cols], pad_scr, ctx))
        vf[0:ctx, :] = _short_conv(vc_ref[:, cols], wv_ref[:, cols], pad_scr, ctx)
        qf[0:ctx, :] = jnp.zeros((ctx, LANES), F32)
        kf[ctx:total, :] = _l2n(_short_conv(k_ref[:, cols], wk_ref[:, cols], pad_scr, seq))
        vf[ctx:total, :] = _short_conv(v_ref[:, cols], wv_ref[:, cols], pad_scr, seq)
        qf[ctx:total, :] = _l2n(_short_conv(q_ref[:, cols], wq_ref[:, cols], pad_scr, seq)) * (DK ** -0.5)

        def gate_cols(src_ref, lo, rows):
            x = src_ref[...]
            lane = lax.broadcasted_iota(jnp.int32, (rows, LANES), 1)
            beta = jax.nn.sigmoid(x)
            g = -jnp.exp(ab_ref[0:1, :]) * _softplus(x + ab_ref[1:2, :])
            for slot, (arr, base) in enumerate(((beta, 0), (beta, HEADS), (g, 2 * HEADS), (g, 3 * HEADS))):
                col = jnp.sum(jnp.where(lane == base + head, arr, 0.0), axis=-1, keepdims=True)
                gates[slot, lo:lo + rows, :] = jnp.broadcast_to(col, (rows, LANES))

        gate_cols(bac_ref, 0, ctx)
        gate_cols(ba_ref, ctx, seq)

        def chunk_body(i, carry, hh=hh):
            steps = [i * unroll + u for u in range(unroll)]
            loaded = [chunk_load(t) for t in steps]
            tris = [chunk_triangle(k, q, beta, g) for (k, v, q, beta, g) in loaded]
            tmats = [eye - a * level_masks[0] for (a, _, _) in tris]
            for m in level_masks[1:]:
                nxt = []
                for (a, _, _), tmat in zip(tris, tmats):
                    cs = (a * m).astype(BF16)
                    tb = tmat.astype(BF16)
                    tc = jnp.dot(tb, cs, preferred_element_type=F32)
                    nxt.append(tmat - jnp.dot(tc.astype(BF16), tb, preferred_element_type=F32))
                tmats = nxt
            for t, (k, v, q, beta, g), (a, attn, gc), tmat in zip(steps, loaded, tris, tmats):
                chunk_finish(hh, t, k, v, q, beta, gc, attn, tmat)
            return carry

        lax.fori_loop(0, nc // unroll, chunk_body, 0)

    def scan_body(t, carry):
        rf = pl.multiple_of(t * CHUNK, CHUNK)
        rb = pl.multiple_of(bwd_chunk(t) * CHUNK, CHUNK)
        r0 = pl.multiple_of(t * two, two)
        d0 = pl.multiple_of(t * SUBLANES, SUBLANES)
        rs = [(jnp.dot(lhs[hh, 0, pl.ds(r0, two), :], carry[2 * hh].astype(BF16), preferred_element_type=F32),
               jnp.dot(lhs[hh, 1, pl.ds(r0, two), :], carry[2 * hh + 1].astype(BF16), preferred_element_type=F32))
              for hh in range(group)]
        v_news = [val[hh, pl.ds(r0, two), :] - jnp.concatenate([r_f[:CHUNK], r_b[:CHUNK]], axis=0)
                  for hh, (r_f, r_b) in enumerate(rs)]
        outs, new_states = [], []
        for hh, ((r_f, r_b), v_new) in enumerate(zip(rs, v_news)):
            zeros = jnp.zeros_like(v_new)
            v_bd = jnp.concatenate([jnp.where(top_rows, v_new, zeros), jnp.where(top_rows, zeros, v_new)],
                                   axis=1).astype(BF16)
            upd = jnp.dot(ktt[hh, pl.ds(r0, two), :], v_bd, preferred_element_type=F32)
            new_states.append(carry[2 * hh] * dec[hh, 0, pl.ds(d0, SUBLANES), :][0:1, :] + upd[:, :DV])
            new_states.append(carry[2 * hh + 1] * dec[hh, 1, pl.ds(d0, SUBLANES), :][0:1, :] + upd[:, DV:])
            outs.append(jnp.concatenate([r_f[CHUNK:], r_b[CHUNK:]], axis=0)
                        + jnp.dot(att[hh, pl.ds(r0, two), :], v_new.astype(BF16), preferred_element_type=F32))
        for hh, o in enumerate(outs):
            osc[hh, 0, pl.ds(rf, CHUNK), :] = o[:CHUNK]
            osc[hh, 1, pl.ds(rb, CHUNK), :] = o[CHUNK:]
        return tuple(new_states)

    zero = jnp.zeros((DK, DV), F32)
    lax.fori_loop(0, nc, scan_body, (zero,) * (2 * group))

    for hh in range(group):
        cols = slice(hh * LANES, (hh + 1) * LANES)
        o = osc[hh, 0, ctx:total, :] + osc[hh, 1, ctx:total, :]
        o = o * lax.rsqrt(jnp.mean(o * o, axis=-1, keepdims=True) + EPS) * gn_ref[...]
        o_ref[:, cols] = (o * _silu(z_ref[:, cols].astype(F32))).astype(o_ref.dtype)


def gated_deltanet(proj, ba, proj_ctx, ba_ctx, conv_kv, conv_q, a_log, dt_bias, gn, *, batch, seq, ctx,
                   unroll=12, group=2):
    total = seq + ctx
    nc = total // CHUNK
    gl = group * LANES
    ng = HEADS // group
    taps = jnp.zeros((SUBLANES, conv_kv.shape[1]), F32).at[:SHORT_CONV].set(conv_kv)
    taps_q = jnp.zeros((SUBLANES, conv_q.shape[1]), F32).at[:SHORT_CONV].set(conv_q)
    ab = jnp.zeros((SUBLANES, LANES), F32)
    ab = ab.at[0, 2 * HEADS:4 * HEADS].set(a_log.reshape(-1)).at[1, 2 * HEADS:4 * HEADS].set(dt_bias.reshape(-1))
    col = lambda base: (lambda b, h: (b, base * ng + h))
    return pl.pallas_call(
        functools.partial(_gdn_kernel, seq=seq, ctx=ctx, unroll=unroll, group=group),
        grid=(batch, ng),
        in_specs=[
            pl.BlockSpec((seq, gl), col(COL_K)),
            pl.BlockSpec((seq, gl), col(COL_V)),
            pl.BlockSpec((seq, gl), col(COL_Q)),
            pl.BlockSpec((seq, gl), col(COL_Z)),
            pl.BlockSpec((seq, LANES), lambda b, h: (b, 0)),
            pl.BlockSpec((ctx, gl), col(COL_K)),
            pl.BlockSpec((ctx, gl), col(COL_V)),
            pl.BlockSpec((ctx, LANES), lambda b, h: (b, 0)),
            pl.BlockSpec((SUBLANES, gl), lambda b, h: (0, h)),
            pl.BlockSpec((SUBLANES, gl), lambda b, h: (0, ng + h)),
            pl.BlockSpec((SUBLANES, gl), lambda b, h: (0, h)),
            pl.BlockSpec((SUBLANES, LANES), lambda b, h: (0, 0)),
            pl.BlockSpec((1, LANES), lambda b, h: (0, 0)),
        ],
        out_specs=pl.BlockSpec((seq, gl), lambda b, h: (b, h)),
        out_shape=jax.ShapeDtypeStruct((batch * seq, HEADS * DV), BF16),
        scratch_shapes=[
            pltpu.VMEM((total, LANES), F32),
            pltpu.VMEM((total, LANES), F32),
            pltpu.VMEM((total, LANES), F32),
            pltpu.VMEM((4, total, LANES), F32),
            pltpu.VMEM((seq + 2 * SUBLANES, LANES), F32),
            pltpu.VMEM((group, 2, nc * 2 * CHUNK, LANES), BF16),
            pltpu.VMEM((group, nc * 2 * CHUNK, LANES), F32),
            pltpu.VMEM((group, nc * 2 * CHUNK, LANES), BF16),
            pltpu.VMEM((group, nc * 2 * CHUNK, LANES), BF16),
            pltpu.VMEM((group, 2, nc * SUBLANES, LANES), F32),
            pltpu.VMEM((group, 2, total, LANES), F32),
        ],
        compiler_params=_params("arbitrary", "arbitrary"),
        name="gated_deltanet",
    )(proj, proj, proj, proj, ba, proj_ctx, proj_ctx, ba_ctx, taps, taps, taps_q, ab, gn.reshape(1, DV))


CONF_ROWS = 128
CONF_HALO = 16


def _conf_kernel(a_ref, g_ref, w_ref, b_ref, o_ref, pad_scr, *, seq):
    cb = a_ref.shape[1]
    u = a_ref[...].astype(F32) * jax.nn.sigmoid(g_ref[...].astype(F32))
    pad_scr[0:CONF_HALO, :] = jnp.zeros((CONF_HALO, cb), F32)
    pad_scr[CONF_HALO:CONF_HALO + seq, :] = u
    pad_scr[CONF_HALO + seq:2 * CONF_HALO + seq, :] = jnp.zeros((CONF_HALO, cb), F32)
    win = CONF_ROWS + 2 * CONF_HALO
    first = CONF_HALO - CONF_KERNEL // 2

    def body(i, carry):
        r0 = pl.multiple_of(i * CONF_ROWS, CONF_ROWS)
        w = pad_scr[pl.ds(r0, win), :]
        acc = jnp.zeros((CONF_ROWS, cb), F32) + b_ref[...]
        for sub in range(SUBLANES):
            shifted = w if sub == 0 else pltpu.roll(w, win - sub, axis=0)
            for j in range(CONF_KERNEL):
                off = first + j
                if off % SUBLANES == sub:
                    base = off - sub
                    acc = acc + shifted[base:base + CONF_ROWS, :] * w_ref[j:j + 1, :]
        o_ref[pl.ds(r0, CONF_ROWS), :] = acc.astype(o_ref.dtype)
        return carry

    lax.fori_loop(0, seq // CONF_ROWS, body, 0)


def conformer_conv(proj, conf_dw, conf_dw_b, *, batch, seq, cb=256):
    c = conf_dw.shape[1]
    nb = c // cb
    taps = jnp.zeros((32, c), F32).at[:CONF_KERNEL].set(conf_dw)
    return pl.pallas_call(
        functools.partial(_conf_kernel, seq=seq),
        grid=(batch, nb),
        in_specs=[pl.BlockSpec((seq, cb), lambda b, j: (b, COL_GLU_A * nb + j)),
                  pl.BlockSpec((seq, cb), lambda b, j: (b, COL_GLU_G * nb + j)),
                  pl.BlockSpec((32, cb), lambda b, j: (0, j)),
                  pl.BlockSpec((1, cb), lambda b, j: (0, j))],
        out_specs=pl.BlockSpec((seq, cb), lambda b, j: (b, j)),
        out_shape=jax.ShapeDtypeStruct((batch * seq, c), BF16),
        scratch_shapes=[pltpu.VMEM((seq + 2 * CONF_HALO, cb), F32)],
        compiler_params=_params("arbitrary", "arbitrary"),
        name="conformer_conv",
    )(proj, proj, taps, conf_dw_b.reshape(1, c))


def _mixer_out_kernel(og_ref, uc_ref, ga_ref, gb_ref, x_ref, pos_ref, gm_ref, shf_ref, scf_ref,
                      gpost_ref, gpre_ref, lng_ref, lnb_ref, wa_ref, wb_ref, wo_ref, wr_ref, br_ref,
                      x2_ref, h2_ref, ti_ref, tw_ref, cnt_ref, cnt_scr):
    uc = uc_ref[...].astype(F32)
    mu = jnp.mean(uc, axis=-1, keepdims=True)
    var = jnp.mean(jnp.square(uc - mu), axis=-1, keepdims=True)
    u = _silu((uc - mu) * lax.rsqrt(var + EPS) * lng_ref[...] + lnb_ref[...]).astype(BF16)
    ya = jnp.dot(og_ref[...], wa_ref[...], preferred_element_type=F32)
    yb = jnp.dot(u, wb_ref[...], preferred_element_type=F32)
    merged = (jax.nn.sigmoid(ga_ref[...].astype(F32)) * ya
              + jax.nn.sigmoid(gb_ref[...].astype(F32)) * yb).astype(BF16)
    y = jnp.dot(merged, wo_ref[...], preferred_element_type=F32)
    yn = y * lax.rsqrt(jnp.mean(y * y, axis=-1, keepdims=True) + EPS) * gpost_ref[...]
    x2 = x_ref[...] + pos_ref[...] + gm_ref[0] * yn
    x2_ref[...] = x2
    hn = x2 * lax.rsqrt(jnp.mean(x2 * x2, axis=-1, keepdims=True) + EPS) * gpre_ref[...]
    h2 = (hn * (1.0 + scf_ref[0]) + shf_ref[0]).astype(BF16)
    h2_ref[...] = h2
    logits = jnp.dot(h2, wr_ref[...], preferred_element_type=F32) + br_ref[...]
    rows = logits.shape[0]
    lane = lax.broadcasted_iota(jnp.int32, (rows, LANES), 1).astype(F32)
    live = jnp.where(lane < N_EXPERTS, logits, -jnp.inf)
    top_v, top_i = [], []
    for _ in range(TOP_K):
        m = jnp.max(live, axis=-1, keepdims=True)
        idx = jnp.min(jnp.where(live == m, lane, float(LANES)), axis=-1, keepdims=True)
        top_v.append(m)
        top_i.append(idx)
        live = jnp.where(lane == idx, -jnp.inf, live)
    ex = [jnp.exp(v - top_v[0]) for v in top_v]
    denom = ex[0] + ex[1] + ex[2] + ex[3]
    @pl.when(pl.program_id(0) == 0)
    def _():
        cnt_scr[...] = jnp.zeros_like(cnt_scr)

    picked = jnp.zeros((rows, LANES), F32)
    for k in range(TOP_K):
        picked = jnp.where(lane == top_i[k], 1.0, picked)
    ri = lax.broadcasted_iota(jnp.int32, (rows, rows), 0)
    ci = lax.broadcasted_iota(jnp.int32, (rows, rows), 1)
    before = jnp.where(ri > ci, 1.0, 0.0).astype(BF16)
    prefix = jnp.dot(before, picked.astype(BF16), preferred_element_type=F32) + cnt_scr[0:1, :]
    ti = jnp.zeros((rows, LANES), F32)
    tw = jnp.zeros((rows, LANES), F32)
    for k in range(TOP_K):
        rank = jnp.sum(jnp.where(lane == top_i[k], prefix, 0.0), axis=-1, keepdims=True)
        ti = jnp.where(lane == k, top_i[k], ti)
        ti = jnp.where(lane == TOP_K + k, rank, ti)
        tw = jnp.where(lane == k, ex[k] / denom, tw)
    ti_ref[...] = ti.astype(jnp.int32)
    tw_ref[...] = tw
    cnt_scr[...] = cnt_scr[...] + jnp.sum(picked, axis=0, keepdims=True)
    cnt_ref[0] = cnt_scr[...]


def mixer_output(og, uc, proj, x, pos, mod, mod_row_of_tile, g_post, g_pre_f, ln_g, ln_b,
                 w_a, w_b, w_o, w_r, b_r, *, tm):
    n, d = x.shape
    pos_tiles = pos.shape[0] // tm
    row = lambda i: (i, 0)
    fixed = lambda i: (0, 0)
    modspec = lambda k: pl.BlockSpec((1, 1, d), lambda i: (mod_row_of_tile(i), 0, k))
    vec = pl.BlockSpec((1, d), fixed)
    mat = pl.BlockSpec((d, d), fixed)
    return pl.pallas_call(
        _mixer_out_kernel,
        grid=(n // tm,),
        in_specs=[pl.BlockSpec((tm, d), row), pl.BlockSpec((tm, d), row),
                  pl.BlockSpec((tm, d), lambda i: (i, COL_GATE_A)),
                  pl.BlockSpec((tm, d), lambda i: (i, COL_GATE_B)),
                  pl.BlockSpec((tm, d), row),
                  pl.BlockSpec((tm, d), lambda i: (i % pos_tiles, 0)),
                  modspec(2), modspec(3), modspec(4),
                  vec, vec, vec, vec, mat, mat, mat,
                  pl.BlockSpec((d, LANES), fixed), pl.BlockSpec((1, LANES), fixed)],
        out_specs=[pl.BlockSpec((tm, d), row), pl.BlockSpec((tm, d), row),
                   pl.BlockSpec((tm, LANES), row), pl.BlockSpec((tm, LANES), row),
                   pl.BlockSpec((1, SUBLANES, LANES), lambda i: (i, 0, 0))],
        out_shape=[jax.ShapeDtypeStruct((n, d), F32), jax.ShapeDtypeStruct((n, d), BF16),
                   jax.ShapeDtypeStruct((n, LANES), jnp.int32), jax.ShapeDtypeStruct((n, LANES), F32),
                   jax.ShapeDtypeStruct((n // tm, SUBLANES, LANES), F32)],
        scratch_shapes=[pltpu.VMEM((SUBLANES, LANES), F32)],
        compiler_params=_params("arbitrary"),
        name="mixer_output",
    )(og, uc, proj, proj, x, pos, mod, mod, mod,
      g_post.reshape(1, d), g_pre_f.reshape(1, d), ln_g.reshape(1, d), ln_b.reshape(1, d),
      w_a, w_b, w_o, w_r, b_r)


MOE_CHUNK = 512
ITEM_FIRST, ITEM_LAST, ITEM_ACTIVE = 1, 2, 4


def _item_flags(flag):
    return (flag & ITEM_FIRST) != 0, (flag & ITEM_LAST) != 0, (flag & ITEM_ACTIVE) != 0


def _expert_kernel(iblk_ref, ichk_ref, iflag_ref, bexp_ref,
                   x_ref, dest_ref, wt_ref, wgu_ref, bgu_ref, wd_ref, bd_ref, o_ref,
                   acc, rw_acc, wgu_bf, wd_bf):
    w = pl.program_id(0)
    blk = iblk_ref[w]
    first, last, active = _item_flags(iflag_ref[w])
    expert = bexp_ref[blk]

    @pl.when((w == 0) | (expert != bexp_ref[iblk_ref[jnp.maximum(w - 1, 0)]]))
    def _():
        wgu_bf[...] = wgu_ref[0].astype(BF16)
        wd_bf[...] = wd_ref[0].astype(BF16)

    @pl.when(first)
    def _():
        acc[...] = jnp.zeros_like(acc)
        rw_acc[...] = jnp.zeros_like(rw_acc)

    @pl.when(active)
    def _():
        chunk = x_ref.shape[0]
        row = lax.broadcasted_iota(jnp.int32, (MOE_BLOCK, chunk), 0) + blk * MOE_BLOCK
        dest = dest_ref[...]
        wt = wt_ref[...]
        onehot = jnp.zeros((MOE_BLOCK, chunk), F32)
        wsel = jnp.zeros((MOE_BLOCK, chunk), F32)
        for k in range(TOP_K):
            hit = row == dest[k:k + 1, :]
            onehot = jnp.where(hit, 1.0, onehot)
            wsel = jnp.where(hit, wt[k:k + 1, :], wsel)
        acc[...] += jnp.dot(onehot.astype(BF16), x_ref[...], preferred_element_type=F32)
        rw_acc[...] += jnp.sum(wsel, axis=1, keepdims=True)

    @pl.when(last)
    def _():
        gu = jnp.dot(acc[...].astype(BF16), wgu_bf[...], preferred_element_type=F32) + bgu_ref[0]
        dff = gu.shape[1] // 2
        gl = jnp.minimum(gu[:, :dff], SWIGLU_LIMIT)
        lin = jnp.clip(gu[:, dff:], -SWIGLU_LIMIT, SWIGLU_LIMIT)
        act = (gl * jax.nn.sigmoid(SWIGLU_ALPHA * gl) * (lin + 1.0)).astype(BF16)
        y = jnp.dot(act, wd_bf[...], preferred_element_type=F32) + bd_ref[0]
        o_ref[...] = (y * rw_acc[...]).astype(o_ref.dtype)


def expert_blocks(h2, dest_t, wt_t, items, block_expert, w_gate_up, b_gate_up, w_down, b_down, *, n_blocks):
    n, d = h2.shape
    e, _, f2 = w_gate_up.shape
    iblk, ichk, iflag = items
    by_expert = lambda w, ib, ic, fl, be: (be[ib[w]], 0, 0)
    grid_spec = pltpu.PrefetchScalarGridSpec(
        num_scalar_prefetch=4,
        grid=(iblk.shape[0],),
        in_specs=[pl.BlockSpec((MOE_CHUNK, d), lambda w, ib, ic, fl, be: (ic[w], 0)),
                  pl.BlockSpec((SUBLANES, MOE_CHUNK), lambda w, ib, ic, fl, be: (0, ic[w])),
                  pl.BlockSpec((SUBLANES, MOE_CHUNK), lambda w, ib, ic, fl, be: (0, ic[w])),
                  pl.BlockSpec((1, d, f2), by_expert),
                  pl.BlockSpec((1, 1, f2), by_expert),
                  pl.BlockSpec((1, f2 // 2, d), by_expert),
                  pl.BlockSpec((1, 1, d), by_expert)],
        out_specs=pl.BlockSpec((MOE_BLOCK, d), lambda w, ib, ic, fl, be: (ib[w], 0)),
        scratch_shapes=[pltpu.VMEM((MOE_BLOCK, d), F32), pltpu.VMEM((MOE_BLOCK, 1), F32),
                        pltpu.VMEM((d, f2), BF16), pltpu.VMEM((f2 // 2, d), BF16)],
    )
    return pl.pallas_call(
        _expert_kernel,
        grid_spec=grid_spec,
        out_shape=jax.ShapeDtypeStruct((n_blocks * MOE_BLOCK, d), BF16),
        compiler_params=_params("arbitrary"),
        name="expert_blocks",
    )(iblk, ichk, iflag, block_expert, h2, dest_t, wt_t, w_gate_up, b_gate_up.reshape(e, 1, f2), w_down,
      b_down.reshape(e, 1, d))


def _combine_kernel(ichk_ref, iblk_ref, iflag_ref, yb_ref, dest_ref, x2_ref, gf_ref, g_ref, o_ref, acc):
    w = pl.program_id(0)
    first, last, active = _item_flags(iflag_ref[w])

    @pl.when(first)
    def _():
        acc[...] = jnp.zeros_like(acc)

    @pl.when(active)
    def _():
        chunk = x2_ref.shape[0]
        col = lax.broadcasted_iota(jnp.int32, (chunk, MOE_BLOCK), 1) + iblk_ref[w] * MOE_BLOCK
        dest = dest_ref[...]
        onehot = jnp.zeros((chunk, MOE_BLOCK), F32)
        for k in range(TOP_K):
            onehot = jnp.where(col == dest[:, k:k + 1], 1.0, onehot)
        acc[...] += jnp.dot(onehot.astype(BF16), yb_ref[...], preferred_element_type=F32)

    @pl.when(last)
    def _():
        y = acc[...]
        yn = y * lax.rsqrt(jnp.mean(y * y, axis=-1, keepdims=True) + EPS) * g_ref[...]
        o_ref[...] = x2_ref[...] + gf_ref[0] * yn


def combine_residual(yb, dest, x2, items, mod, mod_row_of_chunk, g_post_f):
    n, d = x2.shape
    ichk, iblk, iflag = items
    chunk_row = lambda w, ic, ib, fl: (ic[w], 0)
    grid_spec = pltpu.PrefetchScalarGridSpec(
        num_scalar_prefetch=3,
        grid=(ichk.shape[0],),
        in_specs=[pl.BlockSpec((MOE_BLOCK, d), lambda w, ic, ib, fl: (ib[w], 0)),
                  pl.BlockSpec((MOE_CHUNK, LANES), chunk_row),
                  pl.BlockSpec((MOE_CHUNK, d), chunk_row),
                  pl.BlockSpec((1, 1, d), lambda w, ic, ib, fl: (mod_row_of_chunk(ic[w]), 0, 5)),
                  pl.BlockSpec((1, d), lambda w, ic, ib, fl: (0, 0))],
        out_specs=pl.BlockSpec((MOE_CHUNK, d), chunk_row),
        scratch_shapes=[pltpu.VMEM((MOE_CHUNK, d), F32)],
    )
    return pl.pallas_call(
        _combine_kernel,
        grid_spec=grid_spec,
        out_shape=jax.ShapeDtypeStruct((n, d), F32),
        compiler_params=_params("arbitrary"),
        name="combine_residual",
    )(ichk, iblk, iflag, yb, dest, x2, mod, g_post_f.reshape(1, d))


def _grid_pos_embedding(rows, d):
    row = jnp.repeat(jnp.arange(rows, dtype=F32), GRID_W)
    col = jnp.tile(jnp.arange(GRID_W, dtype=F32), rows)
    quarter = d // 4
    omega = POS_BASE ** (-jnp.arange(quarter, dtype=F32) / quarter)

    def emb(p):
        ang = p[:, None] * omega[None, :]
        return jnp.concatenate([jnp.sin(ang), jnp.cos(ang)], axis=-1)

    return jnp.concatenate([emb(row), emb(col)], axis=-1)


def _count_le(sorted_vals, queries):
    return jnp.sum(sorted_vals[None, :] <= queries[:, None], axis=1).astype(jnp.int32)


def _work_items(n_it, b_lo, chunk_ids, key_is_block, n_items):
    off_end = jnp.cumsum(n_it)
    off = off_end - n_it
    total = off_end[-1]
    w = jnp.arange(n_items, dtype=jnp.int32)
    w_eff = jnp.minimum(w, total - 1)
    cell = jnp.minimum(_count_le(off_end, w_eff), n_it.shape[0] - 1)
    blk = (b_lo[cell] + w_eff - off[cell]).astype(jnp.int32)
    chk = chunk_ids[cell].astype(jnp.int32)
    key = blk if key_is_block else chk
    active = w < total
    first = active & ((w == 0) | (key != jnp.roll(key, 1)))
    last = active & ((w == total - 1) | (key != jnp.roll(key, -1)))
    flags = (first * ITEM_FIRST + last * ITEM_LAST + active * ITEM_ACTIVE).astype(jnp.int32)
    return blk, chk, flags


def _moe_plan(ti, tw, cnt_after, n_tok, tm):
    e4 = ti[:, :TOP_K]
    rank4 = ti[:, TOP_K:2 * TOP_K]
    counts = cnt_after[-1, 0, :N_EXPERTS].astype(jnp.int32)
    padded = (counts + MOE_BLOCK - 1) // MOE_BLOCK * MOE_BLOCK
    padded_end = jnp.cumsum(padded)
    padded_start = padded_end - padded
    dest4 = padded_start[e4] + rank4
    n_blocks = n_tok * TOP_K // MOE_BLOCK + N_EXPERTS
    block_expert = jnp.minimum(_count_le(padded_end, jnp.arange(n_blocks, dtype=jnp.int32) * MOE_BLOCK),
                               N_EXPERTS - 1)
    nch = n_tok // MOE_CHUNK
    per_chunk = MOE_CHUNK // tm
    cb_after = cnt_after[per_chunk - 1::per_chunk, 0, :N_EXPERTS].astype(jnp.int32)
    cb_before = jnp.concatenate([jnp.zeros((1, N_EXPERTS), jnp.int32), cb_after[:-1]], axis=0)
    b_lo = (padded_start[None, :] + cb_before) // MOE_BLOCK
    b_hi = (padded_start[None, :] + cb_after - 1) // MOE_BLOCK
    n_it = jnp.where(cb_after > cb_before, b_hi - b_lo + 1, 0)
    chunk_of = jnp.broadcast_to(jnp.arange(nch, dtype=jnp.int32)[:, None], (nch, N_EXPERTS))
    n_items = n_blocks + N_EXPERTS * (nch - 1)
    gather_items = _work_items(n_it.T.reshape(-1), b_lo.T.reshape(-1), chunk_of.T.reshape(-1), True, n_items)
    cblk, cchk, cflag = _work_items(n_it.reshape(-1), b_lo.reshape(-1), chunk_of.reshape(-1), False, n_items)
    pad_rows = SUBLANES - TOP_K
    dest_t = jnp.concatenate([dest4.T, jnp.full((pad_rows, n_tok), -1, jnp.int32)], axis=0)
    wt_t = jnp.concatenate([tw[:, :TOP_K].T, jnp.zeros((pad_rows, n_tok), F32)], axis=0)
    dest_l = jnp.concatenate([dest4, jnp.full((n_tok, LANES - TOP_K), -1, jnp.int32)], axis=1)
    return dest_t, wt_t, dest_l, block_expert, gather_items, (cchk, cblk, cflag), n_blocks


def kernel(x, c, ctx, c_ctx, w_mod, b_mod, g_pre_mix, g_post_mix, g_pre_ffn, g_post_ffn, w_in, conv_kv,
           conv_q, a_log, dt_bias, gdn_norm_g, w_proj_a, conf_dw, conf_dw_b, conf_ln_g, conf_ln_b,
           w_proj_b, w_out, w_router, b_router, w_gate_up, b_gate_up, w_down, b_down):
    batch, seq, d = x.shape
    ctx_len = ctx.shape[1]
    n_tok = batch * seq
    gw = HEADS * DV
    beta_off = HEADS * DK + gw
    state_cols = beta_off + 4 * HEADS
    q_off = state_cols

    mod_rows = 2 * SUBLANES
    cond = jnp.zeros((mod_rows, d), F32).at[:batch].set(c).at[batch].set(c_ctx)
    mod = modulation(cond, w_mod[0], b_mod[0]).reshape(mod_rows, 1, 6 * d)

    w = w_in[0]
    w_main = jnp.concatenate([w[:, :beta_off], w[:, q_off:]], axis=1).astype(BF16)
    w_ba = jnp.zeros((d, LANES), F32).at[:, :4 * HEADS].set(w[:, beta_off:state_cols]).astype(BF16)

    pos = _grid_pos_embedding(seq // GRID_W, d)
    x_flat = x.reshape(n_tok, d)
    tm = 1024
    tiles_per_seq = seq // tm
    proj, ba = input_projection(x_flat, pos, mod, lambda i: i // tiles_per_seq, g_pre_mix[0],
                                w_main, w_ba, tm=tm, tn=1024)
    proj_ctx, ba_ctx = input_projection(ctx.reshape(batch * ctx_len, d), None, mod, lambda i: batch,
                                        g_pre_mix[0], w_main[:, :beta_off], w_ba, tm=ctx_len, tn=1024)

    og = gated_deltanet(proj, ba, proj_ctx, ba_ctx, conv_kv[0], conv_q[0], a_log[0], dt_bias[0],
                        gdn_norm_g[0], batch=batch, seq=seq, ctx=ctx_len)
    uc = conformer_conv(proj, conf_dw[0], conf_dw_b[0], batch=batch, seq=seq)

    tm2 = 256
    w_r = jnp.zeros((d, LANES), F32).at[:, :N_EXPERTS].set(w_router[0]).astype(BF16)
    b_r = jnp.zeros((1, LANES), F32).at[0, :N_EXPERTS].set(b_router[0])
    x2, h2, top_i, top_w, cnt_after = mixer_output(
        og, uc, proj, x_flat, pos, mod, lambda i: i // (seq // tm2), g_post_mix[0], g_pre_ffn[0],
        conf_ln_g[0], conf_ln_b[0], w_proj_a[0].astype(BF16), w_proj_b[0].astype(BF16),
        w_out[0].astype(BF16), w_r, b_r, tm=tm2)

    dest_t, wt_t, dest_l, block_expert, gather_items, combine_items, n_blocks = _moe_plan(
        top_i, top_w, cnt_after, n_tok, tm2)
    yb = expert_blocks(h2, dest_t, wt_t, gather_items, block_expert, w_gate_up[0], b_gate_up[0],
                       w_down[0], b_down[0], n_blocks=n_blocks)
    out = combine_residual(yb, dest_l, x2, combine_items, mod, lambda c: c // (seq // MOE_CHUNK),
                           g_post_ffn[0])
    return out.reshape(batch, seq, d)
```

```python
import functools

import jax
import jax.numpy as jnp
from jax import lax
from jax.experimental import pallas as pl
from jax.experimental.pallas import tpu as pltpu

F32 = jnp.float32
BF16 = jnp.bfloat16

D_MODEL = 1024
GRID_W = 64
HEADS = 8
DK = 128
DV = 128
SHORT_CONV = 5
CHUNK = 64
CONF_KERNEL = 31
N_EXPERTS = 32
TOP_K = 4
SWIGLU_LIMIT = 7.0
SWIGLU_ALPHA = 1.702
MOE_BLOCK = 128
EPS = 1e-6
POS_BASE = 10000.0

LANES = 128
SUBLANES = 8
VMEM_LIMIT = 56 * 1024 * 1024

COL_K, COL_V, COL_Q, COL_Z, COL_GLU_A, COL_GLU_G, COL_GATE_A, COL_GATE_B = range(8)


def _params(*sem):
    return pltpu.CompilerParams(dimension_semantics=sem, vmem_limit_bytes=VMEM_LIMIT)


def _mod_kernel(c_ref, w_ref, b_ref, o_ref):
    c = c_ref[...]
    s = c * jax.nn.sigmoid(c)
    o_ref[...] = jnp.dot(s, w_ref[...], preferred_element_type=F32,
                         precision=lax.Precision.HIGHEST) + b_ref[...]


def modulation(cond, w_mod, b_mod):
    r, d = cond.shape
    n = w_mod.shape[1]
    tn = 1024
    return pl.pallas_call(
        _mod_kernel,
        grid=(n // tn,),
        in_specs=[pl.BlockSpec((r, d), lambda j: (0, 0)),
                  pl.BlockSpec((d, tn), lambda j: (0, j)),
                  pl.BlockSpec((1, tn), lambda j: (0, j))],
        out_specs=pl.BlockSpec((r, tn), lambda j: (0, j)),
        out_shape=jax.ShapeDtypeStruct((r, n), F32),
        compiler_params=_params("arbitrary"),
        name="modulation",
    )(cond, w_mod, b_mod.reshape(1, n))


def _inproj_kernel(*refs, has_pos):
    if has_pos:
        x_ref, pos_ref, sh_ref, sc_ref, g_ref, w_ref, wba_ref, o_ref, ba_ref, h_scr = refs
    else:
        x_ref, sh_ref, sc_ref, g_ref, w_ref, wba_ref, o_ref, ba_ref, h_scr = refs

    @pl.when(pl.program_id(1) == 0)
    def _():
        x = x_ref[...]
        if has_pos:
            x = x + pos_ref[...]
        y = x * lax.rsqrt(jnp.mean(x * x, axis=-1, keepdims=True) + EPS) * g_ref[...]
        h = (y * (1.0 + sc_ref[0]) + sh_ref[0]).astype(BF16)
        h_scr[...] = h
        ba_ref[...] = jnp.dot(h, wba_ref[...], preferred_element_type=F32)

    o_ref[...] = jnp.dot(h_scr[...], w_ref[...], preferred_element_type=F32).astype(o_ref.dtype)


def input_projection(x, pos, mod, mod_row_of_tile, g_pre, w_main, w_ba, *, tm, tn):
    n, d = x.shape
    w = w_main.shape[1]
    has_pos = pos is not None
    in_specs = [pl.BlockSpec((tm, d), lambda i, j: (i, 0))]
    args = [x]
    if has_pos:
        pos_tiles = pos.shape[0] // tm
        in_specs.append(pl.BlockSpec((tm, d), lambda i, j: (i % pos_tiles, 0)))
        args.append(pos)
    in_specs += [
        pl.BlockSpec((1, 1, d), lambda i, j: (mod_row_of_tile(i), 0, 0)),
        pl.BlockSpec((1, 1, d), lambda i, j: (mod_row_of_tile(i), 0, 1)),
        pl.BlockSpec((1, d), lambda i, j: (0, 0)),
        pl.BlockSpec((d, tn), lambda i, j: (0, j)),
        pl.BlockSpec((d, LANES), lambda i, j: (0, 0)),
    ]
    args += [mod, mod, g_pre.reshape(1, d), w_main, w_ba]
    return pl.pallas_call(
        functools.partial(_inproj_kernel, has_pos=has_pos),
        grid=(n // tm, w // tn),
        in_specs=in_specs,
        out_specs=[pl.BlockSpec((tm, tn), lambda i, j: (i, j)),
                   pl.BlockSpec((tm, LANES), lambda i, j: (i, 0))],
        out_shape=[jax.ShapeDtypeStruct((n, w), BF16), jax.ShapeDtypeStruct((n, LANES), F32)],
        scratch_shapes=[pltpu.VMEM((tm, d), BF16)],
        compiler_params=_params("arbitrary", "arbitrary"),
        name="input_projection_pos" if has_pos else "input_projection_ctx",
    )(*args)


def _silu(x):
    return x * jax.nn.sigmoid(x)


def _softplus(x):
    return jnp.maximum(x, 0.0) + jnp.log(1.0 + jnp.exp(-jnp.abs(x)))


def _short_conv(src, taps, pad_scr, rows):
    half = SHORT_CONV // 2
    pad_scr[0:SUBLANES, :] = jnp.zeros((SUBLANES, LANES), F32)
    pad_scr[SUBLANES:SUBLANES + rows, :] = src.astype(F32)
    pad_scr[SUBLANES + rows:2 * SUBLANES + rows, :] = jnp.zeros((SUBLANES, LANES), F32)
    acc = None
    for j in range(SHORT_CONV):
        off = SUBLANES + j - half
        term = pad_scr[off:off + rows, :] * taps[j:j + 1, :]
        acc = term if acc is None else acc + term
    return _silu(acc)


def _l2n(x):
    return x * lax.rsqrt(jnp.sum(x * x, axis=-1, keepdims=True) + EPS)


def _gdn_kernel(k_ref, v_ref, q_ref, z_ref, ba_ref, kc_ref, vc_ref, bac_ref,
                wk_ref, wv_ref, wq_ref, ab_ref, gn_ref, o_ref,
                kf, vf, qf, gates, pad_scr, lhs, val, att, ktt, dec, osc, *, seq, ctx, unroll, group):
    total = ctx + seq
    nc = total // CHUNK
    ncc = ctx // CHUNK
    two = 2 * CHUNK
    lg_chunk = CHUNK.bit_length() - 1

    ri = lax.broadcasted_iota(jnp.int32, (two, two), 0)
    ci = lax.broadcasted_iota(jnp.int32, (two, two), 1)
    same_dir = (ri >> lg_chunk) == (ci >> lg_chunk)
    incl = same_dir & (((ri < CHUNK) & (ri >= ci)) | ((ri >= CHUNK) & (ri <= ci)))
    strict = incl & (ri != ci)
    eye = jnp.where(ri == ci, 1.0, 0.0)
    tri = jnp.where(incl, 1.0, 0.0).astype(BF16)
    level_masks = []
    for lg in range(lg_chunk):
        same_parent = (ri >> (lg + 1)) == (ci >> (lg + 1))
        level_masks.append(jnp.where(same_parent & ((ri >> lg) != (ci >> lg)), 1.0, 0.0))
    top_rows = lax.broadcasted_iota(jnp.int32, (two, LANES), 0) < CHUNK

    def bwd_chunk(t):
        return jnp.where(t < ncc, ncc - 1 - t, nc + ncc - 1 - t)

    def stacked(ref, rf, rb):
        return jnp.concatenate([ref[pl.ds(rf, CHUNK), :], ref[pl.ds(rb, CHUNK), :]], axis=0)

    def chunk_load(t):
        rf = pl.multiple_of(t * CHUNK, CHUNK)
        rb = pl.multiple_of(bwd_chunk(t) * CHUNK, CHUNK)
        beta = jnp.concatenate([gates[0, pl.ds(rf, CHUNK), :], gates[1, pl.ds(rb, CHUNK), :]], axis=0)
        g = jnp.concatenate([gates[2, pl.ds(rf, CHUNK), :], gates[3, pl.ds(rb, CHUNK), :]], axis=0)
        return stacked(kf, rf, rb), stacked(vf, rf, rb), stacked(qf, rf, rb), beta, g

    def chunk_triangle(k, q, beta, g):
        kb = k.astype(BF16)
        kq = jnp.concatenate([kb, q.astype(BF16)], axis=0)
        kk_qk = lax.dot_general(kq, kb, (((1,), (1,)), ((), ())), preferred_element_type=F32)
        g_hi = g.astype(BF16)
        r1 = g - g_hi.astype(F32)
        g_mid = r1.astype(BF16)
        g_lo = (r1 - g_mid.astype(F32)).astype(BF16)
        gc3 = jnp.dot(tri, jnp.concatenate([g_hi, g_mid, g_lo], axis=1), preferred_element_type=F32)
        gc = gc3[:, :LANES] + gc3[:, LANES:2 * LANES] + gc3[:, 2 * LANES:]
        gc_row = jnp.transpose(gc)
        decay = jnp.where(incl, jnp.exp(gc - gc_row), 0.0)
        a = jnp.where(strict, beta * kk_qk[:two] * decay, 0.0)
        return a, (kk_qk[two:] * decay).astype(BF16), gc

    def chunk_finish(hh, t, k, v, q, beta, gc, attn, tmat):
        eg = jnp.exp(gc)
        rhs = jnp.concatenate([v * beta, k * beta * eg], axis=1).astype(BF16)
        sol = jnp.dot(tmat.astype(BF16), rhs, preferred_element_type=F32)
        g_end = jnp.where(top_rows, jnp.broadcast_to(gc[CHUNK - 1:CHUNK, :], (two, LANES)),
                          jnp.broadcast_to(gc[CHUNK:CHUNK + 1, :], (two, LANES)))
        k_tail = k * jnp.exp(g_end - gc)
        qg = (q * eg).astype(BF16)
        kcum = sol[:, DV:].astype(BF16)
        r0 = pl.multiple_of(t * two, two)
        lhs[hh, 0, pl.ds(r0, two), :] = jnp.concatenate([kcum[:CHUNK], qg[:CHUNK]], axis=0)
        lhs[hh, 1, pl.ds(r0, two), :] = jnp.concatenate([kcum[CHUNK:], qg[CHUNK:]], axis=0)
        val[hh, pl.ds(r0, two), :] = sol[:, :DV]
        att[hh, pl.ds(r0, two), :] = attn
        ktt[hh, pl.ds(r0, two), :] = jnp.transpose(k_tail).astype(BF16)
        d0 = pl.multiple_of(t * SUBLANES, SUBLANES)
        e_end = jnp.exp(g_end)
        dec[hh, 0, pl.ds(d0, SUBLANES), :] = e_end[:SUBLANES]
        dec[hh, 1, pl.ds(d0, SUBLANES), :] = e_end[CHUNK:CHUNK + SUBLANES]

    for hh in range(group):
        head = pl.program_id(1) * group + hh
        cols = slice(hh * LANES, (hh + 1) * LANES)

        kf[0:ctx, :] = _l2n(_short_conv(kc_ref[:, cols], wk_ref[:, cols], pad_scr, ctx))
        vf[0:ctx, :] = _short_conv(vc_ref[:, cols], wv_ref[:, cols], pad_scr, ctx)
        qf[0:ctx, :] = jnp.zeros((ctx, LANES), F32)
        kf[ctx:total, :] = _l2n(_short_conv(k_ref[:, cols], wk_ref[:, cols], pad_scr, seq))
        vf[ctx:total, :] = _short_conv(v_ref[:, cols], wv_ref[:, cols], pad_scr, seq)
        qf[ctx:total, :] = _l2n(_short_conv(q_ref[:, cols], wq_ref[:, cols], pad_scr, seq)) * (DK ** -0.5)

        def gate_cols(src_ref, lo, rows):
            x = src_ref[...]
            lane = lax.broadcasted_iota(jnp.int32, (rows, LANES), 1)
            beta = jax.nn.sigmoid(x)
            g = -jnp.exp(ab_ref[0:1, :]) * _softplus(x + ab_ref[1:2, :])
            for slot, (arr, base) in enumerate(((beta, 0), (beta, HEADS), (g, 2 * HEADS), (g, 3 * HEADS))):
                col = jnp.sum(jnp.where(lane == base + head, arr, 0.0), axis=-1, keepdims=True)
                gates[slot, lo:lo + rows, :] = jnp.broadcast_to(col, (rows, LANES))

        gate_cols(bac_ref, 0, ctx)
        gate_cols(ba_ref, ctx, seq)

        def chunk_body(i, carry, hh=hh):
            steps = [i * unroll + u for u in range(unroll)]
            loaded = [chunk_load(t) for t in steps]
            tris = [chunk_triangle(k, q, beta, g) for (k, v, q, beta, g) in loaded]
            tmats = [eye - a * level_masks[0] for (a, _, _) in tris]
            for m in level_masks[1:]:
                nxt = []
                for (a, _, _), tmat in zip(tris, tmats):
                    cs = (a * m).astype(BF16)
                    tb = tmat.astype(BF16)
                    tc = jnp.dot(tb, cs, preferred_element_type=F32)
                    nxt.append(tmat - jnp.dot(tc.astype(BF16), tb, preferred_element_type=F32))
                tmats = nxt
            for t, (k, v, q, beta, g), (a, attn, gc), tmat in zip(steps, loaded, tris, tmats):
                chunk_finish(hh, t, k, v, q, beta, gc, attn, tmat)
            return carry

        lax.fori_loop(0, nc // unroll, chunk_body, 0)

    def scan_body(t, carry):
        rf = pl.multiple_of(t * CHUNK, CHUNK)
        rb = pl.multiple_of(bwd_chunk(t) * CHUNK, CHUNK)
        r0 = pl.multiple_of(t * two, two)
        d0 = pl.multiple_of(t * SUBLANES, SUBLANES)
        rs = [(jnp.dot(lhs[hh, 0, pl.ds(r0, two), :], carry[2 * hh].astype(BF16), preferred_element_type=F32),
               jnp.dot(lhs[hh, 1, pl.ds(r0, two), :], carry[2 * hh + 1].astype(BF16), preferred_element_type=F32))
              for hh in range(group)]
        v_news = [val[hh, pl.ds(r0, two), :] - jnp.concatenate([r_f[:CHUNK], r_b[:CHUNK]], axis=0)
                  for hh, (r_f, r_b) in enumerate(rs)]
        outs, new_states = [], []
        for hh, ((r_f, r_b), v_new) in enumerate(zip(rs, v_news)):
            zeros = jnp.zeros_like(v_new)
            v_bd = jnp.concatenate([jnp.where(top_rows, v_new, zeros), jnp.where(top_rows, zeros, v_new)],
                                   axis=1).astype(BF16)
            upd = jnp.dot(ktt[hh, pl.ds(r0, two), :], v_bd, preferred_element_type=F32)
            new_states.append(carry[2 * hh] * dec[hh, 0, pl.ds(d0, SUBLANES), :][0:1, :] + upd[:, :DV])
            new_states.append(carry[2 * hh + 1] * dec[hh, 1, pl.ds(d0, SUBLANES), :][0:1, :] + upd[:, DV:])
            outs.append(jnp.concatenate([r_f[CHUNK:], r_b[CHUNK:]], axis=0)
                        + jnp.dot(att[hh, pl.ds(r0, two), :], v_new.astype(BF16), preferred_element_type=F32))
        for hh, o in enumerate(outs):
            osc[hh, 0, pl.ds(rf, CHUNK), :] = o[:CHUNK]
            osc[hh, 1, pl.ds(rb, CHUNK), :] = o[CHUNK:]
        return tuple(new_states)

    zero = jnp.zeros((DK, DV), F32)
    lax.fori_loop(0, nc, scan_body, (zero,) * (2 * group))

    for hh in range(group):
        cols = slice(hh * LANES, (hh + 1) * LANES)
        o = osc[hh, 0, ctx:total, :] + osc[hh, 1, ctx:total, :]
        o = o * lax.rsqrt(jnp.mean(o * o, axis=-1, keepdims=True) + EPS) * gn_ref[...]
        o_ref[:, cols] = (o * _silu(z_ref[:, cols].astype(F32))).astype(o_ref.dtype)


def gated_deltanet(proj, ba, proj_ctx, ba_ctx, conv_kv, conv_q, a_log, dt_bias, gn, *, batch, seq, ctx,
                   unroll=12, group=2):
    total = seq + ctx
    nc = total // CHUNK
    gl = group * LANES
    ng = HEADS // group
    taps = jnp.zeros((SUBLANES, conv_kv.shape[1]), F32).at[:SHORT_CONV].set(conv_kv)
    taps_q = jnp.zeros((SUBLANES, conv_q.shape[1]), F32).at[:SHORT_CONV].set(conv_q)
    ab = jnp.zeros((SUBLANES, LANES), F32)
    ab = ab.at[0, 2 * HEADS:4 * HEADS].set(a_log.reshape(-1)).at[1, 2 * HEADS:4 * HEADS].set(dt_bias.reshape(-1))
    col = lambda base: (lambda b, h: (b, base * ng + h))
    return pl.pallas_call(
        functools.partial(_gdn_kernel, seq=seq, ctx=ctx, unroll=unroll, group=group),
        grid=(batch, ng),
        in_specs=[
            pl.BlockSpec((seq, gl), col(COL_K)),
            pl.BlockSpec((seq, gl), col(COL_V)),
            pl.BlockSpec((seq, gl), col(COL_Q)),
            pl.BlockSpec((seq, gl), col(COL_Z)),
            pl.BlockSpec((seq, LANES), lambda b, h: (b, 0)),
            pl.BlockSpec((ctx, gl), col(COL_K)),
            pl.BlockSpec((ctx, gl), col(COL_V)),
            pl.BlockSpec((ctx, LANES), lambda b, h: (b, 0)),
            pl.BlockSpec((SUBLANES, gl), lambda b, h: (0, h)),
            pl.BlockSpec((SUBLANES, gl), lambda b, h: (0, ng + h)),
            pl.BlockSpec((SUBLANES, gl), lambda b, h: (0, h)),
            pl.BlockSpec((SUBLANES, LANES), lambda b, h: (0, 0)),
            pl.BlockSpec((1, LANES), lambda b, h: (0, 0)),
        ],
        out_specs=pl.BlockSpec((seq, gl), lambda b, h: (b, h)),
        out_shape=jax.ShapeDtypeStruct((batch * seq, HEADS * DV), BF16),
        scratch_shapes=[
            pltpu.VMEM((total, LANES), F32),
            pltpu.VMEM((total, LANES), F32),
            pltpu.VMEM((total, LANES), F32),
            pltpu.VMEM((4, total, LANES), F32),
            pltpu.VMEM((seq + 2 * SUBLANES, LANES), F32),
            pltpu.VMEM((group, 2, nc * 2 * CHUNK, LANES), BF16),
            pltpu.VMEM((group, nc * 2 * CHUNK, LANES), F32),
            pltpu.VMEM((group, nc * 2 * CHUNK, LANES), BF16),
            pltpu.VMEM((group, nc * 2 * CHUNK, LANES), BF16),
            pltpu.VMEM((group, 2, nc * SUBLANES, LANES), F32),
            pltpu.VMEM((group, 2, total, LANES), F32),
        ],
        compiler_params=_params("arbitrary", "arbitrary"),
        name="gated_deltanet",
    )(proj, proj, proj, proj, ba, proj_ctx, proj_ctx, ba_ctx, taps, taps, taps_q, ab, gn.reshape(1, DV))


CONF_ROWS = 128
CONF_HALO = 16


def _conf_kernel(a_ref, g_ref, w_ref, b_ref, o_ref, pad_scr, *, seq):
    cb = a_ref.shape[1]
    u = a_ref[...].astype(F32) * jax.nn.sigmoid(g_ref[...].astype(F32))
    pad_scr[0:CONF_HALO, :] = jnp.zeros((CONF_HALO, cb), F32)
    pad_scr[CONF_HALO:CONF_HALO + seq, :] = u
    pad_scr[CONF_HALO + seq:2 * CONF_HALO + seq, :] = jnp.zeros((CONF_HALO, cb), F32)
    win = CONF_ROWS + 2 * CONF_HALO
    first = CONF_HALO - CONF_KERNEL // 2

    def body(i, carry):
        r0 = pl.multiple_of(i * CONF_ROWS, CONF_ROWS)
        w = pad_scr[pl.ds(r0, win), :]
        acc = jnp.zeros((CONF_ROWS, cb), F32) + b_ref[...]
        for sub in range(SUBLANES):
            shifted = w if sub == 0 else pltpu.roll(w, win - sub, axis=0)
            for j in range(CONF_KERNEL):
                off = first + j
                if off % SUBLANES == sub:
                    base = off - sub
                    acc = acc + shifted[base:base + CONF_ROWS, :] * w_ref[j:j + 1, :]
        o_ref[pl.ds(r0, CONF_ROWS), :] = acc.astype(o_ref.dtype)
        return carry

    lax.fori_loop(0, seq // CONF_ROWS, body, 0)


def conformer_conv(proj, conf_dw, conf_dw_b, *, batch, seq, cb=256):
    c = conf_dw.shape[1]
    nb = c // cb
    taps = jnp.zeros((32, c), F32).at[:CONF_KERNEL].set(conf_dw)
    return pl.pallas_call(
        functools.partial(_conf_kernel, seq=seq),
        grid=(batch, nb),
        in_specs=[pl.BlockSpec((seq, cb), lambda b, j: (b, COL_GLU_A * nb + j)),
                  pl.BlockSpec((seq, cb), lambda b, j: (b, COL_GLU_G * nb + j)),
                  pl.BlockSpec((32, cb), lambda b, j: (0, j)),
                  pl.BlockSpec((1, cb), lambda b, j: (0, j))],
        out_specs=pl.BlockSpec((seq, cb), lambda b, j: (b, j)),
        out_shape=jax.ShapeDtypeStruct((batch * seq, c), BF16),
        scratch_shapes=[pltpu.VMEM((seq + 2 * CONF_HALO, cb), F32)],
        compiler_params=_params("arbitrary", "arbitrary"),
        name="conformer_conv",
    )(proj, proj, taps, conf_dw_b.reshape(1, c))


def _mixer_out_kernel(og_ref, uc_ref, ga_ref, gb_ref, x_ref, pos_ref, gm_ref, shf_ref, scf_ref,
                      gpost_ref, gpre_ref, lng_ref, lnb_ref, wa_ref, wb_ref, wo_ref, wr_ref, br_ref,
                      x2_ref, h2_ref, ti_ref, tw_ref, cnt_ref, cnt_scr):
    uc = uc_ref[...].astype(F32)
    mu = jnp.mean(uc, axis=-1, keepdims=True)
    var = jnp.mean(jnp.square(uc - mu), axis=-1, keepdims=True)
    u = _silu((uc - mu) * lax.rsqrt(var + EPS) * lng_ref[...] + lnb_ref[...]).astype(BF16)
    ya = jnp.dot(og_ref[...], wa_ref[...], preferred_element_type=F32)
    yb = jnp.dot(u, wb_ref[...], preferred_element_type=F32)
    merged = (jax.nn.sigmoid(ga_ref[...].astype(F32)) * ya
              + jax.nn.sigmoid(gb_ref[...].astype(F32)) * yb).astype(BF16)
    y = jnp.dot(merged, wo_ref[...], preferred_element_type=F32)
    yn = y * lax.rsqrt(jnp.mean(y * y, axis=-1, keepdims=True) + EPS) * gpost_ref[...]
    x2 = x_ref[...] + pos_ref[...] + gm_ref[0] * yn
    x2_ref[...] = x2
    hn = x2 * lax.rsqrt(jnp.mean(x2 * x2, axis=-1, keepdims=True) + EPS) * gpre_ref[...]
    h2 = (hn * (1.0 + scf_ref[0]) + shf_ref[0]).astype(BF16)
    h2_ref[...] = h2
    logits = jnp.dot(h2, wr_ref[...], preferred_element_type=F32) + br_ref[...]
    rows = logits.shape[0]
    lane = lax.broadcasted_iota(jnp.int32, (rows, LANES), 1).astype(F32)
    live = jnp.where(lane < N_EXPERTS, logits, -jnp.inf)
    top_v, top_i = [], []
    for _ in range(TOP_K):
        m = jnp.max(live, axis=-1, keepdims=True)
        idx = jnp.min(jnp.where(live == m, lane, float(LANES)), axis=-1, keepdims=True)
        top_v.append(m)
        top_i.append(idx)
        live = jnp.where(lane == idx, -jnp.inf, live)
    ex = [jnp.exp(v - top_v[0]) for v in top_v]
    denom = ex[0] + ex[1] + ex[2] + ex[3]
    @pl.when(pl.program_id(0) == 0)
    def _():
        cnt_scr[...] = jnp.zeros_like(cnt_scr)

    picked = jnp.zeros((rows, LANES), F32)
    for k in range(TOP_K):
        picked = jnp.where(lane == top_i[k], 1.0, picked)
    ri = lax.broadcasted_iota(jnp.int32, (rows, rows), 0)
    ci = lax.broadcasted_iota(jnp.int32, (rows, rows), 1)
    before = jnp.where(ri > ci, 1.0, 0.0).astype(BF16)
    prefix = jnp.dot(before, picked.astype(BF16), preferred_element_type=F32) + cnt_scr[0:1, :]
    ti = jnp.zeros((rows, LANES), F32)
    tw = jnp.zeros((rows, LANES), F32)
    for k in range(TOP_K):
        rank = jnp.sum(jnp.where(lane == top_i[k], prefix, 0.0), axis=-1, keepdims=True)
        ti = jnp.where(lane == k, top_i[k], ti)
        ti = jnp.where(lane == TOP_K + k, rank, ti)
        tw = jnp.where(lane == k, ex[k] / denom, tw)
    ti_ref[...] = ti.astype(jnp.int32)
    tw_ref[...] = tw
    cnt_scr[...] = cnt_scr[...] + jnp.sum(picked, axis=0, keepdims=True)
    cnt_ref[0] = cnt_scr[...]


def mixer_output(og, uc, proj, x, pos, mod, mod_row_of_tile, g_post, g_pre_f, ln_g, ln_b,
                 w_a, w_b, w_o, w_r, b_r, *, tm):
    n, d = x.shape
    pos_tiles = pos.shape[0] // tm
    row = lambda i: (i, 0)
    fixed = lambda i: (0, 0)
    modspec = lambda k: pl.BlockSpec((1, 1, d), lambda i: (mod_row_of_tile(i), 0, k))
    vec = pl.BlockSpec((1, d), fixed)
    mat = pl.BlockSpec((d, d), fixed)
    return pl.pallas_call(
        _mixer_out_kernel,
        grid=(n // tm,),
        in_specs=[pl.BlockSpec((tm, d), row), pl.BlockSpec((tm, d), row),
                  pl.BlockSpec((tm, d), lambda i: (i, COL_GATE_A)),
                  pl.BlockSpec((tm, d), lambda i: (i, COL_GATE_B)),
                  pl.BlockSpec((tm, d), row),
                  pl.BlockSpec((tm, d), lambda i: (i % pos_tiles, 0)),
                  modspec(2), modspec(3), modspec(4),
                  vec, vec, vec, vec, mat, mat, mat,
                  pl.BlockSpec((d, LANES), fixed), pl.BlockSpec((1, LANES), fixed)],
        out_specs=[pl.BlockSpec((tm, d), row), pl.BlockSpec((tm, d), row),
                   pl.BlockSpec((tm, LANES), row), pl.BlockSpec((tm, LANES), row),
                   pl.BlockSpec((1, SUBLANES, LANES), lambda i: (i, 0, 0))],
        out_shape=[jax.ShapeDtypeStruct((n, d), F32), jax.ShapeDtypeStruct((n, d), BF16),
                   jax.ShapeDtypeStruct((n, LANES), jnp.int32), jax.ShapeDtypeStruct((n, LANES), F32),
                   jax.ShapeDtypeStruct((n // tm, SUBLANES, LANES), F32)],
        scratch_shapes=[pltpu.VMEM((SUBLANES, LANES), F32)],
        compiler_params=_params("arbitrary"),
        name="mixer_output",
    )(og, uc, proj, proj, x, pos, mod, mod, mod,
      g_post.reshape(1, d), g_pre_f.reshape(1, d), ln_g.reshape(1, d), ln_b.reshape(1, d),
      w_a, w_b, w_o, w_r, b_r)


MOE_CHUNK = 512
GATHER_SUB = 256
COMBINE_GROUP = 8


def _expert_kernel(bexp_ref, slo_ref, shi_ref, nused_ref,
                   h2_ref, dest_ref, wt_ref, wgu_ref, bgu_ref, wd_ref, bd_ref, o_ref,
                   acc, rw_acc, wgu_bf, wd_bf):
    b = pl.program_id(0)
    expert = bexp_ref[b]

    @pl.when((b == 0) | (expert != bexp_ref[jnp.maximum(b - 1, 0)]))
    def _():
        wgu_bf[...] = wgu_ref[0].astype(BF16)
        wd_bf[...] = wd_ref[0].astype(BF16)

    @pl.when(b < nused_ref[0])
    def _():
        acc[...] = jnp.zeros_like(acc)
        rw_acc[...] = jnp.zeros_like(rw_acc)
        row = lax.broadcasted_iota(jnp.int32, (MOE_BLOCK, GATHER_SUB), 0) + b * MOE_BLOCK

        def gather(s, carry):
            t0 = pl.multiple_of(s * GATHER_SUB, GATHER_SUB)
            dest = dest_ref[:, pl.ds(t0, GATHER_SUB)]
            wt = wt_ref[:, pl.ds(t0, GATHER_SUB)]
            onehot = jnp.zeros((MOE_BLOCK, GATHER_SUB), F32)
            wsel = jnp.zeros((MOE_BLOCK, GATHER_SUB), F32)
            for k in range(TOP_K):
                hit = row == dest[k:k + 1, :]
                onehot = jnp.where(hit, 1.0, onehot)
                wsel = jnp.where(hit, wt[k:k + 1, :], wsel)
            acc[...] += jnp.dot(onehot.astype(BF16), h2_ref[pl.ds(t0, GATHER_SUB), :],
                                preferred_element_type=F32)
            rw_acc[...] += jnp.sum(wsel, axis=1, keepdims=True)
            return carry

        lax.fori_loop(slo_ref[b], shi_ref[b] + 1, gather, 0)
        gu = jnp.dot(acc[...].astype(BF16), wgu_bf[...], preferred_element_type=F32) + bgu_ref[0]
        dff = gu.shape[1] // 2
        gl = jnp.minimum(gu[:, :dff], SWIGLU_LIMIT)
        lin = jnp.clip(gu[:, dff:], -SWIGLU_LIMIT, SWIGLU_LIMIT)
        act = (gl * jax.nn.sigmoid(SWIGLU_ALPHA * gl) * (lin + 1.0)).astype(BF16)
        y = jnp.dot(act, wd_bf[...], preferred_element_type=F32) + bd_ref[0]
        o_ref[...] = (y * rw_acc[...]).astype(o_ref.dtype)

    @pl.when(b >= nused_ref[0])
    def _():
        o_ref[...] = jnp.zeros_like(o_ref)


def expert_blocks(h2, dest_t, wt_t, block_expert, sub_lo, sub_hi, n_used, w_gate_up, b_gate_up, w_down, b_down):
    n, d = h2.shape
    e, _, f2 = w_gate_up.shape
    n_blocks = block_expert.shape[0]
    whole = lambda b, be, lo, hi, nu: (0, 0)
    by_expert = lambda b, be, lo, hi, nu: (be[b], 0, 0)
    once = pl.Buffered(1)
    grid_spec = pltpu.PrefetchScalarGridSpec(
        num_scalar_prefetch=4,
        grid=(n_blocks,),
        in_specs=[pl.BlockSpec((n, d), whole, pipeline_mode=once),
                  pl.BlockSpec((SUBLANES, n), whole, pipeline_mode=once),
                  pl.BlockSpec((SUBLANES, n), whole, pipeline_mode=once),
                  pl.BlockSpec((1, d, f2), by_expert, pipeline_mode=once),
                  pl.BlockSpec((1, 1, f2), by_expert),
                  pl.BlockSpec((1, f2 // 2, d), by_expert, pipeline_mode=once),
                  pl.BlockSpec((1, 1, d), by_expert)],
        out_specs=pl.BlockSpec((MOE_BLOCK, d), lambda b, be, lo, hi, nu: (b, 0)),
        scratch_shapes=[pltpu.VMEM((MOE_BLOCK, d), F32), pltpu.VMEM((MOE_BLOCK, 1), F32),
                        pltpu.VMEM((d, f2), BF16), pltpu.VMEM((f2 // 2, d), BF16)],
    )
    return pl.pallas_call(
        _expert_kernel,
        grid_spec=grid_spec,
        out_shape=jax.ShapeDtypeStruct((n_blocks * MOE_BLOCK, d), BF16),
        compiler_params=_params("arbitrary"),
        name="expert_blocks",
    )(block_expert, sub_lo, sub_hi, n_used, h2, dest_t, wt_t, w_gate_up, b_gate_up.reshape(e, 1, f2), w_down,
      b_down.reshape(e, 1, d))


def _combine_kernel(coff_ref, cblk_ref, yb_hbm, dest_ref, x2_ref, gf_ref, g_ref, o_ref, slab, sem, acc):
    c = pl.program_id(0)
    start = coff_ref[c]
    count = coff_ref[c + 1] - start
    n_groups = (count + COMBINE_GROUP - 1) // COMBINE_GROUP

    def block_of(g, s):
        return cblk_ref[start + jnp.minimum(g * COMBINE_GROUP + s, count - 1)]

    def copies(g, buf):
        return [pltpu.make_async_copy(
            yb_hbm.at[pl.ds(pl.multiple_of(block_of(g, s) * MOE_BLOCK, MOE_BLOCK), MOE_BLOCK), :],
            slab.at[buf, pl.ds(s * MOE_BLOCK, MOE_BLOCK), :], sem.at[buf]) for s in range(COMBINE_GROUP)]

    for cp in copies(0, 0):
        cp.start()
    acc[...] = jnp.zeros_like(acc)
    chunk = x2_ref.shape[0]
    lane = lax.broadcasted_iota(jnp.int32, (chunk, MOE_BLOCK), 1)

    def group(g, carry):
        buf = g % 2

        @pl.when(g + 1 < n_groups)
        def _():
            for cp in copies(g + 1, 1 - buf):
                cp.start()

        for cp in copies(g, buf):
            cp.wait()
        dest = dest_ref[...]
        pieces = []
        for s in range(COMBINE_GROUP):
            live = g * COMBINE_GROUP + s < count
            col = lane + jnp.where(live, block_of(g, s) * MOE_BLOCK, -MOE_BLOCK)
            piece = jnp.zeros((chunk, MOE_BLOCK), F32)
            for k in range(TOP_K):
                piece = jnp.where(col == dest[:, k:k + 1], 1.0, piece)
            pieces.append(piece.astype(BF16))
        onehot = jnp.concatenate(pieces, axis=1)
        acc[...] += jnp.dot(onehot, slab[buf], preferred_element_type=F32)
        return carry

    lax.fori_loop(0, n_groups, group, 0)
    y = acc[...]
    yn = y * lax.rsqrt(jnp.mean(y * y, axis=-1, keepdims=True) + EPS) * g_ref[...]
    o_ref[...] = x2_ref[...] + gf_ref[0] * yn


def combine_residual(yb, dest, x2, chunk_off, chunk_blocks, mod, mod_row_of_chunk, g_post_f):
    n, d = x2.shape
    chunk_row = lambda c, off, blk: (c, 0)
    grid_spec = pltpu.PrefetchScalarGridSpec(
        num_scalar_prefetch=2,
        grid=(n // MOE_CHUNK,),
        in_specs=[pl.BlockSpec(memory_space=pl.ANY),
                  pl.BlockSpec((MOE_CHUNK, LANES), chunk_row),
                  pl.BlockSpec((MOE_CHUNK, d), chunk_row),
                  pl.BlockSpec((1, 1, d), lambda c, off, blk: (mod_row_of_chunk(c), 0, 5)),
                  pl.BlockSpec((1, d), lambda c, off, blk: (0, 0))],
        out_specs=pl.BlockSpec((MOE_CHUNK, d), chunk_row),
        scratch_shapes=[pltpu.VMEM((2, COMBINE_GROUP * MOE_BLOCK, d), BF16),
                        pltpu.SemaphoreType.DMA((2,)),
                        pltpu.VMEM((MOE_CHUNK, d), F32)],
    )
    return pl.pallas_call(
        _combine_kernel,
        grid_spec=grid_spec,
        out_shape=jax.ShapeDtypeStruct((n, d), F32),
        compiler_params=_params("arbitrary"),
        name="combine_residual",
    )(chunk_off, chunk_blocks, yb, dest, x2, mod, g_post_f.reshape(1, d))


def _grid_pos_embedding(rows, d):
    row = jnp.repeat(jnp.arange(rows, dtype=F32), GRID_W)
    col = jnp.tile(jnp.arange(GRID_W, dtype=F32), rows)
    quarter = d // 4
    omega = POS_BASE ** (-jnp.arange(quarter, dtype=F32) / quarter)

    def emb(p):
        ang = p[:, None] * omega[None, :]
        return jnp.concatenate([jnp.sin(ang), jnp.cos(ang)], axis=-1)

    return jnp.concatenate([emb(row), emb(col)], axis=-1)


def _count_le(sorted_vals, queries):
    return jnp.sum(sorted_vals <= queries[:, None], axis=1).astype(jnp.int32)


def _moe_plan(ti, tw, cnt_after, n_tok, tm):
    assert tm == GATHER_SUB
    e4 = ti[:, :TOP_K]
    rank4 = ti[:, TOP_K:2 * TOP_K]
    sub_after = cnt_after[:, 0, :N_EXPERTS].astype(jnp.int32)
    counts = sub_after[-1]
    padded = (counts + MOE_BLOCK - 1) // MOE_BLOCK * MOE_BLOCK
    padded_end = jnp.cumsum(padded)
    padded_start = padded_end - padded
    dest4 = padded_start[e4] + rank4
    n_blocks = n_tok * TOP_K // MOE_BLOCK + N_EXPERTS
    blocks = jnp.arange(n_blocks, dtype=jnp.int32)
    block_expert = jnp.minimum(_count_le(padded_end[None, :], blocks * MOE_BLOCK), N_EXPERTS - 1)
    n_used = (padded_end[-1:] // MOE_BLOCK).astype(jnp.int32)
    r_lo = (blocks - padded_start[block_expert] // MOE_BLOCK) * MOE_BLOCK
    r_hi = jnp.minimum(counts[block_expert], r_lo + MOE_BLOCK) - 1
    through = sub_after.T[block_expert]
    n_sub = sub_after.shape[0]
    sub_lo = jnp.minimum(_count_le(through, r_lo), n_sub - 1)
    sub_hi = jnp.minimum(_count_le(through, r_hi), n_sub - 1)
    nch = n_tok // MOE_CHUNK
    per_chunk = MOE_CHUNK // tm
    cb_after = sub_after[per_chunk - 1::per_chunk]
    cb_before = jnp.concatenate([jnp.zeros((1, N_EXPERTS), jnp.int32), cb_after[:-1]], axis=0)
    b_lo = ((padded_start[None, :] + cb_before) // MOE_BLOCK).reshape(-1)
    b_hi = ((padded_start[None, :] + cb_after - 1) // MOE_BLOCK).reshape(-1)
    n_it = jnp.where((cb_after > cb_before).reshape(-1), b_hi - b_lo + 1, 0)
    off_end = jnp.cumsum(n_it)
    off = off_end - n_it
    n_items = n_blocks + N_EXPERTS * (nch - 1)
    w = jnp.minimum(jnp.arange(n_items, dtype=jnp.int32), off_end[-1] - 1)
    cell = jnp.minimum(_count_le(off_end[None, :], w), n_it.shape[0] - 1)
    chunk_blocks = (b_lo[cell] + w - off[cell]).astype(jnp.int32)
    chunk_off = jnp.concatenate([jnp.zeros((1,), jnp.int32), off_end[N_EXPERTS - 1::N_EXPERTS]]).astype(jnp.int32)
    pad_rows = SUBLANES - TOP_K
    dest_t = jnp.concatenate([dest4.T, jnp.full((pad_rows, n_tok), -1, jnp.int32)], axis=0)
    wt_t = jnp.concatenate([tw[:, :TOP_K].T, jnp.zeros((pad_rows, n_tok), F32)], axis=0)
    dest_l = jnp.concatenate([dest4, jnp.full((n_tok, LANES - TOP_K), -1, jnp.int32)], axis=1)
    return dest_t, wt_t, dest_l, block_expert, sub_lo, sub_hi, n_used, chunk_off, chunk_blocks


def kernel(x, c, ctx, c_ctx, w_mod, b_mod, g_pre_mix, g_post_mix, g_pre_ffn, g_post_ffn, w_in, conv_kv,
           conv_q, a_log, dt_bias, gdn_norm_g, w_proj_a, conf_dw, conf_dw_b, conf_ln_g, conf_ln_b,
           w_proj_b, w_out, w_router, b_router, w_gate_up, b_gate_up, w_down, b_down):
    batch, seq, d = x.shape
    ctx_len = ctx.shape[1]
    n_tok = batch * seq
    gw = HEADS * DV
    beta_off = HEADS * DK + gw
    state_cols = beta_off + 4 * HEADS
    q_off = state_cols

    mod_rows = 2 * SUBLANES
    cond = jnp.zeros((mod_rows, d), F32).at[:batch].set(c).at[batch].set(c_ctx)
    mod = modulation(cond, w_mod[0], b_mod[0]).reshape(mod_rows, 1, 6 * d)

    w = w_in[0]
    w_main = jnp.concatenate([w[:, :beta_off], w[:, q_off:]], axis=1).astype(BF16)
    w_ba = jnp.zeros((d, LANES), F32).at[:, :4 * HEADS].set(w[:, beta_off:state_cols]).astype(BF16)

    pos = _grid_pos_embedding(seq // GRID_W, d)
    x_flat = x.reshape(n_tok, d)
    tm = 1024
    tiles_per_seq = seq // tm
    proj, ba = input_projection(x_flat, pos, mod, lambda i: i // tiles_per_seq, g_pre_mix[0],
                                w_main, w_ba, tm=tm, tn=1024)
    proj_ctx, ba_ctx = input_projection(ctx.reshape(batch * ctx_len, d), None, mod, lambda i: batch,
                                        g_pre_mix[0], w_main[:, :beta_off], w_ba, tm=ctx_len, tn=1024)

    og = gated_deltanet(proj, ba, proj_ctx, ba_ctx, conv_kv[0], conv_q[0], a_log[0], dt_bias[0],
                        gdn_norm_g[0], batch=batch, seq=seq, ctx=ctx_len)
    uc = conformer_conv(proj, conf_dw[0], conf_dw_b[0], batch=batch, seq=seq)

    tm2 = 256
    w_r = jnp.zeros((d, LANES), F32).at[:, :N_EXPERTS].set(w_router[0]).astype(BF16)
    b_r = jnp.zeros((1, LANES), F32).at[0, :N_EXPERTS].set(b_router[0])
    x2, h2, top_i, top_w, cnt_after = mixer_output(
        og, uc, proj, x_flat, pos, mod, lambda i: i // (seq // tm2), g_post_mix[0], g_pre_ffn[0],
        conf_ln_g[0], conf_ln_b[0], w_proj_a[0].astype(BF16), w_proj_b[0].astype(BF16),
        w_out[0].astype(BF16), w_r, b_r, tm=tm2)

    dest_t, wt_t, dest_l, block_expert, sub_lo, sub_hi, n_used, chunk_off, chunk_blocks = _moe_plan(
        top_i, top_w, cnt_after, n_tok, tm2)
    yb = expert_blocks(h2, dest_t, wt_t, block_expert, sub_lo, sub_hi, n_used, w_gate_up[0], b_gate_up[0],
                       w_down[0], b_down[0])
    out = combine_residual(yb, dest_l, x2, chunk_off, chunk_blocks, mod, lambda c: c // (seq // MOE_CHUNK),
                           g_post_ffn[0])
    return out.reshape(batch, seq, d)
```

```python
import functools

import jax
import jax.numpy as jnp
from jax import lax
from jax.experimental import pallas as pl
from jax.experimental.pallas import tpu as pltpu

F32 = jnp.float32
BF16 = jnp.bfloat16

D_MODEL = 1024
GRID_W = 64
HEADS = 8
DK = 128
DV = 128
SHORT_CONV = 5
CHUNK = 64
CONF_KERNEL = 31
N_EXPERTS = 32
TOP_K = 4
SWIGLU_LIMIT = 7.0
SWIGLU_ALPHA = 1.702
MOE_BLOCK = 128
EPS = 1e-6
POS_BASE = 10000.0

LANES = 128
SUBLANES = 8
VMEM_LIMIT = 56 * 1024 * 1024

COL_K, COL_V, COL_Q, COL_Z, COL_GLU_A, COL_GLU_G, COL_GATE_A, COL_GATE_B = range(8)


def _params(*sem):
    return pltpu.CompilerParams(dimension_semantics=sem, vmem_limit_bytes=VMEM_LIMIT)


def _mod_kernel(c_ref, w_ref, b_ref, o_ref):
    c = c_ref[...]
    s = c * jax.nn.sigmoid(c)
    o_ref[...] = jnp.dot(s, w_ref[...], preferred_element_type=F32,
                         precision=lax.Precision.HIGHEST) + b_ref[...]


def modulation(cond, w_mod, b_mod):
    r, d = cond.shape
    n = w_mod.shape[1]
    tn = 1024
    return pl.pallas_call(
        _mod_kernel,
        grid=(n // tn,),
        in_specs=[pl.BlockSpec((r, d), lambda j: (0, 0)),
                  pl.BlockSpec((d, tn), lambda j: (0, j)),
                  pl.BlockSpec((1, tn), lambda j: (0, j))],
        out_specs=pl.BlockSpec((r, tn), lambda j: (0, j)),
        out_shape=jax.ShapeDtypeStruct((r, n), F32),
        compiler_params=_params("arbitrary"),
        name="modulation",
    )(cond, w_mod, b_mod.reshape(1, n))


def _inproj_kernel(*refs, has_pos):
    if has_pos:
        x_ref, pos_ref, sh_ref, sc_ref, g_ref, w_ref, wba_ref, o_ref, ba_ref, h_scr = refs
    else:
        x_ref, sh_ref, sc_ref, g_ref, w_ref, wba_ref, o_ref, ba_ref, h_scr = refs

    @pl.when(pl.program_id(1) == 0)
    def _():
        x = x_ref[...]
        if has_pos:
            x = x + pos_ref[...]
        y = x * lax.rsqrt(jnp.mean(x * x, axis=-1, keepdims=True) + EPS) * g_ref[...]
        h = (y * (1.0 + sc_ref[0]) + sh_ref[0]).astype(BF16)
        h_scr[...] = h
        ba_ref[...] = jnp.dot(h, wba_ref[...], preferred_element_type=F32)

    o_ref[...] = jnp.dot(h_scr[...], w_ref[...], preferred_element_type=F32).astype(o_ref.dtype)


def input_projection(x, pos, mod, mod_row_of_tile, g_pre, w_main, w_ba, *, tm, tn):
    n, d = x.shape
    w = w_main.shape[1]
    has_pos = pos is not None
    in_specs = [pl.BlockSpec((tm, d), lambda i, j: (i, 0))]
    args = [x]
    if has_pos:
        pos_tiles = pos.shape[0] // tm
        in_specs.append(pl.BlockSpec((tm, d), lambda i, j: (i % pos_tiles, 0)))
        args.append(pos)
    in_specs += [
        pl.BlockSpec((1, 1, d), lambda i, j: (mod_row_of_tile(i), 0, 0)),
        pl.BlockSpec((1, 1, d), lambda i, j: (mod_row_of_tile(i), 0, 1)),
        pl.BlockSpec((1, d), lambda i, j: (0, 0)),
        pl.BlockSpec((d, tn), lambda i, j: (0, j)),
        pl.BlockSpec((d, LANES), lambda i, j: (0, 0)),
    ]
    args += [mod, mod, g_pre.reshape(1, d), w_main, w_ba]
    return pl.pallas_call(
        functools.partial(_inproj_kernel, has_pos=has_pos),
        grid=(n // tm, w // tn),
        in_specs=in_specs,
        out_specs=[pl.BlockSpec((tm, tn), lambda i, j: (i, j)),
                   pl.BlockSpec((tm, LANES), lambda i, j: (i, 0))],
        out_shape=[jax.ShapeDtypeStruct((n, w), BF16), jax.ShapeDtypeStruct((n, LANES), F32)],
        scratch_shapes=[pltpu.VMEM((tm, d), BF16)],
        compiler_params=_params("arbitrary", "arbitrary"),
        name="input_projection_pos" if has_pos else "input_projection_ctx",
    )(*args)


def _silu(x):
    return x * jax.nn.sigmoid(x)


def _softplus(x):
    return jnp.maximum(x, 0.0) + jnp.log(1.0 + jnp.exp(-jnp.abs(x)))


def _short_conv(src, taps, pad_scr, rows):
    half = SHORT_CONV // 2
    pad_scr[0:SUBLANES, :] = jnp.zeros((SUBLANES, LANES), F32)
    pad_scr[SUBLANES:SUBLANES + rows, :] = src.astype(F32)
    pad_scr[SUBLANES + rows:2 * SUBLANES + rows, :] = jnp.zeros((SUBLANES, LANES), F32)
    acc = None
    for j in range(SHORT_CONV):
        off = SUBLANES + j - half
        term = pad_scr[off:off + rows, :] * taps[j:j + 1, :]
        acc = term if acc is None else acc + term
    return _silu(acc)


def _l2n(x):
    return x * lax.rsqrt(jnp.sum(x * x, axis=-1, keepdims=True) + EPS)


def _gdn_kernel(k_ref, v_ref, q_ref, z_ref, ba_ref, kc_ref, vc_ref, bac_ref,
                wk_ref, wv_ref, wq_ref, ab_ref, gn_ref, o_ref,
                kf, vf, qf, gates, pad_scr, lhs, val, att, ktt, dec, osc, *, seq, ctx, unroll, group):
    total = ctx + seq
    nc = total // CHUNK
    ncc = ctx // CHUNK
    two = 2 * CHUNK
    lg_chunk = CHUNK.bit_length() - 1

    ri = lax.broadcasted_iota(jnp.int32, (two, two), 0)
    ci = lax.broadcasted_iota(jnp.int32, (two, two), 1)
    same_dir = (ri >> lg_chunk) == (ci >> lg_chunk)
    incl = same_dir & (((ri < CHUNK) & (ri >= ci)) | ((ri >= CHUNK) & (ri <= ci)))
    strict = incl & (ri != ci)
    eye = jnp.where(ri == ci, 1.0, 0.0)
    tri = jnp.where(incl, 1.0, 0.0).astype(BF16)
    level_masks = []
    for lg in range(lg_chunk):
        same_parent = (ri >> (lg + 1)) == (ci >> (lg + 1))
        level_masks.append(jnp.where(same_parent & ((ri >> lg) != (ci >> lg)), 1.0, 0.0))
    top_rows = lax.broadcasted_iota(jnp.int32, (two, LANES), 0) < CHUNK

    def bwd_chunk(t):
        return jnp.where(t < ncc, ncc - 1 - t, nc + ncc - 1 - t)

    def stacked(ref, rf, rb):
        return jnp.concatenate([ref[pl.ds(rf, CHUNK), :], ref[pl.ds(rb, CHUNK), :]], axis=0)

    def chunk_load(t):
        rf = pl.multiple_of(t * CHUNK, CHUNK)
        rb = pl.multiple_of(bwd_chunk(t) * CHUNK, CHUNK)
        beta = jnp.concatenate([gates[0, pl.ds(rf, CHUNK), :], gates[1, pl.ds(rb, CHUNK), :]], axis=0)
        g = jnp.concatenate([gates[2, pl.ds(rf, CHUNK), :], gates[3, pl.ds(rb, CHUNK), :]], axis=0)
        return stacked(kf, rf, rb), stacked(vf, rf, rb), stacked(qf, rf, rb), beta, g

    def chunk_triangle(k, q, beta, g):
        kb = k.astype(BF16)
        kq = jnp.concatenate([kb, q.astype(BF16)], axis=0)
        kk_qk = lax.dot_general(kq, kb, (((1,), (1,)), ((), ())), preferred_element_type=F32)
        g_hi = g.astype(BF16)
        r1 = g - g_hi.astype(F32)
        g_mid = r1.astype(BF16)
        g_lo = (r1 - g_mid.astype(F32)).astype(BF16)
        gc3 = jnp.dot(tri, jnp.concatenate([g_hi, g_mid, g_lo], axis=1), preferred_element_type=F32)
        gc = gc3[:, :LANES] + gc3[:, LANES:2 * LANES] + gc3[:, 2 * LANES:]
        gc_row = jnp.transpose(gc)
        decay = jnp.where(incl, jnp.exp(gc - gc_row), 0.0)
        a = jnp.where(strict, beta * kk_qk[:two] * decay, 0.0)
        return a, (kk_qk[two:] * decay).astype(BF16), gc

    def chunk_finish(hh, t, k, v, q, beta, gc, attn, tmat):
        eg = jnp.exp(gc)
        rhs = jnp.concatenate([v * beta, k * beta * eg], axis=1).astype(BF16)
        sol = jnp.dot(tmat.astype(BF16), rhs, preferred_element_type=F32)
        g_end = jnp.where(top_rows, jnp.broadcast_to(gc[CHUNK - 1:CHUNK, :], (two, LANES)),
                          jnp.broadcast_to(gc[CHUNK:CHUNK + 1, :], (two, LANES)))
        k_tail = k * jnp.exp(g_end - gc)
        qg = (q * eg).astype(BF16)
        kcum = sol[:, DV:].astype(BF16)
        r0 = pl.multiple_of(t * two, two)
        lhs[hh, 0, pl.ds(r0, two), :] = jnp.concatenate([kcum[:CHUNK], qg[:CHUNK]], axis=0)
        lhs[hh, 1, pl.ds(r0, two), :] = jnp.concatenate([kcum[CHUNK:], qg[CHUNK:]], axis=0)
        val[hh, pl.ds(r0, two), :] = sol[:, :DV]
        att[hh, pl.ds(r0, two), :] = attn
        ktt[hh, pl.ds(r0, two), :] = jnp.transpose(k_tail).astype(BF16)
        d0 = pl.multiple_of(t * SUBLANES, SUBLANES)
        e_end = jnp.exp(g_end)
        dec[hh, 0, pl.ds(d0, SUBLANES), :] = e_end[:SUBLANES]
        dec[hh, 1, pl.ds(d0, SUBLANES), :] = e_end[CHUNK:CHUNK + SUBLANES]

    for hh in range(group):
        head = pl.program_id(1) * group + hh
        cols = slice(hh * LANES, (hh + 1) * LANES)

        kf[0:ctx, :] = _l2n(_short_conv(kc_ref[:, cols], wk_ref[:, cols], pad_scr, ctx))
        vf[0:ctx, :] = _short_conv(vc_ref[:, cols], wv_ref[:, cols], pad_scr, ctx)
        qf[0:ctx, :] = jnp.zeros((ctx, LANES), F32)
        kf[ctx:total, :] = _l2n(_short_conv(k_ref[:, cols], wk_ref[:, cols], pad_scr, seq))
        vf[ctx:total, :] = _short_conv(v_ref[:, cols], wv_ref[:, cols], pad_scr, seq)
        qf[ctx:total, :] = _l2n(_short_conv(q_ref[:, cols], wq_ref[:, cols], pad_scr, seq)) * (DK ** -0.5)

        def gate_cols(src_ref, lo, rows):
            x = src_ref[...]
            lane = lax.broadcasted_iota(jnp.int32, (rows, LANES), 1)
            beta = jax.nn.sigmoid(x)
            g = -jnp.exp(ab_ref[0:1, :]) * _softplus(x + ab_ref[1:2, :])
            for slot, (arr, base) in enumerate(((beta, 0), (beta, HEADS), (g, 2 * HEADS), (g, 3 * HEADS))):
                col = jnp.sum(jnp.where(lane == base + head, arr, 0.0), axis=-1, keepdims=True)
                gates[slot, lo:lo + rows, :] = jnp.broadcast_to(col, (rows, LANES))

        gate_cols(bac_ref, 0, ctx)
        gate_cols(ba_ref, ctx, seq)

        def chunk_body(i, carry, hh=hh):
            steps = [i * unroll + u for u in range(unroll)]
            loaded = [chunk_load(t) for t in steps]
            tris = [chunk_triangle(k, q, beta, g) for (k, v, q, beta, g) in loaded]
            tmats = [eye - a * level_masks[0] for (a, _, _) in tris]
            for m in level_masks[1:]:
                nxt = []
                for (a, _, _), tmat in zip(tris, tmats):
                    cs = (a * m).astype(BF16)
                    tb = tmat.astype(BF16)
                    tc = jnp.dot(tb, cs, preferred_element_type=F32)
                    nxt.append(tmat - jnp.dot(tc.astype(BF16), tb, preferred_element_type=F32))
                tmats = nxt
            for t, (k, v, q, beta, g), (a, attn, gc), tmat in zip(steps, loaded, tris, tmats):
                chunk_finish(hh, t, k, v, q, beta, gc, attn, tmat)
            return carry

        lax.fori_loop(0, nc // unroll, chunk_body, 0)

    def scan_body(t, carry):
        rf = pl.multiple_of(t * CHUNK, CHUNK)
        rb = pl.multiple_of(bwd_chunk(t) * CHUNK, CHUNK)
        r0 = pl.multiple_of(t * two, two)
        d0 = pl.multiple_of(t * SUBLANES, SUBLANES)
        rs = [(jnp.dot(lhs[hh, 0, pl.ds(r0, two), :], carry[2 * hh].astype(BF16), preferred_element_type=F32),
               jnp.dot(lhs[hh, 1, pl.ds(r0, two), :], carry[2 * hh + 1].astype(BF16), preferred_element_type=F32))
              for hh in range(group)]
        v_news = [val[hh, pl.ds(r0, two), :] - jnp.concatenate([r_f[:CHUNK], r_b[:CHUNK]], axis=0)
                  for hh, (r_f, r_b) in enumerate(rs)]
        outs, new_states = [], []
        for hh, ((r_f, r_b), v_new) in enumerate(zip(rs, v_news)):
            zeros = jnp.zeros_like(v_new)
            v_bd = jnp.concatenate([jnp.where(top_rows, v_new, zeros), jnp.where(top_rows, zeros, v_new)],
                                   axis=1).astype(BF16)
            upd = jnp.dot(ktt[hh, pl.ds(r0, two), :], v_bd, preferred_element_type=F32)
            new_states.append(carry[2 * hh] * dec[hh, 0, pl.ds(d0, SUBLANES), :][0:1, :] + upd[:, :DV])
            new_states.append(carry[2 * hh + 1] * dec[hh, 1, pl.ds(d0, SUBLANES), :][0:1, :] + upd[:, DV:])
            outs.append(jnp.concatenate([r_f[CHUNK:], r_b[CHUNK:]], axis=0)
                        + jnp.dot(att[hh, pl.ds(r0, two), :], v_new.astype(BF16), preferred_element_type=F32))
        for hh, o in enumerate(outs):
            osc[hh, 0, pl.ds(rf, CHUNK), :] = o[:CHUNK]
            osc[hh, 1, pl.ds(rb, CHUNK), :] = o[CHUNK:]
        return tuple(new_states)

    zero = jnp.zeros((DK, DV), F32)
    lax.fori_loop(0, nc, scan_body, (zero,) * (2 * group))

    for hh in range(group):
        cols = slice(hh * LANES, (hh + 1) * LANES)
        o = osc[hh, 0, ctx:total, :] + osc[hh, 1, ctx:total, :]
        o = o * lax.rsqrt(jnp.mean(o * o, axis=-1, keepdims=True) + EPS) * gn_ref[...]
        o_ref[:, cols] = (o * _silu(z_ref[:, cols].astype(F32))).astype(o_ref.dtype)


def gated_deltanet(proj, ba, proj_ctx, ba_ctx, conv_kv, conv_q, a_log, dt_bias, gn, *, batch, seq, ctx,
                   unroll=12, group=2):
    total = seq + ctx
    nc = total // CHUNK
    gl = group * LANES
    ng = HEADS // group
    taps = jnp.zeros((SUBLANES, conv_kv.shape[1]), F32).at[:SHORT_CONV].set(conv_kv)
    taps_q = jnp.zeros((SUBLANES, conv_q.shape[1]), F32).at[:SHORT_CONV].set(conv_q)
    ab = jnp.zeros((SUBLANES, LANES), F32)
    ab = ab.at[0, 2 * HEADS:4 * HEADS].set(a_log.reshape(-1)).at[1, 2 * HEADS:4 * HEADS].set(dt_bias.reshape(-1))
    col = lambda base: (lambda b, h: (b, base * ng + h))
    return pl.pallas_call(
        functools.partial(_gdn_kernel, seq=seq, ctx=ctx, unroll=unroll, group=group),
        grid=(batch, ng),
        in_specs=[
            pl.BlockSpec((seq, gl), col(COL_K)),
            pl.BlockSpec((seq, gl), col(COL_V)),
            pl.BlockSpec((seq, gl), col(COL_Q)),
            pl.BlockSpec((seq, gl), col(COL_Z)),
            pl.BlockSpec((seq, LANES), lambda b, h: (b, 0)),
            pl.BlockSpec((ctx, gl), col(COL_K)),
            pl.BlockSpec((ctx, gl), col(COL_V)),
            pl.BlockSpec((ctx, LANES), lambda b, h: (b, 0)),
            pl.BlockSpec((SUBLANES, gl), lambda b, h: (0, h)),
            pl.BlockSpec((SUBLANES, gl), lambda b, h: (0, ng + h)),
            pl.BlockSpec((SUBLANES, gl), lambda b, h: (0, h)),
            pl.BlockSpec((SUBLANES, LANES), lambda b, h: (0, 0)),
            pl.BlockSpec((1, LANES), lambda b, h: (0, 0)),
        ],
        out_specs=pl.BlockSpec((seq, gl), lambda b, h: (b, h)),
        out_shape=jax.ShapeDtypeStruct((batch * seq, HEADS * DV), BF16),
        scratch_shapes=[
            pltpu.VMEM((total, LANES), F32),
            pltpu.VMEM((total, LANES), F32),
            pltpu.VMEM((total, LANES), F32),
            pltpu.VMEM((4, total, LANES), F32),
            pltpu.VMEM((seq + 2 * SUBLANES, LANES), F32),
            pltpu.VMEM((group, 2, nc * 2 * CHUNK, LANES), BF16),
            pltpu.VMEM((group, nc * 2 * CHUNK, LANES), F32),
            pltpu.VMEM((group, nc * 2 * CHUNK, LANES), BF16),
            pltpu.VMEM((group, nc * 2 * CHUNK, LANES), BF16),
            pltpu.VMEM((group, 2, nc * SUBLANES, LANES), F32),
            pltpu.VMEM((group, 2, total, LANES), F32),
        ],
        compiler_params=_params("arbitrary", "arbitrary"),
        name="gated_deltanet",
    )(proj, proj, proj, proj, ba, proj_ctx, proj_ctx, ba_ctx, taps, taps, taps_q, ab, gn.reshape(1, DV))


CONF_ROWS = 128
CONF_HALO = 16


def _conf_kernel(a_ref, g_ref, w_ref, b_ref, o_ref, pad_scr, *, seq):
    cb = a_ref.shape[1]
    u = a_ref[...].astype(F32) * jax.nn.sigmoid(g_ref[...].astype(F32))
    pad_scr[0:CONF_HALO, :] = jnp.zeros((CONF_HALO, cb), F32)
    pad_scr[CONF_HALO:CONF_HALO + seq, :] = u
    pad_scr[CONF_HALO + seq:2 * CONF_HALO + seq, :] = jnp.zeros((CONF_HALO, cb), F32)
    win = CONF_ROWS + 2 * CONF_HALO
    first = CONF_HALO - CONF_KERNEL // 2

    def body(i, carry):
        r0 = pl.multiple_of(i * CONF_ROWS, CONF_ROWS)
        w = pad_scr[pl.ds(r0, win), :]
        acc = jnp.zeros((CONF_ROWS, cb), F32) + b_ref[...]
        for sub in range(SUBLANES):
            shifted = w if sub == 0 else pltpu.roll(w, win - sub, axis=0)
            for j in range(CONF_KERNEL):
                off = first + j
                if off % SUBLANES == sub:
                    base = off - sub
                    acc = acc + shifted[base:base + CONF_ROWS, :] * w_ref[j:j + 1, :]
        o_ref[pl.ds(r0, CONF_ROWS), :] = acc.astype(o_ref.dtype)
        return carry

    lax.fori_loop(0, seq // CONF_ROWS, body, 0)


def conformer_conv(proj, conf_dw, conf_dw_b, *, batch, seq, cb=256):
    c = conf_dw.shape[1]
    nb = c // cb
    taps = jnp.zeros((32, c), F32).at[:CONF_KERNEL].set(conf_dw)
    return pl.pallas_call(
        functools.partial(_conf_kernel, seq=seq),
        grid=(batch, nb),
        in_specs=[pl.BlockSpec((seq, cb), lambda b, j: (b, COL_GLU_A * nb + j)),
                  pl.BlockSpec((seq, cb), lambda b, j: (b, COL_GLU_G * nb + j)),
                  pl.BlockSpec((32, cb), lambda b, j: (0, j)),
                  pl.BlockSpec((1, cb), lambda b, j: (0, j))],
        out_specs=pl.BlockSpec((seq, cb), lambda b, j: (b, j)),
        out_shape=jax.ShapeDtypeStruct((batch * seq, c), BF16),
        scratch_shapes=[pltpu.VMEM((seq + 2 * CONF_HALO, cb), F32)],
        compiler_params=_params("arbitrary", "arbitrary"),
        name="conformer_conv",
    )(proj, proj, taps, conf_dw_b.reshape(1, c))


def _mixer_out_kernel(og_ref, uc_ref, ga_ref, gb_ref, x_ref, pos_ref, gm_ref, shf_ref, scf_ref,
                      gpost_ref, gpre_ref, lng_ref, lnb_ref, wa_ref, wb_ref, wo_ref, wr_ref, br_ref,
                      x2_ref, h2_ref, ti_ref, tw_ref, cnt_ref, cnt_scr):
    uc = uc_ref[...].astype(F32)
    mu = jnp.mean(uc, axis=-1, keepdims=True)
    var = jnp.mean(jnp.square(uc - mu), axis=-1, keepdims=True)
    u = _silu((uc - mu) * lax.rsqrt(var + EPS) * lng_ref[...] + lnb_ref[...]).astype(BF16)
    ya = jnp.dot(og_ref[...], wa_ref[...], preferred_element_type=F32)
    yb = jnp.dot(u, wb_ref[...], preferred_element_type=F32)
    merged = (jax.nn.sigmoid(ga_ref[...].astype(F32)) * ya
              + jax.nn.sigmoid(gb_ref[...].astype(F32)) * yb).astype(BF16)
    y = jnp.dot(merged, wo_ref[...], preferred_element_type=F32)
    yn = y * lax.rsqrt(jnp.mean(y * y, axis=-1, keepdims=True) + EPS) * gpost_ref[...]
    x2 = x_ref[...] + pos_ref[...] + gm_ref[0] * yn
    x2_ref[...] = x2
    hn = x2 * lax.rsqrt(jnp.mean(x2 * x2, axis=-1, keepdims=True) + EPS) * gpre_ref[...]
    h2 = (hn * (1.0 + scf_ref[0]) + shf_ref[0]).astype(BF16)
    h2_ref[...] = h2
    logits = jnp.dot(h2, wr_ref[...], preferred_element_type=F32) + br_ref[...]
    rows = logits.shape[0]
    lane = lax.broadcasted_iota(jnp.int32, (rows, LANES), 1).astype(F32)
    live = jnp.where(lane < N_EXPERTS, logits, -jnp.inf)
    top_v, top_i = [], []
    for _ in range(TOP_K):
        m = jnp.max(live, axis=-1, keepdims=True)
        idx = jnp.min(jnp.where(live == m, lane, float(LANES)), axis=-1, keepdims=True)
        top_v.append(m)
        top_i.append(idx)
        live = jnp.where(lane == idx, -jnp.inf, live)
    ex = [jnp.exp(v - top_v[0]) for v in top_v]
    denom = ex[0] + ex[1] + ex[2] + ex[3]
    @pl.when(pl.program_id(0) == 0)
    def _():
        cnt_scr[...] = jnp.zeros_like(cnt_scr)

    picked = jnp.zeros((rows, LANES), F32)
    for k in range(TOP_K):
        picked = jnp.where(lane == top_i[k], 1.0, picked)
    ri = lax.broadcasted_iota(jnp.int32, (rows, rows), 0)
    ci = lax.broadcasted_iota(jnp.int32, (rows, rows), 1)
    before = jnp.where(ri > ci, 1.0, 0.0).astype(BF16)
    prefix = jnp.dot(before, picked.astype(BF16), preferred_element_type=F32) + cnt_scr[0:1, :]
    ti = jnp.zeros((rows, LANES), F32)
    tw = jnp.zeros((rows, LANES), F32)
    for k in range(TOP_K):
        rank = jnp.sum(jnp.where(lane == top_i[k], prefix, 0.0), axis=-1, keepdims=True)
        ti = jnp.where(lane == k, top_i[k], ti)
        ti = jnp.where(lane == TOP_K + k, rank, ti)
        tw = jnp.where(lane == k, ex[k] / denom, tw)
    ti_ref[...] = jnp.transpose(ti)[:SUBLANES, :].astype(jnp.int32)
    tw_ref[...] = jnp.transpose(tw)[:SUBLANES, :]
    cnt_scr[...] = cnt_scr[...] + jnp.sum(picked, axis=0, keepdims=True)
    cnt_ref[0] = cnt_scr[...]


def mixer_output(og, uc, proj, x, pos, mod, mod_row_of_tile, g_post, g_pre_f, ln_g, ln_b,
                 w_a, w_b, w_o, w_r, b_r, *, tm):
    n, d = x.shape
    pos_tiles = pos.shape[0] // tm
    row = lambda i: (i, 0)
    fixed = lambda i: (0, 0)
    modspec = lambda k: pl.BlockSpec((1, 1, d), lambda i: (mod_row_of_tile(i), 0, k))
    vec = pl.BlockSpec((1, d), fixed)
    mat = pl.BlockSpec((d, d), fixed)
    return pl.pallas_call(
        _mixer_out_kernel,
        grid=(n // tm,),
        in_specs=[pl.BlockSpec((tm, d), row), pl.BlockSpec((tm, d), row),
                  pl.BlockSpec((tm, d), lambda i: (i, COL_GATE_A)),
                  pl.BlockSpec((tm, d), lambda i: (i, COL_GATE_B)),
                  pl.BlockSpec((tm, d), row),
                  pl.BlockSpec((tm, d), lambda i: (i % pos_tiles, 0)),
                  modspec(2), modspec(3), modspec(4),
                  vec, vec, vec, vec, mat, mat, mat,
                  pl.BlockSpec((d, LANES), fixed), pl.BlockSpec((1, LANES), fixed)],
        out_specs=[pl.BlockSpec((tm, d), row), pl.BlockSpec((tm, d), row),
                   pl.BlockSpec((SUBLANES, tm), lambda i: (0, i)), pl.BlockSpec((SUBLANES, tm), lambda i: (0, i)),
                   pl.BlockSpec((1, SUBLANES, LANES), lambda i: (i, 0, 0))],
        out_shape=[jax.ShapeDtypeStruct((n, d), F32), jax.ShapeDtypeStruct((n, d), BF16),
                   jax.ShapeDtypeStruct((SUBLANES, n), jnp.int32), jax.ShapeDtypeStruct((SUBLANES, n), F32),
                   jax.ShapeDtypeStruct((n // tm, SUBLANES, LANES), F32)],
        scratch_shapes=[pltpu.VMEM((SUBLANES, LANES), F32)],
        compiler_params=_params("arbitrary"),
        name="mixer_output",
    )(og, uc, proj, proj, x, pos, mod, mod, mod,
      g_post.reshape(1, d), g_pre_f.reshape(1, d), ln_g.reshape(1, d), ln_b.reshape(1, d),
      w_a, w_b, w_o, w_r, b_r)


MOE_CHUNK = 512
GATHER_SUB = 256
COMBINE_GROUP = 8


def _expert_kernel(bexp_ref, slo_ref, shi_ref, nused_ref,
                   h2_ref, dest_ref, wt_ref, wgu_ref, bgu_ref, wd_ref, bd_ref, o_ref,
                   acc, rw_acc, wgu_bf, wd_bf):
    b = pl.program_id(0)
    expert = bexp_ref[b]

    @pl.when((b == 0) | (expert != bexp_ref[jnp.maximum(b - 1, 0)]))
    def _():
        wgu_bf[...] = wgu_ref[0].astype(BF16)
        wd_bf[...] = wd_ref[0].astype(BF16)

    @pl.when(b < nused_ref[0])
    def _():
        acc[...] = jnp.zeros_like(acc)
        rw_acc[...] = jnp.zeros_like(rw_acc)
        row = lax.broadcasted_iota(jnp.int32, (MOE_BLOCK, GATHER_SUB), 0) + b * MOE_BLOCK
        s_lo, s_hi = slo_ref[b], shi_ref[b]
        last_sub = h2_ref.shape[0] // GATHER_SUB - 1

        def select(s, live):
            t0 = pl.multiple_of(s * GATHER_SUB, GATHER_SUB)
            dest = dest_ref[:, pl.ds(t0, GATHER_SUB)]
            if live is not True:
                dest = jnp.where(live, dest, -1)
            wt = wt_ref[:, pl.ds(t0, GATHER_SUB)]
            onehot = jnp.zeros((MOE_BLOCK, GATHER_SUB), F32)
            wsel = jnp.zeros((MOE_BLOCK, GATHER_SUB), F32)
            for k in range(TOP_K):
                hit = row == dest[k:k + 1, :]
                onehot = jnp.where(hit, 1.0, onehot)
                wsel = jnp.where(hit, wt[k:k + 1, :], wsel)
            return onehot.astype(BF16), wsel, h2_ref[pl.ds(t0, GATHER_SUB), :]

        def gather(i, carry):
            s0 = s_lo + 2 * i
            p0, w0, x0 = select(s0, True)
            p1, w1, x1 = select(jnp.minimum(s0 + 1, last_sub), s0 + 1 <= s_hi)
            acc[...] += jnp.dot(jnp.concatenate([p0, p1], axis=1), jnp.concatenate([x0, x1], axis=0),
                                preferred_element_type=F32)
            rw_acc[...] += jnp.sum(w0 + w1, axis=1, keepdims=True)
            return carry

        lax.fori_loop(0, (s_hi - s_lo + 2) // 2, gather, 0)
        gu = jnp.dot(acc[...].astype(BF16), wgu_bf[...], preferred_element_type=F32) + bgu_ref[0]
        dff = gu.shape[1] // 2
        gl = jnp.minimum(gu[:, :dff], SWIGLU_LIMIT)
        lin = jnp.clip(gu[:, dff:], -SWIGLU_LIMIT, SWIGLU_LIMIT)
        act = (gl * jax.nn.sigmoid(SWIGLU_ALPHA * gl) * (lin + 1.0)).astype(BF16)
        y = jnp.dot(act, wd_bf[...], preferred_element_type=F32) + bd_ref[0]
        o_ref[...] = (y * rw_acc[...]).astype(o_ref.dtype)

    @pl.when(b >= nused_ref[0])
    def _():
        o_ref[...] = jnp.zeros_like(o_ref)


def expert_blocks(h2, dest_t, wt_t, block_expert, sub_lo, sub_hi, n_used, w_gate_up, b_gate_up, w_down, b_down):
    n, d = h2.shape
    e, _, f2 = w_gate_up.shape
    n_blocks = block_expert.shape[0]
    whole = lambda b, be, lo, hi, nu: (0, 0)
    by_expert = lambda b, be, lo, hi, nu: (be[b], 0, 0)
    once = pl.Buffered(1)
    grid_spec = pltpu.PrefetchScalarGridSpec(
        num_scalar_prefetch=4,
        grid=(n_blocks,),
        in_specs=[pl.BlockSpec((n, d), whole, pipeline_mode=once),
                  pl.BlockSpec((SUBLANES, n), whole, pipeline_mode=once),
                  pl.BlockSpec((SUBLANES, n), whole, pipeline_mode=once),
                  pl.BlockSpec((1, d, f2), by_expert, pipeline_mode=once),
                  pl.BlockSpec((1, 1, f2), by_expert),
                  pl.BlockSpec((1, f2 // 2, d), by_expert, pipeline_mode=once),
                  pl.BlockSpec((1, 1, d), by_expert)],
        out_specs=pl.BlockSpec((MOE_BLOCK, d), lambda b, be, lo, hi, nu: (b, 0)),
        scratch_shapes=[pltpu.VMEM((MOE_BLOCK, d), F32), pltpu.VMEM((MOE_BLOCK, 1), F32),
                        pltpu.VMEM((d, f2), BF16), pltpu.VMEM((f2 // 2, d), BF16)],
    )
    return pl.pallas_call(
        _expert_kernel,
        grid_spec=grid_spec,
        out_shape=jax.ShapeDtypeStruct((n_blocks * MOE_BLOCK, d), BF16),
        compiler_params=_params("arbitrary"),
        name="expert_blocks",
    )(block_expert, sub_lo, sub_hi, n_used, h2, dest_t, wt_t, w_gate_up, b_gate_up.reshape(e, 1, f2), w_down,
      b_down.reshape(e, 1, d))


def _combine_kernel(coff_ref, cblk_ref, yb_hbm, dest_ref, x2_ref, gf_ref, g_ref, o_ref, slab, sem, acc):
    c = pl.program_id(0)
    start = coff_ref[c]
    count = coff_ref[c + 1] - start
    n_groups = (count + COMBINE_GROUP - 1) // COMBINE_GROUP

    def block_of(g, s):
        return cblk_ref[start + jnp.minimum(g * COMBINE_GROUP + s, count - 1)]

    def copies(g, buf):
        return [pltpu.make_async_copy(
            yb_hbm.at[pl.ds(pl.multiple_of(block_of(g, s) * MOE_BLOCK, MOE_BLOCK), MOE_BLOCK), :],
            slab.at[buf, pl.ds(s * MOE_BLOCK, MOE_BLOCK), :], sem.at[buf]) for s in range(COMBINE_GROUP)]

    for cp in copies(0, 0):
        cp.start()
    acc[...] = jnp.zeros_like(acc)
    chunk = x2_ref.shape[0]
    lane = lax.broadcasted_iota(jnp.int32, (chunk, MOE_BLOCK), 1)
    dest_rows = jnp.concatenate([dest_ref[...].astype(F32), jnp.zeros((LANES - SUBLANES, chunk), F32)], axis=0)
    dest = jnp.transpose(dest_rows).astype(jnp.int32)

    def group(g, carry):
        buf = g % 2

        @pl.when(g + 1 < n_groups)
        def _():
            for cp in copies(g + 1, 1 - buf):
                cp.start()

        for cp in copies(g, buf):
            cp.wait()
        pieces = []
        for s in range(COMBINE_GROUP):
            live = g * COMBINE_GROUP + s < count
            col = lane + jnp.where(live, block_of(g, s) * MOE_BLOCK, -MOE_BLOCK)
            piece = jnp.zeros((chunk, MOE_BLOCK), F32)
            for k in range(TOP_K):
                piece = jnp.where(col == dest[:, k:k + 1], 1.0, piece)
            pieces.append(piece.astype(BF16))
        onehot = jnp.concatenate(pieces, axis=1)
        acc[...] += jnp.dot(onehot, slab[buf], preferred_element_type=F32)
        return carry

    lax.fori_loop(0, n_groups, group, 0)
    y = acc[...]
    yn = y * lax.rsqrt(jnp.mean(y * y, axis=-1, keepdims=True) + EPS) * g_ref[...]
    o_ref[...] = x2_ref[...] + gf_ref[0] * yn


def combine_residual(yb, dest, x2, chunk_off, chunk_blocks, mod, mod_row_of_chunk, g_post_f):
    n, d = x2.shape
    chunk_row = lambda c, off, blk: (c, 0)
    grid_spec = pltpu.PrefetchScalarGridSpec(
        num_scalar_prefetch=2,
        grid=(n // MOE_CHUNK,),
        in_specs=[pl.BlockSpec(memory_space=pl.ANY),
                  pl.BlockSpec((SUBLANES, MOE_CHUNK), lambda c, off, blk: (0, c)),
                  pl.BlockSpec((MOE_CHUNK, d), chunk_row),
                  pl.BlockSpec((1, 1, d), lambda c, off, blk: (mod_row_of_chunk(c), 0, 5)),
                  pl.BlockSpec((1, d), lambda c, off, blk: (0, 0))],
        out_specs=pl.BlockSpec((MOE_CHUNK, d), chunk_row),
        scratch_shapes=[pltpu.VMEM((2, COMBINE_GROUP * MOE_BLOCK, d), BF16),
                        pltpu.SemaphoreType.DMA((2,)),
                        pltpu.VMEM((MOE_CHUNK, d), F32)],
    )
    return pl.pallas_call(
        _combine_kernel,
        grid_spec=grid_spec,
        out_shape=jax.ShapeDtypeStruct((n, d), F32),
        compiler_params=_params("arbitrary"),
        name="combine_residual",
    )(chunk_off, chunk_blocks, yb, dest, x2, mod, g_post_f.reshape(1, d))


def _grid_pos_embedding(rows, d):
    row = jnp.repeat(jnp.arange(rows, dtype=F32), GRID_W)
    col = jnp.tile(jnp.arange(GRID_W, dtype=F32), rows)
    quarter = d // 4
    omega = POS_BASE ** (-jnp.arange(quarter, dtype=F32) / quarter)

    def emb(p):
        ang = p[:, None] * omega[None, :]
        return jnp.concatenate([jnp.sin(ang), jnp.cos(ang)], axis=-1)

    return jnp.concatenate([emb(row), emb(col)], axis=-1)


def _count_le(sorted_vals, queries):
    return jnp.sum(sorted_vals <= queries[:, None], axis=1).astype(jnp.int32)


def _moe_plan(ti, cnt_after, n_tok, tm):
    assert tm == GATHER_SUB
    sub_after = cnt_after[:, 0, :N_EXPERTS].astype(jnp.int32)
    counts = sub_after[-1]
    padded = (counts + MOE_BLOCK - 1) // MOE_BLOCK * MOE_BLOCK
    padded_end = jnp.cumsum(padded)
    padded_start = padded_end - padded
    experts = jnp.arange(N_EXPERTS, dtype=jnp.int32)[:, None, None]
    start_of = jnp.sum(jnp.where(ti[None, :TOP_K] == experts, padded_start[:, None, None], 0), axis=0)
    dest_t = jnp.concatenate([start_of + ti[TOP_K:2 * TOP_K],
                              jnp.full((SUBLANES - TOP_K, n_tok), -1, jnp.int32)], axis=0)
    n_blocks = n_tok * TOP_K // MOE_BLOCK + N_EXPERTS
    blocks = jnp.arange(n_blocks, dtype=jnp.int32)
    block_expert = jnp.minimum(_count_le(padded_end[None, :], blocks * MOE_BLOCK), N_EXPERTS - 1)
    n_used = (padded_end[-1:] // MOE_BLOCK).astype(jnp.int32)
    r_lo = (blocks - padded_start[block_expert] // MOE_BLOCK) * MOE_BLOCK
    r_hi = jnp.minimum(counts[block_expert], r_lo + MOE_BLOCK) - 1
    through = sub_after.T[block_expert]
    n_sub = sub_after.shape[0]
    sub_lo = jnp.minimum(_count_le(through, r_lo), n_sub - 1)
    sub_hi = jnp.minimum(_count_le(through, r_hi), n_sub - 1)
    nch = n_tok // MOE_CHUNK
    per_chunk = MOE_CHUNK // tm
    cb_after = sub_after[per_chunk - 1::per_chunk]
    cb_before = jnp.concatenate([jnp.zeros((1, N_EXPERTS), jnp.int32), cb_after[:-1]], axis=0)
    b_lo = ((padded_start[None, :] + cb_before) // MOE_BLOCK).reshape(-1)
    b_hi = ((padded_start[None, :] + cb_after - 1) // MOE_BLOCK).reshape(-1)
    n_it = jnp.where((cb_after > cb_before).reshape(-1), b_hi - b_lo + 1, 0)
    off_end = jnp.cumsum(n_it)
    off = off_end - n_it
    n_items = n_blocks + N_EXPERTS * (nch - 1)
    w = jnp.minimum(jnp.arange(n_items, dtype=jnp.int32), off_end[-1] - 1)
    cell = jnp.minimum(_count_le(off_end[None, :], w), n_it.shape[0] - 1)
    chunk_blocks = (b_lo[cell] + w - off[cell]).astype(jnp.int32)
    chunk_off = jnp.concatenate([jnp.zeros((1,), jnp.int32), off_end[N_EXPERTS - 1::N_EXPERTS]]).astype(jnp.int32)
    return dest_t, block_expert, sub_lo, sub_hi, n_used, chunk_off, chunk_blocks


def kernel(x, c, ctx, c_ctx, w_mod, b_mod, g_pre_mix, g_post_mix, g_pre_ffn, g_post_ffn, w_in, conv_kv,
           conv_q, a_log, dt_bias, gdn_norm_g, w_proj_a, conf_dw, conf_dw_b, conf_ln_g, conf_ln_b,
           w_proj_b, w_out, w_router, b_router, w_gate_up, b_gate_up, w_down, b_down):
    batch, seq, d = x.shape
    ctx_len = ctx.shape[1]
    n_tok = batch * seq
    gw = HEADS * DV
    beta_off = HEADS * DK + gw
    state_cols = beta_off + 4 * HEADS
    q_off = state_cols

    mod_rows = 2 * SUBLANES
    cond = jnp.zeros((mod_rows, d), F32).at[:batch].set(c).at[batch].set(c_ctx)
    mod = modulation(cond, w_mod[0], b_mod[0]).reshape(mod_rows, 1, 6 * d)

    w = w_in[0]
    w_main = jnp.concatenate([w[:, :beta_off], w[:, q_off:]], axis=1).astype(BF16)
    w_ba = jnp.zeros((d, LANES), F32).at[:, :4 * HEADS].set(w[:, beta_off:state_cols]).astype(BF16)

    pos = _grid_pos_embedding(seq // GRID_W, d)
    x_flat = x.reshape(n_tok, d)
    tm = 1024
    tiles_per_seq = seq // tm
    proj, ba = input_projection(x_flat, pos, mod, lambda i: i // tiles_per_seq, g_pre_mix[0],
                                w_main, w_ba, tm=tm, tn=2048)
    proj_ctx, ba_ctx = input_projection(ctx.reshape(batch * ctx_len, d), None, mod, lambda i: batch,
                                        g_pre_mix[0], w_main[:, :beta_off], w_ba, tm=ctx_len, tn=1024)

    og = gated_deltanet(proj, ba, proj_ctx, ba_ctx, conv_kv[0], conv_q[0], a_log[0], dt_bias[0],
                        gdn_norm_g[0], batch=batch, seq=seq, ctx=ctx_len)
    uc = conformer_conv(proj, conf_dw[0], conf_dw_b[0], batch=batch, seq=seq)

    tm2 = 256
    w_r = jnp.zeros((d, LANES), F32).at[:, :N_EXPERTS].set(w_router[0]).astype(BF16)
    b_r = jnp.zeros((1, LANES), F32).at[0, :N_EXPERTS].set(b_router[0])
    x2, h2, top_i, top_w, cnt_after = mixer_output(
        og, uc, proj, x_flat, pos, mod, lambda i: i // (seq // tm2), g_post_mix[0], g_pre_ffn[0],
        conf_ln_g[0], conf_ln_b[0], w_proj_a[0].astype(BF16), w_proj_b[0].astype(BF16),
        w_out[0].astype(BF16), w_r, b_r, tm=tm2)

    dest_t, block_expert, sub_lo, sub_hi, n_used, chunk_off, chunk_blocks = _moe_plan(
        top_i, cnt_after, n_tok, tm2)
    yb = expert_blocks(h2, dest_t, top_w, block_expert, sub_lo, sub_hi, n_used, w_gate_up[0], b_gate_up[0],
                       w_down[0], b_down[0])
    out = combine_residual(yb, dest_t, x2, chunk_off, chunk_blocks, mod, lambda c: c // (seq // MOE_CHUNK),
                           g_post_ffn[0])
    return out.reshape(batch, seq, d)
```

```python
import functools

import jax
import jax.numpy as jnp
from jax import lax
from jax.experimental import pallas as pl
from jax.experimental.pallas import tpu as pltpu

F32 = jnp.float32
BF16 = jnp.bfloat16

D_MODEL = 1024
GRID_W = 64
HEADS = 8
DK = 128
DV = 128
SHORT_CONV = 5
CHUNK = 64
CONF_KERNEL = 31
N_EXPERTS = 32
TOP_K = 4
SWIGLU_LIMIT = 7.0
SWIGLU_ALPHA = 1.702
MOE_BLOCK = 128
EPS = 1e-6
POS_BASE = 10000.0

LANES = 128
SUBLANES = 8
VMEM_LIMIT = 56 * 1024 * 1024

COL_K, COL_V, COL_Q, COL_Z, COL_GLU_A, COL_GLU_G, COL_GATE_A, COL_GATE_B = range(8)


def _params(*sem):
    return pltpu.CompilerParams(dimension_semantics=sem, vmem_limit_bytes=VMEM_LIMIT)


def _mod_kernel(c_ref, w_ref, b_ref, o_ref):
    c = c_ref[...]
    s = c * jax.nn.sigmoid(c)
    o_ref[...] = jnp.dot(s, w_ref[...], preferred_element_type=F32,
                         precision=lax.Precision.HIGHEST) + b_ref[...]


def modulation(cond, w_mod, b_mod):
    r, d = cond.shape
    n = w_mod.shape[1]
    tn = 1024
    return pl.pallas_call(
        _mod_kernel,
        grid=(n // tn,),
        in_specs=[pl.BlockSpec((r, d), lambda j: (0, 0)),
                  pl.BlockSpec((d, tn), lambda j: (0, j)),
                  pl.BlockSpec((1, tn), lambda j: (0, j))],
        out_specs=pl.BlockSpec((r, tn), lambda j: (0, j)),
        out_shape=jax.ShapeDtypeStruct((r, n), F32),
        compiler_params=_params("arbitrary"),
        name="modulation",
    )(cond, w_mod, b_mod.reshape(1, n))


def _inproj_kernel(*refs, has_pos):
    if has_pos:
        x_ref, pos_ref, sh_ref, sc_ref, g_ref, w_ref, wba_ref, o_ref, ba_ref, h_scr = refs
    else:
        x_ref, sh_ref, sc_ref, g_ref, w_ref, wba_ref, o_ref, ba_ref, h_scr = refs

    @pl.when(pl.program_id(1) == 0)
    def _():
        x = x_ref[...]
        if has_pos:
            x = x + pos_ref[...]
        y = x * lax.rsqrt(jnp.mean(x * x, axis=-1, keepdims=True) + EPS) * g_ref[...]
        h = (y * (1.0 + sc_ref[0]) + sh_ref[0]).astype(BF16)
        h_scr[...] = h
        ba_ref[...] = jnp.dot(h, wba_ref[...], preferred_element_type=F32)

    o_ref[...] = jnp.dot(h_scr[...], w_ref[...], preferred_element_type=F32).astype(o_ref.dtype)


def input_projection(x, pos, mod, mod_row_of_tile, g_pre, w_main, w_ba, *, tm, tn):
    n, d = x.shape
    w = w_main.shape[1]
    has_pos = pos is not None
    in_specs = [pl.BlockSpec((tm, d), lambda i, j: (i, 0))]
    args = [x]
    if has_pos:
        pos_tiles = pos.shape[0] // tm
        in_specs.append(pl.BlockSpec((tm, d), lambda i, j: (i % pos_tiles, 0)))
        args.append(pos)
    in_specs += [
        pl.BlockSpec((1, 1, d), lambda i, j: (mod_row_of_tile(i), 0, 0)),
        pl.BlockSpec((1, 1, d), lambda i, j: (mod_row_of_tile(i), 0, 1)),
        pl.BlockSpec((1, d), lambda i, j: (0, 0)),
        pl.BlockSpec((d, tn), lambda i, j: (0, j)),
        pl.BlockSpec((d, LANES), lambda i, j: (0, 0)),
    ]
    args += [mod, mod, g_pre.reshape(1, d), w_main, w_ba]
    return pl.pallas_call(
        functools.partial(_inproj_kernel, has_pos=has_pos),
        grid=(n // tm, w // tn),
        in_specs=in_specs,
        out_specs=[pl.BlockSpec((tm, tn), lambda i, j: (i, j)),
                   pl.BlockSpec((tm, LANES), lambda i, j: (i, 0))],
        out_shape=[jax.ShapeDtypeStruct((n, w), BF16), jax.ShapeDtypeStruct((n, LANES), F32)],
        scratch_shapes=[pltpu.VMEM((tm, d), BF16)],
        compiler_params=_params("arbitrary", "arbitrary"),
        name="input_projection_pos" if has_pos else "input_projection_ctx",
    )(*args)


def _silu(x):
    return x * jax.nn.sigmoid(x)


def _softplus(x):
    return jnp.maximum(x, 0.0) + jnp.log(1.0 + jnp.exp(-jnp.abs(x)))


def _short_conv(src, taps, pad_scr, rows):
    half = SHORT_CONV // 2
    pad_scr[0:SUBLANES, :] = jnp.zeros((SUBLANES, LANES), F32)
    pad_scr[SUBLANES:SUBLANES + rows, :] = src.astype(F32)
    pad_scr[SUBLANES + rows:2 * SUBLANES + rows, :] = jnp.zeros((SUBLANES, LANES), F32)
    acc = None
    for j in range(SHORT_CONV):
        off = SUBLANES + j - half
        term = pad_scr[off:off + rows, :] * taps[j:j + 1, :]
        acc = term if acc is None else acc + term
    return _silu(acc)


def _l2n(x):
    return x * lax.rsqrt(jnp.sum(x * x, axis=-1, keepdims=True) + EPS)


def _gdn_kernel(k_ref, v_ref, q_ref, z_ref, ba_ref, kc_ref, vc_ref, bac_ref,
                wk_ref, wv_ref, wq_ref, ab_ref, gn_ref, o_ref,
                kf, vf, qf, gates, pad_scr, lhs, val, att, ktt, dec, osc, *, seq, ctx, unroll, group):
    total = ctx + seq
    nc = total // CHUNK
    ncc = ctx // CHUNK
    two = 2 * CHUNK
    lg_chunk = CHUNK.bit_length() - 1

    ri = lax.broadcasted_iota(jnp.int32, (two, two), 0)
    ci = lax.broadcasted_iota(jnp.int32, (two, two), 1)
    same_dir = (ri >> lg_chunk) == (ci >> lg_chunk)
    incl = same_dir & (((ri < CHUNK) & (ri >= ci)) | ((ri >= CHUNK) & (ri <= ci)))
    strict = incl & (ri != ci)
    eye = jnp.where(ri == ci, 1.0, 0.0)
    tri = jnp.where(incl, 1.0, 0.0).astype(BF16)
    level_masks = []
    for lg in range(lg_chunk):
        same_parent = (ri >> (lg + 1)) == (ci >> (lg + 1))
        level_masks.append(jnp.where(same_parent & ((ri >> lg) != (ci >> lg)), 1.0, 0.0))
    top_rows = lax.broadcasted_iota(jnp.int32, (two, LANES), 0) < CHUNK

    def bwd_chunk(t):
        return jnp.where(t < ncc, ncc - 1 - t, nc + ncc - 1 - t)

    def stacked(ref, rf, rb):
        return jnp.concatenate([ref[pl.ds(rf, CHUNK), :], ref[pl.ds(rb, CHUNK), :]], axis=0)

    def chunk_load(t):
        rf = pl.multiple_of(t * CHUNK, CHUNK)
        rb = pl.multiple_of(bwd_chunk(t) * CHUNK, CHUNK)
        beta = jnp.concatenate([gates[0, pl.ds(rf, CHUNK), :], gates[1, pl.ds(rb, CHUNK), :]], axis=0)
        g = jnp.concatenate([gates[2, pl.ds(rf, CHUNK), :], gates[3, pl.ds(rb, CHUNK), :]], axis=0)
        return stacked(kf, rf, rb), stacked(vf, rf, rb), stacked(qf, rf, rb), beta, g

    def chunk_triangle(k, q, beta, g):
        kb = k.astype(BF16)
        kq = jnp.concatenate([kb, q.astype(BF16)], axis=0)
        kk_qk = lax.dot_general(kq, kb, (((1,), (1,)), ((), ())), preferred_element_type=F32)
        g_hi = g.astype(BF16)
        r1 = g - g_hi.astype(F32)
        g_mid = r1.astype(BF16)
        g_lo = (r1 - g_mid.astype(F32)).astype(BF16)
        gc3 = jnp.dot(tri, jnp.concatenate([g_hi, g_mid, g_lo], axis=1), preferred_element_type=F32)
        gc = gc3[:, :LANES] + gc3[:, LANES:2 * LANES] + gc3[:, 2 * LANES:]
        gc_row = jnp.transpose(gc)
        decay = jnp.where(incl, jnp.exp(gc - gc_row), 0.0)
        a = jnp.where(strict, beta * kk_qk[:two] * decay, 0.0)
        return a, (kk_qk[two:] * decay).astype(BF16), gc

    def chunk_finish(hh, t, k, v, q, beta, gc, attn, tmat):
        eg = jnp.exp(gc)
        rhs = jnp.concatenate([v * beta, k * beta * eg], axis=1).astype(BF16)
        sol = jnp.dot(tmat.astype(BF16), rhs, preferred_element_type=F32)
        g_end = jnp.where(top_rows, jnp.broadcast_to(gc[CHUNK - 1:CHUNK, :], (two, LANES)),
                          jnp.broadcast_to(gc[CHUNK:CHUNK + 1, :], (two, LANES)))
        k_tail = k * jnp.exp(g_end - gc)
        qg = (q * eg).astype(BF16)
        kcum = sol[:, DV:].astype(BF16)
        r0 = pl.multiple_of(t * two, two)
        lhs[hh, 0, pl.ds(r0, two), :] = jnp.concatenate([kcum[:CHUNK], qg[:CHUNK]], axis=0)
        lhs[hh, 1, pl.ds(r0, two), :] = jnp.concatenate([kcum[CHUNK:], qg[CHUNK:]], axis=0)
        val[hh, pl.ds(r0, two), :] = sol[:, :DV]
        att[hh, pl.ds(r0, two), :] = attn
        ktt[hh, pl.ds(r0, two), :] = jnp.transpose(k_tail).astype(BF16)
        d0 = pl.multiple_of(t * SUBLANES, SUBLANES)
        e_end = jnp.exp(g_end)
        dec[hh, 0, pl.ds(d0, SUBLANES), :] = e_end[:SUBLANES]
        dec[hh, 1, pl.ds(d0, SUBLANES), :] = e_end[CHUNK:CHUNK + SUBLANES]

    for hh in range(group):
        head = pl.program_id(1) * group + hh
        cols = slice(hh * LANES, (hh + 1) * LANES)

        kf[0:ctx, :] = _l2n(_short_conv(kc_ref[:, cols], wk_ref[:, cols], pad_scr, ctx))
        vf[0:ctx, :] = _short_conv(vc_ref[:, cols], wv_ref[:, cols], pad_scr, ctx)
        qf[0:ctx, :] = jnp.zeros((ctx, LANES), F32)
        kf[ctx:total, :] = _l2n(_short_conv(k_ref[:, cols], wk_ref[:, cols], pad_scr, seq))
        vf[ctx:total, :] = _short_conv(v_ref[:, cols], wv_ref[:, cols], pad_scr, seq)
        qf[ctx:total, :] = _l2n(_short_conv(q_ref[:, cols], wq_ref[:, cols], pad_scr, seq)) * (DK ** -0.5)

        def gate_cols(src_ref, lo, rows):
            x = src_ref[...]
            lane = lax.broadcasted_iota(jnp.int32, (rows, LANES), 1)
            beta = jax.nn.sigmoid(x)
            g = -jnp.exp(ab_ref[0:1, :]) * _softplus(x + ab_ref[1:2, :])
            for slot, (arr, base) in enumerate(((beta, 0), (beta, HEADS), (g, 2 * HEADS), (g, 3 * HEADS))):
                col = jnp.sum(jnp.where(lane == base + head, arr, 0.0), axis=-1, keepdims=True)
                gates[slot, lo:lo + rows, :] = jnp.broadcast_to(col, (rows, LANES))

        gate_cols(bac_ref, 0, ctx)
        gate_cols(ba_ref, ctx, seq)

        def chunk_body(i, carry, hh=hh):
            steps = [i * unroll + u for u in range(unroll)]
            loaded = [chunk_load(t) for t in steps]
            tris = [chunk_triangle(k, q, beta, g) for (k, v, q, beta, g) in loaded]
            tmats = [eye - a * level_masks[0] for (a, _, _) in tris]
            for m in level_masks[1:]:
                nxt = []
                for (a, _, _), tmat in zip(tris, tmats):
                    cs = (a * m).astype(BF16)
                    tb = tmat.astype(BF16)
                    tc = jnp.dot(tb, cs, preferred_element_type=F32)
                    nxt.append(tmat - jnp.dot(tc.astype(BF16), tb, preferred_element_type=F32))
                tmats = nxt
            for t, (k, v, q, beta, g), (a, attn, gc), tmat in zip(steps, loaded, tris, tmats):
                chunk_finish(hh, t, k, v, q, beta, gc, attn, tmat)
            return carry

        lax.fori_loop(0, nc // unroll, chunk_body, 0)

    def scan_body(t, carry):
        rf = pl.multiple_of(t * CHUNK, CHUNK)
        rb = pl.multiple_of(bwd_chunk(t) * CHUNK, CHUNK)
        r0 = pl.multiple_of(t * two, two)
        d0 = pl.multiple_of(t * SUBLANES, SUBLANES)
        rs = [(jnp.dot(lhs[hh, 0, pl.ds(r0, two), :], carry[2 * hh].astype(BF16), preferred_element_type=F32),
               jnp.dot(lhs[hh, 1, pl.ds(r0, two), :], carry[2 * hh + 1].astype(BF16), preferred_element_type=F32))
              for hh in range(group)]
        v_news = [val[hh, pl.ds(r0, two), :] - jnp.concatenate([r_f[:CHUNK], r_b[:CHUNK]], axis=0)
                  for hh, (r_f, r_b) in enumerate(rs)]
        outs, new_states = [], []
        for hh, ((r_f, r_b), v_new) in enumerate(zip(rs, v_news)):
            zeros = jnp.zeros_like(v_new)
            v_bd = jnp.concatenate([jnp.where(top_rows, v_new, zeros), jnp.where(top_rows, zeros, v_new)],
                                   axis=1).astype(BF16)
            upd = jnp.dot(ktt[hh, pl.ds(r0, two), :], v_bd, preferred_element_type=F32)
            new_states.append(carry[2 * hh] * dec[hh, 0, pl.ds(d0, SUBLANES), :][0:1, :] + upd[:, :DV])
            new_states.append(carry[2 * hh + 1] * dec[hh, 1, pl.ds(d0, SUBLANES), :][0:1, :] + upd[:, DV:])
            outs.append(jnp.concatenate([r_f[CHUNK:], r_b[CHUNK:]], axis=0)
                        + jnp.dot(att[hh, pl.ds(r0, two), :], v_new.astype(BF16), preferred_element_type=F32))
        for hh, o in enumerate(outs):
            osc[hh, 0, pl.ds(rf, CHUNK), :] = o[:CHUNK]
            osc[hh, 1, pl.ds(rb, CHUNK), :] = o[CHUNK:]
        return tuple(new_states)

    zero = jnp.zeros((DK, DV), F32)
    lax.fori_loop(0, nc, scan_body, (zero,) * (2 * group))

    for hh in range(group):
        cols = slice(hh * LANES, (hh + 1) * LANES)
        o = osc[hh, 0, ctx:total, :] + osc[hh, 1, ctx:total, :]
        o = o * lax.rsqrt(jnp.mean(o * o, axis=-1, keepdims=True) + EPS) * gn_ref[...]
        o_ref[:, cols] = (o * _silu(z_ref[:, cols].astype(F32))).astype(o_ref.dtype)


def gated_deltanet(proj, ba, proj_ctx, ba_ctx, conv_kv, conv_q, a_log, dt_bias, gn, *, batch, seq, ctx,
                   unroll=12, group=2):
    total = seq + ctx
    nc = total // CHUNK
    gl = group * LANES
    ng = HEADS // group
    taps = jnp.zeros((SUBLANES, conv_kv.shape[1]), F32).at[:SHORT_CONV].set(conv_kv)
    taps_q = jnp.zeros((SUBLANES, conv_q.shape[1]), F32).at[:SHORT_CONV].set(conv_q)
    ab = jnp.zeros((SUBLANES, LANES), F32)
    ab = ab.at[0, 2 * HEADS:4 * HEADS].set(a_log.reshape(-1)).at[1, 2 * HEADS:4 * HEADS].set(dt_bias.reshape(-1))
    col = lambda base: (lambda b, h: (b, base * ng + h))
    return pl.pallas_call(
        functools.partial(_gdn_kernel, seq=seq, ctx=ctx, unroll=unroll, group=group),
        grid=(batch, ng),
        in_specs=[
            pl.BlockSpec((seq, gl), col(COL_K)),
            pl.BlockSpec((seq, gl), col(COL_V)),
            pl.BlockSpec((seq, gl), col(COL_Q)),
            pl.BlockSpec((seq, gl), col(COL_Z)),
            pl.BlockSpec((seq, LANES), lambda b, h: (b, 0)),
            pl.BlockSpec((ctx, gl), col(COL_K)),
            pl.BlockSpec((ctx, gl), col(COL_V)),
            pl.BlockSpec((ctx, LANES), lambda b, h: (b, 0)),
            pl.BlockSpec((SUBLANES, gl), lambda b, h: (0, h)),
            pl.BlockSpec((SUBLANES, gl), lambda b, h: (0, ng + h)),
            pl.BlockSpec((SUBLANES, gl), lambda b, h: (0, h)),
            pl.BlockSpec((SUBLANES, LANES), lambda b, h: (0, 0)),
            pl.BlockSpec((1, LANES), lambda b, h: (0, 0)),
        ],
        out_specs=pl.BlockSpec((seq, gl), lambda b, h: (b, h)),
        out_shape=jax.ShapeDtypeStruct((batch * seq, HEADS * DV), BF16),
        scratch_shapes=[
            pltpu.VMEM((total, LANES), F32),
            pltpu.VMEM((total, LANES), F32),
            pltpu.VMEM((total, LANES), F32),
            pltpu.VMEM((4, total, LANES), F32),
            pltpu.VMEM((seq + 2 * SUBLANES, LANES), F32),
            pltpu.VMEM((group, 2, nc * 2 * CHUNK, LANES), BF16),
            pltpu.VMEM((group, nc * 2 * CHUNK, LANES), F32),
            pltpu.VMEM((group, nc * 2 * CHUNK, LANES), BF16),
            pltpu.VMEM((group, nc * 2 * CHUNK, LANES), BF16),
            pltpu.VMEM((group, 2, nc * SUBLANES, LANES), F32),
            pltpu.VMEM((group, 2, total, LANES), F32),
        ],
        compiler_params=_params("arbitrary", "arbitrary"),
        name="gated_deltanet",
    )(proj, proj, proj, proj, ba, proj_ctx, proj_ctx, ba_ctx, taps, taps, taps_q, ab, gn.reshape(1, DV))


CONF_ROWS = 128
CONF_HALO = 16


def _conf_kernel(a_ref, g_ref, w_ref, b_ref, o_ref, pad_scr, *, seq):
    cb = a_ref.shape[1]
    u = a_ref[...].astype(F32) * jax.nn.sigmoid(g_ref[...].astype(F32))
    pad_scr[0:CONF_HALO, :] = jnp.zeros((CONF_HALO, cb), F32)
    pad_scr[CONF_HALO:CONF_HALO + seq, :] = u
    pad_scr[CONF_HALO + seq:2 * CONF_HALO + seq, :] = jnp.zeros((CONF_HALO, cb), F32)
    win = CONF_ROWS + 2 * CONF_HALO
    first = CONF_HALO - CONF_KERNEL // 2

    def body(i, carry):
        r0 = pl.multiple_of(i * CONF_ROWS, CONF_ROWS)
        w = pad_scr[pl.ds(r0, win), :]
        acc = jnp.zeros((CONF_ROWS, cb), F32) + b_ref[...]
        for sub in range(SUBLANES):
            shifted = w if sub == 0 else pltpu.roll(w, win - sub, axis=0)
            for j in range(CONF_KERNEL):
                off = first + j
                if off % SUBLANES == sub:
                    base = off - sub
                    acc = acc + shifted[base:base + CONF_ROWS, :] * w_ref[j:j + 1, :]
        o_ref[pl.ds(r0, CONF_ROWS), :] = acc.astype(o_ref.dtype)
        return carry

    lax.fori_loop(0, seq // CONF_ROWS, body, 0)


def conformer_conv(proj, conf_dw, conf_dw_b, *, batch, seq, cb=256):
    c = conf_dw.shape[1]
    nb = c // cb
    taps = jnp.zeros((32, c), F32).at[:CONF_KERNEL].set(conf_dw)
    return pl.pallas_call(
        functools.partial(_conf_kernel, seq=seq),
        grid=(batch, nb),
        in_specs=[pl.BlockSpec((seq, cb), lambda b, j: (b, COL_GLU_A * nb + j)),
                  pl.BlockSpec((seq, cb), lambda b, j: (b, COL_GLU_G * nb + j)),
                  pl.BlockSpec((32, cb), lambda b, j: (0, j)),
                  pl.BlockSpec((1, cb), lambda b, j: (0, j))],
        out_specs=pl.BlockSpec((seq, cb), lambda b, j: (b, j)),
        out_shape=jax.ShapeDtypeStruct((batch * seq, c), BF16),
        scratch_shapes=[pltpu.VMEM((seq + 2 * CONF_HALO, cb), F32)],
        compiler_params=_params("arbitrary", "arbitrary"),
        name="conformer_conv",
    )(proj, proj, taps, conf_dw_b.reshape(1, c))


def _mixer_out_kernel(og_ref, uc_ref, ga_ref, gb_ref, x_ref, pos_ref, gm_ref, shf_ref, scf_ref,
                      gpost_ref, gpre_ref, lng_ref, lnb_ref, wa_ref, wb_ref, wo_ref, wr_ref, br_ref,
                      x2_ref, h2_ref, ti_ref, tw_ref, cnt_ref, cnt_scr, *, sub):
    parts = [slice(r, r + sub) for r in range(0, og_ref.shape[0], sub)]

    def rms(v):
        return v * lax.rsqrt(jnp.mean(v * v, axis=-1, keepdims=True) + EPS)

    def layer_norm_silu(uc):
        mu = jnp.mean(uc, axis=-1, keepdims=True)
        var = jnp.mean(jnp.square(uc - mu), axis=-1, keepdims=True)
        return _silu((uc - mu) * lax.rsqrt(var + EPS) * lng_ref[...] + lnb_ref[...]).astype(BF16)

    us = [layer_norm_silu(uc_ref[p, :].astype(F32)) for p in parts]
    yas = [jnp.dot(og_ref[p, :], wa_ref[...], preferred_element_type=F32) for p in parts]
    ybs = [jnp.dot(u, wb_ref[...], preferred_element_type=F32) for u in us]
    merged = [(jax.nn.sigmoid(ga_ref[p, :].astype(F32)) * ya
               + jax.nn.sigmoid(gb_ref[p, :].astype(F32)) * yb).astype(BF16) for p, ya, yb in zip(parts, yas, ybs)]
    ys = [jnp.dot(m, wo_ref[...], preferred_element_type=F32) for m in merged]
    x2s = [x_ref[p, :] + pos_ref[p, :] + gm_ref[0] * (rms(y) * gpost_ref[...]) for p, y in zip(parts, ys)]
    h2s = [(rms(x2) * gpre_ref[...] * (1.0 + scf_ref[0]) + shf_ref[0]).astype(BF16) for x2 in x2s]
    for p, x2, h2 in zip(parts, x2s, h2s):
        x2_ref[p, :] = x2
        h2_ref[p, :] = h2
    all_logits = [jnp.dot(h2, wr_ref[...], preferred_element_type=F32) + br_ref[...] for h2 in h2s]
    lane = lax.broadcasted_iota(jnp.int32, (sub, LANES), 1).astype(F32)
    ri = lax.broadcasted_iota(jnp.int32, (sub, sub), 0)
    ci = lax.broadcasted_iota(jnp.int32, (sub, sub), 1)
    before = jnp.where(ri > ci, 1.0, 0.0).astype(BF16)

    @pl.when(pl.program_id(0) == 0)
    def _():
        cnt_scr[...] = jnp.zeros_like(cnt_scr)

    routed = []
    for logits in all_logits:
        live = jnp.where(lane < N_EXPERTS, logits, -jnp.inf)
        top_v, top_i = [], []
        for _ in range(TOP_K):
            m = jnp.max(live, axis=-1, keepdims=True)
            idx = jnp.min(jnp.where(live == m, lane, float(LANES)), axis=-1, keepdims=True)
            top_v.append(m)
            top_i.append(idx)
            live = jnp.where(lane == idx, -jnp.inf, live)
        ex = [jnp.exp(v - top_v[0]) for v in top_v]
        denom = ex[0] + ex[1] + ex[2] + ex[3]
        picked = jnp.zeros((sub, LANES), F32)
        for k in range(TOP_K):
            picked = jnp.where(lane == top_i[k], 1.0, picked)
        routed.append((top_i, [e / denom for e in ex], picked,
                       jnp.dot(before, picked.astype(BF16), preferred_element_type=F32)))

    count = cnt_scr[...]
    for j, (p, (top_i, top_w, picked, within)) in enumerate(zip(parts, routed)):
        prefix = within + count[0:1, :]
        ti = jnp.zeros((sub, LANES), F32)
        tw = jnp.zeros((sub, LANES), F32)
        for k in range(TOP_K):
            rank = jnp.sum(jnp.where(lane == top_i[k], prefix, 0.0), axis=-1, keepdims=True)
            ti = jnp.where(lane == k, top_i[k], ti)
            ti = jnp.where(lane == TOP_K + k, rank, ti)
            tw = jnp.where(lane == k, top_w[k], tw)
        ti_ref[:, p] = jnp.transpose(ti)[:SUBLANES, :].astype(jnp.int32)
        tw_ref[:, p] = jnp.transpose(tw)[:SUBLANES, :]
        count = count + jnp.sum(picked, axis=0, keepdims=True)
        cnt_ref[j] = count
    cnt_scr[...] = count


def mixer_output(og, uc, proj, x, pos, mod, mod_row_of_tile, g_post, g_pre_f, ln_g, ln_b,
                 w_a, w_b, w_o, w_r, b_r, *, tm, sub):
    n, d = x.shape
    pos_tiles = pos.shape[0] // tm
    row = lambda i: (i, 0)
    fixed = lambda i: (0, 0)
    modspec = lambda k: pl.BlockSpec((1, 1, d), lambda i: (mod_row_of_tile(i), 0, k))
    vec = pl.BlockSpec((1, d), fixed)
    mat = pl.BlockSpec((d, d), fixed)
    return pl.pallas_call(
        functools.partial(_mixer_out_kernel, sub=sub),
        grid=(n // tm,),
        in_specs=[pl.BlockSpec((tm, d), row), pl.BlockSpec((tm, d), row),
                  pl.BlockSpec((tm, d), lambda i: (i, COL_GATE_A)),
                  pl.BlockSpec((tm, d), lambda i: (i, COL_GATE_B)),
                  pl.BlockSpec((tm, d), row),
                  pl.BlockSpec((tm, d), lambda i: (i % pos_tiles, 0)),
                  modspec(2), modspec(3), modspec(4),
                  vec, vec, vec, vec, mat, mat, mat,
                  pl.BlockSpec((d, LANES), fixed), pl.BlockSpec((1, LANES), fixed)],
        out_specs=[pl.BlockSpec((tm, d), row), pl.BlockSpec((tm, d), row),
                   pl.BlockSpec((SUBLANES, tm), lambda i: (0, i)), pl.BlockSpec((SUBLANES, tm), lambda i: (0, i)),
                   pl.BlockSpec((tm // sub, SUBLANES, LANES), lambda i: (i, 0, 0))],
        out_shape=[jax.ShapeDtypeStruct((n, d), F32), jax.ShapeDtypeStruct((n, d), BF16),
                   jax.ShapeDtypeStruct((SUBLANES, n), jnp.int32), jax.ShapeDtypeStruct((SUBLANES, n), F32),
                   jax.ShapeDtypeStruct((n // sub, SUBLANES, LANES), F32)],
        scratch_shapes=[pltpu.VMEM((SUBLANES, LANES), F32)],
        compiler_params=_params("arbitrary"),
        name="mixer_output",
    )(og, uc, proj, proj, x, pos, mod, mod, mod,
      g_post.reshape(1, d), g_pre_f.reshape(1, d), ln_g.reshape(1, d), ln_b.reshape(1, d),
      w_a, w_b, w_o, w_r, b_r)


MOE_CHUNK = 512
GATHER_SUB = 256
COMBINE_GROUP = 8


def _expert_kernel(bexp_ref, slo_ref, shi_ref, nused_ref,
                   h2_ref, dest_ref, wt_ref, wgu_ref, bgu_ref, wd_ref, bd_ref, o_ref,
                   acc, rw_acc, wgu_bf, wd_bf):
    b = pl.program_id(0)
    expert = bexp_ref[b]

    @pl.when((b == 0) | (expert != bexp_ref[jnp.maximum(b - 1, 0)]))
    def _():
        wgu_bf[...] = wgu_ref[0].astype(BF16)
        wd_bf[...] = wd_ref[0].astype(BF16)

    @pl.when(b < nused_ref[0])
    def _():
        acc[...] = jnp.zeros_like(acc)
        rw_acc[...] = jnp.zeros_like(rw_acc)
        row = lax.broadcasted_iota(jnp.int32, (MOE_BLOCK, GATHER_SUB), 0)
        s_lo, s_hi = slo_ref[b], shi_ref[b]
        last_sub = h2_ref.shape[0] // GATHER_SUB - 1

        def select(s, live):
            t0 = pl.multiple_of(s * GATHER_SUB, GATHER_SUB)
            local = dest_ref[:, pl.ds(t0, GATHER_SUB)] - b * MOE_BLOCK
            if live is not True:
                local = jnp.where(live, local, -1)
            inside = (local >= 0) & (local < MOE_BLOCK)
            hit_row = jnp.sum(jnp.where(inside, local + 1, 0), axis=0, keepdims=True) - 1
            hit_w = jnp.sum(jnp.where(inside, wt_ref[:, pl.ds(t0, GATHER_SUB)], 0.0), axis=0, keepdims=True)
            hit = row == hit_row
            return (jnp.where(hit, 1.0, 0.0).astype(BF16), jnp.where(hit, hit_w, 0.0),
                    h2_ref[pl.ds(t0, GATHER_SUB), :])

        def gather(i, carry):
            s0 = s_lo + 2 * i
            p0, w0, x0 = select(s0, True)
            p1, w1, x1 = select(jnp.minimum(s0 + 1, last_sub), s0 + 1 <= s_hi)
            acc[...] += jnp.dot(jnp.concatenate([p0, p1], axis=1), jnp.concatenate([x0, x1], axis=0),
                                preferred_element_type=F32)
            rw_acc[...] += jnp.sum(w0 + w1, axis=1, keepdims=True)
            return carry

        lax.fori_loop(0, (s_hi - s_lo + 2) // 2, gather, 0)
        gu = jnp.dot(acc[...].astype(BF16), wgu_bf[...], preferred_element_type=F32) + bgu_ref[0]
        dff = gu.shape[1] // 2
        gl = jnp.minimum(gu[:, :dff], SWIGLU_LIMIT)
        lin = jnp.clip(gu[:, dff:], -SWIGLU_LIMIT, SWIGLU_LIMIT)
        act = (gl * jax.nn.sigmoid(SWIGLU_ALPHA * gl) * (lin + 1.0)).astype(BF16)
        y = jnp.dot(act, wd_bf[...], preferred_element_type=F32) + bd_ref[0]
        o_ref[...] = (y * rw_acc[...]).astype(o_ref.dtype)

    @pl.when(b >= nused_ref[0])
    def _():
        o_ref[...] = jnp.zeros_like(o_ref)


def expert_blocks(h2, dest_t, wt_t, block_expert, sub_lo, sub_hi, n_used, w_gate_up, b_gate_up, w_down, b_down):
    n, d = h2.shape
    e, _, f2 = w_gate_up.shape
    n_blocks = block_expert.shape[0]
    whole = lambda b, be, lo, hi, nu: (0, 0)
    by_expert = lambda b, be, lo, hi, nu: (be[b], 0, 0)
    once = pl.Buffered(1)
    grid_spec = pltpu.PrefetchScalarGridSpec(
        num_scalar_prefetch=4,
        grid=(n_blocks,),
        in_specs=[pl.BlockSpec((n, d), whole, pipeline_mode=once),
                  pl.BlockSpec((SUBLANES, n), whole, pipeline_mode=once),
                  pl.BlockSpec((SUBLANES, n), whole, pipeline_mode=once),
                  pl.BlockSpec((1, d, f2), by_expert, pipeline_mode=once),
                  pl.BlockSpec((1, 1, f2), by_expert),
                  pl.BlockSpec((1, f2 // 2, d), by_expert, pipeline_mode=once),
                  pl.BlockSpec((1, 1, d), by_expert)],
        out_specs=pl.BlockSpec((MOE_BLOCK, d), lambda b, be, lo, hi, nu: (b, 0)),
        scratch_shapes=[pltpu.VMEM((MOE_BLOCK, d), F32), pltpu.VMEM((MOE_BLOCK, 1), F32),
                        pltpu.VMEM((d, f2), BF16), pltpu.VMEM((f2 // 2, d), BF16)],
    )
    return pl.pallas_call(
        _expert_kernel,
        grid_spec=grid_spec,
        out_shape=jax.ShapeDtypeStruct((n_blocks * MOE_BLOCK, d), BF16),
        compiler_params=_params("arbitrary"),
        name="expert_blocks",
    )(block_expert, sub_lo, sub_hi, n_used, h2, dest_t, wt_t, w_gate_up, b_gate_up.reshape(e, 1, f2), w_down,
      b_down.reshape(e, 1, d))


def _combine_kernel(coff_ref, cblk_ref, yb_hbm, dest_ref, x2_ref, gf_ref, g_ref, o_ref, slab, sem, acc):
    c = pl.program_id(0)
    start = coff_ref[c]
    count = coff_ref[c + 1] - start
    n_groups = (count + COMBINE_GROUP - 1) // COMBINE_GROUP

    def block_of(g, s):
        return cblk_ref[start + jnp.minimum(g * COMBINE_GROUP + s, count - 1)]

    def copies(g, buf):
        return [pltpu.make_async_copy(
            yb_hbm.at[pl.ds(pl.multiple_of(block_of(g, s) * MOE_BLOCK, MOE_BLOCK), MOE_BLOCK), :],
            slab.at[buf, pl.ds(s * MOE_BLOCK, MOE_BLOCK), :], sem.at[buf]) for s in range(COMBINE_GROUP)]

    for cp in copies(0, 0):
        cp.start()
    acc[...] = jnp.zeros_like(acc)
    chunk = x2_ref.shape[0]
    lane = lax.broadcasted_iota(jnp.int32, (chunk, MOE_BLOCK), 1)
    dest_rows = dest_ref[...]
    n_rows = yb_hbm.shape[0]

    def group(g, carry):
        buf = g % 2

        @pl.when(g + 1 < n_groups)
        def _():
            for cp in copies(g + 1, 1 - buf):
                cp.start()

        for cp in copies(g, buf):
            cp.wait()
        hit_rows = []
        for s in range(COMBINE_GROUP):
            live = g * COMBINE_GROUP + s < count
            local = dest_rows - jnp.where(live, block_of(g, s) * MOE_BLOCK, n_rows)
            inside = (local >= 0) & (local < MOE_BLOCK)
            hit_rows.append(jnp.sum(jnp.where(inside, local + 1, 0), axis=0, keepdims=True) - 1)
        hit_rows.append(jnp.full((LANES - COMBINE_GROUP, chunk), -1, jnp.int32))
        hit_row = jnp.transpose(jnp.concatenate(hit_rows, axis=0).astype(F32)).astype(jnp.int32)
        pieces = [jnp.where(lane == hit_row[:, s:s + 1], 1.0, 0.0).astype(BF16) for s in range(COMBINE_GROUP)]
        onehot = jnp.concatenate(pieces, axis=1)
        acc[...] += jnp.dot(onehot, slab[buf], preferred_element_type=F32)
        return carry

    lax.fori_loop(0, n_groups, group, 0)
    y = acc[...]
    yn = y * lax.rsqrt(jnp.mean(y * y, axis=-1, keepdims=True) + EPS) * g_ref[...]
    o_ref[...] = x2_ref[...] + gf_ref[0] * yn


def combine_residual(yb, dest, x2, chunk_off, chunk_blocks, mod, mod_row_of_chunk, g_post_f):
    n, d = x2.shape
    chunk_row = lambda c, off, blk: (c, 0)
    grid_spec = pltpu.PrefetchScalarGridSpec(
        num_scalar_prefetch=2,
        grid=(n // MOE_CHUNK,),
        in_specs=[pl.BlockSpec(memory_space=pl.ANY),
                  pl.BlockSpec((SUBLANES, MOE_CHUNK), lambda c, off, blk: (0, c)),
                  pl.BlockSpec((MOE_CHUNK, d), chunk_row),
                  pl.BlockSpec((1, 1, d), lambda c, off, blk: (mod_row_of_chunk(c), 0, 5)),
                  pl.BlockSpec((1, d), lambda c, off, blk: (0, 0))],
        out_specs=pl.BlockSpec((MOE_CHUNK, d), chunk_row),
        scratch_shapes=[pltpu.VMEM((2, COMBINE_GROUP * MOE_BLOCK, d), BF16),
                        pltpu.SemaphoreType.DMA((2,)),
                        pltpu.VMEM((MOE_CHUNK, d), F32)],
    )
    return pl.pallas_call(
        _combine_kernel,
        grid_spec=grid_spec,
        out_shape=jax.ShapeDtypeStruct((n, d), F32),
        compiler_params=_params("arbitrary"),
        name="combine_residual",
    )(chunk_off, chunk_blocks, yb, dest, x2, mod, g_post_f.reshape(1, d))


def _grid_pos_embedding(rows, d):
    row = jnp.repeat(jnp.arange(rows, dtype=F32), GRID_W)
    col = jnp.tile(jnp.arange(GRID_W, dtype=F32), rows)
    quarter = d // 4
    omega = POS_BASE ** (-jnp.arange(quarter, dtype=F32) / quarter)

    def emb(p):
        ang = p[:, None] * omega[None, :]
        return jnp.concatenate([jnp.sin(ang), jnp.cos(ang)], axis=-1)

    return jnp.concatenate([emb(row), emb(col)], axis=-1)


def _count_le(sorted_vals, queries):
    return jnp.sum(sorted_vals <= queries[:, None], axis=1).astype(jnp.int32)


def _moe_plan(ti, cnt_after, n_tok, tm):
    assert tm == GATHER_SUB
    sub_after = cnt_after[:, 0, :N_EXPERTS].astype(jnp.int32)
    counts = sub_after[-1]
    padded = (counts + MOE_BLOCK - 1) // MOE_BLOCK * MOE_BLOCK
    padded_end = jnp.cumsum(padded)
    padded_start = padded_end - padded
    experts = jnp.arange(N_EXPERTS, dtype=jnp.int32)[:, None, None]
    start_of = jnp.sum(jnp.where(ti[None, :TOP_K] == experts, padded_start[:, None, None], 0), axis=0)
    dest_t = jnp.concatenate([start_of + ti[TOP_K:2 * TOP_K],
                              jnp.full((SUBLANES - TOP_K, n_tok), -1, jnp.int32)], axis=0)
    n_blocks = n_tok * TOP_K // MOE_BLOCK + N_EXPERTS
    blocks = jnp.arange(n_blocks, dtype=jnp.int32)
    block_expert = jnp.minimum(_count_le(padded_end[None, :], blocks * MOE_BLOCK), N_EXPERTS - 1)
    n_used = (padded_end[-1:] // MOE_BLOCK).astype(jnp.int32)
    r_lo = (blocks - padded_start[block_expert] // MOE_BLOCK) * MOE_BLOCK
    r_hi = jnp.minimum(counts[block_expert], r_lo + MOE_BLOCK) - 1
    through = sub_after.T[block_expert]
    n_sub = sub_after.shape[0]
    sub_lo = jnp.minimum(_count_le(through, r_lo), n_sub - 1)
    sub_hi = jnp.minimum(_count_le(through, r_hi), n_sub - 1)
    nch = n_tok // MOE_CHUNK
    per_chunk = MOE_CHUNK // tm
    cb_after = sub_after[per_chunk - 1::per_chunk]
    cb_before = jnp.concatenate([jnp.zeros((1, N_EXPERTS), jnp.int32), cb_after[:-1]], axis=0)
    b_lo = ((padded_start[None, :] + cb_before) // MOE_BLOCK).reshape(-1)
    b_hi = ((padded_start[None, :] + cb_after - 1) // MOE_BLOCK).reshape(-1)
    n_it = jnp.where((cb_after > cb_before).reshape(-1), b_hi - b_lo + 1, 0)
    off_end = jnp.cumsum(n_it)
    off = off_end - n_it
    n_items = n_blocks + N_EXPERTS * (nch - 1)
    w = jnp.minimum(jnp.arange(n_items, dtype=jnp.int32), off_end[-1] - 1)
    cell = jnp.minimum(_count_le(off_end[None, :], w), n_it.shape[0] - 1)
    chunk_blocks = (b_lo[cell] + w - off[cell]).astype(jnp.int32)
    chunk_off = jnp.concatenate([jnp.zeros((1,), jnp.int32), off_end[N_EXPERTS - 1::N_EXPERTS]]).astype(jnp.int32)
    return dest_t, block_expert, sub_lo, sub_hi, n_used, chunk_off, chunk_blocks


def kernel(x, c, ctx, c_ctx, w_mod, b_mod, g_pre_mix, g_post_mix, g_pre_ffn, g_post_ffn, w_in, conv_kv,
           conv_q, a_log, dt_bias, gdn_norm_g, w_proj_a, conf_dw, conf_dw_b, conf_ln_g, conf_ln_b,
           w_proj_b, w_out, w_router, b_router, w_gate_up, b_gate_up, w_down, b_down):
    batch, seq, d = x.shape
    ctx_len = ctx.shape[1]
    n_tok = batch * seq
    gw = HEADS * DV
    beta_off = HEADS * DK + gw
    state_cols = beta_off + 4 * HEADS
    q_off = state_cols

    mod_rows = 2 * SUBLANES
    cond = jnp.zeros((mod_rows, d), F32).at[:batch].set(c).at[batch].set(c_ctx)
    mod = modulation(cond, w_mod[0], b_mod[0]).reshape(mod_rows, 1, 6 * d)

    w = w_in[0]
    w_main = jnp.concatenate([w[:, :beta_off], w[:, q_off:]], axis=1).astype(BF16)
    w_ba = jnp.zeros((d, LANES), F32).at[:, :4 * HEADS].set(w[:, beta_off:state_cols]).astype(BF16)

    pos = _grid_pos_embedding(seq // GRID_W, d)
    x_flat = x.reshape(n_tok, d)
    tm = 1024
    tiles_per_seq = seq // tm
    proj, ba = input_projection(x_flat, pos, mod, lambda i: i // tiles_per_seq, g_pre_mix[0],
                                w_main, w_ba, tm=tm, tn=2048)
    proj_ctx, ba_ctx = input_projection(ctx.reshape(batch * ctx_len, d), None, mod, lambda i: batch,
                                        g_pre_mix[0], w_main[:, :beta_off], w_ba, tm=ctx_len, tn=1024)

    og = gated_deltanet(proj, ba, proj_ctx, ba_ctx, conv_kv[0], conv_q[0], a_log[0], dt_bias[0],
                        gdn_norm_g[0], batch=batch, seq=seq, ctx=ctx_len)
    uc = conformer_conv(proj, conf_dw[0], conf_dw_b[0], batch=batch, seq=seq)

    tm2 = 512
    w_r = jnp.zeros((d, LANES), F32).at[:, :N_EXPERTS].set(w_router[0]).astype(BF16)
    b_r = jnp.zeros((1, LANES), F32).at[0, :N_EXPERTS].set(b_router[0])
    x2, h2, top_i, top_w, cnt_after = mixer_output(
        og, uc, proj, x_flat, pos, mod, lambda i: i // (seq // tm2), g_post_mix[0], g_pre_ffn[0],
        conf_ln_g[0], conf_ln_b[0], w_proj_a[0].astype(BF16), w_proj_b[0].astype(BF16),
        w_out[0].astype(BF16), w_r, b_r, tm=tm2, sub=GATHER_SUB)

    dest_t, block_expert, sub_lo, sub_hi, n_used, chunk_off, chunk_blocks = _moe_plan(
        top_i, cnt_after, n_tok, GATHER_SUB)
    yb = expert_blocks(h2, dest_t, top_w, block_expert, sub_lo, sub_hi, n_used, w_gate_up[0], b_gate_up[0],
                       w_down[0], b_down[0])
    out = combine_residual(yb, dest_t, x2, chunk_off, chunk_blocks, mod, lambda c: c // (seq // MOE_CHUNK),
                           g_post_ffn[0])
    return out.reshape(batch, seq, d)
```

```python
import functools

import jax
import jax.numpy as jnp
from jax import lax
from jax.experimental import pallas as pl
from jax.experimental.pallas import tpu as pltpu

F32 = jnp.float32
BF16 = jnp.bfloat16

D_MODEL = 1024
GRID_W = 64
HEADS = 8
DK = 128
DV = 128
SHORT_CONV = 5
CHUNK = 64
CONF_KERNEL = 31
N_EXPERTS = 32
TOP_K = 4
SWIGLU_LIMIT = 7.0
SWIGLU_ALPHA = 1.702
MOE_BLOCK = 128
EPS = 1e-6
POS_BASE = 10000.0

LANES = 128
SUBLANES = 8
VMEM_LIMIT = 56 * 1024 * 1024

COL_K, COL_V, COL_Q, COL_Z, COL_GLU_A, COL_GLU_G, COL_GATE_A, COL_GATE_B = range(8)


def _params(*sem):
    return pltpu.CompilerParams(dimension_semantics=sem, vmem_limit_bytes=VMEM_LIMIT)


def _mod_kernel(c_ref, w_ref, b_ref, o_ref):
    c = c_ref[...]
    s = c * jax.nn.sigmoid(c)
    o_ref[...] = jnp.dot(s, w_ref[...], preferred_element_type=F32,
                         precision=lax.Precision.HIGHEST) + b_ref[...]


def modulation(cond, w_mod, b_mod):
    r, d = cond.shape
    n = w_mod.shape[1]
    tn = 1024
    return pl.pallas_call(
        _mod_kernel,
        grid=(n // tn,),
        in_specs=[pl.BlockSpec((r, d), lambda j: (0, 0)),
                  pl.BlockSpec((d, tn), lambda j: (0, j)),
                  pl.BlockSpec((1, tn), lambda j: (0, j))],
        out_specs=pl.BlockSpec((r, tn), lambda j: (0, j)),
        out_shape=jax.ShapeDtypeStruct((r, n), F32),
        compiler_params=_params("arbitrary"),
        name="modulation",
    )(cond, w_mod, b_mod.reshape(1, n))


def _inproj_kernel(*refs, has_pos):
    if has_pos:
        x_ref, pos_ref, sh_ref, sc_ref, g_ref, w_ref, wba_ref, o_ref, ba_ref, h_scr = refs
    else:
        x_ref, sh_ref, sc_ref, g_ref, w_ref, wba_ref, o_ref, ba_ref, h_scr = refs

    @pl.when(pl.program_id(1) == 0)
    def _():
        x = x_ref[...]
        if has_pos:
            x = x + pos_ref[...]
        y = x * lax.rsqrt(jnp.mean(x * x, axis=-1, keepdims=True) + EPS) * g_ref[...]
        h = (y * (1.0 + sc_ref[0]) + sh_ref[0]).astype(BF16)
        h_scr[...] = h
        ba_ref[...] = jnp.dot(h, wba_ref[...], preferred_element_type=F32)

    o_ref[...] = jnp.dot(h_scr[...], w_ref[...], preferred_element_type=F32).astype(o_ref.dtype)


def input_projection(x, pos, mod, mod_row_of_tile, g_pre, w_main, w_ba, *, tm, tn):
    n, d = x.shape
    w = w_main.shape[1]
    has_pos = pos is not None
    in_specs = [pl.BlockSpec((tm, d), lambda i, j: (i, 0))]
    args = [x]
    if has_pos:
        pos_tiles = pos.shape[0] // tm
        in_specs.append(pl.BlockSpec((tm, d), lambda i, j: (i % pos_tiles, 0)))
        args.append(pos)
    in_specs += [
        pl.BlockSpec((1, 1, d), lambda i, j: (mod_row_of_tile(i), 0, 0)),
        pl.BlockSpec((1, 1, d), lambda i, j: (mod_row_of_tile(i), 0, 1)),
        pl.BlockSpec((1, d), lambda i, j: (0, 0)),
        pl.BlockSpec((d, tn), lambda i, j: (0, j)),
        pl.BlockSpec((d, LANES), lambda i, j: (0, 0)),
    ]
    args += [mod, mod, g_pre.reshape(1, d), w_main, w_ba]
    return pl.pallas_call(
        functools.partial(_inproj_kernel, has_pos=has_pos),
        grid=(n // tm, w // tn),
        in_specs=in_specs,
        out_specs=[pl.BlockSpec((tm, tn), lambda i, j: (i, j)),
                   pl.BlockSpec((tm, LANES), lambda i, j: (i, 0))],
        out_shape=[jax.ShapeDtypeStruct((n, w), BF16), jax.ShapeDtypeStruct((n, LANES), F32)],
        scratch_shapes=[pltpu.VMEM((tm, d), BF16)],
        compiler_params=_params("arbitrary", "arbitrary"),
        name="input_projection_pos" if has_pos else "input_projection_ctx",
    )(*args)


def _silu(x):
    return x * jax.nn.sigmoid(x)


def _softplus(x):
    return jnp.maximum(x, 0.0) + jnp.log(1.0 + jnp.exp(-jnp.abs(x)))


def _short_conv(src, taps, pad_scr, rows):
    half = SHORT_CONV // 2
    pad_scr[0:SUBLANES, :] = jnp.zeros((SUBLANES, LANES), F32)
    pad_scr[SUBLANES:SUBLANES + rows, :] = src.astype(F32)
    pad_scr[SUBLANES + rows:2 * SUBLANES + rows, :] = jnp.zeros((SUBLANES, LANES), F32)
    acc = None
    for j in range(SHORT_CONV):
        off = SUBLANES + j - half
        term = pad_scr[off:off + rows, :] * taps[j:j + 1, :]
        acc = term if acc is None else acc + term
    return _silu(acc)


def _l2n(x):
    return x * lax.rsqrt(jnp.sum(x * x, axis=-1, keepdims=True) + EPS)


def _gdn_kernel(k_ref, v_ref, q_ref, z_ref, ba_ref, kc_ref, vc_ref, bac_ref,
                wk_ref, wv_ref, wq_ref, ab_ref, gn_ref, o_ref,
                kf, vf, qf, gates, pad_scr, lhs, val, att, ktt, dec, osc, *, seq, ctx, unroll, group):
    total = ctx + seq
    nc = total // CHUNK
    ncc = ctx // CHUNK
    two = 2 * CHUNK
    lg_chunk = CHUNK.bit_length() - 1

    ri = lax.broadcasted_iota(jnp.int32, (two, two), 0)
    ci = lax.broadcasted_iota(jnp.int32, (two, two), 1)
    same_dir = (ri >> lg_chunk) == (ci >> lg_chunk)
    incl = same_dir & (((ri < CHUNK) & (ri >= ci)) | ((ri >= CHUNK) & (ri <= ci)))
    strict = incl & (ri != ci)
    eye = jnp.where(ri == ci, 1.0, 0.0)
    tri = jnp.where(incl, 1.0, 0.0).astype(BF16)
    level_masks = []
    for lg in range(lg_chunk):
        same_parent = (ri >> (lg + 1)) == (ci >> (lg + 1))
        level_masks.append(jnp.where(same_parent & ((ri >> lg) != (ci >> lg)), 1.0, 0.0))
    top_rows = lax.broadcasted_iota(jnp.int32, (two, LANES), 0) < CHUNK

    def bwd_chunk(t):
        return jnp.where(t < ncc, ncc - 1 - t, nc + ncc - 1 - t)

    def stacked(ref, rf, rb):
        return jnp.concatenate([ref[pl.ds(rf, CHUNK), :], ref[pl.ds(rb, CHUNK), :]], axis=0)

    def chunk_load(t):
        rf = pl.multiple_of(t * CHUNK, CHUNK)
        rb = pl.multiple_of(bwd_chunk(t) * CHUNK, CHUNK)
        beta = jnp.concatenate([gates[0, pl.ds(rf, CHUNK), :], gates[1, pl.ds(rb, CHUNK), :]], axis=0)
        g = jnp.concatenate([gates[2, pl.ds(rf, CHUNK), :], gates[3, pl.ds(rb, CHUNK), :]], axis=0)
        return stacked(kf, rf, rb), stacked(vf, rf, rb), stacked(qf, rf, rb), beta, g

    def chunk_triangle(k, q, beta, g):
        kb = k.astype(BF16)
        kq = jnp.concatenate([kb, q.astype(BF16)], axis=0)
        kk_qk = lax.dot_general(kq, kb, (((1,), (1,)), ((), ())), preferred_element_type=F32)
        g_hi = g.astype(BF16)
        r1 = g - g_hi.astype(F32)
        g_mid = r1.astype(BF16)
        g_lo = (r1 - g_mid.astype(F32)).astype(BF16)
        gc3 = jnp.dot(tri, jnp.concatenate([g_hi, g_mid, g_lo], axis=1), preferred_element_type=F32)
        gc = gc3[:, :LANES] + gc3[:, LANES:2 * LANES] + gc3[:, 2 * LANES:]
        gc_row = jnp.transpose(gc)
        decay = jnp.where(incl, jnp.exp(gc - gc_row), 0.0)
        a = jnp.where(strict, beta * kk_qk[:two] * decay, 0.0)
        return a, (kk_qk[two:] * decay).astype(BF16), gc

    def chunk_finish(hh, t, k, v, q, beta, gc, attn, tmat):
        eg = jnp.exp(gc)
        rhs = jnp.concatenate([v * beta, k * beta * eg], axis=1).astype(BF16)
        sol = jnp.dot(tmat.astype(BF16), rhs, preferred_element_type=F32)
        g_end = jnp.where(top_rows, jnp.broadcast_to(gc[CHUNK - 1:CHUNK, :], (two, LANES)),
                          jnp.broadcast_to(gc[CHUNK:CHUNK + 1, :], (two, LANES)))
        k_tail = k * jnp.exp(g_end - gc)
        qg = (q * eg).astype(BF16)
        kcum = sol[:, DV:].astype(BF16)
        r0 = pl.multiple_of(t * two, two)
        lhs[hh, 0, pl.ds(r0, two), :] = jnp.concatenate([kcum[:CHUNK], qg[:CHUNK]], axis=0)
        lhs[hh, 1, pl.ds(r0, two), :] = jnp.concatenate([kcum[CHUNK:], qg[CHUNK:]], axis=0)
        val[hh, pl.ds(r0, two), :] = sol[:, :DV].astype(BF16)
        att[hh, pl.ds(r0, two), :] = attn
        ktt[hh, pl.ds(r0, two), :] = jnp.transpose(k_tail).astype(BF16)
        d0 = pl.multiple_of(t * SUBLANES, SUBLANES)
        e_end = jnp.exp(g_end)
        dec[hh, 0, pl.ds(d0, SUBLANES), :] = e_end[:SUBLANES]
        dec[hh, 1, pl.ds(d0, SUBLANES), :] = e_end[CHUNK:CHUNK + SUBLANES]

    for hh in range(group):
        head = pl.program_id(1) * group + hh
        cols = slice(hh * LANES, (hh + 1) * LANES)

        kf[0:ctx, :] = _l2n(_short_conv(kc_ref[:, cols], wk_ref[:, cols], pad_scr, ctx))
        vf[0:ctx, :] = _short_conv(vc_ref[:, cols], wv_ref[:, cols], pad_scr, ctx)
        qf[0:ctx, :] = jnp.zeros((ctx, LANES), F32)
        kf[ctx:total, :] = _l2n(_short_conv(k_ref[:, cols], wk_ref[:, cols], pad_scr, seq))
        vf[ctx:total, :] = _short_conv(v_ref[:, cols], wv_ref[:, cols], pad_scr, seq)
        qf[ctx:total, :] = _l2n(_short_conv(q_ref[:, cols], wq_ref[:, cols], pad_scr, seq)) * (DK ** -0.5)

        def gate_cols(src_ref, lo, rows):
            x = src_ref[...]
            lane = lax.broadcasted_iota(jnp.int32, (rows, LANES), 1)
            beta = jax.nn.sigmoid(x)
            g = -jnp.exp(ab_ref[0:1, :]) * _softplus(x + ab_ref[1:2, :])
            for slot, (arr, base) in enumerate(((beta, 0), (beta, HEADS), (g, 2 * HEADS), (g, 3 * HEADS))):
                col = jnp.sum(jnp.where(lane == base + head, arr, 0.0), axis=-1, keepdims=True)
                gates[slot, lo:lo + rows, :] = jnp.broadcast_to(col, (rows, LANES))

        gate_cols(bac_ref, 0, ctx)
        gate_cols(ba_ref, ctx, seq)

        def chunk_body(i, carry, hh=hh):
            steps = [i * unroll + u for u in range(unroll)]
            loaded = [chunk_load(t) for t in steps]
            tris = [chunk_triangle(k, q, beta, g) for (k, v, q, beta, g) in loaded]
            tmats = [eye - a * level_masks[0] for (a, _, _) in tris]
            for m in level_masks[1:]:
                nxt = []
                for (a, _, _), tmat in zip(tris, tmats):
                    cs = (a * m).astype(BF16)
                    tb = tmat.astype(BF16)
                    tc = jnp.dot(tb, cs, preferred_element_type=F32)
                    nxt.append(tmat - jnp.dot(tc.astype(BF16), tb, preferred_element_type=F32))
                tmats = nxt
            for t, (k, v, q, beta, g), (a, attn, gc), tmat in zip(steps, loaded, tris, tmats):
                chunk_finish(hh, t, k, v, q, beta, gc, attn, tmat)
            return carry

        lax.fori_loop(0, nc // unroll, chunk_body, 0)

    def scan_body(t, carry):
        rf = pl.multiple_of(t * CHUNK, CHUNK)
        rb = pl.multiple_of(bwd_chunk(t) * CHUNK, CHUNK)
        r0 = pl.multiple_of(t * two, two)
        d0 = pl.multiple_of(t * SUBLANES, SUBLANES)
        rs = [(jnp.dot(lhs[hh, 0, pl.ds(r0, two), :], carry[2 * hh].astype(BF16), preferred_element_type=F32),
               jnp.dot(lhs[hh, 1, pl.ds(r0, two), :], carry[2 * hh + 1].astype(BF16), preferred_element_type=F32))
              for hh in range(group)]
        v_news = [val[hh, pl.ds(r0, two), :].astype(F32) - jnp.concatenate([r_f[:CHUNK], r_b[:CHUNK]], axis=0)
                  for hh, (r_f, r_b) in enumerate(rs)]
        outs, new_states = [], []
        for hh, ((r_f, r_b), v_new) in enumerate(zip(rs, v_news)):
            zeros = jnp.zeros_like(v_new)
            v_bd = jnp.concatenate([jnp.where(top_rows, v_new, zeros), jnp.where(top_rows, zeros, v_new)],
                                   axis=1).astype(BF16)
            upd = jnp.dot(ktt[hh, pl.ds(r0, two), :], v_bd, preferred_element_type=F32)
            new_states.append(carry[2 * hh] * dec[hh, 0, pl.ds(d0, SUBLANES), :][0:1, :] + upd[:, :DV])
            new_states.append(carry[2 * hh + 1] * dec[hh, 1, pl.ds(d0, SUBLANES), :][0:1, :] + upd[:, DV:])
            outs.append(jnp.concatenate([r_f[CHUNK:], r_b[CHUNK:]], axis=0)
                        + jnp.dot(att[hh, pl.ds(r0, two), :], v_new.astype(BF16), preferred_element_type=F32))
        for hh, o in enumerate(outs):
            osc[hh, pl.ds(rf, CHUNK), :] += o[:CHUNK]
            osc[hh, pl.ds(rb, CHUNK), :] += o[CHUNK:]
        return tuple(new_states)

    osc[...] = jnp.zeros_like(osc)
    zero = jnp.zeros((DK, DV), F32)
    lax.fori_loop(0, nc, scan_body, (zero,) * (2 * group))

    for hh in range(group):
        cols = slice(hh * LANES, (hh + 1) * LANES)
        o = osc[hh, ctx:total, :]
        o = o * lax.rsqrt(jnp.mean(o * o, axis=-1, keepdims=True) + EPS) * gn_ref[...]
        o_ref[:, cols] = (o * _silu(z_ref[:, cols].astype(F32))).astype(o_ref.dtype)


def gated_deltanet(proj, ba, proj_ctx, ba_ctx, conv_kv, conv_q, a_log, dt_bias, gn, *, batch, seq, ctx,
                   unroll=12, group=4):
    total = seq + ctx
    nc = total // CHUNK
    gl = group * LANES
    ng = HEADS // group
    taps = jnp.zeros((SUBLANES, conv_kv.shape[1]), F32).at[:SHORT_CONV].set(conv_kv)
    taps_q = jnp.zeros((SUBLANES, conv_q.shape[1]), F32).at[:SHORT_CONV].set(conv_q)
    ab = jnp.zeros((SUBLANES, LANES), F32)
    ab = ab.at[0, 2 * HEADS:4 * HEADS].set(a_log.reshape(-1)).at[1, 2 * HEADS:4 * HEADS].set(dt_bias.reshape(-1))
    col = lambda base: (lambda b, h: (b, base * ng + h))
    once = pl.Buffered(1)
    return pl.pallas_call(
        functools.partial(_gdn_kernel, seq=seq, ctx=ctx, unroll=unroll, group=group),
        grid=(batch, ng),
        in_specs=[
            pl.BlockSpec((seq, gl), col(COL_K), pipeline_mode=once),
            pl.BlockSpec((seq, gl), col(COL_V), pipeline_mode=once),
            pl.BlockSpec((seq, gl), col(COL_Q), pipeline_mode=once),
            pl.BlockSpec((seq, gl), col(COL_Z), pipeline_mode=once),
            pl.BlockSpec((seq, LANES), lambda b, h: (b, 0), pipeline_mode=once),
            pl.BlockSpec((ctx, gl), col(COL_K)),
            pl.BlockSpec((ctx, gl), col(COL_V)),
            pl.BlockSpec((ctx, LANES), lambda b, h: (b, 0)),
            pl.BlockSpec((SUBLANES, gl), lambda b, h: (0, h)),
            pl.BlockSpec((SUBLANES, gl), lambda b, h: (0, ng + h)),
            pl.BlockSpec((SUBLANES, gl), lambda b, h: (0, h)),
            pl.BlockSpec((SUBLANES, LANES), lambda b, h: (0, 0)),
            pl.BlockSpec((1, LANES), lambda b, h: (0, 0)),
        ],
        out_specs=pl.BlockSpec((seq, gl), lambda b, h: (b, h)),
        out_shape=jax.ShapeDtypeStruct((batch * seq, HEADS * DV), BF16),
        scratch_shapes=[
            pltpu.VMEM((total, LANES), F32),
            pltpu.VMEM((total, LANES), F32),
            pltpu.VMEM((total, LANES), F32),
            pltpu.VMEM((4, total, LANES), F32),
            pltpu.VMEM((seq + 2 * SUBLANES, LANES), F32),
            pltpu.VMEM((group, 2, nc * 2 * CHUNK, LANES), BF16),
            pltpu.VMEM((group, nc * 2 * CHUNK, LANES), BF16),
            pltpu.VMEM((group, nc * 2 * CHUNK, LANES), BF16),
            pltpu.VMEM((group, nc * 2 * CHUNK, LANES), BF16),
            pltpu.VMEM((group, 2, nc * SUBLANES, LANES), F32),
            pltpu.VMEM((group, total, LANES), F32),
        ],
        compiler_params=_params("arbitrary", "arbitrary"),
        name="gated_deltanet",
    )(proj, proj, proj, proj, ba, proj_ctx, proj_ctx, ba_ctx, taps, taps, taps_q, ab, gn.reshape(1, DV))


CONF_ROWS = 128
CONF_HALO = 16


def _conf_kernel(a_ref, g_ref, w_ref, b_ref, o_ref, pad_scr, *, seq):
    cb = a_ref.shape[1]
    u = a_ref[...].astype(F32) * jax.nn.sigmoid(g_ref[...].astype(F32))
    pad_scr[0:CONF_HALO, :] = jnp.zeros((CONF_HALO, cb), F32)
    pad_scr[CONF_HALO:CONF_HALO + seq, :] = u
    pad_scr[CONF_HALO + seq:2 * CONF_HALO + seq, :] = jnp.zeros((CONF_HALO, cb), F32)
    win = CONF_ROWS + 2 * CONF_HALO
    first = CONF_HALO - CONF_KERNEL // 2

    def body(i, carry):
        r0 = pl.multiple_of(i * CONF_ROWS, CONF_ROWS)
        w = pad_scr[pl.ds(r0, win), :]
        acc = jnp.zeros((CONF_ROWS, cb), F32) + b_ref[...]
        for sub in range(SUBLANES):
            shifted = w if sub == 0 else pltpu.roll(w, win - sub, axis=0)
            for j in range(CONF_KERNEL):
                off = first + j
                if off % SUBLANES == sub:
                    base = off - sub
                    acc = acc + shifted[base:base + CONF_ROWS, :] * w_ref[j:j + 1, :]
        o_ref[pl.ds(r0, CONF_ROWS), :] = acc.astype(o_ref.dtype)
        return carry

    lax.fori_loop(0, seq // CONF_ROWS, body, 0)


def conformer_conv(proj, conf_dw, conf_dw_b, *, batch, seq, cb=256):
    c = conf_dw.shape[1]
    nb = c // cb
    taps = jnp.zeros((32, c), F32).at[:CONF_KERNEL].set(conf_dw)
    return pl.pallas_call(
        functools.partial(_conf_kernel, seq=seq),
        grid=(batch, nb),
        in_specs=[pl.BlockSpec((seq, cb), lambda b, j: (b, COL_GLU_A * nb + j)),
                  pl.BlockSpec((seq, cb), lambda b, j: (b, COL_GLU_G * nb + j)),
                  pl.BlockSpec((32, cb), lambda b, j: (0, j)),
                  pl.BlockSpec((1, cb), lambda b, j: (0, j))],
        out_specs=pl.BlockSpec((seq, cb), lambda b, j: (b, j)),
        out_shape=jax.ShapeDtypeStruct((batch * seq, c), BF16),
        scratch_shapes=[pltpu.VMEM((seq + 2 * CONF_HALO, cb), F32)],
        compiler_params=_params("arbitrary", "arbitrary"),
        name="conformer_conv",
    )(proj, proj, taps, conf_dw_b.reshape(1, c))


def _mixer_out_kernel(og_ref, uc_ref, ga_ref, gb_ref, x_ref, pos_ref, gm_ref, shf_ref, scf_ref,
                      gpost_ref, gpre_ref, lng_ref, lnb_ref, wa_ref, wb_ref, wo_ref, wr_ref, br_ref,
                      x2_ref, h2_ref, ti_ref, tw_ref, cnt_ref, cnt_scr, *, sub):
    parts = [slice(r, r + sub) for r in range(0, og_ref.shape[0], sub)]

    def rms(v):
        return v * lax.rsqrt(jnp.mean(v * v, axis=-1, keepdims=True) + EPS)

    def layer_norm_silu(uc):
        mu = jnp.mean(uc, axis=-1, keepdims=True)
        var = jnp.mean(jnp.square(uc - mu), axis=-1, keepdims=True)
        return _silu((uc - mu) * lax.rsqrt(var + EPS) * lng_ref[...] + lnb_ref[...]).astype(BF16)

    us = [layer_norm_silu(uc_ref[p, :].astype(F32)) for p in parts]
    yas = [jnp.dot(og_ref[p, :], wa_ref[...], preferred_element_type=F32) for p in parts]
    ybs = [jnp.dot(u, wb_ref[...], preferred_element_type=F32) for u in us]
    merged = [(jax.nn.sigmoid(ga_ref[p, :].astype(F32)) * ya
               + jax.nn.sigmoid(gb_ref[p, :].astype(F32)) * yb).astype(BF16) for p, ya, yb in zip(parts, yas, ybs)]
    ys = [jnp.dot(m, wo_ref[...], preferred_element_type=F32) for m in merged]
    x2s = [x_ref[p, :] + pos_ref[p, :] + gm_ref[0] * (rms(y) * gpost_ref[...]) for p, y in zip(parts, ys)]
    h2s = [(rms(x2) * gpre_ref[...] * (1.0 + scf_ref[0]) + shf_ref[0]).astype(BF16) for x2 in x2s]
    for p, x2, h2 in zip(parts, x2s, h2s):
        x2_ref[p, :] = x2
        h2_ref[p, :] = h2
    all_logits = [jnp.dot(h2, wr_ref[...], preferred_element_type=F32) + br_ref[...] for h2 in h2s]
    lane = lax.broadcasted_iota(jnp.int32, (sub, LANES), 1).astype(F32)
    ri = lax.broadcasted_iota(jnp.int32, (sub, sub), 0)
    ci = lax.broadcasted_iota(jnp.int32, (sub, sub), 1)
    before = jnp.where(ri > ci, 1.0, 0.0).astype(BF16)

    @pl.when(pl.program_id(0) == 0)
    def _():
        cnt_scr[...] = jnp.zeros_like(cnt_scr)

    routed = []
    for logits in all_logits:
        live = jnp.where(lane < N_EXPERTS, logits, -jnp.inf)
        top_v, top_i = [], []
        for _ in range(TOP_K):
            m = jnp.max(live, axis=-1, keepdims=True)
            idx = jnp.min(jnp.where(live == m, lane, float(LANES)), axis=-1, keepdims=True)
            top_v.append(m)
            top_i.append(idx)
            live = jnp.where(lane == idx, -jnp.inf, live)
        ex = [jnp.exp(v - top_v[0]) for v in top_v]
        denom = ex[0] + ex[1] + ex[2] + ex[3]
        picked = jnp.zeros((sub, LANES), F32)
        for k in range(TOP_K):
            picked = jnp.where(lane == top_i[k], 1.0, picked)
        routed.append((top_i, [e / denom for e in ex], picked,
                       jnp.dot(before, picked.astype(BF16), preferred_element_type=F32)))

    count = cnt_scr[...]
    for j, (p, (top_i, top_w, picked, within)) in enumerate(zip(parts, routed)):
        prefix = within + count[0:1, :]
        ti = jnp.zeros((sub, LANES), F32)
        tw = jnp.zeros((sub, LANES), F32)
        for k in range(TOP_K):
            rank = jnp.sum(jnp.where(lane == top_i[k], prefix, 0.0), axis=-1, keepdims=True)
            ti = jnp.where(lane == k, top_i[k], ti)
            ti = jnp.where(lane == TOP_K + k, rank, ti)
            tw = jnp.where(lane == k, top_w[k], tw)
        ti_ref[:, p] = jnp.transpose(ti)[:SUBLANES, :].astype(jnp.int32)
        tw_ref[:, p] = jnp.transpose(tw)[:SUBLANES, :]
        count = count + jnp.sum(picked, axis=0, keepdims=True)
        cnt_ref[j] = count
    cnt_scr[...] = count


def mixer_output(og, uc, proj, x, pos, mod, mod_row_of_tile, g_post, g_pre_f, ln_g, ln_b,
                 w_a, w_b, w_o, w_r, b_r, *, tm, sub):
    n, d = x.shape
    pos_tiles = pos.shape[0] // tm
    row = lambda i: (i, 0)
    fixed = lambda i: (0, 0)
    modspec = lambda k: pl.BlockSpec((1, 1, d), lambda i: (mod_row_of_tile(i), 0, k))
    vec = pl.BlockSpec((1, d), fixed)
    mat = pl.BlockSpec((d, d), fixed)
    return pl.pallas_call(
        functools.partial(_mixer_out_kernel, sub=sub),
        grid=(n // tm,),
        in_specs=[pl.BlockSpec((tm, d), row), pl.BlockSpec((tm, d), row),
                  pl.BlockSpec((tm, d), lambda i: (i, COL_GATE_A)),
                  pl.BlockSpec((tm, d), lambda i: (i, COL_GATE_B)),
                  pl.BlockSpec((tm, d), row),
                  pl.BlockSpec((tm, d), lambda i: (i % pos_tiles, 0)),
                  modspec(2), modspec(3), modspec(4),
                  vec, vec, vec, vec, mat, mat, mat,
                  pl.BlockSpec((d, LANES), fixed), pl.BlockSpec((1, LANES), fixed)],
        out_specs=[pl.BlockSpec((tm, d), row), pl.BlockSpec((tm, d), row),
                   pl.BlockSpec((SUBLANES, tm), lambda i: (0, i)), pl.BlockSpec((SUBLANES, tm), lambda i: (0, i)),
                   pl.BlockSpec((tm // sub, SUBLANES, LANES), lambda i: (i, 0, 0))],
        out_shape=[jax.ShapeDtypeStruct((n, d), F32), jax.ShapeDtypeStruct((n, d), BF16),
                   jax.ShapeDtypeStruct((SUBLANES, n), jnp.int32), jax.ShapeDtypeStruct((SUBLANES, n), F32),
                   jax.ShapeDtypeStruct((n // sub, SUBLANES, LANES), F32)],
        scratch_shapes=[pltpu.VMEM((SUBLANES, LANES), F32)],
        compiler_params=_params("arbitrary"),
        name="mixer_output",
    )(og, uc, proj, proj, x, pos, mod, mod, mod,
      g_post.reshape(1, d), g_pre_f.reshape(1, d), ln_g.reshape(1, d), ln_b.reshape(1, d),
      w_a, w_b, w_o, w_r, b_r)


MOE_CHUNK = 512
GATHER_SUB = 256
COMBINE_GROUP = 8


def _expert_kernel(bexp_ref, slo_ref, shi_ref, nused_ref,
                   h2_ref, dest_ref, wt_ref, wgu_ref, bgu_ref, wd_ref, bd_ref, o_ref,
                   acc, rw_acc, wgu_bf, wd_bf):
    b = pl.program_id(0)
    expert = bexp_ref[b]

    @pl.when((b == 0) | (expert != bexp_ref[jnp.maximum(b - 1, 0)]))
    def _():
        wgu_bf[...] = wgu_ref[0].astype(BF16)
        wd_bf[...] = wd_ref[0].astype(BF16)

    @pl.when(b < nused_ref[0])
    def _():
        acc[...] = jnp.zeros_like(acc)
        rw_acc[...] = jnp.zeros_like(rw_acc)
        row = lax.broadcasted_iota(jnp.int32, (MOE_BLOCK, GATHER_SUB), 0)
        s_lo, s_hi = slo_ref[b], shi_ref[b]
        last_sub = h2_ref.shape[0] // GATHER_SUB - 1

        def select(s, live):
            t0 = pl.multiple_of(s * GATHER_SUB, GATHER_SUB)
            local = dest_ref[:, pl.ds(t0, GATHER_SUB)] - b * MOE_BLOCK
            if live is not True:
                local = jnp.where(live, local, -1)
            inside = (local >= 0) & (local < MOE_BLOCK)
            hit_row = jnp.sum(jnp.where(inside, local + 1, 0), axis=0, keepdims=True) - 1
            hit_w = jnp.sum(jnp.where(inside, wt_ref[:, pl.ds(t0, GATHER_SUB)], 0.0), axis=0, keepdims=True)
            hit = row == hit_row
            return (jnp.where(hit, 1.0, 0.0).astype(BF16), jnp.where(hit, hit_w, 0.0),
                    h2_ref[pl.ds(t0, GATHER_SUB), :])

        def gather(i, carry):
            s0 = s_lo + 2 * i
            p0, w0, x0 = select(s0, True)
            p1, w1, x1 = select(jnp.minimum(s0 + 1, last_sub), s0 + 1 <= s_hi)
            acc[...] += jnp.dot(jnp.concatenate([p0, p1], axis=1), jnp.concatenate([x0, x1], axis=0),
                                preferred_element_type=F32)
            rw_acc[...] += jnp.sum(w0 + w1, axis=1, keepdims=True)
            return carry

        lax.fori_loop(0, (s_hi - s_lo + 2) // 2, gather, 0)
        gu = jnp.dot(acc[...].astype(BF16), wgu_bf[...], preferred_element_type=F32) + bgu_ref[0]
        dff = gu.shape[1] // 2
        gl = jnp.minimum(gu[:, :dff], SWIGLU_LIMIT)
        lin = jnp.clip(gu[:, dff:], -SWIGLU_LIMIT, SWIGLU_LIMIT)
        act = (gl * jax.nn.sigmoid(SWIGLU_ALPHA * gl) * (lin + 1.0)).astype(BF16)
        y = jnp.dot(act, wd_bf[...], preferred_element_type=F32) + bd_ref[0]
        o_ref[...] = (y * rw_acc[...]).astype(o_ref.dtype)

    @pl.when(b >= nused_ref[0])
    def _():
        o_ref[...] = jnp.zeros_like(o_ref)


def expert_blocks(h2, dest_t, wt_t, block_expert, sub_lo, sub_hi, n_used, w_gate_up, b_gate_up, w_down, b_down):
    n, d = h2.shape
    e, _, f2 = w_gate_up.shape
    n_blocks = block_expert.shape[0]
    whole = lambda b, be, lo, hi, nu: (0, 0)
    by_expert = lambda b, be, lo, hi, nu: (be[b], 0, 0)
    once = pl.Buffered(1)
    grid_spec = pltpu.PrefetchScalarGridSpec(
        num_scalar_prefetch=4,
        grid=(n_blocks,),
        in_specs=[pl.BlockSpec((n, d), whole, pipeline_mode=once),
                  pl.BlockSpec((SUBLANES, n), whole, pipeline_mode=once),
                  pl.BlockSpec((SUBLANES, n), whole, pipeline_mode=once),
                  pl.BlockSpec((1, d, f2), by_expert, pipeline_mode=once),
                  pl.BlockSpec((1, 1, f2), by_expert),
                  pl.BlockSpec((1, f2 // 2, d), by_expert, pipeline_mode=once),
                  pl.BlockSpec((1, 1, d), by_expert)],
        out_specs=pl.BlockSpec((MOE_BLOCK, d), lambda b, be, lo, hi, nu: (b, 0)),
        scratch_shapes=[pltpu.VMEM((MOE_BLOCK, d), F32), pltpu.VMEM((MOE_BLOCK, 1), F32),
                        pltpu.VMEM((d, f2), BF16), pltpu.VMEM((f2 // 2, d), BF16)],
    )
    return pl.pallas_call(
        _expert_kernel,
        grid_spec=grid_spec,
        out_shape=jax.ShapeDtypeStruct((n_blocks * MOE_BLOCK, d), BF16),
        compiler_params=_params("arbitrary"),
        name="expert_blocks",
    )(block_expert, sub_lo, sub_hi, n_used, h2, dest_t, wt_t, w_gate_up, b_gate_up.reshape(e, 1, f2), w_down,
      b_down.reshape(e, 1, d))


def _combine_kernel(coff_ref, cblk_ref, yb_hbm, dest_ref, x2_ref, gf_ref, g_ref, o_ref, slab, sem, acc):
    c = pl.program_id(0)
    start = coff_ref[c]
    count = coff_ref[c + 1] - start
    n_groups = (count + COMBINE_GROUP - 1) // COMBINE_GROUP

    def block_of(g, s):
        return cblk_ref[start + jnp.minimum(g * COMBINE_GROUP + s, count - 1)]

    def copies(g, buf):
        return [pltpu.make_async_copy(
            yb_hbm.at[pl.ds(pl.multiple_of(block_of(g, s) * MOE_BLOCK, MOE_BLOCK), MOE_BLOCK), :],
            slab.at[buf, pl.ds(s * MOE_BLOCK, MOE_BLOCK), :], sem.at[buf]) for s in range(COMBINE_GROUP)]

    for cp in copies(0, 0):
        cp.start()
    acc[...] = jnp.zeros_like(acc)
    chunk = x2_ref.shape[0]
    lane = lax.broadcasted_iota(jnp.int32, (chunk, MOE_BLOCK), 1)
    dest_rows = dest_ref[...]
    n_rows = yb_hbm.shape[0]

    def group(g, carry):
        buf = g % 2

        @pl.when(g + 1 < n_groups)
        def _():
            for cp in copies(g + 1, 1 - buf):
                cp.start()

        for cp in copies(g, buf):
            cp.wait()
        hit_rows = []
        for s in range(COMBINE_GROUP):
            live = g * COMBINE_GROUP + s < count
            local = dest_rows - jnp.where(live, block_of(g, s) * MOE_BLOCK, n_rows)
            inside = (local >= 0) & (local < MOE_BLOCK)
            hit_rows.append(jnp.sum(jnp.where(inside, local + 1, 0), axis=0, keepdims=True) - 1)
        hit_rows.append(jnp.full((LANES - COMBINE_GROUP, chunk), -1, jnp.int32))
        hit_row = jnp.transpose(jnp.concatenate(hit_rows, axis=0).astype(F32)).astype(jnp.int32)
        pieces = [jnp.where(lane == hit_row[:, s:s + 1], 1.0, 0.0).astype(BF16) for s in range(COMBINE_GROUP)]
        onehot = jnp.concatenate(pieces, axis=1)
        acc[...] += jnp.dot(onehot, slab[buf], preferred_element_type=F32)
        return carry

    lax.fori_loop(0, n_groups, group, 0)
    y = acc[...]
    yn = y * lax.rsqrt(jnp.mean(y * y, axis=-1, keepdims=True) + EPS) * g_ref[...]
    o_ref[...] = x2_ref[...] + gf_ref[0] * yn


def combine_residual(yb, dest, x2, chunk_off, chunk_blocks, mod, mod_row_of_chunk, g_post_f):
    n, d = x2.shape
    chunk_row = lambda c, off, blk: (c, 0)
    grid_spec = pltpu.PrefetchScalarGridSpec(
        num_scalar_prefetch=2,
        grid=(n // MOE_CHUNK,),
        in_specs=[pl.BlockSpec(memory_space=pl.ANY),
                  pl.BlockSpec((SUBLANES, MOE_CHUNK), lambda c, off, blk: (0, c)),
                  pl.BlockSpec((MOE_CHUNK, d), chunk_row),
                  pl.BlockSpec((1, 1, d), lambda c, off, blk: (mod_row_of_chunk(c), 0, 5)),
                  pl.BlockSpec((1, d), lambda c, off, blk: (0, 0))],
        out_specs=pl.BlockSpec((MOE_CHUNK, d), chunk_row),
        scratch_shapes=[pltpu.VMEM((2, COMBINE_GROUP * MOE_BLOCK, d), BF16),
                        pltpu.SemaphoreType.DMA((2,)),
                        pltpu.VMEM((MOE_CHUNK, d), F32)],
    )
    return pl.pallas_call(
        _combine_kernel,
        grid_spec=grid_spec,
        out_shape=jax.ShapeDtypeStruct((n, d), F32),
        compiler_params=_params("arbitrary"),
        name="combine_residual",
    )(chunk_off, chunk_blocks, yb, dest, x2, mod, g_post_f.reshape(1, d))


def _grid_pos_embedding(rows, d):
    row = jnp.repeat(jnp.arange(rows, dtype=F32), GRID_W)
    col = jnp.tile(jnp.arange(GRID_W, dtype=F32), rows)
    quarter = d // 4
    omega = POS_BASE ** (-jnp.arange(quarter, dtype=F32) / quarter)

    def emb(p):
        ang = p[:, None] * omega[None, :]
        return jnp.concatenate([jnp.sin(ang), jnp.cos(ang)], axis=-1)

    return jnp.concatenate([emb(row), emb(col)], axis=-1)


def _count_le(sorted_vals, queries):
    return jnp.sum(sorted_vals <= queries[:, None], axis=1).astype(jnp.int32)


def _moe_plan(ti, cnt_after, n_tok, tm):
    assert tm == GATHER_SUB
    sub_after = cnt_after[:, 0, :N_EXPERTS].astype(jnp.int32)
    counts = sub_after[-1]
    padded = (counts + MOE_BLOCK - 1) // MOE_BLOCK * MOE_BLOCK
    padded_end = jnp.cumsum(padded)
    padded_start = padded_end - padded
    dest_t = jnp.concatenate([padded_start[ti[:TOP_K]] + ti[TOP_K:2 * TOP_K],
                              jnp.full((SUBLANES - TOP_K, n_tok), -1, jnp.int32)], axis=0)
    n_blocks = n_tok * TOP_K // MOE_BLOCK + N_EXPERTS
    blocks = jnp.arange(n_blocks, dtype=jnp.int32)
    block_expert = jnp.minimum(_count_le(padded_end[None, :], blocks * MOE_BLOCK), N_EXPERTS - 1)
    n_used = (padded_end[-1:] // MOE_BLOCK).astype(jnp.int32)
    r_lo = (blocks - padded_start[block_expert] // MOE_BLOCK) * MOE_BLOCK
    r_hi = jnp.minimum(counts[block_expert], r_lo + MOE_BLOCK) - 1
    through = sub_after.T[block_expert]
    n_sub = sub_after.shape[0]
    sub_lo = jnp.minimum(_count_le(through, r_lo), n_sub - 1)
    sub_hi = jnp.minimum(_count_le(through, r_hi), n_sub - 1)
    nch = n_tok // MOE_CHUNK
    per_chunk = MOE_CHUNK // tm
    cb_after = sub_after[per_chunk - 1::per_chunk]
    cb_before = jnp.concatenate([jnp.zeros((1, N_EXPERTS), jnp.int32), cb_after[:-1]], axis=0)
    b_lo = ((padded_start[None, :] + cb_before) // MOE_BLOCK).reshape(-1)
    b_hi = ((padded_start[None, :] + cb_after - 1) // MOE_BLOCK).reshape(-1)
    n_it = jnp.where((cb_after > cb_before).reshape(-1), b_hi - b_lo + 1, 0)
    off_end = jnp.cumsum(n_it)
    off = off_end - n_it
    n_items = n_blocks + N_EXPERTS * (nch - 1)
    w = jnp.minimum(jnp.arange(n_items, dtype=jnp.int32), off_end[-1] - 1)
    cell = jnp.minimum(_count_le(off_end[None, :], w), n_it.shape[0] - 1)
    chunk_blocks = (b_lo[cell] + w - off[cell]).astype(jnp.int32)
    chunk_off = jnp.concatenate([jnp.zeros((1,), jnp.int32), off_end[N_EXPERTS - 1::N_EXPERTS]]).astype(jnp.int32)
    return dest_t, block_expert, sub_lo, sub_hi, n_used, chunk_off, chunk_blocks


def kernel(x, c, ctx, c_ctx, w_mod, b_mod, g_pre_mix, g_post_mix, g_pre_ffn, g_post_ffn, w_in, conv_kv,
           conv_q, a_log, dt_bias, gdn_norm_g, w_proj_a, conf_dw, conf_dw_b, conf_ln_g, conf_ln_b,
           w_proj_b, w_out, w_router, b_router, w_gate_up, b_gate_up, w_down, b_down):
    batch, seq, d = x.shape
    ctx_len = ctx.shape[1]
    n_tok = batch * seq
    gw = HEADS * DV
    beta_off = HEADS * DK + gw
    state_cols = beta_off + 4 * HEADS
    q_off = state_cols

    mod_rows = 2 * SUBLANES
    cond = jnp.zeros((mod_rows, d), F32).at[:batch].set(c).at[batch].set(c_ctx)
    mod = modulation(cond, w_mod[0], b_mod[0]).reshape(mod_rows, 1, 6 * d)

    w = w_in[0]
    w_main = jnp.concatenate([w[:, :beta_off], w[:, q_off:]], axis=1).astype(BF16)
    w_ba = jnp.zeros((d, LANES), F32).at[:, :4 * HEADS].set(w[:, beta_off:state_cols]).astype(BF16)

    pos = _grid_pos_embedding(seq // GRID_W, d)
    x_flat = x.reshape(n_tok, d)
    tm = 1024
    tiles_per_seq = seq // tm
    proj, ba = input_projection(x_flat, pos, mod, lambda i: i // tiles_per_seq, g_pre_mix[0],
                                w_main, w_ba, tm=tm, tn=2048)
    proj_ctx, ba_ctx = input_projection(ctx.reshape(batch * ctx_len, d), None, mod, lambda i: batch,
                                        g_pre_mix[0], w_main[:, :beta_off], w_ba, tm=ctx_len, tn=1024)

    og = gated_deltanet(proj, ba, proj_ctx, ba_ctx, conv_kv[0], conv_q[0], a_log[0], dt_bias[0],
                        gdn_norm_g[0], batch=batch, seq=seq, ctx=ctx_len)
    uc = conformer_conv(proj, conf_dw[0], conf_dw_b[0], batch=batch, seq=seq)

    tm2 = 512
    w_r = jnp.zeros((d, LANES), F32).at[:, :N_EXPERTS].set(w_router[0]).astype(BF16)
    b_r = jnp.zeros((1, LANES), F32).at[0, :N_EXPERTS].set(b_router[0])
    x2, h2, top_i, top_w, cnt_after = mixer_output(
        og, uc, proj, x_flat, pos, mod, lambda i: i // (seq // tm2), g_post_mix[0], g_pre_ffn[0],
        conf_ln_g[0], conf_ln_b[0], w_proj_a[0].astype(BF16), w_proj_b[0].astype(BF16),
        w_out[0].astype(BF16), w_r, b_r, tm=tm2, sub=GATHER_SUB)

    dest_t, block_expert, sub_lo, sub_hi, n_used, chunk_off, chunk_blocks = _moe_plan(
        top_i, cnt_after, n_tok, GATHER_SUB)
    yb = expert_blocks(h2, dest_t, top_w, block_expert, sub_lo, sub_hi, n_used, w_gate_up[0], b_gate_up[0],
                       w_down[0], b_down[0])
    out = combine_residual(yb, dest_t, x2, chunk_off, chunk_blocks, mod, lambda c: c // (seq // MOE_CHUNK),
                           g_post_ffn[0])
    return out.reshape(batch, seq, d)
```

```python
import functools

import jax
import jax.numpy as jnp
from jax import lax
from jax.experimental import pallas as pl
from jax.experimental.pallas import tpu as pltpu

F32 = jnp.float32
BF16 = jnp.bfloat16

D_MODEL = 1024
GRID_W = 64
HEADS = 8
DK = 128
DV = 128
SHORT_CONV = 5
CHUNK = 64
CONF_KERNEL = 31
N_EXPERTS = 32
TOP_K = 4
SWIGLU_LIMIT = 7.0
SWIGLU_ALPHA = 1.702
MOE_BLOCK = 128
EPS = 1e-6
POS_BASE = 10000.0

LANES = 128
SUBLANES = 8
VMEM_LIMIT = 56 * 1024 * 1024

COL_K, COL_V, COL_Q, COL_Z, COL_GLU_A, COL_GLU_G, COL_GATE_A, COL_GATE_B = range(8)


def _params(*sem):
    return pltpu.CompilerParams(dimension_semantics=sem, vmem_limit_bytes=VMEM_LIMIT)


def _mod_kernel(c_ref, w_ref, b_ref, o_ref):
    c = c_ref[...]
    s = c * jax.nn.sigmoid(c)
    o_ref[...] = jnp.dot(s, w_ref[...], preferred_element_type=F32,
                         precision=lax.Precision.HIGHEST) + b_ref[...]


def modulation(cond, w_mod, b_mod):
    r, d = cond.shape
    n = w_mod.shape[1]
    tn = 1024
    return pl.pallas_call(
        _mod_kernel,
        grid=(n // tn,),
        in_specs=[pl.BlockSpec((r, d), lambda j: (0, 0)),
                  pl.BlockSpec((d, tn), lambda j: (0, j)),
                  pl.BlockSpec((1, tn), lambda j: (0, j))],
        out_specs=pl.BlockSpec((r, tn), lambda j: (0, j)),
        out_shape=jax.ShapeDtypeStruct((r, n), F32),
        compiler_params=_params("arbitrary"),
        name="modulation",
    )(cond, w_mod, b_mod.reshape(1, n))


def _inproj_kernel(*refs, has_pos):
    if has_pos:
        x_ref, pos_ref, sh_ref, sc_ref, g_ref, w_ref, wba_ref, o_ref, ba_ref, h_scr = refs
    else:
        x_ref, sh_ref, sc_ref, g_ref, w_ref, wba_ref, o_ref, ba_ref, h_scr = refs

    @pl.when(pl.program_id(1) == 0)
    def _():
        x = x_ref[...]
        if has_pos:
            x = x + pos_ref[...]
        y = x * lax.rsqrt(jnp.mean(x * x, axis=-1, keepdims=True) + EPS) * g_ref[...]
        h = (y * (1.0 + sc_ref[0]) + sh_ref[0]).astype(BF16)
        h_scr[...] = h
        ba_ref[...] = jnp.dot(h, wba_ref[...], preferred_element_type=F32)

    o_ref[...] = jnp.dot(h_scr[...], w_ref[...], preferred_element_type=F32).astype(o_ref.dtype)


def input_projection(x, pos, mod, mod_row_of_tile, g_pre, w_main, w_ba, *, tm, tn):
    n, d = x.shape
    w = w_main.shape[1]
    has_pos = pos is not None
    in_specs = [pl.BlockSpec((tm, d), lambda i, j: (i, 0))]
    args = [x]
    if has_pos:
        pos_tiles = pos.shape[0] // tm
        in_specs.append(pl.BlockSpec((tm, d), lambda i, j: (i % pos_tiles, 0)))
        args.append(pos)
    in_specs += [
        pl.BlockSpec((1, 1, d), lambda i, j: (mod_row_of_tile(i), 0, 0)),
        pl.BlockSpec((1, 1, d), lambda i, j: (mod_row_of_tile(i), 0, 1)),
        pl.BlockSpec((1, d), lambda i, j: (0, 0)),
        pl.BlockSpec((d, tn), lambda i, j: (0, j)),
        pl.BlockSpec((d, LANES), lambda i, j: (0, 0)),
    ]
    args += [mod, mod, g_pre.reshape(1, d), w_main, w_ba]
    return pl.pallas_call(
        functools.partial(_inproj_kernel, has_pos=has_pos),
        grid=(n // tm, w // tn),
        in_specs=in_specs,
        out_specs=[pl.BlockSpec((tm, tn), lambda i, j: (i, j)),
                   pl.BlockSpec((tm, LANES), lambda i, j: (i, 0))],
        out_shape=[jax.ShapeDtypeStruct((n, w), BF16), jax.ShapeDtypeStruct((n, LANES), F32)],
        scratch_shapes=[pltpu.VMEM((tm, d), BF16)],
        compiler_params=_params("arbitrary", "arbitrary"),
        name="input_projection_pos" if has_pos else "input_projection_ctx",
    )(*args)


def _silu(x):
    return x * jax.nn.sigmoid(x)


def _softplus(x):
    return jnp.maximum(x, 0.0) + jnp.log(1.0 + jnp.exp(-jnp.abs(x)))


def _short_conv(src, taps, pad_scr, rows):
    half = SHORT_CONV // 2
    pad_scr[0:SUBLANES, :] = jnp.zeros((SUBLANES, LANES), F32)
    pad_scr[SUBLANES:SUBLANES + rows, :] = src.astype(F32)
    pad_scr[SUBLANES + rows:2 * SUBLANES + rows, :] = jnp.zeros((SUBLANES, LANES), F32)
    acc = None
    for j in range(SHORT_CONV):
        off = SUBLANES + j - half
        term = pad_scr[off:off + rows, :] * taps[j:j + 1, :]
        acc = term if acc is None else acc + term
    return _silu(acc)


def _l2n(x):
    return x * lax.rsqrt(jnp.sum(x * x, axis=-1, keepdims=True) + EPS)


def _gdn_kernel(k_ref, v_ref, q_ref, z_ref, ba_ref, kc_ref, vc_ref, bac_ref,
                wk_ref, wv_ref, wq_ref, ab_ref, gn_ref, o_ref,
                kf, vf, qf, gates, pad_scr, lhs, val, att, ktt, dec, osc, *, seq, ctx, unroll, group):
    total = ctx + seq
    nc = total // CHUNK
    ncc = ctx // CHUNK
    two = 2 * CHUNK
    lg_chunk = CHUNK.bit_length() - 1

    ri = lax.broadcasted_iota(jnp.int32, (two, two), 0)
    ci = lax.broadcasted_iota(jnp.int32, (two, two), 1)
    same_dir = (ri >> lg_chunk) == (ci >> lg_chunk)
    incl = same_dir & (((ri < CHUNK) & (ri >= ci)) | ((ri >= CHUNK) & (ri <= ci)))
    strict = incl & (ri != ci)
    eye = jnp.where(ri == ci, 1.0, 0.0)
    tri = jnp.where(incl, 1.0, 0.0).astype(BF16)
    level_masks = []
    for lg in range(lg_chunk):
        same_parent = (ri >> (lg + 1)) == (ci >> (lg + 1))
        level_masks.append(jnp.where(same_parent & ((ri >> lg) != (ci >> lg)), 1.0, 0.0))
    top_rows = lax.broadcasted_iota(jnp.int32, (two, LANES), 0) < CHUNK

    def bwd_chunk(t):
        return jnp.where(t < ncc, ncc - 1 - t, nc + ncc - 1 - t)

    def stacked(ref, rf, rb):
        return jnp.concatenate([ref[pl.ds(rf, CHUNK), :], ref[pl.ds(rb, CHUNK), :]], axis=0)

    def chunk_load(t):
        rf = pl.multiple_of(t * CHUNK, CHUNK)
        rb = pl.multiple_of(bwd_chunk(t) * CHUNK, CHUNK)
        beta = jnp.concatenate([gates[0, pl.ds(rf, CHUNK), :], gates[1, pl.ds(rb, CHUNK), :]], axis=0)
        g = jnp.concatenate([gates[2, pl.ds(rf, CHUNK), :], gates[3, pl.ds(rb, CHUNK), :]], axis=0)
        return stacked(kf, rf, rb), stacked(vf, rf, rb), stacked(qf, rf, rb), beta, g

    def chunk_triangle(k, q, beta, g):
        kb = k.astype(BF16)
        kq = jnp.concatenate([kb, q.astype(BF16)], axis=0)
        kk_qk = lax.dot_general(kq, kb, (((1,), (1,)), ((), ())), preferred_element_type=F32)
        g_hi = g.astype(BF16)
        r1 = g - g_hi.astype(F32)
        g_mid = r1.astype(BF16)
        g_lo = (r1 - g_mid.astype(F32)).astype(BF16)
        gc3 = jnp.dot(tri, jnp.concatenate([g_hi, g_mid, g_lo], axis=1), preferred_element_type=F32)
        gc = gc3[:, :LANES] + gc3[:, LANES:2 * LANES] + gc3[:, 2 * LANES:]
        gc_row = jnp.transpose(gc)
        decay = jnp.where(incl, jnp.exp(gc - gc_row), 0.0)
        a = jnp.where(strict, beta * kk_qk[:two] * decay, 0.0)
        return a, (kk_qk[two:] * decay).astype(BF16), gc

    def chunk_finish(hh, t, k, v, q, beta, gc, attn, tmat):
        eg = jnp.exp(gc)
        rhs = jnp.concatenate([v * beta, k * beta * eg], axis=1).astype(BF16)
        sol = jnp.dot(tmat.astype(BF16), rhs, preferred_element_type=F32)
        g_end = jnp.where(top_rows, jnp.broadcast_to(gc[CHUNK - 1:CHUNK, :], (two, LANES)),
                          jnp.broadcast_to(gc[CHUNK:CHUNK + 1, :], (two, LANES)))
        k_tail = k * jnp.exp(g_end - gc)
        qg = (q * eg).astype(BF16)
        kcum = sol[:, DV:].astype(BF16)
        r0 = pl.multiple_of(t * two, two)
        lhs[hh, 0, pl.ds(r0, two), :] = jnp.concatenate([kcum[:CHUNK], qg[:CHUNK]], axis=0)
        lhs[hh, 1, pl.ds(r0, two), :] = jnp.concatenate([kcum[CHUNK:], qg[CHUNK:]], axis=0)
        val[hh, pl.ds(r0, two), :] = sol[:, :DV].astype(BF16)
        att[hh, pl.ds(r0, two), :] = attn
        ktt[hh, pl.ds(r0, two), :] = jnp.transpose(k_tail).astype(BF16)
        d0 = pl.multiple_of(t * SUBLANES, SUBLANES)
        e_end = jnp.exp(g_end)
        dec[hh, 0, pl.ds(d0, SUBLANES), :] = e_end[:SUBLANES]
        dec[hh, 1, pl.ds(d0, SUBLANES), :] = e_end[CHUNK:CHUNK + SUBLANES]

    for hh in range(group):
        head = pl.program_id(1) * group + hh
        cols = slice(hh * LANES, (hh + 1) * LANES)

        kf[0:ctx, :] = _l2n(_short_conv(kc_ref[:, cols], wk_ref[:, cols], pad_scr, ctx))
        vf[0:ctx, :] = _short_conv(vc_ref[:, cols], wv_ref[:, cols], pad_scr, ctx)
        qf[0:ctx, :] = jnp.zeros((ctx, LANES), F32)
        kf[ctx:total, :] = _l2n(_short_conv(k_ref[:, cols], wk_ref[:, cols], pad_scr, seq))
        vf[ctx:total, :] = _short_conv(v_ref[:, cols], wv_ref[:, cols], pad_scr, seq)
        qf[ctx:total, :] = _l2n(_short_conv(q_ref[:, cols], wq_ref[:, cols], pad_scr, seq)) * (DK ** -0.5)

        def gate_cols(src_ref, lo, rows):
            x = src_ref[...]
            lane = lax.broadcasted_iota(jnp.int32, (rows, LANES), 1)
            beta = jax.nn.sigmoid(x)
            g = -jnp.exp(ab_ref[0:1, :]) * _softplus(x + ab_ref[1:2, :])
            for slot, (arr, base) in enumerate(((beta, 0), (beta, HEADS), (g, 2 * HEADS), (g, 3 * HEADS))):
                col = jnp.sum(jnp.where(lane == base + head, arr, 0.0), axis=-1, keepdims=True)
                gates[slot, lo:lo + rows, :] = jnp.broadcast_to(col, (rows, LANES))

        gate_cols(bac_ref, 0, ctx)
        gate_cols(ba_ref, ctx, seq)

        def chunk_body(i, carry, hh=hh):
            steps = [i * unroll + u for u in range(unroll)]
            loaded = [chunk_load(t) for t in steps]
            tris = [chunk_triangle(k, q, beta, g) for (k, v, q, beta, g) in loaded]
            tmats = [eye - a * level_masks[0] for (a, _, _) in tris]
            for m in level_masks[1:]:
                nxt = []
                for (a, _, _), tmat in zip(tris, tmats):
                    cs = (a * m).astype(BF16)
                    tb = tmat.astype(BF16)
                    tc = jnp.dot(tb, cs, preferred_element_type=F32)
                    nxt.append(tmat - jnp.dot(tc.astype(BF16), tb, preferred_element_type=F32))
                tmats = nxt
            for t, (k, v, q, beta, g), (a, attn, gc), tmat in zip(steps, loaded, tris, tmats):
                chunk_finish(hh, t, k, v, q, beta, gc, attn, tmat)
            return carry

        lax.fori_loop(0, nc // unroll, chunk_body, 0)

    def scan_body(t, carry):
        rf = pl.multiple_of(t * CHUNK, CHUNK)
        rb = pl.multiple_of(bwd_chunk(t) * CHUNK, CHUNK)
        r0 = pl.multiple_of(t * two, two)
        d0 = pl.multiple_of(t * SUBLANES, SUBLANES)
        rs = [(jnp.dot(lhs[hh, 0, pl.ds(r0, two), :], carry[2 * hh].astype(BF16), preferred_element_type=F32),
               jnp.dot(lhs[hh, 1, pl.ds(r0, two), :], carry[2 * hh + 1].astype(BF16), preferred_element_type=F32))
              for hh in range(group)]
        v_news = [val[hh, pl.ds(r0, two), :].astype(F32) - jnp.concatenate([r_f[:CHUNK], r_b[:CHUNK]], axis=0)
                  for hh, (r_f, r_b) in enumerate(rs)]
        outs, new_states = [], []
        for hh, ((r_f, r_b), v_new) in enumerate(zip(rs, v_news)):
            zeros = jnp.zeros_like(v_new)
            v_bd = jnp.concatenate([jnp.where(top_rows, v_new, zeros), jnp.where(top_rows, zeros, v_new)],
                                   axis=1).astype(BF16)
            upd = jnp.dot(ktt[hh, pl.ds(r0, two), :], v_bd, preferred_element_type=F32)
            new_states.append(carry[2 * hh] * dec[hh, 0, pl.ds(d0, SUBLANES), :][0:1, :] + upd[:, :DV])
            new_states.append(carry[2 * hh + 1] * dec[hh, 1, pl.ds(d0, SUBLANES), :][0:1, :] + upd[:, DV:])
            outs.append(jnp.concatenate([r_f[CHUNK:], r_b[CHUNK:]], axis=0)
                        + jnp.dot(att[hh, pl.ds(r0, two), :], v_new.astype(BF16), preferred_element_type=F32))
        for hh, o in enumerate(outs):
            osc[hh, pl.ds(rf, CHUNK), :] += o[:CHUNK]
            osc[hh, pl.ds(rb, CHUNK), :] += o[CHUNK:]
        return tuple(new_states)

    osc[...] = jnp.zeros_like(osc)
    zero = jnp.zeros((DK, DV), F32)
    lax.fori_loop(0, nc, scan_body, (zero,) * (2 * group))

    for hh in range(group):
        cols = slice(hh * LANES, (hh + 1) * LANES)
        o = osc[hh, ctx:total, :]
        o = o * lax.rsqrt(jnp.mean(o * o, axis=-1, keepdims=True) + EPS) * gn_ref[...]
        o_ref[:, cols] = (o * _silu(z_ref[:, cols].astype(F32))).astype(o_ref.dtype)


def gated_deltanet(proj, ba, proj_ctx, ba_ctx, conv_kv, conv_q, a_log, dt_bias, gn, *, batch, seq, ctx,
                   unroll=12, group=4):
    total = seq + ctx
    nc = total // CHUNK
    gl = group * LANES
    ng = HEADS // group
    taps = jnp.zeros((SUBLANES, conv_kv.shape[1]), F32).at[:SHORT_CONV].set(conv_kv)
    taps_q = jnp.zeros((SUBLANES, conv_q.shape[1]), F32).at[:SHORT_CONV].set(conv_q)
    ab = jnp.zeros((SUBLANES, LANES), F32)
    ab = ab.at[0, 2 * HEADS:4 * HEADS].set(a_log.reshape(-1)).at[1, 2 * HEADS:4 * HEADS].set(dt_bias.reshape(-1))
    col = lambda base: (lambda b, h: (b, base * ng + h))
    once = pl.Buffered(1)
    return pl.pallas_call(
        functools.partial(_gdn_kernel, seq=seq, ctx=ctx, unroll=unroll, group=group),
        grid=(batch, ng),
        in_specs=[
            pl.BlockSpec((seq, gl), col(COL_K), pipeline_mode=once),
            pl.BlockSpec((seq, gl), col(COL_V), pipeline_mode=once),
            pl.BlockSpec((seq, gl), col(COL_Q), pipeline_mode=once),
            pl.BlockSpec((seq, gl), col(COL_Z), pipeline_mode=once),
            pl.BlockSpec((seq, LANES), lambda b, h: (b, 0), pipeline_mode=once),
            pl.BlockSpec((ctx, gl), col(COL_K)),
            pl.BlockSpec((ctx, gl), col(COL_V)),
            pl.BlockSpec((ctx, LANES), lambda b, h: (b, 0)),
            pl.BlockSpec((SUBLANES, gl), lambda b, h: (0, h)),
            pl.BlockSpec((SUBLANES, gl), lambda b, h: (0, ng + h)),
            pl.BlockSpec((SUBLANES, gl), lambda b, h: (0, h)),
            pl.BlockSpec((SUBLANES, LANES), lambda b, h: (0, 0)),
            pl.BlockSpec((1, LANES), lambda b, h: (0, 0)),
        ],
        out_specs=pl.BlockSpec((seq, gl), lambda b, h: (b, h)),
        out_shape=jax.ShapeDtypeStruct((batch * seq, HEADS * DV), BF16),
        scratch_shapes=[
            pltpu.VMEM((total, LANES), F32),
            pltpu.VMEM((total, LANES), F32),
            pltpu.VMEM((total, LANES), F32),
            pltpu.VMEM((4, total, LANES), F32),
            pltpu.VMEM((seq + 2 * SUBLANES, LANES), F32),
            pltpu.VMEM((group, 2, nc * 2 * CHUNK, LANES), BF16),
            pltpu.VMEM((group, nc * 2 * CHUNK, LANES), BF16),
            pltpu.VMEM((group, nc * 2 * CHUNK, LANES), BF16),
            pltpu.VMEM((group, nc * 2 * CHUNK, LANES), BF16),
            pltpu.VMEM((group, 2, nc * SUBLANES, LANES), F32),
            pltpu.VMEM((group, total, LANES), F32),
        ],
        compiler_params=_params("arbitrary", "arbitrary"),
        name="gated_deltanet",
    )(proj, proj, proj, proj, ba, proj_ctx, proj_ctx, ba_ctx, taps, taps, taps_q, ab, gn.reshape(1, DV))


CONF_ROWS = 128
CONF_HALO = 16


def _conf_kernel(a_ref, g_ref, w_ref, b_ref, o_ref, pad_scr, *, seq):
    cb = a_ref.shape[1]
    u = a_ref[...].astype(F32) * jax.nn.sigmoid(g_ref[...].astype(F32))
    pad_scr[0:CONF_HALO, :] = jnp.zeros((CONF_HALO, cb), F32)
    pad_scr[CONF_HALO:CONF_HALO + seq, :] = u
    pad_scr[CONF_HALO + seq:2 * CONF_HALO + seq, :] = jnp.zeros((CONF_HALO, cb), F32)
    win = CONF_ROWS + 2 * CONF_HALO
    first = CONF_HALO - CONF_KERNEL // 2

    def body(i, carry):
        r0 = pl.multiple_of(i * CONF_ROWS, CONF_ROWS)
        w = pad_scr[pl.ds(r0, win), :]
        acc = jnp.zeros((CONF_ROWS, cb), F32) + b_ref[...]
        for sub in range(SUBLANES):
            shifted = w if sub == 0 else pltpu.roll(w, win - sub, axis=0)
            for j in range(CONF_KERNEL):
                off = first + j
                if off % SUBLANES == sub:
                    base = off - sub
                    acc = acc + shifted[base:base + CONF_ROWS, :] * w_ref[j:j + 1, :]
        o_ref[pl.ds(r0, CONF_ROWS), :] = acc.astype(o_ref.dtype)
        return carry

    lax.fori_loop(0, seq // CONF_ROWS, body, 0)


def conformer_conv(proj, conf_dw, conf_dw_b, *, batch, seq, cb=256):
    c = conf_dw.shape[1]
    nb = c // cb
    taps = jnp.zeros((32, c), F32).at[:CONF_KERNEL].set(conf_dw)
    return pl.pallas_call(
        functools.partial(_conf_kernel, seq=seq),
        grid=(batch, nb),
        in_specs=[pl.BlockSpec((seq, cb), lambda b, j: (b, COL_GLU_A * nb + j)),
                  pl.BlockSpec((seq, cb), lambda b, j: (b, COL_GLU_G * nb + j)),
                  pl.BlockSpec((32, cb), lambda b, j: (0, j)),
                  pl.BlockSpec((1, cb), lambda b, j: (0, j))],
        out_specs=pl.BlockSpec((seq, cb), lambda b, j: (b, j)),
        out_shape=jax.ShapeDtypeStruct((batch * seq, c), BF16),
        scratch_shapes=[pltpu.VMEM((seq + 2 * CONF_HALO, cb), F32)],
        compiler_params=_params("arbitrary", "arbitrary"),
        name="conformer_conv",
    )(proj, proj, taps, conf_dw_b.reshape(1, c))


def _mixer_out_kernel(og_ref, uc_ref, ga_ref, gb_ref, x_ref, pos_ref, gm_ref, shf_ref, scf_ref,
                      gpost_ref, gpre_ref, lng_ref, lnb_ref, wa_ref, wb_ref, wo_ref, wr_ref, br_ref,
                      x2_ref, h2_ref, ti_ref, tw_ref, cnt_ref, cnt_scr, *, sub):
    parts = [slice(r, r + sub) for r in range(0, og_ref.shape[0], sub)]

    def rms(v):
        return v * lax.rsqrt(jnp.mean(v * v, axis=-1, keepdims=True) + EPS)

    def layer_norm_silu(uc):
        mu = jnp.mean(uc, axis=-1, keepdims=True)
        var = jnp.mean(jnp.square(uc - mu), axis=-1, keepdims=True)
        return _silu((uc - mu) * lax.rsqrt(var + EPS) * lng_ref[...] + lnb_ref[...]).astype(BF16)

    us = [layer_norm_silu(uc_ref[p, :].astype(F32)) for p in parts]
    yas = [jnp.dot(og_ref[p, :], wa_ref[...], preferred_element_type=F32) for p in parts]
    ybs = [jnp.dot(u, wb_ref[...], preferred_element_type=F32) for u in us]
    merged = [(jax.nn.sigmoid(ga_ref[p, :].astype(F32)) * ya
               + jax.nn.sigmoid(gb_ref[p, :].astype(F32)) * yb).astype(BF16) for p, ya, yb in zip(parts, yas, ybs)]
    ys = [jnp.dot(m, wo_ref[...], preferred_element_type=F32) for m in merged]
    x2s = [x_ref[p, :] + pos_ref[p, :] + gm_ref[0] * (rms(y) * gpost_ref[...]) for p, y in zip(parts, ys)]
    h2s = [(rms(x2) * gpre_ref[...] * (1.0 + scf_ref[0]) + shf_ref[0]).astype(BF16) for x2 in x2s]
    for p, x2, h2 in zip(parts, x2s, h2s):
        x2_ref[p, :] = x2
        h2_ref[p, :] = h2
    all_logits = [jnp.dot(h2, wr_ref[...], preferred_element_type=F32) + br_ref[...] for h2 in h2s]
    lane = lax.broadcasted_iota(jnp.int32, (sub, LANES), 1).astype(F32)
    ri = lax.broadcasted_iota(jnp.int32, (sub, sub), 0)
    ci = lax.broadcasted_iota(jnp.int32, (sub, sub), 1)
    before = jnp.where(ri > ci, 1.0, 0.0).astype(BF16)

    @pl.when(pl.program_id(0) == 0)
    def _():
        cnt_scr[...] = jnp.zeros_like(cnt_scr)

    routed = []
    for logits in all_logits:
        live = jnp.where(lane < N_EXPERTS, logits, -jnp.inf)
        top_v, top_i = [], []
        for _ in range(TOP_K):
            m = jnp.max(live, axis=-1, keepdims=True)
            idx = jnp.min(jnp.where(live == m, lane, float(LANES)), axis=-1, keepdims=True)
            top_v.append(m)
            top_i.append(idx)
            live = jnp.where(lane == idx, -jnp.inf, live)
        ex = [jnp.exp(v - top_v[0]) for v in top_v]
        denom = ex[0] + ex[1] + ex[2] + ex[3]
        picked = jnp.zeros((sub, LANES), F32)
        for k in range(TOP_K):
            picked = jnp.where(lane == top_i[k], 1.0, picked)
        routed.append((top_i, [e / denom for e in ex], picked,
                       jnp.dot(before, picked.astype(BF16), preferred_element_type=F32)))

    count = cnt_scr[...]
    for j, (p, (top_i, top_w, picked, within)) in enumerate(zip(parts, routed)):
        prefix = within + count[0:1, :]
        ti = jnp.zeros((sub, LANES), F32)
        tw = jnp.zeros((sub, LANES), F32)
        for k in range(TOP_K):
            rank = jnp.sum(jnp.where(lane == top_i[k], prefix, 0.0), axis=-1, keepdims=True)
            ti = jnp.where(lane == k, top_i[k], ti)
            ti = jnp.where(lane == TOP_K + k, rank, ti)
            tw = jnp.where(lane == k, top_w[k], tw)
        ti_ref[:, p] = jnp.transpose(ti)[:SUBLANES, :].astype(jnp.int32)
        tw_ref[:, p] = jnp.transpose(tw)[:SUBLANES, :]
        count = count + jnp.sum(picked, axis=0, keepdims=True)
        cnt_ref[j] = count
    cnt_scr[...] = count


def mixer_output(og, uc, proj, x, pos, mod, mod_row_of_tile, g_post, g_pre_f, ln_g, ln_b,
                 w_a, w_b, w_o, w_r, b_r, *, tm, sub):
    n, d = x.shape
    pos_tiles = pos.shape[0] // tm
    row = lambda i: (i, 0)
    fixed = lambda i: (0, 0)
    modspec = lambda k: pl.BlockSpec((1, 1, d), lambda i: (mod_row_of_tile(i), 0, k))
    vec = pl.BlockSpec((1, d), fixed)
    mat = pl.BlockSpec((d, d), fixed)
    return pl.pallas_call(
        functools.partial(_mixer_out_kernel, sub=sub),
        grid=(n // tm,),
        in_specs=[pl.BlockSpec((tm, d), row), pl.BlockSpec((tm, d), row),
                  pl.BlockSpec((tm, d), lambda i: (i, COL_GATE_A)),
                  pl.BlockSpec((tm, d), lambda i: (i, COL_GATE_B)),
                  pl.BlockSpec((tm, d), row),
                  pl.BlockSpec((tm, d), lambda i: (i % pos_tiles, 0)),
                  modspec(2), modspec(3), modspec(4),
                  vec, vec, vec, vec, mat, mat, mat,
                  pl.BlockSpec((d, LANES), fixed), pl.BlockSpec((1, LANES), fixed)],
        out_specs=[pl.BlockSpec((tm, d), row), pl.BlockSpec((tm, d), row),
                   pl.BlockSpec((SUBLANES, tm), lambda i: (0, i)), pl.BlockSpec((SUBLANES, tm), lambda i: (0, i)),
                   pl.BlockSpec((tm // sub, SUBLANES, LANES), lambda i: (i, 0, 0))],
        out_shape=[jax.ShapeDtypeStruct((n, d), F32), jax.ShapeDtypeStruct((n, d), BF16),
                   jax.ShapeDtypeStruct((SUBLANES, n), jnp.int32), jax.ShapeDtypeStruct((SUBLANES, n), F32),
                   jax.ShapeDtypeStruct((n // sub, SUBLANES, LANES), F32)],
        scratch_shapes=[pltpu.VMEM((SUBLANES, LANES), F32)],
        compiler_params=_params("arbitrary"),
        name="mixer_output",
    )(og, uc, proj, proj, x, pos, mod, mod, mod,
      g_post.reshape(1, d), g_pre_f.reshape(1, d), ln_g.reshape(1, d), ln_b.reshape(1, d),
      w_a, w_b, w_o, w_r, b_r)


EXPERT_BLOCK = 256
MOE_CHUNK = 512
GATHER_SUB = 256
COMBINE_GROUP = 8


def _expert_kernel(bexp_ref, slo_ref, shi_ref, nused_ref,
                   h2_ref, dest_ref, wt_ref, wgu_ref, bgu_ref, wd_ref, bd_ref, o_ref,
                   acc, rw_acc, wgu_bf, wd_bf):
    b = pl.program_id(0)
    expert = bexp_ref[b]

    @pl.when((b == 0) | (expert != bexp_ref[jnp.maximum(b - 1, 0)]))
    def _():
        wgu_bf[...] = wgu_ref[0].astype(BF16)
        wd_bf[...] = wd_ref[0].astype(BF16)

    @pl.when(b < nused_ref[0])
    def _():
        acc[...] = jnp.zeros_like(acc)
        rw_acc[...] = jnp.zeros_like(rw_acc)
        row = lax.broadcasted_iota(jnp.int32, (EXPERT_BLOCK, GATHER_SUB), 0)
        s_lo, s_hi = slo_ref[b], shi_ref[b]
        last_sub = h2_ref.shape[0] // GATHER_SUB - 1

        def select(s, live):
            t0 = pl.multiple_of(s * GATHER_SUB, GATHER_SUB)
            local = dest_ref[:, pl.ds(t0, GATHER_SUB)] - b * EXPERT_BLOCK
            if live is not True:
                local = jnp.where(live, local, -1)
            inside = (local >= 0) & (local < EXPERT_BLOCK)
            hit_row = jnp.sum(jnp.where(inside, local + 1, 0), axis=0, keepdims=True) - 1
            hit_w = jnp.sum(jnp.where(inside, wt_ref[:, pl.ds(t0, GATHER_SUB)], 0.0), axis=0, keepdims=True)
            hit = row == hit_row
            return (jnp.where(hit, 1.0, 0.0).astype(BF16), jnp.where(hit, hit_w, 0.0),
                    h2_ref[pl.ds(t0, GATHER_SUB), :])

        def gather(i, carry):
            s0 = s_lo + 2 * i
            p0, w0, x0 = select(s0, True)
            p1, w1, x1 = select(jnp.minimum(s0 + 1, last_sub), s0 + 1 <= s_hi)
            acc[...] += jnp.dot(jnp.concatenate([p0, p1], axis=1), jnp.concatenate([x0, x1], axis=0),
                                preferred_element_type=F32)
            rw_acc[...] += jnp.sum(w0 + w1, axis=1, keepdims=True)
            return carry

        lax.fori_loop(0, (s_hi - s_lo + 2) // 2, gather, 0)
        gu = jnp.dot(acc[...].astype(BF16), wgu_bf[...], preferred_element_type=F32) + bgu_ref[0]
        dff = gu.shape[1] // 2
        gl = jnp.minimum(gu[:, :dff], SWIGLU_LIMIT)
        lin = jnp.clip(gu[:, dff:], -SWIGLU_LIMIT, SWIGLU_LIMIT)
        act = (gl * jax.nn.sigmoid(SWIGLU_ALPHA * gl) * (lin + 1.0)).astype(BF16)
        y = jnp.dot(act, wd_bf[...], preferred_element_type=F32) + bd_ref[0]
        o_ref[...] = (y * rw_acc[...]).astype(o_ref.dtype)

    @pl.when(b >= nused_ref[0])
    def _():
        o_ref[...] = jnp.zeros_like(o_ref)


def expert_blocks(h2, dest_t, wt_t, block_expert, sub_lo, sub_hi, n_used, w_gate_up, b_gate_up, w_down, b_down):
    n, d = h2.shape
    e, _, f2 = w_gate_up.shape
    n_blocks = block_expert.shape[0]
    whole = lambda b, be, lo, hi, nu: (0, 0)
    by_expert = lambda b, be, lo, hi, nu: (be[b], 0, 0)
    once = pl.Buffered(1)
    grid_spec = pltpu.PrefetchScalarGridSpec(
        num_scalar_prefetch=4,
        grid=(n_blocks,),
        in_specs=[pl.BlockSpec((n, d), whole, pipeline_mode=once),
                  pl.BlockSpec((SUBLANES, n), whole, pipeline_mode=once),
                  pl.BlockSpec((SUBLANES, n), whole, pipeline_mode=once),
                  pl.BlockSpec((1, d, f2), by_expert, pipeline_mode=once),
                  pl.BlockSpec((1, 1, f2), by_expert),
                  pl.BlockSpec((1, f2 // 2, d), by_expert, pipeline_mode=once),
                  pl.BlockSpec((1, 1, d), by_expert)],
        out_specs=pl.BlockSpec((EXPERT_BLOCK, d), lambda b, be, lo, hi, nu: (b, 0)),
        scratch_shapes=[pltpu.VMEM((EXPERT_BLOCK, d), F32), pltpu.VMEM((EXPERT_BLOCK, 1), F32),
                        pltpu.VMEM((d, f2), BF16), pltpu.VMEM((f2 // 2, d), BF16)],
    )
    return pl.pallas_call(
        _expert_kernel,
        grid_spec=grid_spec,
        out_shape=jax.ShapeDtypeStruct((n_blocks * EXPERT_BLOCK, d), BF16),
        compiler_params=_params("arbitrary"),
        name="expert_blocks",
    )(block_expert, sub_lo, sub_hi, n_used, h2, dest_t, wt_t, w_gate_up, b_gate_up.reshape(e, 1, f2), w_down,
      b_down.reshape(e, 1, d))


def _combine_kernel(coff_ref, cblk_ref, yb_hbm, dest_ref, x2_ref, gf_ref, g_ref, o_ref, slab, sem, acc):
    c = pl.program_id(0)
    start = coff_ref[c]
    count = coff_ref[c + 1] - start
    n_groups = (count + COMBINE_GROUP - 1) // COMBINE_GROUP

    def block_of(g, s):
        return cblk_ref[start + jnp.minimum(g * COMBINE_GROUP + s, count - 1)]

    def copies(g, buf):
        return [pltpu.make_async_copy(
            yb_hbm.at[pl.ds(pl.multiple_of(block_of(g, s) * MOE_BLOCK, MOE_BLOCK), MOE_BLOCK), :],
            slab.at[buf, pl.ds(s * MOE_BLOCK, MOE_BLOCK), :], sem.at[buf]) for s in range(COMBINE_GROUP)]

    for cp in copies(0, 0):
        cp.start()
    acc[...] = jnp.zeros_like(acc)
    chunk = x2_ref.shape[0]
    lane = lax.broadcasted_iota(jnp.int32, (chunk, MOE_BLOCK), 1)
    dest_rows = dest_ref[...]
    n_rows = yb_hbm.shape[0]

    def group(g, carry):
        buf = g % 2

        @pl.when(g + 1 < n_groups)
        def _():
            for cp in copies(g + 1, 1 - buf):
                cp.start()

        for cp in copies(g, buf):
            cp.wait()
        hit_rows = []
        for s in range(COMBINE_GROUP):
            live = g * COMBINE_GROUP + s < count
            local = dest_rows - jnp.where(live, block_of(g, s) * MOE_BLOCK, n_rows)
            inside = (local >= 0) & (local < MOE_BLOCK)
            hit_rows.append(jnp.sum(jnp.where(inside, local + 1, 0), axis=0, keepdims=True) - 1)
        hit_rows.append(jnp.full((LANES - COMBINE_GROUP, chunk), -1, jnp.int32))
        hit_row = jnp.transpose(jnp.concatenate(hit_rows, axis=0).astype(F32)).astype(jnp.int32)
        pieces = [jnp.where(lane == hit_row[:, s:s + 1], 1.0, 0.0).astype(BF16) for s in range(COMBINE_GROUP)]
        onehot = jnp.concatenate(pieces, axis=1)
        acc[...] += jnp.dot(onehot, slab[buf], preferred_element_type=F32)
        return carry

    lax.fori_loop(0, n_groups, group, 0)
    y = acc[...]
    yn = y * lax.rsqrt(jnp.mean(y * y, axis=-1, keepdims=True) + EPS) * g_ref[...]
    o_ref[...] = x2_ref[...] + gf_ref[0] * yn


def combine_residual(yb, dest, x2, chunk_off, chunk_blocks, mod, mod_row_of_chunk, g_post_f):
    n, d = x2.shape
    chunk_row = lambda c, off, blk: (c, 0)
    grid_spec = pltpu.PrefetchScalarGridSpec(
        num_scalar_prefetch=2,
        grid=(n // MOE_CHUNK,),
        in_specs=[pl.BlockSpec(memory_space=pl.ANY),
                  pl.BlockSpec((SUBLANES, MOE_CHUNK), lambda c, off, blk: (0, c)),
                  pl.BlockSpec((MOE_CHUNK, d), chunk_row),
                  pl.BlockSpec((1, 1, d), lambda c, off, blk: (mod_row_of_chunk(c), 0, 5)),
                  pl.BlockSpec((1, d), lambda c, off, blk: (0, 0))],
        out_specs=pl.BlockSpec((MOE_CHUNK, d), chunk_row),
        scratch_shapes=[pltpu.VMEM((2, COMBINE_GROUP * MOE_BLOCK, d), BF16),
                        pltpu.SemaphoreType.DMA((2,)),
                        pltpu.VMEM((MOE_CHUNK, d), F32)],
    )
    return pl.pallas_call(
        _combine_kernel,
        grid_spec=grid_spec,
        out_shape=jax.ShapeDtypeStruct((n, d), F32),
        compiler_params=_params("arbitrary"),
        name="combine_residual",
    )(chunk_off, chunk_blocks, yb, dest, x2, mod, g_post_f.reshape(1, d))


def _grid_pos_embedding(rows, d):
    row = jnp.repeat(jnp.arange(rows, dtype=F32), GRID_W)
    col = jnp.tile(jnp.arange(GRID_W, dtype=F32), rows)
    quarter = d // 4
    omega = POS_BASE ** (-jnp.arange(quarter, dtype=F32) / quarter)

    def emb(p):
        ang = p[:, None] * omega[None, :]
        return jnp.concatenate([jnp.sin(ang), jnp.cos(ang)], axis=-1)

    return jnp.concatenate([emb(row), emb(col)], axis=-1)


def _count_le(sorted_vals, queries):
    return jnp.sum(sorted_vals <= queries[:, None], axis=1).astype(jnp.int32)


def _moe_plan(ti, cnt_after, n_tok, tm):
    assert tm == GATHER_SUB
    sub_after = cnt_after[:, 0, :N_EXPERTS].astype(jnp.int32)
    counts = sub_after[-1]
    padded = (counts + EXPERT_BLOCK - 1) // EXPERT_BLOCK * EXPERT_BLOCK
    padded_end = jnp.cumsum(padded)
    padded_start = padded_end - padded
    experts = jnp.arange(N_EXPERTS, dtype=jnp.int32)[:, None, None]
    start_of = jnp.sum(jnp.where(ti[None, :TOP_K] == experts, padded_start[:, None, None], 0), axis=0)
    dest_t = jnp.concatenate([start_of + ti[TOP_K:2 * TOP_K],
                              jnp.full((SUBLANES - TOP_K, n_tok), -1, jnp.int32)], axis=0)
    n_blocks = n_tok * TOP_K // EXPERT_BLOCK + N_EXPERTS
    blocks = jnp.arange(n_blocks, dtype=jnp.int32)
    block_expert = jnp.minimum(_count_le(padded_end[None, :], blocks * EXPERT_BLOCK), N_EXPERTS - 1)
    n_used = (padded_end[-1:] // EXPERT_BLOCK).astype(jnp.int32)
    r_lo = (blocks - padded_start[block_expert] // EXPERT_BLOCK) * EXPERT_BLOCK
    r_hi = jnp.minimum(counts[block_expert], r_lo + EXPERT_BLOCK) - 1
    through = sub_after.T[block_expert]
    n_sub = sub_after.shape[0]
    sub_lo = jnp.minimum(_count_le(through, r_lo), n_sub - 1)
    sub_hi = jnp.minimum(_count_le(through, r_hi), n_sub - 1)
    nch = n_tok // MOE_CHUNK
    per_chunk = MOE_CHUNK // tm
    cb_after = sub_after[per_chunk - 1::per_chunk]
    cb_before = jnp.concatenate([jnp.zeros((1, N_EXPERTS), jnp.int32), cb_after[:-1]], axis=0)
    b_lo = ((padded_start[None, :] + cb_before) // MOE_BLOCK).reshape(-1)
    b_hi = ((padded_start[None, :] + cb_after - 1) // MOE_BLOCK).reshape(-1)
    n_it = jnp.where((cb_after > cb_before).reshape(-1), b_hi - b_lo + 1, 0)
    off_end = jnp.cumsum(n_it)
    off = off_end - n_it
    n_items = n_blocks * (EXPERT_BLOCK // MOE_BLOCK) + N_EXPERTS * (nch - 1)
    w = jnp.minimum(jnp.arange(n_items, dtype=jnp.int32), off_end[-1] - 1)
    cell = jnp.minimum(_count_le(off_end[None, :], w), n_it.shape[0] - 1)
    chunk_blocks = (b_lo[cell] + w - off[cell]).astype(jnp.int32)
    chunk_off = jnp.concatenate([jnp.zeros((1,), jnp.int32), off_end[N_EXPERTS - 1::N_EXPERTS]]).astype(jnp.int32)
    return dest_t, block_expert, sub_lo, sub_hi, n_used, chunk_off, chunk_blocks


def kernel(x, c, ctx, c_ctx, w_mod, b_mod, g_pre_mix, g_post_mix, g_pre_ffn, g_post_ffn, w_in, conv_kv,
           conv_q, a_log, dt_bias, gdn_norm_g, w_proj_a, conf_dw, conf_dw_b, conf_ln_g, conf_ln_b,
           w_proj_b, w_out, w_router, b_router, w_gate_up, b_gate_up, w_down, b_down):
    batch, seq, d = x.shape
    ctx_len = ctx.shape[1]
    n_tok = batch * seq
    gw = HEADS * DV
    beta_off = HEADS * DK + gw
    state_cols = beta_off + 4 * HEADS
    q_off = state_cols

    mod_rows = 2 * SUBLANES
    cond = jnp.zeros((mod_rows, d), F32).at[:batch].set(c).at[batch].set(c_ctx)
    mod = modulation(cond, w_mod[0], b_mod[0]).reshape(mod_rows, 1, 6 * d)

    w = w_in[0]
    w_main = jnp.concatenate([w[:, :beta_off], w[:, q_off:]], axis=1).astype(BF16)
    w_ba = jnp.zeros((d, LANES), F32).at[:, :4 * HEADS].set(w[:, beta_off:state_cols]).astype(BF16)

    pos = _grid_pos_embedding(seq // GRID_W, d)
    x_flat = x.reshape(n_tok, d)
    tm = 1024
    tiles_per_seq = seq // tm
    proj, ba = input_projection(x_flat, pos, mod, lambda i: i // tiles_per_seq, g_pre_mix[0],
                                w_main, w_ba, tm=tm, tn=2048)
    proj_ctx, ba_ctx = input_projection(ctx.reshape(batch * ctx_len, d), None, mod, lambda i: batch,
                                        g_pre_mix[0], w_main[:, :beta_off], w_ba, tm=ctx_len, tn=1024)

    og = gated_deltanet(proj, ba, proj_ctx, ba_ctx, conv_kv[0], conv_q[0], a_log[0], dt_bias[0],
                        gdn_norm_g[0], batch=batch, seq=seq, ctx=ctx_len)
    uc = conformer_conv(proj, conf_dw[0], conf_dw_b[0], batch=batch, seq=seq)

    tm2 = 512
    w_r = jnp.zeros((d, LANES), F32).at[:, :N_EXPERTS].set(w_router[0]).astype(BF16)
    b_r = jnp.zeros((1, LANES), F32).at[0, :N_EXPERTS].set(b_router[0])
    x2, h2, top_i, top_w, cnt_after = mixer_output(
        og, uc, proj, x_flat, pos, mod, lambda i: i // (seq // tm2), g_post_mix[0], g_pre_ffn[0],
        conf_ln_g[0], conf_ln_b[0], w_proj_a[0].astype(BF16), w_proj_b[0].astype(BF16),
        w_out[0].astype(BF16), w_r, b_r, tm=tm2, sub=GATHER_SUB)

    dest_t, block_expert, sub_lo, sub_hi, n_used, chunk_off, chunk_blocks = _moe_plan(
        top_i, cnt_after, n_tok, GATHER_SUB)
    yb = expert_blocks(h2, dest_t, top_w, block_expert, sub_lo, sub_hi, n_used, w_gate_up[0], b_gate_up[0],
                       w_down[0], b_down[0])
    out = combine_residual(yb, dest_t, x2, chunk_off, chunk_blocks, mod, lambda c: c // (seq // MOE_CHUNK),
                           g_post_ffn[0])
    return out.reshape(batch, seq, d)
```

```python
import functools

import jax
import jax.numpy as jnp
from jax import lax
from jax.experimental import pallas as pl
from jax.experimental.pallas import tpu as pltpu

F32 = jnp.float32
BF16 = jnp.bfloat16

D_MODEL = 1024
GRID_W = 64
HEADS = 8
DK = 128
DV = 128
SHORT_CONV = 5
CHUNK = 64
CONF_KERNEL = 31
N_EXPERTS = 32
TOP_K = 4
SWIGLU_LIMIT = 7.0
SWIGLU_ALPHA = 1.702
MOE_BLOCK = 128
EPS = 1e-6
POS_BASE = 10000.0

LANES = 128
SUBLANES = 8
VMEM_LIMIT = 56 * 1024 * 1024

COL_K, COL_V, COL_Q, COL_Z, COL_GLU_A, COL_GLU_G, COL_GATE_A, COL_GATE_B = range(8)


def _params(*sem):
    return pltpu.CompilerParams(dimension_semantics=sem, vmem_limit_bytes=VMEM_LIMIT)


def _mod_kernel(c_ref, w_ref, b_ref, o_ref):
    c = c_ref[...]
    s = c * jax.nn.sigmoid(c)
    o_ref[...] = jnp.dot(s, w_ref[...], preferred_element_type=F32,
                         precision=lax.Precision.HIGHEST) + b_ref[...]


def modulation(cond, w_mod, b_mod):
    r, d = cond.shape
    n = w_mod.shape[1]
    tn = 1024
    return pl.pallas_call(
        _mod_kernel,
        grid=(n // tn,),
        in_specs=[pl.BlockSpec((r, d), lambda j: (0, 0)),
                  pl.BlockSpec((d, tn), lambda j: (0, j)),
                  pl.BlockSpec((1, tn), lambda j: (0, j))],
        out_specs=pl.BlockSpec((r, tn), lambda j: (0, j)),
        out_shape=jax.ShapeDtypeStruct((r, n), F32),
        compiler_params=_params("arbitrary"),
        name="modulation",
    )(cond, w_mod, b_mod.reshape(1, n))


def _inproj_kernel(*refs, has_pos):
    if has_pos:
        x_ref, pos_ref, sh_ref, sc_ref, g_ref, w_ref, wba_ref, o_ref, ba_ref, h_scr = refs
    else:
        x_ref, sh_ref, sc_ref, g_ref, w_ref, wba_ref, o_ref, ba_ref, h_scr = refs

    @pl.when(pl.program_id(1) == 0)
    def _():
        x = x_ref[...]
        if has_pos:
            x = x + pos_ref[...]
        y = x * lax.rsqrt(jnp.mean(x * x, axis=-1, keepdims=True) + EPS) * g_ref[...]
        h = (y * (1.0 + sc_ref[0]) + sh_ref[0]).astype(BF16)
        h_scr[...] = h
        ba_ref[...] = jnp.dot(h, wba_ref[...], preferred_element_type=F32)

    o_ref[...] = jnp.dot(h_scr[...], w_ref[...], preferred_element_type=F32).astype(o_ref.dtype)


def input_projection(x, pos, mod, mod_row_of_tile, g_pre, w_main, w_ba, *, tm, tn):
    n, d = x.shape
    w = w_main.shape[1]
    has_pos = pos is not None
    in_specs = [pl.BlockSpec((tm, d), lambda i, j: (i, 0))]
    args = [x]
    if has_pos:
        pos_tiles = pos.shape[0] // tm
        in_specs.append(pl.BlockSpec((tm, d), lambda i, j: (i % pos_tiles, 0)))
        args.append(pos)
    in_specs += [
        pl.BlockSpec((1, 1, d), lambda i, j: (mod_row_of_tile(i), 0, 0)),
        pl.BlockSpec((1, 1, d), lambda i, j: (mod_row_of_tile(i), 0, 1)),
        pl.BlockSpec((1, d), lambda i, j: (0, 0)),
        pl.BlockSpec((d, tn), lambda i, j: (0, j)),
        pl.BlockSpec((d, LANES), lambda i, j: (0, 0)),
    ]
    args += [mod, mod, g_pre.reshape(1, d), w_main, w_ba]
    return pl.pallas_call(
        functools.partial(_inproj_kernel, has_pos=has_pos),
        grid=(n // tm, w // tn),
        in_specs=in_specs,
        out_specs=[pl.BlockSpec((tm, tn), lambda i, j: (i, j)),
                   pl.BlockSpec((tm, LANES), lambda i, j: (i, 0))],
        out_shape=[jax.ShapeDtypeStruct((n, w), BF16), jax.ShapeDtypeStruct((n, LANES), F32)],
        scratch_shapes=[pltpu.VMEM((tm, d), BF16)],
        compiler_params=_params("arbitrary", "arbitrary"),
        name="input_projection_pos" if has_pos else "input_projection_ctx",
    )(*args)


def _silu(x):
    return x * jax.nn.sigmoid(x)


def _softplus(x):
    return jnp.maximum(x, 0.0) + jnp.log(1.0 + jnp.exp(-jnp.abs(x)))


def _short_conv(src, taps, pad_scr, rows):
    half = SHORT_CONV // 2
    pad_scr[0:SUBLANES, :] = jnp.zeros((SUBLANES, LANES), F32)
    pad_scr[SUBLANES:SUBLANES + rows, :] = src.astype(F32)
    pad_scr[SUBLANES + rows:2 * SUBLANES + rows, :] = jnp.zeros((SUBLANES, LANES), F32)
    acc = None
    for j in range(SHORT_CONV):
        off = SUBLANES + j - half
        term = pad_scr[off:off + rows, :] * taps[j:j + 1, :]
        acc = term if acc is None else acc + term
    return _silu(acc)


def _l2n(x):
    return x * lax.rsqrt(jnp.sum(x * x, axis=-1, keepdims=True) + EPS)


def _gdn_kernel(k_ref, v_ref, q_ref, z_ref, ba_ref, kc_ref, vc_ref, bac_ref,
                wk_ref, wv_ref, wq_ref, ab_ref, gn_ref, o_ref,
                kf, vf, qf, gates, pad_scr, lhs, val, att, ktt, dec, osc, *, seq, ctx, unroll, group):
    total = ctx + seq
    nc = total // CHUNK
    ncc = ctx // CHUNK
    two = 2 * CHUNK
    lg_chunk = CHUNK.bit_length() - 1

    ri = lax.broadcasted_iota(jnp.int32, (two, two), 0)
    ci = lax.broadcasted_iota(jnp.int32, (two, two), 1)
    same_dir = (ri >> lg_chunk) == (ci >> lg_chunk)
    incl = same_dir & (((ri < CHUNK) & (ri >= ci)) | ((ri >= CHUNK) & (ri <= ci)))
    strict = incl & (ri != ci)
    eye = jnp.where(ri == ci, 1.0, 0.0)
    tri = jnp.where(incl, 1.0, 0.0).astype(BF16)
    level_masks = []
    for lg in range(lg_chunk):
        same_parent = (ri >> (lg + 1)) == (ci >> (lg + 1))
        level_masks.append(jnp.where(same_parent & ((ri >> lg) != (ci >> lg)), 1.0, 0.0))
    top_rows = lax.broadcasted_iota(jnp.int32, (two, LANES), 0) < CHUNK

    def bwd_chunk(t):
        return jnp.where(t < ncc, ncc - 1 - t, nc + ncc - 1 - t)

    def stacked(ref, rf, rb):
        return jnp.concatenate([ref[pl.ds(rf, CHUNK), :], ref[pl.ds(rb, CHUNK), :]], axis=0)

    def chunk_load(t):
        rf = pl.multiple_of(t * CHUNK, CHUNK)
        rb = pl.multiple_of(bwd_chunk(t) * CHUNK, CHUNK)
        beta = jnp.concatenate([gates[0, pl.ds(rf, CHUNK), :], gates[1, pl.ds(rb, CHUNK), :]], axis=0)
        g = jnp.concatenate([gates[2, pl.ds(rf, CHUNK), :], gates[3, pl.ds(rb, CHUNK), :]], axis=0)
        return stacked(kf, rf, rb), stacked(vf, rf, rb), stacked(qf, rf, rb), beta, g

    def chunk_triangle(k, q, beta, g):
        kb = k.astype(BF16)
        kq = jnp.concatenate([kb, q.astype(BF16)], axis=0)
        kk_qk = lax.dot_general(kq, kb, (((1,), (1,)), ((), ())), preferred_element_type=F32)
        g_hi = g.astype(BF16)
        r1 = g - g_hi.astype(F32)
        g_mid = r1.astype(BF16)
        g_lo = (r1 - g_mid.astype(F32)).astype(BF16)
        gc3 = jnp.dot(tri, jnp.concatenate([g_hi, g_mid, g_lo], axis=1), preferred_element_type=F32)
        gc = gc3[:, :LANES] + gc3[:, LANES:2 * LANES] + gc3[:, 2 * LANES:]
        gc_row = jnp.transpose(gc)
        decay = jnp.where(incl, jnp.exp(gc - gc_row), 0.0)
        a = jnp.where(strict, beta * kk_qk[:two] * decay, 0.0)
        return a, (kk_qk[two:] * decay).astype(BF16), gc

    def chunk_finish(hh, t, k, v, q, beta, gc, attn, tmat):
        eg = jnp.exp(gc)
        rhs = jnp.concatenate([v * beta, k * beta * eg], axis=1).astype(BF16)
        sol = jnp.dot(tmat.astype(BF16), rhs, preferred_element_type=F32)
        g_end = jnp.where(top_rows, jnp.broadcast_to(gc[CHUNK - 1:CHUNK, :], (two, LANES)),
                          jnp.broadcast_to(gc[CHUNK:CHUNK + 1, :], (two, LANES)))
        k_tail = k * jnp.exp(g_end - gc)
        qg = (q * eg).astype(BF16)
        kcum = sol[:, DV:].astype(BF16)
        r0 = pl.multiple_of(t * two, two)
        lhs[hh, 0, pl.ds(r0, two), :] = jnp.concatenate([kcum[:CHUNK], qg[:CHUNK]], axis=0)
        lhs[hh, 1, pl.ds(r0, two), :] = jnp.concatenate([kcum[CHUNK:], qg[CHUNK:]], axis=0)
        val[hh, pl.ds(r0, two), :] = sol[:, :DV].astype(BF16)
        att[hh, pl.ds(r0, two), :] = attn
        ktt[hh, pl.ds(r0, two), :] = jnp.transpose(k_tail).astype(BF16)
        d0 = pl.multiple_of(t * SUBLANES, SUBLANES)
        e_end = jnp.exp(g_end)
        dec[hh, 0, pl.ds(d0, SUBLANES), :] = e_end[:SUBLANES]
        dec[hh, 1, pl.ds(d0, SUBLANES), :] = e_end[CHUNK:CHUNK + SUBLANES]

    for hh in range(group):
        head = pl.program_id(1) * group + hh
        cols = slice(hh * LANES, (hh + 1) * LANES)

        kf[0:ctx, :] = _l2n(_short_conv(kc_ref[:, cols], wk_ref[:, cols], pad_scr, ctx))
        vf[0:ctx, :] = _short_conv(vc_ref[:, cols], wv_ref[:, cols], pad_scr, ctx)
        qf[0:ctx, :] = jnp.zeros((ctx, LANES), F32)
        kf[ctx:total, :] = _l2n(_short_conv(k_ref[:, cols], wk_ref[:, cols], pad_scr, seq))
        vf[ctx:total, :] = _short_conv(v_ref[:, cols], wv_ref[:, cols], pad_scr, seq)
        qf[ctx:total, :] = _l2n(_short_conv(q_ref[:, cols], wq_ref[:, cols], pad_scr, seq)) * (DK ** -0.5)

        def gate_cols(src_ref, lo, rows):
            x = src_ref[...]
            lane = lax.broadcasted_iota(jnp.int32, (rows, LANES), 1)
            beta = jax.nn.sigmoid(x)
            g = -jnp.exp(ab_ref[0:1, :]) * _softplus(x + ab_ref[1:2, :])
            for slot, (arr, base) in enumerate(((beta, 0), (beta, HEADS), (g, 2 * HEADS), (g, 3 * HEADS))):
                col = jnp.sum(jnp.where(lane == base + head, arr, 0.0), axis=-1, keepdims=True)
                gates[slot, lo:lo + rows, :] = jnp.broadcast_to(col, (rows, LANES))

        gate_cols(bac_ref, 0, ctx)
        gate_cols(ba_ref, ctx, seq)

        def chunk_body(i, carry, hh=hh):
            steps = [i * unroll + u for u in range(unroll)]
            loaded = [chunk_load(t) for t in steps]
            tris = [chunk_triangle(k, q, beta, g) for (k, v, q, beta, g) in loaded]
            tmats = [eye - a * level_masks[0] for (a, _, _) in tris]
            for m in level_masks[1:]:
                nxt = []
                for (a, _, _), tmat in zip(tris, tmats):
                    cs = (a * m).astype(BF16)
                    tb = tmat.astype(BF16)
                    tc = jnp.dot(tb, cs, preferred_element_type=F32)
                    nxt.append(tmat - jnp.dot(tc.astype(BF16), tb, preferred_element_type=F32))
                tmats = nxt
            for t, (k, v, q, beta, g), (a, attn, gc), tmat in zip(steps, loaded, tris, tmats):
                chunk_finish(hh, t, k, v, q, beta, gc, attn, tmat)
            return carry

        lax.fori_loop(0, nc // unroll, chunk_body, 0)

    def scan_body(t, carry):
        rf = pl.multiple_of(t * CHUNK, CHUNK)
        rb = pl.multiple_of(bwd_chunk(t) * CHUNK, CHUNK)
        r0 = pl.multiple_of(t * two, two)
        d0 = pl.multiple_of(t * SUBLANES, SUBLANES)
        rs = [(jnp.dot(lhs[hh, 0, pl.ds(r0, two), :], carry[2 * hh].astype(BF16), preferred_element_type=F32),
               jnp.dot(lhs[hh, 1, pl.ds(r0, two), :], carry[2 * hh + 1].astype(BF16), preferred_element_type=F32))
              for hh in range(group)]
        v_news = [val[hh, pl.ds(r0, two), :].astype(F32) - jnp.concatenate([r_f[:CHUNK], r_b[:CHUNK]], axis=0)
                  for hh, (r_f, r_b) in enumerate(rs)]
        outs, new_states = [], []
        for hh, ((r_f, r_b), v_new) in enumerate(zip(rs, v_news)):
            zeros = jnp.zeros_like(v_new)
            v_bd = jnp.concatenate([jnp.where(top_rows, v_new, zeros), jnp.where(top_rows, zeros, v_new)],
                                   axis=1).astype(BF16)
            upd = jnp.dot(ktt[hh, pl.ds(r0, two), :], v_bd, preferred_element_type=F32)
            new_states.append(carry[2 * hh] * dec[hh, 0, pl.ds(d0, SUBLANES), :][0:1, :] + upd[:, :DV])
            new_states.append(carry[2 * hh + 1] * dec[hh, 1, pl.ds(d0, SUBLANES), :][0:1, :] + upd[:, DV:])
            outs.append(jnp.concatenate([r_f[CHUNK:], r_b[CHUNK:]], axis=0)
                        + jnp.dot(att[hh, pl.ds(r0, two), :], v_new.astype(BF16), preferred_element_type=F32))
        for hh, o in enumerate(outs):
            osc[hh, pl.ds(rf, CHUNK), :] += o[:CHUNK]
            osc[hh, pl.ds(rb, CHUNK), :] += o[CHUNK:]
        return tuple(new_states)

    osc[...] = jnp.zeros_like(osc)
    zero = jnp.zeros((DK, DV), F32)
    lax.fori_loop(0, nc, scan_body, (zero,) * (2 * group))

    for hh in range(group):
        cols = slice(hh * LANES, (hh + 1) * LANES)
        o = osc[hh, ctx:total, :]
        o = o * lax.rsqrt(jnp.mean(o * o, axis=-1, keepdims=True) + EPS) * gn_ref[...]
        o_ref[:, cols] = (o * _silu(z_ref[:, cols].astype(F32))).astype(o_ref.dtype)


def gated_deltanet(proj, ba, proj_ctx, ba_ctx, conv_kv, conv_q, a_log, dt_bias, gn, *, batch, seq, ctx,
                   unroll=12, group=4):
    total = seq + ctx
    nc = total // CHUNK
    gl = group * LANES
    ng = HEADS // group
    taps = jnp.zeros((SUBLANES, conv_kv.shape[1]), F32).at[:SHORT_CONV].set(conv_kv)
    taps_q = jnp.zeros((SUBLANES, conv_q.shape[1]), F32).at[:SHORT_CONV].set(conv_q)
    ab = jnp.zeros((SUBLANES, LANES), F32)
    ab = ab.at[0, 2 * HEADS:4 * HEADS].set(a_log.reshape(-1)).at[1, 2 * HEADS:4 * HEADS].set(dt_bias.reshape(-1))
    col = lambda base: (lambda b, h: (b, base * ng + h))
    once = pl.Buffered(1)
    return pl.pallas_call(
        functools.partial(_gdn_kernel, seq=seq, ctx=ctx, unroll=unroll, group=group),
        grid=(batch, ng),
        in_specs=[
            pl.BlockSpec((seq, gl), col(COL_K), pipeline_mode=once),
            pl.BlockSpec((seq, gl), col(COL_V), pipeline_mode=once),
            pl.BlockSpec((seq, gl), col(COL_Q), pipeline_mode=once),
            pl.BlockSpec((seq, gl), col(COL_Z), pipeline_mode=once),
            pl.BlockSpec((seq, LANES), lambda b, h: (b, 0), pipeline_mode=once),
            pl.BlockSpec((ctx, gl), col(COL_K)),
            pl.BlockSpec((ctx, gl), col(COL_V)),
            pl.BlockSpec((ctx, LANES), lambda b, h: (b, 0)),
            pl.BlockSpec((SUBLANES, gl), lambda b, h: (0, h)),
            pl.BlockSpec((SUBLANES, gl), lambda b, h: (0, ng + h)),
            pl.BlockSpec((SUBLANES, gl), lambda b, h: (0, h)),
            pl.BlockSpec((SUBLANES, LANES), lambda b, h: (0, 0)),
            pl.BlockSpec((1, LANES), lambda b, h: (0, 0)),
        ],
        out_specs=pl.BlockSpec((seq, gl), lambda b, h: (b, h)),
        out_shape=jax.ShapeDtypeStruct((batch * seq, HEADS * DV), BF16),
        scratch_shapes=[
            pltpu.VMEM((total, LANES), F32),
            pltpu.VMEM((total, LANES), F32),
            pltpu.VMEM((total, LANES), F32),
            pltpu.VMEM((4, total, LANES), F32),
            pltpu.VMEM((seq + 2 * SUBLANES, LANES), F32),
            pltpu.VMEM((group, 2, nc * 2 * CHUNK, LANES), BF16),
            pltpu.VMEM((group, nc * 2 * CHUNK, LANES), BF16),
            pltpu.VMEM((group, nc * 2 * CHUNK, LANES), BF16),
            pltpu.VMEM((group, nc * 2 * CHUNK, LANES), BF16),
            pltpu.VMEM((group, 2, nc * SUBLANES, LANES), F32),
            pltpu.VMEM((group, total, LANES), F32),
        ],
        compiler_params=_params("arbitrary", "arbitrary"),
        name="gated_deltanet",
    )(proj, proj, proj, proj, ba, proj_ctx, proj_ctx, ba_ctx, taps, taps, taps_q, ab, gn.reshape(1, DV))


CONF_ROWS = 128
CONF_HALO = 16


def _conf_kernel(a_ref, g_ref, w_ref, b_ref, o_ref, pad_scr, *, seq):
    cb = a_ref.shape[1]
    u = a_ref[...].astype(F32) * jax.nn.sigmoid(g_ref[...].astype(F32))
    pad_scr[0:CONF_HALO, :] = jnp.zeros((CONF_HALO, cb), F32)
    pad_scr[CONF_HALO:CONF_HALO + seq, :] = u
    pad_scr[CONF_HALO + seq:2 * CONF_HALO + seq, :] = jnp.zeros((CONF_HALO, cb), F32)
    win = CONF_ROWS + 2 * CONF_HALO
    first = CONF_HALO - CONF_KERNEL // 2

    def body(i, carry):
        r0 = pl.multiple_of(i * CONF_ROWS, CONF_ROWS)
        w = pad_scr[pl.ds(r0, win), :]
        acc = jnp.zeros((CONF_ROWS, cb), F32) + b_ref[...]
        for sub in range(SUBLANES):
            shifted = w if sub == 0 else pltpu.roll(w, win - sub, axis=0)
            for j in range(CONF_KERNEL):
                off = first + j
                if off % SUBLANES == sub:
                    base = off - sub
                    acc = acc + shifted[base:base + CONF_ROWS, :] * w_ref[j:j + 1, :]
        o_ref[pl.ds(r0, CONF_ROWS), :] = acc.astype(o_ref.dtype)
        return carry

    lax.fori_loop(0, seq // CONF_ROWS, body, 0)


def conformer_conv(proj, conf_dw, conf_dw_b, *, batch, seq, cb=256):
    c = conf_dw.shape[1]
    nb = c // cb
    taps = jnp.zeros((32, c), F32).at[:CONF_KERNEL].set(conf_dw)
    return pl.pallas_call(
        functools.partial(_conf_kernel, seq=seq),
        grid=(batch, nb),
        in_specs=[pl.BlockSpec((seq, cb), lambda b, j: (b, COL_GLU_A * nb + j)),
                  pl.BlockSpec((seq, cb), lambda b, j: (b, COL_GLU_G * nb + j)),
                  pl.BlockSpec((32, cb), lambda b, j: (0, j)),
                  pl.BlockSpec((1, cb), lambda b, j: (0, j))],
        out_specs=pl.BlockSpec((seq, cb), lambda b, j: (b, j)),
        out_shape=jax.ShapeDtypeStruct((batch * seq, c), BF16),
        scratch_shapes=[pltpu.VMEM((seq + 2 * CONF_HALO, cb), F32)],
        compiler_params=_params("arbitrary", "arbitrary"),
        name="conformer_conv",
    )(proj, proj, taps, conf_dw_b.reshape(1, c))


def _mixer_out_kernel(og_ref, uc_ref, ga_ref, gb_ref, x_ref, pos_ref, gm_ref, shf_ref, scf_ref,
                      gpost_ref, gpre_ref, lng_ref, lnb_ref, wa_ref, wb_ref, wo_ref, wr_ref, br_ref,
                      x2_ref, h2_ref, ti_ref, tw_ref, cnt_ref, cnt_scr, *, sub):
    parts = [slice(r, r + sub) for r in range(0, og_ref.shape[0], sub)]

    def rms(v):
        return v * lax.rsqrt(jnp.mean(v * v, axis=-1, keepdims=True) + EPS)

    def layer_norm_silu(uc):
        mu = jnp.mean(uc, axis=-1, keepdims=True)
        var = jnp.mean(jnp.square(uc - mu), axis=-1, keepdims=True)
        return _silu((uc - mu) * lax.rsqrt(var + EPS) * lng_ref[...] + lnb_ref[...]).astype(BF16)

    us = [layer_norm_silu(uc_ref[p, :].astype(F32)) for p in parts]
    yas = [jnp.dot(og_ref[p, :], wa_ref[...], preferred_element_type=F32) for p in parts]
    ybs = [jnp.dot(u, wb_ref[...], preferred_element_type=F32) for u in us]
    merged = [(jax.nn.sigmoid(ga_ref[p, :].astype(F32)) * ya
               + jax.nn.sigmoid(gb_ref[p, :].astype(F32)) * yb).astype(BF16) for p, ya, yb in zip(parts, yas, ybs)]
    ys = [jnp.dot(m, wo_ref[...], preferred_element_type=F32) for m in merged]
    x2s = [x_ref[p, :] + pos_ref[p, :] + gm_ref[0] * (rms(y) * gpost_ref[...]) for p, y in zip(parts, ys)]
    h2s = [(rms(x2) * gpre_ref[...] * (1.0 + scf_ref[0]) + shf_ref[0]).astype(BF16) for x2 in x2s]
    for p, x2, h2 in zip(parts, x2s, h2s):
        x2_ref[p, :] = x2
        h2_ref[p, :] = h2
    all_logits = [jnp.dot(h2, wr_ref[...], preferred_element_type=F32) + br_ref[...] for h2 in h2s]
    lane = lax.broadcasted_iota(jnp.int32, (sub, LANES), 1).astype(F32)
    ri = lax.broadcasted_iota(jnp.int32, (sub, sub), 0)
    ci = lax.broadcasted_iota(jnp.int32, (sub, sub), 1)
    before = jnp.where(ri > ci, 1.0, 0.0).astype(BF16)

    @pl.when(pl.program_id(0) == 0)
    def _():
        cnt_scr[...] = jnp.zeros_like(cnt_scr)

    routed = []
    for logits in all_logits:
        live = jnp.where(lane < N_EXPERTS, logits, -jnp.inf)
        top_v, top_i = [], []
        for _ in range(TOP_K):
            m = jnp.max(live, axis=-1, keepdims=True)
            idx = jnp.min(jnp.where(live == m, lane, float(LANES)), axis=-1, keepdims=True)
            top_v.append(m)
            top_i.append(idx)
            live = jnp.where(lane == idx, -jnp.inf, live)
        ex = [jnp.exp(v - top_v[0]) for v in top_v]
        denom = ex[0] + ex[1] + ex[2] + ex[3]
        picked = jnp.zeros((sub, LANES), F32)
        for k in range(TOP_K):
            picked = jnp.where(lane == top_i[k], 1.0, picked)
        routed.append((top_i, [e / denom for e in ex], picked,
                       jnp.dot(before, picked.astype(BF16), preferred_element_type=F32)))

    count = cnt_scr[...]
    for j, (p, (top_i, top_w, picked, within)) in enumerate(zip(parts, routed)):
        prefix = within + count[0:1, :]
        ti = jnp.zeros((sub, LANES), F32)
        tw = jnp.zeros((sub, LANES), F32)
        for k in range(TOP_K):
            rank = jnp.sum(jnp.where(lane == top_i[k], prefix, 0.0), axis=-1, keepdims=True)
            ti = jnp.where(lane == k, top_i[k], ti)
            ti = jnp.where(lane == TOP_K + k, rank, ti)
            tw = jnp.where(lane == k, top_w[k], tw)
        ti_ref[:, p] = jnp.transpose(ti)[:SUBLANES, :].astype(jnp.int32)
        tw_ref[:, p] = jnp.transpose(tw)[:SUBLANES, :]
        count = count + jnp.sum(picked, axis=0, keepdims=True)
        cnt_ref[j] = count
    cnt_scr[...] = count


def mixer_output(og, uc, proj, x, pos, mod, mod_row_of_tile, g_post, g_pre_f, ln_g, ln_b,
                 w_a, w_b, w_o, w_r, b_r, *, tm, sub):
    n, d = x.shape
    pos_tiles = pos.shape[0] // tm
    row = lambda i: (i, 0)
    fixed = lambda i: (0, 0)
    modspec = lambda k: pl.BlockSpec((1, 1, d), lambda i: (mod_row_of_tile(i), 0, k))
    vec = pl.BlockSpec((1, d), fixed)
    mat = pl.BlockSpec((d, d), fixed)
    return pl.pallas_call(
        functools.partial(_mixer_out_kernel, sub=sub),
        grid=(n // tm,),
        in_specs=[pl.BlockSpec((tm, d), row), pl.BlockSpec((tm, d), row),
                  pl.BlockSpec((tm, d), lambda i: (i, COL_GATE_A)),
                  pl.BlockSpec((tm, d), lambda i: (i, COL_GATE_B)),
                  pl.BlockSpec((tm, d), row),
                  pl.BlockSpec((tm, d), lambda i: (i % pos_tiles, 0)),
                  modspec(2), modspec(3), modspec(4),
                  vec, vec, vec, vec, mat, mat, mat,
                  pl.BlockSpec((d, LANES), fixed), pl.BlockSpec((1, LANES), fixed)],
        out_specs=[pl.BlockSpec((tm, d), row), pl.BlockSpec((tm, d), row),
                   pl.BlockSpec((SUBLANES, tm), lambda i: (0, i)), pl.BlockSpec((SUBLANES, tm), lambda i: (0, i)),
                   pl.BlockSpec((tm // sub, SUBLANES, LANES), lambda i: (i, 0, 0))],
        out_shape=[jax.ShapeDtypeStruct((n, d), F32), jax.ShapeDtypeStruct((n, d), BF16),
                   jax.ShapeDtypeStruct((SUBLANES, n), jnp.int32), jax.ShapeDtypeStruct((SUBLANES, n), F32),
                   jax.ShapeDtypeStruct((n // sub, SUBLANES, LANES), F32)],
        scratch_shapes=[pltpu.VMEM((SUBLANES, LANES), F32)],
        compiler_params=_params("arbitrary"),
        name="mixer_output",
    )(og, uc, proj, proj, x, pos, mod, mod, mod,
      g_post.reshape(1, d), g_pre_f.reshape(1, d), ln_g.reshape(1, d), ln_b.reshape(1, d),
      w_a, w_b, w_o, w_r, b_r)


EXPERT_BLOCK = 256
MOE_CHUNK = 512
GATHER_SUB = 256
COMBINE_GRANULE = 32
COMBINE_SLOTS = 32


def _expert_kernel(bexp_ref, slo_ref, shi_ref, nused_ref,
                   h2_ref, dest_ref, wt_ref, wgu_ref, bgu_ref, wd_ref, bd_ref, o_ref,
                   acc, rw_acc, wgu_bf, wd_bf):
    b = pl.program_id(0)
    expert = bexp_ref[b]

    @pl.when((b == 0) | (expert != bexp_ref[jnp.maximum(b - 1, 0)]))
    def _():
        wgu_bf[...] = wgu_ref[0].astype(BF16)
        wd_bf[...] = wd_ref[0].astype(BF16)

    @pl.when(b < nused_ref[0])
    def _():
        acc[...] = jnp.zeros_like(acc)
        rw_acc[...] = jnp.zeros_like(rw_acc)
        row = lax.broadcasted_iota(jnp.int32, (EXPERT_BLOCK, GATHER_SUB), 0)
        s_lo, s_hi = slo_ref[b], shi_ref[b]
        last_sub = h2_ref.shape[0] // GATHER_SUB - 1

        def select(s, live):
            t0 = pl.multiple_of(s * GATHER_SUB, GATHER_SUB)
            local = dest_ref[:, pl.ds(t0, GATHER_SUB)] - b * EXPERT_BLOCK
            if live is not True:
                local = jnp.where(live, local, -1)
            inside = (local >= 0) & (local < EXPERT_BLOCK)
            hit_row = jnp.sum(jnp.where(inside, local + 1, 0), axis=0, keepdims=True) - 1
            hit_w = jnp.sum(jnp.where(inside, wt_ref[:, pl.ds(t0, GATHER_SUB)], 0.0), axis=0, keepdims=True)
            hit = row == hit_row
            return (jnp.where(hit, 1.0, 0.0).astype(BF16), jnp.where(hit, hit_w, 0.0),
                    h2_ref[pl.ds(t0, GATHER_SUB), :])

        def gather(i, carry):
            s0 = s_lo + 2 * i
            p0, w0, x0 = select(s0, True)
            p1, w1, x1 = select(jnp.minimum(s0 + 1, last_sub), s0 + 1 <= s_hi)
            acc[...] += jnp.dot(jnp.concatenate([p0, p1], axis=1), jnp.concatenate([x0, x1], axis=0),
                                preferred_element_type=F32)
            rw_acc[...] += jnp.sum(w0 + w1, axis=1, keepdims=True)
            return carry

        lax.fori_loop(0, (s_hi - s_lo + 2) // 2, gather, 0)
        gu = jnp.dot(acc[...].astype(BF16), wgu_bf[...], preferred_element_type=F32) + bgu_ref[0]
        dff = gu.shape[1] // 2
        gl = jnp.minimum(gu[:, :dff], SWIGLU_LIMIT)
        lin = jnp.clip(gu[:, dff:], -SWIGLU_LIMIT, SWIGLU_LIMIT)
        act = (gl * jax.nn.sigmoid(SWIGLU_ALPHA * gl) * (lin + 1.0)).astype(BF16)
        y = jnp.dot(act, wd_bf[...], preferred_element_type=F32) + bd_ref[0]
        o_ref[...] = (y * rw_acc[...]).astype(o_ref.dtype)

    @pl.when(b >= nused_ref[0])
    def _():
        o_ref[...] = jnp.zeros_like(o_ref)


def expert_blocks(h2, dest_t, wt_t, block_expert, sub_lo, sub_hi, n_used, w_gate_up, b_gate_up, w_down, b_down):
    n, d = h2.shape
    e, _, f2 = w_gate_up.shape
    n_blocks = block_expert.shape[0]
    whole = lambda b, be, lo, hi, nu: (0, 0)
    by_expert = lambda b, be, lo, hi, nu: (be[b], 0, 0)
    once = pl.Buffered(1)
    grid_spec = pltpu.PrefetchScalarGridSpec(
        num_scalar_prefetch=4,
        grid=(n_blocks,),
        in_specs=[pl.BlockSpec((n, d), whole, pipeline_mode=once),
                  pl.BlockSpec((SUBLANES, n), whole, pipeline_mode=once),
                  pl.BlockSpec((SUBLANES, n), whole, pipeline_mode=once),
                  pl.BlockSpec((1, d, f2), by_expert, pipeline_mode=once),
                  pl.BlockSpec((1, 1, f2), by_expert),
                  pl.BlockSpec((1, f2 // 2, d), by_expert, pipeline_mode=once),
                  pl.BlockSpec((1, 1, d), by_expert)],
        out_specs=pl.BlockSpec((EXPERT_BLOCK, d), lambda b, be, lo, hi, nu: (b, 0)),
        scratch_shapes=[pltpu.VMEM((EXPERT_BLOCK, d), F32), pltpu.VMEM((EXPERT_BLOCK, 1), F32),
                        pltpu.VMEM((d, f2), BF16), pltpu.VMEM((f2 // 2, d), BF16)],
    )
    return pl.pallas_call(
        _expert_kernel,
        grid_spec=grid_spec,
        out_shape=jax.ShapeDtypeStruct((n_blocks * EXPERT_BLOCK, d), BF16),
        compiler_params=_params("arbitrary"),
        name="expert_blocks",
    )(block_expert, sub_lo, sub_hi, n_used, h2, dest_t, wt_t, w_gate_up, b_gate_up.reshape(e, 1, f2), w_down,
      b_down.reshape(e, 1, d))


def _combine_kernel(coff_ref, cgran_ref, yb_hbm, dest_ref, x2_ref, gf_ref, g_ref, o_ref, slab, sem, acc):
    c = pl.program_id(0)
    start = coff_ref[c]
    count = coff_ref[c + 1] - start
    n_groups = (count + COMBINE_SLOTS - 1) // COMBINE_SLOTS

    def granule_of(g, s):
        return cgran_ref[start + jnp.minimum(g * COMBINE_SLOTS + s, count - 1)]

    def copies(g, buf):
        return [pltpu.make_async_copy(
            yb_hbm.at[pl.ds(pl.multiple_of(granule_of(g, s) * COMBINE_GRANULE, COMBINE_GRANULE), COMBINE_GRANULE), :],
            slab.at[buf, pl.ds(s * COMBINE_GRANULE, COMBINE_GRANULE), :], sem.at[buf])
            for s in range(COMBINE_SLOTS)]

    for cp in copies(0, 0):
        cp.start()
    acc[...] = jnp.zeros_like(acc)
    chunk = x2_ref.shape[0]
    row = lax.broadcasted_iota(jnp.int32, (COMBINE_GRANULE, chunk), 0)
    dest_rows = dest_ref[...]
    n_rows = yb_hbm.shape[0]

    def group(g, carry):
        buf = g % 2

        @pl.when(g + 1 < n_groups)
        def _():
            for cp in copies(g + 1, 1 - buf):
                cp.start()

        for cp in copies(g, buf):
            cp.wait()
        pieces = []
        for s in range(COMBINE_SLOTS):
            live = g * COMBINE_SLOTS + s < count
            local = dest_rows - jnp.where(live, granule_of(g, s) * COMBINE_GRANULE, n_rows)
            inside = (local >= 0) & (local < COMBINE_GRANULE)
            hit_row = jnp.sum(jnp.where(inside, local + 1, 0), axis=0, keepdims=True) - 1
            pieces.append(jnp.where(row == hit_row, 1.0, 0.0).astype(BF16))
        onehot_t = jnp.concatenate(pieces, axis=0)
        acc[...] += lax.dot_general(onehot_t, slab[buf], (((0,), (0,)), ((), ())),
                                    preferred_element_type=F32)
        return carry

    lax.fori_loop(0, n_groups, group, 0)
    y = acc[...]
    yn = y * lax.rsqrt(jnp.mean(y * y, axis=-1, keepdims=True) + EPS) * g_ref[...]
    o_ref[...] = x2_ref[...] + gf_ref[0] * yn


def combine_residual(yb, dest, x2, chunk_off, chunk_granules, mod, mod_row_of_chunk, g_post_f):
    n, d = x2.shape
    chunk_row = lambda c, off, blk: (c, 0)
    grid_spec = pltpu.PrefetchScalarGridSpec(
        num_scalar_prefetch=2,
        grid=(n // MOE_CHUNK,),
        in_specs=[pl.BlockSpec(memory_space=pl.ANY),
                  pl.BlockSpec((SUBLANES, MOE_CHUNK), lambda c, off, blk: (0, c)),
                  pl.BlockSpec((MOE_CHUNK, d), chunk_row),
                  pl.BlockSpec((1, 1, d), lambda c, off, blk: (mod_row_of_chunk(c), 0, 5)),
                  pl.BlockSpec((1, d), lambda c, off, blk: (0, 0))],
        out_specs=pl.BlockSpec((MOE_CHUNK, d), chunk_row),
        scratch_shapes=[pltpu.VMEM((2, COMBINE_SLOTS * COMBINE_GRANULE, d), BF16),
                        pltpu.SemaphoreType.DMA((2,)),
                        pltpu.VMEM((MOE_CHUNK, d), F32)],
    )
    return pl.pallas_call(
        _combine_kernel,
        grid_spec=grid_spec,
        out_shape=jax.ShapeDtypeStruct((n, d), F32),
        compiler_params=_params("arbitrary"),
        name="combine_residual",
    )(chunk_off, chunk_granules, yb, dest, x2, mod, g_post_f.reshape(1, d))


def _grid_pos_embedding(rows, d):
    row = jnp.repeat(jnp.arange(rows, dtype=F32), GRID_W)
    col = jnp.tile(jnp.arange(GRID_W, dtype=F32), rows)
    quarter = d // 4
    omega = POS_BASE ** (-jnp.arange(quarter, dtype=F32) / quarter)

    def emb(p):
        ang = p[:, None] * omega[None, :]
        return jnp.concatenate([jnp.sin(ang), jnp.cos(ang)], axis=-1)

    return jnp.concatenate([emb(row), emb(col)], axis=-1)


def _count_le(sorted_vals, queries):
    return jnp.sum(sorted_vals <= queries[:, None], axis=1).astype(jnp.int32)


def _moe_plan(ti, cnt_after, n_tok, tm):
    assert tm == GATHER_SUB
    sub_after = cnt_after[:, 0, :N_EXPERTS].astype(jnp.int32)
    counts = sub_after[-1]
    padded = (counts + EXPERT_BLOCK - 1) // EXPERT_BLOCK * EXPERT_BLOCK
    padded_end = jnp.cumsum(padded)
    padded_start = padded_end - padded
    experts = jnp.arange(N_EXPERTS, dtype=jnp.int32)[:, None, None]
    start_of = jnp.sum(jnp.where(ti[None, :TOP_K] == experts, padded_start[:, None, None], 0), axis=0)
    dest_t = jnp.concatenate([start_of + ti[TOP_K:2 * TOP_K],
                              jnp.full((SUBLANES - TOP_K, n_tok), -1, jnp.int32)], axis=0)
    n_blocks = n_tok * TOP_K // EXPERT_BLOCK + N_EXPERTS
    blocks = jnp.arange(n_blocks, dtype=jnp.int32)
    block_expert = jnp.minimum(_count_le(padded_end[None, :], blocks * EXPERT_BLOCK), N_EXPERTS - 1)
    n_used = (padded_end[-1:] // EXPERT_BLOCK).astype(jnp.int32)
    r_lo = (blocks - padded_start[block_expert] // EXPERT_BLOCK) * EXPERT_BLOCK
    r_hi = jnp.minimum(counts[block_expert], r_lo + EXPERT_BLOCK) - 1
    through = sub_after.T[block_expert]
    n_sub = sub_after.shape[0]
    sub_lo = jnp.minimum(_count_le(through, r_lo), n_sub - 1)
    sub_hi = jnp.minimum(_count_le(through, r_hi), n_sub - 1)
    nch = n_tok // MOE_CHUNK
    per_chunk = MOE_CHUNK // tm
    cb_after = sub_after[per_chunk - 1::per_chunk]
    cb_before = jnp.concatenate([jnp.zeros((1, N_EXPERTS), jnp.int32), cb_after[:-1]], axis=0)
    b_lo = ((padded_start[None, :] + cb_before) // COMBINE_GRANULE).reshape(-1)
    b_hi = ((padded_start[None, :] + cb_after - 1) // COMBINE_GRANULE).reshape(-1)
    n_it = jnp.where((cb_after > cb_before).reshape(-1), b_hi - b_lo + 1, 0)
    off_end = jnp.cumsum(n_it)
    off = off_end - n_it
    n_items = n_blocks * (EXPERT_BLOCK // COMBINE_GRANULE) + N_EXPERTS * (nch - 1)
    w = jnp.minimum(jnp.arange(n_items, dtype=jnp.int32), off_end[-1] - 1)
    cell = jnp.minimum(_count_le(off_end[None, :], w), n_it.shape[0] - 1)
    chunk_granules = (b_lo[cell] + w - off[cell]).astype(jnp.int32)
    chunk_off = jnp.concatenate([jnp.zeros((1,), jnp.int32), off_end[N_EXPERTS - 1::N_EXPERTS]]).astype(jnp.int32)
    return dest_t, block_expert, sub_lo, sub_hi, n_used, chunk_off, chunk_granules


def kernel(x, c, ctx, c_ctx, w_mod, b_mod, g_pre_mix, g_post_mix, g_pre_ffn, g_post_ffn, w_in, conv_kv,
           conv_q, a_log, dt_bias, gdn_norm_g, w_proj_a, conf_dw, conf_dw_b, conf_ln_g, conf_ln_b,
           w_proj_b, w_out, w_router, b_router, w_gate_up, b_gate_up, w_down, b_down):
    batch, seq, d = x.shape
    ctx_len = ctx.shape[1]
    n_tok = batch * seq
    gw = HEADS * DV
    beta_off = HEADS * DK + gw
    state_cols = beta_off + 4 * HEADS
    q_off = state_cols

    mod_rows = 2 * SUBLANES
    cond = jnp.zeros((mod_rows, d), F32).at[:batch].set(c).at[batch].set(c_ctx)
    mod = modulation(cond, w_mod[0], b_mod[0]).reshape(mod_rows, 1, 6 * d)

    w = w_in[0]
    w_main = jnp.concatenate([w[:, :beta_off], w[:, q_off:]], axis=1).astype(BF16)
    w_ba = jnp.zeros((d, LANES), F32).at[:, :4 * HEADS].set(w[:, beta_off:state_cols]).astype(BF16)

    pos = _grid_pos_embedding(seq // GRID_W, d)
    x_flat = x.reshape(n_tok, d)
    tm = 1024
    tiles_per_seq = seq // tm
    proj, ba = input_projection(x_flat, pos, mod, lambda i: i // tiles_per_seq, g_pre_mix[0],
                                w_main, w_ba, tm=tm, tn=2048)
    proj_ctx, ba_ctx = input_projection(ctx.reshape(batch * ctx_len, d), None, mod, lambda i: batch,
                                        g_pre_mix[0], w_main[:, :beta_off], w_ba, tm=ctx_len, tn=1024)

    og = gated_deltanet(proj, ba, proj_ctx, ba_ctx, conv_kv[0], conv_q[0], a_log[0], dt_bias[0],
                        gdn_norm_g[0], batch=batch, seq=seq, ctx=ctx_len)
    uc = conformer_conv(proj, conf_dw[0], conf_dw_b[0], batch=batch, seq=seq)

    tm2 = 512
    w_r = jnp.zeros((d, LANES), F32).at[:, :N_EXPERTS].set(w_router[0]).astype(BF16)
    b_r = jnp.zeros((1, LANES), F32).at[0, :N_EXPERTS].set(b_router[0])
    x2, h2, top_i, top_w, cnt_after = mixer_output(
        og, uc, proj, x_flat, pos, mod, lambda i: i // (seq // tm2), g_post_mix[0], g_pre_ffn[0],
        conf_ln_g[0], conf_ln_b[0], w_proj_a[0].astype(BF16), w_proj_b[0].astype(BF16),
        w_out[0].astype(BF16), w_r, b_r, tm=tm2, sub=GATHER_SUB)

    dest_t, block_expert, sub_lo, sub_hi, n_used, chunk_off, chunk_blocks = _moe_plan(
        top_i, cnt_after, n_tok, GATHER_SUB)
    yb = expert_blocks(h2, dest_t, top_w, block_expert, sub_lo, sub_hi, n_used, w_gate_up[0], b_gate_up[0],
                       w_down[0], b_down[0])
    out = combine_residual(yb, dest_t, x2, chunk_off, chunk_blocks, mod, lambda c: c // (seq // MOE_CHUNK),
                           g_post_ffn[0])
    return out.reshape(batch, seq, d)
```

```python
import functools

import jax
import jax.numpy as jnp
from jax import lax
from jax.experimental import pallas as pl
from jax.experimental.pallas import tpu as pltpu

F32 = jnp.float32
BF16 = jnp.bfloat16

D_MODEL = 1024
GRID_W = 64
HEADS = 8
DK = 128
DV = 128
SHORT_CONV = 5
CHUNK = 64
CONF_KERNEL = 31
N_EXPERTS = 32
TOP_K = 4
SWIGLU_LIMIT = 7.0
SWIGLU_ALPHA = 1.702
MOE_BLOCK = 128
EPS = 1e-6
POS_BASE = 10000.0

LANES = 128
SUBLANES = 8
VMEM_LIMIT = 60 * 1024 * 1024

COL_K, COL_V, COL_Q, COL_Z, COL_GLU_A, COL_GLU_G, COL_GATE_A, COL_GATE_B = range(8)


def _params(*sem):
    return pltpu.CompilerParams(dimension_semantics=sem, vmem_limit_bytes=VMEM_LIMIT)


def _mod_kernel(c_ref, w_ref, b_ref, o_ref):
    c = c_ref[...]
    s = c * jax.nn.sigmoid(c)
    o_ref[...] = jnp.dot(s, w_ref[...], preferred_element_type=F32,
                         precision=lax.Precision.HIGHEST) + b_ref[...]


def modulation(cond, w_mod, b_mod):
    r, d = cond.shape
    n = w_mod.shape[1]
    tn = 1024
    return pl.pallas_call(
        _mod_kernel,
        grid=(n // tn,),
        in_specs=[pl.BlockSpec((r, d), lambda j: (0, 0)),
                  pl.BlockSpec((d, tn), lambda j: (0, j)),
                  pl.BlockSpec((1, tn), lambda j: (0, j))],
        out_specs=pl.BlockSpec((r, tn), lambda j: (0, j)),
        out_shape=jax.ShapeDtypeStruct((r, n), F32),
        compiler_params=_params("arbitrary"),
        name="modulation",
    )(cond, w_mod, b_mod.reshape(1, n))


def _proj_weights_kernel(a_ref, b_ref, o_ref, *, first_shifted, shift):
    @pl.when(pl.program_id(0) < first_shifted)
    def _():
        o_ref[...] = a_ref[...].astype(o_ref.dtype)

    @pl.when(pl.program_id(0) >= first_shifted)
    def _():
        o_ref[...] = jnp.concatenate([a_ref[:, shift:], b_ref[:, :shift]], axis=1).astype(o_ref.dtype)


def projection_weights(w, *, keep, skip, tn=1024):
    d, total = w.shape
    n_out = total - skip
    assert keep % tn == 0 and n_out % tn == 0 and 0 < skip < LANES
    per_tile = tn // LANES
    return pl.pallas_call(
        functools.partial(_proj_weights_kernel, first_shifted=keep // tn, shift=skip),
        grid=(n_out // tn,),
        in_specs=[pl.BlockSpec((d, tn), lambda j: (0, j)),
                  pl.BlockSpec((d, LANES), lambda j: (0, (j + 1) * per_tile))],
        out_specs=pl.BlockSpec((d, tn), lambda j: (0, j)),
        out_shape=jax.ShapeDtypeStruct((d, n_out), BF16),
        compiler_params=_params("arbitrary"),
        name="projection_weights",
    )(w, w)


def _inproj_kernel(*refs, has_pos):
    if has_pos:
        x_ref, pos_ref, sh_ref, sc_ref, g_ref, w_ref, wba_ref, o_ref, ba_ref, h_scr = refs
    else:
        x_ref, sh_ref, sc_ref, g_ref, w_ref, wba_ref, o_ref, ba_ref, h_scr = refs

    @pl.when(pl.program_id(1) == 0)
    def _():
        x = x_ref[...]
        if has_pos:
            x = x + pos_ref[...]
        y = x * lax.rsqrt(jnp.mean(x * x, axis=-1, keepdims=True) + EPS) * g_ref[...]
        h = (y * (1.0 + sc_ref[0]) + sh_ref[0]).astype(BF16)
        h_scr[...] = h
        ba_ref[...] = jnp.dot(h, wba_ref[...], preferred_element_type=F32)

    o_ref[...] = jnp.dot(h_scr[...], w_ref[...], preferred_element_type=F32).astype(o_ref.dtype)


def input_projection(x, pos, mod, mod_row_of_tile, g_pre, w_main, w_ba, *, tm, tn):
    n, d = x.shape
    w = w_main.shape[1]
    has_pos = pos is not None
    in_specs = [pl.BlockSpec((tm, d), lambda i, j: (i, 0))]
    args = [x]
    if has_pos:
        pos_tiles = pos.shape[0] // tm
        in_specs.append(pl.BlockSpec((tm, d), lambda i, j: (i % pos_tiles, 0)))
        args.append(pos)
    in_specs += [
        pl.BlockSpec((1, 1, d), lambda i, j: (mod_row_of_tile(i), 0, 0)),
        pl.BlockSpec((1, 1, d), lambda i, j: (mod_row_of_tile(i), 0, 1)),
        pl.BlockSpec((1, d), lambda i, j: (0, 0)),
        pl.BlockSpec((d, tn), lambda i, j: (0, j)),
        pl.BlockSpec((d, LANES), lambda i, j: (0, 0)),
    ]
    args += [mod, mod, g_pre.reshape(1, d), w_main, w_ba]
    return pl.pallas_call(
        functools.partial(_inproj_kernel, has_pos=has_pos),
        grid=(n // tm, w // tn),
        in_specs=in_specs,
        out_specs=[pl.BlockSpec((tm, tn), lambda i, j: (i, j)),
                   pl.BlockSpec((tm, LANES), lambda i, j: (i, 0))],
        out_shape=[jax.ShapeDtypeStruct((n, w), BF16), jax.ShapeDtypeStruct((n, LANES), F32)],
        scratch_shapes=[pltpu.VMEM((tm, d), BF16)],
        compiler_params=_params("arbitrary", "arbitrary"),
        name="input_projection_pos" if has_pos else "input_projection_ctx",
    )(*args)


def _silu(x):
    return x * jax.nn.sigmoid(x)


def _softplus(x):
    return jnp.maximum(x, 0.0) + jnp.log(1.0 + jnp.exp(-jnp.abs(x)))


def _short_conv(src, taps, pad_scr, rows):
    half = SHORT_CONV // 2
    pad_scr[0:SUBLANES, :] = jnp.zeros((SUBLANES, LANES), F32)
    pad_scr[SUBLANES:SUBLANES + rows, :] = src.astype(F32)
    pad_scr[SUBLANES + rows:2 * SUBLANES + rows, :] = jnp.zeros((SUBLANES, LANES), F32)
    acc = None
    for j in range(SHORT_CONV):
        off = SUBLANES + j - half
        term = pad_scr[off:off + rows, :] * taps[j:j + 1, :]
        acc = term if acc is None else acc + term
    return _silu(acc)


def _l2n(x):
    return x * lax.rsqrt(jnp.sum(x * x, axis=-1, keepdims=True) + EPS)


def _gdn_kernel(k_ref, v_ref, q_ref, z_ref, ba_ref, kc_ref, vc_ref, bac_ref,
                wk_ref, wv_ref, wq_ref, ab_ref, gn_ref, o_ref,
                kf, vf, qf, gates, pad_scr, lhs, val, att, ktt, dec, osc, *, seq, ctx, unroll, group):
    total = ctx + seq
    nc = total // CHUNK
    ncc = ctx // CHUNK
    two = 2 * CHUNK
    lg_chunk = CHUNK.bit_length() - 1

    ri = lax.broadcasted_iota(jnp.int32, (two, two), 0)
    ci = lax.broadcasted_iota(jnp.int32, (two, two), 1)
    same_dir = (ri >> lg_chunk) == (ci >> lg_chunk)
    incl = same_dir & (((ri < CHUNK) & (ri >= ci)) | ((ri >= CHUNK) & (ri <= ci)))
    strict = incl & (ri != ci)
    eye = jnp.where(ri == ci, 1.0, 0.0)
    tri = jnp.where(incl, 1.0, 0.0).astype(BF16)
    level_masks = []
    for lg in range(lg_chunk):
        same_parent = (ri >> (lg + 1)) == (ci >> (lg + 1))
        level_masks.append(jnp.where(same_parent & ((ri >> lg) != (ci >> lg)), 1.0, 0.0))
    top_rows = lax.broadcasted_iota(jnp.int32, (two, LANES), 0) < CHUNK

    def bwd_chunk(t):
        return jnp.where(t < ncc, ncc - 1 - t, nc + ncc - 1 - t)

    def stacked(ref, rf, rb):
        return jnp.concatenate([ref[pl.ds(rf, CHUNK), :], ref[pl.ds(rb, CHUNK), :]], axis=0)

    def chunk_load(t):
        rf = pl.multiple_of(t * CHUNK, CHUNK)
        rb = pl.multiple_of(bwd_chunk(t) * CHUNK, CHUNK)
        beta = jnp.concatenate([gates[0, pl.ds(rf, CHUNK), :], gates[1, pl.ds(rb, CHUNK), :]], axis=0)
        g = jnp.concatenate([gates[2, pl.ds(rf, CHUNK), :], gates[3, pl.ds(rb, CHUNK), :]], axis=0)
        return stacked(kf, rf, rb), stacked(vf, rf, rb), stacked(qf, rf, rb), beta, g

    def chunk_triangle(k, q, beta, g):
        kb = k.astype(BF16)
        kq = jnp.concatenate([kb, q.astype(BF16)], axis=0)
        kk_qk = lax.dot_general(kq, kb, (((1,), (1,)), ((), ())), preferred_element_type=F32)
        g_hi = g.astype(BF16)
        r1 = g - g_hi.astype(F32)
        g_mid = r1.astype(BF16)
        g_lo = (r1 - g_mid.astype(F32)).astype(BF16)
        gc3 = jnp.dot(tri, jnp.concatenate([g_hi, g_mid, g_lo], axis=1), preferred_element_type=F32)
        gc = gc3[:, :LANES] + gc3[:, LANES:2 * LANES] + gc3[:, 2 * LANES:]
        gc_row = jnp.transpose(gc)
        decay = jnp.where(incl, jnp.exp(gc - gc_row), 0.0)
        a = jnp.where(strict, beta * kk_qk[:two] * decay, 0.0)
        return a, (kk_qk[two:] * decay).astype(BF16), gc

    def chunk_finish(hh, t, k, v, q, beta, gc, attn, tmat):
        eg = jnp.exp(gc)
        rhs = jnp.concatenate([v * beta, k * beta * eg], axis=1).astype(BF16)
        sol = jnp.dot(tmat.astype(BF16), rhs, preferred_element_type=F32)
        g_end = jnp.where(top_rows, jnp.broadcast_to(gc[CHUNK - 1:CHUNK, :], (two, LANES)),
                          jnp.broadcast_to(gc[CHUNK:CHUNK + 1, :], (two, LANES)))
        k_tail = k * jnp.exp(g_end - gc)
        qg = (q * eg).astype(BF16)
        kcum = sol[:, DV:].astype(BF16)
        r0 = pl.multiple_of(t * two, two)
        lhs[hh, 0, pl.ds(r0, two), :] = jnp.concatenate([kcum[:CHUNK], qg[:CHUNK]], axis=0)
        lhs[hh, 1, pl.ds(r0, two), :] = jnp.concatenate([kcum[CHUNK:], qg[CHUNK:]], axis=0)
        val[hh, pl.ds(r0, two), :] = sol[:, :DV].astype(BF16)
        att[hh, pl.ds(r0, two), :] = attn
        ktt[hh, pl.ds(r0, two), :] = jnp.transpose(k_tail).astype(BF16)
        d0 = pl.multiple_of(t * SUBLANES, SUBLANES)
        e_end = jnp.exp(g_end)
        dec[hh, 0, pl.ds(d0, SUBLANES), :] = e_end[:SUBLANES]
        dec[hh, 1, pl.ds(d0, SUBLANES), :] = e_end[CHUNK:CHUNK + SUBLANES]

    for hh in range(group):
        head = pl.program_id(1) * group + hh
        cols = slice(hh * LANES, (hh + 1) * LANES)

        kf[0:ctx, :] = _l2n(_short_conv(kc_ref[:, cols], wk_ref[:, cols], pad_scr, ctx))
        vf[0:ctx, :] = _short_conv(vc_ref[:, cols], wv_ref[:, cols], pad_scr, ctx)
        qf[0:ctx, :] = jnp.zeros((ctx, LANES), F32)
        kf[ctx:total, :] = _l2n(_short_conv(k_ref[:, cols], wk_ref[:, cols], pad_scr, seq))
        vf[ctx:total, :] = _short_conv(v_ref[:, cols], wv_ref[:, cols], pad_scr, seq)
        qf[ctx:total, :] = _l2n(_short_conv(q_ref[:, cols], wq_ref[:, cols], pad_scr, seq)) * (DK ** -0.5)

        def gate_cols(src_ref, lo, rows):
            x = src_ref[...]
            lane = lax.broadcasted_iota(jnp.int32, (rows, LANES), 1)
            beta = jax.nn.sigmoid(x)
            g = -jnp.exp(ab_ref[0:1, :]) * _softplus(x + ab_ref[1:2, :])
            for slot, (arr, base) in enumerate(((beta, 0), (beta, HEADS), (g, 2 * HEADS), (g, 3 * HEADS))):
                col = jnp.sum(jnp.where(lane == base + head, arr, 0.0), axis=-1, keepdims=True)
                gates[slot, lo:lo + rows, :] = jnp.broadcast_to(col, (rows, LANES))

        gate_cols(bac_ref, 0, ctx)
        gate_cols(ba_ref, ctx, seq)

        def chunk_body(i, carry, hh=hh):
            steps = [i * unroll + u for u in range(unroll)]
            loaded = [chunk_load(t) for t in steps]
            tris = [chunk_triangle(k, q, beta, g) for (k, v, q, beta, g) in loaded]
            tmats = [eye - a * level_masks[0] for (a, _, _) in tris]
            for m in level_masks[1:]:
                nxt = []
                for (a, _, _), tmat in zip(tris, tmats):
                    cs = (a * m).astype(BF16)
                    tb = tmat.astype(BF16)
                    tc = jnp.dot(tb, cs, preferred_element_type=F32)
                    nxt.append(tmat - jnp.dot(tc.astype(BF16), tb, preferred_element_type=F32))
                tmats = nxt
            for t, (k, v, q, beta, g), (a, attn, gc), tmat in zip(steps, loaded, tris, tmats):
                chunk_finish(hh, t, k, v, q, beta, gc, attn, tmat)
            return carry

        lax.fori_loop(0, nc // unroll, chunk_body, 0)

    def scan_body(t, carry):
        rf = pl.multiple_of(t * CHUNK, CHUNK)
        rb = pl.multiple_of(bwd_chunk(t) * CHUNK, CHUNK)
        r0 = pl.multiple_of(t * two, two)
        d0 = pl.multiple_of(t * SUBLANES, SUBLANES)
        rs = [(jnp.dot(lhs[hh, 0, pl.ds(r0, two), :], carry[2 * hh].astype(BF16), preferred_element_type=F32),
               jnp.dot(lhs[hh, 1, pl.ds(r0, two), :], carry[2 * hh + 1].astype(BF16), preferred_element_type=F32))
              for hh in range(group)]
        v_news = [val[hh, pl.ds(r0, two), :].astype(F32) - jnp.concatenate([r_f[:CHUNK], r_b[:CHUNK]], axis=0)
                  for hh, (r_f, r_b) in enumerate(rs)]
        outs, new_states = [], []
        for hh, ((r_f, r_b), v_new) in enumerate(zip(rs, v_news)):
            zeros = jnp.zeros_like(v_new)
            v_bd = jnp.concatenate([jnp.where(top_rows, v_new, zeros), jnp.where(top_rows, zeros, v_new)],
                                   axis=1).astype(BF16)
            upd = jnp.dot(ktt[hh, pl.ds(r0, two), :], v_bd, preferred_element_type=F32)
            new_states.append(carry[2 * hh] * dec[hh, 0, pl.ds(d0, SUBLANES), :][0:1, :] + upd[:, :DV])
            new_states.append(carry[2 * hh + 1] * dec[hh, 1, pl.ds(d0, SUBLANES), :][0:1, :] + upd[:, DV:])
            outs.append(jnp.concatenate([r_f[CHUNK:], r_b[CHUNK:]], axis=0)
                        + jnp.dot(att[hh, pl.ds(r0, two), :], v_new.astype(BF16), preferred_element_type=F32))
        for hh, o in enumerate(outs):
            osc[hh, pl.ds(rf, CHUNK), :] += o[:CHUNK]
            osc[hh, pl.ds(rb, CHUNK), :] += o[CHUNK:]
        return tuple(new_states)

    osc[...] = jnp.zeros_like(osc)
    zero = jnp.zeros((DK, DV), F32)
    lax.fori_loop(0, nc, scan_body, (zero,) * (2 * group))

    for hh in range(group):
        cols = slice(hh * LANES, (hh + 1) * LANES)
        o = osc[hh, ctx:total, :]
        o = o * lax.rsqrt(jnp.mean(o * o, axis=-1, keepdims=True) + EPS) * gn_ref[...]
        o_ref[:, cols] = (o * _silu(z_ref[:, cols].astype(F32))).astype(o_ref.dtype)


def gated_deltanet(proj, ba, proj_ctx, ba_ctx, conv_kv, conv_q, a_log, dt_bias, gn, *, batch, seq, ctx,
                   unroll=12, group=4):
    total = seq + ctx
    nc = total // CHUNK
    gl = group * LANES
    ng = HEADS // group
    taps = jnp.zeros((SUBLANES, conv_kv.shape[1]), F32).at[:SHORT_CONV].set(conv_kv)
    taps_q = jnp.zeros((SUBLANES, conv_q.shape[1]), F32).at[:SHORT_CONV].set(conv_q)
    ab = jnp.zeros((SUBLANES, LANES), F32)
    ab = ab.at[0, 2 * HEADS:4 * HEADS].set(a_log.reshape(-1)).at[1, 2 * HEADS:4 * HEADS].set(dt_bias.reshape(-1))
    col = lambda base: (lambda b, h: (b, base * ng + h))
    once = pl.Buffered(1)
    return pl.pallas_call(
        functools.partial(_gdn_kernel, seq=seq, ctx=ctx, unroll=unroll, group=group),
        grid=(batch, ng),
        in_specs=[
            pl.BlockSpec((seq, gl), col(COL_K), pipeline_mode=once),
            pl.BlockSpec((seq, gl), col(COL_V), pipeline_mode=once),
            pl.BlockSpec((seq, gl), col(COL_Q), pipeline_mode=once),
            pl.BlockSpec((seq, gl), col(COL_Z), pipeline_mode=once),
            pl.BlockSpec((seq, LANES), lambda b, h: (b, 0), pipeline_mode=once),
            pl.BlockSpec((ctx, gl), col(COL_K)),
            pl.BlockSpec((ctx, gl), col(COL_V)),
            pl.BlockSpec((ctx, LANES), lambda b, h: (b, 0)),
            pl.BlockSpec((SUBLANES, gl), lambda b, h: (0, h)),
            pl.BlockSpec((SUBLANES, gl), lambda b, h: (0, ng + h)),
            pl.BlockSpec((SUBLANES, gl), lambda b, h: (0, h)),
            pl.BlockSpec((SUBLANES, LANES), lambda b, h: (0, 0)),
            pl.BlockSpec((1, LANES), lambda b, h: (0, 0)),
        ],
        out_specs=pl.BlockSpec((seq, gl), lambda b, h: (b, h)),
        out_shape=jax.ShapeDtypeStruct((batch * seq, HEADS * DV), BF16),
        scratch_shapes=[
            pltpu.VMEM((total, LANES), F32),
            pltpu.VMEM((total, LANES), F32),
            pltpu.VMEM((total, LANES), F32),
            pltpu.VMEM((4, total, LANES), F32),
            pltpu.VMEM((seq + 2 * SUBLANES, LANES), F32),
            pltpu.VMEM((group, 2, nc * 2 * CHUNK, LANES), BF16),
            pltpu.VMEM((group, nc * 2 * CHUNK, LANES), BF16),
            pltpu.VMEM((group, nc * 2 * CHUNK, LANES), BF16),
            pltpu.VMEM((group, nc * 2 * CHUNK, LANES), BF16),
            pltpu.VMEM((group, 2, nc * SUBLANES, LANES), F32),
            pltpu.VMEM((group, total, LANES), F32),
        ],
        compiler_params=_params("arbitrary", "arbitrary"),
        name="gated_deltanet",
    )(proj, proj, proj, proj, ba, proj_ctx, proj_ctx, ba_ctx, taps, taps, taps_q, ab, gn.reshape(1, DV))


CONF_ROWS = 128
CONF_HALO = 16


def _conf_kernel(a_ref, g_ref, w_ref, b_ref, o_ref, pad_scr, *, seq):
    cb = a_ref.shape[1]
    u = a_ref[...].astype(F32) * jax.nn.sigmoid(g_ref[...].astype(F32))
    pad_scr[0:CONF_HALO, :] = jnp.zeros((CONF_HALO, cb), F32)
    pad_scr[CONF_HALO:CONF_HALO + seq, :] = u
    pad_scr[CONF_HALO + seq:2 * CONF_HALO + seq, :] = jnp.zeros((CONF_HALO, cb), F32)
    win = CONF_ROWS + 2 * CONF_HALO
    first = CONF_HALO - CONF_KERNEL // 2

    def body(i, carry):
        r0 = pl.multiple_of(i * CONF_ROWS, CONF_ROWS)
        w = pad_scr[pl.ds(r0, win), :]
        acc = jnp.zeros((CONF_ROWS, cb), F32) + b_ref[...]
        for sub in range(SUBLANES):
            shifted = w if sub == 0 else pltpu.roll(w, win - sub, axis=0)
            for j in range(CONF_KERNEL):
                off = first + j
                if off % SUBLANES == sub:
                    base = off - sub
                    acc = acc + shifted[base:base + CONF_ROWS, :] * w_ref[j:j + 1, :]
        o_ref[pl.ds(r0, CONF_ROWS), :] = acc.astype(o_ref.dtype)
        return carry

    lax.fori_loop(0, seq // CONF_ROWS, body, 0)


def conformer_conv(proj, conf_dw, conf_dw_b, *, batch, seq, cb=256):
    c = conf_dw.shape[1]
    nb = c // cb
    taps = jnp.zeros((32, c), F32).at[:CONF_KERNEL].set(conf_dw)
    return pl.pallas_call(
        functools.partial(_conf_kernel, seq=seq),
        grid=(batch, nb),
        in_specs=[pl.BlockSpec((seq, cb), lambda b, j: (b, COL_GLU_A * nb + j)),
                  pl.BlockSpec((seq, cb), lambda b, j: (b, COL_GLU_G * nb + j)),
                  pl.BlockSpec((32, cb), lambda b, j: (0, j)),
                  pl.BlockSpec((1, cb), lambda b, j: (0, j))],
        out_specs=pl.BlockSpec((seq, cb), lambda b, j: (b, j)),
        out_shape=jax.ShapeDtypeStruct((batch * seq, c), BF16),
        scratch_shapes=[pltpu.VMEM((seq + 2 * CONF_HALO, cb), F32)],
        compiler_params=_params("arbitrary", "arbitrary"),
        name="conformer_conv",
    )(proj, proj, taps, conf_dw_b.reshape(1, c))


def _mixer_out_kernel(og_ref, uc_ref, ga_ref, gb_ref, x_ref, pos_ref, gm_ref, shf_ref, scf_ref,
                      gpost_ref, gpre_ref, lng_ref, lnb_ref, wa_ref, wb_ref, wo_ref, wr_ref, br_ref,
                      x2_ref, h2_ref, ti_ref, tw_ref, cnt_ref, cnt_scr, *, sub):
    parts = [slice(r, r + sub) for r in range(0, og_ref.shape[0], sub)]

    def rms(v):
        return v * lax.rsqrt(jnp.mean(v * v, axis=-1, keepdims=True) + EPS)

    def layer_norm_silu(uc):
        mu = jnp.mean(uc, axis=-1, keepdims=True)
        var = jnp.mean(jnp.square(uc - mu), axis=-1, keepdims=True)
        return _silu((uc - mu) * lax.rsqrt(var + EPS) * lng_ref[...] + lnb_ref[...]).astype(BF16)

    us = [layer_norm_silu(uc_ref[p, :].astype(F32)) for p in parts]
    yas = [jnp.dot(og_ref[p, :], wa_ref[...], preferred_element_type=F32) for p in parts]
    ybs = [jnp.dot(u, wb_ref[...], preferred_element_type=F32) for u in us]
    merged = [(jax.nn.sigmoid(ga_ref[p, :].astype(F32)) * ya
               + jax.nn.sigmoid(gb_ref[p, :].astype(F32)) * yb).astype(BF16) for p, ya, yb in zip(parts, yas, ybs)]
    ys = [jnp.dot(m, wo_ref[...], preferred_element_type=F32) for m in merged]
    x2s = [x_ref[p, :] + pos_ref[p, :] + gm_ref[0] * (rms(y) * gpost_ref[...]) for p, y in zip(parts, ys)]
    h2s = [(rms(x2) * gpre_ref[...] * (1.0 + scf_ref[0]) + shf_ref[0]).astype(BF16) for x2 in x2s]
    for p, x2, h2 in zip(parts, x2s, h2s):
        x2_ref[p, :] = x2
        h2_ref[p, :] = h2
    all_logits = [jnp.dot(h2, wr_ref[...], preferred_element_type=F32) + br_ref[...] for h2 in h2s]
    lane = lax.broadcasted_iota(jnp.int32, (sub, LANES), 1).astype(F32)
    ri = lax.broadcasted_iota(jnp.int32, (sub, sub), 0)
    ci = lax.broadcasted_iota(jnp.int32, (sub, sub), 1)
    before = jnp.where(ri > ci, 1.0, 0.0).astype(BF16)

    @pl.when(pl.program_id(0) == 0)
    def _():
        cnt_scr[...] = jnp.zeros_like(cnt_scr)

    routed = []
    for logits in all_logits:
        live = jnp.where(lane < N_EXPERTS, logits, -jnp.inf)
        top_v, top_i = [], []
        for _ in range(TOP_K):
            m = jnp.max(live, axis=-1, keepdims=True)
            idx = jnp.min(jnp.where(live == m, lane, float(LANES)), axis=-1, keepdims=True)
            top_v.append(m)
            top_i.append(idx)
            live = jnp.where(lane == idx, -jnp.inf, live)
        ex = [jnp.exp(v - top_v[0]) for v in top_v]
        denom = ex[0] + ex[1] + ex[2] + ex[3]
        picked = jnp.zeros((sub, LANES), F32)
        for k in range(TOP_K):
            picked = jnp.where(lane == top_i[k], 1.0, picked)
        routed.append((top_i, [e / denom for e in ex], picked,
                       jnp.dot(before, picked.astype(BF16), preferred_element_type=F32)))

    count = cnt_scr[...]
    for j, (p, (top_i, top_w, picked, within)) in enumerate(zip(parts, routed)):
        prefix = within + count[0:1, :]
        ti = jnp.zeros((sub, LANES), F32)
        tw = jnp.zeros((sub, LANES), F32)
        for k in range(TOP_K):
            rank = jnp.sum(jnp.where(lane == top_i[k], prefix, 0.0), axis=-1, keepdims=True)
            ti = jnp.where(lane == k, top_i[k], ti)
            ti = jnp.where(lane == TOP_K + k, rank, ti)
            tw = jnp.where(lane == k, top_w[k], tw)
        ti_ref[:, p] = jnp.transpose(ti)[:SUBLANES, :].astype(jnp.int32)
        tw_ref[:, p] = jnp.transpose(tw)[:SUBLANES, :]
        count = count + jnp.sum(picked, axis=0, keepdims=True)
        cnt_ref[j] = count
    cnt_scr[...] = count


def mixer_output(og, uc, proj, x, pos, mod, mod_row_of_tile, g_post, g_pre_f, ln_g, ln_b,
                 w_a, w_b, w_o, w_r, b_r, *, tm, sub):
    n, d = x.shape
    pos_tiles = pos.shape[0] // tm
    row = lambda i: (i, 0)
    fixed = lambda i: (0, 0)
    modspec = lambda k: pl.BlockSpec((1, 1, d), lambda i: (mod_row_of_tile(i), 0, k))
    vec = pl.BlockSpec((1, d), fixed)
    mat = pl.BlockSpec((d, d), fixed)
    return pl.pallas_call(
        functools.partial(_mixer_out_kernel, sub=sub),
        grid=(n // tm,),
        in_specs=[pl.BlockSpec((tm, d), row), pl.BlockSpec((tm, d), row),
                  pl.BlockSpec((tm, d), lambda i: (i, COL_GATE_A)),
                  pl.BlockSpec((tm, d), lambda i: (i, COL_GATE_B)),
                  pl.BlockSpec((tm, d), row),
                  pl.BlockSpec((tm, d), lambda i: (i % pos_tiles, 0)),
                  modspec(2), modspec(3), modspec(4),
                  vec, vec, vec, vec, mat, mat, mat,
                  pl.BlockSpec((d, LANES), fixed), pl.BlockSpec((1, LANES), fixed)],
        out_specs=[pl.BlockSpec((tm, d), row), pl.BlockSpec((tm, d), row),
                   pl.BlockSpec((SUBLANES, tm), lambda i: (0, i)), pl.BlockSpec((SUBLANES, tm), lambda i: (0, i)),
                   pl.BlockSpec((tm // sub, SUBLANES, LANES), lambda i: (i, 0, 0))],
        out_shape=[jax.ShapeDtypeStruct((n, d), F32), jax.ShapeDtypeStruct((n, d), BF16),
                   jax.ShapeDtypeStruct((SUBLANES, n), jnp.int32), jax.ShapeDtypeStruct((SUBLANES, n), F32),
                   jax.ShapeDtypeStruct((n // sub, SUBLANES, LANES), F32)],
        scratch_shapes=[pltpu.VMEM((SUBLANES, LANES), F32)],
        compiler_params=_params("arbitrary"),
        name="mixer_output",
    )(og, uc, proj, proj, x, pos, mod, mod, mod,
      g_post.reshape(1, d), g_pre_f.reshape(1, d), ln_g.reshape(1, d), ln_b.reshape(1, d),
      w_a, w_b, w_o, w_r, b_r)


EXPERT_BLOCK = 256
MOE_CHUNK = 512
GATHER_SUB = 256
COMBINE_GRANULE = 32
COMBINE_SLOTS = 32


def _expert_kernel(bexp_ref, slo_ref, shi_ref, nused_ref,
                   h2_ref, dest_ref, wt_ref, wgu_ref, bgu_ref, wd_ref, bd_ref, o_ref,
                   acc, rw_acc, wgu_bf, wd_bf):
    b = pl.program_id(0)
    expert = bexp_ref[b]

    @pl.when((b == 0) | (expert != bexp_ref[jnp.maximum(b - 1, 0)]))
    def _():
        wgu_bf[...] = wgu_ref[0].astype(BF16)
        wd_bf[...] = wd_ref[0].astype(BF16)

    @pl.when(b < nused_ref[0])
    def _():
        acc[...] = jnp.zeros_like(acc)
        rw_acc[...] = jnp.zeros_like(rw_acc)
        row = lax.broadcasted_iota(jnp.int32, (EXPERT_BLOCK, GATHER_SUB), 0)
        s_lo, s_hi = slo_ref[b], shi_ref[b]
        last_sub = h2_ref.shape[0] // GATHER_SUB - 1

        def select(s, live):
            t0 = pl.multiple_of(s * GATHER_SUB, GATHER_SUB)
            local = dest_ref[:, pl.ds(t0, GATHER_SUB)] - b * EXPERT_BLOCK
            if live is not True:
                local = jnp.where(live, local, -1)
            inside = (local >= 0) & (local < EXPERT_BLOCK)
            hit_row = jnp.sum(jnp.where(inside, local + 1, 0), axis=0, keepdims=True) - 1
            hit_w = jnp.sum(jnp.where(inside, wt_ref[:, pl.ds(t0, GATHER_SUB)], 0.0), axis=0, keepdims=True)
            hit = row == hit_row
            return (jnp.where(hit, 1.0, 0.0).astype(BF16), jnp.where(hit, hit_w, 0.0),
                    h2_ref[pl.ds(t0, GATHER_SUB), :])

        def gather(i, carry):
            s0 = s_lo + 2 * i
            p0, w0, x0 = select(s0, True)
            p1, w1, x1 = select(jnp.minimum(s0 + 1, last_sub), s0 + 1 <= s_hi)
            acc[...] += jnp.dot(jnp.concatenate([p0, p1], axis=1), jnp.concatenate([x0, x1], axis=0),
                                preferred_element_type=F32)
            rw_acc[...] += jnp.sum(w0 + w1, axis=1, keepdims=True)
            return carry

        lax.fori_loop(0, (s_hi - s_lo + 2) // 2, gather, 0)
        gu = jnp.dot(acc[...].astype(BF16), wgu_bf[...], preferred_element_type=F32) + bgu_ref[0]
        dff = gu.shape[1] // 2
        gl = jnp.minimum(gu[:, :dff], SWIGLU_LIMIT)
        lin = jnp.clip(gu[:, dff:], -SWIGLU_LIMIT, SWIGLU_LIMIT)
        act = (gl * jax.nn.sigmoid(SWIGLU_ALPHA * gl) * (lin + 1.0)).astype(BF16)
        y = jnp.dot(act, wd_bf[...], preferred_element_type=F32) + bd_ref[0]
        o_ref[...] = (y * rw_acc[...]).astype(o_ref.dtype)

    @pl.when(b >= nused_ref[0])
    def _():
        o_ref[...] = jnp.zeros_like(o_ref)


def expert_blocks(h2, dest_t, wt_t, block_expert, sub_lo, sub_hi, n_used, w_gate_up, b_gate_up, w_down, b_down):
    n, d = h2.shape
    e, _, f2 = w_gate_up.shape
    n_blocks = block_expert.shape[0]
    whole = lambda b, be, lo, hi, nu: (0, 0)
    by_expert = lambda b, be, lo, hi, nu: (be[b], 0, 0)
    once = pl.Buffered(1)
    grid_spec = pltpu.PrefetchScalarGridSpec(
        num_scalar_prefetch=4,
        grid=(n_blocks,),
        in_specs=[pl.BlockSpec((n, d), whole, pipeline_mode=once),
                  pl.BlockSpec((SUBLANES, n), whole, pipeline_mode=once),
                  pl.BlockSpec((SUBLANES, n), whole, pipeline_mode=once),
                  pl.BlockSpec((1, d, f2), by_expert, pipeline_mode=once),
                  pl.BlockSpec((1, 1, f2), by_expert),
                  pl.BlockSpec((1, f2 // 2, d), by_expert),
                  pl.BlockSpec((1, 1, d), by_expert)],
        out_specs=pl.BlockSpec((EXPERT_BLOCK, d), lambda b, be, lo, hi, nu: (b, 0)),
        scratch_shapes=[pltpu.VMEM((EXPERT_BLOCK, d), F32), pltpu.VMEM((EXPERT_BLOCK, 1), F32),
                        pltpu.VMEM((d, f2), BF16), pltpu.VMEM((f2 // 2, d), BF16)],
    )
    return pl.pallas_call(
        _expert_kernel,
        grid_spec=grid_spec,
        out_shape=jax.ShapeDtypeStruct((n_blocks * EXPERT_BLOCK, d), BF16),
        compiler_params=_params("arbitrary"),
        name="expert_blocks",
    )(block_expert, sub_lo, sub_hi, n_used, h2, dest_t, wt_t, w_gate_up, b_gate_up.reshape(e, 1, f2), w_down,
      b_down.reshape(e, 1, d))


def _combine_kernel(coff_ref, cgran_ref, yb_hbm, dest_ref, x2_ref, gf_ref, g_ref, o_ref, slab, sem, acc):
    c = pl.program_id(0)
    start = coff_ref[c]
    count = coff_ref[c + 1] - start
    n_groups = (count + COMBINE_SLOTS - 1) // COMBINE_SLOTS

    def granule_of(g, s):
        return cgran_ref[start + jnp.minimum(g * COMBINE_SLOTS + s, count - 1)]

    def copies(g, buf):
        return [pltpu.make_async_copy(
            yb_hbm.at[pl.ds(pl.multiple_of(granule_of(g, s) * COMBINE_GRANULE, COMBINE_GRANULE), COMBINE_GRANULE), :],
            slab.at[buf, pl.ds(s * COMBINE_GRANULE, COMBINE_GRANULE), :], sem.at[buf])
            for s in range(COMBINE_SLOTS)]

    for cp in copies(0, 0):
        cp.start()
    acc[...] = jnp.zeros_like(acc)
    chunk = x2_ref.shape[0]
    row = lax.broadcasted_iota(jnp.int32, (COMBINE_GRANULE, chunk), 0)
    dest_rows = dest_ref[...]
    n_rows = yb_hbm.shape[0]

    def group(g, carry):
        buf = g % 2

        @pl.when(g + 1 < n_groups)
        def _():
            for cp in copies(g + 1, 1 - buf):
                cp.start()

        for cp in copies(g, buf):
            cp.wait()
        pieces = []
        for s in range(COMBINE_SLOTS):
            live = g * COMBINE_SLOTS + s < count
            local = dest_rows - jnp.where(live, granule_of(g, s) * COMBINE_GRANULE, n_rows)
            inside = (local >= 0) & (local < COMBINE_GRANULE)
            hit_row = jnp.sum(jnp.where(inside, local + 1, 0), axis=0, keepdims=True) - 1
            pieces.append(jnp.where(row == hit_row, 1.0, 0.0).astype(BF16))
        onehot_t = jnp.concatenate(pieces, axis=0)
        acc[...] += lax.dot_general(onehot_t, slab[buf], (((0,), (0,)), ((), ())),
                                    preferred_element_type=F32)
        return carry

    lax.fori_loop(0, n_groups, group, 0)
    y = acc[...]
    yn = y * lax.rsqrt(jnp.mean(y * y, axis=-1, keepdims=True) + EPS) * g_ref[...]
    o_ref[...] = x2_ref[...] + gf_ref[0] * yn


def combine_residual(yb, dest, x2, chunk_off, chunk_granules, mod, mod_row_of_chunk, g_post_f):
    n, d = x2.shape
    chunk_row = lambda c, off, blk: (c, 0)
    grid_spec = pltpu.PrefetchScalarGridSpec(
        num_scalar_prefetch=2,
        grid=(n // MOE_CHUNK,),
        in_specs=[pl.BlockSpec(memory_space=pl.ANY),
                  pl.BlockSpec((SUBLANES, MOE_CHUNK), lambda c, off, blk: (0, c)),
                  pl.BlockSpec((MOE_CHUNK, d), chunk_row),
                  pl.BlockSpec((1, 1, d), lambda c, off, blk: (mod_row_of_chunk(c), 0, 5)),
                  pl.BlockSpec((1, d), lambda c, off, blk: (0, 0))],
        out_specs=pl.BlockSpec((MOE_CHUNK, d), chunk_row),
        scratch_shapes=[pltpu.VMEM((2, COMBINE_SLOTS * COMBINE_GRANULE, d), BF16),
                        pltpu.SemaphoreType.DMA((2,)),
                        pltpu.VMEM((MOE_CHUNK, d), F32)],
    )
    return pl.pallas_call(
        _combine_kernel,
        grid_spec=grid_spec,
        out_shape=jax.ShapeDtypeStruct((n, d), F32),
        compiler_params=_params("arbitrary"),
        name="combine_residual",
    )(chunk_off, chunk_granules, yb, dest, x2, mod, g_post_f.reshape(1, d))


def _grid_pos_embedding(rows, d):
    row = jnp.repeat(jnp.arange(rows, dtype=F32), GRID_W)
    col = jnp.tile(jnp.arange(GRID_W, dtype=F32), rows)
    quarter = d // 4
    omega = POS_BASE ** (-jnp.arange(quarter, dtype=F32) / quarter)

    def emb(p):
        ang = p[:, None] * omega[None, :]
        return jnp.concatenate([jnp.sin(ang), jnp.cos(ang)], axis=-1)

    return jnp.concatenate([emb(row), emb(col)], axis=-1)


def _count_le(sorted_vals, queries):
    return jnp.sum(sorted_vals <= queries[:, None], axis=1).astype(jnp.int32)


def _moe_plan(ti, cnt_after, n_tok, tm):
    assert tm == GATHER_SUB
    sub_after = cnt_after[:, 0, :N_EXPERTS].astype(jnp.int32)
    counts = sub_after[-1]
    padded = (counts + EXPERT_BLOCK - 1) // EXPERT_BLOCK * EXPERT_BLOCK
    padded_end = jnp.cumsum(padded)
    padded_start = padded_end - padded
    experts = jnp.arange(N_EXPERTS, dtype=jnp.int32)[:, None, None]
    start_of = jnp.sum(jnp.where(ti[None, :TOP_K] == experts, padded_start[:, None, None], 0), axis=0)
    dest_t = jnp.concatenate([start_of + ti[TOP_K:2 * TOP_K],
                              jnp.full((SUBLANES - TOP_K, n_tok), -1, jnp.int32)], axis=0)
    n_blocks = n_tok * TOP_K // EXPERT_BLOCK + N_EXPERTS
    blocks = jnp.arange(n_blocks, dtype=jnp.int32)
    block_expert = jnp.minimum(_count_le(padded_end[None, :], blocks * EXPERT_BLOCK), N_EXPERTS - 1)
    n_used = (padded_end[-1:] // EXPERT_BLOCK).astype(jnp.int32)
    r_lo = (blocks - padded_start[block_expert] // EXPERT_BLOCK) * EXPERT_BLOCK
    r_hi = jnp.minimum(counts[block_expert], r_lo + EXPERT_BLOCK) - 1
    through = sub_after.T[block_expert]
    n_sub = sub_after.shape[0]
    sub_lo = jnp.minimum(_count_le(through, r_lo), n_sub - 1)
    sub_hi = jnp.minimum(_count_le(through, r_hi), n_sub - 1)
    nch = n_tok // MOE_CHUNK
    per_chunk = MOE_CHUNK // tm
    cb_after = sub_after[per_chunk - 1::per_chunk]
    cb_before = jnp.concatenate([jnp.zeros((1, N_EXPERTS), jnp.int32), cb_after[:-1]], axis=0)
    b_lo = ((padded_start[None, :] + cb_before) // COMBINE_GRANULE).reshape(-1)
    b_hi = ((padded_start[None, :] + cb_after - 1) // COMBINE_GRANULE).reshape(-1)
    n_it = jnp.where((cb_after > cb_before).reshape(-1), b_hi - b_lo + 1, 0)
    off_end = jnp.cumsum(n_it)
    off = off_end - n_it
    n_items = n_blocks * (EXPERT_BLOCK // COMBINE_GRANULE) + N_EXPERTS * (nch - 1)
    w = jnp.minimum(jnp.arange(n_items, dtype=jnp.int32), off_end[-1] - 1)
    cell = jnp.minimum(_count_le(off_end[None, :], w), n_it.shape[0] - 1)
    chunk_granules = (b_lo[cell] + w - off[cell]).astype(jnp.int32)
    chunk_off = jnp.concatenate([jnp.zeros((1,), jnp.int32), off_end[N_EXPERTS - 1::N_EXPERTS]]).astype(jnp.int32)
    return dest_t, block_expert, sub_lo, sub_hi, n_used, chunk_off, chunk_granules


def kernel(x, c, ctx, c_ctx, w_mod, b_mod, g_pre_mix, g_post_mix, g_pre_ffn, g_post_ffn, w_in, conv_kv,
           conv_q, a_log, dt_bias, gdn_norm_g, w_proj_a, conf_dw, conf_dw_b, conf_ln_g, conf_ln_b,
           w_proj_b, w_out, w_router, b_router, w_gate_up, b_gate_up, w_down, b_down):
    batch, seq, d = x.shape
    ctx_len = ctx.shape[1]
    n_tok = batch * seq
    gw = HEADS * DV
    beta_off = HEADS * DK + gw
    state_cols = beta_off + 4 * HEADS
    q_off = state_cols

    mod_rows = 2 * SUBLANES
    cond = jnp.zeros((mod_rows, d), F32).at[:batch].set(c).at[batch].set(c_ctx)
    mod = modulation(cond, w_mod[0], b_mod[0]).reshape(mod_rows, 1, 6 * d)

    w = w_in[0]
    w_main = projection_weights(w, keep=beta_off, skip=state_cols - beta_off)
    w_ba = jnp.zeros((d, LANES), F32).at[:, :4 * HEADS].set(w[:, beta_off:state_cols]).astype(BF16)

    pos = _grid_pos_embedding(seq // GRID_W, d)
    x_flat = x.reshape(n_tok, d)
    tm = 1024
    tiles_per_seq = seq // tm
    proj, ba = input_projection(x_flat, pos, mod, lambda i: i // tiles_per_seq, g_pre_mix[0],
                                w_main, w_ba, tm=tm, tn=2048)
    proj_ctx, ba_ctx = input_projection(ctx.reshape(batch * ctx_len, d), None, mod, lambda i: batch,
                                        g_pre_mix[0], w_main[:, :beta_off], w_ba, tm=ctx_len, tn=1024)

    og = gated_deltanet(proj, ba, proj_ctx, ba_ctx, conv_kv[0], conv_q[0], a_log[0], dt_bias[0],
                        gdn_norm_g[0], batch=batch, seq=seq, ctx=ctx_len)
    uc = conformer_conv(proj, conf_dw[0], conf_dw_b[0], batch=batch, seq=seq)

    tm2 = 512
    w_r = jnp.zeros((d, LANES), F32).at[:, :N_EXPERTS].set(w_router[0]).astype(BF16)
    b_r = jnp.zeros((1, LANES), F32).at[0, :N_EXPERTS].set(b_router[0])
    x2, h2, top_i, top_w, cnt_after = mixer_output(
        og, uc, proj, x_flat, pos, mod, lambda i: i // (seq // tm2), g_post_mix[0], g_pre_ffn[0],
        conf_ln_g[0], conf_ln_b[0], w_proj_a[0].astype(BF16), w_proj_b[0].astype(BF16),
        w_out[0].astype(BF16), w_r, b_r, tm=tm2, sub=GATHER_SUB)

    dest_t, block_expert, sub_lo, sub_hi, n_used, chunk_off, chunk_blocks = _moe_plan(
        top_i, cnt_after, n_tok, GATHER_SUB)
    yb = expert_blocks(h2, dest_t, top_w, block_expert, sub_lo, sub_hi, n_used, w_gate_up[0], b_gate_up[0],
                       w_down[0], b_down[0])
    out = combine_residual(yb, dest_t, x2, chunk_off, chunk_blocks, mod, lambda c: c // (seq // MOE_CHUNK),
                           g_post_ffn[0])
    return out.reshape(batch, seq, d)
```

```python
import functools

import jax
import jax.numpy as jnp
from jax import lax
from jax.experimental import pallas as pl
from jax.experimental.pallas import tpu as pltpu

F32 = jnp.float32
BF16 = jnp.bfloat16

D_MODEL = 1024
GRID_W = 64
HEADS = 8
DK = 128
DV = 128
SHORT_CONV = 5
CHUNK = 64
CONF_KERNEL = 31
N_EXPERTS = 32
TOP_K = 4
SWIGLU_LIMIT = 7.0
SWIGLU_ALPHA = 1.702
MOE_BLOCK = 128
EPS = 1e-6
POS_BASE = 10000.0

LANES = 128
SUBLANES = 8
VMEM_LIMIT = 60 * 1024 * 1024

COL_K, COL_V, COL_Q, COL_Z, COL_GLU_A, COL_GLU_G, COL_GATE_A, COL_GATE_B = range(8)


def _params(*sem):
    return pltpu.CompilerParams(dimension_semantics=sem, vmem_limit_bytes=VMEM_LIMIT)


def _mod_kernel(c_ref, w_ref, b_ref, o_ref):
    c = c_ref[...]
    s = c * jax.nn.sigmoid(c)
    o_ref[...] = jnp.dot(s, w_ref[...], preferred_element_type=F32,
                         precision=lax.Precision.HIGHEST) + b_ref[...]


def modulation(cond, w_mod, b_mod):
    r, d = cond.shape
    n = w_mod.shape[1]
    tn = 1024
    return pl.pallas_call(
        _mod_kernel,
        grid=(n // tn,),
        in_specs=[pl.BlockSpec((r, d), lambda j: (0, 0)),
                  pl.BlockSpec((d, tn), lambda j: (0, j)),
                  pl.BlockSpec((1, tn), lambda j: (0, j))],
        out_specs=pl.BlockSpec((r, tn), lambda j: (0, j)),
        out_shape=jax.ShapeDtypeStruct((r, n), F32),
        compiler_params=_params("arbitrary"),
        name="modulation",
    )(cond, w_mod, b_mod.reshape(1, n))


def _proj_weights_kernel(a_ref, b_ref, o_ref, *, first_shifted, shift):
    @pl.when(pl.program_id(0) < first_shifted)
    def _():
        o_ref[...] = a_ref[...].astype(o_ref.dtype)

    @pl.when(pl.program_id(0) >= first_shifted)
    def _():
        o_ref[...] = jnp.concatenate([a_ref[:, shift:], b_ref[:, :shift]], axis=1).astype(o_ref.dtype)


def projection_weights(w, *, keep, skip, tn=1024):
    d, total = w.shape
    n_out = total - skip
    assert keep % tn == 0 and n_out % tn == 0 and 0 < skip < LANES
    per_tile = tn // LANES
    return pl.pallas_call(
        functools.partial(_proj_weights_kernel, first_shifted=keep // tn, shift=skip),
        grid=(n_out // tn,),
        in_specs=[pl.BlockSpec((d, tn), lambda j: (0, j)),
                  pl.BlockSpec((d, LANES), lambda j: (0, (j + 1) * per_tile))],
        out_specs=pl.BlockSpec((d, tn), lambda j: (0, j)),
        out_shape=jax.ShapeDtypeStruct((d, n_out), BF16),
        compiler_params=_params("arbitrary"),
        name="projection_weights",
    )(w, w)


def _inproj_kernel(*refs, has_pos):
    if has_pos:
        x_ref, pos_ref, sh_ref, sc_ref, g_ref, w_ref, wba_ref, o_ref, ba_ref, h_scr = refs
    else:
        x_ref, sh_ref, sc_ref, g_ref, w_ref, wba_ref, o_ref, ba_ref, h_scr = refs

    @pl.when(pl.program_id(1) == 0)
    def _():
        x = x_ref[...]
        if has_pos:
            x = x + pos_ref[...]
        y = x * lax.rsqrt(jnp.mean(x * x, axis=-1, keepdims=True) + EPS) * g_ref[...]
        h = (y * (1.0 + sc_ref[0]) + sh_ref[0]).astype(BF16)
        h_scr[...] = h
        ba_ref[...] = jnp.dot(h, wba_ref[...], preferred_element_type=F32)

    o_ref[...] = jnp.dot(h_scr[...], w_ref[...], preferred_element_type=F32).astype(o_ref.dtype)


def input_projection(x, pos, mod, mod_row_of_tile, g_pre, w_main, w_ba, *, tm, tn):
    n, d = x.shape
    w = w_main.shape[1]
    has_pos = pos is not None
    in_specs = [pl.BlockSpec((tm, d), lambda i, j: (i, 0))]
    args = [x]
    if has_pos:
        pos_tiles = pos.shape[0] // tm
        in_specs.append(pl.BlockSpec((tm, d), lambda i, j: (i % pos_tiles, 0)))
        args.append(pos)
    in_specs += [
        pl.BlockSpec((1, 1, d), lambda i, j: (mod_row_of_tile(i), 0, 0)),
        pl.BlockSpec((1, 1, d), lambda i, j: (mod_row_of_tile(i), 0, 1)),
        pl.BlockSpec((1, d), lambda i, j: (0, 0)),
        pl.BlockSpec((d, tn), lambda i, j: (0, j)),
        pl.BlockSpec((d, LANES), lambda i, j: (0, 0)),
    ]
    args += [mod, mod, g_pre.reshape(1, d), w_main, w_ba]
    return pl.pallas_call(
        functools.partial(_inproj_kernel, has_pos=has_pos),
        grid=(n // tm, w // tn),
        in_specs=in_specs,
        out_specs=[pl.BlockSpec((tm, tn), lambda i, j: (i, j)),
                   pl.BlockSpec((tm, LANES), lambda i, j: (i, 0))],
        out_shape=[jax.ShapeDtypeStruct((n, w), BF16), jax.ShapeDtypeStruct((n, LANES), F32)],
        scratch_shapes=[pltpu.VMEM((tm, d), BF16)],
        compiler_params=_params("arbitrary", "arbitrary"),
        name="input_projection_pos" if has_pos else "input_projection_ctx",
    )(*args)


def _silu(x):
    return x * jax.nn.sigmoid(x)


def _softplus(x):
    return jnp.maximum(x, 0.0) + jnp.log(1.0 + jnp.exp(-jnp.abs(x)))


def _short_conv(src, taps, pad_scr, rows):
    half = SHORT_CONV // 2
    pad_scr[0:SUBLANES, :] = jnp.zeros((SUBLANES, LANES), F32)
    pad_scr[SUBLANES:SUBLANES + rows, :] = src.astype(F32)
    pad_scr[SUBLANES + rows:2 * SUBLANES + rows, :] = jnp.zeros((SUBLANES, LANES), F32)
    acc = None
    for j in range(SHORT_CONV):
        off = SUBLANES + j - half
        term = pad_scr[off:off + rows, :] * taps[j:j + 1, :]
        acc = term if acc is None else acc + term
    return _silu(acc)


def _l2n(x):
    return x * lax.rsqrt(jnp.sum(x * x, axis=-1, keepdims=True) + EPS)


def _gdn_kernel(k_ref, v_ref, q_ref, z_ref, ba_ref, kc_ref, vc_ref, bac_ref,
                wk_ref, wv_ref, wq_ref, ab_ref, gn_ref, o_ref,
                kf, vf, qf, gates, pad_scr, lhs, val, att, ktt, dec, osc, *, seq, ctx, unroll, group):
    total = ctx + seq
    nc = total // CHUNK
    ncc = ctx // CHUNK
    two = 2 * CHUNK
    lg_chunk = CHUNK.bit_length() - 1

    ri = lax.broadcasted_iota(jnp.int32, (two, two), 0)
    ci = lax.broadcasted_iota(jnp.int32, (two, two), 1)
    same_dir = (ri >> lg_chunk) == (ci >> lg_chunk)
    incl = same_dir & (((ri < CHUNK) & (ri >= ci)) | ((ri >= CHUNK) & (ri <= ci)))
    strict = incl & (ri != ci)
    eye = jnp.where(ri == ci, 1.0, 0.0)
    tri = jnp.where(incl, 1.0, 0.0).astype(BF16)
    level_masks = []
    for lg in range(lg_chunk):
        same_parent = (ri >> (lg + 1)) == (ci >> (lg + 1))
        level_masks.append(jnp.where(same_parent & ((ri >> lg) != (ci >> lg)), 1.0, 0.0))
    top_rows = lax.broadcasted_iota(jnp.int32, (two, LANES), 0) < CHUNK

    def bwd_chunk(t):
        return jnp.where(t < ncc, ncc - 1 - t, nc + ncc - 1 - t)

    def stacked(ref, rf, rb):
        return jnp.concatenate([ref[pl.ds(rf, CHUNK), :], ref[pl.ds(rb, CHUNK), :]], axis=0)

    def chunk_load(t):
        rf = pl.multiple_of(t * CHUNK, CHUNK)
        rb = pl.multiple_of(bwd_chunk(t) * CHUNK, CHUNK)
        beta = jnp.concatenate([gates[0, pl.ds(rf, CHUNK), :], gates[1, pl.ds(rb, CHUNK), :]], axis=0)
        g = jnp.concatenate([gates[2, pl.ds(rf, CHUNK), :], gates[3, pl.ds(rb, CHUNK), :]], axis=0)
        return stacked(kf, rf, rb), stacked(vf, rf, rb), stacked(qf, rf, rb), beta, g

    def chunk_triangle(k, q, beta, g):
        kb = k.astype(BF16)
        kq = jnp.concatenate([kb, q.astype(BF16)], axis=0)
        kk_qk = lax.dot_general(kq, kb, (((1,), (1,)), ((), ())), preferred_element_type=F32)
        g_hi = g.astype(BF16)
        r1 = g - g_hi.astype(F32)
        g_mid = r1.astype(BF16)
        g_lo = (r1 - g_mid.astype(F32)).astype(BF16)
        gc3 = jnp.dot(tri, jnp.concatenate([g_hi, g_mid, g_lo], axis=1), preferred_element_type=F32)
        gc = gc3[:, :LANES] + gc3[:, LANES:2 * LANES] + gc3[:, 2 * LANES:]
        gc_row = jnp.transpose(gc)
        decay = jnp.where(incl, jnp.exp(gc - gc_row), 0.0)
        a = jnp.where(strict, beta * kk_qk[:two] * decay, 0.0)
        return a, (kk_qk[two:] * decay).astype(BF16), gc

    def chunk_finish(hh, t, k, v, q, beta, gc, attn, tmat):
        eg = jnp.exp(gc)
        rhs = jnp.concatenate([v * beta, k * beta * eg], axis=1).astype(BF16)
        sol = jnp.dot(tmat.astype(BF16), rhs, preferred_element_type=F32)
        g_end = jnp.where(top_rows, jnp.broadcast_to(gc[CHUNK - 1:CHUNK, :], (two, LANES)),
                          jnp.broadcast_to(gc[CHUNK:CHUNK + 1, :], (two, LANES)))
        k_tail = k * jnp.exp(g_end - gc)
        qg = (q * eg).astype(BF16)
        kcum = sol[:, DV:].astype(BF16)
        r0 = pl.multiple_of(t * two, two)
        lhs[hh, 0, pl.ds(r0, two), :] = jnp.concatenate([kcum[:CHUNK], qg[:CHUNK]], axis=0)
        lhs[hh, 1, pl.ds(r0, two), :] = jnp.concatenate([kcum[CHUNK:], qg[CHUNK:]], axis=0)
        val[hh, pl.ds(r0, two), :] = sol[:, :DV].astype(BF16)
        att[hh, pl.ds(r0, two), :] = attn
        ktt[hh, pl.ds(r0, two), :] = jnp.transpose(k_tail).astype(BF16)
        d0 = pl.multiple_of(t * SUBLANES, SUBLANES)
        e_end = jnp.exp(g_end)
        dec[hh, 0, pl.ds(d0, SUBLANES), :] = e_end[:SUBLANES]
        dec[hh, 1, pl.ds(d0, SUBLANES), :] = e_end[CHUNK:CHUNK + SUBLANES]

    for hh in range(group):
        head = pl.program_id(1) * group + hh
        cols = slice(hh * LANES, (hh + 1) * LANES)

        kf[0:ctx, :] = _l2n(_short_conv(kc_ref[:, cols], wk_ref[:, cols], pad_scr, ctx))
        vf[0:ctx, :] = _short_conv(vc_ref[:, cols], wv_ref[:, cols], pad_scr, ctx)
        qf[0:ctx, :] = jnp.zeros((ctx, LANES), F32)
        kf[ctx:total, :] = _l2n(_short_conv(k_ref[:, cols], wk_ref[:, cols], pad_scr, seq))
        vf[ctx:total, :] = _short_conv(v_ref[:, cols], wv_ref[:, cols], pad_scr, seq)
        qf[ctx:total, :] = _l2n(_short_conv(q_ref[:, cols], wq_ref[:, cols], pad_scr, seq)) * (DK ** -0.5)

        def gate_cols(src_ref, lo, rows):
            x = src_ref[...]
            lane = lax.broadcasted_iota(jnp.int32, (rows, LANES), 1)
            beta = jax.nn.sigmoid(x)
            g = -jnp.exp(ab_ref[0:1, :]) * _softplus(x + ab_ref[1:2, :])
            for slot, (arr, base) in enumerate(((beta, 0), (beta, HEADS), (g, 2 * HEADS), (g, 3 * HEADS))):
                col = jnp.sum(jnp.where(lane == base + head, arr, 0.0), axis=-1, keepdims=True)
                gates[slot, lo:lo + rows, :] = jnp.broadcast_to(col, (rows, LANES))

        gate_cols(bac_ref, 0, ctx)
        gate_cols(ba_ref, ctx, seq)

        def chunk_body(i, carry, hh=hh):
            steps = [i * unroll + u for u in range(unroll)]
            loaded = [chunk_load(t) for t in steps]
            tris = [chunk_triangle(k, q, beta, g) for (k, v, q, beta, g) in loaded]
            tmats = [eye - a * level_masks[0] for (a, _, _) in tris]
            for m in level_masks[1:]:
                nxt = []
                for (a, _, _), tmat in zip(tris, tmats):
                    cs = (a * m).astype(BF16)
                    tb = tmat.astype(BF16)
                    tc = jnp.dot(tb, cs, preferred_element_type=F32)
                    nxt.append(tmat - jnp.dot(tc.astype(BF16), tb, preferred_element_type=F32))
                tmats = nxt
            for t, (k, v, q, beta, g), (a, attn, gc), tmat in zip(steps, loaded, tris, tmats):
                chunk_finish(hh, t, k, v, q, beta, gc, attn, tmat)
            return carry

        lax.fori_loop(0, nc // unroll, chunk_body, 0)

    def scan_body(t, carry):
        rf = pl.multiple_of(t * CHUNK, CHUNK)
        rb = pl.multiple_of(bwd_chunk(t) * CHUNK, CHUNK)
        r0 = pl.multiple_of(t * two, two)
        d0 = pl.multiple_of(t * SUBLANES, SUBLANES)
        rs = [(jnp.dot(lhs[hh, 0, pl.ds(r0, two), :], carry[2 * hh].astype(BF16), preferred_element_type=F32),
               jnp.dot(lhs[hh, 1, pl.ds(r0, two), :], carry[2 * hh + 1].astype(BF16), preferred_element_type=F32))
              for hh in range(group)]
        v_news = [val[hh, pl.ds(r0, two), :].astype(F32) - jnp.concatenate([r_f[:CHUNK], r_b[:CHUNK]], axis=0)
                  for hh, (r_f, r_b) in enumerate(rs)]
        outs, new_states = [], []
        for hh, ((r_f, r_b), v_new) in enumerate(zip(rs, v_news)):
            zeros = jnp.zeros_like(v_new)
            v_bd = jnp.concatenate([jnp.where(top_rows, v_new, zeros), jnp.where(top_rows, zeros, v_new)],
                                   axis=1).astype(BF16)
            upd = jnp.dot(ktt[hh, pl.ds(r0, two), :], v_bd, preferred_element_type=F32)
            new_states.append(carry[2 * hh] * dec[hh, 0, pl.ds(d0, SUBLANES), :][0:1, :] + upd[:, :DV])
            new_states.append(carry[2 * hh + 1] * dec[hh, 1, pl.ds(d0, SUBLANES), :][0:1, :] + upd[:, DV:])
            outs.append(jnp.concatenate([r_f[CHUNK:], r_b[CHUNK:]], axis=0)
                        + jnp.dot(att[hh, pl.ds(r0, two), :], v_new.astype(BF16), preferred_element_type=F32))
        for hh, o in enumerate(outs):
            osc[hh, pl.ds(rf, CHUNK), :] += o[:CHUNK]
            osc[hh, pl.ds(rb, CHUNK), :] += o[CHUNK:]
        return tuple(new_states)

    osc[...] = jnp.zeros_like(osc)
    zero = jnp.zeros((DK, DV), F32)
    lax.fori_loop(0, nc, scan_body, (zero,) * (2 * group))

    for hh in range(group):
        cols = slice(hh * LANES, (hh + 1) * LANES)
        o = osc[hh, ctx:total, :]
        o = o * lax.rsqrt(jnp.mean(o * o, axis=-1, keepdims=True) + EPS) * gn_ref[...]
        o_ref[:, cols] = (o * _silu(z_ref[:, cols].astype(F32))).astype(o_ref.dtype)


def gated_deltanet(proj, ba, proj_ctx, ba_ctx, conv_kv, conv_q, a_log, dt_bias, gn, *, batch, seq, ctx,
                   unroll=12, group=4):
    total = seq + ctx
    nc = total // CHUNK
    gl = group * LANES
    ng = HEADS // group
    taps = jnp.zeros((SUBLANES, conv_kv.shape[1]), F32).at[:SHORT_CONV].set(conv_kv)
    taps_q = jnp.zeros((SUBLANES, conv_q.shape[1]), F32).at[:SHORT_CONV].set(conv_q)
    ab = jnp.zeros((SUBLANES, LANES), F32)
    ab = ab.at[0, 2 * HEADS:4 * HEADS].set(a_log.reshape(-1)).at[1, 2 * HEADS:4 * HEADS].set(dt_bias.reshape(-1))
    col = lambda base: (lambda b, h: (b, base * ng + h))
    once = pl.Buffered(1)
    return pl.pallas_call(
        functools.partial(_gdn_kernel, seq=seq, ctx=ctx, unroll=unroll, group=group),
        grid=(batch, ng),
        in_specs=[
            pl.BlockSpec((seq, gl), col(COL_K), pipeline_mode=once),
            pl.BlockSpec((seq, gl), col(COL_V), pipeline_mode=once),
            pl.BlockSpec((seq, gl), col(COL_Q), pipeline_mode=once),
            pl.BlockSpec((seq, gl), col(COL_Z), pipeline_mode=once),
            pl.BlockSpec((seq, LANES), lambda b, h: (b, 0), pipeline_mode=once),
            pl.BlockSpec((ctx, gl), col(COL_K)),
            pl.BlockSpec((ctx, gl), col(COL_V)),
            pl.BlockSpec((ctx, LANES), lambda b, h: (b, 0)),
            pl.BlockSpec((SUBLANES, gl), lambda b, h: (0, h)),
            pl.BlockSpec((SUBLANES, gl), lambda b, h: (0, ng + h)),
            pl.BlockSpec((SUBLANES, gl), lambda b, h: (0, h)),
            pl.BlockSpec((SUBLANES, LANES), lambda b, h: (0, 0)),
            pl.BlockSpec((1, LANES), lambda b, h: (0, 0)),
        ],
        out_specs=pl.BlockSpec((seq, gl), lambda b, h: (b, h)),
        out_shape=jax.ShapeDtypeStruct((batch * seq, HEADS * DV), BF16),
        scratch_shapes=[
            pltpu.VMEM((total, LANES), F32),
            pltpu.VMEM((total, LANES), F32),
            pltpu.VMEM((total, LANES), F32),
            pltpu.VMEM((4, total, LANES), F32),
            pltpu.VMEM((seq + 2 * SUBLANES, LANES), F32),
            pltpu.VMEM((group, 2, nc * 2 * CHUNK, LANES), BF16),
            pltpu.VMEM((group, nc * 2 * CHUNK, LANES), BF16),
            pltpu.VMEM((group, nc * 2 * CHUNK, LANES), BF16),
            pltpu.VMEM((group, nc * 2 * CHUNK, LANES), BF16),
            pltpu.VMEM((group, 2, nc * SUBLANES, LANES), F32),
            pltpu.VMEM((group, total, LANES), F32),
        ],
        compiler_params=_params("arbitrary", "arbitrary"),
        name="gated_deltanet",
    )(proj, proj, proj, proj, ba, proj_ctx, proj_ctx, ba_ctx, taps, taps, taps_q, ab, gn.reshape(1, DV))


CONF_ROWS = 128
CONF_HALO = 16


def _conf_kernel(a_ref, g_ref, w_ref, b_ref, o_ref, pad_scr, *, seq):
    cb = a_ref.shape[1]
    u = a_ref[...].astype(F32) * jax.nn.sigmoid(g_ref[...].astype(F32))
    pad_scr[0:CONF_HALO, :] = jnp.zeros((CONF_HALO, cb), F32)
    pad_scr[CONF_HALO:CONF_HALO + seq, :] = u
    pad_scr[CONF_HALO + seq:2 * CONF_HALO + seq, :] = jnp.zeros((CONF_HALO, cb), F32)
    win = CONF_ROWS + 2 * CONF_HALO
    first = CONF_HALO - CONF_KERNEL // 2

    def body(i, carry):
        r0 = pl.multiple_of(i * CONF_ROWS, CONF_ROWS)
        w = pad_scr[pl.ds(r0, win), :]
        acc = jnp.zeros((CONF_ROWS, cb), F32) + b_ref[...]
        for sub in range(SUBLANES):
            shifted = w if sub == 0 else pltpu.roll(w, win - sub, axis=0)
            for j in range(CONF_KERNEL):
                off = first + j
                if off % SUBLANES == sub:
                    base = off - sub
                    acc = acc + shifted[base:base + CONF_ROWS, :] * w_ref[j:j + 1, :]
        o_ref[pl.ds(r0, CONF_ROWS), :] = acc.astype(o_ref.dtype)
        return carry

    lax.fori_loop(0, seq // CONF_ROWS, body, 0)


def conformer_conv(proj, conf_dw, conf_dw_b, *, batch, seq, cb=256):
    c = conf_dw.shape[1]
    nb = c // cb
    taps = jnp.zeros((32, c), F32).at[:CONF_KERNEL].set(conf_dw)
    return pl.pallas_call(
        functools.partial(_conf_kernel, seq=seq),
        grid=(batch, nb),
        in_specs=[pl.BlockSpec((seq, cb), lambda b, j: (b, COL_GLU_A * nb + j)),
                  pl.BlockSpec((seq, cb), lambda b, j: (b, COL_GLU_G * nb + j)),
                  pl.BlockSpec((32, cb), lambda b, j: (0, j)),
                  pl.BlockSpec((1, cb), lambda b, j: (0, j))],
        out_specs=pl.BlockSpec((seq, cb), lambda b, j: (b, j)),
        out_shape=jax.ShapeDtypeStruct((batch * seq, c), BF16),
        scratch_shapes=[pltpu.VMEM((seq + 2 * CONF_HALO, cb), F32)],
        compiler_params=_params("arbitrary", "arbitrary"),
        name="conformer_conv",
    )(proj, proj, taps, conf_dw_b.reshape(1, c))


def _mixer_out_kernel(og_ref, uc_ref, ga_ref, gb_ref, x_ref, pos_ref, gm_ref, shf_ref, scf_ref,
                      gpost_ref, gpre_ref, lng_ref, lnb_ref, wa_ref, wb_ref, wo_ref, wr_ref, br_ref,
                      x2_ref, h2_ref, ti_ref, tw_ref, cnt_ref, cnt_scr, *, sub):
    parts = [slice(r, r + sub) for r in range(0, og_ref.shape[0], sub)]

    def rms(v):
        return v * lax.rsqrt(jnp.mean(v * v, axis=-1, keepdims=True) + EPS)

    def layer_norm_silu(uc):
        mu = jnp.mean(uc, axis=-1, keepdims=True)
        var = jnp.mean(jnp.square(uc - mu), axis=-1, keepdims=True)
        return _silu((uc - mu) * lax.rsqrt(var + EPS) * lng_ref[...] + lnb_ref[...]).astype(BF16)

    us = [layer_norm_silu(uc_ref[p, :].astype(F32)) for p in parts]
    yas = [jnp.dot(og_ref[p, :], wa_ref[...], preferred_element_type=F32) for p in parts]
    ybs = [jnp.dot(u, wb_ref[...], preferred_element_type=F32) for u in us]
    merged = [(jax.nn.sigmoid(ga_ref[p, :].astype(F32)) * ya
               + jax.nn.sigmoid(gb_ref[p, :].astype(F32)) * yb).astype(BF16) for p, ya, yb in zip(parts, yas, ybs)]
    ys = [jnp.dot(m, wo_ref[...], preferred_element_type=F32) for m in merged]
    x2s = [x_ref[p, :] + pos_ref[p, :] + gm_ref[0] * (rms(y) * gpost_ref[...]) for p, y in zip(parts, ys)]
    h2s = [(rms(x2) * gpre_ref[...] * (1.0 + scf_ref[0]) + shf_ref[0]).astype(BF16) for x2 in x2s]
    for p, x2, h2 in zip(parts, x2s, h2s):
        x2_ref[p, :] = x2
        h2_ref[p, :] = h2
    all_logits = [jnp.dot(h2, wr_ref[...], preferred_element_type=F32) + br_ref[...] for h2 in h2s]
    lane = lax.broadcasted_iota(jnp.int32, (sub, LANES), 1).astype(F32)
    ri = lax.broadcasted_iota(jnp.int32, (sub, sub), 0)
    ci = lax.broadcasted_iota(jnp.int32, (sub, sub), 1)
    before = jnp.where(ri > ci, 1.0, 0.0).astype(BF16)

    @pl.when(pl.program_id(0) == 0)
    def _():
        cnt_scr[...] = jnp.zeros_like(cnt_scr)

    routed = []
    for logits in all_logits:
        live = jnp.where(lane < N_EXPERTS, logits, -jnp.inf)
        top_v, top_i = [], []
        for _ in range(TOP_K):
            m = jnp.max(live, axis=-1, keepdims=True)
            idx = jnp.min(jnp.where(live == m, lane, float(LANES)), axis=-1, keepdims=True)
            top_v.append(m)
            top_i.append(idx)
            live = jnp.where(lane == idx, -jnp.inf, live)
        ex = [jnp.exp(v - top_v[0]) for v in top_v]
        denom = ex[0] + ex[1] + ex[2] + ex[3]
        picked = jnp.zeros((sub, LANES), F32)
        for k in range(TOP_K):
            picked = jnp.where(lane == top_i[k], 1.0, picked)
        routed.append((top_i, [e / denom for e in ex], picked,
                       jnp.dot(before, picked.astype(BF16), preferred_element_type=F32)))

    count = cnt_scr[...]
    for j, (p, (top_i, top_w, picked, within)) in enumerate(zip(parts, routed)):
        prefix = within + count[0:1, :]
        ti = jnp.zeros((sub, LANES), F32)
        tw = jnp.zeros((sub, LANES), F32)
        for k in range(TOP_K):
            rank = jnp.sum(jnp.where(lane == top_i[k], prefix, 0.0), axis=-1, keepdims=True)
            ti = jnp.where(lane == k, top_i[k], ti)
            ti = jnp.where(lane == TOP_K + k, rank, ti)
            tw = jnp.where(lane == k, top_w[k], tw)
        ti_ref[:, p] = jnp.transpose(ti)[:SUBLANES, :].astype(jnp.int32)
        tw_ref[:, p] = jnp.transpose(tw)[:SUBLANES, :]
        count = count + jnp.sum(picked, axis=0, keepdims=True)
        cnt_ref[j] = count
    cnt_scr[...] = count


def mixer_output(og, uc, proj, x, pos, mod, mod_row_of_tile, g_post, g_pre_f, ln_g, ln_b,
                 w_a, w_b, w_o, w_r, b_r, *, tm, sub):
    n, d = x.shape
    pos_tiles = pos.shape[0] // tm
    row = lambda i: (i, 0)
    fixed = lambda i: (0, 0)
    modspec = lambda k: pl.BlockSpec((1, 1, d), lambda i: (mod_row_of_tile(i), 0, k))
    vec = pl.BlockSpec((1, d), fixed)
    mat = pl.BlockSpec((d, d), fixed)
    return pl.pallas_call(
        functools.partial(_mixer_out_kernel, sub=sub),
        grid=(n // tm,),
        in_specs=[pl.BlockSpec((tm, d), row), pl.BlockSpec((tm, d), row),
                  pl.BlockSpec((tm, d), lambda i: (i, COL_GATE_A)),
                  pl.BlockSpec((tm, d), lambda i: (i, COL_GATE_B)),
                  pl.BlockSpec((tm, d), row),
                  pl.BlockSpec((tm, d), lambda i: (i % pos_tiles, 0)),
                  modspec(2), modspec(3), modspec(4),
                  vec, vec, vec, vec, mat, mat, mat,
                  pl.BlockSpec((d, LANES), fixed), pl.BlockSpec((1, LANES), fixed)],
        out_specs=[pl.BlockSpec((tm, d), row), pl.BlockSpec((tm, d), row),
                   pl.BlockSpec((SUBLANES, tm), lambda i: (0, i)), pl.BlockSpec((SUBLANES, tm), lambda i: (0, i)),
                   pl.BlockSpec((tm // sub, SUBLANES, LANES), lambda i: (i, 0, 0))],
        out_shape=[jax.ShapeDtypeStruct((n, d), F32), jax.ShapeDtypeStruct((n, d), BF16),
                   jax.ShapeDtypeStruct((SUBLANES, n), jnp.int32), jax.ShapeDtypeStruct((SUBLANES, n), F32),
                   jax.ShapeDtypeStruct((n // sub, SUBLANES, LANES), F32)],
        scratch_shapes=[pltpu.VMEM((SUBLANES, LANES), F32)],
        compiler_params=_params("arbitrary"),
        name="mixer_output",
    )(og, uc, proj, proj, x, pos, mod, mod, mod,
      g_post.reshape(1, d), g_pre_f.reshape(1, d), ln_g.reshape(1, d), ln_b.reshape(1, d),
      w_a, w_b, w_o, w_r, b_r)


EXPERT_BLOCK = 256
MOE_CHUNK = 512
GATHER_SUB = 256
COMBINE_GRANULE = 32
COMBINE_SLOTS = 32


def _expert_kernel(bexp_ref, rlo_ref, slo_ref, shi_ref, nused_ref,
                   h2_ref, ti_ref, wt_ref, wgu_ref, bgu_ref, wd_ref, bd_ref, o_ref,
                   acc, rw_acc, wgu_bf, wd_bf):
    b = pl.program_id(0)
    expert = bexp_ref[b]

    @pl.when((b == 0) | (expert != bexp_ref[jnp.maximum(b - 1, 0)]))
    def _():
        wgu_bf[...] = wgu_ref[0].astype(BF16)
        wd_bf[...] = wd_ref[0].astype(BF16)

    @pl.when(b < nused_ref[0])
    def _():
        acc[...] = jnp.zeros_like(acc)
        rw_acc[...] = jnp.zeros_like(rw_acc)
        row = lax.broadcasted_iota(jnp.int32, (EXPERT_BLOCK, GATHER_SUB), 0)
        s_lo, s_hi = slo_ref[b], shi_ref[b]
        last_sub = h2_ref.shape[0] // GATHER_SUB - 1

        def select(s, live):
            t0 = pl.multiple_of(s * GATHER_SUB, GATHER_SUB)
            ids = ti_ref[0:TOP_K, pl.ds(t0, GATHER_SUB)]
            local = ti_ref[TOP_K:2 * TOP_K, pl.ds(t0, GATHER_SUB)] - rlo_ref[b]
            if live is not True:
                local = jnp.where(live, local, -1)
            inside = (ids == expert) & (local >= 0) & (local < EXPERT_BLOCK)
            hit_row = jnp.sum(jnp.where(inside, local + 1, 0), axis=0, keepdims=True) - 1
            hit_w = jnp.sum(jnp.where(inside, wt_ref[0:TOP_K, pl.ds(t0, GATHER_SUB)], 0.0), axis=0, keepdims=True)
            hit = row == hit_row
            return (jnp.where(hit, 1.0, 0.0).astype(BF16), jnp.where(hit, hit_w, 0.0),
                    h2_ref[pl.ds(t0, GATHER_SUB), :])

        def gather(i, carry):
            s0 = s_lo + 2 * i
            p0, w0, x0 = select(s0, True)
            p1, w1, x1 = select(jnp.minimum(s0 + 1, last_sub), s0 + 1 <= s_hi)
            acc[...] += jnp.dot(jnp.concatenate([p0, p1], axis=1), jnp.concatenate([x0, x1], axis=0),
                                preferred_element_type=F32)
            rw_acc[...] += jnp.sum(w0 + w1, axis=1, keepdims=True)
            return carry

        lax.fori_loop(0, (s_hi - s_lo + 2) // 2, gather, 0)
        gu = jnp.dot(acc[...].astype(BF16), wgu_bf[...], preferred_element_type=F32) + bgu_ref[0]
        dff = gu.shape[1] // 2
        gl = jnp.minimum(gu[:, :dff], SWIGLU_LIMIT)
        lin = jnp.clip(gu[:, dff:], -SWIGLU_LIMIT, SWIGLU_LIMIT)
        act = (gl * jax.nn.sigmoid(SWIGLU_ALPHA * gl) * (lin + 1.0)).astype(BF16)
        y = jnp.dot(act, wd_bf[...], preferred_element_type=F32) + bd_ref[0]
        o_ref[...] = (y * rw_acc[...]).astype(o_ref.dtype)

    @pl.when(b >= nused_ref[0])
    def _():
        o_ref[...] = jnp.zeros_like(o_ref)


def expert_blocks(h2, ti_t, wt_t, block_expert, rank_lo, sub_lo, sub_hi, n_used, w_gate_up, b_gate_up, w_down,
                  b_down):
    n, d = h2.shape
    e, _, f2 = w_gate_up.shape
    n_blocks = block_expert.shape[0]
    whole = lambda b, be, rl, lo, hi, nu: (0, 0)
    by_expert = lambda b, be, rl, lo, hi, nu: (be[b], 0, 0)
    once = pl.Buffered(1)
    grid_spec = pltpu.PrefetchScalarGridSpec(
        num_scalar_prefetch=5,
        grid=(n_blocks,),
        in_specs=[pl.BlockSpec((n, d), whole, pipeline_mode=once),
                  pl.BlockSpec((SUBLANES, n), whole, pipeline_mode=once),
                  pl.BlockSpec((SUBLANES, n), whole, pipeline_mode=once),
                  pl.BlockSpec((1, d, f2), by_expert, pipeline_mode=once),
                  pl.BlockSpec((1, 1, f2), by_expert),
                  pl.BlockSpec((1, f2 // 2, d), by_expert),
                  pl.BlockSpec((1, 1, d), by_expert)],
        out_specs=pl.BlockSpec((EXPERT_BLOCK, d), lambda b, be, rl, lo, hi, nu: (b, 0)),
        scratch_shapes=[pltpu.VMEM((EXPERT_BLOCK, d), F32), pltpu.VMEM((EXPERT_BLOCK, 1), F32),
                        pltpu.VMEM((d, f2), BF16), pltpu.VMEM((f2 // 2, d), BF16)],
    )
    return pl.pallas_call(
        _expert_kernel,
        grid_spec=grid_spec,
        out_shape=jax.ShapeDtypeStruct((n_blocks * EXPERT_BLOCK, d), BF16),
        compiler_params=_params("arbitrary"),
        name="expert_blocks",
    )(block_expert, rank_lo, sub_lo, sub_hi, n_used, h2, ti_t, wt_t, w_gate_up, b_gate_up.reshape(e, 1, f2), w_down,
      b_down.reshape(e, 1, d))


def _combine_kernel(coff_ref, cgran_ref, cexp_ref, cbase_ref, yb_hbm, ti_ref, x2_ref, gf_ref, g_ref, o_ref,
                    slab, sem, acc):
    c = pl.program_id(0)
    start = coff_ref[c]
    count = coff_ref[c + 1] - start
    n_groups = (count + COMBINE_SLOTS - 1) // COMBINE_SLOTS

    def item(g, s):
        return start + jnp.minimum(g * COMBINE_SLOTS + s, count - 1)

    def granule_of(g, s):
        return cgran_ref[item(g, s)]

    def copies(g, buf):
        return [pltpu.make_async_copy(
            yb_hbm.at[pl.ds(pl.multiple_of(granule_of(g, s) * COMBINE_GRANULE, COMBINE_GRANULE), COMBINE_GRANULE), :],
            slab.at[buf, pl.ds(s * COMBINE_GRANULE, COMBINE_GRANULE), :], sem.at[buf])
            for s in range(COMBINE_SLOTS)]

    for cp in copies(0, 0):
        cp.start()
    acc[...] = jnp.zeros_like(acc)
    chunk = x2_ref.shape[0]
    row = lax.broadcasted_iota(jnp.int32, (COMBINE_GRANULE, chunk), 0)
    ids = ti_ref[0:TOP_K, :]
    ranks = ti_ref[TOP_K:2 * TOP_K, :]

    def group(g, carry):
        buf = g % 2

        @pl.when(g + 1 < n_groups)
        def _():
            for cp in copies(g + 1, 1 - buf):
                cp.start()

        for cp in copies(g, buf):
            cp.wait()
        pieces = []
        for s in range(COMBINE_SLOTS):
            live = g * COMBINE_SLOTS + s < count
            local = ranks - cbase_ref[item(g, s)]
            inside = (ids == jnp.where(live, cexp_ref[item(g, s)], -1)) & (local >= 0) & (local < COMBINE_GRANULE)
            hit_row = jnp.sum(jnp.where(inside, local + 1, 0), axis=0, keepdims=True) - 1
            pieces.append(jnp.where(row == hit_row, 1.0, 0.0).astype(BF16))
        onehot_t = jnp.concatenate(pieces, axis=0)
        acc[...] += lax.dot_general(onehot_t, slab[buf], (((0,), (0,)), ((), ())),
                                    preferred_element_type=F32)
        return carry

    lax.fori_loop(0, n_groups, group, 0)
    y = acc[...]
    yn = y * lax.rsqrt(jnp.mean(y * y, axis=-1, keepdims=True) + EPS) * g_ref[...]
    o_ref[...] = x2_ref[...] + gf_ref[0] * yn


def combine_residual(yb, ti_t, x2, chunk_off, chunk_items, mod, mod_row_of_chunk, g_post_f):
    n, d = x2.shape
    chunk_row = lambda c, off, gran, exp, base: (c, 0)
    grid_spec = pltpu.PrefetchScalarGridSpec(
        num_scalar_prefetch=4,
        grid=(n // MOE_CHUNK,),
        in_specs=[pl.BlockSpec(memory_space=pl.ANY),
                  pl.BlockSpec((SUBLANES, MOE_CHUNK), lambda c, off, gran, exp, base: (0, c)),
                  pl.BlockSpec((MOE_CHUNK, d), chunk_row),
                  pl.BlockSpec((1, 1, d), lambda c, off, gran, exp, base: (mod_row_of_chunk(c), 0, 5)),
                  pl.BlockSpec((1, d), lambda c, off, gran, exp, base: (0, 0))],
        out_specs=pl.BlockSpec((MOE_CHUNK, d), chunk_row),
        scratch_shapes=[pltpu.VMEM((2, COMBINE_SLOTS * COMBINE_GRANULE, d), BF16),
                        pltpu.SemaphoreType.DMA((2,)),
                        pltpu.VMEM((MOE_CHUNK, d), F32)],
    )
    return pl.pallas_call(
        _combine_kernel,
        grid_spec=grid_spec,
        out_shape=jax.ShapeDtypeStruct((n, d), F32),
        compiler_params=_params("arbitrary"),
        name="combine_residual",
    )(chunk_off, *chunk_items, yb, ti_t, x2, mod, g_post_f.reshape(1, d))


def _grid_pos_embedding(rows, d):
    row = jnp.repeat(jnp.arange(rows, dtype=F32), GRID_W)
    col = jnp.tile(jnp.arange(GRID_W, dtype=F32), rows)
    quarter = d // 4
    omega = POS_BASE ** (-jnp.arange(quarter, dtype=F32) / quarter)

    def emb(p):
        ang = p[:, None] * omega[None, :]
        return jnp.concatenate([jnp.sin(ang), jnp.cos(ang)], axis=-1)

    return jnp.concatenate([emb(row), emb(col)], axis=-1)


def _count_le(sorted_vals, queries):
    return jnp.sum(sorted_vals <= queries[:, None], axis=1).astype(jnp.int32)


def _moe_plan(cnt_after, n_tok, tm):
    assert tm == GATHER_SUB
    sub_after = cnt_after[:, 0, :N_EXPERTS].astype(jnp.int32)
    counts = sub_after[-1]
    padded = (counts + EXPERT_BLOCK - 1) // EXPERT_BLOCK * EXPERT_BLOCK
    padded_end = jnp.cumsum(padded)
    padded_start = padded_end - padded
    n_blocks = n_tok * TOP_K // EXPERT_BLOCK + N_EXPERTS
    blocks = jnp.arange(n_blocks, dtype=jnp.int32)
    block_expert = jnp.minimum(_count_le(padded_end[None, :], blocks * EXPERT_BLOCK), N_EXPERTS - 1)
    n_used = (padded_end[-1:] // EXPERT_BLOCK).astype(jnp.int32)
    r_lo = (blocks - padded_start[block_expert] // EXPERT_BLOCK) * EXPERT_BLOCK
    r_hi = jnp.minimum(counts[block_expert], r_lo + EXPERT_BLOCK) - 1
    through = sub_after.T[block_expert]
    n_sub = sub_after.shape[0]
    sub_lo = jnp.minimum(_count_le(through, r_lo), n_sub - 1)
    sub_hi = jnp.minimum(_count_le(through, r_hi), n_sub - 1)
    nch = n_tok // MOE_CHUNK
    per_chunk = MOE_CHUNK // tm
    cb_after = sub_after[per_chunk - 1::per_chunk]
    cb_before = jnp.concatenate([jnp.zeros((1, N_EXPERTS), jnp.int32), cb_after[:-1]], axis=0)
    b_lo = ((padded_start[None, :] + cb_before) // COMBINE_GRANULE).reshape(-1)
    b_hi = ((padded_start[None, :] + cb_after - 1) // COMBINE_GRANULE).reshape(-1)
    n_it = jnp.where((cb_after > cb_before).reshape(-1), b_hi - b_lo + 1, 0)
    off_end = jnp.cumsum(n_it)
    off = off_end - n_it
    n_items = n_blocks * (EXPERT_BLOCK // COMBINE_GRANULE) + N_EXPERTS * (nch - 1)
    w = jnp.minimum(jnp.arange(n_items, dtype=jnp.int32), off_end[-1] - 1)
    cell = jnp.minimum(_count_le(off_end[None, :], w), n_it.shape[0] - 1)
    chunk_granules = (b_lo[cell] + w - off[cell]).astype(jnp.int32)
    chunk_expert = (cell % N_EXPERTS).astype(jnp.int32)
    chunk_base = chunk_granules * COMBINE_GRANULE - padded_start[chunk_expert]
    chunk_off = jnp.concatenate([jnp.zeros((1,), jnp.int32), off_end[N_EXPERTS - 1::N_EXPERTS]]).astype(jnp.int32)
    return block_expert, r_lo, sub_lo, sub_hi, n_used, chunk_off, (chunk_granules, chunk_expert, chunk_base)


def kernel(x, c, ctx, c_ctx, w_mod, b_mod, g_pre_mix, g_post_mix, g_pre_ffn, g_post_ffn, w_in, conv_kv,
           conv_q, a_log, dt_bias, gdn_norm_g, w_proj_a, conf_dw, conf_dw_b, conf_ln_g, conf_ln_b,
           w_proj_b, w_out, w_router, b_router, w_gate_up, b_gate_up, w_down, b_down):
    batch, seq, d = x.shape
    ctx_len = ctx.shape[1]
    n_tok = batch * seq
    gw = HEADS * DV
    beta_off = HEADS * DK + gw
    state_cols = beta_off + 4 * HEADS
    q_off = state_cols

    mod_rows = 2 * SUBLANES
    cond = jnp.zeros((mod_rows, d), F32).at[:batch].set(c).at[batch].set(c_ctx)
    mod = modulation(cond, w_mod[0], b_mod[0]).reshape(mod_rows, 1, 6 * d)

    w = w_in[0]
    w_main = projection_weights(w, keep=beta_off, skip=state_cols - beta_off)
    w_ba = jnp.zeros((d, LANES), F32).at[:, :4 * HEADS].set(w[:, beta_off:state_cols]).astype(BF16)

    pos = _grid_pos_embedding(seq // GRID_W, d)
    x_flat = x.reshape(n_tok, d)
    tm = 1024
    tiles_per_seq = seq // tm
    proj, ba = input_projection(x_flat, pos, mod, lambda i: i // tiles_per_seq, g_pre_mix[0],
                                w_main, w_ba, tm=tm, tn=2048)
    proj_ctx, ba_ctx = input_projection(ctx.reshape(batch * ctx_len, d), None, mod, lambda i: batch,
                                        g_pre_mix[0], w_main[:, :beta_off], w_ba, tm=ctx_len, tn=1024)

    og = gated_deltanet(proj, ba, proj_ctx, ba_ctx, conv_kv[0], conv_q[0], a_log[0], dt_bias[0],
                        gdn_norm_g[0], batch=batch, seq=seq, ctx=ctx_len)
    uc = conformer_conv(proj, conf_dw[0], conf_dw_b[0], batch=batch, seq=seq)

    tm2 = 512
    w_r = jnp.zeros((d, LANES), F32).at[:, :N_EXPERTS].set(w_router[0]).astype(BF16)
    b_r = jnp.zeros((1, LANES), F32).at[0, :N_EXPERTS].set(b_router[0])
    x2, h2, top_i, top_w, cnt_after = mixer_output(
        og, uc, proj, x_flat, pos, mod, lambda i: i // (seq // tm2), g_post_mix[0], g_pre_ffn[0],
        conf_ln_g[0], conf_ln_b[0], w_proj_a[0].astype(BF16), w_proj_b[0].astype(BF16),
        w_out[0].astype(BF16), w_r, b_r, tm=tm2, sub=GATHER_SUB)

    block_expert, rank_lo, sub_lo, sub_hi, n_used, chunk_off, chunk_items = _moe_plan(cnt_after, n_tok, GATHER_SUB)
    yb = expert_blocks(h2, top_i, top_w, block_expert, rank_lo, sub_lo, sub_hi, n_used, w_gate_up[0], b_gate_up[0],
                       w_down[0], b_down[0])
    out = combine_residual(yb, top_i, x2, chunk_off, chunk_items, mod, lambda c: c // (seq // MOE_CHUNK),
                           g_post_ffn[0])
    return out.reshape(batch, seq, d)
```

```python
import functools

import jax
import jax.numpy as jnp
from jax import lax
from jax.experimental import pallas as pl
from jax.experimental.pallas import tpu as pltpu

F32 = jnp.float32
BF16 = jnp.bfloat16

D_MODEL = 1024
GRID_W = 64
HEADS = 8
DK = 128
DV = 128
SHORT_CONV = 5
CHUNK = 64
CONF_KERNEL = 31
N_EXPERTS = 32
TOP_K = 4
SWIGLU_LIMIT = 7.0
SWIGLU_ALPHA = 1.702
MOE_BLOCK = 128
EPS = 1e-6
POS_BASE = 10000.0

LANES = 128
SUBLANES = 8
VMEM_LIMIT = 60 * 1024 * 1024

COL_K, COL_V, COL_Q, COL_Z, COL_GLU_A, COL_GLU_G, COL_GATE_A, COL_GATE_B = range(8)


def _params(*sem):
    return pltpu.CompilerParams(dimension_semantics=sem, vmem_limit_bytes=VMEM_LIMIT)


def _mod_kernel(c_ref, w_ref, b_ref, o_ref):
    c = c_ref[...]
    s = c * jax.nn.sigmoid(c)
    o_ref[...] = jnp.dot(s, w_ref[...], preferred_element_type=F32,
                         precision=lax.Precision.HIGHEST) + b_ref[...]


def modulation(cond, w_mod, b_mod):
    r, d = cond.shape
    n = w_mod.shape[1]
    tn = 1024
    return pl.pallas_call(
        _mod_kernel,
        grid=(n // tn,),
        in_specs=[pl.BlockSpec((r, d), lambda j: (0, 0)),
                  pl.BlockSpec((d, tn), lambda j: (0, j)),
                  pl.BlockSpec((1, tn), lambda j: (0, j))],
        out_specs=pl.BlockSpec((r, tn), lambda j: (0, j)),
        out_shape=jax.ShapeDtypeStruct((r, n), F32),
        compiler_params=_params("arbitrary"),
        name="modulation",
    )(cond, w_mod, b_mod.reshape(1, n))


def _proj_weights_kernel(a_ref, b_ref, o_ref, *, first_shifted, shift):
    @pl.when(pl.program_id(0) < first_shifted)
    def _():
        o_ref[...] = a_ref[...].astype(o_ref.dtype)

    @pl.when(pl.program_id(0) >= first_shifted)
    def _():
        o_ref[...] = jnp.concatenate([a_ref[:, shift:], b_ref[:, :shift]], axis=1).astype(o_ref.dtype)


def projection_weights(w, *, keep, skip, tn=1024):
    d, total = w.shape
    n_out = total - skip
    assert keep % tn == 0 and n_out % tn == 0 and 0 < skip < LANES
    per_tile = tn // LANES
    return pl.pallas_call(
        functools.partial(_proj_weights_kernel, first_shifted=keep // tn, shift=skip),
        grid=(n_out // tn,),
        in_specs=[pl.BlockSpec((d, tn), lambda j: (0, j)),
                  pl.BlockSpec((d, LANES), lambda j: (0, (j + 1) * per_tile))],
        out_specs=pl.BlockSpec((d, tn), lambda j: (0, j)),
        out_shape=jax.ShapeDtypeStruct((d, n_out), BF16),
        compiler_params=_params("arbitrary"),
        name="projection_weights",
    )(w, w)


def _inproj_kernel(*refs, has_pos):
    if has_pos:
        x_ref, pos_ref, sh_ref, sc_ref, g_ref, w_ref, wba_ref, o_ref, ba_ref, h_scr = refs
    else:
        x_ref, sh_ref, sc_ref, g_ref, w_ref, wba_ref, o_ref, ba_ref, h_scr = refs

    @pl.when(pl.program_id(1) == 0)
    def _():
        x = x_ref[...]
        if has_pos:
            x = x + pos_ref[...]
        y = x * lax.rsqrt(jnp.mean(x * x, axis=-1, keepdims=True) + EPS) * g_ref[...]
        h = (y * (1.0 + sc_ref[0]) + sh_ref[0]).astype(BF16)
        h_scr[...] = h
        ba_ref[...] = jnp.dot(h, wba_ref[...], preferred_element_type=F32)

    o_ref[...] = jnp.dot(h_scr[...], w_ref[...], preferred_element_type=F32).astype(o_ref.dtype)


def input_projection(x, pos, mod, mod_row_of_tile, g_pre, w_main, w_ba, *, tm, tn):
    n, d = x.shape
    w = w_main.shape[1]
    has_pos = pos is not None
    in_specs = [pl.BlockSpec((tm, d), lambda i, j: (i, 0))]
    args = [x]
    if has_pos:
        pos_tiles = pos.shape[0] // tm
        in_specs.append(pl.BlockSpec((tm, d), lambda i, j: (i % pos_tiles, 0)))
        args.append(pos)
    in_specs += [
        pl.BlockSpec((1, 1, d), lambda i, j: (mod_row_of_tile(i), 0, 0)),
        pl.BlockSpec((1, 1, d), lambda i, j: (mod_row_of_tile(i), 0, 1)),
        pl.BlockSpec((1, d), lambda i, j: (0, 0)),
        pl.BlockSpec((d, tn), lambda i, j: (0, j)),
        pl.BlockSpec((d, LANES), lambda i, j: (0, 0)),
    ]
    args += [mod, mod, g_pre.reshape(1, d), w_main, w_ba]
    return pl.pallas_call(
        functools.partial(_inproj_kernel, has_pos=has_pos),
        grid=(n // tm, w // tn),
        in_specs=in_specs,
        out_specs=[pl.BlockSpec((tm, tn), lambda i, j: (i, j)),
                   pl.BlockSpec((tm, LANES), lambda i, j: (i, 0))],
        out_shape=[jax.ShapeDtypeStruct((n, w), BF16), jax.ShapeDtypeStruct((n, LANES), F32)],
        scratch_shapes=[pltpu.VMEM((tm, d), BF16)],
        compiler_params=_params("arbitrary", "arbitrary"),
        name="input_projection_pos" if has_pos else "input_projection_ctx",
    )(*args)


def _silu(x):
    return x * jax.nn.sigmoid(x)


def _softplus(x):
    return jnp.maximum(x, 0.0) + jnp.log(1.0 + jnp.exp(-jnp.abs(x)))


def _short_conv(src, taps, pad_scr, rows):
    half = SHORT_CONV // 2
    pad_scr[0:SUBLANES, :] = jnp.zeros((SUBLANES, LANES), F32)
    pad_scr[SUBLANES:SUBLANES + rows, :] = src.astype(F32)
    pad_scr[SUBLANES + rows:2 * SUBLANES + rows, :] = jnp.zeros((SUBLANES, LANES), F32)
    acc = None
    for j in range(SHORT_CONV):
        off = SUBLANES + j - half
        term = pad_scr[off:off + rows, :] * taps[j:j + 1, :]
        acc = term if acc is None else acc + term
    return _silu(acc)


def _l2n(x):
    return x * lax.rsqrt(jnp.sum(x * x, axis=-1, keepdims=True) + EPS)


def _gdn_kernel(k_ref, v_ref, q_ref, z_ref, ba_ref, kc_ref, vc_ref, bac_ref,
                wk_ref, wv_ref, wq_ref, ab_ref, gn_ref, o_ref,
                kf, vf, qf, gates, pad_scr, lhs, val, att, ktt, dec, osc, *, seq, ctx, unroll, group):
    total = ctx + seq
    nc = total // CHUNK
    ncc = ctx // CHUNK
    two = 2 * CHUNK
    lg_chunk = CHUNK.bit_length() - 1

    ri = lax.broadcasted_iota(jnp.int32, (two, two), 0)
    ci = lax.broadcasted_iota(jnp.int32, (two, two), 1)
    same_dir = (ri >> lg_chunk) == (ci >> lg_chunk)
    incl = same_dir & (((ri < CHUNK) & (ri >= ci)) | ((ri >= CHUNK) & (ri <= ci)))
    strict = incl & (ri != ci)
    eye = jnp.where(ri == ci, 1.0, 0.0)
    tri = jnp.where(incl, 1.0, 0.0).astype(BF16)
    level_masks = []
    for lg in range(lg_chunk):
        same_parent = (ri >> (lg + 1)) == (ci >> (lg + 1))
        level_masks.append(jnp.where(same_parent & ((ri >> lg) != (ci >> lg)), 1.0, 0.0))
    top_rows = lax.broadcasted_iota(jnp.int32, (two, LANES), 0) < CHUNK

    def bwd_chunk(t):
        return jnp.where(t < ncc, ncc - 1 - t, nc + ncc - 1 - t)

    def stacked(ref, rf, rb):
        return jnp.concatenate([ref[pl.ds(rf, CHUNK), :], ref[pl.ds(rb, CHUNK), :]], axis=0)

    def chunk_load(t):
        rf = pl.multiple_of(t * CHUNK, CHUNK)
        rb = pl.multiple_of(bwd_chunk(t) * CHUNK, CHUNK)
        beta = jnp.concatenate([gates[0, pl.ds(rf, CHUNK), :], gates[1, pl.ds(rb, CHUNK), :]], axis=0)
        g = jnp.concatenate([gates[2, pl.ds(rf, CHUNK), :], gates[3, pl.ds(rb, CHUNK), :]], axis=0)
        return stacked(kf, rf, rb), stacked(vf, rf, rb), stacked(qf, rf, rb), beta, g

    def chunk_triangle(k, q, beta, g):
        kb = k.astype(BF16)
        kq = jnp.concatenate([kb, q.astype(BF16)], axis=0)
        kk_qk = lax.dot_general(kq, kb, (((1,), (1,)), ((), ())), preferred_element_type=F32)
        g_hi = g.astype(BF16)
        r1 = g - g_hi.astype(F32)
        g_mid = r1.astype(BF16)
        g_lo = (r1 - g_mid.astype(F32)).astype(BF16)
        gc3 = jnp.dot(tri, jnp.concatenate([g_hi, g_mid, g_lo], axis=1), preferred_element_type=F32)
        gc = gc3[:, :LANES] + gc3[:, LANES:2 * LANES] + gc3[:, 2 * LANES:]
        gc_row = jnp.transpose(gc)
        decay = jnp.where(incl, jnp.exp(gc - gc_row), 0.0)
        a = jnp.where(strict, beta * kk_qk[:two] * decay, 0.0)
        return a, (kk_qk[two:] * decay).astype(BF16), gc

    def chunk_finish(hh, t, k, v, q, beta, gc, attn, tmat):
        eg = jnp.exp(gc)
        rhs = jnp.concatenate([v * beta, k * beta * eg], axis=1).astype(BF16)
        sol = jnp.dot(tmat.astype(BF16), rhs, preferred_element_type=F32)
        g_end = jnp.where(top_rows, jnp.broadcast_to(gc[CHUNK - 1:CHUNK, :], (two, LANES)),
                          jnp.broadcast_to(gc[CHUNK:CHUNK + 1, :], (two, LANES)))
        k_tail = k * jnp.exp(g_end - gc)
        qg = (q * eg).astype(BF16)
        kcum = sol[:, DV:].astype(BF16)
        r0 = pl.multiple_of(t * two, two)
        lhs[hh, 0, pl.ds(r0, two), :] = jnp.concatenate([kcum[:CHUNK], qg[:CHUNK]], axis=0)
        lhs[hh, 1, pl.ds(r0, two), :] = jnp.concatenate([kcum[CHUNK:], qg[CHUNK:]], axis=0)
        val[hh, pl.ds(r0, two), :] = sol[:, :DV].astype(BF16)
        att[hh, pl.ds(r0, two), :] = attn
        ktt[hh, pl.ds(r0, two), :] = jnp.transpose(k_tail).astype(BF16)
        d0 = pl.multiple_of(t * SUBLANES, SUBLANES)
        e_end = jnp.exp(g_end)
        dec[hh, 0, pl.ds(d0, SUBLANES), :] = e_end[:SUBLANES]
        dec[hh, 1, pl.ds(d0, SUBLANES), :] = e_end[CHUNK:CHUNK + SUBLANES]

    for hh in range(group):
        head = pl.program_id(1) * group + hh
        cols = slice(hh * LANES, (hh + 1) * LANES)

        kf[0:ctx, :] = _l2n(_short_conv(kc_ref[:, cols], wk_ref[:, cols], pad_scr, ctx))
        vf[0:ctx, :] = _short_conv(vc_ref[:, cols], wv_ref[:, cols], pad_scr, ctx)
        qf[0:ctx, :] = jnp.zeros((ctx, LANES), F32)
        kf[ctx:total, :] = _l2n(_short_conv(k_ref[:, cols], wk_ref[:, cols], pad_scr, seq))
        vf[ctx:total, :] = _short_conv(v_ref[:, cols], wv_ref[:, cols], pad_scr, seq)
        qf[ctx:total, :] = _l2n(_short_conv(q_ref[:, cols], wq_ref[:, cols], pad_scr, seq)) * (DK ** -0.5)

        def gate_cols(src_ref, lo, rows):
            x = src_ref[...]
            lane = lax.broadcasted_iota(jnp.int32, (rows, LANES), 1)
            beta = jax.nn.sigmoid(x)
            g = -jnp.exp(ab_ref[0:1, :]) * _softplus(x + ab_ref[1:2, :])
            for slot, (arr, base) in enumerate(((beta, 0), (beta, HEADS), (g, 2 * HEADS), (g, 3 * HEADS))):
                col = jnp.sum(jnp.where(lane == base + head, arr, 0.0), axis=-1, keepdims=True)
                gates[slot, lo:lo + rows, :] = jnp.broadcast_to(col, (rows, LANES))

        gate_cols(bac_ref, 0, ctx)
        gate_cols(ba_ref, ctx, seq)

        def chunk_body(i, carry, hh=hh):
            steps = [i * unroll + u for u in range(unroll)]
            loaded = [chunk_load(t) for t in steps]
            tris = [chunk_triangle(k, q, beta, g) for (k, v, q, beta, g) in loaded]
            tmats = [eye - a * level_masks[0] for (a, _, _) in tris]
            for m in level_masks[1:]:
                nxt = []
                for (a, _, _), tmat in zip(tris, tmats):
                    cs = (a * m).astype(BF16)
                    tb = tmat.astype(BF16)
                    tc = jnp.dot(tb, cs, preferred_element_type=F32)
                    nxt.append(tmat - jnp.dot(tc.astype(BF16), tb, preferred_element_type=F32))
                tmats = nxt
            for t, (k, v, q, beta, g), (a, attn, gc), tmat in zip(steps, loaded, tris, tmats):
                chunk_finish(hh, t, k, v, q, beta, gc, attn, tmat)
            return carry

        lax.fori_loop(0, nc // unroll, chunk_body, 0)

    def scan_body(t, carry):
        rf = pl.multiple_of(t * CHUNK, CHUNK)
        rb = pl.multiple_of(bwd_chunk(t) * CHUNK, CHUNK)
        r0 = pl.multiple_of(t * two, two)
        d0 = pl.multiple_of(t * SUBLANES, SUBLANES)
        rs = [(jnp.dot(lhs[hh, 0, pl.ds(r0, two), :], carry[2 * hh].astype(BF16), preferred_element_type=F32),
               jnp.dot(lhs[hh, 1, pl.ds(r0, two), :], carry[2 * hh + 1].astype(BF16), preferred_element_type=F32))
              for hh in range(group)]
        v_news = [val[hh, pl.ds(r0, two), :].astype(F32) - jnp.concatenate([r_f[:CHUNK], r_b[:CHUNK]], axis=0)
                  for hh, (r_f, r_b) in enumerate(rs)]
        outs, new_states = [], []
        for hh, ((r_f, r_b), v_new) in enumerate(zip(rs, v_news)):
            zeros = jnp.zeros_like(v_new)
            v_bd = jnp.concatenate([jnp.where(top_rows, v_new, zeros), jnp.where(top_rows, zeros, v_new)],
                                   axis=1).astype(BF16)
            upd = jnp.dot(ktt[hh, pl.ds(r0, two), :], v_bd, preferred_element_type=F32)
            new_states.append(carry[2 * hh] * dec[hh, 0, pl.ds(d0, SUBLANES), :][0:1, :] + upd[:, :DV])
            new_states.append(carry[2 * hh + 1] * dec[hh, 1, pl.ds(d0, SUBLANES), :][0:1, :] + upd[:, DV:])
            outs.append(jnp.concatenate([r_f[CHUNK:], r_b[CHUNK:]], axis=0)
                        + jnp.dot(att[hh, pl.ds(r0, two), :], v_new.astype(BF16), preferred_element_type=F32))
        for hh, o in enumerate(outs):
            osc[hh, pl.ds(rf, CHUNK), :] += o[:CHUNK]
            osc[hh, pl.ds(rb, CHUNK), :] += o[CHUNK:]
        return tuple(new_states)

    osc[...] = jnp.zeros_like(osc)
    zero = jnp.zeros((DK, DV), F32)
    lax.fori_loop(0, nc, scan_body, (zero,) * (2 * group))

    for hh in range(group):
        cols = slice(hh * LANES, (hh + 1) * LANES)
        o = osc[hh, ctx:total, :]
        o = o * lax.rsqrt(jnp.mean(o * o, axis=-1, keepdims=True) + EPS) * gn_ref[...]
        o_ref[:, cols] = (o * _silu(z_ref[:, cols].astype(F32))).astype(o_ref.dtype)


def gated_deltanet(proj, ba, proj_ctx, ba_ctx, conv_kv, conv_q, a_log, dt_bias, gn, *, batch, seq, ctx,
                   unroll=12, group=4):
    total = seq + ctx
    nc = total // CHUNK
    gl = group * LANES
    ng = HEADS // group
    taps = jnp.zeros((SUBLANES, conv_kv.shape[1]), F32).at[:SHORT_CONV].set(conv_kv)
    taps_q = jnp.zeros((SUBLANES, conv_q.shape[1]), F32).at[:SHORT_CONV].set(conv_q)
    ab = jnp.zeros((SUBLANES, LANES), F32)
    ab = ab.at[0, 2 * HEADS:4 * HEADS].set(a_log.reshape(-1)).at[1, 2 * HEADS:4 * HEADS].set(dt_bias.reshape(-1))
    col = lambda base: (lambda b, h: (b, base * ng + h))
    once = pl.Buffered(1)
    return pl.pallas_call(
        functools.partial(_gdn_kernel, seq=seq, ctx=ctx, unroll=unroll, group=group),
        grid=(batch, ng),
        in_specs=[
            pl.BlockSpec((seq, gl), col(COL_K), pipeline_mode=once),
            pl.BlockSpec((seq, gl), col(COL_V), pipeline_mode=once),
            pl.BlockSpec((seq, gl), col(COL_Q), pipeline_mode=once),
            pl.BlockSpec((seq, gl), col(COL_Z), pipeline_mode=once),
            pl.BlockSpec((seq, LANES), lambda b, h: (b, 0), pipeline_mode=once),
            pl.BlockSpec((ctx, gl), col(COL_K)),
            pl.BlockSpec((ctx, gl), col(COL_V)),
            pl.BlockSpec((ctx, LANES), lambda b, h: (b, 0)),
            pl.BlockSpec((SUBLANES, gl), lambda b, h: (0, h)),
            pl.BlockSpec((SUBLANES, gl), lambda b, h: (0, ng + h)),
            pl.BlockSpec((SUBLANES, gl), lambda b, h: (0, h)),
            pl.BlockSpec((SUBLANES, LANES), lambda b, h: (0, 0)),
            pl.BlockSpec((1, LANES), lambda b, h: (0, 0)),
        ],
        out_specs=pl.BlockSpec((seq, gl), lambda b, h: (b, h)),
        out_shape=jax.ShapeDtypeStruct((batch * seq, HEADS * DV), BF16),
        scratch_shapes=[
            pltpu.VMEM((total, LANES), F32),
            pltpu.VMEM((total, LANES), F32),
            pltpu.VMEM((total, LANES), F32),
            pltpu.VMEM((4, total, LANES), F32),
            pltpu.VMEM((seq + 2 * SUBLANES, LANES), F32),
            pltpu.VMEM((group, 2, nc * 2 * CHUNK, LANES), BF16),
            pltpu.VMEM((group, nc * 2 * CHUNK, LANES), BF16),
            pltpu.VMEM((group, nc * 2 * CHUNK, LANES), BF16),
            pltpu.VMEM((group, nc * 2 * CHUNK, LANES), BF16),
            pltpu.VMEM((group, 2, nc * SUBLANES, LANES), F32),
            pltpu.VMEM((group, total, LANES), F32),
        ],
        compiler_params=_params("arbitrary", "arbitrary"),
        name="gated_deltanet",
    )(proj, proj, proj, proj, ba, proj_ctx, proj_ctx, ba_ctx, taps, taps, taps_q, ab, gn.reshape(1, DV))


CONF_ROWS = 128
CONF_HALO = 16


def _conf_kernel(a_ref, g_ref, w_ref, b_ref, o_ref, pad_scr, *, seq):
    cb = a_ref.shape[1]
    u = a_ref[...].astype(F32) * jax.nn.sigmoid(g_ref[...].astype(F32))
    pad_scr[0:CONF_HALO, :] = jnp.zeros((CONF_HALO, cb), F32)
    pad_scr[CONF_HALO:CONF_HALO + seq, :] = u
    pad_scr[CONF_HALO + seq:2 * CONF_HALO + seq, :] = jnp.zeros((CONF_HALO, cb), F32)
    win = CONF_ROWS + 2 * CONF_HALO
    first = CONF_HALO - CONF_KERNEL // 2

    def body(i, carry):
        r0 = pl.multiple_of(i * CONF_ROWS, CONF_ROWS)
        w = pad_scr[pl.ds(r0, win), :]
        acc = jnp.zeros((CONF_ROWS, cb), F32) + b_ref[...]
        for sub in range(SUBLANES):
            shifted = w if sub == 0 else pltpu.roll(w, win - sub, axis=0)
            for j in range(CONF_KERNEL):
                off = first + j
                if off % SUBLANES == sub:
                    base = off - sub
                    acc = acc + shifted[base:base + CONF_ROWS, :] * w_ref[j:j + 1, :]
        o_ref[pl.ds(r0, CONF_ROWS), :] = acc.astype(o_ref.dtype)
        return carry

    lax.fori_loop(0, seq // CONF_ROWS, body, 0)


def conformer_conv(proj, conf_dw, conf_dw_b, *, batch, seq, cb=256):
    c = conf_dw.shape[1]
    nb = c // cb
    taps = jnp.zeros((32, c), F32).at[:CONF_KERNEL].set(conf_dw)
    return pl.pallas_call(
        functools.partial(_conf_kernel, seq=seq),
        grid=(batch, nb),
        in_specs=[pl.BlockSpec((seq, cb), lambda b, j: (b, COL_GLU_A * nb + j)),
                  pl.BlockSpec((seq, cb), lambda b, j: (b, COL_GLU_G * nb + j)),
                  pl.BlockSpec((32, cb), lambda b, j: (0, j)),
                  pl.BlockSpec((1, cb), lambda b, j: (0, j))],
        out_specs=pl.BlockSpec((seq, cb), lambda b, j: (b, j)),
        out_shape=jax.ShapeDtypeStruct((batch * seq, c), BF16),
        scratch_shapes=[pltpu.VMEM((seq + 2 * CONF_HALO, cb), F32)],
        compiler_params=_params("arbitrary", "arbitrary"),
        name="conformer_conv",
    )(proj, proj, taps, conf_dw_b.reshape(1, c))


def _mixer_out_kernel(og_ref, uc_ref, ga_ref, gb_ref, x_ref, pos_ref, gm_ref, shf_ref, scf_ref,
                      gpost_ref, gpre_ref, lng_ref, lnb_ref, wa_ref, wb_ref, wo_ref, wr_ref, br_ref,
                      x2_ref, h2_ref, ti_ref, tw_ref, cnt_ref, cnt_scr, *, sub):
    parts = [slice(r, r + sub) for r in range(0, og_ref.shape[0], sub)]

    def rms(v):
        return v * lax.rsqrt(jnp.mean(v * v, axis=-1, keepdims=True) + EPS)

    def layer_norm_silu(uc):
        mu = jnp.mean(uc, axis=-1, keepdims=True)
        var = jnp.mean(jnp.square(uc - mu), axis=-1, keepdims=True)
        return _silu((uc - mu) * lax.rsqrt(var + EPS) * lng_ref[...] + lnb_ref[...]).astype(BF16)

    us = [layer_norm_silu(uc_ref[p, :].astype(F32)) for p in parts]
    yas = [jnp.dot(og_ref[p, :], wa_ref[...], preferred_element_type=F32) for p in parts]
    ybs = [jnp.dot(u, wb_ref[...], preferred_element_type=F32) for u in us]
    merged = [(jax.nn.sigmoid(ga_ref[p, :].astype(F32)) * ya
               + jax.nn.sigmoid(gb_ref[p, :].astype(F32)) * yb).astype(BF16) for p, ya, yb in zip(parts, yas, ybs)]
    ys = [jnp.dot(m, wo_ref[...], preferred_element_type=F32) for m in merged]
    x2s = [x_ref[p, :] + pos_ref[p, :] + gm_ref[0] * (rms(y) * gpost_ref[...]) for p, y in zip(parts, ys)]
    h2s = [(rms(x2) * gpre_ref[...] * (1.0 + scf_ref[0]) + shf_ref[0]).astype(BF16) for x2 in x2s]
    for p, x2, h2 in zip(parts, x2s, h2s):
        x2_ref[p, :] = x2
        h2_ref[p, :] = h2
    all_logits = [jnp.dot(h2, wr_ref[...], preferred_element_type=F32) + br_ref[...] for h2 in h2s]
    lane = lax.broadcasted_iota(jnp.int32, (sub, LANES), 1).astype(F32)
    ri = lax.broadcasted_iota(jnp.int32, (sub, sub), 0)
    ci = lax.broadcasted_iota(jnp.int32, (sub, sub), 1)
    before = jnp.where(ri > ci, 1.0, 0.0).astype(BF16)

    @pl.when(pl.program_id(0) == 0)
    def _():
        cnt_scr[...] = jnp.zeros_like(cnt_scr)

    routed = []
    for logits in all_logits:
        live = jnp.where(lane < N_EXPERTS, logits, -jnp.inf)
        top_v, top_i = [], []
        for _ in range(TOP_K):
            m = jnp.max(live, axis=-1, keepdims=True)
            idx = jnp.min(jnp.where(live == m, lane, float(LANES)), axis=-1, keepdims=True)
            top_v.append(m)
            top_i.append(idx)
            live = jnp.where(lane == idx, -jnp.inf, live)
        ex = [jnp.exp(v - top_v[0]) for v in top_v]
        denom = ex[0] + ex[1] + ex[2] + ex[3]
        picked = jnp.zeros((sub, LANES), F32)
        for k in range(TOP_K):
            picked = jnp.where(lane == top_i[k], 1.0, picked)
        routed.append((top_i, [e / denom for e in ex], picked,
                       jnp.dot(before, picked.astype(BF16), preferred_element_type=F32)))

    count = cnt_scr[...]
    for j, (p, (top_i, top_w, picked, within)) in enumerate(zip(parts, routed)):
        prefix = within + count[0:1, :]
        ti = jnp.zeros((sub, LANES), F32)
        tw = jnp.zeros((sub, LANES), F32)
        for k in range(TOP_K):
            rank = jnp.sum(jnp.where(lane == top_i[k], prefix, 0.0), axis=-1, keepdims=True)
            ti = jnp.where(lane == k, top_i[k], ti)
            ti = jnp.where(lane == TOP_K + k, rank, ti)
            tw = jnp.where(lane == k, top_w[k], tw)
        ti_ref[:, p] = jnp.transpose(ti)[:SUBLANES, :].astype(jnp.int32)
        tw_ref[:, p] = jnp.transpose(tw)[:SUBLANES, :]
        count = count + jnp.sum(picked, axis=0, keepdims=True)
        cnt_ref[j] = count
    cnt_scr[...] = count


def mixer_output(og, uc, proj, x, pos, mod, mod_row_of_tile, g_post, g_pre_f, ln_g, ln_b,
                 w_a, w_b, w_o, w_r, b_r, *, tm, sub):
    n, d = x.shape
    pos_tiles = pos.shape[0] // tm
    row = lambda i: (i, 0)
    fixed = lambda i: (0, 0)
    modspec = lambda k: pl.BlockSpec((1, 1, d), lambda i: (mod_row_of_tile(i), 0, k))
    vec = pl.BlockSpec((1, d), fixed)
    mat = pl.BlockSpec((d, d), fixed)
    return pl.pallas_call(
        functools.partial(_mixer_out_kernel, sub=sub),
        grid=(n // tm,),
        in_specs=[pl.BlockSpec((tm, d), row), pl.BlockSpec((tm, d), row),
                  pl.BlockSpec((tm, d), lambda i: (i, COL_GATE_A)),
                  pl.BlockSpec((tm, d), lambda i: (i, COL_GATE_B)),
                  pl.BlockSpec((tm, d), row),
                  pl.BlockSpec((tm, d), lambda i: (i % pos_tiles, 0)),
                  modspec(2), modspec(3), modspec(4),
                  vec, vec, vec, vec, mat, mat, mat,
                  pl.BlockSpec((d, LANES), fixed), pl.BlockSpec((1, LANES), fixed)],
        out_specs=[pl.BlockSpec((tm, d), row), pl.BlockSpec((tm, d), row),
                   pl.BlockSpec((SUBLANES, tm), lambda i: (0, i)), pl.BlockSpec((SUBLANES, tm), lambda i: (0, i)),
                   pl.BlockSpec((tm // sub, SUBLANES, LANES), lambda i: (i, 0, 0))],
        out_shape=[jax.ShapeDtypeStruct((n, d), F32), jax.ShapeDtypeStruct((n, d), BF16),
                   jax.ShapeDtypeStruct((SUBLANES, n), jnp.int32), jax.ShapeDtypeStruct((SUBLANES, n), F32),
                   jax.ShapeDtypeStruct((n // sub, SUBLANES, LANES), F32)],
        scratch_shapes=[pltpu.VMEM((SUBLANES, LANES), F32)],
        compiler_params=_params("arbitrary"),
        name="mixer_output",
    )(og, uc, proj, proj, x, pos, mod, mod, mod,
      g_post.reshape(1, d), g_pre_f.reshape(1, d), ln_g.reshape(1, d), ln_b.reshape(1, d),
      w_a, w_b, w_o, w_r, b_r)


EXPERT_BLOCK = 256
MOE_CHUNK = 512
GATHER_SUB = 256
COMBINE_GRANULE = 32
COMBINE_SLOTS = 32


def _expert_kernel(bexp_ref, rlo_ref, slo_ref, shi_ref, nused_ref,
                   h2_ref, ti_ref, wt_ref, wgu_ref, bgu_ref, wd_ref, bd_ref, o_ref,
                   acc, rw_acc, wgu_bf, wd_bf):
    b = pl.program_id(0)
    expert = bexp_ref[b]

    @pl.when((b == 0) | (expert != bexp_ref[jnp.maximum(b - 1, 0)]))
    def _():
        wgu_bf[...] = wgu_ref[0].astype(BF16)
        wd_bf[...] = wd_ref[0].astype(BF16)

    @pl.when(b < nused_ref[0])
    def _():
        acc[...] = jnp.zeros_like(acc)
        rw_acc[...] = jnp.zeros_like(rw_acc)
        row = lax.broadcasted_iota(jnp.int32, (EXPERT_BLOCK, GATHER_SUB), 0)
        s_lo, s_hi = slo_ref[b], shi_ref[b]
        last_sub = h2_ref.shape[0] // GATHER_SUB - 1

        def select(s, live):
            t0 = pl.multiple_of(s * GATHER_SUB, GATHER_SUB)
            ids = ti_ref[0:TOP_K, pl.ds(t0, GATHER_SUB)]
            local = ti_ref[TOP_K:2 * TOP_K, pl.ds(t0, GATHER_SUB)] - rlo_ref[b]
            if live is not True:
                local = jnp.where(live, local, -1)
            inside = (ids == expert) & (local >= 0) & (local < EXPERT_BLOCK)
            hit_row = jnp.sum(jnp.where(inside, local + 1, 0), axis=0, keepdims=True) - 1
            hit_w = jnp.sum(jnp.where(inside, wt_ref[0:TOP_K, pl.ds(t0, GATHER_SUB)], 0.0), axis=0, keepdims=True)
            hit = row == hit_row
            return (jnp.where(hit, 1.0, 0.0).astype(BF16), jnp.where(hit, hit_w, 0.0),
                    h2_ref[pl.ds(t0, GATHER_SUB), :])

        def gather(i, carry):
            s0 = s_lo + 2 * i
            p0, w0, x0 = select(s0, True)
            p1, w1, x1 = select(jnp.minimum(s0 + 1, last_sub), s0 + 1 <= s_hi)
            acc[...] += jnp.dot(jnp.concatenate([p0, p1], axis=1), jnp.concatenate([x0, x1], axis=0),
                                preferred_element_type=F32)
            rw_acc[...] += jnp.sum(w0 + w1, axis=1, keepdims=True)
            return carry

        lax.fori_loop(0, (s_hi - s_lo + 2) // 2, gather, 0)
        gu = jnp.dot(acc[...].astype(BF16), wgu_bf[...], preferred_element_type=F32) + bgu_ref[0]
        dff = gu.shape[1] // 2
        gl = jnp.minimum(gu[:, :dff], SWIGLU_LIMIT)
        lin = jnp.clip(gu[:, dff:], -SWIGLU_LIMIT, SWIGLU_LIMIT)
        act = (gl * jax.nn.sigmoid(SWIGLU_ALPHA * gl) * (lin + 1.0)).astype(BF16)
        y = jnp.dot(act, wd_bf[...], preferred_element_type=F32) + bd_ref[0]
        o_ref[...] = (y * rw_acc[...]).astype(o_ref.dtype)

    @pl.when(b >= nused_ref[0])
    def _():
        o_ref[...] = jnp.zeros_like(o_ref)


def expert_blocks(h2, ti_t, wt_t, block_expert, rank_lo, sub_lo, sub_hi, n_used, w_gate_up, b_gate_up, w_down,
                  b_down):
    n, d = h2.shape
    e, _, f2 = w_gate_up.shape
    n_blocks = block_expert.shape[0]
    whole = lambda b, be, rl, lo, hi, nu: (0, 0)
    by_expert = lambda b, be, rl, lo, hi, nu: (be[b], 0, 0)
    once = pl.Buffered(1)
    grid_spec = pltpu.PrefetchScalarGridSpec(
        num_scalar_prefetch=5,
        grid=(n_blocks,),
        in_specs=[pl.BlockSpec((n, d), whole, pipeline_mode=once),
                  pl.BlockSpec((SUBLANES, n), whole, pipeline_mode=once),
                  pl.BlockSpec((SUBLANES, n), whole, pipeline_mode=once),
                  pl.BlockSpec((1, d, f2), by_expert, pipeline_mode=once),
                  pl.BlockSpec((1, 1, f2), by_expert),
                  pl.BlockSpec((1, f2 // 2, d), by_expert),
                  pl.BlockSpec((1, 1, d), by_expert)],
        out_specs=pl.BlockSpec((EXPERT_BLOCK, d), lambda b, be, rl, lo, hi, nu: (b, 0)),
        scratch_shapes=[pltpu.VMEM((EXPERT_BLOCK, d), F32), pltpu.VMEM((EXPERT_BLOCK, 1), F32),
                        pltpu.VMEM((d, f2), BF16), pltpu.VMEM((f2 // 2, d), BF16)],
    )
    return pl.pallas_call(
        _expert_kernel,
        grid_spec=grid_spec,
        out_shape=jax.ShapeDtypeStruct((n_blocks * EXPERT_BLOCK, d), BF16),
        compiler_params=_params("arbitrary"),
        name="expert_blocks",
    )(block_expert, rank_lo, sub_lo, sub_hi, n_used, h2, ti_t, wt_t, w_gate_up, b_gate_up.reshape(e, 1, f2), w_down,
      b_down.reshape(e, 1, d))


def _combine_kernel(coff_ref, cgran_ref, cexp_ref, cbase_ref, yb_hbm, ti_ref, x2_ref, gf_ref, g_ref, o_ref,
                    slab, sem, acc):
    c = pl.program_id(0)
    start = coff_ref[c]
    count = coff_ref[c + 1] - start
    n_groups = (count + COMBINE_SLOTS - 1) // COMBINE_SLOTS

    def item(g, s):
        return start + jnp.minimum(g * COMBINE_SLOTS + s, count - 1)

    def granule_of(g, s):
        return cgran_ref[item(g, s)]

    def copies(g, buf):
        return [pltpu.make_async_copy(
            yb_hbm.at[pl.ds(pl.multiple_of(granule_of(g, s) * COMBINE_GRANULE, COMBINE_GRANULE), COMBINE_GRANULE), :],
            slab.at[buf, pl.ds(s * COMBINE_GRANULE, COMBINE_GRANULE), :], sem.at[buf])
            for s in range(COMBINE_SLOTS)]

    for cp in copies(0, 0):
        cp.start()
    acc[...] = jnp.zeros_like(acc)
    chunk = x2_ref.shape[0]
    row = lax.broadcasted_iota(jnp.int32, (COMBINE_GRANULE, chunk), 0)
    ids = ti_ref[0:TOP_K, :]
    ranks = ti_ref[TOP_K:2 * TOP_K, :]

    def group(g, carry):
        buf = g % 2

        @pl.when(g + 1 < n_groups)
        def _():
            for cp in copies(g + 1, 1 - buf):
                cp.start()

        for cp in copies(g, buf):
            cp.wait()
        pieces = []
        for s in range(COMBINE_SLOTS):
            live = g * COMBINE_SLOTS + s < count
            local = ranks - cbase_ref[item(g, s)]
            inside = (ids == jnp.where(live, cexp_ref[item(g, s)], -1)) & (local >= 0) & (local < COMBINE_GRANULE)
            hit_row = jnp.sum(jnp.where(inside, local + 1, 0), axis=0, keepdims=True) - 1
            pieces.append(jnp.where(row == hit_row, 1.0, 0.0).astype(BF16))
        onehot_t = jnp.concatenate(pieces, axis=0)
        acc[...] += lax.dot_general(onehot_t, slab[buf], (((0,), (0,)), ((), ())),
                                    preferred_element_type=F32)
        return carry

    lax.fori_loop(0, n_groups, group, 0)
    y = acc[...]
    yn = y * lax.rsqrt(jnp.mean(y * y, axis=-1, keepdims=True) + EPS) * g_ref[...]
    o_ref[...] = x2_ref[...] + gf_ref[0] * yn


def combine_residual(yb, ti_t, x2, chunk_off, chunk_items, mod, mod_row_of_chunk, g_post_f):
    n, d = x2.shape
    chunk_row = lambda c, off, gran, exp, base: (c, 0)
    grid_spec = pltpu.PrefetchScalarGridSpec(
        num_scalar_prefetch=4,
        grid=(n // MOE_CHUNK,),
        in_specs=[pl.BlockSpec(memory_space=pl.ANY),
                  pl.BlockSpec((SUBLANES, MOE_CHUNK), lambda c, off, gran, exp, base: (0, c)),
                  pl.BlockSpec((MOE_CHUNK, d), chunk_row),
                  pl.BlockSpec((1, 1, d), lambda c, off, gran, exp, base: (mod_row_of_chunk(c), 0, 5)),
                  pl.BlockSpec((1, d), lambda c, off, gran, exp, base: (0, 0))],
        out_specs=pl.BlockSpec((MOE_CHUNK, d), chunk_row),
        scratch_shapes=[pltpu.VMEM((2, COMBINE_SLOTS * COMBINE_GRANULE, d), BF16),
                        pltpu.SemaphoreType.DMA((2,)),
                        pltpu.VMEM((MOE_CHUNK, d), F32)],
    )
    return pl.pallas_call(
        _combine_kernel,
        grid_spec=grid_spec,
        out_shape=jax.ShapeDtypeStruct((n, d), F32),
        compiler_params=_params("arbitrary"),
        name="combine_residual",
    )(chunk_off, *chunk_items, yb, ti_t, x2, mod, g_post_f.reshape(1, d))


def _grid_pos_embedding(rows, d):
    row = jnp.repeat(jnp.arange(rows, dtype=F32), GRID_W)
    col = jnp.tile(jnp.arange(GRID_W, dtype=F32), rows)
    quarter = d // 4
    omega = POS_BASE ** (-jnp.arange(quarter, dtype=F32) / quarter)

    def emb(p):
        ang = p[:, None] * omega[None, :]
        return jnp.concatenate([jnp.sin(ang), jnp.cos(ang)], axis=-1)

    return jnp.concatenate([emb(row), emb(col)], axis=-1)


def _count_le(sorted_vals, queries):
    return jnp.sum(sorted_vals <= queries[:, None], axis=1).astype(jnp.int32)


def _moe_plan(cnt_after, n_tok, tm):
    assert tm == GATHER_SUB
    sub_after = cnt_after[:, 0, :N_EXPERTS].astype(jnp.int32)
    counts = sub_after[-1]
    padded = (counts + EXPERT_BLOCK - 1) // EXPERT_BLOCK * EXPERT_BLOCK
    padded_end = jnp.cumsum(padded)
    padded_start = padded_end - padded
    n_blocks = n_tok * TOP_K // EXPERT_BLOCK + N_EXPERTS
    blocks = jnp.arange(n_blocks, dtype=jnp.int32)
    block_expert = jnp.minimum(_count_le(padded_end[None, :], blocks * EXPERT_BLOCK), N_EXPERTS - 1)
    n_used = (padded_end[-1:] // EXPERT_BLOCK).astype(jnp.int32)
    r_lo = (blocks - padded_start[block_expert] // EXPERT_BLOCK) * EXPERT_BLOCK
    r_hi = jnp.minimum(counts[block_expert], r_lo + EXPERT_BLOCK) - 1
    through = sub_after.T[block_expert]
    n_sub = sub_after.shape[0]
    sub_lo = jnp.minimum(_count_le(through, r_lo), n_sub - 1)
    sub_hi = jnp.minimum(_count_le(through, r_hi), n_sub - 1)
    nch = n_tok // MOE_CHUNK
    per_chunk = MOE_CHUNK // tm
    cb_after = sub_after[per_chunk - 1::per_chunk]
    cb_before = jnp.concatenate([jnp.zeros((1, N_EXPERTS), jnp.int32), cb_after[:-1]], axis=0)
    b_lo = ((padded_start[None, :] + cb_before) // COMBINE_GRANULE).reshape(-1)
    b_hi = ((padded_start[None, :] + cb_after - 1) // COMBINE_GRANULE).reshape(-1)
    n_it = jnp.where((cb_after > cb_before).reshape(-1), b_hi - b_lo + 1, 0)
    off_end = jnp.cumsum(n_it)
    off = off_end - n_it
    n_items = n_blocks * (EXPERT_BLOCK // COMBINE_GRANULE) + N_EXPERTS * (nch - 1)
    w = jnp.minimum(jnp.arange(n_items, dtype=jnp.int32), off_end[-1] - 1)
    chunk_end = off_end[N_EXPERTS - 1::N_EXPERTS]
    chunk_off = jnp.concatenate([jnp.zeros((1,), jnp.int32), chunk_end]).astype(jnp.int32)
    chunk_of = jnp.minimum(_count_le(chunk_end[None, :], w), nch - 1)
    chunk_expert = jnp.minimum(_count_le(off_end.reshape(nch, N_EXPERTS)[chunk_of], w), N_EXPERTS - 1)
    cell = chunk_of * N_EXPERTS + chunk_expert
    chunk_granules = (b_lo[cell] + w - off[cell]).astype(jnp.int32)
    chunk_base = chunk_granules * COMBINE_GRANULE - padded_start[chunk_expert]
    return block_expert, r_lo, sub_lo, sub_hi, n_used, chunk_off, (chunk_granules, chunk_expert, chunk_base)


def kernel(x, c, ctx, c_ctx, w_mod, b_mod, g_pre_mix, g_post_mix, g_pre_ffn, g_post_ffn, w_in, conv_kv,
           conv_q, a_log, dt_bias, gdn_norm_g, w_proj_a, conf_dw, conf_dw_b, conf_ln_g, conf_ln_b,
           w_proj_b, w_out, w_router, b_router, w_gate_up, b_gate_up, w_down, b_down):
    batch, seq, d = x.shape
    ctx_len = ctx.shape[1]
    n_tok = batch * seq
    gw = HEADS * DV
    beta_off = HEADS * DK + gw
    state_cols = beta_off + 4 * HEADS
    q_off = state_cols

    mod_rows = 2 * SUBLANES
    cond = jnp.zeros((mod_rows, d), F32).at[:batch].set(c).at[batch].set(c_ctx)
    mod = modulation(cond, w_mod[0], b_mod[0]).reshape(mod_rows, 1, 6 * d)

    w = w_in[0]
    w_main = projection_weights(w, keep=beta_off, skip=state_cols - beta_off)
    w_ba = jnp.zeros((d, LANES), F32).at[:, :4 * HEADS].set(w[:, beta_off:state_cols]).astype(BF16)

    pos = _grid_pos_embedding(seq // GRID_W, d)
    x_flat = x.reshape(n_tok, d)
    tm = 1024
    tiles_per_seq = seq // tm
    proj, ba = input_projection(x_flat, pos, mod, lambda i: i // tiles_per_seq, g_pre_mix[0],
                                w_main, w_ba, tm=tm, tn=2048)
    proj_ctx, ba_ctx = input_projection(ctx.reshape(batch * ctx_len, d), None, mod, lambda i: batch,
                                        g_pre_mix[0], w_main[:, :beta_off], w_ba, tm=ctx_len, tn=1024)

    og = gated_deltanet(proj, ba, proj_ctx, ba_ctx, conv_kv[0], conv_q[0], a_log[0], dt_bias[0],
                        gdn_norm_g[0], batch=batch, seq=seq, ctx=ctx_len)
    uc = conformer_conv(proj, conf_dw[0], conf_dw_b[0], batch=batch, seq=seq)

    tm2 = 512
    w_r = jnp.zeros((d, LANES), F32).at[:, :N_EXPERTS].set(w_router[0]).astype(BF16)
    b_r = jnp.zeros((1, LANES), F32).at[0, :N_EXPERTS].set(b_router[0])
    x2, h2, top_i, top_w, cnt_after = mixer_output(
        og, uc, proj, x_flat, pos, mod, lambda i: i // (seq // tm2), g_post_mix[0], g_pre_ffn[0],
        conf_ln_g[0], conf_ln_b[0], w_proj_a[0].astype(BF16), w_proj_b[0].astype(BF16),
        w_out[0].astype(BF16), w_r, b_r, tm=tm2, sub=GATHER_SUB)

    block_expert, rank_lo, sub_lo, sub_hi, n_used, chunk_off, chunk_items = _moe_plan(cnt_after, n_tok, GATHER_SUB)
    yb = expert_blocks(h2, top_i, top_w, block_expert, rank_lo, sub_lo, sub_hi, n_used, w_gate_up[0], b_gate_up[0],
                       w_down[0], b_down[0])
    out = combine_residual(yb, top_i, x2, chunk_off, chunk_items, mod, lambda c: c // (seq // MOE_CHUNK),
                           g_post_ffn[0])
    return out.reshape(batch, seq, d)
```

```python
import functools

import jax
import jax.numpy as jnp
from jax import lax
from jax.experimental import pallas as pl
from jax.experimental.pallas import tpu as pltpu

F32 = jnp.float32
BF16 = jnp.bfloat16

D_MODEL = 1024
GRID_W = 64
HEADS = 8
DK = 128
DV = 128
SHORT_CONV = 5
CHUNK = 64
CONF_KERNEL = 31
N_EXPERTS = 32
TOP_K = 4
SWIGLU_LIMIT = 7.0
SWIGLU_ALPHA = 1.702
MOE_BLOCK = 128
EPS = 1e-6
POS_BASE = 10000.0

LANES = 128
SUBLANES = 8
VMEM_LIMIT = 60 * 1024 * 1024

COL_K, COL_V, COL_Q, COL_Z, COL_GLU_A, COL_GLU_G, COL_GATE_A, COL_GATE_B = range(8)


def _params(*sem):
    return pltpu.CompilerParams(dimension_semantics=sem, vmem_limit_bytes=VMEM_LIMIT)


def _mod_kernel(c_ref, w_ref, b_ref, o_ref):
    c = c_ref[...]
    s = c * jax.nn.sigmoid(c)
    o_ref[...] = jnp.dot(s, w_ref[...], preferred_element_type=F32,
                         precision=lax.Precision.HIGHEST) + b_ref[...]


def modulation(cond, w_mod, b_mod):
    r, d = cond.shape
    n = w_mod.shape[1]
    tn = 1024
    return pl.pallas_call(
        _mod_kernel,
        grid=(n // tn,),
        in_specs=[pl.BlockSpec((r, d), lambda j: (0, 0)),
                  pl.BlockSpec((d, tn), lambda j: (0, j)),
                  pl.BlockSpec((1, tn), lambda j: (0, j))],
        out_specs=pl.BlockSpec((r, tn), lambda j: (0, j)),
        out_shape=jax.ShapeDtypeStruct((r, n), F32),
        compiler_params=_params("arbitrary"),
        name="modulation",
    )(cond, w_mod, b_mod.reshape(1, n))


def _proj_weights_kernel(a_ref, b_ref, o_ref, *, first_shifted, shift):
    @pl.when(pl.program_id(0) < first_shifted)
    def _():
        o_ref[...] = a_ref[...].astype(o_ref.dtype)

    @pl.when(pl.program_id(0) >= first_shifted)
    def _():
        o_ref[...] = jnp.concatenate([a_ref[:, shift:], b_ref[:, :shift]], axis=1).astype(o_ref.dtype)


def projection_weights(w, *, keep, skip, tn=1024):
    d, total = w.shape
    n_out = total - skip
    assert keep % tn == 0 and n_out % tn == 0 and 0 < skip < LANES
    per_tile = tn // LANES
    return pl.pallas_call(
        functools.partial(_proj_weights_kernel, first_shifted=keep // tn, shift=skip),
        grid=(n_out // tn,),
        in_specs=[pl.BlockSpec((d, tn), lambda j: (0, j)),
                  pl.BlockSpec((d, LANES), lambda j: (0, (j + 1) * per_tile))],
        out_specs=pl.BlockSpec((d, tn), lambda j: (0, j)),
        out_shape=jax.ShapeDtypeStruct((d, n_out), BF16),
        compiler_params=_params("arbitrary"),
        name="projection_weights",
    )(w, w)


def _inproj_kernel(*refs, has_pos):
    if has_pos:
        x_ref, pos_ref, sh_ref, sc_ref, g_ref, w_ref, wba_ref, o_ref, ba_ref, h_scr = refs
    else:
        x_ref, sh_ref, sc_ref, g_ref, w_ref, wba_ref, o_ref, ba_ref, h_scr = refs

    @pl.when(pl.program_id(1) == 0)
    def _():
        x = x_ref[...]
        if has_pos:
            x = x + pos_ref[...]
        y = x * lax.rsqrt(jnp.mean(x * x, axis=-1, keepdims=True) + EPS) * g_ref[...]
        h = (y * (1.0 + sc_ref[0]) + sh_ref[0]).astype(BF16)
        h_scr[...] = h
        ba_ref[...] = jnp.dot(h, wba_ref[...], preferred_element_type=F32)

    o_ref[...] = jnp.dot(h_scr[...], w_ref[...], preferred_element_type=F32).astype(o_ref.dtype)


def input_projection(x, pos, mod, mod_row_of_tile, g_pre, w_main, w_ba, *, tm, tn):
    n, d = x.shape
    w = w_main.shape[1]
    has_pos = pos is not None
    in_specs = [pl.BlockSpec((tm, d), lambda i, j: (i, 0))]
    args = [x]
    if has_pos:
        pos_tiles = pos.shape[0] // tm
        in_specs.append(pl.BlockSpec((tm, d), lambda i, j: (i % pos_tiles, 0)))
        args.append(pos)
    in_specs += [
        pl.BlockSpec((1, 1, d), lambda i, j: (mod_row_of_tile(i), 0, 0)),
        pl.BlockSpec((1, 1, d), lambda i, j: (mod_row_of_tile(i), 0, 1)),
        pl.BlockSpec((1, d), lambda i, j: (0, 0)),
        pl.BlockSpec((d, tn), lambda i, j: (0, j)),
        pl.BlockSpec((d, LANES), lambda i, j: (0, 0)),
    ]
    args += [mod, mod, g_pre.reshape(1, d), w_main, w_ba]
    return pl.pallas_call(
        functools.partial(_inproj_kernel, has_pos=has_pos),
        grid=(n // tm, w // tn),
        in_specs=in_specs,
        out_specs=[pl.BlockSpec((tm, tn), lambda i, j: (i, j)),
                   pl.BlockSpec((tm, LANES), lambda i, j: (i, 0))],
        out_shape=[jax.ShapeDtypeStruct((n, w), BF16), jax.ShapeDtypeStruct((n, LANES), F32)],
        scratch_shapes=[pltpu.VMEM((tm, d), BF16)],
        compiler_params=_params("arbitrary", "arbitrary"),
        name="input_projection_pos" if has_pos else "input_projection_ctx",
    )(*args)


def _silu(x):
    return x * jax.nn.sigmoid(x)


def _softplus(x):
    return jnp.maximum(x, 0.0) + jnp.log(1.0 + jnp.exp(-jnp.abs(x)))


def _short_conv(src, taps, pad_scr, rows):
    half = SHORT_CONV // 2
    pad_scr[0:SUBLANES, :] = jnp.zeros((SUBLANES, LANES), F32)
    pad_scr[SUBLANES:SUBLANES + rows, :] = src.astype(F32)
    pad_scr[SUBLANES + rows:2 * SUBLANES + rows, :] = jnp.zeros((SUBLANES, LANES), F32)
    acc = None
    for j in range(SHORT_CONV):
        off = SUBLANES + j - half
        term = pad_scr[off:off + rows, :] * taps[j:j + 1, :]
        acc = term if acc is None else acc + term
    return _silu(acc)


def _l2n(x):
    return x * lax.rsqrt(jnp.sum(x * x, axis=-1, keepdims=True) + EPS)


def _gdn_kernel(k_ref, v_ref, q_ref, z_ref, ba_ref, kc_ref, vc_ref, bac_ref,
                wk_ref, wv_ref, wq_ref, ab_ref, gn_ref, o_ref,
                kf, vf, qf, gates, pad_scr, lhs, val, att, ktt, dec, osc, *, seq, ctx, unroll, group):
    total = ctx + seq
    nc = total // CHUNK
    ncc = ctx // CHUNK
    two = 2 * CHUNK
    lg_chunk = CHUNK.bit_length() - 1

    ri = lax.broadcasted_iota(jnp.int32, (two, two), 0)
    ci = lax.broadcasted_iota(jnp.int32, (two, two), 1)
    same_dir = (ri >> lg_chunk) == (ci >> lg_chunk)
    incl = same_dir & (((ri < CHUNK) & (ri >= ci)) | ((ri >= CHUNK) & (ri <= ci)))
    strict = incl & (ri != ci)
    eye = jnp.where(ri == ci, 1.0, 0.0)
    tri = jnp.where(incl, 1.0, 0.0).astype(BF16)
    level_masks = []
    for lg in range(lg_chunk):
        same_parent = (ri >> (lg + 1)) == (ci >> (lg + 1))
        level_masks.append(jnp.where(same_parent & ((ri >> lg) != (ci >> lg)), 1.0, 0.0))
    top_rows = lax.broadcasted_iota(jnp.int32, (two, LANES), 0) < CHUNK

    def bwd_chunk(t):
        return jnp.where(t < ncc, ncc - 1 - t, nc + ncc - 1 - t)

    def stacked(ref, rf, rb):
        return jnp.concatenate([ref[pl.ds(rf, CHUNK), :], ref[pl.ds(rb, CHUNK), :]], axis=0)

    def chunk_load(t):
        rf = pl.multiple_of(t * CHUNK, CHUNK)
        rb = pl.multiple_of(bwd_chunk(t) * CHUNK, CHUNK)
        beta = jnp.concatenate([gates[0, pl.ds(rf, CHUNK), :], gates[1, pl.ds(rb, CHUNK), :]], axis=0)
        g = jnp.concatenate([gates[2, pl.ds(rf, CHUNK), :], gates[3, pl.ds(rb, CHUNK), :]], axis=0)
        return stacked(kf, rf, rb), stacked(vf, rf, rb), stacked(qf, rf, rb), beta, g

    def chunk_triangle(k, q, beta, g):
        kb = k.astype(BF16)
        kq = jnp.concatenate([kb, q.astype(BF16)], axis=0)
        kk_qk = lax.dot_general(kq, kb, (((1,), (1,)), ((), ())), preferred_element_type=F32)
        g_hi = g.astype(BF16)
        r1 = g - g_hi.astype(F32)
        g_mid = r1.astype(BF16)
        g_lo = (r1 - g_mid.astype(F32)).astype(BF16)
        gc3 = jnp.dot(tri, jnp.concatenate([g_hi, g_mid, g_lo], axis=1), preferred_element_type=F32)
        gc = gc3[:, :LANES] + gc3[:, LANES:2 * LANES] + gc3[:, 2 * LANES:]
        gc_row = jnp.transpose(gc)
        decay = jnp.where(incl, jnp.exp(gc - gc_row), 0.0)
        a = jnp.where(strict, beta * kk_qk[:two] * decay, 0.0)
        return a, (kk_qk[two:] * decay).astype(BF16), gc

    def chunk_finish(hh, t, k, v, q, beta, gc, attn, tmat):
        eg = jnp.exp(gc)
        rhs = jnp.concatenate([v * beta, k * beta * eg], axis=1).astype(BF16)
        sol = jnp.dot(tmat.astype(BF16), rhs, preferred_element_type=F32)
        g_end = jnp.where(top_rows, jnp.broadcast_to(gc[CHUNK - 1:CHUNK, :], (two, LANES)),
                          jnp.broadcast_to(gc[CHUNK:CHUNK + 1, :], (two, LANES)))
        k_tail = k * jnp.exp(g_end - gc)
        qg = (q * eg).astype(BF16)
        kcum = sol[:, DV:].astype(BF16)
        r0 = pl.multiple_of(t * two, two)
        lhs[hh, 0, pl.ds(r0, two), :] = jnp.concatenate([kcum[:CHUNK], qg[:CHUNK]], axis=0)
        lhs[hh, 1, pl.ds(r0, two), :] = jnp.concatenate([kcum[CHUNK:], qg[CHUNK:]], axis=0)
        val[hh, pl.ds(r0, two), :] = sol[:, :DV].astype(BF16)
        att[hh, pl.ds(r0, two), :] = attn
        ktt[hh, pl.ds(r0, two), :] = jnp.transpose(k_tail).astype(BF16)
        d0 = pl.multiple_of(t * SUBLANES, SUBLANES)
        e_end = jnp.exp(g_end)
        dec[hh, 0, pl.ds(d0, SUBLANES), :] = e_end[:SUBLANES]
        dec[hh, 1, pl.ds(d0, SUBLANES), :] = e_end[CHUNK:CHUNK + SUBLANES]

    for hh in range(group):
        head = pl.program_id(1) * group + hh
        cols = slice(hh * LANES, (hh + 1) * LANES)

        kf[0:ctx, :] = _l2n(_short_conv(kc_ref[:, cols], wk_ref[:, cols], pad_scr, ctx))
        vf[0:ctx, :] = _short_conv(vc_ref[:, cols], wv_ref[:, cols], pad_scr, ctx)
        qf[0:ctx, :] = jnp.zeros((ctx, LANES), F32)
        kf[ctx:total, :] = _l2n(_short_conv(k_ref[:, cols], wk_ref[:, cols], pad_scr, seq))
        vf[ctx:total, :] = _short_conv(v_ref[:, cols], wv_ref[:, cols], pad_scr, seq)
        qf[ctx:total, :] = _l2n(_short_conv(q_ref[:, cols], wq_ref[:, cols], pad_scr, seq)) * (DK ** -0.5)

        def gate_cols(src_ref, lo, rows):
            x = src_ref[...]
            lane = lax.broadcasted_iota(jnp.int32, (rows, LANES), 1)
            beta = jax.nn.sigmoid(x)
            g = -jnp.exp(ab_ref[0:1, :]) * _softplus(x + ab_ref[1:2, :])
            for slot, (arr, base) in enumerate(((beta, 0), (beta, HEADS), (g, 2 * HEADS), (g, 3 * HEADS))):
                col = jnp.sum(jnp.where(lane == base + head, arr, 0.0), axis=-1, keepdims=True)
                gates[slot, lo:lo + rows, :] = jnp.broadcast_to(col, (rows, LANES))

        gate_cols(bac_ref, 0, ctx)
        gate_cols(ba_ref, ctx, seq)

        def chunk_body(i, carry, hh=hh):
            steps = [i * unroll + u for u in range(unroll)]
            loaded = [chunk_load(t) for t in steps]
            tris = [chunk_triangle(k, q, beta, g) for (k, v, q, beta, g) in loaded]
            tmats = [eye - a * level_masks[0] for (a, _, _) in tris]
            for m in level_masks[1:]:
                nxt = []
                for (a, _, _), tmat in zip(tris, tmats):
                    cs = (a * m).astype(BF16)
                    tb = tmat.astype(BF16)
                    tc = jnp.dot(tb, cs, preferred_element_type=F32)
                    nxt.append(tmat - jnp.dot(tc.astype(BF16), tb, preferred_element_type=F32))
                tmats = nxt
            for t, (k, v, q, beta, g), (a, attn, gc), tmat in zip(steps, loaded, tris, tmats):
                chunk_finish(hh, t, k, v, q, beta, gc, attn, tmat)
            return carry

        lax.fori_loop(0, nc // unroll, chunk_body, 0)

    def scan_body(t, carry):
        rf = pl.multiple_of(t * CHUNK, CHUNK)
        rb = pl.multiple_of(bwd_chunk(t) * CHUNK, CHUNK)
        r0 = pl.multiple_of(t * two, two)
        d0 = pl.multiple_of(t * SUBLANES, SUBLANES)
        rs = [(jnp.dot(lhs[hh, 0, pl.ds(r0, two), :], carry[2 * hh].astype(BF16), preferred_element_type=F32),
               jnp.dot(lhs[hh, 1, pl.ds(r0, two), :], carry[2 * hh + 1].astype(BF16), preferred_element_type=F32))
              for hh in range(group)]
        v_news = [val[hh, pl.ds(r0, two), :].astype(F32) - jnp.concatenate([r_f[:CHUNK], r_b[:CHUNK]], axis=0)
                  for hh, (r_f, r_b) in enumerate(rs)]
        outs, new_states = [], []
        for hh, ((r_f, r_b), v_new) in enumerate(zip(rs, v_news)):
            zeros = jnp.zeros_like(v_new)
            v_bd = jnp.concatenate([jnp.where(top_rows, v_new, zeros), jnp.where(top_rows, zeros, v_new)],
                                   axis=1).astype(BF16)
            upd = jnp.dot(ktt[hh, pl.ds(r0, two), :], v_bd, preferred_element_type=F32)
            new_states.append(carry[2 * hh] * dec[hh, 0, pl.ds(d0, SUBLANES), :][0:1, :] + upd[:, :DV])
            new_states.append(carry[2 * hh + 1] * dec[hh, 1, pl.ds(d0, SUBLANES), :][0:1, :] + upd[:, DV:])
            outs.append(jnp.concatenate([r_f[CHUNK:], r_b[CHUNK:]], axis=0)
                        + jnp.dot(att[hh, pl.ds(r0, two), :], v_new.astype(BF16), preferred_element_type=F32))
        for hh, o in enumerate(outs):
            osc[hh, pl.ds(rf, CHUNK), :] += o[:CHUNK]
            osc[hh, pl.ds(rb, CHUNK), :] += o[CHUNK:]
        return tuple(new_states)

    osc[...] = jnp.zeros_like(osc)
    zero = jnp.zeros((DK, DV), F32)
    lax.fori_loop(0, nc, scan_body, (zero,) * (2 * group))

    for hh in range(group):
        cols = slice(hh * LANES, (hh + 1) * LANES)
        o = osc[hh, ctx:total, :]
        o = o * lax.rsqrt(jnp.mean(o * o, axis=-1, keepdims=True) + EPS) * gn_ref[...]
        o_ref[:, cols] = (o * _silu(z_ref[:, cols].astype(F32))).astype(o_ref.dtype)


def gated_deltanet(proj, ba, proj_ctx, ba_ctx, conv_kv, conv_q, a_log, dt_bias, gn, *, batch, seq, ctx,
                   unroll=12, group=4):
    total = seq + ctx
    nc = total // CHUNK
    gl = group * LANES
    ng = HEADS // group
    taps = jnp.zeros((SUBLANES, conv_kv.shape[1]), F32).at[:SHORT_CONV].set(conv_kv)
    taps_q = jnp.zeros((SUBLANES, conv_q.shape[1]), F32).at[:SHORT_CONV].set(conv_q)
    ab = jnp.zeros((SUBLANES, LANES), F32)
    ab = ab.at[0, 2 * HEADS:4 * HEADS].set(a_log.reshape(-1)).at[1, 2 * HEADS:4 * HEADS].set(dt_bias.reshape(-1))
    col = lambda base: (lambda b, h: (b, base * ng + h))
    once = pl.Buffered(1)
    return pl.pallas_call(
        functools.partial(_gdn_kernel, seq=seq, ctx=ctx, unroll=unroll, group=group),
        grid=(batch, ng),
        in_specs=[
            pl.BlockSpec((seq, gl), col(COL_K), pipeline_mode=once),
            pl.BlockSpec((seq, gl), col(COL_V), pipeline_mode=once),
            pl.BlockSpec((seq, gl), col(COL_Q), pipeline_mode=once),
            pl.BlockSpec((seq, gl), col(COL_Z), pipeline_mode=once),
            pl.BlockSpec((seq, LANES), lambda b, h: (b, 0), pipeline_mode=once),
            pl.BlockSpec((ctx, gl), col(COL_K)),
            pl.BlockSpec((ctx, gl), col(COL_V)),
            pl.BlockSpec((ctx, LANES), lambda b, h: (b, 0)),
            pl.BlockSpec((SUBLANES, gl), lambda b, h: (0, h)),
            pl.BlockSpec((SUBLANES, gl), lambda b, h: (0, ng + h)),
            pl.BlockSpec((SUBLANES, gl), lambda b, h: (0, h)),
            pl.BlockSpec((SUBLANES, LANES), lambda b, h: (0, 0)),
            pl.BlockSpec((1, LANES), lambda b, h: (0, 0)),
        ],
        out_specs=pl.BlockSpec((seq, gl), lambda b, h: (b, h)),
        out_shape=jax.ShapeDtypeStruct((batch * seq, HEADS * DV), BF16),
        scratch_shapes=[
            pltpu.VMEM((total, LANES), F32),
            pltpu.VMEM((total, LANES), F32),
            pltpu.VMEM((total, LANES), F32),
            pltpu.VMEM((4, total, LANES), F32),
            pltpu.VMEM((seq + 2 * SUBLANES, LANES), F32),
            pltpu.VMEM((group, 2, nc * 2 * CHUNK, LANES), BF16),
            pltpu.VMEM((group, nc * 2 * CHUNK, LANES), BF16),
            pltpu.VMEM((group, nc * 2 * CHUNK, LANES), BF16),
            pltpu.VMEM((group, nc * 2 * CHUNK, LANES), BF16),
            pltpu.VMEM((group, 2, nc * SUBLANES, LANES), F32),
            pltpu.VMEM((group, total, LANES), F32),
        ],
        compiler_params=_params("arbitrary", "arbitrary"),
        name="gated_deltanet",
    )(proj, proj, proj, proj, ba, proj_ctx, proj_ctx, ba_ctx, taps, taps, taps_q, ab, gn.reshape(1, DV))


CONF_ROWS = 128
CONF_HALO = 16


def _conf_kernel(a_ref, g_ref, w_ref, b_ref, o_ref, pad_scr, *, seq):
    cb = a_ref.shape[1]
    u = a_ref[...].astype(F32) * jax.nn.sigmoid(g_ref[...].astype(F32))
    pad_scr[0:CONF_HALO, :] = jnp.zeros((CONF_HALO, cb), F32)
    pad_scr[CONF_HALO:CONF_HALO + seq, :] = u
    pad_scr[CONF_HALO + seq:2 * CONF_HALO + seq, :] = jnp.zeros((CONF_HALO, cb), F32)
    win = CONF_ROWS + 2 * CONF_HALO
    first = CONF_HALO - CONF_KERNEL // 2

    def body(i, carry):
        r0 = pl.multiple_of(i * CONF_ROWS, CONF_ROWS)
        w = pad_scr[pl.ds(r0, win), :]
        acc = jnp.zeros((CONF_ROWS, cb), F32) + b_ref[...]
        for sub in range(SUBLANES):
            shifted = w if sub == 0 else pltpu.roll(w, win - sub, axis=0)
            for j in range(CONF_KERNEL):
                off = first + j
                if off % SUBLANES == sub:
                    base = off - sub
                    acc = acc + shifted[base:base + CONF_ROWS, :] * w_ref[j:j + 1, :]
        o_ref[pl.ds(r0, CONF_ROWS), :] = acc.astype(o_ref.dtype)
        return carry

    lax.fori_loop(0, seq // CONF_ROWS, body, 0)


def conformer_conv(proj, conf_dw, conf_dw_b, *, batch, seq, cb=256):
    c = conf_dw.shape[1]
    nb = c // cb
    taps = jnp.zeros((32, c), F32).at[:CONF_KERNEL].set(conf_dw)
    return pl.pallas_call(
        functools.partial(_conf_kernel, seq=seq),
        grid=(batch, nb),
        in_specs=[pl.BlockSpec((seq, cb), lambda b, j: (b, COL_GLU_A * nb + j)),
                  pl.BlockSpec((seq, cb), lambda b, j: (b, COL_GLU_G * nb + j)),
                  pl.BlockSpec((32, cb), lambda b, j: (0, j)),
                  pl.BlockSpec((1, cb), lambda b, j: (0, j))],
        out_specs=pl.BlockSpec((seq, cb), lambda b, j: (b, j)),
        out_shape=jax.ShapeDtypeStruct((batch * seq, c), BF16),
        scratch_shapes=[pltpu.VMEM((seq + 2 * CONF_HALO, cb), F32)],
        compiler_params=_params("arbitrary", "arbitrary"),
        name="conformer_conv",
    )(proj, proj, taps, conf_dw_b.reshape(1, c))


def _mixer_out_kernel(og_ref, uc_ref, ga_ref, gb_ref, x_ref, pos_ref, gm_ref, shf_ref, scf_ref,
                      gpost_ref, gpre_ref, lng_ref, lnb_ref, wa_ref, wb_ref, wo_ref, wr_ref, br_ref,
                      x2_ref, h2_ref, ti_ref, tw_ref, cnt_ref, cnt_scr, *, sub):
    parts = [slice(r, r + sub) for r in range(0, og_ref.shape[0], sub)]

    def rms(v):
        return v * lax.rsqrt(jnp.mean(v * v, axis=-1, keepdims=True) + EPS)

    def layer_norm_silu(uc):
        mu = jnp.mean(uc, axis=-1, keepdims=True)
        var = jnp.mean(jnp.square(uc - mu), axis=-1, keepdims=True)
        return _silu((uc - mu) * lax.rsqrt(var + EPS) * lng_ref[...] + lnb_ref[...]).astype(BF16)

    us = [layer_norm_silu(uc_ref[p, :].astype(F32)) for p in parts]
    yas = [jnp.dot(og_ref[p, :], wa_ref[...], preferred_element_type=F32) for p in parts]
    ybs = [jnp.dot(u, wb_ref[...], preferred_element_type=F32) for u in us]
    merged = [(jax.nn.sigmoid(ga_ref[p, :].astype(F32)) * ya
               + jax.nn.sigmoid(gb_ref[p, :].astype(F32)) * yb).astype(BF16) for p, ya, yb in zip(parts, yas, ybs)]
    ys = [jnp.dot(m, wo_ref[...], preferred_element_type=F32) for m in merged]
    x2s = [x_ref[p, :] + pos_ref[p, :] + gm_ref[0] * (rms(y) * gpost_ref[...]) for p, y in zip(parts, ys)]
    h2s = [(rms(x2) * gpre_ref[...] * (1.0 + scf_ref[0]) + shf_ref[0]).astype(BF16) for x2 in x2s]
    for p, x2, h2 in zip(parts, x2s, h2s):
        x2_ref[p, :] = x2
        h2_ref[p, :] = h2
    all_logits = [jnp.dot(h2, wr_ref[...], preferred_element_type=F32) + br_ref[...] for h2 in h2s]
    lane = lax.broadcasted_iota(jnp.int32, (sub, LANES), 1).astype(F32)
    ri = lax.broadcasted_iota(jnp.int32, (sub, sub), 0)
    ci = lax.broadcasted_iota(jnp.int32, (sub, sub), 1)
    before = jnp.where(ri > ci, 1.0, 0.0).astype(BF16)

    @pl.when(pl.program_id(0) == 0)
    def _():
        cnt_scr[...] = jnp.zeros_like(cnt_scr)

    routed = []
    for logits in all_logits:
        live = jnp.where(lane < N_EXPERTS, logits, -jnp.inf)
        top_v, top_i = [], []
        for _ in range(TOP_K):
            m = jnp.max(live, axis=-1, keepdims=True)
            idx = jnp.min(jnp.where(live == m, lane, float(LANES)), axis=-1, keepdims=True)
            top_v.append(m)
            top_i.append(idx)
            live = jnp.where(lane == idx, -jnp.inf, live)
        ex = [jnp.exp(v - top_v[0]) for v in top_v]
        denom = ex[0] + ex[1] + ex[2] + ex[3]
        picked = jnp.zeros((sub, LANES), F32)
        for k in range(TOP_K):
            picked = jnp.where(lane == top_i[k], 1.0, picked)
        routed.append((top_i, [e / denom for e in ex], picked,
                       jnp.dot(before, picked.astype(BF16), preferred_element_type=F32)))

    count = cnt_scr[...]
    for j, (p, (top_i, top_w, picked, within)) in enumerate(zip(parts, routed)):
        prefix = within + count[0:1, :]
        ti = jnp.zeros((sub, LANES), F32)
        tw = jnp.zeros((sub, LANES), F32)
        for k in range(TOP_K):
            rank = jnp.sum(jnp.where(lane == top_i[k], prefix, 0.0), axis=-1, keepdims=True)
            ti = jnp.where(lane == k, top_i[k], ti)
            ti = jnp.where(lane == TOP_K + k, rank, ti)
            tw = jnp.where(lane == k, top_w[k], tw)
        ti_ref[:, p] = jnp.transpose(ti)[:SUBLANES, :].astype(jnp.int32)
        tw_ref[:, p] = jnp.transpose(tw)[:SUBLANES, :]
        count = count + jnp.sum(picked, axis=0, keepdims=True)
        cnt_ref[j] = count
    cnt_scr[...] = count


def mixer_output(og, uc, proj, x, pos, mod, mod_row_of_tile, g_post, g_pre_f, ln_g, ln_b,
                 w_a, w_b, w_o, w_r, b_r, *, tm, sub):
    n, d = x.shape
    pos_tiles = pos.shape[0] // tm
    row = lambda i: (i, 0)
    fixed = lambda i: (0, 0)
    modspec = lambda k: pl.BlockSpec((1, 1, d), lambda i: (mod_row_of_tile(i), 0, k))
    vec = pl.BlockSpec((1, d), fixed)
    mat = pl.BlockSpec((d, d), fixed)
    return pl.pallas_call(
        functools.partial(_mixer_out_kernel, sub=sub),
        grid=(n // tm,),
        in_specs=[pl.BlockSpec((tm, d), row), pl.BlockSpec((tm, d), row),
                  pl.BlockSpec((tm, d), lambda i: (i, COL_GATE_A)),
                  pl.BlockSpec((tm, d), lambda i: (i, COL_GATE_B)),
                  pl.BlockSpec((tm, d), row),
                  pl.BlockSpec((tm, d), lambda i: (i % pos_tiles, 0)),
                  modspec(2), modspec(3), modspec(4),
                  vec, vec, vec, vec, mat, mat, mat,
                  pl.BlockSpec((d, LANES), fixed), pl.BlockSpec((1, LANES), fixed)],
        out_specs=[pl.BlockSpec((tm, d), row), pl.BlockSpec((tm, d), row),
                   pl.BlockSpec((SUBLANES, tm), lambda i: (0, i)), pl.BlockSpec((SUBLANES, tm), lambda i: (0, i)),
                   pl.BlockSpec((tm // sub, SUBLANES, LANES), lambda i: (i, 0, 0))],
        out_shape=[jax.ShapeDtypeStruct((n, d), F32), jax.ShapeDtypeStruct((n, d), BF16),
                   jax.ShapeDtypeStruct((SUBLANES, n), jnp.int32), jax.ShapeDtypeStruct((SUBLANES, n), F32),
                   jax.ShapeDtypeStruct((n // sub, SUBLANES, LANES), F32)],
        scratch_shapes=[pltpu.VMEM((SUBLANES, LANES), F32)],
        compiler_params=_params("arbitrary"),
        name="mixer_output",
    )(og, uc, proj, proj, x, pos, mod, mod, mod,
      g_post.reshape(1, d), g_pre_f.reshape(1, d), ln_g.reshape(1, d), ln_b.reshape(1, d),
      w_a, w_b, w_o, w_r, b_r)


EXPERT_BLOCK = 256
MOE_CHUNK = 512
GATHER_SUB = 256
COMBINE_GRANULE = 32
COMBINE_SLOTS = 32


def _expert_kernel(bexp_ref, rlo_ref, slo_ref, shi_ref, nused_ref,
                   h2_ref, ti_ref, wt_ref, wgu_ref, bgu_ref, wd_ref, bd_ref, o_ref,
                   acc, rw_acc, wgu_bf, wd_bf):
    b = pl.program_id(0)
    expert = bexp_ref[b]

    @pl.when((b == 0) | (expert != bexp_ref[jnp.maximum(b - 1, 0)]))
    def _():
        wgu_bf[...] = wgu_ref[0].astype(BF16)
        wd_bf[...] = wd_ref[0].astype(BF16)

    @pl.when(b < nused_ref[0])
    def _():
        acc[...] = jnp.zeros_like(acc)
        rw_acc[...] = jnp.zeros_like(rw_acc)
        row = lax.broadcasted_iota(jnp.int32, (EXPERT_BLOCK, GATHER_SUB), 0)
        s_lo, s_hi = slo_ref[b], shi_ref[b]
        last_sub = h2_ref.shape[0] // GATHER_SUB - 1

        def select(s, live):
            t0 = pl.multiple_of(s * GATHER_SUB, GATHER_SUB)
            ids = ti_ref[0:TOP_K, pl.ds(t0, GATHER_SUB)]
            local = ti_ref[TOP_K:2 * TOP_K, pl.ds(t0, GATHER_SUB)] - rlo_ref[b]
            if live is not True:
                local = jnp.where(live, local, -1)
            inside = (ids == expert) & (local >= 0) & (local < EXPERT_BLOCK)
            hit_row = jnp.sum(jnp.where(inside, local + 1, 0), axis=0, keepdims=True) - 1
            hit_w = jnp.sum(jnp.where(inside, wt_ref[0:TOP_K, pl.ds(t0, GATHER_SUB)], 0.0), axis=0, keepdims=True)
            hit = row == hit_row
            return (jnp.where(hit, 1.0, 0.0).astype(BF16), jnp.where(hit, hit_w, 0.0),
                    h2_ref[pl.ds(t0, GATHER_SUB), :])

        def gather(i, carry):
            s0 = s_lo + 2 * i
            p0, w0, x0 = select(s0, True)
            p1, w1, x1 = select(jnp.minimum(s0 + 1, last_sub), s0 + 1 <= s_hi)
            acc[...] += jnp.dot(jnp.concatenate([p0, p1], axis=1), jnp.concatenate([x0, x1], axis=0),
                                preferred_element_type=F32)
            rw_acc[...] += jnp.sum(w0 + w1, axis=1, keepdims=True)
            return carry

        lax.fori_loop(0, (s_hi - s_lo + 2) // 2, gather, 0)
        gu = jnp.dot(acc[...].astype(BF16), wgu_bf[...], preferred_element_type=F32) + bgu_ref[0]
        dff = gu.shape[1] // 2
        gl = jnp.minimum(gu[:, :dff], SWIGLU_LIMIT)
        lin = jnp.clip(gu[:, dff:], -SWIGLU_LIMIT, SWIGLU_LIMIT)
        act = (gl * jax.nn.sigmoid(SWIGLU_ALPHA * gl) * (lin + 1.0)).astype(BF16)
        y = jnp.dot(act, wd_bf[...], preferred_element_type=F32) + bd_ref[0]
        o_ref[...] = (y * rw_acc[...]).astype(o_ref.dtype)

    @pl.when(b >= nused_ref[0])
    def _():
        o_ref[...] = jnp.zeros_like(o_ref)


def expert_blocks(h2, ti_t, wt_t, block_expert, rank_lo, sub_lo, sub_hi, n_used, w_gate_up, b_gate_up, w_down,
                  b_down):
    n, d = h2.shape
    e, _, f2 = w_gate_up.shape
    n_blocks = block_expert.shape[0]
    whole = lambda b, be, rl, lo, hi, nu: (0, 0)
    by_expert = lambda b, be, rl, lo, hi, nu: (be[b], 0, 0)
    once = pl.Buffered(1)
    grid_spec = pltpu.PrefetchScalarGridSpec(
        num_scalar_prefetch=5,
        grid=(n_blocks,),
        in_specs=[pl.BlockSpec((n, d), whole, pipeline_mode=once),
                  pl.BlockSpec((SUBLANES, n), whole, pipeline_mode=once),
                  pl.BlockSpec((SUBLANES, n), whole, pipeline_mode=once),
                  pl.BlockSpec((1, d, f2), by_expert, pipeline_mode=once),
                  pl.BlockSpec((1, 1, f2), by_expert),
                  pl.BlockSpec((1, f2 // 2, d), by_expert),
                  pl.BlockSpec((1, 1, d), by_expert)],
        out_specs=pl.BlockSpec((EXPERT_BLOCK, d), lambda b, be, rl, lo, hi, nu: (b, 0)),
        scratch_shapes=[pltpu.VMEM((EXPERT_BLOCK, d), F32), pltpu.VMEM((EXPERT_BLOCK, 1), F32),
                        pltpu.VMEM((d, f2), BF16), pltpu.VMEM((f2 // 2, d), BF16)],
    )
    return pl.pallas_call(
        _expert_kernel,
        grid_spec=grid_spec,
        out_shape=jax.ShapeDtypeStruct((n_blocks * EXPERT_BLOCK, d), BF16),
        compiler_params=_params("arbitrary"),
        name="expert_blocks",
    )(block_expert, rank_lo, sub_lo, sub_hi, n_used, h2, ti_t, wt_t, w_gate_up, b_gate_up.reshape(e, 1, f2), w_down,
      b_down.reshape(e, 1, d))


def _combine_kernel(coff_ref, cgran_ref, cexp_ref, cbase_ref, yb_hbm, ti_ref, x2_ref, gf_ref, g_ref, o_ref,
                    slab, sem, acc):
    c = pl.program_id(0)
    start = coff_ref[c]
    count = coff_ref[c + 1] - start
    n_groups = (count + COMBINE_SLOTS - 1) // COMBINE_SLOTS

    def item(g, s):
        return start + jnp.minimum(g * COMBINE_SLOTS + s, count - 1)

    def granule_of(g, s):
        return cgran_ref[item(g, s)]

    def copies(g, buf):
        return [pltpu.make_async_copy(
            yb_hbm.at[pl.ds(pl.multiple_of(granule_of(g, s) * COMBINE_GRANULE, COMBINE_GRANULE), COMBINE_GRANULE), :],
            slab.at[buf, pl.ds(s * COMBINE_GRANULE, COMBINE_GRANULE), :], sem.at[buf])
            for s in range(COMBINE_SLOTS)]

    for cp in copies(0, 0):
        cp.start()
    acc[...] = jnp.zeros_like(acc)
    chunk = x2_ref.shape[0]
    row = lax.broadcasted_iota(jnp.int32, (COMBINE_GRANULE, chunk), 0)
    ids = ti_ref[0:TOP_K, :]
    ranks = ti_ref[TOP_K:2 * TOP_K, :]

    def group(g, carry):
        buf = g % 2

        @pl.when(g + 1 < n_groups)
        def _():
            for cp in copies(g + 1, 1 - buf):
                cp.start()

        for cp in copies(g, buf):
            cp.wait()
        pieces = []
        for s in range(COMBINE_SLOTS):
            live = g * COMBINE_SLOTS + s < count
            local = ranks - cbase_ref[item(g, s)]
            inside = (ids == jnp.where(live, cexp_ref[item(g, s)], -1)) & (local >= 0) & (local < COMBINE_GRANULE)
            hit_row = jnp.sum(jnp.where(inside, local + 1, 0), axis=0, keepdims=True) - 1
            pieces.append(jnp.where(row == hit_row, 1.0, 0.0).astype(BF16))
        onehot_t = jnp.concatenate(pieces, axis=0)
        acc[...] += lax.dot_general(onehot_t, slab[buf], (((0,), (0,)), ((), ())),
                                    preferred_element_type=F32)
        return carry

    lax.fori_loop(0, n_groups, group, 0)
    y = acc[...]
    yn = y * lax.rsqrt(jnp.mean(y * y, axis=-1, keepdims=True) + EPS) * g_ref[...]
    o_ref[...] = x2_ref[...] + gf_ref[0] * yn


def combine_residual(yb, ti_t, x2, chunk_off, chunk_items, mod, mod_row_of_chunk, g_post_f):
    n, d = x2.shape
    chunk_row = lambda c, off, gran, exp, base: (c, 0)
    grid_spec = pltpu.PrefetchScalarGridSpec(
        num_scalar_prefetch=4,
        grid=(n // MOE_CHUNK,),
        in_specs=[pl.BlockSpec(memory_space=pl.ANY),
                  pl.BlockSpec((SUBLANES, MOE_CHUNK), lambda c, off, gran, exp, base: (0, c)),
                  pl.BlockSpec((MOE_CHUNK, d), chunk_row),
                  pl.BlockSpec((1, 1, d), lambda c, off, gran, exp, base: (mod_row_of_chunk(c), 0, 5)),
                  pl.BlockSpec((1, d), lambda c, off, gran, exp, base: (0, 0))],
        out_specs=pl.BlockSpec((MOE_CHUNK, d), chunk_row),
        scratch_shapes=[pltpu.VMEM((2, COMBINE_SLOTS * COMBINE_GRANULE, d), BF16),
                        pltpu.SemaphoreType.DMA((2,)),
                        pltpu.VMEM((MOE_CHUNK, d), F32)],
    )
    return pl.pallas_call(
        _combine_kernel,
        grid_spec=grid_spec,
        out_shape=jax.ShapeDtypeStruct((n, d), F32),
        compiler_params=_params("arbitrary"),
        name="combine_residual",
    )(chunk_off, *chunk_items, yb, ti_t, x2, mod, g_post_f.reshape(1, d))


def _grid_pos_embedding(rows, d):
    row = jnp.repeat(jnp.arange(rows, dtype=F32), GRID_W)
    col = jnp.tile(jnp.arange(GRID_W, dtype=F32), rows)
    quarter = d // 4
    omega = POS_BASE ** (-jnp.arange(quarter, dtype=F32) / quarter)

    def emb(p):
        ang = p[:, None] * omega[None, :]
        return jnp.concatenate([jnp.sin(ang), jnp.cos(ang)], axis=-1)

    return jnp.concatenate([emb(row), emb(col)], axis=-1)


def _count_le(sorted_vals, queries):
    return jnp.sum(sorted_vals <= queries[:, None], axis=1).astype(jnp.int32)


def _moe_plan(cnt_after, n_tok, tm):
    assert tm == GATHER_SUB
    sub_after = cnt_after[:, 0, :N_EXPERTS].astype(jnp.int32)
    counts = sub_after[-1]
    padded = (counts + EXPERT_BLOCK - 1) // EXPERT_BLOCK * EXPERT_BLOCK
    padded_end = jnp.cumsum(padded)
    padded_start = padded_end - padded
    n_blocks = n_tok * TOP_K // EXPERT_BLOCK + N_EXPERTS
    blocks = jnp.arange(n_blocks, dtype=jnp.int32)
    block_expert = jnp.minimum(_count_le(padded_end[None, :], blocks * EXPERT_BLOCK), N_EXPERTS - 1)
    n_used = (padded_end[-1:] // EXPERT_BLOCK).astype(jnp.int32)
    of_block = (block_expert[:, None] == jnp.arange(N_EXPERTS, dtype=jnp.int32)[None, :]).astype(jnp.int32)
    r_lo = (blocks - jnp.sum(of_block * padded_start[None, :], axis=1) // EXPERT_BLOCK) * EXPERT_BLOCK
    r_hi = jnp.minimum(jnp.sum(of_block * counts[None, :], axis=1), r_lo + EXPERT_BLOCK) - 1
    through = jnp.sum(of_block[:, :, None] * sub_after.T[None, :, :], axis=1)
    n_sub = sub_after.shape[0]
    sub_lo = jnp.minimum(_count_le(through, r_lo), n_sub - 1)
    sub_hi = jnp.minimum(_count_le(through, r_hi), n_sub - 1)
    nch = n_tok // MOE_CHUNK
    per_chunk = MOE_CHUNK // tm
    cb_after = sub_after[per_chunk - 1::per_chunk]
    cb_before = jnp.concatenate([jnp.zeros((1, N_EXPERTS), jnp.int32), cb_after[:-1]], axis=0)
    b_lo = ((padded_start[None, :] + cb_before) // COMBINE_GRANULE).reshape(-1)
    b_hi = ((padded_start[None, :] + cb_after - 1) // COMBINE_GRANULE).reshape(-1)
    n_it = jnp.where((cb_after > cb_before).reshape(-1), b_hi - b_lo + 1, 0)
    off_end = jnp.cumsum(n_it)
    off = off_end - n_it
    n_items = n_blocks * (EXPERT_BLOCK // COMBINE_GRANULE) + N_EXPERTS * (nch - 1)
    w = jnp.minimum(jnp.arange(n_items, dtype=jnp.int32), off_end[-1] - 1)
    chunk_end = off_end[N_EXPERTS - 1::N_EXPERTS]
    chunk_off = jnp.concatenate([jnp.zeros((1,), jnp.int32), chunk_end]).astype(jnp.int32)
    chunk_of = jnp.minimum(_count_le(chunk_end[None, :], w), nch - 1)
    chunk_expert = jnp.minimum(_count_le(off_end.reshape(nch, N_EXPERTS)[chunk_of], w), N_EXPERTS - 1)
    cell = chunk_of * N_EXPERTS + chunk_expert
    chunk_granules = (b_lo[cell] + w - off[cell]).astype(jnp.int32)
    chunk_base = chunk_granules * COMBINE_GRANULE - padded_start[chunk_expert]
    return block_expert, r_lo, sub_lo, sub_hi, n_used, chunk_off, (chunk_granules, chunk_expert, chunk_base)


def kernel(x, c, ctx, c_ctx, w_mod, b_mod, g_pre_mix, g_post_mix, g_pre_ffn, g_post_ffn, w_in, conv_kv,
           conv_q, a_log, dt_bias, gdn_norm_g, w_proj_a, conf_dw, conf_dw_b, conf_ln_g, conf_ln_b,
           w_proj_b, w_out, w_router, b_router, w_gate_up, b_gate_up, w_down, b_down):
    batch, seq, d = x.shape
    ctx_len = ctx.shape[1]
    n_tok = batch * seq
    gw = HEADS * DV
    beta_off = HEADS * DK + gw
    state_cols = beta_off + 4 * HEADS
    q_off = state_cols

    mod_rows = 2 * SUBLANES
    cond = jnp.zeros((mod_rows, d), F32).at[:batch].set(c).at[batch].set(c_ctx)
    mod = modulation(cond, w_mod[0], b_mod[0]).reshape(mod_rows, 1, 6 * d)

    w = w_in[0]
    w_main = projection_weights(w, keep=beta_off, skip=state_cols - beta_off)
    w_ba = jnp.zeros((d, LANES), F32).at[:, :4 * HEADS].set(w[:, beta_off:state_cols]).astype(BF16)

    pos = _grid_pos_embedding(seq // GRID_W, d)
    x_flat = x.reshape(n_tok, d)
    tm = 1024
    tiles_per_seq = seq // tm
    proj, ba = input_projection(x_flat, pos, mod, lambda i: i // tiles_per_seq, g_pre_mix[0],
                                w_main, w_ba, tm=tm, tn=2048)
    proj_ctx, ba_ctx = input_projection(ctx.reshape(batch * ctx_len, d), None, mod, lambda i: batch,
                                        g_pre_mix[0], w_main[:, :beta_off], w_ba, tm=ctx_len, tn=1024)

    og = gated_deltanet(proj, ba, proj_ctx, ba_ctx, conv_kv[0], conv_q[0], a_log[0], dt_bias[0],
                        gdn_norm_g[0], batch=batch, seq=seq, ctx=ctx_len)
    uc = conformer_conv(proj, conf_dw[0], conf_dw_b[0], batch=batch, seq=seq)

    tm2 = 512
    w_r = jnp.zeros((d, LANES), F32).at[:, :N_EXPERTS].set(w_router[0]).astype(BF16)
    b_r = jnp.zeros((1, LANES), F32).at[0, :N_EXPERTS].set(b_router[0])
    x2, h2, top_i, top_w, cnt_after = mixer_output(
        og, uc, proj, x_flat, pos, mod, lambda i: i // (seq // tm2), g_post_mix[0], g_pre_ffn[0],
        conf_ln_g[0], conf_ln_b[0], w_proj_a[0].astype(BF16), w_proj_b[0].astype(BF16),
        w_out[0].astype(BF16), w_r, b_r, tm=tm2, sub=GATHER_SUB)

    block_expert, rank_lo, sub_lo, sub_hi, n_used, chunk_off, chunk_items = _moe_plan(cnt_after, n_tok, GATHER_SUB)
    yb = expert_blocks(h2, top_i, top_w, block_expert, rank_lo, sub_lo, sub_hi, n_used, w_gate_up[0], b_gate_up[0],
                       w_down[0], b_down[0])
    out = combine_residual(yb, top_i, x2, chunk_off, chunk_items, mod, lambda c: c // (seq // MOE_CHUNK),
                           g_post_ffn[0])
    return out.reshape(batch, seq, d)
```

```python
import functools

import jax
import jax.numpy as jnp
from jax import lax
from jax.experimental import pallas as pl
from jax.experimental.pallas import tpu as pltpu

F32 = jnp.float32
BF16 = jnp.bfloat16

D_MODEL = 1024
GRID_W = 64
HEADS = 8
DK = 128
DV = 128
SHORT_CONV = 5
CHUNK = 64
CONF_KERNEL = 31
N_EXPERTS = 32
TOP_K = 4
SWIGLU_LIMIT = 7.0
SWIGLU_ALPHA = 1.702
MOE_BLOCK = 128
EPS = 1e-6
POS_BASE = 10000.0

LANES = 128
SUBLANES = 8
VMEM_LIMIT = 60 * 1024 * 1024

COL_K, COL_V, COL_Q, COL_Z, COL_GLU_A, COL_GLU_G, COL_GATE_A, COL_GATE_B = range(8)


def _params(*sem):
    return pltpu.CompilerParams(dimension_semantics=sem, vmem_limit_bytes=VMEM_LIMIT)


def _mod_kernel(c_ref, w_ref, b_ref, o_ref):
    c = c_ref[...]
    s = c * jax.nn.sigmoid(c)
    o_ref[...] = jnp.dot(s, w_ref[...], preferred_element_type=F32,
                         precision=lax.Precision.HIGHEST) + b_ref[...]


def modulation(cond, w_mod, b_mod):
    r, d = cond.shape
    n = w_mod.shape[1]
    tn = 1024
    return pl.pallas_call(
        _mod_kernel,
        grid=(n // tn,),
        in_specs=[pl.BlockSpec((r, d), lambda j: (0, 0)),
                  pl.BlockSpec((d, tn), lambda j: (0, j)),
                  pl.BlockSpec((1, tn), lambda j: (0, j))],
        out_specs=pl.BlockSpec((r, tn), lambda j: (0, j)),
        out_shape=jax.ShapeDtypeStruct((r, n), F32),
        compiler_params=_params("arbitrary"),
        name="modulation",
    )(cond, w_mod, b_mod.reshape(1, n))


def _proj_weights_kernel(a_ref, b_ref, o_ref, *, first_shifted, shift):
    @pl.when(pl.program_id(0) < first_shifted)
    def _():
        o_ref[...] = a_ref[...].astype(o_ref.dtype)

    @pl.when(pl.program_id(0) >= first_shifted)
    def _():
        o_ref[...] = jnp.concatenate([a_ref[:, shift:], b_ref[:, :shift]], axis=1).astype(o_ref.dtype)


def projection_weights(w, *, keep, skip, tn=1024):
    d, total = w.shape
    n_out = total - skip
    assert keep % tn == 0 and n_out % tn == 0 and 0 < skip < LANES
    per_tile = tn // LANES
    return pl.pallas_call(
        functools.partial(_proj_weights_kernel, first_shifted=keep // tn, shift=skip),
        grid=(n_out // tn,),
        in_specs=[pl.BlockSpec((d, tn), lambda j: (0, j)),
                  pl.BlockSpec((d, LANES), lambda j: (0, (j + 1) * per_tile))],
        out_specs=pl.BlockSpec((d, tn), lambda j: (0, j)),
        out_shape=jax.ShapeDtypeStruct((d, n_out), BF16),
        compiler_params=_params("arbitrary"),
        name="projection_weights",
    )(w, w)


def _inproj_kernel(*refs, has_pos):
    if has_pos:
        x_ref, pos_ref, sh_ref, sc_ref, g_ref, w_ref, wba_ref, o_ref, ba_ref, h_scr = refs
    else:
        x_ref, sh_ref, sc_ref, g_ref, w_ref, wba_ref, o_ref, ba_ref, h_scr = refs

    @pl.when(pl.program_id(1) == 0)
    def _():
        x = x_ref[...]
        if has_pos:
            x = x + pos_ref[...]
        y = x * lax.rsqrt(jnp.mean(x * x, axis=-1, keepdims=True) + EPS) * g_ref[...]
        h = (y * (1.0 + sc_ref[0]) + sh_ref[0]).astype(BF16)
        h_scr[...] = h
        ba_ref[...] = jnp.dot(h, wba_ref[...], preferred_element_type=F32)

    o_ref[...] = jnp.dot(h_scr[...], w_ref[...], preferred_element_type=F32).astype(o_ref.dtype)


def input_projection(x, pos, mod, mod_row_of_tile, g_pre, w_main, w_ba, *, tm, tn):
    n, d = x.shape
    w = w_main.shape[1]
    has_pos = pos is not None
    in_specs = [pl.BlockSpec((tm, d), lambda i, j: (i, 0))]
    args = [x]
    if has_pos:
        pos_tiles = pos.shape[0] // tm
        in_specs.append(pl.BlockSpec((tm, d), lambda i, j: (i % pos_tiles, 0)))
        args.append(pos)
    in_specs += [
        pl.BlockSpec((1, 1, d), lambda i, j: (mod_row_of_tile(i), 0, 0)),
        pl.BlockSpec((1, 1, d), lambda i, j: (mod_row_of_tile(i), 0, 1)),
        pl.BlockSpec((1, d), lambda i, j: (0, 0)),
        pl.BlockSpec((d, tn), lambda i, j: (0, j)),
        pl.BlockSpec((d, LANES), lambda i, j: (0, 0)),
    ]
    args += [mod, mod, g_pre.reshape(1, d), w_main, w_ba]
    return pl.pallas_call(
        functools.partial(_inproj_kernel, has_pos=has_pos),
        grid=(n // tm, w // tn),
        in_specs=in_specs,
        out_specs=[pl.BlockSpec((tm, tn), lambda i, j: (i, j)),
                   pl.BlockSpec((tm, LANES), lambda i, j: (i, 0))],
        out_shape=[jax.ShapeDtypeStruct((n, w), BF16), jax.ShapeDtypeStruct((n, LANES), F32)],
        scratch_shapes=[pltpu.VMEM((tm, d), BF16)],
        compiler_params=_params("arbitrary", "arbitrary"),
        name="input_projection_pos" if has_pos else "input_projection_ctx",
    )(*args)


def _silu(x):
    return x * jax.nn.sigmoid(x)


def _softplus(x):
    return jnp.maximum(x, 0.0) + jnp.log(1.0 + jnp.exp(-jnp.abs(x)))


def _short_conv(src, taps, pad_scr, rows):
    half = SHORT_CONV // 2
    pad_scr[0:SUBLANES, :] = jnp.zeros((SUBLANES, LANES), F32)
    pad_scr[SUBLANES:SUBLANES + rows, :] = src.astype(F32)
    pad_scr[SUBLANES + rows:2 * SUBLANES + rows, :] = jnp.zeros((SUBLANES, LANES), F32)
    acc = None
    for j in range(SHORT_CONV):
        off = SUBLANES + j - half
        term = pad_scr[off:off + rows, :] * taps[j:j + 1, :]
        acc = term if acc is None else acc + term
    return _silu(acc)


def _l2n(x):
    return x * lax.rsqrt(jnp.sum(x * x, axis=-1, keepdims=True) + EPS)


def _gdn_kernel(k_ref, v_ref, q_ref, z_ref, ba_ref, kc_ref, vc_ref, bac_ref,
                wk_ref, wv_ref, wq_ref, ab_ref, gn_ref, o_ref,
                kf, vf, qf, gates, pad_scr, lhs, val, att, ktt, dec, osc, *, seq, ctx, unroll, group):
    total = ctx + seq
    nc = total // CHUNK
    ncc = ctx // CHUNK
    two = 2 * CHUNK
    lg_chunk = CHUNK.bit_length() - 1

    ri = lax.broadcasted_iota(jnp.int32, (two, two), 0)
    ci = lax.broadcasted_iota(jnp.int32, (two, two), 1)
    same_dir = (ri >> lg_chunk) == (ci >> lg_chunk)
    incl = same_dir & (((ri < CHUNK) & (ri >= ci)) | ((ri >= CHUNK) & (ri <= ci)))
    strict = incl & (ri != ci)
    eye = jnp.where(ri == ci, 1.0, 0.0)
    tri = jnp.where(incl, 1.0, 0.0).astype(BF16)
    level_masks = []
    for lg in range(lg_chunk):
        same_parent = (ri >> (lg + 1)) == (ci >> (lg + 1))
        level_masks.append(jnp.where(same_parent & ((ri >> lg) != (ci >> lg)), 1.0, 0.0))
    top_rows = lax.broadcasted_iota(jnp.int32, (two, LANES), 0) < CHUNK

    def bwd_chunk(t):
        return jnp.where(t < ncc, ncc - 1 - t, nc + ncc - 1 - t)

    def stacked(ref, rf, rb):
        return jnp.concatenate([ref[pl.ds(rf, CHUNK), :], ref[pl.ds(rb, CHUNK), :]], axis=0)

    def chunk_load(t):
        rf = pl.multiple_of(t * CHUNK, CHUNK)
        rb = pl.multiple_of(bwd_chunk(t) * CHUNK, CHUNK)
        beta = jnp.concatenate([gates[0, pl.ds(rf, CHUNK), :], gates[1, pl.ds(rb, CHUNK), :]], axis=0)
        g = jnp.concatenate([gates[2, pl.ds(rf, CHUNK), :], gates[3, pl.ds(rb, CHUNK), :]], axis=0)
        return stacked(kf, rf, rb), stacked(vf, rf, rb), stacked(qf, rf, rb), beta, g

    def chunk_triangle(k, q, beta, g):
        kb = k.astype(BF16)
        kq = jnp.concatenate([kb, q.astype(BF16)], axis=0)
        kk_qk = lax.dot_general(kq, kb, (((1,), (1,)), ((), ())), preferred_element_type=F32)
        g_hi = g.astype(BF16)
        r1 = g - g_hi.astype(F32)
        g_mid = r1.astype(BF16)
        g_lo = (r1 - g_mid.astype(F32)).astype(BF16)
        gc3 = jnp.dot(tri, jnp.concatenate([g_hi, g_mid, g_lo], axis=1), preferred_element_type=F32)
        gc = gc3[:, :LANES] + gc3[:, LANES:2 * LANES] + gc3[:, 2 * LANES:]
        gc_row = jnp.transpose(gc)
        decay = jnp.where(incl, jnp.exp(gc - gc_row), 0.0)
        a = jnp.where(strict, beta * kk_qk[:two] * decay, 0.0)
        return a, (kk_qk[two:] * decay).astype(BF16), gc

    def chunk_finish(hh, t, k, v, q, beta, gc, attn, tmat):
        eg = jnp.exp(gc)
        rhs = jnp.concatenate([v * beta, k * beta * eg], axis=1).astype(BF16)
        sol = jnp.dot(tmat.astype(BF16), rhs, preferred_element_type=F32)
        g_end = jnp.where(top_rows, jnp.broadcast_to(gc[CHUNK - 1:CHUNK, :], (two, LANES)),
                          jnp.broadcast_to(gc[CHUNK:CHUNK + 1, :], (two, LANES)))
        k_tail = k * jnp.exp(g_end - gc)
        qg = (q * eg).astype(BF16)
        kcum = sol[:, DV:].astype(BF16)
        r0 = pl.multiple_of(t * two, two)
        lhs[hh, 0, pl.ds(r0, two), :] = jnp.concatenate([kcum[:CHUNK], qg[:CHUNK]], axis=0)
        lhs[hh, 1, pl.ds(r0, two), :] = jnp.concatenate([kcum[CHUNK:], qg[CHUNK:]], axis=0)
        val[hh, pl.ds(r0, two), :] = sol[:, :DV].astype(BF16)
        att[hh, pl.ds(r0, two), :] = attn
        ktt[hh, pl.ds(r0, two), :] = jnp.transpose(k_tail).astype(BF16)
        d0 = pl.multiple_of(t * SUBLANES, SUBLANES)
        e_end = jnp.exp(g_end)
        dec[hh, 0, pl.ds(d0, SUBLANES), :] = e_end[:SUBLANES]
        dec[hh, 1, pl.ds(d0, SUBLANES), :] = e_end[CHUNK:CHUNK + SUBLANES]

    for hh in range(group):
        head = pl.program_id(1) * group + hh
        cols = slice(hh * LANES, (hh + 1) * LANES)

        kf[0:ctx, :] = _l2n(_short_conv(kc_ref[:, cols], wk_ref[:, cols], pad_scr, ctx))
        vf[0:ctx, :] = _short_conv(vc_ref[:, cols], wv_ref[:, cols], pad_scr, ctx)
        qf[0:ctx, :] = jnp.zeros((ctx, LANES), F32)
        kf[ctx:total, :] = _l2n(_short_conv(k_ref[:, cols], wk_ref[:, cols], pad_scr, seq))
        vf[ctx:total, :] = _short_conv(v_ref[:, cols], wv_ref[:, cols], pad_scr, seq)
        qf[ctx:total, :] = _l2n(_short_conv(q_ref[:, cols], wq_ref[:, cols], pad_scr, seq)) * (DK ** -0.5)

        def gate_cols(src_ref, lo, rows):
            x = src_ref[...]
            lane = lax.broadcasted_iota(jnp.int32, (rows, LANES), 1)
            beta = jax.nn.sigmoid(x)
            g = -jnp.exp(ab_ref[0:1, :]) * _softplus(x + ab_ref[1:2, :])
            for slot, (arr, base) in enumerate(((beta, 0), (beta, HEADS), (g, 2 * HEADS), (g, 3 * HEADS))):
                col = jnp.sum(jnp.where(lane == base + head, arr, 0.0), axis=-1, keepdims=True)
                gates[slot, lo:lo + rows, :] = jnp.broadcast_to(col, (rows, LANES))

        gate_cols(bac_ref, 0, ctx)
        gate_cols(ba_ref, ctx, seq)

        def chunk_body(i, carry, hh=hh):
            steps = [i * unroll + u for u in range(unroll)]
            loaded = [chunk_load(t) for t in steps]
            tris = [chunk_triangle(k, q, beta, g) for (k, v, q, beta, g) in loaded]
            tmats = [eye - a * level_masks[0] for (a, _, _) in tris]
            for m in level_masks[1:]:
                nxt = []
                for (a, _, _), tmat in zip(tris, tmats):
                    cs = (a * m).astype(BF16)
                    tb = tmat.astype(BF16)
                    tc = jnp.dot(tb, cs, preferred_element_type=F32)
                    nxt.append(tmat - jnp.dot(tc.astype(BF16), tb, preferred_element_type=F32))
                tmats = nxt
            for t, (k, v, q, beta, g), (a, attn, gc), tmat in zip(steps, loaded, tris, tmats):
                chunk_finish(hh, t, k, v, q, beta, gc, attn, tmat)
            return carry

        lax.fori_loop(0, nc // unroll, chunk_body, 0)

    def scan_body(t, carry):
        rf = pl.multiple_of(t * CHUNK, CHUNK)
        rb = pl.multiple_of(bwd_chunk(t) * CHUNK, CHUNK)
        r0 = pl.multiple_of(t * two, two)
        d0 = pl.multiple_of(t * SUBLANES, SUBLANES)
        rs = [(jnp.dot(lhs[hh, 0, pl.ds(r0, two), :], carry[2 * hh].astype(BF16), preferred_element_type=F32),
               jnp.dot(lhs[hh, 1, pl.ds(r0, two), :], carry[2 * hh + 1].astype(BF16), preferred_element_type=F32))
              for hh in range(group)]
        v_news = [val[hh, pl.ds(r0, two), :].astype(F32) - jnp.concatenate([r_f[:CHUNK], r_b[:CHUNK]], axis=0)
                  for hh, (r_f, r_b) in enumerate(rs)]
        outs, new_states = [], []
        for hh, ((r_f, r_b), v_new) in enumerate(zip(rs, v_news)):
            zeros = jnp.zeros_like(v_new)
            v_bd = jnp.concatenate([jnp.where(top_rows, v_new, zeros), jnp.where(top_rows, zeros, v_new)],
                                   axis=1).astype(BF16)
            upd = jnp.dot(ktt[hh, pl.ds(r0, two), :], v_bd, preferred_element_type=F32)
            new_states.append(carry[2 * hh] * dec[hh, 0, pl.ds(d0, SUBLANES), :][0:1, :] + upd[:, :DV])
            new_states.append(carry[2 * hh + 1] * dec[hh, 1, pl.ds(d0, SUBLANES), :][0:1, :] + upd[:, DV:])
            outs.append(jnp.concatenate([r_f[CHUNK:], r_b[CHUNK:]], axis=0)
                        + jnp.dot(att[hh, pl.ds(r0, two), :], v_new.astype(BF16), preferred_element_type=F32))
        for hh, o in enumerate(outs):
            osc[hh, pl.ds(rf, CHUNK), :] += o[:CHUNK]
            osc[hh, pl.ds(rb, CHUNK), :] += o[CHUNK:]
        return tuple(new_states)

    osc[...] = jnp.zeros_like(osc)
    zero = jnp.zeros((DK, DV), F32)
    lax.fori_loop(0, nc, scan_body, (zero,) * (2 * group))

    for hh in range(group):
        cols = slice(hh * LANES, (hh + 1) * LANES)
        o = osc[hh, ctx:total, :]
        o = o * lax.rsqrt(jnp.mean(o * o, axis=-1, keepdims=True) + EPS) * gn_ref[...]
        o_ref[:, cols] = (o * _silu(z_ref[:, cols].astype(F32))).astype(o_ref.dtype)


def gated_deltanet(proj, ba, proj_ctx, ba_ctx, conv_kv, conv_q, a_log, dt_bias, gn, *, batch, seq, ctx,
                   unroll=12, group=4):
    total = seq + ctx
    nc = total // CHUNK
    gl = group * LANES
    ng = HEADS // group
    taps = jnp.zeros((SUBLANES, conv_kv.shape[1]), F32).at[:SHORT_CONV].set(conv_kv)
    taps_q = jnp.zeros((SUBLANES, conv_q.shape[1]), F32).at[:SHORT_CONV].set(conv_q)
    ab = jnp.zeros((SUBLANES, LANES), F32)
    ab = ab.at[0, 2 * HEADS:4 * HEADS].set(a_log.reshape(-1)).at[1, 2 * HEADS:4 * HEADS].set(dt_bias.reshape(-1))
    col = lambda base: (lambda b, h: (b, base * ng + h))
    once = pl.Buffered(1)
    return pl.pallas_call(
        functools.partial(_gdn_kernel, seq=seq, ctx=ctx, unroll=unroll, group=group),
        grid=(batch, ng),
        in_specs=[
            pl.BlockSpec((seq, gl), col(COL_K), pipeline_mode=once),
            pl.BlockSpec((seq, gl), col(COL_V), pipeline_mode=once),
            pl.BlockSpec((seq, gl), col(COL_Q), pipeline_mode=once),
            pl.BlockSpec((seq, gl), col(COL_Z), pipeline_mode=once),
            pl.BlockSpec((seq, LANES), lambda b, h: (b, 0), pipeline_mode=once),
            pl.BlockSpec((ctx, gl), col(COL_K)),
            pl.BlockSpec((ctx, gl), col(COL_V)),
            pl.BlockSpec((ctx, LANES), lambda b, h: (b, 0)),
            pl.BlockSpec((SUBLANES, gl), lambda b, h: (0, h)),
            pl.BlockSpec((SUBLANES, gl), lambda b, h: (0, ng + h)),
            pl.BlockSpec((SUBLANES, gl), lambda b, h: (0, h)),
            pl.BlockSpec((SUBLANES, LANES), lambda b, h: (0, 0)),
            pl.BlockSpec((1, LANES), lambda b, h: (0, 0)),
        ],
        out_specs=pl.BlockSpec((seq, gl), lambda b, h: (b, h)),
        out_shape=jax.ShapeDtypeStruct((batch * seq, HEADS * DV), BF16),
        scratch_shapes=[
            pltpu.VMEM((total, LANES), F32),
            pltpu.VMEM((total, LANES), F32),
            pltpu.VMEM((total, LANES), F32),
            pltpu.VMEM((4, total, LANES), F32),
            pltpu.VMEM((seq + 2 * SUBLANES, LANES), F32),
            pltpu.VMEM((group, 2, nc * 2 * CHUNK, LANES), BF16),
            pltpu.VMEM((group, nc * 2 * CHUNK, LANES), BF16),
            pltpu.VMEM((group, nc * 2 * CHUNK, LANES), BF16),
            pltpu.VMEM((group, nc * 2 * CHUNK, LANES), BF16),
            pltpu.VMEM((group, 2, nc * SUBLANES, LANES), F32),
            pltpu.VMEM((group, total, LANES), F32),
        ],
        compiler_params=_params("arbitrary", "arbitrary"),
        name="gated_deltanet",
    )(proj, proj, proj, proj, ba, proj_ctx, proj_ctx, ba_ctx, taps, taps, taps_q, ab, gn.reshape(1, DV))


CONF_ROWS = 128
CONF_HALO = 16


def _conf_kernel(a_ref, g_ref, w_ref, b_ref, o_ref, pad_scr, *, seq):
    cb = a_ref.shape[1]
    u = a_ref[...].astype(F32) * jax.nn.sigmoid(g_ref[...].astype(F32))
    pad_scr[0:CONF_HALO, :] = jnp.zeros((CONF_HALO, cb), F32)
    pad_scr[CONF_HALO:CONF_HALO + seq, :] = u
    pad_scr[CONF_HALO + seq:2 * CONF_HALO + seq, :] = jnp.zeros((CONF_HALO, cb), F32)
    win = CONF_ROWS + 2 * CONF_HALO
    first = CONF_HALO - CONF_KERNEL // 2

    def body(i, carry):
        r0 = pl.multiple_of(i * CONF_ROWS, CONF_ROWS)
        w = pad_scr[pl.ds(r0, win), :]
        acc = jnp.zeros((CONF_ROWS, cb), F32) + b_ref[...]
        for sub in range(SUBLANES):
            shifted = w if sub == 0 else pltpu.roll(w, win - sub, axis=0)
            for j in range(CONF_KERNEL):
                off = first + j
                if off % SUBLANES == sub:
                    base = off - sub
                    acc = acc + shifted[base:base + CONF_ROWS, :] * w_ref[j:j + 1, :]
        o_ref[pl.ds(r0, CONF_ROWS), :] = acc.astype(o_ref.dtype)
        return carry

    lax.fori_loop(0, seq // CONF_ROWS, body, 0)


def conformer_conv(proj, conf_dw, conf_dw_b, *, batch, seq, cb=256):
    c = conf_dw.shape[1]
    nb = c // cb
    taps = jnp.zeros((32, c), F32).at[:CONF_KERNEL].set(conf_dw)
    return pl.pallas_call(
        functools.partial(_conf_kernel, seq=seq),
        grid=(batch, nb),
        in_specs=[pl.BlockSpec((seq, cb), lambda b, j: (b, COL_GLU_A * nb + j)),
                  pl.BlockSpec((seq, cb), lambda b, j: (b, COL_GLU_G * nb + j)),
                  pl.BlockSpec((32, cb), lambda b, j: (0, j)),
                  pl.BlockSpec((1, cb), lambda b, j: (0, j))],
        out_specs=pl.BlockSpec((seq, cb), lambda b, j: (b, j)),
        out_shape=jax.ShapeDtypeStruct((batch * seq, c), BF16),
        scratch_shapes=[pltpu.VMEM((seq + 2 * CONF_HALO, cb), F32)],
        compiler_params=_params("arbitrary", "arbitrary"),
        name="conformer_conv",
    )(proj, proj, taps, conf_dw_b.reshape(1, c))


def _mixer_out_kernel(og_ref, uc_ref, ga_ref, gb_ref, x_ref, pos_ref, gm_ref, shf_ref, scf_ref,
                      gpost_ref, gpre_ref, lng_ref, lnb_ref, wa_ref, wb_ref, wo_ref, wr_ref, br_ref,
                      x2_ref, h2_ref, ti_ref, tw_ref, cnt_ref, cnt_scr, *, sub):
    parts = [slice(r, r + sub) for r in range(0, og_ref.shape[0], sub)]

    def rms(v):
        return v * lax.rsqrt(jnp.mean(v * v, axis=-1, keepdims=True) + EPS)

    def layer_norm_silu(uc):
        mu = jnp.mean(uc, axis=-1, keepdims=True)
        var = jnp.mean(jnp.square(uc - mu), axis=-1, keepdims=True)
        return _silu((uc - mu) * lax.rsqrt(var + EPS) * lng_ref[...] + lnb_ref[...]).astype(BF16)

    us = [layer_norm_silu(uc_ref[p, :].astype(F32)) for p in parts]
    yas = [jnp.dot(og_ref[p, :], wa_ref[...], preferred_element_type=F32) for p in parts]
    ybs = [jnp.dot(u, wb_ref[...], preferred_element_type=F32) for u in us]
    merged = [(jax.nn.sigmoid(ga_ref[p, :].astype(F32)) * ya
               + jax.nn.sigmoid(gb_ref[p, :].astype(F32)) * yb).astype(BF16) for p, ya, yb in zip(parts, yas, ybs)]
    ys = [jnp.dot(m, wo_ref[...], preferred_element_type=F32) for m in merged]
    x2s = [x_ref[p, :] + pos_ref[p, :] + gm_ref[0] * (rms(y) * gpost_ref[...]) for p, y in zip(parts, ys)]
    h2s = [(rms(x2) * gpre_ref[...] * (1.0 + scf_ref[0]) + shf_ref[0]).astype(BF16) for x2 in x2s]
    for p, x2, h2 in zip(parts, x2s, h2s):
        x2_ref[p, :] = x2
        h2_ref[p, :] = h2
    all_logits = [jnp.dot(h2, wr_ref[...], preferred_element_type=F32) + br_ref[...] for h2 in h2s]
    lane = lax.broadcasted_iota(jnp.int32, (sub, LANES), 1).astype(F32)
    ri = lax.broadcasted_iota(jnp.int32, (sub, sub), 0)
    ci = lax.broadcasted_iota(jnp.int32, (sub, sub), 1)
    before = jnp.where(ri > ci, 1.0, 0.0).astype(BF16)

    @pl.when(pl.program_id(0) == 0)
    def _():
        cnt_scr[...] = jnp.zeros_like(cnt_scr)

    routed = []
    for logits in all_logits:
        live = jnp.where(lane < N_EXPERTS, logits, -jnp.inf)
        top_v, top_i = [], []
        for _ in range(TOP_K):
            m = jnp.max(live, axis=-1, keepdims=True)
            idx = jnp.min(jnp.where(live == m, lane, float(LANES)), axis=-1, keepdims=True)
            top_v.append(m)
            top_i.append(idx)
            live = jnp.where(lane == idx, -jnp.inf, live)
        ex = [jnp.exp(v - top_v[0]) for v in top_v]
        denom = ex[0] + ex[1] + ex[2] + ex[3]
        picked = jnp.zeros((sub, LANES), F32)
        for k in range(TOP_K):
            picked = jnp.where(lane == top_i[k], 1.0, picked)
        routed.append((top_i, [e / denom for e in ex], picked,
                       jnp.dot(before, picked.astype(BF16), preferred_element_type=F32)))

    count = cnt_scr[...]
    for j, (p, (top_i, top_w, picked, within)) in enumerate(zip(parts, routed)):
        prefix = within + count[0:1, :]
        ti = jnp.zeros((sub, LANES), F32)
        tw = jnp.zeros((sub, LANES), F32)
        for k in range(TOP_K):
            rank = jnp.sum(jnp.where(lane == top_i[k], prefix, 0.0), axis=-1, keepdims=True)
            ti = jnp.where(lane == k, top_i[k], ti)
            ti = jnp.where(lane == TOP_K + k, rank, ti)
            tw = jnp.where(lane == k, top_w[k], tw)
        ti_ref[:, p] = jnp.transpose(ti)[:SUBLANES, :].astype(jnp.int32)
        tw_ref[:, p] = jnp.transpose(tw)[:SUBLANES, :]
        count = count + jnp.sum(picked, axis=0, keepdims=True)
        cnt_ref[j] = count
    cnt_scr[...] = count


def mixer_output(og, uc, proj, x, pos, mod, mod_row_of_tile, g_post, g_pre_f, ln_g, ln_b,
                 w_a, w_b, w_o, w_r, b_r, *, tm, sub):
    n, d = x.shape
    pos_tiles = pos.shape[0] // tm
    row = lambda i: (i, 0)
    fixed = lambda i: (0, 0)
    modspec = lambda k: pl.BlockSpec((1, 1, d), lambda i: (mod_row_of_tile(i), 0, k))
    vec = pl.BlockSpec((1, d), fixed)
    mat = pl.BlockSpec((d, d), fixed)
    return pl.pallas_call(
        functools.partial(_mixer_out_kernel, sub=sub),
        grid=(n // tm,),
        in_specs=[pl.BlockSpec((tm, d), row), pl.BlockSpec((tm, d), row),
                  pl.BlockSpec((tm, d), lambda i: (i, COL_GATE_A)),
                  pl.BlockSpec((tm, d), lambda i: (i, COL_GATE_B)),
                  pl.BlockSpec((tm, d), row),
                  pl.BlockSpec((tm, d), lambda i: (i % pos_tiles, 0)),
                  modspec(2), modspec(3), modspec(4),
                  vec, vec, vec, vec, mat, mat, mat,
                  pl.BlockSpec((d, LANES), fixed), pl.BlockSpec((1, LANES), fixed)],
        out_specs=[pl.BlockSpec((tm, d), row), pl.BlockSpec((tm, d), row),
                   pl.BlockSpec((SUBLANES, tm), lambda i: (0, i)), pl.BlockSpec((SUBLANES, tm), lambda i: (0, i)),
                   pl.BlockSpec((tm // sub, SUBLANES, LANES), lambda i: (i, 0, 0))],
        out_shape=[jax.ShapeDtypeStruct((n, d), F32), jax.ShapeDtypeStruct((n, d), BF16),
                   jax.ShapeDtypeStruct((SUBLANES, n), jnp.int32), jax.ShapeDtypeStruct((SUBLANES, n), F32),
                   jax.ShapeDtypeStruct((n // sub, SUBLANES, LANES), F32)],
        scratch_shapes=[pltpu.VMEM((SUBLANES, LANES), F32)],
        compiler_params=_params("arbitrary"),
        name="mixer_output",
    )(og, uc, proj, proj, x, pos, mod, mod, mod,
      g_post.reshape(1, d), g_pre_f.reshape(1, d), ln_g.reshape(1, d), ln_b.reshape(1, d),
      w_a, w_b, w_o, w_r, b_r)


EXPERT_BLOCK = 256
MOE_CHUNK = 512
GATHER_SUB = 256
COMBINE_GRANULE = 32
COMBINE_SLOTS = 32


def _expert_kernel(bexp_ref, bnext_ref, rlo_ref, slo_ref, shi_ref, nused_ref,
                   h2_ref, ti_ref, wt_ref, wgu_hbm, bgu_ref, wd_hbm, bd_ref, o_ref,
                   acc, rw_acc, wgu_bf, wd_bf, wgu_f32, wd_f32, wsem):
    b = pl.program_id(0)
    expert = bexp_ref[b]

    def fetch(e):
        return (pltpu.make_async_copy(wgu_hbm.at[e], wgu_f32, wsem.at[0]),
                pltpu.make_async_copy(wd_hbm.at[e], wd_f32, wsem.at[1]))

    @pl.when(((b == 0) | (expert != bexp_ref[jnp.maximum(b - 1, 0)])) & (b < nused_ref[0]))
    def _():
        @pl.when(b == 0)
        def _():
            for cp in fetch(expert):
                cp.start()

        for cp in fetch(expert):
            cp.wait()
        wgu_bf[...] = wgu_f32[...].astype(BF16)
        wd_bf[...] = wd_f32[...].astype(BF16)

        @pl.when(bnext_ref[b] >= 0)
        def _():
            for cp in fetch(bnext_ref[b]):
                cp.start()

    @pl.when(b < nused_ref[0])
    def _():
        acc[...] = jnp.zeros_like(acc)
        rw_acc[...] = jnp.zeros_like(rw_acc)
        row = lax.broadcasted_iota(jnp.int32, (EXPERT_BLOCK, GATHER_SUB), 0)
        s_lo, s_hi = slo_ref[b], shi_ref[b]
        last_sub = h2_ref.shape[0] // GATHER_SUB - 1

        def select(s, live):
            t0 = pl.multiple_of(s * GATHER_SUB, GATHER_SUB)
            ids = ti_ref[0:TOP_K, pl.ds(t0, GATHER_SUB)]
            local = ti_ref[TOP_K:2 * TOP_K, pl.ds(t0, GATHER_SUB)] - rlo_ref[b]
            if live is not True:
                local = jnp.where(live, local, -1)
            inside = (ids == expert) & (local >= 0) & (local < EXPERT_BLOCK)
            hit_row = jnp.sum(jnp.where(inside, local + 1, 0), axis=0, keepdims=True) - 1
            hit_w = jnp.sum(jnp.where(inside, wt_ref[0:TOP_K, pl.ds(t0, GATHER_SUB)], 0.0), axis=0, keepdims=True)
            hit = row == hit_row
            return (jnp.where(hit, 1.0, 0.0).astype(BF16), jnp.where(hit, hit_w, 0.0),
                    h2_ref[pl.ds(t0, GATHER_SUB), :])

        def gather(i, carry):
            s0 = s_lo + 2 * i
            p0, w0, x0 = select(s0, True)
            p1, w1, x1 = select(jnp.minimum(s0 + 1, last_sub), s0 + 1 <= s_hi)
            acc[...] += jnp.dot(jnp.concatenate([p0, p1], axis=1), jnp.concatenate([x0, x1], axis=0),
                                preferred_element_type=F32)
            rw_acc[...] += jnp.sum(w0 + w1, axis=1, keepdims=True)
            return carry

        lax.fori_loop(0, (s_hi - s_lo + 2) // 2, gather, 0)
        gu = jnp.dot(acc[...].astype(BF16), wgu_bf[...], preferred_element_type=F32) + bgu_ref[0]
        dff = gu.shape[1] // 2
        gl = jnp.minimum(gu[:, :dff], SWIGLU_LIMIT)
        lin = jnp.clip(gu[:, dff:], -SWIGLU_LIMIT, SWIGLU_LIMIT)
        act = (gl * jax.nn.sigmoid(SWIGLU_ALPHA * gl) * (lin + 1.0)).astype(BF16)
        y = jnp.dot(act, wd_bf[...], preferred_element_type=F32) + bd_ref[0]
        o_ref[...] = (y * rw_acc[...]).astype(o_ref.dtype)

    @pl.when(b >= nused_ref[0])
    def _():
        o_ref[...] = jnp.zeros_like(o_ref)


def expert_blocks(h2, ti_t, wt_t, block_expert, next_expert, rank_lo, sub_lo, sub_hi, n_used, w_gate_up, b_gate_up,
                  w_down, b_down):
    n, d = h2.shape
    e, _, f2 = w_gate_up.shape
    n_blocks = block_expert.shape[0]
    whole = lambda b, be, bn, rl, lo, hi, nu: (0, 0)
    by_expert = lambda b, be, bn, rl, lo, hi, nu: (be[b], 0, 0)
    once = pl.Buffered(1)
    grid_spec = pltpu.PrefetchScalarGridSpec(
        num_scalar_prefetch=6,
        grid=(n_blocks,),
        in_specs=[pl.BlockSpec((n, d), whole, pipeline_mode=once),
                  pl.BlockSpec((SUBLANES, n), whole, pipeline_mode=once),
                  pl.BlockSpec((SUBLANES, n), whole, pipeline_mode=once),
                  pl.BlockSpec(memory_space=pl.ANY),
                  pl.BlockSpec((1, 1, f2), by_expert),
                  pl.BlockSpec(memory_space=pl.ANY),
                  pl.BlockSpec((1, 1, d), by_expert)],
        out_specs=pl.BlockSpec((EXPERT_BLOCK, d), lambda b, be, bn, rl, lo, hi, nu: (b, 0)),
        scratch_shapes=[pltpu.VMEM((EXPERT_BLOCK, d), F32), pltpu.VMEM((EXPERT_BLOCK, 1), F32),
                        pltpu.VMEM((d, f2), BF16), pltpu.VMEM((f2 // 2, d), BF16),
                        pltpu.VMEM((d, f2), F32), pltpu.VMEM((f2 // 2, d), F32),
                        pltpu.SemaphoreType.DMA((2,))],
    )
    return pl.pallas_call(
        _expert_kernel,
        grid_spec=grid_spec,
        out_shape=jax.ShapeDtypeStruct((n_blocks * EXPERT_BLOCK, d), BF16),
        compiler_params=_params("arbitrary"),
        name="expert_blocks",
    )(block_expert, next_expert, rank_lo, sub_lo, sub_hi, n_used, h2, ti_t, wt_t, w_gate_up, b_gate_up.reshape(e, 1, f2), w_down,
      b_down.reshape(e, 1, d))


def _combine_kernel(coff_ref, cgran_ref, cexp_ref, cbase_ref, yb_hbm, ti_ref, x2_ref, gf_ref, g_ref, o_ref,
                    slab, sem, acc):
    c = pl.program_id(0)
    start = coff_ref[c]
    count = coff_ref[c + 1] - start
    n_groups = (count + COMBINE_SLOTS - 1) // COMBINE_SLOTS

    def item(g, s):
        return start + jnp.minimum(g * COMBINE_SLOTS + s, count - 1)

    def granule_of(g, s):
        return cgran_ref[item(g, s)]

    def copies(g, buf):
        return [pltpu.make_async_copy(
            yb_hbm.at[pl.ds(pl.multiple_of(granule_of(g, s) * COMBINE_GRANULE, COMBINE_GRANULE), COMBINE_GRANULE), :],
            slab.at[buf, pl.ds(s * COMBINE_GRANULE, COMBINE_GRANULE), :], sem.at[buf])
            for s in range(COMBINE_SLOTS)]

    for cp in copies(0, 0):
        cp.start()
    acc[...] = jnp.zeros_like(acc)
    chunk = x2_ref.shape[0]
    row = lax.broadcasted_iota(jnp.int32, (COMBINE_GRANULE, chunk), 0)
    ids = ti_ref[0:TOP_K, :]
    ranks = ti_ref[TOP_K:2 * TOP_K, :]

    def group(g, carry):
        buf = g % 2

        @pl.when(g + 1 < n_groups)
        def _():
            for cp in copies(g + 1, 1 - buf):
                cp.start()

        for cp in copies(g, buf):
            cp.wait()
        pieces = []
        for s in range(COMBINE_SLOTS):
            live = g * COMBINE_SLOTS + s < count
            local = ranks - cbase_ref[item(g, s)]
            inside = (ids == jnp.where(live, cexp_ref[item(g, s)], -1)) & (local >= 0) & (local < COMBINE_GRANULE)
            hit_row = jnp.sum(jnp.where(inside, local + 1, 0), axis=0, keepdims=True) - 1
            pieces.append(jnp.where(row == hit_row, 1.0, 0.0).astype(BF16))
        onehot_t = jnp.concatenate(pieces, axis=0)
        acc[...] += lax.dot_general(onehot_t, slab[buf], (((0,), (0,)), ((), ())),
                                    preferred_element_type=F32)
        return carry

    lax.fori_loop(0, n_groups, group, 0)
    y = acc[...]
    yn = y * lax.rsqrt(jnp.mean(y * y, axis=-1, keepdims=True) + EPS) * g_ref[...]
    o_ref[...] = x2_ref[...] + gf_ref[0] * yn


def combine_residual(yb, ti_t, x2, chunk_off, chunk_items, mod, mod_row_of_chunk, g_post_f):
    n, d = x2.shape
    chunk_row = lambda c, off, gran, exp, base: (c, 0)
    grid_spec = pltpu.PrefetchScalarGridSpec(
        num_scalar_prefetch=4,
        grid=(n // MOE_CHUNK,),
        in_specs=[pl.BlockSpec(memory_space=pl.ANY),
                  pl.BlockSpec((SUBLANES, MOE_CHUNK), lambda c, off, gran, exp, base: (0, c)),
                  pl.BlockSpec((MOE_CHUNK, d), chunk_row),
                  pl.BlockSpec((1, 1, d), lambda c, off, gran, exp, base: (mod_row_of_chunk(c), 0, 5)),
                  pl.BlockSpec((1, d), lambda c, off, gran, exp, base: (0, 0))],
        out_specs=pl.BlockSpec((MOE_CHUNK, d), chunk_row),
        scratch_shapes=[pltpu.VMEM((2, COMBINE_SLOTS * COMBINE_GRANULE, d), BF16),
                        pltpu.SemaphoreType.DMA((2,)),
                        pltpu.VMEM((MOE_CHUNK, d), F32)],
    )
    return pl.pallas_call(
        _combine_kernel,
        grid_spec=grid_spec,
        out_shape=jax.ShapeDtypeStruct((n, d), F32),
        compiler_params=_params("arbitrary"),
        name="combine_residual",
    )(chunk_off, *chunk_items, yb, ti_t, x2, mod, g_post_f.reshape(1, d))


def _grid_pos_embedding(rows, d):
    row = jnp.repeat(jnp.arange(rows, dtype=F32), GRID_W)
    col = jnp.tile(jnp.arange(GRID_W, dtype=F32), rows)
    quarter = d // 4
    omega = POS_BASE ** (-jnp.arange(quarter, dtype=F32) / quarter)

    def emb(p):
        ang = p[:, None] * omega[None, :]
        return jnp.concatenate([jnp.sin(ang), jnp.cos(ang)], axis=-1)

    return jnp.concatenate([emb(row), emb(col)], axis=-1)


def _count_le(sorted_vals, queries):
    return jnp.sum(sorted_vals <= queries[:, None], axis=1).astype(jnp.int32)


def _moe_plan(cnt_after, n_tok, tm):
    assert tm == GATHER_SUB
    sub_after = cnt_after[:, 0, :N_EXPERTS].astype(jnp.int32)
    counts = sub_after[-1]
    padded = (counts + EXPERT_BLOCK - 1) // EXPERT_BLOCK * EXPERT_BLOCK
    padded_end = jnp.cumsum(padded)
    padded_start = padded_end - padded
    n_blocks = n_tok * TOP_K // EXPERT_BLOCK + N_EXPERTS
    blocks = jnp.arange(n_blocks, dtype=jnp.int32)
    block_expert = jnp.minimum(_count_le(padded_end[None, :], blocks * EXPERT_BLOCK), N_EXPERTS - 1)
    n_used = (padded_end[-1:] // EXPERT_BLOCK).astype(jnp.int32)
    of_block = (block_expert[:, None] == jnp.arange(N_EXPERTS, dtype=jnp.int32)[None, :]).astype(jnp.int32)
    r_lo = (blocks - jnp.sum(of_block * padded_start[None, :], axis=1) // EXPERT_BLOCK) * EXPERT_BLOCK
    r_hi = jnp.minimum(jnp.sum(of_block * counts[None, :], axis=1), r_lo + EXPERT_BLOCK) - 1
    through = jnp.sum(of_block[:, :, None] * sub_after.T[None, :, :], axis=1)
    after = jnp.sum(of_block * padded_end[None, :], axis=1) // EXPERT_BLOCK
    next_expert = jnp.sum((after[:, None] == blocks[None, :]).astype(jnp.int32) * block_expert[None, :], axis=1)
    next_expert = jnp.where(after < n_used[0], next_expert, -1).astype(jnp.int32)
    n_sub = sub_after.shape[0]
    sub_lo = jnp.minimum(_count_le(through, r_lo), n_sub - 1)
    sub_hi = jnp.minimum(_count_le(through, r_hi), n_sub - 1)
    nch = n_tok // MOE_CHUNK
    per_chunk = MOE_CHUNK // tm
    cb_after = sub_after[per_chunk - 1::per_chunk]
    cb_before = jnp.concatenate([jnp.zeros((1, N_EXPERTS), jnp.int32), cb_after[:-1]], axis=0)
    b_lo = ((padded_start[None, :] + cb_before) // COMBINE_GRANULE).reshape(-1)
    b_hi = ((padded_start[None, :] + cb_after - 1) // COMBINE_GRANULE).reshape(-1)
    n_it = jnp.where((cb_after > cb_before).reshape(-1), b_hi - b_lo + 1, 0)
    off_end = jnp.cumsum(n_it)
    off = off_end - n_it
    n_items = n_blocks * (EXPERT_BLOCK // COMBINE_GRANULE) + N_EXPERTS * (nch - 1)
    w = jnp.minimum(jnp.arange(n_items, dtype=jnp.int32), off_end[-1] - 1)
    chunk_end = off_end[N_EXPERTS - 1::N_EXPERTS]
    chunk_off = jnp.concatenate([jnp.zeros((1,), jnp.int32), chunk_end]).astype(jnp.int32)
    chunk_of = jnp.minimum(_count_le(chunk_end[None, :], w), nch - 1)
    chunk_expert = jnp.minimum(_count_le(off_end.reshape(nch, N_EXPERTS)[chunk_of], w), N_EXPERTS - 1)
    cell = chunk_of * N_EXPERTS + chunk_expert
    chunk_granules = (b_lo[cell] + w - off[cell]).astype(jnp.int32)
    chunk_base = chunk_granules * COMBINE_GRANULE - padded_start[chunk_expert]
    return block_expert, next_expert, r_lo, sub_lo, sub_hi, n_used, chunk_off, (chunk_granules, chunk_expert, chunk_base)


def kernel(x, c, ctx, c_ctx, w_mod, b_mod, g_pre_mix, g_post_mix, g_pre_ffn, g_post_ffn, w_in, conv_kv,
           conv_q, a_log, dt_bias, gdn_norm_g, w_proj_a, conf_dw, conf_dw_b, conf_ln_g, conf_ln_b,
           w_proj_b, w_out, w_router, b_router, w_gate_up, b_gate_up, w_down, b_down):
    batch, seq, d = x.shape
    ctx_len = ctx.shape[1]
    n_tok = batch * seq
    gw = HEADS * DV
    beta_off = HEADS * DK + gw
    state_cols = beta_off + 4 * HEADS
    q_off = state_cols

    mod_rows = 2 * SUBLANES
    cond = jnp.zeros((mod_rows, d), F32).at[:batch].set(c).at[batch].set(c_ctx)
    mod = modulation(cond, w_mod[0], b_mod[0]).reshape(mod_rows, 1, 6 * d)

    w = w_in[0]
    w_main = projection_weights(w, keep=beta_off, skip=state_cols - beta_off)
    w_ba = jnp.zeros((d, LANES), F32).at[:, :4 * HEADS].set(w[:, beta_off:state_cols]).astype(BF16)

    pos = _grid_pos_embedding(seq // GRID_W, d)
    x_flat = x.reshape(n_tok, d)
    tm = 1024
    tiles_per_seq = seq // tm
    proj, ba = input_projection(x_flat, pos, mod, lambda i: i // tiles_per_seq, g_pre_mix[0],
                                w_main, w_ba, tm=tm, tn=2048)
    proj_ctx, ba_ctx = input_projection(ctx.reshape(batch * ctx_len, d), None, mod, lambda i: batch,
                                        g_pre_mix[0], w_main[:, :beta_off], w_ba, tm=ctx_len, tn=1024)

    og = gated_deltanet(proj, ba, proj_ctx, ba_ctx, conv_kv[0], conv_q[0], a_log[0], dt_bias[0],
                        gdn_norm_g[0], batch=batch, seq=seq, ctx=ctx_len)
    uc = conformer_conv(proj, conf_dw[0], conf_dw_b[0], batch=batch, seq=seq)

    tm2 = 512
    w_r = jnp.zeros((d, LANES), F32).at[:, :N_EXPERTS].set(w_router[0]).astype(BF16)
    b_r = jnp.zeros((1, LANES), F32).at[0, :N_EXPERTS].set(b_router[0])
    x2, h2, top_i, top_w, cnt_after = mixer_output(
        og, uc, proj, x_flat, pos, mod, lambda i: i // (seq // tm2), g_post_mix[0], g_pre_ffn[0],
        conf_ln_g[0], conf_ln_b[0], w_proj_a[0].astype(BF16), w_proj_b[0].astype(BF16),
        w_out[0].astype(BF16), w_r, b_r, tm=tm2, sub=GATHER_SUB)

    block_expert, next_expert, rank_lo, sub_lo, sub_hi, n_used, chunk_off, chunk_items = _moe_plan(
        cnt_after, n_tok, GATHER_SUB)
    yb = expert_blocks(h2, top_i, top_w, block_expert, next_expert, rank_lo, sub_lo, sub_hi, n_used, w_gate_up[0],
                       b_gate_up[0], w_down[0], b_down[0])
    out = combine_residual(yb, top_i, x2, chunk_off, chunk_items, mod, lambda c: c // (seq // MOE_CHUNK),
                           g_post_ffn[0])
    return out.reshape(batch, seq, d)
```

```python
import functools

import jax
import jax.numpy as jnp
from jax import lax
from jax.experimental import pallas as pl
from jax.experimental.pallas import tpu as pltpu

F32 = jnp.float32
BF16 = jnp.bfloat16

D_MODEL = 1024
GRID_W = 64
HEADS = 8
DK = 128
DV = 128
SHORT_CONV = 5
CHUNK = 64
CONF_KERNEL = 31
N_EXPERTS = 32
TOP_K = 4
SWIGLU_LIMIT = 7.0
SWIGLU_ALPHA = 1.702
MOE_BLOCK = 128
EPS = 1e-6
POS_BASE = 10000.0

LANES = 128
SUBLANES = 8
VMEM_LIMIT = 60 * 1024 * 1024

COL_K, COL_V, COL_Q, COL_Z, COL_GLU_A, COL_GLU_G, COL_GATE_A, COL_GATE_B = range(8)


def _params(*sem):
    return pltpu.CompilerParams(dimension_semantics=sem, vmem_limit_bytes=VMEM_LIMIT)


def _mod_kernel(c_ref, w_ref, b_ref, o_ref):
    c = c_ref[...]
    s = c * jax.nn.sigmoid(c)
    o_ref[...] = jnp.dot(s, w_ref[...], preferred_element_type=F32,
                         precision=lax.Precision.HIGHEST) + b_ref[...]


def modulation(cond, w_mod, b_mod):
    r, d = cond.shape
    n = w_mod.shape[1]
    tn = 1024
    return pl.pallas_call(
        _mod_kernel,
        grid=(n // tn,),
        in_specs=[pl.BlockSpec((r, d), lambda j: (0, 0)),
                  pl.BlockSpec((d, tn), lambda j: (0, j)),
                  pl.BlockSpec((1, tn), lambda j: (0, j))],
        out_specs=pl.BlockSpec((r, tn), lambda j: (0, j)),
        out_shape=jax.ShapeDtypeStruct((r, n), F32),
        compiler_params=_params("arbitrary"),
        name="modulation",
    )(cond, w_mod, b_mod.reshape(1, n))


def _proj_weights_kernel(a_ref, b_ref, o_ref, *, first_shifted, shift):
    @pl.when(pl.program_id(0) < first_shifted)
    def _():
        o_ref[...] = a_ref[...].astype(o_ref.dtype)

    @pl.when(pl.program_id(0) >= first_shifted)
    def _():
        o_ref[...] = jnp.concatenate([a_ref[:, shift:], b_ref[:, :shift]], axis=1).astype(o_ref.dtype)


def projection_weights(w, *, keep, skip, tn=1024):
    d, total = w.shape
    n_out = total - skip
    assert keep % tn == 0 and n_out % tn == 0 and 0 < skip < LANES
    per_tile = tn // LANES
    return pl.pallas_call(
        functools.partial(_proj_weights_kernel, first_shifted=keep // tn, shift=skip),
        grid=(n_out // tn,),
        in_specs=[pl.BlockSpec((d, tn), lambda j: (0, j)),
                  pl.BlockSpec((d, LANES), lambda j: (0, (j + 1) * per_tile))],
        out_specs=pl.BlockSpec((d, tn), lambda j: (0, j)),
        out_shape=jax.ShapeDtypeStruct((d, n_out), BF16),
        compiler_params=_params("arbitrary"),
        name="projection_weights",
    )(w, w)


def _inproj_kernel(*refs, has_pos):
    if has_pos:
        x_ref, pos_ref, sh_ref, sc_ref, g_ref, w_ref, wba_ref, o_ref, ba_ref, h_scr = refs
    else:
        x_ref, sh_ref, sc_ref, g_ref, w_ref, wba_ref, o_ref, ba_ref, h_scr = refs

    @pl.when(pl.program_id(1) == 0)
    def _():
        x = x_ref[...]
        if has_pos:
            x = x + pos_ref[...]
        y = x * lax.rsqrt(jnp.mean(x * x, axis=-1, keepdims=True) + EPS) * g_ref[...]
        h = (y * (1.0 + sc_ref[0]) + sh_ref[0]).astype(BF16)
        h_scr[...] = h
        ba_ref[...] = jnp.dot(h, wba_ref[...], preferred_element_type=F32)

    o_ref[...] = jnp.dot(h_scr[...], w_ref[...], preferred_element_type=F32).astype(o_ref.dtype)


def input_projection(x, pos, mod, mod_row_of_tile, g_pre, w_main, w_ba, *, tm, tn):
    n, d = x.shape
    w = w_main.shape[1]
    has_pos = pos is not None
    in_specs = [pl.BlockSpec((tm, d), lambda i, j: (i, 0))]
    args = [x]
    if has_pos:
        pos_tiles = pos.shape[0] // tm
        in_specs.append(pl.BlockSpec((tm, d), lambda i, j: (i % pos_tiles, 0)))
        args.append(pos)
    in_specs += [
        pl.BlockSpec((1, 1, d), lambda i, j: (mod_row_of_tile(i), 0, 0)),
        pl.BlockSpec((1, 1, d), lambda i, j: (mod_row_of_tile(i), 0, 1)),
        pl.BlockSpec((1, d), lambda i, j: (0, 0)),
        pl.BlockSpec((d, tn), lambda i, j: (0, j)),
        pl.BlockSpec((d, LANES), lambda i, j: (0, 0)),
    ]
    args += [mod, mod, g_pre.reshape(1, d), w_main, w_ba]
    return pl.pallas_call(
        functools.partial(_inproj_kernel, has_pos=has_pos),
        grid=(n // tm, w // tn),
        in_specs=in_specs,
        out_specs=[pl.BlockSpec((tm, tn), lambda i, j: (i, j)),
                   pl.BlockSpec((tm, LANES), lambda i, j: (i, 0))],
        out_shape=[jax.ShapeDtypeStruct((n, w), BF16), jax.ShapeDtypeStruct((n, LANES), F32)],
        scratch_shapes=[pltpu.VMEM((tm, d), BF16)],
        compiler_params=_params("arbitrary", "arbitrary"),
        name="input_projection_pos" if has_pos else "input_projection_ctx",
    )(*args)


def _silu(x):
    return x * jax.nn.sigmoid(x)


def _softplus(x):
    return jnp.maximum(x, 0.0) + jnp.log(1.0 + jnp.exp(-jnp.abs(x)))


def _short_conv(src, taps, pad_scr, rows):
    half = SHORT_CONV // 2
    pad_scr[0:SUBLANES, :] = jnp.zeros((SUBLANES, LANES), F32)
    pad_scr[SUBLANES:SUBLANES + rows, :] = src.astype(F32)
    pad_scr[SUBLANES + rows:2 * SUBLANES + rows, :] = jnp.zeros((SUBLANES, LANES), F32)
    acc = None
    for j in range(SHORT_CONV):
        off = SUBLANES + j - half
        term = pad_scr[off:off + rows, :] * taps[j:j + 1, :]
        acc = term if acc is None else acc + term
    return _silu(acc)


def _l2n(x):
    return x * lax.rsqrt(jnp.sum(x * x, axis=-1, keepdims=True) + EPS)


def _gdn_kernel(k_ref, v_ref, q_ref, z_ref, ba_ref, kc_ref, vc_ref, bac_ref,
                wk_ref, wv_ref, wq_ref, ab_ref, gn_ref, o_ref,
                kf, vf, qf, gates, pad_scr, lhs, val, att, ktt, dec, osc, *, seq, ctx, unroll, group):
    total = ctx + seq
    nc = total // CHUNK
    ncc = ctx // CHUNK
    two = 2 * CHUNK
    lg_chunk = CHUNK.bit_length() - 1

    ri = lax.broadcasted_iota(jnp.int32, (two, two), 0)
    ci = lax.broadcasted_iota(jnp.int32, (two, two), 1)
    same_dir = (ri >> lg_chunk) == (ci >> lg_chunk)
    incl = same_dir & (((ri < CHUNK) & (ri >= ci)) | ((ri >= CHUNK) & (ri <= ci)))
    strict = incl & (ri != ci)
    eye = jnp.where(ri == ci, 1.0, 0.0)
    tri = jnp.where(incl, 1.0, 0.0).astype(BF16)
    level_masks = []
    for lg in range(lg_chunk):
        same_parent = (ri >> (lg + 1)) == (ci >> (lg + 1))
        level_masks.append(jnp.where(same_parent & ((ri >> lg) != (ci >> lg)), 1.0, 0.0))
    top_rows = lax.broadcasted_iota(jnp.int32, (two, LANES), 0) < CHUNK

    def bwd_chunk(t):
        return jnp.where(t < ncc, ncc - 1 - t, nc + ncc - 1 - t)

    def stacked(ref, rf, rb):
        return jnp.concatenate([ref[pl.ds(rf, CHUNK), :], ref[pl.ds(rb, CHUNK), :]], axis=0)

    def chunk_load(t):
        rf = pl.multiple_of(t * CHUNK, CHUNK)
        rb = pl.multiple_of(bwd_chunk(t) * CHUNK, CHUNK)
        beta = jnp.concatenate([gates[0, pl.ds(rf, CHUNK), :], gates[1, pl.ds(rb, CHUNK), :]], axis=0)
        g = jnp.concatenate([gates[2, pl.ds(rf, CHUNK), :], gates[3, pl.ds(rb, CHUNK), :]], axis=0)
        return stacked(kf, rf, rb), stacked(vf, rf, rb), stacked(qf, rf, rb), beta, g

    def chunk_triangle(k, q, beta, g):
        kb = k.astype(BF16)
        kq = jnp.concatenate([kb, q.astype(BF16)], axis=0)
        kk_qk = lax.dot_general(kq, kb, (((1,), (1,)), ((), ())), preferred_element_type=F32)
        g_hi = g.astype(BF16)
        r1 = g - g_hi.astype(F32)
        g_mid = r1.astype(BF16)
        g_lo = (r1 - g_mid.astype(F32)).astype(BF16)
        gc3 = jnp.dot(tri, jnp.concatenate([g_hi, g_mid, g_lo], axis=1), preferred_element_type=F32)
        gc = gc3[:, :LANES] + gc3[:, LANES:2 * LANES] + gc3[:, 2 * LANES:]
        gc_row = jnp.transpose(gc)
        decay = jnp.where(incl, jnp.exp(gc - gc_row), 0.0)
        a = jnp.where(strict, beta * kk_qk[:two] * decay, 0.0)
        return a, (kk_qk[two:] * decay).astype(BF16), gc

    def chunk_finish(hh, t, k, v, q, beta, gc, attn, tmat):
        eg = jnp.exp(gc)
        rhs = jnp.concatenate([v * beta, k * beta * eg], axis=1).astype(BF16)
        sol = jnp.dot(tmat.astype(BF16), rhs, preferred_element_type=F32)
        g_end = jnp.where(top_rows, jnp.broadcast_to(gc[CHUNK - 1:CHUNK, :], (two, LANES)),
                          jnp.broadcast_to(gc[CHUNK:CHUNK + 1, :], (two, LANES)))
        k_tail = k * jnp.exp(g_end - gc)
        qg = (q * eg).astype(BF16)
        kcum = sol[:, DV:].astype(BF16)
        r0 = pl.multiple_of(t * two, two)
        lhs[hh, 0, pl.ds(r0, two), :] = jnp.concatenate([kcum[:CHUNK], qg[:CHUNK]], axis=0)
        lhs[hh, 1, pl.ds(r0, two), :] = jnp.concatenate([kcum[CHUNK:], qg[CHUNK:]], axis=0)
        val[hh, pl.ds(r0, two), :] = sol[:, :DV].astype(BF16)
        att[hh, pl.ds(r0, two), :] = attn
        ktt[hh, pl.ds(r0, two), :] = jnp.transpose(k_tail).astype(BF16)
        d0 = pl.multiple_of(t * SUBLANES, SUBLANES)
        e_end = jnp.exp(g_end)
        dec[hh, 0, pl.ds(d0, SUBLANES), :] = e_end[:SUBLANES]
        dec[hh, 1, pl.ds(d0, SUBLANES), :] = e_end[CHUNK:CHUNK + SUBLANES]

    for hh in range(group):
        head = pl.program_id(1) * group + hh
        cols = slice(hh * LANES, (hh + 1) * LANES)

        kf[0:ctx, :] = _l2n(_short_conv(kc_ref[:, cols], wk_ref[:, cols], pad_scr, ctx))
        vf[0:ctx, :] = _short_conv(vc_ref[:, cols], wv_ref[:, cols], pad_scr, ctx)
        qf[0:ctx, :] = jnp.zeros((ctx, LANES), F32)
        kf[ctx:total, :] = _l2n(_short_conv(k_ref[:, cols], wk_ref[:, cols], pad_scr, seq))
        vf[ctx:total, :] = _short_conv(v_ref[:, cols], wv_ref[:, cols], pad_scr, seq)
        qf[ctx:total, :] = _l2n(_short_conv(q_ref[:, cols], wq_ref[:, cols], pad_scr, seq)) * (DK ** -0.5)

        def gate_cols(src_ref, lo, rows):
            x = src_ref[...]
            lane = lax.broadcasted_iota(jnp.int32, (rows, LANES), 1)
            beta = jax.nn.sigmoid(x)
            g = -jnp.exp(ab_ref[0:1, :]) * _softplus(x + ab_ref[1:2, :])
            for slot, (arr, base) in enumerate(((beta, 0), (beta, HEADS), (g, 2 * HEADS), (g, 3 * HEADS))):
                col = jnp.sum(jnp.where(lane == base + head, arr, 0.0), axis=-1, keepdims=True)
                gates[slot, lo:lo + rows, :] = jnp.broadcast_to(col, (rows, LANES))

        gate_cols(bac_ref, 0, ctx)
        gate_cols(ba_ref, ctx, seq)

        def chunk_body(i, carry, hh=hh):
            steps = [i * unroll + u for u in range(unroll)]
            loaded = [chunk_load(t) for t in steps]
            tris = [chunk_triangle(k, q, beta, g) for (k, v, q, beta, g) in loaded]
            tmats = [eye - a * level_masks[0] for (a, _, _) in tris]
            for m in level_masks[1:]:
                nxt = []
                for (a, _, _), tmat in zip(tris, tmats):
                    cs = (a * m).astype(BF16)
                    tb = tmat.astype(BF16)
                    tc = jnp.dot(tb, cs, preferred_element_type=F32)
                    nxt.append(tmat - jnp.dot(tc.astype(BF16), tb, preferred_element_type=F32))
                tmats = nxt
            for t, (k, v, q, beta, g), (a, attn, gc), tmat in zip(steps, loaded, tris, tmats):
                chunk_finish(hh, t, k, v, q, beta, gc, attn, tmat)
            return carry

        lax.fori_loop(0, nc // unroll, chunk_body, 0)

    def scan_body(t, carry):
        rf = pl.multiple_of(t * CHUNK, CHUNK)
        rb = pl.multiple_of(bwd_chunk(t) * CHUNK, CHUNK)
        r0 = pl.multiple_of(t * two, two)
        d0 = pl.multiple_of(t * SUBLANES, SUBLANES)
        rs = [(jnp.dot(lhs[hh, 0, pl.ds(r0, two), :], carry[2 * hh].astype(BF16), preferred_element_type=F32),
               jnp.dot(lhs[hh, 1, pl.ds(r0, two), :], carry[2 * hh + 1].astype(BF16), preferred_element_type=F32))
              for hh in range(group)]
        v_news = [val[hh, pl.ds(r0, two), :].astype(F32) - jnp.concatenate([r_f[:CHUNK], r_b[:CHUNK]], axis=0)
                  for hh, (r_f, r_b) in enumerate(rs)]
        outs, new_states = [], []
        for hh, ((r_f, r_b), v_new) in enumerate(zip(rs, v_news)):
            zeros = jnp.zeros_like(v_new)
            v_bd = jnp.concatenate([jnp.where(top_rows, v_new, zeros), jnp.where(top_rows, zeros, v_new)],
                                   axis=1).astype(BF16)
            upd = jnp.dot(ktt[hh, pl.ds(r0, two), :], v_bd, preferred_element_type=F32)
            new_states.append(carry[2 * hh] * dec[hh, 0, pl.ds(d0, SUBLANES), :][0:1, :] + upd[:, :DV])
            new_states.append(carry[2 * hh + 1] * dec[hh, 1, pl.ds(d0, SUBLANES), :][0:1, :] + upd[:, DV:])
            outs.append(jnp.concatenate([r_f[CHUNK:], r_b[CHUNK:]], axis=0)
                        + jnp.dot(att[hh, pl.ds(r0, two), :], v_new.astype(BF16), preferred_element_type=F32))
        for hh, o in enumerate(outs):
            osc[hh, pl.ds(rf, CHUNK), :] += o[:CHUNK]
            osc[hh, pl.ds(rb, CHUNK), :] += o[CHUNK:]
        return tuple(new_states)

    osc[...] = jnp.zeros_like(osc)
    zero = jnp.zeros((DK, DV), F32)
    lax.fori_loop(0, nc, scan_body, (zero,) * (2 * group))

    for hh in range(group):
        cols = slice(hh * LANES, (hh + 1) * LANES)
        o = osc[hh, ctx:total, :]
        o = o * lax.rsqrt(jnp.mean(o * o, axis=-1, keepdims=True) + EPS) * gn_ref[...]
        o_ref[:, cols] = (o * _silu(z_ref[:, cols].astype(F32))).astype(o_ref.dtype)


def gated_deltanet(proj, ba, proj_ctx, ba_ctx, conv_kv, conv_q, a_log, dt_bias, gn, *, batch, seq, ctx,
                   unroll=18, group=4):
    total = seq + ctx
    nc = total // CHUNK
    gl = group * LANES
    ng = HEADS // group
    taps = jnp.zeros((SUBLANES, conv_kv.shape[1]), F32).at[:SHORT_CONV].set(conv_kv)
    taps_q = jnp.zeros((SUBLANES, conv_q.shape[1]), F32).at[:SHORT_CONV].set(conv_q)
    ab = jnp.zeros((SUBLANES, LANES), F32)
    ab = ab.at[0, 2 * HEADS:4 * HEADS].set(a_log.reshape(-1)).at[1, 2 * HEADS:4 * HEADS].set(dt_bias.reshape(-1))
    col = lambda base: (lambda b, h: (b, base * ng + h))
    once = pl.Buffered(1)
    return pl.pallas_call(
        functools.partial(_gdn_kernel, seq=seq, ctx=ctx, unroll=unroll, group=group),
        grid=(batch, ng),
        in_specs=[
            pl.BlockSpec((seq, gl), col(COL_K), pipeline_mode=once),
            pl.BlockSpec((seq, gl), col(COL_V), pipeline_mode=once),
            pl.BlockSpec((seq, gl), col(COL_Q), pipeline_mode=once),
            pl.BlockSpec((seq, gl), col(COL_Z), pipeline_mode=once),
            pl.BlockSpec((seq, LANES), lambda b, h: (b, 0), pipeline_mode=once),
            pl.BlockSpec((ctx, gl), col(COL_K)),
            pl.BlockSpec((ctx, gl), col(COL_V)),
            pl.BlockSpec((ctx, LANES), lambda b, h: (b, 0)),
            pl.BlockSpec((SUBLANES, gl), lambda b, h: (0, h)),
            pl.BlockSpec((SUBLANES, gl), lambda b, h: (0, ng + h)),
            pl.BlockSpec((SUBLANES, gl), lambda b, h: (0, h)),
            pl.BlockSpec((SUBLANES, LANES), lambda b, h: (0, 0)),
            pl.BlockSpec((1, LANES), lambda b, h: (0, 0)),
        ],
        out_specs=pl.BlockSpec((seq, gl), lambda b, h: (b, h)),
        out_shape=jax.ShapeDtypeStruct((batch * seq, HEADS * DV), BF16),
        scratch_shapes=[
            pltpu.VMEM((total, LANES), F32),
            pltpu.VMEM((total, LANES), F32),
            pltpu.VMEM((total, LANES), F32),
            pltpu.VMEM((4, total, LANES), F32),
            pltpu.VMEM((seq + 2 * SUBLANES, LANES), F32),
            pltpu.VMEM((group, 2, nc * 2 * CHUNK, LANES), BF16),
            pltpu.VMEM((group, nc * 2 * CHUNK, LANES), BF16),
            pltpu.VMEM((group, nc * 2 * CHUNK, LANES), BF16),
            pltpu.VMEM((group, nc * 2 * CHUNK, LANES), BF16),
            pltpu.VMEM((group, 2, nc * SUBLANES, LANES), F32),
            pltpu.VMEM((group, total, LANES), F32),
        ],
        compiler_params=_params("arbitrary", "arbitrary"),
        name="gated_deltanet",
    )(proj, proj, proj, proj, ba, proj_ctx, proj_ctx, ba_ctx, taps, taps, taps_q, ab, gn.reshape(1, DV))


CONF_ROWS = 128
CONF_HALO = 16


def _conf_kernel(a_ref, g_ref, w_ref, b_ref, o_ref, pad_scr, *, seq):
    cb = a_ref.shape[1]
    u = a_ref[...].astype(F32) * jax.nn.sigmoid(g_ref[...].astype(F32))
    pad_scr[0:CONF_HALO, :] = jnp.zeros((CONF_HALO, cb), F32)
    pad_scr[CONF_HALO:CONF_HALO + seq, :] = u
    pad_scr[CONF_HALO + seq:2 * CONF_HALO + seq, :] = jnp.zeros((CONF_HALO, cb), F32)
    win = CONF_ROWS + 2 * CONF_HALO
    first = CONF_HALO - CONF_KERNEL // 2

    def body(i, carry):
        r0 = pl.multiple_of(i * CONF_ROWS, CONF_ROWS)
        w = pad_scr[pl.ds(r0, win), :]
        acc = jnp.zeros((CONF_ROWS, cb), F32) + b_ref[...]
        for sub in range(SUBLANES):
            shifted = w if sub == 0 else pltpu.roll(w, win - sub, axis=0)
            for j in range(CONF_KERNEL):
                off = first + j
                if off % SUBLANES == sub:
                    base = off - sub
                    acc = acc + shifted[base:base + CONF_ROWS, :] * w_ref[j:j + 1, :]
        o_ref[pl.ds(r0, CONF_ROWS), :] = acc.astype(o_ref.dtype)
        return carry

    lax.fori_loop(0, seq // CONF_ROWS, body, 0)


def conformer_conv(proj, conf_dw, conf_dw_b, *, batch, seq, cb=256):
    c = conf_dw.shape[1]
    nb = c // cb
    taps = jnp.zeros((32, c), F32).at[:CONF_KERNEL].set(conf_dw)
    return pl.pallas_call(
        functools.partial(_conf_kernel, seq=seq),
        grid=(batch, nb),
        in_specs=[pl.BlockSpec((seq, cb), lambda b, j: (b, COL_GLU_A * nb + j)),
                  pl.BlockSpec((seq, cb), lambda b, j: (b, COL_GLU_G * nb + j)),
                  pl.BlockSpec((32, cb), lambda b, j: (0, j)),
                  pl.BlockSpec((1, cb), lambda b, j: (0, j))],
        out_specs=pl.BlockSpec((seq, cb), lambda b, j: (b, j)),
        out_shape=jax.ShapeDtypeStruct((batch * seq, c), BF16),
        scratch_shapes=[pltpu.VMEM((seq + 2 * CONF_HALO, cb), F32)],
        compiler_params=_params("arbitrary", "arbitrary"),
        name="conformer_conv",
    )(proj, proj, taps, conf_dw_b.reshape(1, c))


def _mixer_out_kernel(og_ref, uc_ref, ga_ref, gb_ref, x_ref, pos_ref, gm_ref, shf_ref, scf_ref,
                      gpost_ref, gpre_ref, lng_ref, lnb_ref, wa_ref, wb_ref, wo_ref, wr_ref, br_ref,
                      x2_ref, h2_ref, ti_ref, tw_ref, cnt_ref, cnt_scr, *, sub):
    parts = [slice(r, r + sub) for r in range(0, og_ref.shape[0], sub)]

    def rms(v):
        return v * lax.rsqrt(jnp.mean(v * v, axis=-1, keepdims=True) + EPS)

    def layer_norm_silu(uc):
        mu = jnp.mean(uc, axis=-1, keepdims=True)
        var = jnp.mean(jnp.square(uc - mu), axis=-1, keepdims=True)
        return _silu((uc - mu) * lax.rsqrt(var + EPS) * lng_ref[...] + lnb_ref[...]).astype(BF16)

    us = [layer_norm_silu(uc_ref[p, :].astype(F32)) for p in parts]
    yas = [jnp.dot(og_ref[p, :], wa_ref[...], preferred_element_type=F32) for p in parts]
    ybs = [jnp.dot(u, wb_ref[...], preferred_element_type=F32) for u in us]
    merged = [(jax.nn.sigmoid(ga_ref[p, :].astype(F32)) * ya
               + jax.nn.sigmoid(gb_ref[p, :].astype(F32)) * yb).astype(BF16) for p, ya, yb in zip(parts, yas, ybs)]
    ys = [jnp.dot(m, wo_ref[...], preferred_element_type=F32) for m in merged]
    x2s = [x_ref[p, :] + pos_ref[p, :] + gm_ref[0] * (rms(y) * gpost_ref[...]) for p, y in zip(parts, ys)]
    h2s = [(rms(x2) * gpre_ref[...] * (1.0 + scf_ref[0]) + shf_ref[0]).astype(BF16) for x2 in x2s]
    for p, x2, h2 in zip(parts, x2s, h2s):
        x2_ref[p, :] = x2
        h2_ref[p, :] = h2
    all_logits = [jnp.dot(h2, wr_ref[...], preferred_element_type=F32) + br_ref[...] for h2 in h2s]
    lane = lax.broadcasted_iota(jnp.int32, (sub, LANES), 1).astype(F32)
    ri = lax.broadcasted_iota(jnp.int32, (sub, sub), 0)
    ci = lax.broadcasted_iota(jnp.int32, (sub, sub), 1)
    before = jnp.where(ri > ci, 1.0, 0.0).astype(BF16)

    @pl.when(pl.program_id(0) == 0)
    def _():
        cnt_scr[...] = jnp.zeros_like(cnt_scr)

    routed = []
    for logits in all_logits:
        live = jnp.where(lane < N_EXPERTS, logits, -jnp.inf)
        top_v, top_i = [], []
        for _ in range(TOP_K):
            m = jnp.max(live, axis=-1, keepdims=True)
            idx = jnp.min(jnp.where(live == m, lane, float(LANES)), axis=-1, keepdims=True)
            top_v.append(m)
            top_i.append(idx)
            live = jnp.where(lane == idx, -jnp.inf, live)
        ex = [jnp.exp(v - top_v[0]) for v in top_v]
        denom = ex[0] + ex[1] + ex[2] + ex[3]
        picked = jnp.zeros((sub, LANES), F32)
        for k in range(TOP_K):
            picked = jnp.where(lane == top_i[k], 1.0, picked)
        routed.append((top_i, [e / denom for e in ex], picked,
                       jnp.dot(before, picked.astype(BF16), preferred_element_type=F32)))

    count = cnt_scr[...]
    for j, (p, (top_i, top_w, picked, within)) in enumerate(zip(parts, routed)):
        prefix = within + count[0:1, :]
        ti = jnp.zeros((sub, LANES), F32)
        tw = jnp.zeros((sub, LANES), F32)
        for k in range(TOP_K):
            rank = jnp.sum(jnp.where(lane == top_i[k], prefix, 0.0), axis=-1, keepdims=True)
            ti = jnp.where(lane == k, top_i[k], ti)
            ti = jnp.where(lane == TOP_K + k, rank, ti)
            tw = jnp.where(lane == k, top_w[k], tw)
        ti_ref[:, p] = jnp.transpose(ti)[:SUBLANES, :].astype(jnp.int32)
        tw_ref[:, p] = jnp.transpose(tw)[:SUBLANES, :]
        count = count + jnp.sum(picked, axis=0, keepdims=True)
        cnt_ref[j] = count
    cnt_scr[...] = count


def mixer_output(og, uc, proj, x, pos, mod, mod_row_of_tile, g_post, g_pre_f, ln_g, ln_b,
                 w_a, w_b, w_o, w_r, b_r, *, tm, sub):
    n, d = x.shape
    pos_tiles = pos.shape[0] // tm
    row = lambda i: (i, 0)
    fixed = lambda i: (0, 0)
    modspec = lambda k: pl.BlockSpec((1, 1, d), lambda i: (mod_row_of_tile(i), 0, k))
    vec = pl.BlockSpec((1, d), fixed)
    mat = pl.BlockSpec((d, d), fixed)
    return pl.pallas_call(
        functools.partial(_mixer_out_kernel, sub=sub),
        grid=(n // tm,),
        in_specs=[pl.BlockSpec((tm, d), row), pl.BlockSpec((tm, d), row),
                  pl.BlockSpec((tm, d), lambda i: (i, COL_GATE_A)),
                  pl.BlockSpec((tm, d), lambda i: (i, COL_GATE_B)),
                  pl.BlockSpec((tm, d), row),
                  pl.BlockSpec((tm, d), lambda i: (i % pos_tiles, 0)),
                  modspec(2), modspec(3), modspec(4),
                  vec, vec, vec, vec, mat, mat, mat,
                  pl.BlockSpec((d, LANES), fixed), pl.BlockSpec((1, LANES), fixed)],
        out_specs=[pl.BlockSpec((tm, d), row), pl.BlockSpec((tm, d), row),
                   pl.BlockSpec((SUBLANES, tm), lambda i: (0, i)), pl.BlockSpec((SUBLANES, tm), lambda i: (0, i)),
                   pl.BlockSpec((tm // sub, SUBLANES, LANES), lambda i: (i, 0, 0))],
        out_shape=[jax.ShapeDtypeStruct((n, d), F32), jax.ShapeDtypeStruct((n, d), BF16),
                   jax.ShapeDtypeStruct((SUBLANES, n), jnp.int32), jax.ShapeDtypeStruct((SUBLANES, n), F32),
                   jax.ShapeDtypeStruct((n // sub, SUBLANES, LANES), F32)],
        scratch_shapes=[pltpu.VMEM((SUBLANES, LANES), F32)],
        compiler_params=_params("arbitrary"),
        name="mixer_output",
    )(og, uc, proj, proj, x, pos, mod, mod, mod,
      g_post.reshape(1, d), g_pre_f.reshape(1, d), ln_g.reshape(1, d), ln_b.reshape(1, d),
      w_a, w_b, w_o, w_r, b_r)


EXPERT_BLOCK = 256
MOE_CHUNK = 512
GATHER_SUB = 256
COMBINE_GRANULE = 32
COMBINE_SLOTS = 32


def _expert_kernel(bexp_ref, bnext_ref, rlo_ref, slo_ref, shi_ref, nused_ref,
                   h2_ref, ti_ref, wt_ref, wgu_hbm, bgu_ref, wd_hbm, bd_ref, o_ref,
                   acc, rw_acc, wgu_bf, wd_bf, wgu_f32, wd_f32, wsem):
    b = pl.program_id(0)
    expert = bexp_ref[b]

    def fetch(e):
        return (pltpu.make_async_copy(wgu_hbm.at[e], wgu_f32, wsem.at[0]),
                pltpu.make_async_copy(wd_hbm.at[e], wd_f32, wsem.at[1]))

    @pl.when(((b == 0) | (expert != bexp_ref[jnp.maximum(b - 1, 0)])) & (b < nused_ref[0]))
    def _():
        @pl.when(b == 0)
        def _():
            for cp in fetch(expert):
                cp.start()

        for cp in fetch(expert):
            cp.wait()
        wgu_bf[...] = wgu_f32[...].astype(BF16)
        wd_bf[...] = wd_f32[...].astype(BF16)

        @pl.when(bnext_ref[b] >= 0)
        def _():
            for cp in fetch(bnext_ref[b]):
                cp.start()

    @pl.when(b < nused_ref[0])
    def _():
        acc[...] = jnp.zeros_like(acc)
        rw_acc[...] = jnp.zeros_like(rw_acc)
        row = lax.broadcasted_iota(jnp.int32, (EXPERT_BLOCK, GATHER_SUB), 0)
        s_lo, s_hi = slo_ref[b], shi_ref[b]
        last_sub = h2_ref.shape[0] // GATHER_SUB - 1

        def select(s, live):
            t0 = pl.multiple_of(s * GATHER_SUB, GATHER_SUB)
            ids = ti_ref[0:TOP_K, pl.ds(t0, GATHER_SUB)]
            local = ti_ref[TOP_K:2 * TOP_K, pl.ds(t0, GATHER_SUB)] - rlo_ref[b]
            if live is not True:
                local = jnp.where(live, local, -1)
            inside = (ids == expert) & (local >= 0) & (local < EXPERT_BLOCK)
            hit_row = jnp.sum(jnp.where(inside, local + 1, 0), axis=0, keepdims=True) - 1
            hit_w = jnp.sum(jnp.where(inside, wt_ref[0:TOP_K, pl.ds(t0, GATHER_SUB)], 0.0), axis=0, keepdims=True)
            hit = row == hit_row
            return (jnp.where(hit, 1.0, 0.0).astype(BF16), jnp.where(hit, hit_w, 0.0),
                    h2_ref[pl.ds(t0, GATHER_SUB), :])

        def gather(i, carry):
            s0 = s_lo + 2 * i
            p0, w0, x0 = select(s0, True)
            p1, w1, x1 = select(jnp.minimum(s0 + 1, last_sub), s0 + 1 <= s_hi)
            acc[...] += jnp.dot(jnp.concatenate([p0, p1], axis=1), jnp.concatenate([x0, x1], axis=0),
                                preferred_element_type=F32)
            rw_acc[...] += jnp.sum(w0 + w1, axis=1, keepdims=True)
            return carry

        lax.fori_loop(0, (s_hi - s_lo + 2) // 2, gather, 0)
        gu = jnp.dot(acc[...].astype(BF16), wgu_bf[...], preferred_element_type=F32) + bgu_ref[0]
        dff = gu.shape[1] // 2
        gl = jnp.minimum(gu[:, :dff], SWIGLU_LIMIT)
        lin = jnp.clip(gu[:, dff:], -SWIGLU_LIMIT, SWIGLU_LIMIT)
        act = (gl * jax.nn.sigmoid(SWIGLU_ALPHA * gl) * (lin + 1.0)).astype(BF16)
        y = jnp.dot(act, wd_bf[...], preferred_element_type=F32) + bd_ref[0]
        o_ref[...] = (y * rw_acc[...]).astype(o_ref.dtype)

    @pl.when(b >= nused_ref[0])
    def _():
        o_ref[...] = jnp.zeros_like(o_ref)


def expert_blocks(h2, ti_t, wt_t, block_expert, next_expert, rank_lo, sub_lo, sub_hi, n_used, w_gate_up, b_gate_up,
                  w_down, b_down):
    n, d = h2.shape
    e, _, f2 = w_gate_up.shape
    n_blocks = block_expert.shape[0]
    whole = lambda b, be, bn, rl, lo, hi, nu: (0, 0)
    by_expert = lambda b, be, bn, rl, lo, hi, nu: (be[b], 0, 0)
    once = pl.Buffered(1)
    grid_spec = pltpu.PrefetchScalarGridSpec(
        num_scalar_prefetch=6,
        grid=(n_blocks,),
        in_specs=[pl.BlockSpec((n, d), whole, pipeline_mode=once),
                  pl.BlockSpec((SUBLANES, n), whole, pipeline_mode=once),
                  pl.BlockSpec((SUBLANES, n), whole, pipeline_mode=once),
                  pl.BlockSpec(memory_space=pl.ANY),
                  pl.BlockSpec((1, 1, f2), by_expert),
                  pl.BlockSpec(memory_space=pl.ANY),
                  pl.BlockSpec((1, 1, d), by_expert)],
        out_specs=pl.BlockSpec((EXPERT_BLOCK, d), lambda b, be, bn, rl, lo, hi, nu: (b, 0)),
        scratch_shapes=[pltpu.VMEM((EXPERT_BLOCK, d), F32), pltpu.VMEM((EXPERT_BLOCK, 1), F32),
                        pltpu.VMEM((d, f2), BF16), pltpu.VMEM((f2 // 2, d), BF16),
                        pltpu.VMEM((d, f2), F32), pltpu.VMEM((f2 // 2, d), F32),
                        pltpu.SemaphoreType.DMA((2,))],
    )
    return pl.pallas_call(
        _expert_kernel,
        grid_spec=grid_spec,
        out_shape=jax.ShapeDtypeStruct((n_blocks * EXPERT_BLOCK, d), BF16),
        compiler_params=_params("arbitrary"),
        name="expert_blocks",
    )(block_expert, next_expert, rank_lo, sub_lo, sub_hi, n_used, h2, ti_t, wt_t, w_gate_up, b_gate_up.reshape(e, 1, f2), w_down,
      b_down.reshape(e, 1, d))


def _combine_kernel(coff_ref, cgran_ref, cexp_ref, cbase_ref, yb_hbm, ti_ref, x2_ref, gf_ref, g_ref, o_ref,
                    slab, sem, acc):
    c = pl.program_id(0)
    start = coff_ref[c]
    count = coff_ref[c + 1] - start
    n_groups = (count + COMBINE_SLOTS - 1) // COMBINE_SLOTS

    def item(g, s):
        return start + jnp.minimum(g * COMBINE_SLOTS + s, count - 1)

    def granule_of(g, s):
        return cgran_ref[item(g, s)]

    def copies(g, buf):
        return [pltpu.make_async_copy(
            yb_hbm.at[pl.ds(pl.multiple_of(granule_of(g, s) * COMBINE_GRANULE, COMBINE_GRANULE), COMBINE_GRANULE), :],
            slab.at[buf, pl.ds(s * COMBINE_GRANULE, COMBINE_GRANULE), :], sem.at[buf])
            for s in range(COMBINE_SLOTS)]

    for cp in copies(0, 0):
        cp.start()
    acc[...] = jnp.zeros_like(acc)
    chunk = x2_ref.shape[0]
    row = lax.broadcasted_iota(jnp.int32, (COMBINE_GRANULE, chunk), 0)
    ids = ti_ref[0:TOP_K, :]
    ranks = ti_ref[TOP_K:2 * TOP_K, :]

    def group(g, carry):
        buf = g % 2

        @pl.when(g + 1 < n_groups)
        def _():
            for cp in copies(g + 1, 1 - buf):
                cp.start()

        for cp in copies(g, buf):
            cp.wait()
        pieces = []
        for s in range(COMBINE_SLOTS):
            live = g * COMBINE_SLOTS + s < count
            local = ranks - cbase_ref[item(g, s)]
            inside = (ids == jnp.where(live, cexp_ref[item(g, s)], -1)) & (local >= 0) & (local < COMBINE_GRANULE)
            hit_row = jnp.sum(jnp.where(inside, local + 1, 0), axis=0, keepdims=True) - 1
            pieces.append(jnp.where(row == hit_row, 1.0, 0.0).astype(BF16))
        onehot_t = jnp.concatenate(pieces, axis=0)
        acc[...] += lax.dot_general(onehot_t, slab[buf], (((0,), (0,)), ((), ())),
                                    preferred_element_type=F32)
        return carry

    lax.fori_loop(0, n_groups, group, 0)
    y = acc[...]
    yn = y * lax.rsqrt(jnp.mean(y * y, axis=-1, keepdims=True) + EPS) * g_ref[...]
    o_ref[...] = x2_ref[...] + gf_ref[0] * yn


def combine_residual(yb, ti_t, x2, chunk_off, chunk_items, mod, mod_row_of_chunk, g_post_f):
    n, d = x2.shape
    chunk_row = lambda c, off, gran, exp, base: (c, 0)
    grid_spec = pltpu.PrefetchScalarGridSpec(
        num_scalar_prefetch=4,
        grid=(n // MOE_CHUNK,),
        in_specs=[pl.BlockSpec(memory_space=pl.ANY),
                  pl.BlockSpec((SUBLANES, MOE_CHUNK), lambda c, off, gran, exp, base: (0, c)),
                  pl.BlockSpec((MOE_CHUNK, d), chunk_row),
                  pl.BlockSpec((1, 1, d), lambda c, off, gran, exp, base: (mod_row_of_chunk(c), 0, 5)),
                  pl.BlockSpec((1, d), lambda c, off, gran, exp, base: (0, 0))],
        out_specs=pl.BlockSpec((MOE_CHUNK, d), chunk_row),
        scratch_shapes=[pltpu.VMEM((2, COMBINE_SLOTS * COMBINE_GRANULE, d), BF16),
                        pltpu.SemaphoreType.DMA((2,)),
                        pltpu.VMEM((MOE_CHUNK, d), F32)],
    )
    return pl.pallas_call(
        _combine_kernel,
        grid_spec=grid_spec,
        out_shape=jax.ShapeDtypeStruct((n, d), F32),
        compiler_params=_params("arbitrary"),
        name="combine_residual",
    )(chunk_off, *chunk_items, yb, ti_t, x2, mod, g_post_f.reshape(1, d))


def _grid_pos_embedding(rows, d):
    row = jnp.repeat(jnp.arange(rows, dtype=F32), GRID_W)
    col = jnp.tile(jnp.arange(GRID_W, dtype=F32), rows)
    quarter = d // 4
    omega = POS_BASE ** (-jnp.arange(quarter, dtype=F32) / quarter)

    def emb(p):
        ang = p[:, None] * omega[None, :]
        return jnp.concatenate([jnp.sin(ang), jnp.cos(ang)], axis=-1)

    return jnp.concatenate([emb(row), emb(col)], axis=-1)


def _count_le(sorted_vals, queries):
    return jnp.sum(sorted_vals <= queries[:, None], axis=1).astype(jnp.int32)


def _moe_plan(cnt_after, n_tok, tm):
    assert tm == GATHER_SUB
    sub_after = cnt_after[:, 0, :N_EXPERTS].astype(jnp.int32)
    counts = sub_after[-1]
    padded = (counts + EXPERT_BLOCK - 1) // EXPERT_BLOCK * EXPERT_BLOCK
    padded_end = jnp.cumsum(padded)
    padded_start = padded_end - padded
    n_blocks = n_tok * TOP_K // EXPERT_BLOCK + N_EXPERTS
    blocks = jnp.arange(n_blocks, dtype=jnp.int32)
    block_expert = jnp.minimum(_count_le(padded_end[None, :], blocks * EXPERT_BLOCK), N_EXPERTS - 1)
    n_used = (padded_end[-1:] // EXPERT_BLOCK).astype(jnp.int32)
    of_block = (block_expert[:, None] == jnp.arange(N_EXPERTS, dtype=jnp.int32)[None, :]).astype(jnp.int32)
    r_lo = (blocks - jnp.sum(of_block * padded_start[None, :], axis=1) // EXPERT_BLOCK) * EXPERT_BLOCK
    r_hi = jnp.minimum(jnp.sum(of_block * counts[None, :], axis=1), r_lo + EXPERT_BLOCK) - 1
    through = jnp.sum(of_block[:, :, None] * sub_after.T[None, :, :], axis=1)
    after = jnp.sum(of_block * padded_end[None, :], axis=1) // EXPERT_BLOCK
    next_expert = jnp.sum((after[:, None] == blocks[None, :]).astype(jnp.int32) * block_expert[None, :], axis=1)
    next_expert = jnp.where(after < n_used[0], next_expert, -1).astype(jnp.int32)
    n_sub = sub_after.shape[0]
    sub_lo = jnp.minimum(_count_le(through, r_lo), n_sub - 1)
    sub_hi = jnp.minimum(_count_le(through, r_hi), n_sub - 1)
    nch = n_tok // MOE_CHUNK
    per_chunk = MOE_CHUNK // tm
    cb_after = sub_after[per_chunk - 1::per_chunk]
    cb_before = jnp.concatenate([jnp.zeros((1, N_EXPERTS), jnp.int32), cb_after[:-1]], axis=0)
    b_lo = ((padded_start[None, :] + cb_before) // COMBINE_GRANULE).reshape(-1)
    b_hi = ((padded_start[None, :] + cb_after - 1) // COMBINE_GRANULE).reshape(-1)
    n_it = jnp.where((cb_after > cb_before).reshape(-1), b_hi - b_lo + 1, 0)
    off_end = jnp.cumsum(n_it)
    off = off_end - n_it
    n_items = n_blocks * (EXPERT_BLOCK // COMBINE_GRANULE) + N_EXPERTS * (nch - 1)
    w = jnp.minimum(jnp.arange(n_items, dtype=jnp.int32), off_end[-1] - 1)
    chunk_end = off_end[N_EXPERTS - 1::N_EXPERTS]
    chunk_off = jnp.concatenate([jnp.zeros((1,), jnp.int32), chunk_end]).astype(jnp.int32)
    chunk_of = jnp.minimum(_count_le(chunk_end[None, :], w), nch - 1)
    chunk_expert = jnp.minimum(_count_le(off_end.reshape(nch, N_EXPERTS)[chunk_of], w), N_EXPERTS - 1)
    cell = chunk_of * N_EXPERTS + chunk_expert
    chunk_granules = (b_lo[cell] + w - off[cell]).astype(jnp.int32)
    chunk_base = chunk_granules * COMBINE_GRANULE - padded_start[chunk_expert]
    return block_expert, next_expert, r_lo, sub_lo, sub_hi, n_used, chunk_off, (chunk_granules, chunk_expert, chunk_base)


def kernel(x, c, ctx, c_ctx, w_mod, b_mod, g_pre_mix, g_post_mix, g_pre_ffn, g_post_ffn, w_in, conv_kv,
           conv_q, a_log, dt_bias, gdn_norm_g, w_proj_a, conf_dw, conf_dw_b, conf_ln_g, conf_ln_b,
           w_proj_b, w_out, w_router, b_router, w_gate_up, b_gate_up, w_down, b_down):
    batch, seq, d = x.shape
    ctx_len = ctx.shape[1]
    n_tok = batch * seq
    gw = HEADS * DV
    beta_off = HEADS * DK + gw
    state_cols = beta_off + 4 * HEADS
    q_off = state_cols

    mod_rows = 2 * SUBLANES
    cond = jnp.zeros((mod_rows, d), F32).at[:batch].set(c).at[batch].set(c_ctx)
    mod = modulation(cond, w_mod[0], b_mod[0]).reshape(mod_rows, 1, 6 * d)

    w = w_in[0]
    w_main = projection_weights(w, keep=beta_off, skip=state_cols - beta_off)
    w_ba = jnp.zeros((d, LANES), F32).at[:, :4 * HEADS].set(w[:, beta_off:state_cols]).astype(BF16)

    pos = _grid_pos_embedding(seq // GRID_W, d)
    x_flat = x.reshape(n_tok, d)
    tm = 1024
    tiles_per_seq = seq // tm
    proj, ba = input_projection(x_flat, pos, mod, lambda i: i // tiles_per_seq, g_pre_mix[0],
                                w_main, w_ba, tm=tm, tn=2048)
    proj_ctx, ba_ctx = input_projection(ctx.reshape(batch * ctx_len, d), None, mod, lambda i: batch,
                                        g_pre_mix[0], w_main[:, :beta_off], w_ba, tm=ctx_len, tn=1024)

    og = gated_deltanet(proj, ba, proj_ctx, ba_ctx, conv_kv[0], conv_q[0], a_log[0], dt_bias[0],
                        gdn_norm_g[0], batch=batch, seq=seq, ctx=ctx_len)
    uc = conformer_conv(proj, conf_dw[0], conf_dw_b[0], batch=batch, seq=seq)

    tm2 = 512
    w_r = jnp.zeros((d, LANES), F32).at[:, :N_EXPERTS].set(w_router[0]).astype(BF16)
    b_r = jnp.zeros((1, LANES), F32).at[0, :N_EXPERTS].set(b_router[0])
    x2, h2, top_i, top_w, cnt_after = mixer_output(
        og, uc, proj, x_flat, pos, mod, lambda i: i // (seq // tm2), g_post_mix[0], g_pre_ffn[0],
        conf_ln_g[0], conf_ln_b[0], w_proj_a[0].astype(BF16), w_proj_b[0].astype(BF16),
        w_out[0].astype(BF16), w_r, b_r, tm=tm2, sub=GATHER_SUB)

    block_expert, next_expert, rank_lo, sub_lo, sub_hi, n_used, chunk_off, chunk_items = _moe_plan(
        cnt_after, n_tok, GATHER_SUB)
    yb = expert_blocks(h2, top_i, top_w, block_expert, next_expert, rank_lo, sub_lo, sub_hi, n_used, w_gate_up[0],
                       b_gate_up[0], w_down[0], b_down[0])
    out = combine_residual(yb, top_i, x2, chunk_off, chunk_items, mod, lambda c: c // (seq // MOE_CHUNK),
                           g_post_ffn[0])
    return out.reshape(batch, seq, d)
```

```python
import functools

import jax
import jax.numpy as jnp
from jax import lax
from jax.experimental import pallas as pl
from jax.experimental.pallas import tpu as pltpu

F32 = jnp.float32
BF16 = jnp.bfloat16

D_MODEL = 1024
GRID_W = 64
HEADS = 8
DK = 128
DV = 128
SHORT_CONV = 5
CHUNK = 64
CONF_KERNEL = 31
N_EXPERTS = 32
TOP_K = 4
SWIGLU_LIMIT = 7.0
SWIGLU_ALPHA = 1.702
MOE_BLOCK = 128
EPS = 1e-6
POS_BASE = 10000.0

LANES = 128
SUBLANES = 8
VMEM_LIMIT = 60 * 1024 * 1024

COL_K, COL_V, COL_Q, COL_Z, COL_GLU_A, COL_GLU_G, COL_GATE_A, COL_GATE_B = range(8)


def _params(*sem):
    return pltpu.CompilerParams(dimension_semantics=sem, vmem_limit_bytes=VMEM_LIMIT)


def _mod_kernel(c_ref, w_ref, b_ref, o_ref):
    c = c_ref[...]
    s = c * jax.nn.sigmoid(c)
    o_ref[...] = jnp.dot(s, w_ref[...], preferred_element_type=F32,
                         precision=lax.Precision.HIGHEST) + b_ref[...]


def modulation(cond, w_mod, b_mod):
    r, d = cond.shape
    n = w_mod.shape[1]
    tn = 1024
    return pl.pallas_call(
        _mod_kernel,
        grid=(n // tn,),
        in_specs=[pl.BlockSpec((r, d), lambda j: (0, 0)),
                  pl.BlockSpec((d, tn), lambda j: (0, j)),
                  pl.BlockSpec((1, tn), lambda j: (0, j))],
        out_specs=pl.BlockSpec((r, tn), lambda j: (0, j)),
        out_shape=jax.ShapeDtypeStruct((r, n), F32),
        compiler_params=_params("arbitrary"),
        name="modulation",
    )(cond, w_mod, b_mod.reshape(1, n))


def _proj_weights_kernel(a_ref, b_ref, o_ref, *, first_shifted, shift):
    @pl.when(pl.program_id(0) < first_shifted)
    def _():
        o_ref[...] = a_ref[...].astype(o_ref.dtype)

    @pl.when(pl.program_id(0) >= first_shifted)
    def _():
        o_ref[...] = jnp.concatenate([a_ref[:, shift:], b_ref[:, :shift]], axis=1).astype(o_ref.dtype)


def projection_weights(w, *, keep, skip, tn=1024):
    d, total = w.shape
    n_out = total - skip
    assert keep % tn == 0 and n_out % tn == 0 and 0 < skip < LANES
    per_tile = tn // LANES
    return pl.pallas_call(
        functools.partial(_proj_weights_kernel, first_shifted=keep // tn, shift=skip),
        grid=(n_out // tn,),
        in_specs=[pl.BlockSpec((d, tn), lambda j: (0, j)),
                  pl.BlockSpec((d, LANES), lambda j: (0, (j + 1) * per_tile))],
        out_specs=pl.BlockSpec((d, tn), lambda j: (0, j)),
        out_shape=jax.ShapeDtypeStruct((d, n_out), BF16),
        compiler_params=_params("arbitrary"),
        name="projection_weights",
    )(w, w)


def _inproj_kernel(*refs, has_pos):
    if has_pos:
        x_ref, pos_ref, sh_ref, sc_ref, g_ref, w_ref, wba_ref, o_ref, ba_ref, h_scr = refs
    else:
        x_ref, sh_ref, sc_ref, g_ref, w_ref, wba_ref, o_ref, ba_ref, h_scr = refs

    @pl.when(pl.program_id(1) == 0)
    def _():
        x = x_ref[...]
        if has_pos:
            x = x + pos_ref[...]
        y = x * lax.rsqrt(jnp.mean(x * x, axis=-1, keepdims=True) + EPS) * g_ref[...]
        h = (y * (1.0 + sc_ref[0]) + sh_ref[0]).astype(BF16)
        h_scr[...] = h
        ba_ref[...] = jnp.dot(h, wba_ref[...], preferred_element_type=F32)

    o_ref[...] = jnp.dot(h_scr[...], w_ref[...], preferred_element_type=F32).astype(o_ref.dtype)


def input_projection(x, pos, mod, mod_row_of_tile, g_pre, w_main, w_ba, *, tm, tn):
    n, d = x.shape
    w = w_main.shape[1]
    has_pos = pos is not None
    in_specs = [pl.BlockSpec((tm, d), lambda i, j: (i, 0))]
    args = [x]
    if has_pos:
        pos_tiles = pos.shape[0] // tm
        in_specs.append(pl.BlockSpec((tm, d), lambda i, j: (i % pos_tiles, 0)))
        args.append(pos)
    in_specs += [
        pl.BlockSpec((1, 1, d), lambda i, j: (mod_row_of_tile(i), 0, 0)),
        pl.BlockSpec((1, 1, d), lambda i, j: (mod_row_of_tile(i), 0, 1)),
        pl.BlockSpec((1, d), lambda i, j: (0, 0)),
        pl.BlockSpec((d, tn), lambda i, j: (0, j)),
        pl.BlockSpec((d, LANES), lambda i, j: (0, 0)),
    ]
    args += [mod, mod, g_pre.reshape(1, d), w_main, w_ba]
    return pl.pallas_call(
        functools.partial(_inproj_kernel, has_pos=has_pos),
        grid=(n // tm, w // tn),
        in_specs=in_specs,
        out_specs=[pl.BlockSpec((tm, tn), lambda i, j: (i, j)),
                   pl.BlockSpec((tm, LANES), lambda i, j: (i, 0))],
        out_shape=[jax.ShapeDtypeStruct((n, w), BF16), jax.ShapeDtypeStruct((n, LANES), F32)],
        scratch_shapes=[pltpu.VMEM((tm, d), BF16)],
        compiler_params=_params("arbitrary", "arbitrary"),
        name="input_projection_pos" if has_pos else "input_projection_ctx",
    )(*args)


def _silu(x):
    return x * jax.nn.sigmoid(x)


def _softplus(x):
    return jnp.maximum(x, 0.0) + jnp.log(1.0 + jnp.exp(-jnp.abs(x)))


def _short_conv(src, taps, pad_scr, rows):
    half = SHORT_CONV // 2
    pad_scr[0:SUBLANES, :] = jnp.zeros((SUBLANES, LANES), F32)
    pad_scr[SUBLANES:SUBLANES + rows, :] = src.astype(F32)
    pad_scr[SUBLANES + rows:2 * SUBLANES + rows, :] = jnp.zeros((SUBLANES, LANES), F32)
    acc = None
    for j in range(SHORT_CONV):
        off = SUBLANES + j - half
        term = pad_scr[off:off + rows, :] * taps[j:j + 1, :]
        acc = term if acc is None else acc + term
    return _silu(acc)


def _l2n(x):
    return x * lax.rsqrt(jnp.sum(x * x, axis=-1, keepdims=True) + EPS)


def _gdn_kernel(k_ref, v_ref, q_ref, z_ref, ba_ref, kc_ref, vc_ref, bac_ref,
                wk_ref, wv_ref, wq_ref, ab_ref, gn_ref, o_ref,
                kf, vf, qf, gates, pad_scr, lhs, val, att, ktt, dec, osc, *, seq, ctx, unroll, group):
    total = ctx + seq
    nc = total // CHUNK
    ncc = ctx // CHUNK
    two = 2 * CHUNK
    lg_chunk = CHUNK.bit_length() - 1

    ri = lax.broadcasted_iota(jnp.int32, (two, two), 0)
    ci = lax.broadcasted_iota(jnp.int32, (two, two), 1)
    same_dir = (ri >> lg_chunk) == (ci >> lg_chunk)
    incl = same_dir & (((ri < CHUNK) & (ri >= ci)) | ((ri >= CHUNK) & (ri <= ci)))
    strict = incl & (ri != ci)
    eye = jnp.where(ri == ci, 1.0, 0.0)
    tri = jnp.where(incl, 1.0, 0.0).astype(BF16)
    level_masks = []
    for lg in range(lg_chunk):
        same_parent = (ri >> (lg + 1)) == (ci >> (lg + 1))
        level_masks.append(jnp.where(same_parent & ((ri >> lg) != (ci >> lg)), 1.0, 0.0))
    top_rows = lax.broadcasted_iota(jnp.int32, (two, LANES), 0) < CHUNK

    def bwd_chunk(t):
        return jnp.where(t < ncc, ncc - 1 - t, nc + ncc - 1 - t)

    def stacked(ref, rf, rb):
        return jnp.concatenate([ref[pl.ds(rf, CHUNK), :], ref[pl.ds(rb, CHUNK), :]], axis=0)

    def chunk_load(t):
        rf = pl.multiple_of(t * CHUNK, CHUNK)
        rb = pl.multiple_of(bwd_chunk(t) * CHUNK, CHUNK)
        beta = jnp.concatenate([gates[0, pl.ds(rf, CHUNK), :], gates[1, pl.ds(rb, CHUNK), :]], axis=0)
        g = jnp.concatenate([gates[2, pl.ds(rf, CHUNK), :], gates[3, pl.ds(rb, CHUNK), :]], axis=0)
        return stacked(kf, rf, rb), stacked(vf, rf, rb), stacked(qf, rf, rb), beta, g

    def chunk_triangle(k, q, beta, g):
        kb = k.astype(BF16)
        kq = jnp.concatenate([kb, q.astype(BF16)], axis=0)
        kk_qk = lax.dot_general(kq, kb, (((1,), (1,)), ((), ())), preferred_element_type=F32)
        g_hi = g.astype(BF16)
        r1 = g - g_hi.astype(F32)
        g_mid = r1.astype(BF16)
        g_lo = (r1 - g_mid.astype(F32)).astype(BF16)
        gc3 = jnp.dot(tri, jnp.concatenate([g_hi, g_mid, g_lo], axis=1), preferred_element_type=F32)
        gc = gc3[:, :LANES] + gc3[:, LANES:2 * LANES] + gc3[:, 2 * LANES:]
        gc_row = jnp.transpose(gc)
        decay = jnp.where(incl, jnp.exp(gc - gc_row), 0.0)
        a = jnp.where(strict, beta * kk_qk[:two] * decay, 0.0)
        return a, (kk_qk[two:] * decay).astype(BF16), gc

    def chunk_finish(hh, t, k, v, q, beta, gc, attn, tmat):
        eg = jnp.exp(gc)
        rhs = jnp.concatenate([v * beta, k * beta * eg], axis=1).astype(BF16)
        sol = jnp.dot(tmat.astype(BF16), rhs, preferred_element_type=F32)
        g_end = jnp.where(top_rows, jnp.broadcast_to(gc[CHUNK - 1:CHUNK, :], (two, LANES)),
                          jnp.broadcast_to(gc[CHUNK:CHUNK + 1, :], (two, LANES)))
        k_tail = k * jnp.exp(g_end - gc)
        qg = (q * eg).astype(BF16)
        kcum = sol[:, DV:].astype(BF16)
        r0 = pl.multiple_of(t * two, two)
        lhs[hh, 0, pl.ds(r0, two), :] = jnp.concatenate([kcum[:CHUNK], qg[:CHUNK]], axis=0)
        lhs[hh, 1, pl.ds(r0, two), :] = jnp.concatenate([kcum[CHUNK:], qg[CHUNK:]], axis=0)
        val[hh, pl.ds(r0, two), :] = sol[:, :DV].astype(BF16)
        att[hh, pl.ds(r0, two), :] = attn
        ktt[hh, pl.ds(r0, two), :] = jnp.transpose(k_tail).astype(BF16)
        d0 = pl.multiple_of(t * SUBLANES, SUBLANES)
        e_end = jnp.exp(g_end)
        dec[hh, 0, pl.ds(d0, SUBLANES), :] = e_end[:SUBLANES]
        dec[hh, 1, pl.ds(d0, SUBLANES), :] = e_end[CHUNK:CHUNK + SUBLANES]

    for hh in range(group):
        head = pl.program_id(1) * group + hh
        cols = slice(hh * LANES, (hh + 1) * LANES)

        kf[0:ctx, :] = _l2n(_short_conv(kc_ref[:, cols], wk_ref[:, cols], pad_scr, ctx))
        vf[0:ctx, :] = _short_conv(vc_ref[:, cols], wv_ref[:, cols], pad_scr, ctx)
        qf[0:ctx, :] = jnp.zeros((ctx, LANES), F32)
        kf[ctx:total, :] = _l2n(_short_conv(k_ref[:, cols], wk_ref[:, cols], pad_scr, seq))
        vf[ctx:total, :] = _short_conv(v_ref[:, cols], wv_ref[:, cols], pad_scr, seq)
        qf[ctx:total, :] = _l2n(_short_conv(q_ref[:, cols], wq_ref[:, cols], pad_scr, seq)) * (DK ** -0.5)

        def gate_cols(src_ref, lo, rows):
            x = src_ref[...]
            lane = lax.broadcasted_iota(jnp.int32, (rows, LANES), 1)
            beta = jax.nn.sigmoid(x)
            g = -jnp.exp(ab_ref[0:1, :]) * _softplus(x + ab_ref[1:2, :])
            for slot, (arr, base) in enumerate(((beta, 0), (beta, HEADS), (g, 2 * HEADS), (g, 3 * HEADS))):
                col = jnp.sum(jnp.where(lane == base + head, arr, 0.0), axis=-1, keepdims=True)
                gates[slot, lo:lo + rows, :] = jnp.broadcast_to(col, (rows, LANES))

        gate_cols(bac_ref, 0, ctx)
        gate_cols(ba_ref, ctx, seq)

        def chunk_body(i, carry, hh=hh):
            steps = [i * unroll + u for u in range(unroll)]
            loaded = [chunk_load(t) for t in steps]
            tris = [chunk_triangle(k, q, beta, g) for (k, v, q, beta, g) in loaded]
            tmats = [eye - a * level_masks[0] for (a, _, _) in tris]
            for m in level_masks[1:]:
                nxt = []
                for (a, _, _), tmat in zip(tris, tmats):
                    cs = (a * m).astype(BF16)
                    tb = tmat.astype(BF16)
                    tc = jnp.dot(tb, cs, preferred_element_type=F32)
                    nxt.append(tmat - jnp.dot(tc.astype(BF16), tb, preferred_element_type=F32))
                tmats = nxt
            for t, (k, v, q, beta, g), (a, attn, gc), tmat in zip(steps, loaded, tris, tmats):
                chunk_finish(hh, t, k, v, q, beta, gc, attn, tmat)
            return carry

        lax.fori_loop(0, nc // unroll, chunk_body, 0)

    def scan_body(t, carry):
        rf = pl.multiple_of(t * CHUNK, CHUNK)
        rb = pl.multiple_of(bwd_chunk(t) * CHUNK, CHUNK)
        r0 = pl.multiple_of(t * two, two)
        d0 = pl.multiple_of(t * SUBLANES, SUBLANES)
        rs = [(jnp.dot(lhs[hh, 0, pl.ds(r0, two), :], carry[2 * hh].astype(BF16), preferred_element_type=F32),
               jnp.dot(lhs[hh, 1, pl.ds(r0, two), :], carry[2 * hh + 1].astype(BF16), preferred_element_type=F32))
              for hh in range(group)]
        v_news = [val[hh, pl.ds(r0, two), :].astype(F32) - jnp.concatenate([r_f[:CHUNK], r_b[:CHUNK]], axis=0)
                  for hh, (r_f, r_b) in enumerate(rs)]
        outs, new_states = [], []
        for hh, ((r_f, r_b), v_new) in enumerate(zip(rs, v_news)):
            zeros = jnp.zeros_like(v_new)
            v_bd = jnp.concatenate([jnp.where(top_rows, v_new, zeros), jnp.where(top_rows, zeros, v_new)],
                                   axis=1).astype(BF16)
            upd = jnp.dot(ktt[hh, pl.ds(r0, two), :], v_bd, preferred_element_type=F32)
            new_states.append(carry[2 * hh] * dec[hh, 0, pl.ds(d0, SUBLANES), :][0:1, :] + upd[:, :DV])
            new_states.append(carry[2 * hh + 1] * dec[hh, 1, pl.ds(d0, SUBLANES), :][0:1, :] + upd[:, DV:])
            outs.append(jnp.concatenate([r_f[CHUNK:], r_b[CHUNK:]], axis=0)
                        + jnp.dot(att[hh, pl.ds(r0, two), :], v_new.astype(BF16), preferred_element_type=F32))
        for hh, o in enumerate(outs):
            osc[hh, pl.ds(rf, CHUNK), :] += o[:CHUNK]
            osc[hh, pl.ds(rb, CHUNK), :] += o[CHUNK:]
        return tuple(new_states)

    osc[...] = jnp.zeros_like(osc)
    zero = jnp.zeros((DK, DV), F32)
    lax.fori_loop(0, nc, scan_body, (zero,) * (2 * group))

    for hh in range(group):
        cols = slice(hh * LANES, (hh + 1) * LANES)
        o = osc[hh, ctx:total, :]
        o = o * lax.rsqrt(jnp.mean(o * o, axis=-1, keepdims=True) + EPS) * gn_ref[...]
        o_ref[:, cols] = (o * _silu(z_ref[:, cols].astype(F32))).astype(o_ref.dtype)


def gated_deltanet(proj, ba, proj_ctx, ba_ctx, conv_kv, conv_q, a_log, dt_bias, gn, *, batch, seq, ctx,
                   unroll=18, group=4):
    total = seq + ctx
    nc = total // CHUNK
    gl = group * LANES
    ng = HEADS // group
    taps = jnp.zeros((SUBLANES, conv_kv.shape[1]), F32).at[:SHORT_CONV].set(conv_kv)
    taps_q = jnp.zeros((SUBLANES, conv_q.shape[1]), F32).at[:SHORT_CONV].set(conv_q)
    ab = jnp.zeros((SUBLANES, LANES), F32)
    ab = ab.at[0, 2 * HEADS:4 * HEADS].set(a_log.reshape(-1)).at[1, 2 * HEADS:4 * HEADS].set(dt_bias.reshape(-1))
    col = lambda base: (lambda b, h: (b, base * ng + h))
    once = pl.Buffered(1)
    return pl.pallas_call(
        functools.partial(_gdn_kernel, seq=seq, ctx=ctx, unroll=unroll, group=group),
        grid=(batch, ng),
        in_specs=[
            pl.BlockSpec((seq, gl), col(COL_K), pipeline_mode=once),
            pl.BlockSpec((seq, gl), col(COL_V), pipeline_mode=once),
            pl.BlockSpec((seq, gl), col(COL_Q), pipeline_mode=once),
            pl.BlockSpec((seq, gl), col(COL_Z), pipeline_mode=once),
            pl.BlockSpec((seq, LANES), lambda b, h: (b, 0), pipeline_mode=once),
            pl.BlockSpec((ctx, gl), col(COL_K)),
            pl.BlockSpec((ctx, gl), col(COL_V)),
            pl.BlockSpec((ctx, LANES), lambda b, h: (b, 0)),
            pl.BlockSpec((SUBLANES, gl), lambda b, h: (0, h)),
            pl.BlockSpec((SUBLANES, gl), lambda b, h: (0, ng + h)),
            pl.BlockSpec((SUBLANES, gl), lambda b, h: (0, h)),
            pl.BlockSpec((SUBLANES, LANES), lambda b, h: (0, 0)),
            pl.BlockSpec((1, LANES), lambda b, h: (0, 0)),
        ],
        out_specs=pl.BlockSpec((seq, gl), lambda b, h: (b, h)),
        out_shape=jax.ShapeDtypeStruct((batch * seq, HEADS * DV), BF16),
        scratch_shapes=[
            pltpu.VMEM((total, LANES), F32),
            pltpu.VMEM((total, LANES), F32),
            pltpu.VMEM((total, LANES), F32),
            pltpu.VMEM((4, total, LANES), F32),
            pltpu.VMEM((seq + 2 * SUBLANES, LANES), F32),
            pltpu.VMEM((group, 2, nc * 2 * CHUNK, LANES), BF16),
            pltpu.VMEM((group, nc * 2 * CHUNK, LANES), BF16),
            pltpu.VMEM((group, nc * 2 * CHUNK, LANES), BF16),
            pltpu.VMEM((group, nc * 2 * CHUNK, LANES), BF16),
            pltpu.VMEM((group, 2, nc * SUBLANES, LANES), F32),
            pltpu.VMEM((group, total, LANES), F32),
        ],
        compiler_params=_params("arbitrary", "arbitrary"),
        name="gated_deltanet",
    )(proj, proj, proj, proj, ba, proj_ctx, proj_ctx, ba_ctx, taps, taps, taps_q, ab, gn.reshape(1, DV))


CONF_ROWS = 128
CONF_HALO = 16


def _conf_kernel(a_ref, g_ref, w_ref, b_ref, o_ref, pad_scr, *, seq):
    cb = a_ref.shape[1]
    u = a_ref[...].astype(F32) * jax.nn.sigmoid(g_ref[...].astype(F32))
    pad_scr[0:CONF_HALO, :] = jnp.zeros((CONF_HALO, cb), F32)
    pad_scr[CONF_HALO:CONF_HALO + seq, :] = u
    pad_scr[CONF_HALO + seq:2 * CONF_HALO + seq, :] = jnp.zeros((CONF_HALO, cb), F32)
    win = CONF_ROWS + 2 * CONF_HALO
    first = CONF_HALO - CONF_KERNEL // 2

    def body(i, carry):
        r0 = pl.multiple_of(i * CONF_ROWS, CONF_ROWS)
        w = pad_scr[pl.ds(r0, win), :]
        acc = jnp.zeros((CONF_ROWS, cb), F32) + b_ref[...]
        for sub in range(SUBLANES):
            shifted = w if sub == 0 else pltpu.roll(w, win - sub, axis=0)
            for j in range(CONF_KERNEL):
                off = first + j
                if off % SUBLANES == sub:
                    base = off - sub
                    acc = acc + shifted[base:base + CONF_ROWS, :] * w_ref[j:j + 1, :]
        o_ref[pl.ds(r0, CONF_ROWS), :] = acc.astype(o_ref.dtype)
        return carry

    lax.fori_loop(0, seq // CONF_ROWS, body, 0)


def conformer_conv(proj, conf_dw, conf_dw_b, *, batch, seq, cb=256):
    c = conf_dw.shape[1]
    nb = c // cb
    taps = jnp.zeros((32, c), F32).at[:CONF_KERNEL].set(conf_dw)
    return pl.pallas_call(
        functools.partial(_conf_kernel, seq=seq),
        grid=(batch, nb),
        in_specs=[pl.BlockSpec((seq, cb), lambda b, j: (b, COL_GLU_A * nb + j)),
                  pl.BlockSpec((seq, cb), lambda b, j: (b, COL_GLU_G * nb + j)),
                  pl.BlockSpec((32, cb), lambda b, j: (0, j)),
                  pl.BlockSpec((1, cb), lambda b, j: (0, j))],
        out_specs=pl.BlockSpec((seq, cb), lambda b, j: (b, j)),
        out_shape=jax.ShapeDtypeStruct((batch * seq, c), BF16),
        scratch_shapes=[pltpu.VMEM((seq + 2 * CONF_HALO, cb), F32)],
        compiler_params=_params("arbitrary", "arbitrary"),
        name="conformer_conv",
    )(proj, proj, taps, conf_dw_b.reshape(1, c))


def _mixer_out_kernel(og_ref, uc_ref, ga_ref, gb_ref, x_ref, pos_ref, gm_ref, shf_ref, scf_ref,
                      gpost_ref, gpre_ref, lng_ref, lnb_ref, wa_ref, wb_ref, wo_ref, wr_ref, br_ref,
                      x2_ref, h2_ref, ti_ref, tw_ref, cnt_ref, cnt_scr, *, sub):
    parts = [slice(r, r + sub) for r in range(0, og_ref.shape[0], sub)]

    def rms(v):
        return v * lax.rsqrt(jnp.mean(v * v, axis=-1, keepdims=True) + EPS)

    def layer_norm_silu(uc):
        mu = jnp.mean(uc, axis=-1, keepdims=True)
        var = jnp.mean(jnp.square(uc - mu), axis=-1, keepdims=True)
        return _silu((uc - mu) * lax.rsqrt(var + EPS) * lng_ref[...] + lnb_ref[...]).astype(BF16)

    us = [layer_norm_silu(uc_ref[p, :].astype(F32)) for p in parts]
    yas = [jnp.dot(og_ref[p, :], wa_ref[...], preferred_element_type=F32) for p in parts]
    ybs = [jnp.dot(u, wb_ref[...], preferred_element_type=F32) for u in us]
    merged = [(jax.nn.sigmoid(ga_ref[p, :].astype(F32)) * ya
               + jax.nn.sigmoid(gb_ref[p, :].astype(F32)) * yb).astype(BF16) for p, ya, yb in zip(parts, yas, ybs)]
    ys = [jnp.dot(m, wo_ref[...], preferred_element_type=F32) for m in merged]
    x2s = [x_ref[p, :] + pos_ref[p, :] + gm_ref[0] * (rms(y) * gpost_ref[...]) for p, y in zip(parts, ys)]
    h2s = [(rms(x2) * gpre_ref[...] * (1.0 + scf_ref[0]) + shf_ref[0]).astype(BF16) for x2 in x2s]
    for p, x2, h2 in zip(parts, x2s, h2s):
        x2_ref[p, :] = x2
        h2_ref[p, :] = h2
    all_logits = [jnp.dot(h2, wr_ref[...], preferred_element_type=F32) + br_ref[...] for h2 in h2s]
    lane = lax.broadcasted_iota(jnp.int32, (sub, LANES), 1).astype(F32)
    ri = lax.broadcasted_iota(jnp.int32, (sub, sub), 0)
    ci = lax.broadcasted_iota(jnp.int32, (sub, sub), 1)
    before = jnp.where(ri > ci, 1.0, 0.0).astype(BF16)

    @pl.when(pl.program_id(0) == 0)
    def _():
        cnt_scr[...] = jnp.zeros_like(cnt_scr)

    routed = []
    for logits in all_logits:
        live = jnp.where(lane < N_EXPERTS, logits, -jnp.inf)
        top_v, top_i = [], []
        for _ in range(TOP_K):
            m = jnp.max(live, axis=-1, keepdims=True)
            idx = jnp.min(jnp.where(live == m, lane, float(LANES)), axis=-1, keepdims=True)
            top_v.append(m)
            top_i.append(idx)
            live = jnp.where(lane == idx, -jnp.inf, live)
        ex = [jnp.exp(v - top_v[0]) for v in top_v]
        denom = ex[0] + ex[1] + ex[2] + ex[3]
        picked = jnp.zeros((sub, LANES), F32)
        for k in range(TOP_K):
            picked = jnp.where(lane == top_i[k], 1.0, picked)
        routed.append((top_i, [e / denom for e in ex], picked,
                       jnp.dot(before, picked.astype(BF16), preferred_element_type=F32)))

    count = cnt_scr[...]
    for j, (p, (top_i, top_w, picked, within)) in enumerate(zip(parts, routed)):
        prefix = within + count[0:1, :]
        ti = jnp.zeros((sub, LANES), F32)
        tw = jnp.zeros((sub, LANES), F32)
        for k in range(TOP_K):
            rank = jnp.sum(jnp.where(lane == top_i[k], prefix, 0.0), axis=-1, keepdims=True)
            ti = jnp.where(lane == k, top_i[k], ti)
            ti = jnp.where(lane == TOP_K + k, rank, ti)
            tw = jnp.where(lane == k, top_w[k], tw)
        ti_ref[:, p] = jnp.transpose(ti)[:SUBLANES, :].astype(jnp.int32)
        tw_ref[:, p] = jnp.transpose(tw)[:SUBLANES, :]
        count = count + jnp.sum(picked, axis=0, keepdims=True)
        cnt_ref[j] = count
    cnt_scr[...] = count


def mixer_output(og, uc, proj, x, pos, mod, mod_row_of_tile, g_post, g_pre_f, ln_g, ln_b,
                 w_a, w_b, w_o, w_r, b_r, *, tm, sub):
    n, d = x.shape
    pos_tiles = pos.shape[0] // tm
    row = lambda i: (i, 0)
    fixed = lambda i: (0, 0)
    modspec = lambda k: pl.BlockSpec((1, 1, d), lambda i: (mod_row_of_tile(i), 0, k))
    vec = pl.BlockSpec((1, d), fixed)
    mat = pl.BlockSpec((d, d), fixed)
    return pl.pallas_call(
        functools.partial(_mixer_out_kernel, sub=sub),
        grid=(n // tm,),
        in_specs=[pl.BlockSpec((tm, d), row), pl.BlockSpec((tm, d), row),
                  pl.BlockSpec((tm, d), lambda i: (i, COL_GATE_A)),
                  pl.BlockSpec((tm, d), lambda i: (i, COL_GATE_B)),
                  pl.BlockSpec((tm, d), row),
                  pl.BlockSpec((tm, d), lambda i: (i % pos_tiles, 0)),
                  modspec(2), modspec(3), modspec(4),
                  vec, vec, vec, vec, mat, mat, mat,
                  pl.BlockSpec((d, LANES), fixed), pl.BlockSpec((1, LANES), fixed)],
        out_specs=[pl.BlockSpec((tm, d), row), pl.BlockSpec((tm, d), row),
                   pl.BlockSpec((SUBLANES, tm), lambda i: (0, i)), pl.BlockSpec((SUBLANES, tm), lambda i: (0, i)),
                   pl.BlockSpec((tm // sub, SUBLANES, LANES), lambda i: (i, 0, 0))],
        out_shape=[jax.ShapeDtypeStruct((n, d), F32), jax.ShapeDtypeStruct((n, d), BF16),
                   jax.ShapeDtypeStruct((SUBLANES, n), jnp.int32), jax.ShapeDtypeStruct((SUBLANES, n), F32),
                   jax.ShapeDtypeStruct((n // sub, SUBLANES, LANES), F32)],
        scratch_shapes=[pltpu.VMEM((SUBLANES, LANES), F32)],
        compiler_params=_params("arbitrary"),
        name="mixer_output",
    )(og, uc, proj, proj, x, pos, mod, mod, mod,
      g_post.reshape(1, d), g_pre_f.reshape(1, d), ln_g.reshape(1, d), ln_b.reshape(1, d),
      w_a, w_b, w_o, w_r, b_r)


EXPERT_BLOCK = 256
MOE_CHUNK = 512
GATHER_SUB = 256
GATHER_GROUP = 3
COMBINE_GRANULE = 32
COMBINE_SLOTS = 32


def _expert_kernel(bexp_ref, bnext_ref, rlo_ref, slo_ref, shi_ref, nused_ref,
                   h2_ref, ti_ref, wt_ref, wgu_hbm, bgu_ref, wd_hbm, bd_ref, o_ref,
                   acc, rw_acc, wgu_bf, wd_bf, wgu_f32, wd_f32, wsem):
    b = pl.program_id(0)
    expert = bexp_ref[b]

    def fetch(e):
        return (pltpu.make_async_copy(wgu_hbm.at[e], wgu_f32, wsem.at[0]),
                pltpu.make_async_copy(wd_hbm.at[e], wd_f32, wsem.at[1]))

    @pl.when(((b == 0) | (expert != bexp_ref[jnp.maximum(b - 1, 0)])) & (b < nused_ref[0]))
    def _():
        @pl.when(b == 0)
        def _():
            for cp in fetch(expert):
                cp.start()

        for cp in fetch(expert):
            cp.wait()
        wgu_bf[...] = wgu_f32[...].astype(BF16)
        wd_bf[...] = wd_f32[...].astype(BF16)

        @pl.when(bnext_ref[b] >= 0)
        def _():
            for cp in fetch(bnext_ref[b]):
                cp.start()

    @pl.when(b < nused_ref[0])
    def _():
        acc[...] = jnp.zeros_like(acc)
        rw_acc[...] = jnp.zeros_like(rw_acc)
        row = lax.broadcasted_iota(jnp.int32, (EXPERT_BLOCK, GATHER_SUB), 0)
        s_lo, s_hi = slo_ref[b], shi_ref[b]
        last_sub = h2_ref.shape[0] // GATHER_SUB - 1

        def select(s, live):
            t0 = pl.multiple_of(s * GATHER_SUB, GATHER_SUB)
            ids = ti_ref[0:TOP_K, pl.ds(t0, GATHER_SUB)]
            local = ti_ref[TOP_K:2 * TOP_K, pl.ds(t0, GATHER_SUB)] - rlo_ref[b]
            if live is not True:
                local = jnp.where(live, local, -1)
            inside = (ids == expert) & (local >= 0) & (local < EXPERT_BLOCK)
            hit_row = jnp.sum(jnp.where(inside, local + 1, 0), axis=0, keepdims=True) - 1
            hit_w = jnp.sum(jnp.where(inside, wt_ref[0:TOP_K, pl.ds(t0, GATHER_SUB)], 0.0), axis=0, keepdims=True)
            hit = row == hit_row
            return (jnp.where(hit, 1.0, 0.0).astype(BF16), jnp.where(hit, hit_w, 0.0),
                    h2_ref[pl.ds(t0, GATHER_SUB), :])

        def gather(i, carry):
            s0 = s_lo + GATHER_GROUP * i
            parts = [select(s0, True)] + [select(jnp.minimum(s0 + q, last_sub), s0 + q <= s_hi)
                                          for q in range(1, GATHER_GROUP)]
            acc[...] += jnp.dot(jnp.concatenate([p for p, _, _ in parts], axis=1),
                                jnp.concatenate([x for _, _, x in parts], axis=0), preferred_element_type=F32)
            rw_acc[...] += jnp.sum(functools.reduce(lambda a, c: a + c, [w for _, w, _ in parts]), axis=1,
                                   keepdims=True)
            return carry

        lax.fori_loop(0, (s_hi - s_lo + GATHER_GROUP) // GATHER_GROUP, gather, 0)
        gu = jnp.dot(acc[...].astype(BF16), wgu_bf[...], preferred_element_type=F32) + bgu_ref[0]
        dff = gu.shape[1] // 2
        gl = jnp.minimum(gu[:, :dff], SWIGLU_LIMIT)
        lin = jnp.clip(gu[:, dff:], -SWIGLU_LIMIT, SWIGLU_LIMIT)
        act = (gl * jax.nn.sigmoid(SWIGLU_ALPHA * gl) * (lin + 1.0)).astype(BF16)
        y = jnp.dot(act, wd_bf[...], preferred_element_type=F32) + bd_ref[0]
        o_ref[...] = (y * rw_acc[...]).astype(o_ref.dtype)

    @pl.when(b >= nused_ref[0])
    def _():
        o_ref[...] = jnp.zeros_like(o_ref)


def expert_blocks(h2, ti_t, wt_t, block_expert, next_expert, rank_lo, sub_lo, sub_hi, n_used, w_gate_up, b_gate_up,
                  w_down, b_down):
    n, d = h2.shape
    e, _, f2 = w_gate_up.shape
    n_blocks = block_expert.shape[0]
    whole = lambda b, be, bn, rl, lo, hi, nu: (0, 0)
    by_expert = lambda b, be, bn, rl, lo, hi, nu: (be[b], 0, 0)
    once = pl.Buffered(1)
    grid_spec = pltpu.PrefetchScalarGridSpec(
        num_scalar_prefetch=6,
        grid=(n_blocks,),
        in_specs=[pl.BlockSpec((n, d), whole, pipeline_mode=once),
                  pl.BlockSpec((SUBLANES, n), whole, pipeline_mode=once),
                  pl.BlockSpec((SUBLANES, n), whole, pipeline_mode=once),
                  pl.BlockSpec(memory_space=pl.ANY),
                  pl.BlockSpec((1, 1, f2), by_expert),
                  pl.BlockSpec(memory_space=pl.ANY),
                  pl.BlockSpec((1, 1, d), by_expert)],
        out_specs=pl.BlockSpec((EXPERT_BLOCK, d), lambda b, be, bn, rl, lo, hi, nu: (b, 0)),
        scratch_shapes=[pltpu.VMEM((EXPERT_BLOCK, d), F32), pltpu.VMEM((EXPERT_BLOCK, 1), F32),
                        pltpu.VMEM((d, f2), BF16), pltpu.VMEM((f2 // 2, d), BF16),
                        pltpu.VMEM((d, f2), F32), pltpu.VMEM((f2 // 2, d), F32),
                        pltpu.SemaphoreType.DMA((2,))],
    )
    return pl.pallas_call(
        _expert_kernel,
        grid_spec=grid_spec,
        out_shape=jax.ShapeDtypeStruct((n_blocks * EXPERT_BLOCK, d), BF16),
        compiler_params=_params("arbitrary"),
        name="expert_blocks",
    )(block_expert, next_expert, rank_lo, sub_lo, sub_hi, n_used, h2, ti_t, wt_t, w_gate_up, b_gate_up.reshape(e, 1, f2), w_down,
      b_down.reshape(e, 1, d))


def _combine_kernel(coff_ref, cgran_ref, cexp_ref, cbase_ref, yb_hbm, ti_ref, x2_ref, gf_ref, g_ref, o_ref,
                    slab, sem, acc):
    c = pl.program_id(0)
    start = coff_ref[c]
    count = coff_ref[c + 1] - start
    n_groups = (count + COMBINE_SLOTS - 1) // COMBINE_SLOTS

    def item(g, s):
        return start + jnp.minimum(g * COMBINE_SLOTS + s, count - 1)

    def granule_of(g, s):
        return cgran_ref[item(g, s)]

    def copies(g, buf):
        return [pltpu.make_async_copy(
            yb_hbm.at[pl.ds(pl.multiple_of(granule_of(g, s) * COMBINE_GRANULE, COMBINE_GRANULE), COMBINE_GRANULE), :],
            slab.at[buf, pl.ds(s * COMBINE_GRANULE, COMBINE_GRANULE), :], sem.at[buf])
            for s in range(COMBINE_SLOTS)]

    for cp in copies(0, 0):
        cp.start()
    acc[...] = jnp.zeros_like(acc)
    chunk = x2_ref.shape[0]
    row = lax.broadcasted_iota(jnp.int32, (COMBINE_GRANULE, chunk), 0)
    ids = ti_ref[0:TOP_K, :]
    ranks = ti_ref[TOP_K:2 * TOP_K, :]

    def group(g, carry):
        buf = g % 2

        @pl.when(g + 1 < n_groups)
        def _():
            for cp in copies(g + 1, 1 - buf):
                cp.start()

        for cp in copies(g, buf):
            cp.wait()
        pieces = []
        for s in range(COMBINE_SLOTS):
            live = g * COMBINE_SLOTS + s < count
            local = ranks - cbase_ref[item(g, s)]
            inside = (ids == jnp.where(live, cexp_ref[item(g, s)], -1)) & (local >= 0) & (local < COMBINE_GRANULE)
            hit_row = jnp.sum(jnp.where(inside, local + 1, 0), axis=0, keepdims=True) - 1
            pieces.append(jnp.where(row == hit_row, 1.0, 0.0).astype(BF16))
        onehot_t = jnp.concatenate(pieces, axis=0)
        acc[...] += lax.dot_general(onehot_t, slab[buf], (((0,), (0,)), ((), ())),
                                    preferred_element_type=F32)
        return carry

    lax.fori_loop(0, n_groups, group, 0)
    y = acc[...]
    yn = y * lax.rsqrt(jnp.mean(y * y, axis=-1, keepdims=True) + EPS) * g_ref[...]
    o_ref[...] = x2_ref[...] + gf_ref[0] * yn


def combine_residual(yb, ti_t, x2, chunk_off, chunk_items, mod, mod_row_of_chunk, g_post_f):
    n, d = x2.shape
    chunk_row = lambda c, off, gran, exp, base: (c, 0)
    grid_spec = pltpu.PrefetchScalarGridSpec(
        num_scalar_prefetch=4,
        grid=(n // MOE_CHUNK,),
        in_specs=[pl.BlockSpec(memory_space=pl.ANY),
                  pl.BlockSpec((SUBLANES, MOE_CHUNK), lambda c, off, gran, exp, base: (0, c)),
                  pl.BlockSpec((MOE_CHUNK, d), chunk_row),
                  pl.BlockSpec((1, 1, d), lambda c, off, gran, exp, base: (mod_row_of_chunk(c), 0, 5)),
                  pl.BlockSpec((1, d), lambda c, off, gran, exp, base: (0, 0))],
        out_specs=pl.BlockSpec((MOE_CHUNK, d), chunk_row),
        scratch_shapes=[pltpu.VMEM((2, COMBINE_SLOTS * COMBINE_GRANULE, d), BF16),
                        pltpu.SemaphoreType.DMA((2,)),
                        pltpu.VMEM((MOE_CHUNK, d), F32)],
    )
    return pl.pallas_call(
        _combine_kernel,
        grid_spec=grid_spec,
        out_shape=jax.ShapeDtypeStruct((n, d), F32),
        compiler_params=_params("arbitrary"),
        name="combine_residual",
    )(chunk_off, *chunk_items, yb, ti_t, x2, mod, g_post_f.reshape(1, d))


def _grid_pos_embedding(rows, d):
    row = jnp.repeat(jnp.arange(rows, dtype=F32), GRID_W)
    col = jnp.tile(jnp.arange(GRID_W, dtype=F32), rows)
    quarter = d // 4
    omega = POS_BASE ** (-jnp.arange(quarter, dtype=F32) / quarter)

    def emb(p):
        ang = p[:, None] * omega[None, :]
        return jnp.concatenate([jnp.sin(ang), jnp.cos(ang)], axis=-1)

    return jnp.concatenate([emb(row), emb(col)], axis=-1)


def _count_le(sorted_vals, queries):
    return jnp.sum(sorted_vals <= queries[:, None], axis=1).astype(jnp.int32)


def _moe_plan(cnt_after, n_tok, tm):
    assert tm == GATHER_SUB
    sub_after = cnt_after[:, 0, :N_EXPERTS].astype(jnp.int32)
    counts = sub_after[-1]
    padded = (counts + EXPERT_BLOCK - 1) // EXPERT_BLOCK * EXPERT_BLOCK
    padded_end = jnp.cumsum(padded)
    padded_start = padded_end - padded
    n_blocks = n_tok * TOP_K // EXPERT_BLOCK + N_EXPERTS
    blocks = jnp.arange(n_blocks, dtype=jnp.int32)
    block_expert = jnp.minimum(_count_le(padded_end[None, :], blocks * EXPERT_BLOCK), N_EXPERTS - 1)
    n_used = (padded_end[-1:] // EXPERT_BLOCK).astype(jnp.int32)
    of_block = (block_expert[:, None] == jnp.arange(N_EXPERTS, dtype=jnp.int32)[None, :]).astype(jnp.int32)
    r_lo = (blocks - jnp.sum(of_block * padded_start[None, :], axis=1) // EXPERT_BLOCK) * EXPERT_BLOCK
    r_hi = jnp.minimum(jnp.sum(of_block * counts[None, :], axis=1), r_lo + EXPERT_BLOCK) - 1
    through = jnp.sum(of_block[:, :, None] * sub_after.T[None, :, :], axis=1)
    after = jnp.sum(of_block * padded_end[None, :], axis=1) // EXPERT_BLOCK
    next_expert = jnp.sum((after[:, None] == blocks[None, :]).astype(jnp.int32) * block_expert[None, :], axis=1)
    next_expert = jnp.where(after < n_used[0], next_expert, -1).astype(jnp.int32)
    n_sub = sub_after.shape[0]
    sub_lo = jnp.minimum(_count_le(through, r_lo), n_sub - 1)
    sub_hi = jnp.minimum(_count_le(through, r_hi), n_sub - 1)
    nch = n_tok // MOE_CHUNK
    per_chunk = MOE_CHUNK // tm
    cb_after = sub_after[per_chunk - 1::per_chunk]
    cb_before = jnp.concatenate([jnp.zeros((1, N_EXPERTS), jnp.int32), cb_after[:-1]], axis=0)
    b_lo = ((padded_start[None, :] + cb_before) // COMBINE_GRANULE).reshape(-1)
    b_hi = ((padded_start[None, :] + cb_after - 1) // COMBINE_GRANULE).reshape(-1)
    n_it = jnp.where((cb_after > cb_before).reshape(-1), b_hi - b_lo + 1, 0)
    off_end = jnp.cumsum(n_it)
    off = off_end - n_it
    n_items = n_blocks * (EXPERT_BLOCK // COMBINE_GRANULE) + N_EXPERTS * (nch - 1)
    w = jnp.minimum(jnp.arange(n_items, dtype=jnp.int32), off_end[-1] - 1)
    chunk_end = off_end[N_EXPERTS - 1::N_EXPERTS]
    chunk_off = jnp.concatenate([jnp.zeros((1,), jnp.int32), chunk_end]).astype(jnp.int32)
    chunk_of = jnp.minimum(_count_le(chunk_end[None, :], w), nch - 1)
    chunk_expert = jnp.minimum(_count_le(off_end.reshape(nch, N_EXPERTS)[chunk_of], w), N_EXPERTS - 1)
    cell = chunk_of * N_EXPERTS + chunk_expert
    chunk_granules = (b_lo[cell] + w - off[cell]).astype(jnp.int32)
    chunk_base = chunk_granules * COMBINE_GRANULE - padded_start[chunk_expert]
    return block_expert, next_expert, r_lo, sub_lo, sub_hi, n_used, chunk_off, (chunk_granules, chunk_expert, chunk_base)


def kernel(x, c, ctx, c_ctx, w_mod, b_mod, g_pre_mix, g_post_mix, g_pre_ffn, g_post_ffn, w_in, conv_kv,
           conv_q, a_log, dt_bias, gdn_norm_g, w_proj_a, conf_dw, conf_dw_b, conf_ln_g, conf_ln_b,
           w_proj_b, w_out, w_router, b_router, w_gate_up, b_gate_up, w_down, b_down):
    batch, seq, d = x.shape
    ctx_len = ctx.shape[1]
    n_tok = batch * seq
    gw = HEADS * DV
    beta_off = HEADS * DK + gw
    state_cols = beta_off + 4 * HEADS
    q_off = state_cols

    mod_rows = 2 * SUBLANES
    cond = jnp.zeros((mod_rows, d), F32).at[:batch].set(c).at[batch].set(c_ctx)
    mod = modulation(cond, w_mod[0], b_mod[0]).reshape(mod_rows, 1, 6 * d)

    w = w_in[0]
    w_main = projection_weights(w, keep=beta_off, skip=state_cols - beta_off)
    w_ba = jnp.zeros((d, LANES), F32).at[:, :4 * HEADS].set(w[:, beta_off:state_cols]).astype(BF16)

    pos = _grid_pos_embedding(seq // GRID_W, d)
    x_flat = x.reshape(n_tok, d)
    tm = 1024
    tiles_per_seq = seq // tm
    proj, ba = input_projection(x_flat, pos, mod, lambda i: i // tiles_per_seq, g_pre_mix[0],
                                w_main, w_ba, tm=tm, tn=2048)
    proj_ctx, ba_ctx = input_projection(ctx.reshape(batch * ctx_len, d), None, mod, lambda i: batch,
                                        g_pre_mix[0], w_main[:, :beta_off], w_ba, tm=ctx_len, tn=1024)

    og = gated_deltanet(proj, ba, proj_ctx, ba_ctx, conv_kv[0], conv_q[0], a_log[0], dt_bias[0],
                        gdn_norm_g[0], batch=batch, seq=seq, ctx=ctx_len)
    uc = conformer_conv(proj, conf_dw[0], conf_dw_b[0], batch=batch, seq=seq)

    tm2 = 512
    w_r = jnp.zeros((d, LANES), F32).at[:, :N_EXPERTS].set(w_router[0]).astype(BF16)
    b_r = jnp.zeros((1, LANES), F32).at[0, :N_EXPERTS].set(b_router[0])
    x2, h2, top_i, top_w, cnt_after = mixer_output(
        og, uc, proj, x_flat, pos, mod, lambda i: i // (seq // tm2), g_post_mix[0], g_pre_ffn[0],
        conf_ln_g[0], conf_ln_b[0], w_proj_a[0].astype(BF16), w_proj_b[0].astype(BF16),
        w_out[0].astype(BF16), w_r, b_r, tm=tm2, sub=GATHER_SUB)

    block_expert, next_expert, rank_lo, sub_lo, sub_hi, n_used, chunk_off, chunk_items = _moe_plan(
        cnt_after, n_tok, GATHER_SUB)
    yb = expert_blocks(h2, top_i, top_w, block_expert, next_expert, rank_lo, sub_lo, sub_hi, n_used, w_gate_up[0],
                       b_gate_up[0], w_down[0], b_down[0])
    out = combine_residual(yb, top_i, x2, chunk_off, chunk_items, mod, lambda c: c // (seq // MOE_CHUNK),
                           g_post_ffn[0])
    return out.reshape(batch, seq, d)
```

```python
import functools

import jax
import jax.numpy as jnp
from jax import lax
from jax.experimental import pallas as pl
from jax.experimental.pallas import tpu as pltpu

F32 = jnp.float32
BF16 = jnp.bfloat16

D_MODEL = 1024
GRID_W = 64
HEADS = 8
DK = 128
DV = 128
SHORT_CONV = 5
CHUNK = 64
CONF_KERNEL = 31
N_EXPERTS = 32
TOP_K = 4
SWIGLU_LIMIT = 7.0
SWIGLU_ALPHA = 1.702
MOE_BLOCK = 128
EPS = 1e-6
POS_BASE = 10000.0

LANES = 128
SUBLANES = 8
VMEM_LIMIT = 60 * 1024 * 1024

COL_K, COL_V, COL_Q, COL_Z, COL_GLU_A, COL_GLU_G, COL_GATE_A, COL_GATE_B = range(8)


def _params(*sem):
    return pltpu.CompilerParams(dimension_semantics=sem, vmem_limit_bytes=VMEM_LIMIT)


def _mod_kernel(c_ref, w_ref, b_ref, o_ref):
    c = c_ref[...]
    s = c * jax.nn.sigmoid(c)
    o_ref[...] = jnp.dot(s, w_ref[...], preferred_element_type=F32,
                         precision=lax.Precision.HIGHEST) + b_ref[...]


def modulation(cond, w_mod, b_mod):
    r, d = cond.shape
    n = w_mod.shape[1]
    tn = 1024
    return pl.pallas_call(
        _mod_kernel,
        grid=(n // tn,),
        in_specs=[pl.BlockSpec((r, d), lambda j: (0, 0)),
                  pl.BlockSpec((d, tn), lambda j: (0, j)),
                  pl.BlockSpec((1, tn), lambda j: (0, j))],
        out_specs=pl.BlockSpec((r, tn), lambda j: (0, j)),
        out_shape=jax.ShapeDtypeStruct((r, n), F32),
        compiler_params=_params("arbitrary"),
        name="modulation",
    )(cond, w_mod, b_mod.reshape(1, n))


def _proj_weights_kernel(a_ref, b_ref, o_ref, *, first_shifted, shift):
    @pl.when(pl.program_id(0) < first_shifted)
    def _():
        o_ref[...] = a_ref[...].astype(o_ref.dtype)

    @pl.when(pl.program_id(0) >= first_shifted)
    def _():
        o_ref[...] = jnp.concatenate([a_ref[:, shift:], b_ref[:, :shift]], axis=1).astype(o_ref.dtype)


def projection_weights(w, *, keep, skip, tn=1024):
    _, d, total = w.shape
    n_out = total - skip
    assert keep % tn == 0 and n_out % tn == 0 and 0 < skip < LANES
    per_tile = tn // LANES
    return pl.pallas_call(
        functools.partial(_proj_weights_kernel, first_shifted=keep // tn, shift=skip),
        grid=(n_out // tn,),
        in_specs=[pl.BlockSpec((None, d, tn), lambda j: (0, 0, j)),
                  pl.BlockSpec((None, d, LANES), lambda j: (0, 0, (j + 1) * per_tile))],
        out_specs=pl.BlockSpec((d, tn), lambda j: (0, j)),
        out_shape=jax.ShapeDtypeStruct((d, n_out), BF16),
        compiler_params=_params("arbitrary"),
        name="projection_weights",
    )(w, w)


def _inproj_kernel(*refs, has_pos):
    if has_pos:
        x_ref, pos_ref, sh_ref, sc_ref, g_ref, w_ref, wba_ref, o_ref, ba_ref, h_scr = refs
    else:
        x_ref, sh_ref, sc_ref, g_ref, w_ref, wba_ref, o_ref, ba_ref, h_scr = refs

    @pl.when(pl.program_id(1) == 0)
    def _():
        x = x_ref[...]
        if has_pos:
            x = x + pos_ref[...]
        y = x * lax.rsqrt(jnp.mean(x * x, axis=-1, keepdims=True) + EPS) * g_ref[...]
        h = (y * (1.0 + sc_ref[0]) + sh_ref[0]).astype(BF16)
        h_scr[...] = h
        ba_ref[...] = jnp.dot(h, wba_ref[...], preferred_element_type=F32)

    o_ref[...] = jnp.dot(h_scr[...], w_ref[...], preferred_element_type=F32).astype(o_ref.dtype)


def input_projection(x, pos, mod, mod_row_of_tile, g_pre, w_main, w_ba, *, tm, tn):
    n, d = x.shape
    w = w_main.shape[1]
    has_pos = pos is not None
    in_specs = [pl.BlockSpec((tm, d), lambda i, j: (i, 0))]
    args = [x]
    if has_pos:
        pos_tiles = pos.shape[0] // tm
        in_specs.append(pl.BlockSpec((tm, d), lambda i, j: (i % pos_tiles, 0)))
        args.append(pos)
    in_specs += [
        pl.BlockSpec((1, 1, d), lambda i, j: (mod_row_of_tile(i), 0, 0)),
        pl.BlockSpec((1, 1, d), lambda i, j: (mod_row_of_tile(i), 0, 1)),
        pl.BlockSpec((1, d), lambda i, j: (0, 0)),
        pl.BlockSpec((d, tn), lambda i, j: (0, j)),
        pl.BlockSpec((d, LANES), lambda i, j: (0, 0)),
    ]
    args += [mod, mod, g_pre.reshape(1, d), w_main, w_ba]
    return pl.pallas_call(
        functools.partial(_inproj_kernel, has_pos=has_pos),
        grid=(n // tm, w // tn),
        in_specs=in_specs,
        out_specs=[pl.BlockSpec((tm, tn), lambda i, j: (i, j)),
                   pl.BlockSpec((tm, LANES), lambda i, j: (i, 0))],
        out_shape=[jax.ShapeDtypeStruct((n, w), BF16), jax.ShapeDtypeStruct((n, LANES), F32)],
        scratch_shapes=[pltpu.VMEM((tm, d), BF16)],
        compiler_params=_params("arbitrary", "arbitrary"),
        name="input_projection_pos" if has_pos else "input_projection_ctx",
    )(*args)


def _silu(x):
    return x * jax.nn.sigmoid(x)


def _softplus(x):
    return jnp.maximum(x, 0.0) + jnp.log(1.0 + jnp.exp(-jnp.abs(x)))


def _short_conv(src, taps, pad_scr, rows):
    half = SHORT_CONV // 2
    pad_scr[0:SUBLANES, :] = jnp.zeros((SUBLANES, LANES), F32)
    pad_scr[SUBLANES:SUBLANES + rows, :] = src.astype(F32)
    pad_scr[SUBLANES + rows:2 * SUBLANES + rows, :] = jnp.zeros((SUBLANES, LANES), F32)
    acc = None
    for j in range(SHORT_CONV):
        off = SUBLANES + j - half
        term = pad_scr[off:off + rows, :] * taps[j:j + 1, :]
        acc = term if acc is None else acc + term
    return _silu(acc)


def _l2n(x):
    return x * lax.rsqrt(jnp.sum(x * x, axis=-1, keepdims=True) + EPS)


def _gdn_kernel(k_ref, v_ref, q_ref, z_ref, ba_ref, kc_ref, vc_ref, bac_ref,
                wk_ref, wv_ref, wq_ref, ab_ref, gn_ref, o_ref,
                kf, vf, qf, gates, pad_scr, lhs, val, att, ktt, dec, osc, *, seq, ctx, unroll, group):
    total = ctx + seq
    nc = total // CHUNK
    ncc = ctx // CHUNK
    two = 2 * CHUNK
    lg_chunk = CHUNK.bit_length() - 1

    ri = lax.broadcasted_iota(jnp.int32, (two, two), 0)
    ci = lax.broadcasted_iota(jnp.int32, (two, two), 1)
    same_dir = (ri >> lg_chunk) == (ci >> lg_chunk)
    incl = same_dir & (((ri < CHUNK) & (ri >= ci)) | ((ri >= CHUNK) & (ri <= ci)))
    strict = incl & (ri != ci)
    eye = jnp.where(ri == ci, 1.0, 0.0)
    tri = jnp.where(incl, 1.0, 0.0).astype(BF16)
    level_masks = []
    for lg in range(lg_chunk):
        same_parent = (ri >> (lg + 1)) == (ci >> (lg + 1))
        level_masks.append(jnp.where(same_parent & ((ri >> lg) != (ci >> lg)), 1.0, 0.0))
    top_rows = lax.broadcasted_iota(jnp.int32, (two, LANES), 0) < CHUNK

    def bwd_chunk(t):
        return jnp.where(t < ncc, ncc - 1 - t, nc + ncc - 1 - t)

    def stacked(ref, rf, rb):
        return jnp.concatenate([ref[pl.ds(rf, CHUNK), :], ref[pl.ds(rb, CHUNK), :]], axis=0)

    def chunk_load(t):
        rf = pl.multiple_of(t * CHUNK, CHUNK)
        rb = pl.multiple_of(bwd_chunk(t) * CHUNK, CHUNK)
        beta = jnp.concatenate([gates[0, pl.ds(rf, CHUNK), :], gates[1, pl.ds(rb, CHUNK), :]], axis=0)
        g = jnp.concatenate([gates[2, pl.ds(rf, CHUNK), :], gates[3, pl.ds(rb, CHUNK), :]], axis=0)
        return stacked(kf, rf, rb), stacked(vf, rf, rb), stacked(qf, rf, rb), beta, g

    def chunk_triangle(k, q, beta, g):
        kb = k.astype(BF16)
        kq = jnp.concatenate([kb, q.astype(BF16)], axis=0)
        kk_qk = lax.dot_general(kq, kb, (((1,), (1,)), ((), ())), preferred_element_type=F32)
        g_hi = g.astype(BF16)
        r1 = g - g_hi.astype(F32)
        g_mid = r1.astype(BF16)
        g_lo = (r1 - g_mid.astype(F32)).astype(BF16)
        gc3 = jnp.dot(tri, jnp.concatenate([g_hi, g_mid, g_lo], axis=1), preferred_element_type=F32)
        gc = gc3[:, :LANES] + gc3[:, LANES:2 * LANES] + gc3[:, 2 * LANES:]
        gc_row = jnp.transpose(gc)
        decay = jnp.where(incl, jnp.exp(gc - gc_row), 0.0)
        a = jnp.where(strict, beta * kk_qk[:two] * decay, 0.0)
        return a, (kk_qk[two:] * decay).astype(BF16), gc

    def chunk_finish(hh, t, k, v, q, beta, gc, attn, tmat):
        eg = jnp.exp(gc)
        rhs = jnp.concatenate([v * beta, k * beta * eg], axis=1).astype(BF16)
        sol = jnp.dot(tmat.astype(BF16), rhs, preferred_element_type=F32)
        g_end = jnp.where(top_rows, jnp.broadcast_to(gc[CHUNK - 1:CHUNK, :], (two, LANES)),
                          jnp.broadcast_to(gc[CHUNK:CHUNK + 1, :], (two, LANES)))
        k_tail = k * jnp.exp(g_end - gc)
        qg = (q * eg).astype(BF16)
        kcum = sol[:, DV:].astype(BF16)
        r0 = pl.multiple_of(t * two, two)
        lhs[hh, 0, pl.ds(r0, two), :] = jnp.concatenate([kcum[:CHUNK], qg[:CHUNK]], axis=0)
        lhs[hh, 1, pl.ds(r0, two), :] = jnp.concatenate([kcum[CHUNK:], qg[CHUNK:]], axis=0)
        val[hh, pl.ds(r0, two), :] = sol[:, :DV].astype(BF16)
        att[hh, pl.ds(r0, two), :] = attn
        ktt[hh, pl.ds(r0, two), :] = jnp.transpose(k_tail).astype(BF16)
        d0 = pl.multiple_of(t * SUBLANES, SUBLANES)
        e_end = jnp.exp(g_end)
        dec[hh, 0, pl.ds(d0, SUBLANES), :] = e_end[:SUBLANES]
        dec[hh, 1, pl.ds(d0, SUBLANES), :] = e_end[CHUNK:CHUNK + SUBLANES]

    for hh in range(group):
        head = pl.program_id(1) * group + hh
        cols = slice(hh * LANES, (hh + 1) * LANES)

        kf[0:ctx, :] = _l2n(_short_conv(kc_ref[:, cols], wk_ref[:, cols], pad_scr, ctx))
        vf[0:ctx, :] = _short_conv(vc_ref[:, cols], wv_ref[:, cols], pad_scr, ctx)
        qf[0:ctx, :] = jnp.zeros((ctx, LANES), F32)
        kf[ctx:total, :] = _l2n(_short_conv(k_ref[:, cols], wk_ref[:, cols], pad_scr, seq))
        vf[ctx:total, :] = _short_conv(v_ref[:, cols], wv_ref[:, cols], pad_scr, seq)
        qf[ctx:total, :] = _l2n(_short_conv(q_ref[:, cols], wq_ref[:, cols], pad_scr, seq)) * (DK ** -0.5)

        def gate_cols(src_ref, lo, rows):
            x = src_ref[...]
            lane = lax.broadcasted_iota(jnp.int32, (rows, LANES), 1)
            beta = jax.nn.sigmoid(x)
            g = -jnp.exp(ab_ref[0:1, :]) * _softplus(x + ab_ref[1:2, :])
            for slot, (arr, base) in enumerate(((beta, 0), (beta, HEADS), (g, 2 * HEADS), (g, 3 * HEADS))):
                col = jnp.sum(jnp.where(lane == base + head, arr, 0.0), axis=-1, keepdims=True)
                gates[slot, lo:lo + rows, :] = jnp.broadcast_to(col, (rows, LANES))

        gate_cols(bac_ref, 0, ctx)
        gate_cols(ba_ref, ctx, seq)

        def chunk_body(i, carry, hh=hh):
            steps = [i * unroll + u for u in range(unroll)]
            loaded = [chunk_load(t) for t in steps]
            tris = [chunk_triangle(k, q, beta, g) for (k, v, q, beta, g) in loaded]
            tmats = [eye - a * level_masks[0] for (a, _, _) in tris]
            for m in level_masks[1:]:
                nxt = []
                for (a, _, _), tmat in zip(tris, tmats):
                    cs = (a * m).astype(BF16)
                    tb = tmat.astype(BF16)
                    tc = jnp.dot(tb, cs, preferred_element_type=F32)
                    nxt.append(tmat - jnp.dot(tc.astype(BF16), tb, preferred_element_type=F32))
                tmats = nxt
            for t, (k, v, q, beta, g), (a, attn, gc), tmat in zip(steps, loaded, tris, tmats):
                chunk_finish(hh, t, k, v, q, beta, gc, attn, tmat)
            return carry

        lax.fori_loop(0, nc // unroll, chunk_body, 0)

    def scan_body(t, carry):
        rf = pl.multiple_of(t * CHUNK, CHUNK)
        rb = pl.multiple_of(bwd_chunk(t) * CHUNK, CHUNK)
        r0 = pl.multiple_of(t * two, two)
        d0 = pl.multiple_of(t * SUBLANES, SUBLANES)
        rs = [(jnp.dot(lhs[hh, 0, pl.ds(r0, two), :], carry[2 * hh].astype(BF16), preferred_element_type=F32),
               jnp.dot(lhs[hh, 1, pl.ds(r0, two), :], carry[2 * hh + 1].astype(BF16), preferred_element_type=F32))
              for hh in range(group)]
        v_news = [val[hh, pl.ds(r0, two), :].astype(F32) - jnp.concatenate([r_f[:CHUNK], r_b[:CHUNK]], axis=0)
                  for hh, (r_f, r_b) in enumerate(rs)]
        outs, new_states = [], []
        for hh, ((r_f, r_b), v_new) in enumerate(zip(rs, v_news)):
            zeros = jnp.zeros_like(v_new)
            v_bd = jnp.concatenate([jnp.where(top_rows, v_new, zeros), jnp.where(top_rows, zeros, v_new)],
                                   axis=1).astype(BF16)
            upd = jnp.dot(ktt[hh, pl.ds(r0, two), :], v_bd, preferred_element_type=F32)
            new_states.append(carry[2 * hh] * dec[hh, 0, pl.ds(d0, SUBLANES), :][0:1, :] + upd[:, :DV])
            new_states.append(carry[2 * hh + 1] * dec[hh, 1, pl.ds(d0, SUBLANES), :][0:1, :] + upd[:, DV:])
            outs.append(jnp.concatenate([r_f[CHUNK:], r_b[CHUNK:]], axis=0)
                        + jnp.dot(att[hh, pl.ds(r0, two), :], v_new.astype(BF16), preferred_element_type=F32))
        for hh, o in enumerate(outs):
            osc[hh, pl.ds(rf, CHUNK), :] += o[:CHUNK]
            osc[hh, pl.ds(rb, CHUNK), :] += o[CHUNK:]
        return tuple(new_states)

    osc[...] = jnp.zeros_like(osc)
    zero = jnp.zeros((DK, DV), F32)
    lax.fori_loop(0, nc, scan_body, (zero,) * (2 * group))

    for hh in range(group):
        cols = slice(hh * LANES, (hh + 1) * LANES)
        o = osc[hh, ctx:total, :]
        o = o * lax.rsqrt(jnp.mean(o * o, axis=-1, keepdims=True) + EPS) * gn_ref[...]
        o_ref[:, cols] = (o * _silu(z_ref[:, cols].astype(F32))).astype(o_ref.dtype)


def gated_deltanet(proj, ba, proj_ctx, ba_ctx, conv_kv, conv_q, a_log, dt_bias, gn, *, batch, seq, ctx,
                   unroll=18, group=4):
    total = seq + ctx
    nc = total // CHUNK
    gl = group * LANES
    ng = HEADS // group
    taps = jnp.zeros((SUBLANES, conv_kv.shape[1]), F32).at[:SHORT_CONV].set(conv_kv)
    taps_q = jnp.zeros((SUBLANES, conv_q.shape[1]), F32).at[:SHORT_CONV].set(conv_q)
    ab = jnp.zeros((SUBLANES, LANES), F32)
    ab = ab.at[0, 2 * HEADS:4 * HEADS].set(a_log.reshape(-1)).at[1, 2 * HEADS:4 * HEADS].set(dt_bias.reshape(-1))
    col = lambda base: (lambda b, h: (b, base * ng + h))
    once = pl.Buffered(1)
    return pl.pallas_call(
        functools.partial(_gdn_kernel, seq=seq, ctx=ctx, unroll=unroll, group=group),
        grid=(batch, ng),
        in_specs=[
            pl.BlockSpec((seq, gl), col(COL_K), pipeline_mode=once),
            pl.BlockSpec((seq, gl), col(COL_V), pipeline_mode=once),
            pl.BlockSpec((seq, gl), col(COL_Q), pipeline_mode=once),
            pl.BlockSpec((seq, gl), col(COL_Z), pipeline_mode=once),
            pl.BlockSpec((seq, LANES), lambda b, h: (b, 0), pipeline_mode=once),
            pl.BlockSpec((ctx, gl), col(COL_K)),
            pl.BlockSpec((ctx, gl), col(COL_V)),
            pl.BlockSpec((ctx, LANES), lambda b, h: (b, 0)),
            pl.BlockSpec((SUBLANES, gl), lambda b, h: (0, h)),
            pl.BlockSpec((SUBLANES, gl), lambda b, h: (0, ng + h)),
            pl.BlockSpec((SUBLANES, gl), lambda b, h: (0, h)),
            pl.BlockSpec((SUBLANES, LANES), lambda b, h: (0, 0)),
            pl.BlockSpec((1, LANES), lambda b, h: (0, 0)),
        ],
        out_specs=pl.BlockSpec((seq, gl), lambda b, h: (b, h)),
        out_shape=jax.ShapeDtypeStruct((batch * seq, HEADS * DV), BF16),
        scratch_shapes=[
            pltpu.VMEM((total, LANES), F32),
            pltpu.VMEM((total, LANES), F32),
            pltpu.VMEM((total, LANES), F32),
            pltpu.VMEM((4, total, LANES), F32),
            pltpu.VMEM((seq + 2 * SUBLANES, LANES), F32),
            pltpu.VMEM((group, 2, nc * 2 * CHUNK, LANES), BF16),
            pltpu.VMEM((group, nc * 2 * CHUNK, LANES), BF16),
            pltpu.VMEM((group, nc * 2 * CHUNK, LANES), BF16),
            pltpu.VMEM((group, nc * 2 * CHUNK, LANES), BF16),
            pltpu.VMEM((group, 2, nc * SUBLANES, LANES), F32),
            pltpu.VMEM((group, total, LANES), F32),
        ],
        compiler_params=_params("arbitrary", "arbitrary"),
        name="gated_deltanet",
    )(proj, proj, proj, proj, ba, proj_ctx, proj_ctx, ba_ctx, taps, taps, taps_q, ab, gn.reshape(1, DV))


CONF_ROWS = 128
CONF_HALO = 16


def _conf_kernel(a_ref, g_ref, w_ref, b_ref, o_ref, pad_scr, *, seq):
    cb = a_ref.shape[1]
    u = a_ref[...].astype(F32) * jax.nn.sigmoid(g_ref[...].astype(F32))
    pad_scr[0:CONF_HALO, :] = jnp.zeros((CONF_HALO, cb), F32)
    pad_scr[CONF_HALO:CONF_HALO + seq, :] = u
    pad_scr[CONF_HALO + seq:2 * CONF_HALO + seq, :] = jnp.zeros((CONF_HALO, cb), F32)
    win = CONF_ROWS + 2 * CONF_HALO
    first = CONF_HALO - CONF_KERNEL // 2

    def body(i, carry):
        r0 = pl.multiple_of(i * CONF_ROWS, CONF_ROWS)
        w = pad_scr[pl.ds(r0, win), :]
        acc = jnp.zeros((CONF_ROWS, cb), F32) + b_ref[...]
        for sub in range(SUBLANES):
            shifted = w if sub == 0 else pltpu.roll(w, win - sub, axis=0)
            for j in range(CONF_KERNEL):
                off = first + j
                if off % SUBLANES == sub:
                    base = off - sub
                    acc = acc + shifted[base:base + CONF_ROWS, :] * w_ref[j:j + 1, :]
        o_ref[pl.ds(r0, CONF_ROWS), :] = acc.astype(o_ref.dtype)
        return carry

    lax.fori_loop(0, seq // CONF_ROWS, body, 0)


def conformer_conv(proj, conf_dw, conf_dw_b, *, batch, seq, cb=256):
    c = conf_dw.shape[1]
    nb = c // cb
    taps = jnp.zeros((32, c), F32).at[:CONF_KERNEL].set(conf_dw)
    return pl.pallas_call(
        functools.partial(_conf_kernel, seq=seq),
        grid=(batch, nb),
        in_specs=[pl.BlockSpec((seq, cb), lambda b, j: (b, COL_GLU_A * nb + j)),
                  pl.BlockSpec((seq, cb), lambda b, j: (b, COL_GLU_G * nb + j)),
                  pl.BlockSpec((32, cb), lambda b, j: (0, j)),
                  pl.BlockSpec((1, cb), lambda b, j: (0, j))],
        out_specs=pl.BlockSpec((seq, cb), lambda b, j: (b, j)),
        out_shape=jax.ShapeDtypeStruct((batch * seq, c), BF16),
        scratch_shapes=[pltpu.VMEM((seq + 2 * CONF_HALO, cb), F32)],
        compiler_params=_params("arbitrary", "arbitrary"),
        name="conformer_conv",
    )(proj, proj, taps, conf_dw_b.reshape(1, c))


def _mixer_out_kernel(og_ref, uc_ref, ga_ref, gb_ref, x_ref, pos_ref, gm_ref, shf_ref, scf_ref,
                      gpost_ref, gpre_ref, lng_ref, lnb_ref, wa_ref, wb_ref, wo_ref, wr_ref, br_ref,
                      x2_ref, h2_ref, ti_ref, tw_ref, cnt_ref, cnt_scr, *, sub):
    parts = [slice(r, r + sub) for r in range(0, og_ref.shape[0], sub)]

    def rms(v):
        return v * lax.rsqrt(jnp.mean(v * v, axis=-1, keepdims=True) + EPS)

    def layer_norm_silu(uc):
        mu = jnp.mean(uc, axis=-1, keepdims=True)
        var = jnp.mean(jnp.square(uc - mu), axis=-1, keepdims=True)
        return _silu((uc - mu) * lax.rsqrt(var + EPS) * lng_ref[...] + lnb_ref[...]).astype(BF16)

    us = [layer_norm_silu(uc_ref[p, :].astype(F32)) for p in parts]
    yas = [jnp.dot(og_ref[p, :], wa_ref[...], preferred_element_type=F32) for p in parts]
    ybs = [jnp.dot(u, wb_ref[...], preferred_element_type=F32) for u in us]
    merged = [(jax.nn.sigmoid(ga_ref[p, :].astype(F32)) * ya
               + jax.nn.sigmoid(gb_ref[p, :].astype(F32)) * yb).astype(BF16) for p, ya, yb in zip(parts, yas, ybs)]
    ys = [jnp.dot(m, wo_ref[...], preferred_element_type=F32) for m in merged]
    x2s = [x_ref[p, :] + pos_ref[p, :] + gm_ref[0] * (rms(y) * gpost_ref[...]) for p, y in zip(parts, ys)]
    h2s = [(rms(x2) * gpre_ref[...] * (1.0 + scf_ref[0]) + shf_ref[0]).astype(BF16) for x2 in x2s]
    for p, x2, h2 in zip(parts, x2s, h2s):
        x2_ref[p, :] = x2
        h2_ref[p, :] = h2
    all_logits = [jnp.dot(h2, wr_ref[...], preferred_element_type=F32) + br_ref[...] for h2 in h2s]
    lane = lax.broadcasted_iota(jnp.int32, (sub, LANES), 1).astype(F32)
    ri = lax.broadcasted_iota(jnp.int32, (sub, sub), 0)
    ci = lax.broadcasted_iota(jnp.int32, (sub, sub), 1)
    before = jnp.where(ri > ci, 1.0, 0.0).astype(BF16)

    @pl.when(pl.program_id(0) == 0)
    def _():
        cnt_scr[...] = jnp.zeros_like(cnt_scr)

    routed = []
    for logits in all_logits:
        live = jnp.where(lane < N_EXPERTS, logits, -jnp.inf)
        top_v, top_i = [], []
        for _ in range(TOP_K):
            m = jnp.max(live, axis=-1, keepdims=True)
            idx = jnp.min(jnp.where(live == m, lane, float(LANES)), axis=-1, keepdims=True)
            top_v.append(m)
            top_i.append(idx)
            live = jnp.where(lane == idx, -jnp.inf, live)
        ex = [jnp.exp(v - top_v[0]) for v in top_v]
        denom = ex[0] + ex[1] + ex[2] + ex[3]
        picked = jnp.zeros((sub, LANES), F32)
        for k in range(TOP_K):
            picked = jnp.where(lane == top_i[k], 1.0, picked)
        routed.append((top_i, [e / denom for e in ex], picked,
                       jnp.dot(before, picked.astype(BF16), preferred_element_type=F32)))

    count = cnt_scr[...]
    for j, (p, (top_i, top_w, picked, within)) in enumerate(zip(parts, routed)):
        prefix = within + count[0:1, :]
        ti = jnp.zeros((sub, LANES), F32)
        tw = jnp.zeros((sub, LANES), F32)
        for k in range(TOP_K):
            rank = jnp.sum(jnp.where(lane == top_i[k], prefix, 0.0), axis=-1, keepdims=True)
            ti = jnp.where(lane == k, top_i[k], ti)
            ti = jnp.where(lane == TOP_K + k, rank, ti)
            tw = jnp.where(lane == k, top_w[k], tw)
        ti_ref[:, p] = jnp.transpose(ti)[:SUBLANES, :].astype(jnp.int32)
        tw_ref[:, p] = jnp.transpose(tw)[:SUBLANES, :]
        count = count + jnp.sum(picked, axis=0, keepdims=True)
        cnt_ref[j] = count
    cnt_scr[...] = count


def mixer_output(og, uc, proj, x, pos, mod, mod_row_of_tile, g_post, g_pre_f, ln_g, ln_b,
                 w_a, w_b, w_o, w_r, b_r, *, tm, sub):
    n, d = x.shape
    pos_tiles = pos.shape[0] // tm
    row = lambda i: (i, 0)
    fixed = lambda i: (0, 0)
    modspec = lambda k: pl.BlockSpec((1, 1, d), lambda i: (mod_row_of_tile(i), 0, k))
    vec = pl.BlockSpec((1, d), fixed)
    mat = pl.BlockSpec((d, d), fixed)
    return pl.pallas_call(
        functools.partial(_mixer_out_kernel, sub=sub),
        grid=(n // tm,),
        in_specs=[pl.BlockSpec((tm, d), row), pl.BlockSpec((tm, d), row),
                  pl.BlockSpec((tm, d), lambda i: (i, COL_GATE_A)),
                  pl.BlockSpec((tm, d), lambda i: (i, COL_GATE_B)),
                  pl.BlockSpec((tm, d), row),
                  pl.BlockSpec((tm, d), lambda i: (i % pos_tiles, 0)),
                  modspec(2), modspec(3), modspec(4),
                  vec, vec, vec, vec, mat, mat, mat,
                  pl.BlockSpec((d, LANES), fixed), pl.BlockSpec((1, LANES), fixed)],
        out_specs=[pl.BlockSpec((tm, d), row), pl.BlockSpec((tm, d), row),
                   pl.BlockSpec((SUBLANES, tm), lambda i: (0, i)), pl.BlockSpec((SUBLANES, tm), lambda i: (0, i)),
                   pl.BlockSpec((tm // sub, SUBLANES, LANES), lambda i: (i, 0, 0))],
        out_shape=[jax.ShapeDtypeStruct((n, d), F32), jax.ShapeDtypeStruct((n, d), BF16),
                   jax.ShapeDtypeStruct((SUBLANES, n), jnp.int32), jax.ShapeDtypeStruct((SUBLANES, n), F32),
                   jax.ShapeDtypeStruct((n // sub, SUBLANES, LANES), F32)],
        scratch_shapes=[pltpu.VMEM((SUBLANES, LANES), F32)],
        compiler_params=_params("arbitrary"),
        name="mixer_output",
    )(og, uc, proj, proj, x, pos, mod, mod, mod,
      g_post.reshape(1, d), g_pre_f.reshape(1, d), ln_g.reshape(1, d), ln_b.reshape(1, d),
      w_a, w_b, w_o, w_r, b_r)


EXPERT_BLOCK = 256
MOE_CHUNK = 512
GATHER_SUB = 256
GATHER_GROUP = 3
COMBINE_GRANULE = 32
COMBINE_SLOTS = 32


def _expert_kernel(bexp_ref, bnext_ref, rlo_ref, slo_ref, shi_ref, nused_ref,
                   h2_ref, ti_ref, wt_ref, wgu_hbm, bgu_ref, wd_hbm, bd_ref, o_ref,
                   acc, rw_acc, wgu_bf, wd_bf, wgu_f32, wd_f32, wsem):
    b = pl.program_id(0)
    expert = bexp_ref[b]

    def fetch(e):
        return (pltpu.make_async_copy(wgu_hbm.at[e], wgu_f32, wsem.at[0]),
                pltpu.make_async_copy(wd_hbm.at[e], wd_f32, wsem.at[1]))

    @pl.when(((b == 0) | (expert != bexp_ref[jnp.maximum(b - 1, 0)])) & (b < nused_ref[0]))
    def _():
        @pl.when(b == 0)
        def _():
            for cp in fetch(expert):
                cp.start()

        for cp in fetch(expert):
            cp.wait()
        wgu_bf[...] = wgu_f32[...].astype(BF16)
        wd_bf[...] = wd_f32[...].astype(BF16)

        @pl.when(bnext_ref[b] >= 0)
        def _():
            for cp in fetch(bnext_ref[b]):
                cp.start()

    @pl.when(b < nused_ref[0])
    def _():
        acc[...] = jnp.zeros_like(acc)
        rw_acc[...] = jnp.zeros_like(rw_acc)
        row = lax.broadcasted_iota(jnp.int32, (EXPERT_BLOCK, GATHER_SUB), 0)
        s_lo, s_hi = slo_ref[b], shi_ref[b]
        last_sub = h2_ref.shape[0] // GATHER_SUB - 1

        def select(s, live):
            t0 = pl.multiple_of(s * GATHER_SUB, GATHER_SUB)
            ids = ti_ref[0:TOP_K, pl.ds(t0, GATHER_SUB)]
            local = ti_ref[TOP_K:2 * TOP_K, pl.ds(t0, GATHER_SUB)] - rlo_ref[b]
            if live is not True:
                local = jnp.where(live, local, -1)
            inside = (ids == expert) & (local >= 0) & (local < EXPERT_BLOCK)
            hit_row = jnp.sum(jnp.where(inside, local + 1, 0), axis=0, keepdims=True) - 1
            hit_w = jnp.sum(jnp.where(inside, wt_ref[0:TOP_K, pl.ds(t0, GATHER_SUB)], 0.0), axis=0, keepdims=True)
            hit = row == hit_row
            return (jnp.where(hit, 1.0, 0.0).astype(BF16), jnp.where(hit, hit_w, 0.0),
                    h2_ref[pl.ds(t0, GATHER_SUB), :])

        def gather(i, carry):
            s0 = s_lo + GATHER_GROUP * i
            parts = [select(s0, True)] + [select(jnp.minimum(s0 + q, last_sub), s0 + q <= s_hi)
                                          for q in range(1, GATHER_GROUP)]
            acc[...] += jnp.dot(jnp.concatenate([p for p, _, _ in parts], axis=1),
                                jnp.concatenate([x for _, _, x in parts], axis=0), preferred_element_type=F32)
            rw_acc[...] += jnp.sum(functools.reduce(lambda a, c: a + c, [w for _, w, _ in parts]), axis=1,
                                   keepdims=True)
            return carry

        lax.fori_loop(0, (s_hi - s_lo + GATHER_GROUP) // GATHER_GROUP, gather, 0)
        gu = jnp.dot(acc[...].astype(BF16), wgu_bf[...], preferred_element_type=F32) + bgu_ref[0]
        dff = gu.shape[1] // 2
        gl = jnp.minimum(gu[:, :dff], SWIGLU_LIMIT)
        lin = jnp.clip(gu[:, dff:], -SWIGLU_LIMIT, SWIGLU_LIMIT)
        act = (gl * jax.nn.sigmoid(SWIGLU_ALPHA * gl) * (lin + 1.0)).astype(BF16)
        y = jnp.dot(act, wd_bf[...], preferred_element_type=F32) + bd_ref[0]
        o_ref[...] = (y * rw_acc[...]).astype(o_ref.dtype)

    @pl.when(b >= nused_ref[0])
    def _():
        o_ref[...] = jnp.zeros_like(o_ref)


def expert_blocks(h2, ti_t, wt_t, block_expert, next_expert, rank_lo, sub_lo, sub_hi, n_used, w_gate_up, b_gate_up,
                  w_down, b_down):
    n, d = h2.shape
    e, _, f2 = w_gate_up.shape
    n_blocks = block_expert.shape[0]
    whole = lambda b, be, bn, rl, lo, hi, nu: (0, 0)
    by_expert = lambda b, be, bn, rl, lo, hi, nu: (be[b], 0, 0)
    once = pl.Buffered(1)
    grid_spec = pltpu.PrefetchScalarGridSpec(
        num_scalar_prefetch=6,
        grid=(n_blocks,),
        in_specs=[pl.BlockSpec((n, d), whole, pipeline_mode=once),
                  pl.BlockSpec((SUBLANES, n), whole, pipeline_mode=once),
                  pl.BlockSpec((SUBLANES, n), whole, pipeline_mode=once),
                  pl.BlockSpec(memory_space=pl.ANY),
                  pl.BlockSpec((1, 1, f2), by_expert),
                  pl.BlockSpec(memory_space=pl.ANY),
                  pl.BlockSpec((1, 1, d), by_expert)],
        out_specs=pl.BlockSpec((EXPERT_BLOCK, d), lambda b, be, bn, rl, lo, hi, nu: (b, 0)),
        scratch_shapes=[pltpu.VMEM((EXPERT_BLOCK, d), F32), pltpu.VMEM((EXPERT_BLOCK, 1), F32),
                        pltpu.VMEM((d, f2), BF16), pltpu.VMEM((f2 // 2, d), BF16),
                        pltpu.VMEM((d, f2), F32), pltpu.VMEM((f2 // 2, d), F32),
                        pltpu.SemaphoreType.DMA((2,))],
    )
    return pl.pallas_call(
        _expert_kernel,
        grid_spec=grid_spec,
        out_shape=jax.ShapeDtypeStruct((n_blocks * EXPERT_BLOCK, d), BF16),
        compiler_params=_params("arbitrary"),
        name="expert_blocks",
    )(block_expert, next_expert, rank_lo, sub_lo, sub_hi, n_used, h2, ti_t, wt_t, w_gate_up, b_gate_up.reshape(e, 1, f2), w_down,
      b_down.reshape(e, 1, d))


def _combine_kernel(coff_ref, cgran_ref, cexp_ref, cbase_ref, yb_hbm, ti_ref, x2_ref, gf_ref, g_ref, o_ref,
                    slab, sem, acc):
    c = pl.program_id(0)
    start = coff_ref[c]
    count = coff_ref[c + 1] - start
    n_groups = (count + COMBINE_SLOTS - 1) // COMBINE_SLOTS

    def item(g, s):
        return start + jnp.minimum(g * COMBINE_SLOTS + s, count - 1)

    def granule_of(g, s):
        return cgran_ref[item(g, s)]

    def copies(g, buf):
        return [pltpu.make_async_copy(
            yb_hbm.at[pl.ds(pl.multiple_of(granule_of(g, s) * COMBINE_GRANULE, COMBINE_GRANULE), COMBINE_GRANULE), :],
            slab.at[buf, pl.ds(s * COMBINE_GRANULE, COMBINE_GRANULE), :], sem.at[buf])
            for s in range(COMBINE_SLOTS)]

    for cp in copies(0, 0):
        cp.start()
    acc[...] = jnp.zeros_like(acc)
    chunk = x2_ref.shape[0]
    row = lax.broadcasted_iota(jnp.int32, (COMBINE_GRANULE, chunk), 0)
    ids = ti_ref[0:TOP_K, :]
    ranks = ti_ref[TOP_K:2 * TOP_K, :]

    def group(g, carry):
        buf = g % 2

        @pl.when(g + 1 < n_groups)
        def _():
            for cp in copies(g + 1, 1 - buf):
                cp.start()

        for cp in copies(g, buf):
            cp.wait()
        pieces = []
        for s in range(COMBINE_SLOTS):
            live = g * COMBINE_SLOTS + s < count
            local = ranks - cbase_ref[item(g, s)]
            inside = (ids == jnp.where(live, cexp_ref[item(g, s)], -1)) & (local >= 0) & (local < COMBINE_GRANULE)
            hit_row = jnp.sum(jnp.where(inside, local + 1, 0), axis=0, keepdims=True) - 1
            pieces.append(jnp.where(row == hit_row, 1.0, 0.0).astype(BF16))
        onehot_t = jnp.concatenate(pieces, axis=0)
        acc[...] += lax.dot_general(onehot_t, slab[buf], (((0,), (0,)), ((), ())),
                                    preferred_element_type=F32)
        return carry

    lax.fori_loop(0, n_groups, group, 0)
    y = acc[...]
    yn = y * lax.rsqrt(jnp.mean(y * y, axis=-1, keepdims=True) + EPS) * g_ref[...]
    o_ref[...] = x2_ref[...] + gf_ref[0] * yn


def combine_residual(yb, ti_t, x2, chunk_off, chunk_items, mod, mod_row_of_chunk, g_post_f):
    n, d = x2.shape
    chunk_row = lambda c, off, gran, exp, base: (c, 0)
    grid_spec = pltpu.PrefetchScalarGridSpec(
        num_scalar_prefetch=4,
        grid=(n // MOE_CHUNK,),
        in_specs=[pl.BlockSpec(memory_space=pl.ANY),
                  pl.BlockSpec((SUBLANES, MOE_CHUNK), lambda c, off, gran, exp, base: (0, c)),
                  pl.BlockSpec((MOE_CHUNK, d), chunk_row),
                  pl.BlockSpec((1, 1, d), lambda c, off, gran, exp, base: (mod_row_of_chunk(c), 0, 5)),
                  pl.BlockSpec((1, d), lambda c, off, gran, exp, base: (0, 0))],
        out_specs=pl.BlockSpec((MOE_CHUNK, d), chunk_row),
        scratch_shapes=[pltpu.VMEM((2, COMBINE_SLOTS * COMBINE_GRANULE, d), BF16),
                        pltpu.SemaphoreType.DMA((2,)),
                        pltpu.VMEM((MOE_CHUNK, d), F32)],
    )
    return pl.pallas_call(
        _combine_kernel,
        grid_spec=grid_spec,
        out_shape=jax.ShapeDtypeStruct((n, d), F32),
        compiler_params=_params("arbitrary"),
        name="combine_residual",
    )(chunk_off, *chunk_items, yb, ti_t, x2, mod, g_post_f.reshape(1, d))


def _grid_pos_embedding(rows, d):
    quarter = d // 4
    omega = POS_BASE ** (-jnp.arange(quarter, dtype=F32) / quarter)

    def emb(p):
        ang = p[:, None] * omega[None, :]
        return jnp.concatenate([jnp.sin(ang), jnp.cos(ang)], axis=-1)

    row_emb = jnp.repeat(emb(jnp.arange(rows, dtype=F32)), GRID_W, axis=0)
    col_emb = jnp.tile(emb(jnp.arange(GRID_W, dtype=F32)), (rows, 1))
    return jnp.concatenate([row_emb, col_emb], axis=-1)


def _count_le(sorted_vals, queries):
    return jnp.sum(sorted_vals <= queries[:, None], axis=1).astype(jnp.int32)


def _moe_plan(cnt_after, n_tok, tm):
    assert tm == GATHER_SUB
    sub_after = cnt_after[:, 0, :N_EXPERTS].astype(jnp.int32)
    counts = sub_after[-1]
    padded = (counts + EXPERT_BLOCK - 1) // EXPERT_BLOCK * EXPERT_BLOCK
    padded_end = jnp.cumsum(padded)
    padded_start = padded_end - padded
    n_blocks = n_tok * TOP_K // EXPERT_BLOCK + N_EXPERTS
    blocks = jnp.arange(n_blocks, dtype=jnp.int32)
    block_expert = jnp.minimum(_count_le(padded_end[None, :], blocks * EXPERT_BLOCK), N_EXPERTS - 1)
    n_used = (padded_end[-1:] // EXPERT_BLOCK).astype(jnp.int32)
    of_block = (block_expert[:, None] == jnp.arange(N_EXPERTS, dtype=jnp.int32)[None, :]).astype(jnp.int32)
    r_lo = (blocks - jnp.sum(of_block * padded_start[None, :], axis=1) // EXPERT_BLOCK) * EXPERT_BLOCK
    r_hi = jnp.minimum(jnp.sum(of_block * counts[None, :], axis=1), r_lo + EXPERT_BLOCK) - 1
    through = jnp.sum(of_block[:, :, None] * sub_after.T[None, :, :], axis=1)
    after = jnp.sum(of_block * padded_end[None, :], axis=1) // EXPERT_BLOCK
    next_expert = jnp.sum((after[:, None] == blocks[None, :]).astype(jnp.int32) * block_expert[None, :], axis=1)
    next_expert = jnp.where(after < n_used[0], next_expert, -1).astype(jnp.int32)
    n_sub = sub_after.shape[0]
    sub_lo = jnp.minimum(_count_le(through, r_lo), n_sub - 1)
    sub_hi = jnp.minimum(_count_le(through, r_hi), n_sub - 1)
    nch = n_tok // MOE_CHUNK
    per_chunk = MOE_CHUNK // tm
    cb_after = sub_after[per_chunk - 1::per_chunk]
    cb_before = jnp.concatenate([jnp.zeros((1, N_EXPERTS), jnp.int32), cb_after[:-1]], axis=0)
    b_lo = ((padded_start[None, :] + cb_before) // COMBINE_GRANULE).reshape(-1)
    b_hi = ((padded_start[None, :] + cb_after - 1) // COMBINE_GRANULE).reshape(-1)
    n_it = jnp.where((cb_after > cb_before).reshape(-1), b_hi - b_lo + 1, 0)
    off_end = jnp.cumsum(n_it)
    off = off_end - n_it
    n_items = n_blocks * (EXPERT_BLOCK // COMBINE_GRANULE) + N_EXPERTS * (nch - 1)
    w = jnp.minimum(jnp.arange(n_items, dtype=jnp.int32), off_end[-1] - 1)
    chunk_end = off_end[N_EXPERTS - 1::N_EXPERTS]
    chunk_off = jnp.concatenate([jnp.zeros((1,), jnp.int32), chunk_end]).astype(jnp.int32)
    chunk_of = jnp.minimum(_count_le(chunk_end[None, :], w), nch - 1)
    chunk_expert = jnp.minimum(_count_le(off_end.reshape(nch, N_EXPERTS)[chunk_of], w), N_EXPERTS - 1)
    cell = chunk_of * N_EXPERTS + chunk_expert
    chunk_granules = (b_lo[cell] + w - off[cell]).astype(jnp.int32)
    chunk_base = chunk_granules * COMBINE_GRANULE - padded_start[chunk_expert]
    return block_expert, next_expert, r_lo, sub_lo, sub_hi, n_used, chunk_off, (chunk_granules, chunk_expert, chunk_base)


def kernel(x, c, ctx, c_ctx, w_mod, b_mod, g_pre_mix, g_post_mix, g_pre_ffn, g_post_ffn, w_in, conv_kv,
           conv_q, a_log, dt_bias, gdn_norm_g, w_proj_a, conf_dw, conf_dw_b, conf_ln_g, conf_ln_b,
           w_proj_b, w_out, w_router, b_router, w_gate_up, b_gate_up, w_down, b_down):
    batch, seq, d = x.shape
    ctx_len = ctx.shape[1]
    n_tok = batch * seq
    gw = HEADS * DV
    beta_off = HEADS * DK + gw
    state_cols = beta_off + 4 * HEADS
    q_off = state_cols

    mod_rows = 2 * SUBLANES
    cond = jnp.zeros((mod_rows, d), F32).at[:batch].set(c).at[batch].set(c_ctx)
    mod = modulation(cond, w_mod[0], b_mod[0]).reshape(mod_rows, 1, 6 * d)

    w_main = projection_weights(w_in, keep=beta_off, skip=state_cols - beta_off)
    w_ba = jnp.zeros((d, LANES), F32).at[:, :4 * HEADS].set(w_in[0, :, beta_off:state_cols]).astype(BF16)

    pos = _grid_pos_embedding(seq // GRID_W, d)
    x_flat = x.reshape(n_tok, d)
    tm = 1024
    tiles_per_seq = seq // tm
    proj, ba = input_projection(x_flat, pos, mod, lambda i: i // tiles_per_seq, g_pre_mix[0],
                                w_main, w_ba, tm=tm, tn=2048)
    proj_ctx, ba_ctx = input_projection(ctx.reshape(batch * ctx_len, d), None, mod, lambda i: batch,
                                        g_pre_mix[0], w_main[:, :beta_off], w_ba, tm=ctx_len, tn=1024)

    og = gated_deltanet(proj, ba, proj_ctx, ba_ctx, conv_kv[0], conv_q[0], a_log[0], dt_bias[0],
                        gdn_norm_g[0], batch=batch, seq=seq, ctx=ctx_len)
    uc = conformer_conv(proj, conf_dw[0], conf_dw_b[0], batch=batch, seq=seq)

    tm2 = 512
    w_r = jnp.zeros((d, LANES), F32).at[:, :N_EXPERTS].set(w_router[0]).astype(BF16)
    b_r = jnp.zeros((1, LANES), F32).at[0, :N_EXPERTS].set(b_router[0])
    x2, h2, top_i, top_w, cnt_after = mixer_output(
        og, uc, proj, x_flat, pos, mod, lambda i: i // (seq // tm2), g_post_mix[0], g_pre_ffn[0],
        conf_ln_g[0], conf_ln_b[0], w_proj_a[0].astype(BF16), w_proj_b[0].astype(BF16),
        w_out[0].astype(BF16), w_r, b_r, tm=tm2, sub=GATHER_SUB)

    block_expert, next_expert, rank_lo, sub_lo, sub_hi, n_used, chunk_off, chunk_items = _moe_plan(
        cnt_after, n_tok, GATHER_SUB)
    yb = expert_blocks(h2, top_i, top_w, block_expert, next_expert, rank_lo, sub_lo, sub_hi, n_used, w_gate_up[0],
                       b_gate_up[0], w_down[0], b_down[0])
    out = combine_residual(yb, top_i, x2, chunk_off, chunk_items, mod, lambda c: c // (seq // MOE_CHUNK),
                           g_post_ffn[0])
    return out.reshape(batch, seq, d)
```

```python
import functools

import jax
import jax.numpy as jnp
from jax import lax
from jax.experimental import pallas as pl
from jax.experimental.pallas import tpu as pltpu

F32 = jnp.float32
BF16 = jnp.bfloat16

D_MODEL = 1024
GRID_W = 64
HEADS = 8
DK = 128
DV = 128
SHORT_CONV = 5
CHUNK = 64
CONF_KERNEL = 31
N_EXPERTS = 32
TOP_K = 4
SWIGLU_LIMIT = 7.0
SWIGLU_ALPHA = 1.702
MOE_BLOCK = 128
EPS = 1e-6
POS_BASE = 10000.0

LANES = 128
SUBLANES = 8
VMEM_LIMIT = 60 * 1024 * 1024

COL_K, COL_V, COL_Q, COL_Z, COL_GLU_A, COL_GLU_G, COL_GATE_A, COL_GATE_B = range(8)


def _params(*sem):
    return pltpu.CompilerParams(dimension_semantics=sem, vmem_limit_bytes=VMEM_LIMIT)


def _mod_kernel(c_ref, w_ref, b_ref, o_ref):
    c = c_ref[...]
    s = c * jax.nn.sigmoid(c)
    o_ref[...] = jnp.dot(s, w_ref[...], preferred_element_type=F32,
                         precision=lax.Precision.HIGHEST) + b_ref[...]


def modulation(cond, w_mod, b_mod):
    r, d = cond.shape
    n = w_mod.shape[1]
    tn = 1024
    return pl.pallas_call(
        _mod_kernel,
        grid=(n // tn,),
        in_specs=[pl.BlockSpec((r, d), lambda j: (0, 0)),
                  pl.BlockSpec((d, tn), lambda j: (0, j)),
                  pl.BlockSpec((1, tn), lambda j: (0, j))],
        out_specs=pl.BlockSpec((r, tn), lambda j: (0, j)),
        out_shape=jax.ShapeDtypeStruct((r, n), F32),
        compiler_params=_params("arbitrary"),
        name="modulation",
    )(cond, w_mod, b_mod.reshape(1, n))


def _proj_weights_kernel(a_ref, b_ref, o_ref, *, first_shifted, shift):
    @pl.when(pl.program_id(0) < first_shifted)
    def _():
        o_ref[...] = a_ref[...].astype(o_ref.dtype)

    @pl.when(pl.program_id(0) >= first_shifted)
    def _():
        o_ref[...] = jnp.concatenate([a_ref[shift:, :], b_ref[...]], axis=0).astype(o_ref.dtype)


def projection_weights(w_t, *, keep, skip, tr=1024):
    total, d = w_t.shape
    n_out = total - skip
    assert keep % tr == 0 and n_out % tr == 0 and tr % skip == 0 and skip % SUBLANES == 0
    return pl.pallas_call(
        functools.partial(_proj_weights_kernel, first_shifted=keep // tr, shift=skip),
        grid=(n_out // tr,),
        in_specs=[pl.BlockSpec((tr, d), lambda j: (j, 0)),
                  pl.BlockSpec((skip, d), lambda j: ((j + 1) * (tr // skip), 0))],
        out_specs=pl.BlockSpec((tr, d), lambda j: (j, 0)),
        out_shape=jax.ShapeDtypeStruct((n_out, d), BF16),
        compiler_params=_params("arbitrary"),
        name="projection_weights",
    )(w_t, w_t)


_NT = (((1,), (1,)), ((), ()))


def _inproj_kernel(*refs, has_pos):
    if has_pos:
        x_ref, pos_ref, sh_ref, sc_ref, g_ref, w_ref, wba_ref, o_ref, ba_ref, h_scr = refs
    else:
        x_ref, sh_ref, sc_ref, g_ref, w_ref, wba_ref, o_ref, ba_ref, h_scr = refs

    @pl.when(pl.program_id(1) == 0)
    def _():
        x = x_ref[...]
        if has_pos:
            x = x + pos_ref[...]
        y = x * lax.rsqrt(jnp.mean(x * x, axis=-1, keepdims=True) + EPS) * g_ref[...]
        h = (y * (1.0 + sc_ref[0]) + sh_ref[0]).astype(BF16)
        h_scr[...] = h
        ba_ref[...] = lax.dot_general(h, wba_ref[...], _NT, preferred_element_type=F32)

    o_ref[...] = lax.dot_general(h_scr[...], w_ref[...], _NT, preferred_element_type=F32).astype(o_ref.dtype)


def input_projection(x, pos, mod, mod_row_of_tile, g_pre, w_main, w_ba, *, tm, tn):
    n, d = x.shape
    w = w_main.shape[0]
    has_pos = pos is not None
    in_specs = [pl.BlockSpec((tm, d), lambda i, j: (i, 0))]
    args = [x]
    if has_pos:
        pos_tiles = pos.shape[0] // tm
        in_specs.append(pl.BlockSpec((tm, d), lambda i, j: (i % pos_tiles, 0)))
        args.append(pos)
    in_specs += [
        pl.BlockSpec((1, 1, d), lambda i, j: (mod_row_of_tile(i), 0, 0)),
        pl.BlockSpec((1, 1, d), lambda i, j: (mod_row_of_tile(i), 0, 1)),
        pl.BlockSpec((1, d), lambda i, j: (0, 0)),
        pl.BlockSpec((tn, d), lambda i, j: (j, 0)),
        pl.BlockSpec((LANES, d), lambda i, j: (0, 0)),
    ]
    args += [mod, mod, g_pre.reshape(1, d), w_main, w_ba]
    return pl.pallas_call(
        functools.partial(_inproj_kernel, has_pos=has_pos),
        grid=(n // tm, w // tn),
        in_specs=in_specs,
        out_specs=[pl.BlockSpec((tm, tn), lambda i, j: (i, j)),
                   pl.BlockSpec((tm, LANES), lambda i, j: (i, 0))],
        out_shape=[jax.ShapeDtypeStruct((n, w), BF16), jax.ShapeDtypeStruct((n, LANES), F32)],
        scratch_shapes=[pltpu.VMEM((tm, d), BF16)],
        compiler_params=_params("arbitrary", "arbitrary"),
        name="input_projection_pos" if has_pos else "input_projection_ctx",
    )(*args)


def _silu(x):
    return x * jax.nn.sigmoid(x)


def _softplus(x):
    return jnp.maximum(x, 0.0) + jnp.log(1.0 + jnp.exp(-jnp.abs(x)))


def _short_conv(src, taps, pad_scr, rows):
    half = SHORT_CONV // 2
    pad_scr[0:SUBLANES, :] = jnp.zeros((SUBLANES, LANES), F32)
    pad_scr[SUBLANES:SUBLANES + rows, :] = src.astype(F32)
    pad_scr[SUBLANES + rows:2 * SUBLANES + rows, :] = jnp.zeros((SUBLANES, LANES), F32)
    acc = None
    for j in range(SHORT_CONV):
        off = SUBLANES + j - half
        term = pad_scr[off:off + rows, :] * taps[j:j + 1, :]
        acc = term if acc is None else acc + term
    return _silu(acc)


def _l2n(x):
    return x * lax.rsqrt(jnp.sum(x * x, axis=-1, keepdims=True) + EPS)


def _gdn_kernel(k_ref, v_ref, q_ref, z_ref, ba_ref, kc_ref, vc_ref, bac_ref,
                wk_ref, wv_ref, wq_ref, ab_ref, gn_ref, o_ref,
                kf, vf, qf, gates, pad_scr, lhs, val, att, ktt, dec, osc, *, seq, ctx, unroll, group):
    total = ctx + seq
    nc = total // CHUNK
    ncc = ctx // CHUNK
    two = 2 * CHUNK
    lg_chunk = CHUNK.bit_length() - 1

    ri = lax.broadcasted_iota(jnp.int32, (two, two), 0)
    ci = lax.broadcasted_iota(jnp.int32, (two, two), 1)
    same_dir = (ri >> lg_chunk) == (ci >> lg_chunk)
    incl = same_dir & (((ri < CHUNK) & (ri >= ci)) | ((ri >= CHUNK) & (ri <= ci)))
    strict = incl & (ri != ci)
    eye = jnp.where(ri == ci, 1.0, 0.0)
    tri = jnp.where(incl, 1.0, 0.0).astype(BF16)
    level_masks = []
    for lg in range(lg_chunk):
        same_parent = (ri >> (lg + 1)) == (ci >> (lg + 1))
        level_masks.append(jnp.where(same_parent & ((ri >> lg) != (ci >> lg)), 1.0, 0.0))
    top_rows = lax.broadcasted_iota(jnp.int32, (two, LANES), 0) < CHUNK

    def bwd_chunk(t):
        return jnp.where(t < ncc, ncc - 1 - t, nc + ncc - 1 - t)

    def stacked(ref, rf, rb):
        return jnp.concatenate([ref[pl.ds(rf, CHUNK), :], ref[pl.ds(rb, CHUNK), :]], axis=0)

    def chunk_load(t):
        rf = pl.multiple_of(t * CHUNK, CHUNK)
        rb = pl.multiple_of(bwd_chunk(t) * CHUNK, CHUNK)
        beta = jnp.concatenate([gates[0, pl.ds(rf, CHUNK), :], gates[1, pl.ds(rb, CHUNK), :]], axis=0)
        g = jnp.concatenate([gates[2, pl.ds(rf, CHUNK), :], gates[3, pl.ds(rb, CHUNK), :]], axis=0)
        return stacked(kf, rf, rb), stacked(vf, rf, rb), stacked(qf, rf, rb), beta, g

    def chunk_triangle(k, q, beta, g):
        kb = k.astype(BF16)
        kq = jnp.concatenate([kb, q.astype(BF16)], axis=0)
        kk_qk = lax.dot_general(kq, kb, (((1,), (1,)), ((), ())), preferred_element_type=F32)
        g_hi = g.astype(BF16)
        r1 = g - g_hi.astype(F32)
        g_mid = r1.astype(BF16)
        g_lo = (r1 - g_mid.astype(F32)).astype(BF16)
        gc3 = jnp.dot(tri, jnp.concatenate([g_hi, g_mid, g_lo], axis=1), preferred_element_type=F32)
        gc = gc3[:, :LANES] + gc3[:, LANES:2 * LANES] + gc3[:, 2 * LANES:]
        gc_row = jnp.transpose(gc)
        decay = jnp.where(incl, jnp.exp(gc - gc_row), 0.0)
        a = jnp.where(strict, beta * kk_qk[:two] * decay, 0.0)
        return a, (kk_qk[two:] * decay).astype(BF16), gc

    def chunk_finish(hh, t, k, v, q, beta, gc, attn, tmat):
        eg = jnp.exp(gc)
        rhs = jnp.concatenate([v * beta, k * beta * eg], axis=1).astype(BF16)
        sol = jnp.dot(tmat.astype(BF16), rhs, preferred_element_type=F32)
        g_end = jnp.where(top_rows, jnp.broadcast_to(gc[CHUNK - 1:CHUNK, :], (two, LANES)),
                          jnp.broadcast_to(gc[CHUNK:CHUNK + 1, :], (two, LANES)))
        k_tail = k * jnp.exp(g_end - gc)
        qg = (q * eg).astype(BF16)
        kcum = sol[:, DV:].astype(BF16)
        r0 = pl.multiple_of(t * two, two)
        lhs[hh, 0, pl.ds(r0, two), :] = jnp.concatenate([kcum[:CHUNK], qg[:CHUNK]], axis=0)
        lhs[hh, 1, pl.ds(r0, two), :] = jnp.concatenate([kcum[CHUNK:], qg[CHUNK:]], axis=0)
        val[hh, pl.ds(r0, two), :] = sol[:, :DV].astype(BF16)
        att[hh, pl.ds(r0, two), :] = attn
        ktt[hh, pl.ds(r0, two), :] = jnp.transpose(k_tail).astype(BF16)
        d0 = pl.multiple_of(t * SUBLANES, SUBLANES)
        e_end = jnp.exp(g_end)
        dec[hh, 0, pl.ds(d0, SUBLANES), :] = e_end[:SUBLANES]
        dec[hh, 1, pl.ds(d0, SUBLANES), :] = e_end[CHUNK:CHUNK + SUBLANES]

    for hh in range(group):
        head = pl.program_id(1) * group + hh
        cols = slice(hh * LANES, (hh + 1) * LANES)

        kf[0:ctx, :] = _l2n(_short_conv(kc_ref[:, cols], wk_ref[:, cols], pad_scr, ctx))
        vf[0:ctx, :] = _short_conv(vc_ref[:, cols], wv_ref[:, cols], pad_scr, ctx)
        qf[0:ctx, :] = jnp.zeros((ctx, LANES), F32)
        kf[ctx:total, :] = _l2n(_short_conv(k_ref[:, cols], wk_ref[:, cols], pad_scr, seq))
        vf[ctx:total, :] = _short_conv(v_ref[:, cols], wv_ref[:, cols], pad_scr, seq)
        qf[ctx:total, :] = _l2n(_short_conv(q_ref[:, cols], wq_ref[:, cols], pad_scr, seq)) * (DK ** -0.5)

        def gate_cols(src_ref, lo, rows):
            x = src_ref[...]
            lane = lax.broadcasted_iota(jnp.int32, (rows, LANES), 1)
            beta = jax.nn.sigmoid(x)
            g = -jnp.exp(ab_ref[0:1, :]) * _softplus(x + ab_ref[1:2, :])
            for slot, (arr, base) in enumerate(((beta, 0), (beta, HEADS), (g, 2 * HEADS), (g, 3 * HEADS))):
                col = jnp.sum(jnp.where(lane == base + head, arr, 0.0), axis=-1, keepdims=True)
                gates[slot, lo:lo + rows, :] = jnp.broadcast_to(col, (rows, LANES))

        gate_cols(bac_ref, 0, ctx)
        gate_cols(ba_ref, ctx, seq)

        def chunk_body(i, carry, hh=hh):
            steps = [i * unroll + u for u in range(unroll)]
            loaded = [chunk_load(t) for t in steps]
            tris = [chunk_triangle(k, q, beta, g) for (k, v, q, beta, g) in loaded]
            tmats = [eye - a * level_masks[0] for (a, _, _) in tris]
            for m in level_masks[1:]:
                nxt = []
                for (a, _, _), tmat in zip(tris, tmats):
                    cs = (a * m).astype(BF16)
                    tb = tmat.astype(BF16)
                    tc = jnp.dot(tb, cs, preferred_element_type=F32)
                    nxt.append(tmat - jnp.dot(tc.astype(BF16), tb, preferred_element_type=F32))
                tmats = nxt
            for t, (k, v, q, beta, g), (a, attn, gc), tmat in zip(steps, loaded, tris, tmats):
                chunk_finish(hh, t, k, v, q, beta, gc, attn, tmat)
            return carry

        lax.fori_loop(0, nc // unroll, chunk_body, 0)

    def scan_body(t, carry):
        rf = pl.multiple_of(t * CHUNK, CHUNK)
        rb = pl.multiple_of(bwd_chunk(t) * CHUNK, CHUNK)
        r0 = pl.multiple_of(t * two, two)
        d0 = pl.multiple_of(t * SUBLANES, SUBLANES)
        rs = [(jnp.dot(lhs[hh, 0, pl.ds(r0, two), :], carry[2 * hh].astype(BF16), preferred_element_type=F32),
               jnp.dot(lhs[hh, 1, pl.ds(r0, two), :], carry[2 * hh + 1].astype(BF16), preferred_element_type=F32))
              for hh in range(group)]
        v_news = [val[hh, pl.ds(r0, two), :].astype(F32) - jnp.concatenate([r_f[:CHUNK], r_b[:CHUNK]], axis=0)
                  for hh, (r_f, r_b) in enumerate(rs)]
        outs, new_states = [], []
        for hh, ((r_f, r_b), v_new) in enumerate(zip(rs, v_news)):
            zeros = jnp.zeros_like(v_new)
            v_bd = jnp.concatenate([jnp.where(top_rows, v_new, zeros), jnp.where(top_rows, zeros, v_new)],
                                   axis=1).astype(BF16)
            upd = jnp.dot(ktt[hh, pl.ds(r0, two), :], v_bd, preferred_element_type=F32)
            new_states.append(carry[2 * hh] * dec[hh, 0, pl.ds(d0, SUBLANES), :][0:1, :] + upd[:, :DV])
            new_states.append(carry[2 * hh + 1] * dec[hh, 1, pl.ds(d0, SUBLANES), :][0:1, :] + upd[:, DV:])
            outs.append(jnp.concatenate([r_f[CHUNK:], r_b[CHUNK:]], axis=0)
                        + jnp.dot(att[hh, pl.ds(r0, two), :], v_new.astype(BF16), preferred_element_type=F32))
        for hh, o in enumerate(outs):
            osc[hh, pl.ds(rf, CHUNK), :] += o[:CHUNK]
            osc[hh, pl.ds(rb, CHUNK), :] += o[CHUNK:]
        return tuple(new_states)

    osc[...] = jnp.zeros_like(osc)
    zero = jnp.zeros((DK, DV), F32)
    lax.fori_loop(0, nc, scan_body, (zero,) * (2 * group))

    for hh in range(group):
        cols = slice(hh * LANES, (hh + 1) * LANES)
        o = osc[hh, ctx:total, :]
        o = o * lax.rsqrt(jnp.mean(o * o, axis=-1, keepdims=True) + EPS) * gn_ref[...]
        o_ref[:, cols] = (o * _silu(z_ref[:, cols].astype(F32))).astype(o_ref.dtype)


def gated_deltanet(proj, ba, proj_ctx, ba_ctx, conv_kv, conv_q, a_log, dt_bias, gn, *, batch, seq, ctx,
                   unroll=18, group=4):
    total = seq + ctx
    nc = total // CHUNK
    gl = group * LANES
    ng = HEADS // group
    taps = jnp.zeros((SUBLANES, conv_kv.shape[1]), F32).at[:SHORT_CONV].set(conv_kv)
    taps_q = jnp.zeros((SUBLANES, conv_q.shape[1]), F32).at[:SHORT_CONV].set(conv_q)
    ab = jnp.zeros((SUBLANES, LANES), F32)
    ab = ab.at[0, 2 * HEADS:4 * HEADS].set(a_log.reshape(-1)).at[1, 2 * HEADS:4 * HEADS].set(dt_bias.reshape(-1))
    col = lambda base: (lambda b, h: (b, base * ng + h))
    once = pl.Buffered(1)
    return pl.pallas_call(
        functools.partial(_gdn_kernel, seq=seq, ctx=ctx, unroll=unroll, group=group),
        grid=(batch, ng),
        in_specs=[
            pl.BlockSpec((seq, gl), col(COL_K), pipeline_mode=once),
            pl.BlockSpec((seq, gl), col(COL_V), pipeline_mode=once),
            pl.BlockSpec((seq, gl), col(COL_Q), pipeline_mode=once),
            pl.BlockSpec((seq, gl), col(COL_Z), pipeline_mode=once),
            pl.BlockSpec((seq, LANES), lambda b, h: (b, 0), pipeline_mode=once),
            pl.BlockSpec((ctx, gl), col(COL_K)),
            pl.BlockSpec((ctx, gl), col(COL_V)),
            pl.BlockSpec((ctx, LANES), lambda b, h: (b, 0)),
            pl.BlockSpec((SUBLANES, gl), lambda b, h: (0, h)),
            pl.BlockSpec((SUBLANES, gl), lambda b, h: (0, ng + h)),
            pl.BlockSpec((SUBLANES, gl), lambda b, h: (0, h)),
            pl.BlockSpec((SUBLANES, LANES), lambda b, h: (0, 0)),
            pl.BlockSpec((1, LANES), lambda b, h: (0, 0)),
        ],
        out_specs=pl.BlockSpec((seq, gl), lambda b, h: (b, h)),
        out_shape=jax.ShapeDtypeStruct((batch * seq, HEADS * DV), BF16),
        scratch_shapes=[
            pltpu.VMEM((total, LANES), F32),
            pltpu.VMEM((total, LANES), F32),
            pltpu.VMEM((total, LANES), F32),
            pltpu.VMEM((4, total, LANES), F32),
            pltpu.VMEM((seq + 2 * SUBLANES, LANES), F32),
            pltpu.VMEM((group, 2, nc * 2 * CHUNK, LANES), BF16),
            pltpu.VMEM((group, nc * 2 * CHUNK, LANES), BF16),
            pltpu.VMEM((group, nc * 2 * CHUNK, LANES), BF16),
            pltpu.VMEM((group, nc * 2 * CHUNK, LANES), BF16),
            pltpu.VMEM((group, 2, nc * SUBLANES, LANES), F32),
            pltpu.VMEM((group, total, LANES), F32),
        ],
        compiler_params=_params("arbitrary", "arbitrary"),
        name="gated_deltanet",
    )(proj, proj, proj, proj, ba, proj_ctx, proj_ctx, ba_ctx, taps, taps, taps_q, ab, gn.reshape(1, DV))


CONF_ROWS = 128
CONF_HALO = 16


def _conf_kernel(a_ref, g_ref, w_ref, b_ref, o_ref, pad_scr, *, seq):
    cb = a_ref.shape[1]
    u = a_ref[...].astype(F32) * jax.nn.sigmoid(g_ref[...].astype(F32))
    pad_scr[0:CONF_HALO, :] = jnp.zeros((CONF_HALO, cb), F32)
    pad_scr[CONF_HALO:CONF_HALO + seq, :] = u
    pad_scr[CONF_HALO + seq:2 * CONF_HALO + seq, :] = jnp.zeros((CONF_HALO, cb), F32)
    win = CONF_ROWS + 2 * CONF_HALO
    first = CONF_HALO - CONF_KERNEL // 2

    def body(i, carry):
        r0 = pl.multiple_of(i * CONF_ROWS, CONF_ROWS)
        w = pad_scr[pl.ds(r0, win), :]
        acc = jnp.zeros((CONF_ROWS, cb), F32) + b_ref[...]
        for sub in range(SUBLANES):
            shifted = w if sub == 0 else pltpu.roll(w, win - sub, axis=0)
            for j in range(CONF_KERNEL):
                off = first + j
                if off % SUBLANES == sub:
                    base = off - sub
                    acc = acc + shifted[base:base + CONF_ROWS, :] * w_ref[j:j + 1, :]
        o_ref[pl.ds(r0, CONF_ROWS), :] = acc.astype(o_ref.dtype)
        return carry

    lax.fori_loop(0, seq // CONF_ROWS, body, 0)


def conformer_conv(proj, conf_dw, conf_dw_b, *, batch, seq, cb=256):
    c = conf_dw.shape[1]
    nb = c // cb
    taps = jnp.zeros((32, c), F32).at[:CONF_KERNEL].set(conf_dw)
    return pl.pallas_call(
        functools.partial(_conf_kernel, seq=seq),
        grid=(batch, nb),
        in_specs=[pl.BlockSpec((seq, cb), lambda b, j: (b, COL_GLU_A * nb + j)),
                  pl.BlockSpec((seq, cb), lambda b, j: (b, COL_GLU_G * nb + j)),
                  pl.BlockSpec((32, cb), lambda b, j: (0, j)),
                  pl.BlockSpec((1, cb), lambda b, j: (0, j))],
        out_specs=pl.BlockSpec((seq, cb), lambda b, j: (b, j)),
        out_shape=jax.ShapeDtypeStruct((batch * seq, c), BF16),
        scratch_shapes=[pltpu.VMEM((seq + 2 * CONF_HALO, cb), F32)],
        compiler_params=_params("arbitrary", "arbitrary"),
        name="conformer_conv",
    )(proj, proj, taps, conf_dw_b.reshape(1, c))


def _mixer_out_kernel(og_ref, uc_ref, ga_ref, gb_ref, x_ref, pos_ref, gm_ref, shf_ref, scf_ref,
                      gpost_ref, gpre_ref, lng_ref, lnb_ref, wa_ref, wb_ref, wo_ref, wr_ref, br_ref,
                      x2_ref, h2_ref, ti_ref, tw_ref, cnt_ref, cnt_scr, *, sub):
    parts = [slice(r, r + sub) for r in range(0, og_ref.shape[0], sub)]

    def rms(v):
        return v * lax.rsqrt(jnp.mean(v * v, axis=-1, keepdims=True) + EPS)

    def layer_norm_silu(uc):
        mu = jnp.mean(uc, axis=-1, keepdims=True)
        var = jnp.mean(jnp.square(uc - mu), axis=-1, keepdims=True)
        return _silu((uc - mu) * lax.rsqrt(var + EPS) * lng_ref[...] + lnb_ref[...]).astype(BF16)

    us = [layer_norm_silu(uc_ref[p, :].astype(F32)) for p in parts]
    yas = [jnp.dot(og_ref[p, :], wa_ref[...], preferred_element_type=F32) for p in parts]
    ybs = [jnp.dot(u, wb_ref[...], preferred_element_type=F32) for u in us]
    merged = [(jax.nn.sigmoid(ga_ref[p, :].astype(F32)) * ya
               + jax.nn.sigmoid(gb_ref[p, :].astype(F32)) * yb).astype(BF16) for p, ya, yb in zip(parts, yas, ybs)]
    ys = [jnp.dot(m, wo_ref[...], preferred_element_type=F32) for m in merged]
    x2s = [x_ref[p, :] + pos_ref[p, :] + gm_ref[0] * (rms(y) * gpost_ref[...]) for p, y in zip(parts, ys)]
    h2s = [(rms(x2) * gpre_ref[...] * (1.0 + scf_ref[0]) + shf_ref[0]).astype(BF16) for x2 in x2s]
    for p, x2, h2 in zip(parts, x2s, h2s):
        x2_ref[p, :] = x2
        h2_ref[p, :] = h2
    all_logits = [jnp.dot(h2, wr_ref[...], preferred_element_type=F32) + br_ref[...] for h2 in h2s]
    lane = lax.broadcasted_iota(jnp.int32, (sub, LANES), 1).astype(F32)
    ri = lax.broadcasted_iota(jnp.int32, (sub, sub), 0)
    ci = lax.broadcasted_iota(jnp.int32, (sub, sub), 1)
    before = jnp.where(ri > ci, 1.0, 0.0).astype(BF16)

    @pl.when(pl.program_id(0) == 0)
    def _():
        cnt_scr[...] = jnp.zeros_like(cnt_scr)

    routed = []
    for logits in all_logits:
        live = jnp.where(lane < N_EXPERTS, logits, -jnp.inf)
        top_v, top_i = [], []
        for _ in range(TOP_K):
            m = jnp.max(live, axis=-1, keepdims=True)
            idx = jnp.min(jnp.where(live == m, lane, float(LANES)), axis=-1, keepdims=True)
            top_v.append(m)
            top_i.append(idx)
            live = jnp.where(lane == idx, -jnp.inf, live)
        ex = [jnp.exp(v - top_v[0]) for v in top_v]
        denom = ex[0] + ex[1] + ex[2] + ex[3]
        picked = jnp.zeros((sub, LANES), F32)
        for k in range(TOP_K):
            picked = jnp.where(lane == top_i[k], 1.0, picked)
        routed.append((top_i, [e / denom for e in ex], picked,
                       jnp.dot(before, picked.astype(BF16), preferred_element_type=F32)))

    count = cnt_scr[...]
    for j, (p, (top_i, top_w, picked, within)) in enumerate(zip(parts, routed)):
        prefix = within + count[0:1, :]
        ti = jnp.zeros((sub, LANES), F32)
        tw = jnp.zeros((sub, LANES), F32)
        for k in range(TOP_K):
            rank = jnp.sum(jnp.where(lane == top_i[k], prefix, 0.0), axis=-1, keepdims=True)
            ti = jnp.where(lane == k, top_i[k], ti)
            ti = jnp.where(lane == TOP_K + k, rank, ti)
            tw = jnp.where(lane == k, top_w[k], tw)
        ti_ref[:, p] = jnp.transpose(ti)[:SUBLANES, :].astype(jnp.int32)
        tw_ref[:, p] = jnp.transpose(tw)[:SUBLANES, :]
        count = count + jnp.sum(picked, axis=0, keepdims=True)
        cnt_ref[j] = count
    cnt_scr[...] = count


def mixer_output(og, uc, proj, x, pos, mod, mod_row_of_tile, g_post, g_pre_f, ln_g, ln_b,
                 w_a, w_b, w_o, w_r, b_r, *, tm, sub):
    n, d = x.shape
    pos_tiles = pos.shape[0] // tm
    row = lambda i: (i, 0)
    fixed = lambda i: (0, 0)
    modspec = lambda k: pl.BlockSpec((1, 1, d), lambda i: (mod_row_of_tile(i), 0, k))
    vec = pl.BlockSpec((1, d), fixed)
    mat = pl.BlockSpec((d, d), fixed)
    return pl.pallas_call(
        functools.partial(_mixer_out_kernel, sub=sub),
        grid=(n // tm,),
        in_specs=[pl.BlockSpec((tm, d), row), pl.BlockSpec((tm, d), row),
                  pl.BlockSpec((tm, d), lambda i: (i, COL_GATE_A)),
                  pl.BlockSpec((tm, d), lambda i: (i, COL_GATE_B)),
                  pl.BlockSpec((tm, d), row),
                  pl.BlockSpec((tm, d), lambda i: (i % pos_tiles, 0)),
                  modspec(2), modspec(3), modspec(4),
                  vec, vec, vec, vec, mat, mat, mat,
                  pl.BlockSpec((d, LANES), fixed), pl.BlockSpec((1, LANES), fixed)],
        out_specs=[pl.BlockSpec((tm, d), row), pl.BlockSpec((tm, d), row),
                   pl.BlockSpec((SUBLANES, tm), lambda i: (0, i)), pl.BlockSpec((SUBLANES, tm), lambda i: (0, i)),
                   pl.BlockSpec((tm // sub, SUBLANES, LANES), lambda i: (i, 0, 0))],
        out_shape=[jax.ShapeDtypeStruct((n, d), F32), jax.ShapeDtypeStruct((n, d), BF16),
                   jax.ShapeDtypeStruct((SUBLANES, n), jnp.int32), jax.ShapeDtypeStruct((SUBLANES, n), F32),
                   jax.ShapeDtypeStruct((n // sub, SUBLANES, LANES), F32)],
        scratch_shapes=[pltpu.VMEM((SUBLANES, LANES), F32)],
        compiler_params=_params("arbitrary"),
        name="mixer_output",
    )(og, uc, proj, proj, x, pos, mod, mod, mod,
      g_post.reshape(1, d), g_pre_f.reshape(1, d), ln_g.reshape(1, d), ln_b.reshape(1, d),
      w_a, w_b, w_o, w_r, b_r)


EXPERT_BLOCK = 256
MOE_CHUNK = 512
GATHER_SUB = 256
GATHER_GROUP = 3
COMBINE_GRANULE = 32
COMBINE_SLOTS = 32


def _expert_kernel(bexp_ref, bnext_ref, rlo_ref, slo_ref, shi_ref, nused_ref,
                   h2_ref, ti_ref, wt_ref, wgu_hbm, bgu_ref, wd_hbm, bd_ref, o_ref,
                   acc, rw_acc, wgu_bf, wd_bf, wgu_f32, wd_f32, wsem):
    b = pl.program_id(0)
    expert = bexp_ref[b]

    def fetch(e):
        return (pltpu.make_async_copy(wgu_hbm.at[e], wgu_f32, wsem.at[0]),
                pltpu.make_async_copy(wd_hbm.at[e], wd_f32, wsem.at[1]))

    @pl.when(((b == 0) | (expert != bexp_ref[jnp.maximum(b - 1, 0)])) & (b < nused_ref[0]))
    def _():
        @pl.when(b == 0)
        def _():
            for cp in fetch(expert):
                cp.start()

        for cp in fetch(expert):
            cp.wait()
        wgu_bf[...] = wgu_f32[...].astype(BF16)
        wd_bf[...] = wd_f32[...].astype(BF16)

        @pl.when(bnext_ref[b] >= 0)
        def _():
            for cp in fetch(bnext_ref[b]):
                cp.start()

    @pl.when(b < nused_ref[0])
    def _():
        acc[...] = jnp.zeros_like(acc)
        rw_acc[...] = jnp.zeros_like(rw_acc)
        row = lax.broadcasted_iota(jnp.int32, (EXPERT_BLOCK, GATHER_SUB), 0)
        s_lo, s_hi = slo_ref[b], shi_ref[b]
        last_sub = h2_ref.shape[0] // GATHER_SUB - 1

        def select(s, live):
            t0 = pl.multiple_of(s * GATHER_SUB, GATHER_SUB)
            ids = ti_ref[0:TOP_K, pl.ds(t0, GATHER_SUB)]
            local = ti_ref[TOP_K:2 * TOP_K, pl.ds(t0, GATHER_SUB)] - rlo_ref[b]
            if live is not True:
                local = jnp.where(live, local, -1)
            inside = (ids == expert) & (local >= 0) & (local < EXPERT_BLOCK)
            hit_row = jnp.sum(jnp.where(inside, local + 1, 0), axis=0, keepdims=True) - 1
            hit_w = jnp.sum(jnp.where(inside, wt_ref[0:TOP_K, pl.ds(t0, GATHER_SUB)], 0.0), axis=0, keepdims=True)
            hit = row == hit_row
            return (jnp.where(hit, 1.0, 0.0).astype(BF16), jnp.where(hit, hit_w, 0.0),
                    h2_ref[pl.ds(t0, GATHER_SUB), :])

        def gather(i, carry):
            s0 = s_lo + GATHER_GROUP * i
            parts = [select(s0, True)] + [select(jnp.minimum(s0 + q, last_sub), s0 + q <= s_hi)
                                          for q in range(1, GATHER_GROUP)]
            acc[...] += jnp.dot(jnp.concatenate([p for p, _, _ in parts], axis=1),
                                jnp.concatenate([x for _, _, x in parts], axis=0), preferred_element_type=F32)
            rw_acc[...] += jnp.sum(functools.reduce(lambda a, c: a + c, [w for _, w, _ in parts]), axis=1,
                                   keepdims=True)
            return carry

        lax.fori_loop(0, (s_hi - s_lo + GATHER_GROUP) // GATHER_GROUP, gather, 0)
        gu = jnp.dot(acc[...].astype(BF16), wgu_bf[...], preferred_element_type=F32) + bgu_ref[0]
        dff = gu.shape[1] // 2
        gl = jnp.minimum(gu[:, :dff], SWIGLU_LIMIT)
        lin = jnp.clip(gu[:, dff:], -SWIGLU_LIMIT, SWIGLU_LIMIT)
        act = (gl * jax.nn.sigmoid(SWIGLU_ALPHA * gl) * (lin + 1.0)).astype(BF16)
        y = jnp.dot(act, wd_bf[...], preferred_element_type=F32) + bd_ref[0]
        o_ref[...] = (y * rw_acc[...]).astype(o_ref.dtype)

    @pl.when(b >= nused_ref[0])
    def _():
        o_ref[...] = jnp.zeros_like(o_ref)


def expert_blocks(h2, ti_t, wt_t, block_expert, next_expert, rank_lo, sub_lo, sub_hi, n_used, w_gate_up, b_gate_up,
                  w_down, b_down):
    n, d = h2.shape
    e, _, f2 = w_gate_up.shape
    n_blocks = block_expert.shape[0]
    whole = lambda b, be, bn, rl, lo, hi, nu: (0, 0)
    by_expert = lambda b, be, bn, rl, lo, hi, nu: (be[b], 0, 0)
    once = pl.Buffered(1)
    grid_spec = pltpu.PrefetchScalarGridSpec(
        num_scalar_prefetch=6,
        grid=(n_blocks,),
        in_specs=[pl.BlockSpec((n, d), whole, pipeline_mode=once),
                  pl.BlockSpec((SUBLANES, n), whole, pipeline_mode=once),
                  pl.BlockSpec((SUBLANES, n), whole, pipeline_mode=once),
                  pl.BlockSpec(memory_space=pl.ANY),
                  pl.BlockSpec((1, 1, f2), by_expert),
                  pl.BlockSpec(memory_space=pl.ANY),
                  pl.BlockSpec((1, 1, d), by_expert)],
        out_specs=pl.BlockSpec((EXPERT_BLOCK, d), lambda b, be, bn, rl, lo, hi, nu: (b, 0)),
        scratch_shapes=[pltpu.VMEM((EXPERT_BLOCK, d), F32), pltpu.VMEM((EXPERT_BLOCK, 1), F32),
                        pltpu.VMEM((d, f2), BF16), pltpu.VMEM((f2 // 2, d), BF16),
                        pltpu.VMEM((d, f2), F32), pltpu.VMEM((f2 // 2, d), F32),
                        pltpu.SemaphoreType.DMA((2,))],
    )
    return pl.pallas_call(
        _expert_kernel,
        grid_spec=grid_spec,
        out_shape=jax.ShapeDtypeStruct((n_blocks * EXPERT_BLOCK, d), BF16),
        compiler_params=_params("arbitrary"),
        name="expert_blocks",
    )(block_expert, next_expert, rank_lo, sub_lo, sub_hi, n_used, h2, ti_t, wt_t, w_gate_up, b_gate_up.reshape(e, 1, f2), w_down,
      b_down.reshape(e, 1, d))


def _combine_kernel(coff_ref, cgran_ref, cexp_ref, cbase_ref, yb_hbm, ti_ref, x2_ref, gf_ref, g_ref, o_ref,
                    slab, sem, acc):
    c = pl.program_id(0)
    start = coff_ref[c]
    count = coff_ref[c + 1] - start
    n_groups = (count + COMBINE_SLOTS - 1) // COMBINE_SLOTS

    def item(g, s):
        return start + jnp.minimum(g * COMBINE_SLOTS + s, count - 1)

    def granule_of(g, s):
        return cgran_ref[item(g, s)]

    def copies(g, buf):
        return [pltpu.make_async_copy(
            yb_hbm.at[pl.ds(pl.multiple_of(granule_of(g, s) * COMBINE_GRANULE, COMBINE_GRANULE), COMBINE_GRANULE), :],
            slab.at[buf, pl.ds(s * COMBINE_GRANULE, COMBINE_GRANULE), :], sem.at[buf])
            for s in range(COMBINE_SLOTS)]

    for cp in copies(0, 0):
        cp.start()
    acc[...] = jnp.zeros_like(acc)
    chunk = x2_ref.shape[0]
    row = lax.broadcasted_iota(jnp.int32, (COMBINE_GRANULE, chunk), 0)
    ids = ti_ref[0:TOP_K, :]
    ranks = ti_ref[TOP_K:2 * TOP_K, :]

    def group(g, carry):
        buf = g % 2

        @pl.when(g + 1 < n_groups)
        def _():
            for cp in copies(g + 1, 1 - buf):
                cp.start()

        for cp in copies(g, buf):
            cp.wait()
        pieces = []
        for s in range(COMBINE_SLOTS):
            live = g * COMBINE_SLOTS + s < count
            local = ranks - cbase_ref[item(g, s)]
            inside = (ids == jnp.where(live, cexp_ref[item(g, s)], -1)) & (local >= 0) & (local < COMBINE_GRANULE)
            hit_row = jnp.sum(jnp.where(inside, local + 1, 0), axis=0, keepdims=True) - 1
            pieces.append(jnp.where(row == hit_row, 1.0, 0.0).astype(BF16))
        onehot_t = jnp.concatenate(pieces, axis=0)
        acc[...] += lax.dot_general(onehot_t, slab[buf], (((0,), (0,)), ((), ())),
                                    preferred_element_type=F32)
        return carry

    lax.fori_loop(0, n_groups, group, 0)
    y = acc[...]
    yn = y * lax.rsqrt(jnp.mean(y * y, axis=-1, keepdims=True) + EPS) * g_ref[...]
    o_ref[...] = x2_ref[...] + gf_ref[0] * yn


def combine_residual(yb, ti_t, x2, chunk_off, chunk_items, mod, mod_row_of_chunk, g_post_f):
    n, d = x2.shape
    chunk_row = lambda c, off, gran, exp, base: (c, 0)
    grid_spec = pltpu.PrefetchScalarGridSpec(
        num_scalar_prefetch=4,
        grid=(n // MOE_CHUNK,),
        in_specs=[pl.BlockSpec(memory_space=pl.ANY),
                  pl.BlockSpec((SUBLANES, MOE_CHUNK), lambda c, off, gran, exp, base: (0, c)),
                  pl.BlockSpec((MOE_CHUNK, d), chunk_row),
                  pl.BlockSpec((1, 1, d), lambda c, off, gran, exp, base: (mod_row_of_chunk(c), 0, 5)),
                  pl.BlockSpec((1, d), lambda c, off, gran, exp, base: (0, 0))],
        out_specs=pl.BlockSpec((MOE_CHUNK, d), chunk_row),
        scratch_shapes=[pltpu.VMEM((2, COMBINE_SLOTS * COMBINE_GRANULE, d), BF16),
                        pltpu.SemaphoreType.DMA((2,)),
                        pltpu.VMEM((MOE_CHUNK, d), F32)],
    )
    return pl.pallas_call(
        _combine_kernel,
        grid_spec=grid_spec,
        out_shape=jax.ShapeDtypeStruct((n, d), F32),
        compiler_params=_params("arbitrary"),
        name="combine_residual",
    )(chunk_off, *chunk_items, yb, ti_t, x2, mod, g_post_f.reshape(1, d))


def _grid_pos_embedding(rows, d):
    quarter = d // 4
    omega = POS_BASE ** (-jnp.arange(quarter, dtype=F32) / quarter)

    def emb(p):
        ang = p[:, None] * omega[None, :]
        return jnp.concatenate([jnp.sin(ang), jnp.cos(ang)], axis=-1)

    row_emb = jnp.repeat(emb(jnp.arange(rows, dtype=F32)), GRID_W, axis=0)
    col_emb = jnp.tile(emb(jnp.arange(GRID_W, dtype=F32)), (rows, 1))
    return jnp.concatenate([row_emb, col_emb], axis=-1)


def _count_le(sorted_vals, queries):
    return jnp.sum(sorted_vals <= queries[:, None], axis=1).astype(jnp.int32)


def _moe_plan(cnt_after, n_tok, tm):
    assert tm == GATHER_SUB
    sub_after = cnt_after[:, 0, :N_EXPERTS].astype(jnp.int32)
    counts = sub_after[-1]
    padded = (counts + EXPERT_BLOCK - 1) // EXPERT_BLOCK * EXPERT_BLOCK
    padded_end = jnp.cumsum(padded)
    padded_start = padded_end - padded
    n_blocks = n_tok * TOP_K // EXPERT_BLOCK + N_EXPERTS
    blocks = jnp.arange(n_blocks, dtype=jnp.int32)
    block_expert = jnp.minimum(_count_le(padded_end[None, :], blocks * EXPERT_BLOCK), N_EXPERTS - 1)
    n_used = (padded_end[-1:] // EXPERT_BLOCK).astype(jnp.int32)
    of_block = (block_expert[:, None] == jnp.arange(N_EXPERTS, dtype=jnp.int32)[None, :]).astype(jnp.int32)
    r_lo = (blocks - jnp.sum(of_block * padded_start[None, :], axis=1) // EXPERT_BLOCK) * EXPERT_BLOCK
    r_hi = jnp.minimum(jnp.sum(of_block * counts[None, :], axis=1), r_lo + EXPERT_BLOCK) - 1
    through = jnp.sum(of_block[:, :, None] * sub_after.T[None, :, :], axis=1)
    after = jnp.sum(of_block * padded_end[None, :], axis=1) // EXPERT_BLOCK
    next_expert = jnp.sum((after[:, None] == blocks[None, :]).astype(jnp.int32) * block_expert[None, :], axis=1)
    next_expert = jnp.where(after < n_used[0], next_expert, -1).astype(jnp.int32)
    n_sub = sub_after.shape[0]
    sub_lo = jnp.minimum(_count_le(through, r_lo), n_sub - 1)
    sub_hi = jnp.minimum(_count_le(through, r_hi), n_sub - 1)
    nch = n_tok // MOE_CHUNK
    per_chunk = MOE_CHUNK // tm
    cb_after = sub_after[per_chunk - 1::per_chunk]
    cb_before = jnp.concatenate([jnp.zeros((1, N_EXPERTS), jnp.int32), cb_after[:-1]], axis=0)
    b_lo = ((padded_start[None, :] + cb_before) // COMBINE_GRANULE).reshape(-1)
    b_hi = ((padded_start[None, :] + cb_after - 1) // COMBINE_GRANULE).reshape(-1)
    n_it = jnp.where((cb_after > cb_before).reshape(-1), b_hi - b_lo + 1, 0)
    off_end = jnp.cumsum(n_it)
    off = off_end - n_it
    n_items = n_blocks * (EXPERT_BLOCK // COMBINE_GRANULE) + N_EXPERTS * (nch - 1)
    w = jnp.minimum(jnp.arange(n_items, dtype=jnp.int32), off_end[-1] - 1)
    chunk_end = off_end[N_EXPERTS - 1::N_EXPERTS]
    chunk_off = jnp.concatenate([jnp.zeros((1,), jnp.int32), chunk_end]).astype(jnp.int32)
    chunk_of = jnp.minimum(_count_le(chunk_end[None, :], w), nch - 1)
    chunk_expert = jnp.minimum(_count_le(off_end.reshape(nch, N_EXPERTS)[chunk_of], w), N_EXPERTS - 1)
    cell = chunk_of * N_EXPERTS + chunk_expert
    chunk_granules = (b_lo[cell] + w - off[cell]).astype(jnp.int32)
    chunk_base = chunk_granules * COMBINE_GRANULE - padded_start[chunk_expert]
    return block_expert, next_expert, r_lo, sub_lo, sub_hi, n_used, chunk_off, (chunk_granules, chunk_expert, chunk_base)


def kernel(x, c, ctx, c_ctx, w_mod, b_mod, g_pre_mix, g_post_mix, g_pre_ffn, g_post_ffn, w_in, conv_kv,
           conv_q, a_log, dt_bias, gdn_norm_g, w_proj_a, conf_dw, conf_dw_b, conf_ln_g, conf_ln_b,
           w_proj_b, w_out, w_router, b_router, w_gate_up, b_gate_up, w_down, b_down):
    batch, seq, d = x.shape
    ctx_len = ctx.shape[1]
    n_tok = batch * seq
    gw = HEADS * DV
    beta_off = HEADS * DK + gw
    state_cols = beta_off + 4 * HEADS
    q_off = state_cols

    mod_rows = 2 * SUBLANES
    cond = jnp.zeros((mod_rows, d), F32).at[:batch].set(c).at[batch].set(c_ctx)
    mod = modulation(cond, w_mod[0], b_mod[0]).reshape(mod_rows, 1, 6 * d)

    w_t = jnp.swapaxes(w_in[0], 0, 1)
    w_main = projection_weights(w_t, keep=beta_off, skip=state_cols - beta_off)
    w_ba = jnp.zeros((LANES, d), F32).at[:4 * HEADS].set(w_t[beta_off:state_cols]).astype(BF16)

    pos = _grid_pos_embedding(seq // GRID_W, d)
    x_flat = x.reshape(n_tok, d)
    tm = 1024
    tiles_per_seq = seq // tm
    proj, ba = input_projection(x_flat, pos, mod, lambda i: i // tiles_per_seq, g_pre_mix[0],
                                w_main, w_ba, tm=tm, tn=2048)
    proj_ctx, ba_ctx = input_projection(ctx.reshape(batch * ctx_len, d), None, mod, lambda i: batch,
                                        g_pre_mix[0], w_main[:beta_off], w_ba, tm=ctx_len, tn=1024)

    og = gated_deltanet(proj, ba, proj_ctx, ba_ctx, conv_kv[0], conv_q[0], a_log[0], dt_bias[0],
                        gdn_norm_g[0], batch=batch, seq=seq, ctx=ctx_len)
    uc = conformer_conv(proj, conf_dw[0], conf_dw_b[0], batch=batch, seq=seq)

    tm2 = 512
    w_r = jnp.zeros((d, LANES), F32).at[:, :N_EXPERTS].set(w_router[0]).astype(BF16)
    b_r = jnp.zeros((1, LANES), F32).at[0, :N_EXPERTS].set(b_router[0])
    x2, h2, top_i, top_w, cnt_after = mixer_output(
        og, uc, proj, x_flat, pos, mod, lambda i: i // (seq // tm2), g_post_mix[0], g_pre_ffn[0],
        conf_ln_g[0], conf_ln_b[0], w_proj_a[0].astype(BF16), w_proj_b[0].astype(BF16),
        w_out[0].astype(BF16), w_r, b_r, tm=tm2, sub=GATHER_SUB)

    block_expert, next_expert, rank_lo, sub_lo, sub_hi, n_used, chunk_off, chunk_items = _moe_plan(
        cnt_after, n_tok, GATHER_SUB)
    yb = expert_blocks(h2, top_i, top_w, block_expert, next_expert, rank_lo, sub_lo, sub_hi, n_used, w_gate_up[0],
                       b_gate_up[0], w_down[0], b_down[0])
    out = combine_residual(yb, top_i, x2, chunk_off, chunk_items, mod, lambda c: c // (seq // MOE_CHUNK),
                           g_post_ffn[0])
    return out.reshape(batch, seq, d)
```

```python
import functools

import jax
import jax.numpy as jnp
from jax import lax
from jax.experimental import pallas as pl
from jax.experimental.pallas import tpu as pltpu

F32 = jnp.float32
BF16 = jnp.bfloat16

D_MODEL = 1024
GRID_W = 64
HEADS = 8
DK = 128
DV = 128
SHORT_CONV = 5
CHUNK = 64
CONF_KERNEL = 31
N_EXPERTS = 32
TOP_K = 4
SWIGLU_LIMIT = 7.0
SWIGLU_ALPHA = 1.702
MOE_BLOCK = 128
EPS = 1e-6
POS_BASE = 10000.0

LANES = 128
SUBLANES = 8
VMEM_LIMIT = 60 * 1024 * 1024

COL_K, COL_V, COL_Q, COL_Z, COL_GLU_A, COL_GLU_G, COL_GATE_A, COL_GATE_B = range(8)


def _params(*sem):
    return pltpu.CompilerParams(dimension_semantics=sem, vmem_limit_bytes=VMEM_LIMIT)


def _mod_kernel(c_ref, w_ref, b_ref, o_ref):
    c = c_ref[...]
    s = c * jax.nn.sigmoid(c)
    o_ref[...] = jnp.dot(s, w_ref[...], preferred_element_type=F32,
                         precision=lax.Precision.HIGHEST) + b_ref[...]


def modulation(cond, w_mod, b_mod):
    r, d = cond.shape
    n = w_mod.shape[1]
    tn = 1024
    return pl.pallas_call(
        _mod_kernel,
        grid=(n // tn,),
        in_specs=[pl.BlockSpec((r, d), lambda j: (0, 0)),
                  pl.BlockSpec((d, tn), lambda j: (0, j)),
                  pl.BlockSpec((1, tn), lambda j: (0, j))],
        out_specs=pl.BlockSpec((r, tn), lambda j: (0, j)),
        out_shape=jax.ShapeDtypeStruct((r, n), F32),
        compiler_params=_params("arbitrary"),
        name="modulation",
    )(cond, w_mod, b_mod.reshape(1, n))


def _proj_weights_kernel(a_ref, b_ref, o_ref, *, first_shifted, shift):
    @pl.when(pl.program_id(0) < first_shifted)
    def _():
        o_ref[...] = a_ref[...].astype(o_ref.dtype)

    @pl.when(pl.program_id(0) >= first_shifted)
    def _():
        o_ref[...] = jnp.concatenate([a_ref[shift:, :], b_ref[...]], axis=0).astype(o_ref.dtype)


def projection_weights(w_t, *, keep, skip, tr=1024):
    total, d = w_t.shape
    n_out = total - skip
    assert keep % tr == 0 and n_out % tr == 0 and tr % skip == 0 and skip % SUBLANES == 0
    return pl.pallas_call(
        functools.partial(_proj_weights_kernel, first_shifted=keep // tr, shift=skip),
        grid=(n_out // tr,),
        in_specs=[pl.BlockSpec((tr, d), lambda j: (j, 0)),
                  pl.BlockSpec((skip, d), lambda j: ((j + 1) * (tr // skip), 0))],
        out_specs=pl.BlockSpec((tr, d), lambda j: (j, 0)),
        out_shape=jax.ShapeDtypeStruct((n_out, d), BF16),
        compiler_params=_params("arbitrary"),
        name="projection_weights",
    )(w_t, w_t)


_NT = (((1,), (1,)), ((), ()))


def _inproj_kernel(*refs, has_pos):
    if has_pos:
        x_ref, pos_ref, sh_ref, sc_ref, g_ref, w_ref, wba_ref, o_ref, ba_ref, h_scr = refs
    else:
        x_ref, sh_ref, sc_ref, g_ref, w_ref, wba_ref, o_ref, ba_ref, h_scr = refs

    @pl.when(pl.program_id(1) == 0)
    def _():
        x = x_ref[...]
        if has_pos:
            x = x + pos_ref[...]
        y = x * lax.rsqrt(jnp.mean(x * x, axis=-1, keepdims=True) + EPS) * g_ref[...]
        h = (y * (1.0 + sc_ref[0]) + sh_ref[0]).astype(BF16)
        h_scr[...] = h
        ba_ref[...] = lax.dot_general(h, wba_ref[...], _NT, preferred_element_type=F32)

    o_ref[...] = lax.dot_general(h_scr[...], w_ref[...], _NT, preferred_element_type=F32).astype(o_ref.dtype)


def input_projection(x, pos, mod, mod_row_of_tile, g_pre, w_main, w_ba, *, tm, tn):
    n, d = x.shape
    w = w_main.shape[0]
    has_pos = pos is not None
    in_specs = [pl.BlockSpec((tm, d), lambda i, j: (i, 0))]
    args = [x]
    if has_pos:
        pos_tiles = pos.shape[0] // tm
        in_specs.append(pl.BlockSpec((tm, d), lambda i, j: (i % pos_tiles, 0)))
        args.append(pos)
    in_specs += [
        pl.BlockSpec((1, 1, d), lambda i, j: (mod_row_of_tile(i), 0, 0)),
        pl.BlockSpec((1, 1, d), lambda i, j: (mod_row_of_tile(i), 0, 1)),
        pl.BlockSpec((1, d), lambda i, j: (0, 0)),
        pl.BlockSpec((tn, d), lambda i, j: (j, 0)),
        pl.BlockSpec((LANES, d), lambda i, j: (0, 0)),
    ]
    args += [mod, mod, g_pre.reshape(1, d), w_main, w_ba]
    return pl.pallas_call(
        functools.partial(_inproj_kernel, has_pos=has_pos),
        grid=(n // tm, w // tn),
        in_specs=in_specs,
        out_specs=[pl.BlockSpec((tm, tn), lambda i, j: (i, j)),
                   pl.BlockSpec((tm, LANES), lambda i, j: (i, 0))],
        out_shape=[jax.ShapeDtypeStruct((n, w), BF16), jax.ShapeDtypeStruct((n, LANES), F32)],
        scratch_shapes=[pltpu.VMEM((tm, d), BF16)],
        compiler_params=_params("arbitrary", "arbitrary"),
        name="input_projection_pos" if has_pos else "input_projection_ctx",
    )(*args)


def _silu(x):
    return x * jax.nn.sigmoid(x)


def _softplus(x):
    return jnp.maximum(x, 0.0) + jnp.log(1.0 + jnp.exp(-jnp.abs(x)))


def _short_conv(src, taps, pad_scr, rows):
    half = SHORT_CONV // 2
    pad_scr[0:SUBLANES, :] = jnp.zeros((SUBLANES, LANES), F32)
    pad_scr[SUBLANES:SUBLANES + rows, :] = src.astype(F32)
    pad_scr[SUBLANES + rows:2 * SUBLANES + rows, :] = jnp.zeros((SUBLANES, LANES), F32)
    acc = None
    for j in range(SHORT_CONV):
        off = SUBLANES + j - half
        term = pad_scr[off:off + rows, :] * taps[j:j + 1, :]
        acc = term if acc is None else acc + term
    return _silu(acc)


def _l2n(x):
    return x * lax.rsqrt(jnp.sum(x * x, axis=-1, keepdims=True) + EPS)


def _gdn_kernel(k_ref, v_ref, q_ref, z_ref, ba_ref, kc_ref, vc_ref, bac_ref,
                wk_ref, wv_ref, wq_ref, ab_ref, gn_ref, o_ref,
                kf, vf, qf, gates, pad_scr, lhs, val, att, ktt, dec, osc, *, seq, ctx, unroll, group):
    total = ctx + seq
    nc = total // CHUNK
    ncc = ctx // CHUNK
    two = 2 * CHUNK
    lg_chunk = CHUNK.bit_length() - 1

    ri = lax.broadcasted_iota(jnp.int32, (two, two), 0)
    ci = lax.broadcasted_iota(jnp.int32, (two, two), 1)
    same_dir = (ri >> lg_chunk) == (ci >> lg_chunk)
    incl = same_dir & (((ri < CHUNK) & (ri >= ci)) | ((ri >= CHUNK) & (ri <= ci)))
    strict = incl & (ri != ci)
    eye = jnp.where(ri == ci, 1.0, 0.0)
    tri = jnp.where(incl, 1.0, 0.0).astype(BF16)
    level_masks = []
    for lg in range(lg_chunk):
        same_parent = (ri >> (lg + 1)) == (ci >> (lg + 1))
        level_masks.append(jnp.where(same_parent & ((ri >> lg) != (ci >> lg)), 1.0, 0.0))
    top_rows = lax.broadcasted_iota(jnp.int32, (two, LANES), 0) < CHUNK

    def bwd_chunk(t):
        return jnp.where(t < ncc, ncc - 1 - t, nc + ncc - 1 - t)

    def stacked(ref, rf, rb):
        return jnp.concatenate([ref[pl.ds(rf, CHUNK), :], ref[pl.ds(rb, CHUNK), :]], axis=0)

    def chunk_load(t):
        rf = pl.multiple_of(t * CHUNK, CHUNK)
        rb = pl.multiple_of(bwd_chunk(t) * CHUNK, CHUNK)
        beta = jnp.concatenate([gates[0, pl.ds(rf, CHUNK), :], gates[1, pl.ds(rb, CHUNK), :]], axis=0)
        g = jnp.concatenate([gates[2, pl.ds(rf, CHUNK), :], gates[3, pl.ds(rb, CHUNK), :]], axis=0)
        return stacked(kf, rf, rb), stacked(vf, rf, rb), stacked(qf, rf, rb), beta, g

    def chunk_triangle(k, q, beta, g):
        kb = k.astype(BF16)
        kq = jnp.concatenate([kb, q.astype(BF16)], axis=0)
        kk_qk = lax.dot_general(kq, kb, (((1,), (1,)), ((), ())), preferred_element_type=F32)
        g_hi = g.astype(BF16)
        r1 = g - g_hi.astype(F32)
        g_mid = r1.astype(BF16)
        g_lo = (r1 - g_mid.astype(F32)).astype(BF16)
        gc3 = jnp.dot(tri, jnp.concatenate([g_hi, g_mid, g_lo], axis=1), preferred_element_type=F32)
        gc = gc3[:, :LANES] + gc3[:, LANES:2 * LANES] + gc3[:, 2 * LANES:]
        gc_row = jnp.transpose(gc)
        decay = jnp.where(incl, jnp.exp(gc - gc_row), 0.0)
        a = jnp.where(strict, beta * kk_qk[:two] * decay, 0.0)
        return a, (kk_qk[two:] * decay).astype(BF16), gc

    def chunk_finish(hh, t, k, v, q, beta, gc, attn, tmat):
        eg = jnp.exp(gc)
        rhs = jnp.concatenate([v * beta, k * beta * eg], axis=1).astype(BF16)
        sol = jnp.dot(tmat.astype(BF16), rhs, preferred_element_type=F32)
        g_end = jnp.where(top_rows, jnp.broadcast_to(gc[CHUNK - 1:CHUNK, :], (two, LANES)),
                          jnp.broadcast_to(gc[CHUNK:CHUNK + 1, :], (two, LANES)))
        k_tail = k * jnp.exp(g_end - gc)
        qg = (q * eg).astype(BF16)
        kcum = sol[:, DV:].astype(BF16)
        r0 = pl.multiple_of(t * two, two)
        lhs[hh, 0, pl.ds(r0, two), :] = jnp.concatenate([kcum[:CHUNK], qg[:CHUNK]], axis=0)
        lhs[hh, 1, pl.ds(r0, two), :] = jnp.concatenate([kcum[CHUNK:], qg[CHUNK:]], axis=0)
        val[hh, pl.ds(r0, two), :] = sol[:, :DV].astype(BF16)
        att[hh, pl.ds(r0, two), :] = attn
        ktt[hh, pl.ds(r0, two), :] = jnp.transpose(k_tail).astype(BF16)
        d0 = pl.multiple_of(t * SUBLANES, SUBLANES)
        e_end = jnp.exp(g_end)
        dec[hh, 0, pl.ds(d0, SUBLANES), :] = e_end[:SUBLANES]
        dec[hh, 1, pl.ds(d0, SUBLANES), :] = e_end[CHUNK:CHUNK + SUBLANES]

    for hh in range(group):
        head = pl.program_id(1) * group + hh
        cols = slice(hh * LANES, (hh + 1) * LANES)

        kf[0:ctx, :] = _l2n(_short_conv(kc_ref[:, cols], wk_ref[:, cols], pad_scr, ctx))
        vf[0:ctx, :] = _short_conv(vc_ref[:, cols], wv_ref[:, cols], pad_scr, ctx)
        qf[0:ctx, :] = jnp.zeros((ctx, LANES), F32)
        kf[ctx:total, :] = _l2n(_short_conv(k_ref[:, cols], wk_ref[:, cols], pad_scr, seq))
        vf[ctx:total, :] = _short_conv(v_ref[:, cols], wv_ref[:, cols], pad_scr, seq)
        qf[ctx:total, :] = _l2n(_short_conv(q_ref[:, cols], wq_ref[:, cols], pad_scr, seq)) * (DK ** -0.5)

        def gate_cols(src_ref, lo, rows):
            x = src_ref[...]
            lane = lax.broadcasted_iota(jnp.int32, (rows, LANES), 1)
            beta = jax.nn.sigmoid(x)
            g = -jnp.exp(ab_ref[0:1, :]) * _softplus(x + ab_ref[1:2, :])
            for slot, (arr, base) in enumerate(((beta, 0), (beta, HEADS), (g, 2 * HEADS), (g, 3 * HEADS))):
                col = jnp.sum(jnp.where(lane == base + head, arr, 0.0), axis=-1, keepdims=True)
                gates[slot, lo:lo + rows, :] = jnp.broadcast_to(col, (rows, LANES))

        gate_cols(bac_ref, 0, ctx)
        gate_cols(ba_ref, ctx, seq)

        def chunk_body(i, carry, hh=hh):
            steps = [i * unroll + u for u in range(unroll)]
            loaded = [chunk_load(t) for t in steps]
            tris = [chunk_triangle(k, q, beta, g) for (k, v, q, beta, g) in loaded]
            tmats = [eye - a * level_masks[0] for (a, _, _) in tris]
            for m in level_masks[1:]:
                nxt = []
                for (a, _, _), tmat in zip(tris, tmats):
                    cs = (a * m).astype(BF16)
                    tb = tmat.astype(BF16)
                    tc = jnp.dot(tb, cs, preferred_element_type=F32)
                    nxt.append(tmat - jnp.dot(tc.astype(BF16), tb, preferred_element_type=F32))
                tmats = nxt
            for t, (k, v, q, beta, g), (a, attn, gc), tmat in zip(steps, loaded, tris, tmats):
                chunk_finish(hh, t, k, v, q, beta, gc, attn, tmat)
            return carry

        lax.fori_loop(0, nc // unroll, chunk_body, 0)

    def scan_body(t, carry):
        rf = pl.multiple_of(t * CHUNK, CHUNK)
        rb = pl.multiple_of(bwd_chunk(t) * CHUNK, CHUNK)
        r0 = pl.multiple_of(t * two, two)
        d0 = pl.multiple_of(t * SUBLANES, SUBLANES)
        rs = [(jnp.dot(lhs[hh, 0, pl.ds(r0, two), :], carry[2 * hh].astype(BF16), preferred_element_type=F32),
               jnp.dot(lhs[hh, 1, pl.ds(r0, two), :], carry[2 * hh + 1].astype(BF16), preferred_element_type=F32))
              for hh in range(group)]
        v_news = [val[hh, pl.ds(r0, two), :].astype(F32) - jnp.concatenate([r_f[:CHUNK], r_b[:CHUNK]], axis=0)
                  for hh, (r_f, r_b) in enumerate(rs)]
        outs, new_states = [], []
        for hh, ((r_f, r_b), v_new) in enumerate(zip(rs, v_news)):
            zeros = jnp.zeros_like(v_new)
            v_bd = jnp.concatenate([jnp.where(top_rows, v_new, zeros), jnp.where(top_rows, zeros, v_new)],
                                   axis=1).astype(BF16)
            upd = jnp.dot(ktt[hh, pl.ds(r0, two), :], v_bd, preferred_element_type=F32)
            new_states.append(carry[2 * hh] * dec[hh, 0, pl.ds(d0, SUBLANES), :][0:1, :] + upd[:, :DV])
            new_states.append(carry[2 * hh + 1] * dec[hh, 1, pl.ds(d0, SUBLANES), :][0:1, :] + upd[:, DV:])
            outs.append(jnp.concatenate([r_f[CHUNK:], r_b[CHUNK:]], axis=0)
                        + jnp.dot(att[hh, pl.ds(r0, two), :], v_new.astype(BF16), preferred_element_type=F32))
        for hh, o in enumerate(outs):
            osc[hh, pl.ds(rf, CHUNK), :] += o[:CHUNK]
            osc[hh, pl.ds(rb, CHUNK), :] += o[CHUNK:]
        return tuple(new_states)

    osc[...] = jnp.zeros_like(osc)
    zero = jnp.zeros((DK, DV), F32)
    lax.fori_loop(0, nc, scan_body, (zero,) * (2 * group))

    for hh in range(group):
        cols = slice(hh * LANES, (hh + 1) * LANES)
        o = osc[hh, ctx:total, :]
        o = o * lax.rsqrt(jnp.mean(o * o, axis=-1, keepdims=True) + EPS) * gn_ref[...]
        o_ref[:, cols] = (o * _silu(z_ref[:, cols].astype(F32))).astype(o_ref.dtype)


def gated_deltanet(proj, ba, proj_ctx, ba_ctx, conv_kv, conv_q, a_log, dt_bias, gn, *, batch, seq, ctx,
                   unroll=18, group=4):
    total = seq + ctx
    nc = total // CHUNK
    gl = group * LANES
    ng = HEADS // group
    taps = jnp.zeros((SUBLANES, conv_kv.shape[1]), F32).at[:SHORT_CONV].set(conv_kv)
    taps_q = jnp.zeros((SUBLANES, conv_q.shape[1]), F32).at[:SHORT_CONV].set(conv_q)
    ab = jnp.zeros((SUBLANES, LANES), F32)
    ab = ab.at[0, 2 * HEADS:4 * HEADS].set(a_log.reshape(-1)).at[1, 2 * HEADS:4 * HEADS].set(dt_bias.reshape(-1))
    col = lambda base: (lambda b, h: (b, base * ng + h))
    once = pl.Buffered(1)
    return pl.pallas_call(
        functools.partial(_gdn_kernel, seq=seq, ctx=ctx, unroll=unroll, group=group),
        grid=(batch, ng),
        in_specs=[
            pl.BlockSpec((seq, gl), col(COL_K), pipeline_mode=once),
            pl.BlockSpec((seq, gl), col(COL_V), pipeline_mode=once),
            pl.BlockSpec((seq, gl), col(COL_Q), pipeline_mode=once),
            pl.BlockSpec((seq, gl), col(COL_Z), pipeline_mode=once),
            pl.BlockSpec((seq, LANES), lambda b, h: (b, 0), pipeline_mode=once),
            pl.BlockSpec((ctx, gl), col(COL_K)),
            pl.BlockSpec((ctx, gl), col(COL_V)),
            pl.BlockSpec((ctx, LANES), lambda b, h: (b, 0)),
            pl.BlockSpec((SUBLANES, gl), lambda b, h: (0, h)),
            pl.BlockSpec((SUBLANES, gl), lambda b, h: (0, ng + h)),
            pl.BlockSpec((SUBLANES, gl), lambda b, h: (0, h)),
            pl.BlockSpec((SUBLANES, LANES), lambda b, h: (0, 0)),
            pl.BlockSpec((1, LANES), lambda b, h: (0, 0)),
        ],
        out_specs=pl.BlockSpec((seq, gl), lambda b, h: (b, h)),
        out_shape=jax.ShapeDtypeStruct((batch * seq, HEADS * DV), BF16),
        scratch_shapes=[
            pltpu.VMEM((total, LANES), F32),
            pltpu.VMEM((total, LANES), F32),
            pltpu.VMEM((total, LANES), F32),
            pltpu.VMEM((4, total, LANES), F32),
            pltpu.VMEM((seq + 2 * SUBLANES, LANES), F32),
            pltpu.VMEM((group, 2, nc * 2 * CHUNK, LANES), BF16),
            pltpu.VMEM((group, nc * 2 * CHUNK, LANES), BF16),
            pltpu.VMEM((group, nc * 2 * CHUNK, LANES), BF16),
            pltpu.VMEM((group, nc * 2 * CHUNK, LANES), BF16),
            pltpu.VMEM((group, 2, nc * SUBLANES, LANES), F32),
            pltpu.VMEM((group, total, LANES), F32),
        ],
        compiler_params=_params("arbitrary", "arbitrary"),
        name="gated_deltanet",
    )(proj, proj, proj, proj, ba, proj_ctx, proj_ctx, ba_ctx, taps, taps, taps_q, ab, gn.reshape(1, DV))


CONF_ROWS = 128
CONF_HALO = 16


def _conf_kernel(a_ref, g_ref, w_ref, b_ref, o_ref, pad_scr, *, seq):
    cb = a_ref.shape[1]
    u = a_ref[...].astype(F32) * jax.nn.sigmoid(g_ref[...].astype(F32))
    pad_scr[0:CONF_HALO, :] = jnp.zeros((CONF_HALO, cb), F32)
    pad_scr[CONF_HALO:CONF_HALO + seq, :] = u
    pad_scr[CONF_HALO + seq:2 * CONF_HALO + seq, :] = jnp.zeros((CONF_HALO, cb), F32)
    win = CONF_ROWS + 2 * CONF_HALO
    first = CONF_HALO - CONF_KERNEL // 2

    def body(i, carry):
        r0 = pl.multiple_of(i * CONF_ROWS, CONF_ROWS)
        w = pad_scr[pl.ds(r0, win), :]
        acc = jnp.zeros((CONF_ROWS, cb), F32) + b_ref[...]
        for sub in range(SUBLANES):
            shifted = w if sub == 0 else pltpu.roll(w, win - sub, axis=0)
            for j in range(CONF_KERNEL):
                off = first + j
                if off % SUBLANES == sub:
                    base = off - sub
                    acc = acc + shifted[base:base + CONF_ROWS, :] * w_ref[j:j + 1, :]
        o_ref[pl.ds(r0, CONF_ROWS), :] = acc.astype(o_ref.dtype)
        return carry

    lax.fori_loop(0, seq // CONF_ROWS, body, 0)


def conformer_conv(proj, conf_dw, conf_dw_b, *, batch, seq, cb=256):
    c = conf_dw.shape[1]
    nb = c // cb
    taps = jnp.zeros((32, c), F32).at[:CONF_KERNEL].set(conf_dw)
    return pl.pallas_call(
        functools.partial(_conf_kernel, seq=seq),
        grid=(batch, nb),
        in_specs=[pl.BlockSpec((seq, cb), lambda b, j: (b, COL_GLU_A * nb + j)),
                  pl.BlockSpec((seq, cb), lambda b, j: (b, COL_GLU_G * nb + j)),
                  pl.BlockSpec((32, cb), lambda b, j: (0, j)),
                  pl.BlockSpec((1, cb), lambda b, j: (0, j))],
        out_specs=pl.BlockSpec((seq, cb), lambda b, j: (b, j)),
        out_shape=jax.ShapeDtypeStruct((batch * seq, c), BF16),
        scratch_shapes=[pltpu.VMEM((seq + 2 * CONF_HALO, cb), F32)],
        compiler_params=_params("arbitrary", "arbitrary"),
        name="conformer_conv",
    )(proj, proj, taps, conf_dw_b.reshape(1, c))


def _mixer_out_kernel(og_ref, uc_ref, ga_ref, gb_ref, x_ref, pos_ref, gm_ref, shf_ref, scf_ref,
                      gpost_ref, gpre_ref, lng_ref, lnb_ref, wa_ref, wb_ref, wo_ref, wr_ref, br_ref,
                      x2_ref, h2_ref, ti_ref, tw_ref, cnt_ref, cnt_scr, *, sub):
    parts = [slice(r, r + sub) for r in range(0, og_ref.shape[0], sub)]

    def rms(v):
        return v * lax.rsqrt(jnp.mean(v * v, axis=-1, keepdims=True) + EPS)

    def layer_norm_silu(uc):
        mu = jnp.mean(uc, axis=-1, keepdims=True)
        var = jnp.mean(jnp.square(uc - mu), axis=-1, keepdims=True)
        return _silu((uc - mu) * lax.rsqrt(var + EPS) * lng_ref[...] + lnb_ref[...]).astype(BF16)

    us = [layer_norm_silu(uc_ref[p, :].astype(F32)) for p in parts]
    yas = [jnp.dot(og_ref[p, :], wa_ref[...], preferred_element_type=F32) for p in parts]
    ybs = [jnp.dot(u, wb_ref[...], preferred_element_type=F32) for u in us]
    merged = [(jax.nn.sigmoid(ga_ref[p, :].astype(F32)) * ya
               + jax.nn.sigmoid(gb_ref[p, :].astype(F32)) * yb).astype(BF16) for p, ya, yb in zip(parts, yas, ybs)]
    ys = [jnp.dot(m, wo_ref[...], preferred_element_type=F32) for m in merged]
    x2s = [x_ref[p, :] + pos_ref[p, :] + gm_ref[0] * (rms(y) * gpost_ref[...]) for p, y in zip(parts, ys)]
    h2s = [(rms(x2) * gpre_ref[...] * (1.0 + scf_ref[0]) + shf_ref[0]).astype(BF16) for x2 in x2s]
    for p, x2, h2 in zip(parts, x2s, h2s):
        x2_ref[p, :] = x2
        h2_ref[p, :] = h2
    all_logits = [jnp.dot(h2, wr_ref[...], preferred_element_type=F32) + br_ref[...] for h2 in h2s]
    lane = lax.broadcasted_iota(jnp.int32, (sub, LANES), 1).astype(F32)
    ri = lax.broadcasted_iota(jnp.int32, (sub, sub), 0)
    ci = lax.broadcasted_iota(jnp.int32, (sub, sub), 1)
    before = jnp.where(ri > ci, 1.0, 0.0).astype(BF16)

    @pl.when(pl.program_id(0) == 0)
    def _():
        cnt_scr[...] = jnp.zeros_like(cnt_scr)

    routed = []
    for logits in all_logits:
        live = jnp.where(lane < N_EXPERTS, logits, -jnp.inf)
        top_v, top_i = [], []
        for _ in range(TOP_K):
            m = jnp.max(live, axis=-1, keepdims=True)
            idx = jnp.min(jnp.where(live == m, lane, float(LANES)), axis=-1, keepdims=True)
            top_v.append(m)
            top_i.append(idx)
            live = jnp.where(lane == idx, -jnp.inf, live)
        ex = [jnp.exp(v - top_v[0]) for v in top_v]
        denom = ex[0] + ex[1] + ex[2] + ex[3]
        picked = jnp.zeros((sub, LANES), F32)
        for k in range(TOP_K):
            picked = jnp.where(lane == top_i[k], 1.0, picked)
        routed.append((top_i, [e / denom for e in ex], picked,
                       jnp.dot(before, picked.astype(BF16), preferred_element_type=F32)))

    count = cnt_scr[...]
    for j, (p, (top_i, top_w, picked, within)) in enumerate(zip(parts, routed)):
        prefix = within + count[0:1, :]
        ti = jnp.zeros((sub, LANES), F32)
        tw = jnp.zeros((sub, LANES), F32)
        for k in range(TOP_K):
            rank = jnp.sum(jnp.where(lane == top_i[k], prefix, 0.0), axis=-1, keepdims=True)
            ti = jnp.where(lane == k, top_i[k], ti)
            ti = jnp.where(lane == TOP_K + k, rank, ti)
            tw = jnp.where(lane == k, top_w[k], tw)
        ti_ref[:, p] = jnp.transpose(ti)[:SUBLANES, :].astype(jnp.int32)
        tw_ref[:, p] = jnp.transpose(tw)[:SUBLANES, :]
        count = count + jnp.sum(picked, axis=0, keepdims=True)
        cnt_ref[j] = count
    cnt_scr[...] = count


def mixer_output(og, uc, proj, x, pos, mod, mod_row_of_tile, g_post, g_pre_f, ln_g, ln_b,
                 w_a, w_b, w_o, w_r, b_r, *, tm, sub):
    n, d = x.shape
    pos_tiles = pos.shape[0] // tm
    row = lambda i: (i, 0)
    fixed = lambda i: (0, 0)
    modspec = lambda k: pl.BlockSpec((1, 1, d), lambda i: (mod_row_of_tile(i), 0, k))
    vec = pl.BlockSpec((1, d), fixed)
    mat = pl.BlockSpec((d, d), fixed)
    return pl.pallas_call(
        functools.partial(_mixer_out_kernel, sub=sub),
        grid=(n // tm,),
        in_specs=[pl.BlockSpec((tm, d), row), pl.BlockSpec((tm, d), row),
                  pl.BlockSpec((tm, d), lambda i: (i, COL_GATE_A)),
                  pl.BlockSpec((tm, d), lambda i: (i, COL_GATE_B)),
                  pl.BlockSpec((tm, d), row),
                  pl.BlockSpec((tm, d), lambda i: (i % pos_tiles, 0)),
                  modspec(2), modspec(3), modspec(4),
                  vec, vec, vec, vec, mat, mat, mat,
                  pl.BlockSpec((d, LANES), fixed), pl.BlockSpec((1, LANES), fixed)],
        out_specs=[pl.BlockSpec((tm, d), row), pl.BlockSpec((tm, d), row),
                   pl.BlockSpec((SUBLANES, tm), lambda i: (0, i)), pl.BlockSpec((SUBLANES, tm), lambda i: (0, i)),
                   pl.BlockSpec((tm // sub, SUBLANES, LANES), lambda i: (i, 0, 0))],
        out_shape=[jax.ShapeDtypeStruct((n, d), F32), jax.ShapeDtypeStruct((n, d), BF16),
                   jax.ShapeDtypeStruct((SUBLANES, n), jnp.int32), jax.ShapeDtypeStruct((SUBLANES, n), F32),
                   jax.ShapeDtypeStruct((n // sub, SUBLANES, LANES), F32)],
        scratch_shapes=[pltpu.VMEM((SUBLANES, LANES), F32)],
        compiler_params=_params("arbitrary"),
        name="mixer_output",
    )(og, uc, proj, proj, x, pos, mod, mod, mod,
      g_post.reshape(1, d), g_pre_f.reshape(1, d), ln_g.reshape(1, d), ln_b.reshape(1, d),
      w_a, w_b, w_o, w_r, b_r)


EXPERT_BLOCK = 256
MOE_CHUNK = 512
GATHER_SUB = 256
GATHER_GROUP = 3
COMBINE_GRANULE = 32
COMBINE_SLOTS = 32


def _expert_kernel(bexp_ref, bnext_ref, rlo_ref, slo_ref, shi_ref, nused_ref,
                   h2_ref, ti_ref, wt_ref, wgu_hbm, bgu_ref, wd_hbm, bd_ref, o_ref,
                   acc, rw_acc, wgu_bf, wd_bf, wgu_f32, wd_f32, wsem):
    b = pl.program_id(0)
    expert = bexp_ref[b]

    def fetch(e):
        return (pltpu.make_async_copy(wgu_hbm.at[e], wgu_f32, wsem.at[0]),
                pltpu.make_async_copy(wd_hbm.at[e], wd_f32, wsem.at[1]))

    @pl.when(((b == 0) | (expert != bexp_ref[jnp.maximum(b - 1, 0)])) & (b < nused_ref[0]))
    def _():
        @pl.when(b == 0)
        def _():
            for cp in fetch(expert):
                cp.start()

        for cp in fetch(expert):
            cp.wait()
        wgu_bf[...] = wgu_f32[...].astype(BF16)
        wd_bf[...] = wd_f32[...].astype(BF16)

        @pl.when(bnext_ref[b] >= 0)
        def _():
            for cp in fetch(bnext_ref[b]):
                cp.start()

    @pl.when(b < nused_ref[0])
    def _():
        acc[...] = jnp.zeros_like(acc)
        rw_acc[...] = jnp.zeros_like(rw_acc)
        row = lax.broadcasted_iota(jnp.int32, (EXPERT_BLOCK, GATHER_SUB), 0)
        s_lo, s_hi = slo_ref[b], shi_ref[b]
        last_sub = h2_ref.shape[0] // GATHER_SUB - 1

        def select(s, live):
            t0 = pl.multiple_of(s * GATHER_SUB, GATHER_SUB)
            ids = ti_ref[0:TOP_K, pl.ds(t0, GATHER_SUB)]
            local = ti_ref[TOP_K:2 * TOP_K, pl.ds(t0, GATHER_SUB)] - rlo_ref[b]
            if live is not True:
                local = jnp.where(live, local, -1)
            inside = (ids == expert) & (local >= 0) & (local < EXPERT_BLOCK)
            hit_row = jnp.sum(jnp.where(inside, local + 1, 0), axis=0, keepdims=True) - 1
            hit_w = jnp.sum(jnp.where(inside, wt_ref[0:TOP_K, pl.ds(t0, GATHER_SUB)], 0.0), axis=0, keepdims=True)
            hit = row == hit_row
            return (jnp.where(hit, 1.0, 0.0).astype(BF16), jnp.where(hit, hit_w, 0.0),
                    h2_ref[pl.ds(t0, GATHER_SUB), :])

        def gather(i, carry):
            s0 = s_lo + GATHER_GROUP * i
            parts = [select(s0, True)] + [select(jnp.minimum(s0 + q, last_sub), s0 + q <= s_hi)
                                          for q in range(1, GATHER_GROUP)]
            acc[...] += jnp.dot(jnp.concatenate([p for p, _, _ in parts], axis=1),
                                jnp.concatenate([x for _, _, x in parts], axis=0), preferred_element_type=F32)
            rw_acc[...] += jnp.sum(functools.reduce(lambda a, c: a + c, [w for _, w, _ in parts]), axis=1,
                                   keepdims=True)
            return carry

        lax.fori_loop(0, (s_hi - s_lo + GATHER_GROUP) // GATHER_GROUP, gather, 0)
        gu = jnp.dot(acc[...].astype(BF16), wgu_bf[...], preferred_element_type=F32) + bgu_ref[0]
        dff = gu.shape[1] // 2
        gl = jnp.minimum(gu[:, :dff], SWIGLU_LIMIT)
        lin = jnp.clip(gu[:, dff:], -SWIGLU_LIMIT, SWIGLU_LIMIT)
        act = (gl * jax.nn.sigmoid(SWIGLU_ALPHA * gl) * (lin + 1.0)).astype(BF16)
        y = jnp.dot(act, wd_bf[...], preferred_element_type=F32) + bd_ref[0]
        o_ref[...] = (y * rw_acc[...]).astype(o_ref.dtype)

    @pl.when(b >= nused_ref[0])
    def _():
        o_ref[...] = jnp.zeros_like(o_ref)


def expert_blocks(h2, ti_t, wt_t, block_expert, next_expert, rank_lo, sub_lo, sub_hi, n_used, w_gate_up, b_gate_up,
                  w_down, b_down):
    n, d = h2.shape
    e, _, f2 = w_gate_up.shape
    n_blocks = block_expert.shape[0]
    whole = lambda b, be, bn, rl, lo, hi, nu: (0, 0)
    by_expert = lambda b, be, bn, rl, lo, hi, nu: (be[b], 0, 0)
    once = pl.Buffered(1)
    grid_spec = pltpu.PrefetchScalarGridSpec(
        num_scalar_prefetch=6,
        grid=(n_blocks,),
        in_specs=[pl.BlockSpec((n, d), whole, pipeline_mode=once),
                  pl.BlockSpec((SUBLANES, n), whole, pipeline_mode=once),
                  pl.BlockSpec((SUBLANES, n), whole, pipeline_mode=once),
                  pl.BlockSpec(memory_space=pl.ANY),
                  pl.BlockSpec((1, 1, f2), by_expert),
                  pl.BlockSpec(memory_space=pl.ANY),
                  pl.BlockSpec((1, 1, d), by_expert)],
        out_specs=pl.BlockSpec((EXPERT_BLOCK, d), lambda b, be, bn, rl, lo, hi, nu: (b, 0)),
        scratch_shapes=[pltpu.VMEM((EXPERT_BLOCK, d), F32), pltpu.VMEM((EXPERT_BLOCK, 1), F32),
                        pltpu.VMEM((d, f2), BF16), pltpu.VMEM((f2 // 2, d), BF16),
                        pltpu.VMEM((d, f2), F32), pltpu.VMEM((f2 // 2, d), F32),
                        pltpu.SemaphoreType.DMA((2,))],
    )
    return pl.pallas_call(
        _expert_kernel,
        grid_spec=grid_spec,
        out_shape=jax.ShapeDtypeStruct((n_blocks * EXPERT_BLOCK, d), BF16),
        compiler_params=_params("arbitrary"),
        name="expert_blocks",
    )(block_expert, next_expert, rank_lo, sub_lo, sub_hi, n_used, h2, ti_t, wt_t, w_gate_up, b_gate_up.reshape(e, 1, f2), w_down,
      b_down.reshape(e, 1, d))


def _combine_kernel(coff_ref, cgran_ref, cexp_ref, cbase_ref, yb_hbm, ti_ref, x2_ref, gf_ref, g_ref, o_ref,
                    slab, sem, acc):
    c = pl.program_id(0)
    start = coff_ref[c]
    count = coff_ref[c + 1] - start
    n_groups = (count + COMBINE_SLOTS - 1) // COMBINE_SLOTS

    def item(g, s, first=start, n=count):
        return first + jnp.minimum(g * COMBINE_SLOTS + s, n - 1)

    def copies(g, buf, first=start, n=count):
        rows = [pl.multiple_of(cgran_ref[item(g, s, first, n)] * COMBINE_GRANULE, COMBINE_GRANULE)
                for s in range(COMBINE_SLOTS)]
        return [pltpu.make_async_copy(yb_hbm.at[pl.ds(r, COMBINE_GRANULE), :],
                                      slab.at[buf, pl.ds(s * COMBINE_GRANULE, COMBINE_GRANULE), :], sem.at[buf])
                for s, r in enumerate(rows)]

    @pl.when(c == 0)
    def _():
        for cp in copies(0, 0):
            cp.start()

    acc[...] = jnp.zeros_like(acc)
    chunk = x2_ref.shape[0]
    row = lax.broadcasted_iota(jnp.int32, (COMBINE_GRANULE, chunk), 0)
    ids = ti_ref[0:TOP_K, :]
    ranks = ti_ref[TOP_K:2 * TOP_K, :]

    def group(g, carry):
        buf = g % 2

        @pl.when(g + 1 < n_groups)
        def _():
            for cp in copies(g + 1, 1 - buf):
                cp.start()

        for cp in copies(g, buf):
            cp.wait()
        pieces = []
        for s in range(COMBINE_SLOTS):
            live = g * COMBINE_SLOTS + s < count
            local = ranks - cbase_ref[item(g, s)]
            inside = (ids == jnp.where(live, cexp_ref[item(g, s)], -1)) & (local >= 0) & (local < COMBINE_GRANULE)
            hit_row = jnp.sum(jnp.where(inside, local + 1, 0), axis=0, keepdims=True) - 1
            pieces.append(jnp.where(row == hit_row, 1.0, 0.0).astype(BF16))
        onehot_t = jnp.concatenate(pieces, axis=0)
        acc[...] += lax.dot_general(onehot_t, slab[buf], (((0,), (0,)), ((), ())),
                                    preferred_element_type=F32)
        return carry

    lax.fori_loop(0, n_groups, group, 0)

    @pl.when(c + 1 < pl.num_programs(0))
    def _():
        nxt = coff_ref[c + 1]
        for cp in copies(0, 0, nxt, coff_ref[c + 2] - nxt):
            cp.start()

    y = acc[...]
    yn = y * lax.rsqrt(jnp.mean(y * y, axis=-1, keepdims=True) + EPS) * g_ref[...]
    o_ref[...] = x2_ref[...] + gf_ref[0] * yn


def combine_residual(yb, ti_t, x2, chunk_off, chunk_items, mod, mod_row_of_chunk, g_post_f):
    n, d = x2.shape
    chunk_row = lambda c, off, gran, exp, base: (c, 0)
    grid_spec = pltpu.PrefetchScalarGridSpec(
        num_scalar_prefetch=4,
        grid=(n // MOE_CHUNK,),
        in_specs=[pl.BlockSpec(memory_space=pl.ANY),
                  pl.BlockSpec((SUBLANES, MOE_CHUNK), lambda c, off, gran, exp, base: (0, c)),
                  pl.BlockSpec((MOE_CHUNK, d), chunk_row),
                  pl.BlockSpec((1, 1, d), lambda c, off, gran, exp, base: (mod_row_of_chunk(c), 0, 5)),
                  pl.BlockSpec((1, d), lambda c, off, gran, exp, base: (0, 0))],
        out_specs=pl.BlockSpec((MOE_CHUNK, d), chunk_row),
        scratch_shapes=[pltpu.VMEM((2, COMBINE_SLOTS * COMBINE_GRANULE, d), BF16),
                        pltpu.SemaphoreType.DMA((2,)),
                        pltpu.VMEM((MOE_CHUNK, d), F32)],
    )
    return pl.pallas_call(
        _combine_kernel,
        grid_spec=grid_spec,
        out_shape=jax.ShapeDtypeStruct((n, d), F32),
        compiler_params=_params("arbitrary"),
        name="combine_residual",
    )(chunk_off, *chunk_items, yb, ti_t, x2, mod, g_post_f.reshape(1, d))


def _grid_pos_embedding(rows, d):
    quarter = d // 4
    omega = POS_BASE ** (-jnp.arange(quarter, dtype=F32) / quarter)

    def emb(p):
        ang = p[:, None] * omega[None, :]
        return jnp.concatenate([jnp.sin(ang), jnp.cos(ang)], axis=-1)

    row_emb = jnp.repeat(emb(jnp.arange(rows, dtype=F32)), GRID_W, axis=0)
    col_emb = jnp.tile(emb(jnp.arange(GRID_W, dtype=F32)), (rows, 1))
    return jnp.concatenate([row_emb, col_emb], axis=-1)


def _count_le(sorted_vals, queries):
    return jnp.sum(sorted_vals <= queries[:, None], axis=1).astype(jnp.int32)


def _moe_plan(cnt_after, n_tok, tm):
    assert tm == GATHER_SUB
    sub_after = cnt_after[:, 0, :N_EXPERTS].astype(jnp.int32)
    counts = sub_after[-1]
    padded = (counts + EXPERT_BLOCK - 1) // EXPERT_BLOCK * EXPERT_BLOCK
    padded_end = jnp.cumsum(padded)
    padded_start = padded_end - padded
    n_blocks = n_tok * TOP_K // EXPERT_BLOCK + N_EXPERTS
    blocks = jnp.arange(n_blocks, dtype=jnp.int32)
    block_expert = jnp.minimum(_count_le(padded_end[None, :], blocks * EXPERT_BLOCK), N_EXPERTS - 1)
    n_used = (padded_end[-1:] // EXPERT_BLOCK).astype(jnp.int32)
    of_block = (block_expert[:, None] == jnp.arange(N_EXPERTS, dtype=jnp.int32)[None, :]).astype(jnp.int32)
    r_lo = (blocks - jnp.sum(of_block * padded_start[None, :], axis=1) // EXPERT_BLOCK) * EXPERT_BLOCK
    r_hi = jnp.minimum(jnp.sum(of_block * counts[None, :], axis=1), r_lo + EXPERT_BLOCK) - 1
    through = jnp.sum(of_block[:, :, None] * sub_after.T[None, :, :], axis=1)
    after = jnp.sum(of_block * padded_end[None, :], axis=1) // EXPERT_BLOCK
    next_expert = jnp.sum((after[:, None] == blocks[None, :]).astype(jnp.int32) * block_expert[None, :], axis=1)
    next_expert = jnp.where(after < n_used[0], next_expert, -1).astype(jnp.int32)
    n_sub = sub_after.shape[0]
    sub_lo = jnp.minimum(_count_le(through, r_lo), n_sub - 1)
    sub_hi = jnp.minimum(_count_le(through, r_hi), n_sub - 1)
    nch = n_tok // MOE_CHUNK
    per_chunk = MOE_CHUNK // tm
    cb_after = sub_after[per_chunk - 1::per_chunk]
    cb_before = jnp.concatenate([jnp.zeros((1, N_EXPERTS), jnp.int32), cb_after[:-1]], axis=0)
    b_lo = ((padded_start[None, :] + cb_before) // COMBINE_GRANULE).reshape(-1)
    b_hi = ((padded_start[None, :] + cb_after - 1) // COMBINE_GRANULE).reshape(-1)
    n_it = jnp.where((cb_after > cb_before).reshape(-1), b_hi - b_lo + 1, 0)
    off_end = jnp.cumsum(n_it)
    off = off_end - n_it
    n_items = n_blocks * (EXPERT_BLOCK // COMBINE_GRANULE) + N_EXPERTS * (nch - 1)
    w = jnp.minimum(jnp.arange(n_items, dtype=jnp.int32), off_end[-1] - 1)
    chunk_end = off_end[N_EXPERTS - 1::N_EXPERTS]
    chunk_off = jnp.concatenate([jnp.zeros((1,), jnp.int32), chunk_end]).astype(jnp.int32)
    chunk_of = jnp.minimum(_count_le(chunk_end[None, :], w), nch - 1)
    chunk_expert = jnp.minimum(_count_le(off_end.reshape(nch, N_EXPERTS)[chunk_of], w), N_EXPERTS - 1)
    cell = chunk_of * N_EXPERTS + chunk_expert
    chunk_granules = (b_lo[cell] + w - off[cell]).astype(jnp.int32)
    chunk_base = chunk_granules * COMBINE_GRANULE - padded_start[chunk_expert]
    return block_expert, next_expert, r_lo, sub_lo, sub_hi, n_used, chunk_off, (chunk_granules, chunk_expert, chunk_base)


def kernel(x, c, ctx, c_ctx, w_mod, b_mod, g_pre_mix, g_post_mix, g_pre_ffn, g_post_ffn, w_in, conv_kv,
           conv_q, a_log, dt_bias, gdn_norm_g, w_proj_a, conf_dw, conf_dw_b, conf_ln_g, conf_ln_b,
           w_proj_b, w_out, w_router, b_router, w_gate_up, b_gate_up, w_down, b_down):
    batch, seq, d = x.shape
    ctx_len = ctx.shape[1]
    n_tok = batch * seq
    gw = HEADS * DV
    beta_off = HEADS * DK + gw
    state_cols = beta_off + 4 * HEADS
    q_off = state_cols

    mod_rows = 2 * SUBLANES
    cond = jnp.zeros((mod_rows, d), F32).at[:batch].set(c).at[batch].set(c_ctx)
    mod = modulation(cond, w_mod[0], b_mod[0]).reshape(mod_rows, 1, 6 * d)

    w_t = jnp.swapaxes(w_in[0], 0, 1)
    w_main = projection_weights(w_t, keep=beta_off, skip=state_cols - beta_off)
    w_ba = jnp.zeros((LANES, d), F32).at[:4 * HEADS].set(w_t[beta_off:state_cols]).astype(BF16)

    pos = _grid_pos_embedding(seq // GRID_W, d)
    x_flat = x.reshape(n_tok, d)
    tm = 1024
    tiles_per_seq = seq // tm
    proj, ba = input_projection(x_flat, pos, mod, lambda i: i // tiles_per_seq, g_pre_mix[0],
                                w_main, w_ba, tm=tm, tn=2048)
    proj_ctx, ba_ctx = input_projection(ctx.reshape(batch * ctx_len, d), None, mod, lambda i: batch,
                                        g_pre_mix[0], w_main[:beta_off], w_ba, tm=ctx_len, tn=1024)

    og = gated_deltanet(proj, ba, proj_ctx, ba_ctx, conv_kv[0], conv_q[0], a_log[0], dt_bias[0],
                        gdn_norm_g[0], batch=batch, seq=seq, ctx=ctx_len)
    uc = conformer_conv(proj, conf_dw[0], conf_dw_b[0], batch=batch, seq=seq)

    tm2 = 512
    w_r = jnp.zeros((d, LANES), F32).at[:, :N_EXPERTS].set(w_router[0]).astype(BF16)
    b_r = jnp.zeros((1, LANES), F32).at[0, :N_EXPERTS].set(b_router[0])
    x2, h2, top_i, top_w, cnt_after = mixer_output(
        og, uc, proj, x_flat, pos, mod, lambda i: i // (seq // tm2), g_post_mix[0], g_pre_ffn[0],
        conf_ln_g[0], conf_ln_b[0], w_proj_a[0].astype(BF16), w_proj_b[0].astype(BF16),
        w_out[0].astype(BF16), w_r, b_r, tm=tm2, sub=GATHER_SUB)

    block_expert, next_expert, rank_lo, sub_lo, sub_hi, n_used, chunk_off, chunk_items = _moe_plan(
        cnt_after, n_tok, GATHER_SUB)
    yb = expert_blocks(h2, top_i, top_w, block_expert, next_expert, rank_lo, sub_lo, sub_hi, n_used, w_gate_up[0],
                       b_gate_up[0], w_down[0], b_down[0])
    out = combine_residual(yb, top_i, x2, chunk_off, chunk_items, mod, lambda c: c // (seq // MOE_CHUNK),
                           g_post_ffn[0])
    return out.reshape(batch, seq, d)
```

```python
import functools

import jax
import jax.numpy as jnp
from jax import lax
from jax.experimental import pallas as pl
from jax.experimental.pallas import tpu as pltpu

F32 = jnp.float32
BF16 = jnp.bfloat16

D_MODEL = 1024
GRID_W = 64
HEADS = 8
DK = 128
DV = 128
SHORT_CONV = 5
CHUNK = 64
CONF_KERNEL = 31
N_EXPERTS = 32
TOP_K = 4
SWIGLU_LIMIT = 7.0
SWIGLU_ALPHA = 1.702
MOE_BLOCK = 128
EPS = 1e-6
POS_BASE = 10000.0

LANES = 128
SUBLANES = 8
VMEM_LIMIT = 60 * 1024 * 1024

COL_K, COL_V, COL_Q, COL_Z, COL_GLU_A, COL_GLU_G, COL_GATE_A, COL_GATE_B = range(8)


def _params(*sem):
    return pltpu.CompilerParams(dimension_semantics=sem, vmem_limit_bytes=VMEM_LIMIT)


def _mod_kernel(c_ref, w_ref, b_ref, o_ref):
    c = c_ref[...]
    s = c * jax.nn.sigmoid(c)
    o_ref[...] = jnp.dot(s, w_ref[...], preferred_element_type=F32,
                         precision=lax.Precision.HIGHEST) + b_ref[...]


def modulation(cond, w_mod, b_mod):
    r, d = cond.shape
    n = w_mod.shape[1]
    tn = 1024
    return pl.pallas_call(
        _mod_kernel,
        grid=(n // tn,),
        in_specs=[pl.BlockSpec((r, d), lambda j: (0, 0)),
                  pl.BlockSpec((d, tn), lambda j: (0, j)),
                  pl.BlockSpec((1, tn), lambda j: (0, j))],
        out_specs=pl.BlockSpec((r, tn), lambda j: (0, j)),
        out_shape=jax.ShapeDtypeStruct((r, n), F32),
        compiler_params=_params("arbitrary"),
        name="modulation",
    )(cond, w_mod, b_mod.reshape(1, n))


def _proj_weights_kernel(a_ref, b_ref, o_ref, *, first_shifted, shift):
    @pl.when(pl.program_id(0) < first_shifted)
    def _():
        o_ref[...] = a_ref[...].astype(o_ref.dtype)

    @pl.when(pl.program_id(0) >= first_shifted)
    def _():
        o_ref[...] = jnp.concatenate([a_ref[shift:, :], b_ref[...]], axis=0).astype(o_ref.dtype)


def projection_weights(w_t, *, keep, skip, tr=1024):
    total, d = w_t.shape
    n_out = total - skip
    assert keep % tr == 0 and n_out % tr == 0 and tr % skip == 0 and skip % SUBLANES == 0
    return pl.pallas_call(
        functools.partial(_proj_weights_kernel, first_shifted=keep // tr, shift=skip),
        grid=(n_out // tr,),
        in_specs=[pl.BlockSpec((tr, d), lambda j: (j, 0)),
                  pl.BlockSpec((skip, d), lambda j: ((j + 1) * (tr // skip), 0))],
        out_specs=pl.BlockSpec((tr, d), lambda j: (j, 0)),
        out_shape=jax.ShapeDtypeStruct((n_out, d), BF16),
        compiler_params=_params("arbitrary"),
        name="projection_weights",
    )(w_t, w_t)


_NT = (((1,), (1,)), ((), ()))


def _inproj_kernel(*refs, has_pos):
    if has_pos:
        x_ref, pos_ref, sh_ref, sc_ref, g_ref, w_ref, wba_ref, o_ref, ba_ref, h_scr = refs
    else:
        x_ref, sh_ref, sc_ref, g_ref, w_ref, wba_ref, o_ref, ba_ref, h_scr = refs

    @pl.when(pl.program_id(1) == 0)
    def _():
        x = x_ref[...]
        if has_pos:
            x = x + pos_ref[...]
        y = x * lax.rsqrt(jnp.mean(x * x, axis=-1, keepdims=True) + EPS) * g_ref[...]
        h = (y * (1.0 + sc_ref[0]) + sh_ref[0]).astype(BF16)
        h_scr[...] = h
        ba_ref[...] = lax.dot_general(h, wba_ref[...], _NT, preferred_element_type=F32)

    o_ref[...] = lax.dot_general(h_scr[...], w_ref[...], _NT, preferred_element_type=F32).astype(o_ref.dtype)


def input_projection(x, pos, mod, mod_row_of_tile, g_pre, w_main, w_ba, *, tm, tn):
    n, d = x.shape
    w = w_main.shape[0]
    has_pos = pos is not None
    in_specs = [pl.BlockSpec((tm, d), lambda i, j: (i, 0))]
    args = [x]
    if has_pos:
        pos_tiles = pos.shape[0] // tm
        in_specs.append(pl.BlockSpec((tm, d), lambda i, j: (i % pos_tiles, 0)))
        args.append(pos)
    in_specs += [
        pl.BlockSpec((1, 1, d), lambda i, j: (mod_row_of_tile(i), 0, 0)),
        pl.BlockSpec((1, 1, d), lambda i, j: (mod_row_of_tile(i), 0, 1)),
        pl.BlockSpec((1, d), lambda i, j: (0, 0)),
        pl.BlockSpec((tn, d), lambda i, j: (j, 0)),
        pl.BlockSpec((LANES, d), lambda i, j: (0, 0)),
    ]
    args += [mod, mod, g_pre.reshape(1, d), w_main, w_ba]
    return pl.pallas_call(
        functools.partial(_inproj_kernel, has_pos=has_pos),
        grid=(n // tm, w // tn),
        in_specs=in_specs,
        out_specs=[pl.BlockSpec((tm, tn), lambda i, j: (i, j)),
                   pl.BlockSpec((tm, LANES), lambda i, j: (i, 0))],
        out_shape=[jax.ShapeDtypeStruct((n, w), BF16), jax.ShapeDtypeStruct((n, LANES), F32)],
        scratch_shapes=[pltpu.VMEM((tm, d), BF16)],
        compiler_params=_params("arbitrary", "arbitrary"),
        name="input_projection_pos" if has_pos else "input_projection_ctx",
    )(*args)


def _silu(x):
    return x * jax.nn.sigmoid(x)


def _softplus(x):
    return jnp.maximum(x, 0.0) + jnp.log(1.0 + jnp.exp(-jnp.abs(x)))


def _short_conv(src, taps, pad_scr, rows):
    half = SHORT_CONV // 2
    pad_scr[0:SUBLANES, :] = jnp.zeros((SUBLANES, LANES), F32)
    pad_scr[SUBLANES:SUBLANES + rows, :] = src.astype(F32)
    pad_scr[SUBLANES + rows:2 * SUBLANES + rows, :] = jnp.zeros((SUBLANES, LANES), F32)
    acc = None
    for j in range(SHORT_CONV):
        off = SUBLANES + j - half
        term = pad_scr[off:off + rows, :] * taps[j:j + 1, :]
        acc = term if acc is None else acc + term
    return _silu(acc)


def _l2n(x):
    return x * lax.rsqrt(jnp.sum(x * x, axis=-1, keepdims=True) + EPS)


def _gdn_kernel(proj_hbm, ba_hbm, kc_ref, vc_ref, bac_ref,
                wk_ref, wv_ref, wq_ref, ab_ref, gn_ref, o_ref,
                k_ref, v_ref, q_ref, z_ref, ba_ref, in_sem,
                kf, vf, qf, gates, pad_scr, lhs, val, att, ktt, dec, osc, *, seq, ctx, unroll, group):
    total = ctx + seq
    n_batch, n_groups = pl.num_programs(0), pl.num_programs(1)
    gl = group * LANES

    def block(bb, hg, base):
        return proj_hbm.at[pl.ds(pl.multiple_of(bb * seq, seq), seq),
                           pl.ds(pl.multiple_of((base * n_groups + hg) * gl, gl), gl)]

    def input_copies(bb, hg):
        return [pltpu.make_async_copy(block(bb, hg, COL_K), k_ref, in_sem.at[0]),
                pltpu.make_async_copy(block(bb, hg, COL_V), v_ref, in_sem.at[1]),
                pltpu.make_async_copy(block(bb, hg, COL_Q), q_ref, in_sem.at[2]),
                pltpu.make_async_copy(ba_hbm.at[pl.ds(pl.multiple_of(bb * seq, seq), seq), :], ba_ref, in_sem.at[3])]

    b_now, g_now = pl.program_id(0), pl.program_id(1)
    z_copy = pltpu.make_async_copy(block(b_now, g_now, COL_Z), z_ref, in_sem.at[4])

    @pl.when((b_now == 0) & (g_now == 0))
    def _():
        for cp in input_copies(b_now, g_now):
            cp.start()

    for cp in input_copies(b_now, g_now):
        cp.wait()
    z_copy.start()
    nc = total // CHUNK
    ncc = ctx // CHUNK
    two = 2 * CHUNK
    lg_chunk = CHUNK.bit_length() - 1

    ri = lax.broadcasted_iota(jnp.int32, (two, two), 0)
    ci = lax.broadcasted_iota(jnp.int32, (two, two), 1)
    same_dir = (ri >> lg_chunk) == (ci >> lg_chunk)
    incl = same_dir & (((ri < CHUNK) & (ri >= ci)) | ((ri >= CHUNK) & (ri <= ci)))
    strict = incl & (ri != ci)
    eye = jnp.where(ri == ci, 1.0, 0.0)
    tri = jnp.where(incl, 1.0, 0.0).astype(BF16)
    level_masks = []
    for lg in range(lg_chunk):
        same_parent = (ri >> (lg + 1)) == (ci >> (lg + 1))
        level_masks.append(jnp.where(same_parent & ((ri >> lg) != (ci >> lg)), 1.0, 0.0))
    top_rows = lax.broadcasted_iota(jnp.int32, (two, LANES), 0) < CHUNK

    def bwd_chunk(t):
        return jnp.where(t < ncc, ncc - 1 - t, nc + ncc - 1 - t)

    def stacked(ref, rf, rb):
        return jnp.concatenate([ref[pl.ds(rf, CHUNK), :], ref[pl.ds(rb, CHUNK), :]], axis=0)

    def chunk_load(t):
        rf = pl.multiple_of(t * CHUNK, CHUNK)
        rb = pl.multiple_of(bwd_chunk(t) * CHUNK, CHUNK)
        beta = jnp.concatenate([gates[0, pl.ds(rf, CHUNK), :], gates[1, pl.ds(rb, CHUNK), :]], axis=0)
        g = jnp.concatenate([gates[2, pl.ds(rf, CHUNK), :], gates[3, pl.ds(rb, CHUNK), :]], axis=0)
        return stacked(kf, rf, rb), stacked(vf, rf, rb), stacked(qf, rf, rb), beta, g

    def chunk_triangle(k, q, beta, g):
        kb = k.astype(BF16)
        kq = jnp.concatenate([kb, q.astype(BF16)], axis=0)
        kk_qk = lax.dot_general(kq, kb, (((1,), (1,)), ((), ())), preferred_element_type=F32)
        g_hi = g.astype(BF16)
        r1 = g - g_hi.astype(F32)
        g_mid = r1.astype(BF16)
        g_lo = (r1 - g_mid.astype(F32)).astype(BF16)
        gc3 = jnp.dot(tri, jnp.concatenate([g_hi, g_mid, g_lo], axis=1), preferred_element_type=F32)
        gc = gc3[:, :LANES] + gc3[:, LANES:2 * LANES] + gc3[:, 2 * LANES:]
        gc_row = jnp.transpose(gc)
        decay = jnp.where(incl, jnp.exp(gc - gc_row), 0.0)
        a = jnp.where(strict, beta * kk_qk[:two] * decay, 0.0)
        return a, (kk_qk[two:] * decay).astype(BF16), gc

    def chunk_finish(hh, t, k, v, q, beta, gc, attn, tmat):
        eg = jnp.exp(gc)
        rhs = jnp.concatenate([v * beta, k * beta * eg], axis=1).astype(BF16)
        sol = jnp.dot(tmat.astype(BF16), rhs, preferred_element_type=F32)
        g_end = jnp.where(top_rows, jnp.broadcast_to(gc[CHUNK - 1:CHUNK, :], (two, LANES)),
                          jnp.broadcast_to(gc[CHUNK:CHUNK + 1, :], (two, LANES)))
        k_tail = k * jnp.exp(g_end - gc)
        qg = (q * eg).astype(BF16)
        kcum = sol[:, DV:].astype(BF16)
        r0 = pl.multiple_of(t * two, two)
        lhs[hh, 0, pl.ds(r0, two), :] = jnp.concatenate([kcum[:CHUNK], qg[:CHUNK]], axis=0)
        lhs[hh, 1, pl.ds(r0, two), :] = jnp.concatenate([kcum[CHUNK:], qg[CHUNK:]], axis=0)
        val[hh, pl.ds(r0, two), :] = sol[:, :DV].astype(BF16)
        att[hh, pl.ds(r0, two), :] = attn
        ktt[hh, pl.ds(r0, two), :] = jnp.transpose(k_tail).astype(BF16)
        d0 = pl.multiple_of(t * SUBLANES, SUBLANES)
        e_end = jnp.exp(g_end)
        dec[hh, 0, pl.ds(d0, SUBLANES), :] = e_end[:SUBLANES]
        dec[hh, 1, pl.ds(d0, SUBLANES), :] = e_end[CHUNK:CHUNK + SUBLANES]

    for hh in range(group):
        head = pl.program_id(1) * group + hh
        cols = slice(hh * LANES, (hh + 1) * LANES)

        kf[0:ctx, :] = _l2n(_short_conv(kc_ref[:, cols], wk_ref[:, cols], pad_scr, ctx))
        vf[0:ctx, :] = _short_conv(vc_ref[:, cols], wv_ref[:, cols], pad_scr, ctx)
        qf[0:ctx, :] = jnp.zeros((ctx, LANES), F32)
        kf[ctx:total, :] = _l2n(_short_conv(k_ref[:, cols], wk_ref[:, cols], pad_scr, seq))
        vf[ctx:total, :] = _short_conv(v_ref[:, cols], wv_ref[:, cols], pad_scr, seq)
        qf[ctx:total, :] = _l2n(_short_conv(q_ref[:, cols], wq_ref[:, cols], pad_scr, seq)) * (DK ** -0.5)

        def gate_cols(src_ref, lo, rows):
            x = src_ref[...]
            lane = lax.broadcasted_iota(jnp.int32, (rows, LANES), 1)
            beta = jax.nn.sigmoid(x)
            g = -jnp.exp(ab_ref[0:1, :]) * _softplus(x + ab_ref[1:2, :])
            for slot, (arr, base) in enumerate(((beta, 0), (beta, HEADS), (g, 2 * HEADS), (g, 3 * HEADS))):
                col = jnp.sum(jnp.where(lane == base + head, arr, 0.0), axis=-1, keepdims=True)
                gates[slot, lo:lo + rows, :] = jnp.broadcast_to(col, (rows, LANES))

        gate_cols(bac_ref, 0, ctx)
        gate_cols(ba_ref, ctx, seq)

        def chunk_body(i, carry, hh=hh):
            steps = [i * unroll + u for u in range(unroll)]
            loaded = [chunk_load(t) for t in steps]
            tris = [chunk_triangle(k, q, beta, g) for (k, v, q, beta, g) in loaded]
            tmats = [eye - a * level_masks[0] for (a, _, _) in tris]
            for m in level_masks[1:]:
                nxt = []
                for (a, _, _), tmat in zip(tris, tmats):
                    cs = (a * m).astype(BF16)
                    tb = tmat.astype(BF16)
                    tc = jnp.dot(tb, cs, preferred_element_type=F32)
                    nxt.append(tmat - jnp.dot(tc.astype(BF16), tb, preferred_element_type=F32))
                tmats = nxt
            for t, (k, v, q, beta, g), (a, attn, gc), tmat in zip(steps, loaded, tris, tmats):
                chunk_finish(hh, t, k, v, q, beta, gc, attn, tmat)
            return carry

        lax.fori_loop(0, nc // unroll, chunk_body, 0)

    wrap = g_now + 1 == n_groups

    @pl.when(jnp.logical_not(wrap & (b_now + 1 == n_batch)))
    def _():
        for cp in input_copies(jnp.where(wrap, b_now + 1, b_now), jnp.where(wrap, 0, g_now + 1)):
            cp.start()

    def scan_body(t, carry):
        rf = pl.multiple_of(t * CHUNK, CHUNK)
        rb = pl.multiple_of(bwd_chunk(t) * CHUNK, CHUNK)
        r0 = pl.multiple_of(t * two, two)
        d0 = pl.multiple_of(t * SUBLANES, SUBLANES)
        rs = [(jnp.dot(lhs[hh, 0, pl.ds(r0, two), :], carry[2 * hh].astype(BF16), preferred_element_type=F32),
               jnp.dot(lhs[hh, 1, pl.ds(r0, two), :], carry[2 * hh + 1].astype(BF16), preferred_element_type=F32))
              for hh in range(group)]
        v_news = [val[hh, pl.ds(r0, two), :].astype(F32) - jnp.concatenate([r_f[:CHUNK], r_b[:CHUNK]], axis=0)
                  for hh, (r_f, r_b) in enumerate(rs)]
        outs, new_states = [], []
        for hh, ((r_f, r_b), v_new) in enumerate(zip(rs, v_news)):
            zeros = jnp.zeros_like(v_new)
            v_bd = jnp.concatenate([jnp.where(top_rows, v_new, zeros), jnp.where(top_rows, zeros, v_new)],
                                   axis=1).astype(BF16)
            upd = jnp.dot(ktt[hh, pl.ds(r0, two), :], v_bd, preferred_element_type=F32)
            new_states.append(carry[2 * hh] * dec[hh, 0, pl.ds(d0, SUBLANES), :][0:1, :] + upd[:, :DV])
            new_states.append(carry[2 * hh + 1] * dec[hh, 1, pl.ds(d0, SUBLANES), :][0:1, :] + upd[:, DV:])
            outs.append(jnp.concatenate([r_f[CHUNK:], r_b[CHUNK:]], axis=0)
                        + jnp.dot(att[hh, pl.ds(r0, two), :], v_new.astype(BF16), preferred_element_type=F32))
        for hh, o in enumerate(outs):
            osc[hh, pl.ds(rf, CHUNK), :] += o[:CHUNK]
            osc[hh, pl.ds(rb, CHUNK), :] += o[CHUNK:]
        return tuple(new_states)

    osc[...] = jnp.zeros_like(osc)
    zero = jnp.zeros((DK, DV), F32)
    lax.fori_loop(0, nc, scan_body, (zero,) * (2 * group))

    z_copy.wait()
    for hh in range(group):
        cols = slice(hh * LANES, (hh + 1) * LANES)
        o = osc[hh, ctx:total, :]
        o = o * lax.rsqrt(jnp.mean(o * o, axis=-1, keepdims=True) + EPS) * gn_ref[...]
        o_ref[:, cols] = (o * _silu(z_ref[:, cols].astype(F32))).astype(o_ref.dtype)


def gated_deltanet(proj, ba, proj_ctx, ba_ctx, conv_kv, conv_q, a_log, dt_bias, gn, *, batch, seq, ctx,
                   unroll=18, group=4):
    total = seq + ctx
    nc = total // CHUNK
    gl = group * LANES
    ng = HEADS // group
    taps = jnp.zeros((SUBLANES, conv_kv.shape[1]), F32).at[:SHORT_CONV].set(conv_kv)
    taps_q = jnp.zeros((SUBLANES, conv_q.shape[1]), F32).at[:SHORT_CONV].set(conv_q)
    ab = jnp.zeros((SUBLANES, LANES), F32)
    ab = ab.at[0, 2 * HEADS:4 * HEADS].set(a_log.reshape(-1)).at[1, 2 * HEADS:4 * HEADS].set(dt_bias.reshape(-1))
    col = lambda base: (lambda b, h: (b, base * ng + h))
    return pl.pallas_call(
        functools.partial(_gdn_kernel, seq=seq, ctx=ctx, unroll=unroll, group=group),
        grid=(batch, ng),
        in_specs=[
            pl.BlockSpec(memory_space=pl.ANY),
            pl.BlockSpec(memory_space=pl.ANY),
            pl.BlockSpec((ctx, gl), col(COL_K)),
            pl.BlockSpec((ctx, gl), col(COL_V)),
            pl.BlockSpec((ctx, LANES), lambda b, h: (b, 0)),
            pl.BlockSpec((SUBLANES, gl), lambda b, h: (0, h)),
            pl.BlockSpec((SUBLANES, gl), lambda b, h: (0, ng + h)),
            pl.BlockSpec((SUBLANES, gl), lambda b, h: (0, h)),
            pl.BlockSpec((SUBLANES, LANES), lambda b, h: (0, 0)),
            pl.BlockSpec((1, LANES), lambda b, h: (0, 0)),
        ],
        out_specs=pl.BlockSpec((seq, gl), lambda b, h: (b, h)),
        out_shape=jax.ShapeDtypeStruct((batch * seq, HEADS * DV), BF16),
        scratch_shapes=[
            pltpu.VMEM((seq, gl), BF16),
            pltpu.VMEM((seq, gl), BF16),
            pltpu.VMEM((seq, gl), BF16),
            pltpu.VMEM((seq, gl), BF16),
            pltpu.VMEM((seq, LANES), F32),
            pltpu.SemaphoreType.DMA((5,)),
            pltpu.VMEM((total, LANES), F32),
            pltpu.VMEM((total, LANES), F32),
            pltpu.VMEM((total, LANES), F32),
            pltpu.VMEM((4, total, LANES), F32),
            pltpu.VMEM((seq + 2 * SUBLANES, LANES), F32),
            pltpu.VMEM((group, 2, nc * 2 * CHUNK, LANES), BF16),
            pltpu.VMEM((group, nc * 2 * CHUNK, LANES), BF16),
            pltpu.VMEM((group, nc * 2 * CHUNK, LANES), BF16),
            pltpu.VMEM((group, nc * 2 * CHUNK, LANES), BF16),
            pltpu.VMEM((group, 2, nc * SUBLANES, LANES), F32),
            pltpu.VMEM((group, total, LANES), F32),
        ],
        compiler_params=_params("arbitrary", "arbitrary"),
        name="gated_deltanet",
    )(proj, ba, proj_ctx, proj_ctx, ba_ctx, taps, taps, taps_q, ab, gn.reshape(1, DV))


CONF_ROWS = 128
CONF_HALO = 16


def _conf_kernel(a_ref, g_ref, w_ref, b_ref, o_ref, pad_scr, *, seq):
    cb = a_ref.shape[1]
    u = a_ref[...].astype(F32) * jax.nn.sigmoid(g_ref[...].astype(F32))
    pad_scr[0:CONF_HALO, :] = jnp.zeros((CONF_HALO, cb), F32)
    pad_scr[CONF_HALO:CONF_HALO + seq, :] = u
    pad_scr[CONF_HALO + seq:2 * CONF_HALO + seq, :] = jnp.zeros((CONF_HALO, cb), F32)
    win = CONF_ROWS + 2 * CONF_HALO
    first = CONF_HALO - CONF_KERNEL // 2

    def body(i, carry):
        r0 = pl.multiple_of(i * CONF_ROWS, CONF_ROWS)
        w = pad_scr[pl.ds(r0, win), :]
        acc = jnp.zeros((CONF_ROWS, cb), F32) + b_ref[...]
        for sub in range(SUBLANES):
            shifted = w if sub == 0 else pltpu.roll(w, win - sub, axis=0)
            for j in range(CONF_KERNEL):
                off = first + j
                if off % SUBLANES == sub:
                    base = off - sub
                    acc = acc + shifted[base:base + CONF_ROWS, :] * w_ref[j:j + 1, :]
        o_ref[pl.ds(r0, CONF_ROWS), :] = acc.astype(o_ref.dtype)
        return carry

    lax.fori_loop(0, seq // CONF_ROWS, body, 0)


def conformer_conv(proj, conf_dw, conf_dw_b, *, batch, seq, cb=256):
    c = conf_dw.shape[1]
    nb = c // cb
    taps = jnp.zeros((32, c), F32).at[:CONF_KERNEL].set(conf_dw)
    return pl.pallas_call(
        functools.partial(_conf_kernel, seq=seq),
        grid=(batch, nb),
        in_specs=[pl.BlockSpec((seq, cb), lambda b, j: (b, COL_GLU_A * nb + j)),
                  pl.BlockSpec((seq, cb), lambda b, j: (b, COL_GLU_G * nb + j)),
                  pl.BlockSpec((32, cb), lambda b, j: (0, j)),
                  pl.BlockSpec((1, cb), lambda b, j: (0, j))],
        out_specs=pl.BlockSpec((seq, cb), lambda b, j: (b, j)),
        out_shape=jax.ShapeDtypeStruct((batch * seq, c), BF16),
        scratch_shapes=[pltpu.VMEM((seq + 2 * CONF_HALO, cb), F32)],
        compiler_params=_params("arbitrary", "arbitrary"),
        name="conformer_conv",
    )(proj, proj, taps, conf_dw_b.reshape(1, c))


def _mixer_out_kernel(og_ref, uc_ref, ga_ref, gb_ref, x_ref, pos_ref, gm_ref, shf_ref, scf_ref,
                      gpost_ref, gpre_ref, lng_ref, lnb_ref, wa_ref, wb_ref, wo_ref, wr_ref, br_ref,
                      x2_ref, h2_ref, ti_ref, tw_ref, cnt_ref, cnt_scr, *, sub):
    parts = [slice(r, r + sub) for r in range(0, og_ref.shape[0], sub)]

    def rms(v):
        return v * lax.rsqrt(jnp.mean(v * v, axis=-1, keepdims=True) + EPS)

    def layer_norm_silu(uc):
        mu = jnp.mean(uc, axis=-1, keepdims=True)
        var = jnp.mean(jnp.square(uc - mu), axis=-1, keepdims=True)
        return _silu((uc - mu) * lax.rsqrt(var + EPS) * lng_ref[...] + lnb_ref[...]).astype(BF16)

    us = [layer_norm_silu(uc_ref[p, :].astype(F32)) for p in parts]
    yas = [jnp.dot(og_ref[p, :], wa_ref[...], preferred_element_type=F32) for p in parts]
    ybs = [jnp.dot(u, wb_ref[...], preferred_element_type=F32) for u in us]
    merged = [(jax.nn.sigmoid(ga_ref[p, :].astype(F32)) * ya
               + jax.nn.sigmoid(gb_ref[p, :].astype(F32)) * yb).astype(BF16) for p, ya, yb in zip(parts, yas, ybs)]
    ys = [jnp.dot(m, wo_ref[...], preferred_element_type=F32) for m in merged]
    x2s = [x_ref[p, :] + pos_ref[p, :] + gm_ref[0] * (rms(y) * gpost_ref[...]) for p, y in zip(parts, ys)]
    h2s = [(rms(x2) * gpre_ref[...] * (1.0 + scf_ref[0]) + shf_ref[0]).astype(BF16) for x2 in x2s]
    for p, x2, h2 in zip(parts, x2s, h2s):
        x2_ref[p, :] = x2
        h2_ref[p, :] = h2
    all_logits = [jnp.dot(h2, wr_ref[...], preferred_element_type=F32) + br_ref[...] for h2 in h2s]
    lane = lax.broadcasted_iota(jnp.int32, (sub, LANES), 1).astype(F32)
    ri = lax.broadcasted_iota(jnp.int32, (sub, sub), 0)
    ci = lax.broadcasted_iota(jnp.int32, (sub, sub), 1)
    before = jnp.where(ri > ci, 1.0, 0.0).astype(BF16)

    @pl.when(pl.program_id(0) == 0)
    def _():
        cnt_scr[...] = jnp.zeros_like(cnt_scr)

    routed = []
    for logits in all_logits:
        live = jnp.where(lane < N_EXPERTS, logits, -jnp.inf)
        top_v, top_i = [], []
        for _ in range(TOP_K):
            m = jnp.max(live, axis=-1, keepdims=True)
            idx = jnp.min(jnp.where(live == m, lane, float(LANES)), axis=-1, keepdims=True)
            top_v.append(m)
            top_i.append(idx)
            live = jnp.where(lane == idx, -jnp.inf, live)
        ex = [jnp.exp(v - top_v[0]) for v in top_v]
        denom = ex[0] + ex[1] + ex[2] + ex[3]
        picked = jnp.zeros((sub, LANES), F32)
        for k in range(TOP_K):
            picked = jnp.where(lane == top_i[k], 1.0, picked)
        routed.append((top_i, [e / denom for e in ex], picked,
                       jnp.dot(before, picked.astype(BF16), preferred_element_type=F32)))

    count = cnt_scr[...]
    for j, (p, (top_i, top_w, picked, within)) in enumerate(zip(parts, routed)):
        prefix = within + count[0:1, :]
        ti = jnp.zeros((sub, LANES), F32)
        tw = jnp.zeros((sub, LANES), F32)
        for k in range(TOP_K):
            rank = jnp.sum(jnp.where(lane == top_i[k], prefix, 0.0), axis=-1, keepdims=True)
            ti = jnp.where(lane == k, top_i[k], ti)
            ti = jnp.where(lane == TOP_K + k, rank, ti)
            tw = jnp.where(lane == k, top_w[k], tw)
        ti_ref[:, p] = jnp.transpose(ti)[:SUBLANES, :].astype(jnp.int32)
        tw_ref[:, p] = jnp.transpose(tw)[:SUBLANES, :]
        count = count + jnp.sum(picked, axis=0, keepdims=True)
        cnt_ref[j] = count
    cnt_scr[...] = count


def mixer_output(og, uc, proj, x, pos, mod, mod_row_of_tile, g_post, g_pre_f, ln_g, ln_b,
                 w_a, w_b, w_o, w_r, b_r, *, tm, sub):
    n, d = x.shape
    pos_tiles = pos.shape[0] // tm
    row = lambda i: (i, 0)
    fixed = lambda i: (0, 0)
    modspec = lambda k: pl.BlockSpec((1, 1, d), lambda i: (mod_row_of_tile(i), 0, k))
    vec = pl.BlockSpec((1, d), fixed)
    mat = pl.BlockSpec((d, d), fixed)
    return pl.pallas_call(
        functools.partial(_mixer_out_kernel, sub=sub),
        grid=(n // tm,),
        in_specs=[pl.BlockSpec((tm, d), row), pl.BlockSpec((tm, d), row),
                  pl.BlockSpec((tm, d), lambda i: (i, COL_GATE_A)),
                  pl.BlockSpec((tm, d), lambda i: (i, COL_GATE_B)),
                  pl.BlockSpec((tm, d), row),
                  pl.BlockSpec((tm, d), lambda i: (i % pos_tiles, 0)),
                  modspec(2), modspec(3), modspec(4),
                  vec, vec, vec, vec, mat, mat, mat,
                  pl.BlockSpec((d, LANES), fixed), pl.BlockSpec((1, LANES), fixed)],
        out_specs=[pl.BlockSpec((tm, d), row), pl.BlockSpec((tm, d), row),
                   pl.BlockSpec((SUBLANES, tm), lambda i: (0, i)), pl.BlockSpec((SUBLANES, tm), lambda i: (0, i)),
                   pl.BlockSpec((tm // sub, SUBLANES, LANES), lambda i: (i, 0, 0))],
        out_shape=[jax.ShapeDtypeStruct((n, d), F32), jax.ShapeDtypeStruct((n, d), BF16),
                   jax.ShapeDtypeStruct((SUBLANES, n), jnp.int32), jax.ShapeDtypeStruct((SUBLANES, n), F32),
                   jax.ShapeDtypeStruct((n // sub, SUBLANES, LANES), F32)],
        scratch_shapes=[pltpu.VMEM((SUBLANES, LANES), F32)],
        compiler_params=_params("arbitrary"),
        name="mixer_output",
    )(og, uc, proj, proj, x, pos, mod, mod, mod,
      g_post.reshape(1, d), g_pre_f.reshape(1, d), ln_g.reshape(1, d), ln_b.reshape(1, d),
      w_a, w_b, w_o, w_r, b_r)


EXPERT_BLOCK = 256
MOE_CHUNK = 512
GATHER_SUB = 256
GATHER_GROUP = 3
COMBINE_GRANULE = 32
COMBINE_SLOTS = 32


def _expert_kernel(bexp_ref, bnext_ref, rlo_ref, slo_ref, shi_ref, nused_ref,
                   h2_ref, ti_ref, wt_ref, wgu_hbm, bgu_ref, wd_hbm, bd_ref, o_ref,
                   acc, rw_acc, wgu_bf, wd_bf, wgu_f32, wd_f32, wsem):
    b = pl.program_id(0)
    expert = bexp_ref[b]

    def fetch(e):
        return (pltpu.make_async_copy(wgu_hbm.at[e], wgu_f32, wsem.at[0]),
                pltpu.make_async_copy(wd_hbm.at[e], wd_f32, wsem.at[1]))

    @pl.when(((b == 0) | (expert != bexp_ref[jnp.maximum(b - 1, 0)])) & (b < nused_ref[0]))
    def _():
        @pl.when(b == 0)
        def _():
            for cp in fetch(expert):
                cp.start()

        for cp in fetch(expert):
            cp.wait()
        wgu_bf[...] = wgu_f32[...].astype(BF16)
        wd_bf[...] = wd_f32[...].astype(BF16)

        @pl.when(bnext_ref[b] >= 0)
        def _():
            for cp in fetch(bnext_ref[b]):
                cp.start()

    @pl.when(b < nused_ref[0])
    def _():
        acc[...] = jnp.zeros_like(acc)
        rw_acc[...] = jnp.zeros_like(rw_acc)
        row = lax.broadcasted_iota(jnp.int32, (EXPERT_BLOCK, GATHER_SUB), 0)
        s_lo, s_hi = slo_ref[b], shi_ref[b]
        last_sub = h2_ref.shape[0] // GATHER_SUB - 1

        def select(s, live):
            t0 = pl.multiple_of(s * GATHER_SUB, GATHER_SUB)
            ids = ti_ref[0:TOP_K, pl.ds(t0, GATHER_SUB)]
            local = ti_ref[TOP_K:2 * TOP_K, pl.ds(t0, GATHER_SUB)] - rlo_ref[b]
            if live is not True:
                local = jnp.where(live, local, -1)
            inside = (ids == expert) & (local >= 0) & (local < EXPERT_BLOCK)
            hit_row = jnp.sum(jnp.where(inside, local + 1, 0), axis=0, keepdims=True) - 1
            hit_w = jnp.sum(jnp.where(inside, wt_ref[0:TOP_K, pl.ds(t0, GATHER_SUB)], 0.0), axis=0, keepdims=True)
            hit = row == hit_row
            return (jnp.where(hit, 1.0, 0.0).astype(BF16), jnp.where(hit, hit_w, 0.0),
                    h2_ref[pl.ds(t0, GATHER_SUB), :])

        def gather(i, carry):
            s0 = s_lo + GATHER_GROUP * i
            parts = [select(s0, True)] + [select(jnp.minimum(s0 + q, last_sub), s0 + q <= s_hi)
                                          for q in range(1, GATHER_GROUP)]
            acc[...] += jnp.dot(jnp.concatenate([p for p, _, _ in parts], axis=1),
                                jnp.concatenate([x for _, _, x in parts], axis=0), preferred_element_type=F32)
            rw_acc[...] += jnp.sum(functools.reduce(lambda a, c: a + c, [w for _, w, _ in parts]), axis=1,
                                   keepdims=True)
            return carry

        lax.fori_loop(0, (s_hi - s_lo + GATHER_GROUP) // GATHER_GROUP, gather, 0)
        gu = jnp.dot(acc[...].astype(BF16), wgu_bf[...], preferred_element_type=F32) + bgu_ref[0]
        dff = gu.shape[1] // 2
        gl = jnp.minimum(gu[:, :dff], SWIGLU_LIMIT)
        lin = jnp.clip(gu[:, dff:], -SWIGLU_LIMIT, SWIGLU_LIMIT)
        act = (gl * jax.nn.sigmoid(SWIGLU_ALPHA * gl) * (lin + 1.0)).astype(BF16)
        y = jnp.dot(act, wd_bf[...], preferred_element_type=F32) + bd_ref[0]
        o_ref[...] = (y * rw_acc[...]).astype(o_ref.dtype)

    @pl.when(b >= nused_ref[0])
    def _():
        o_ref[...] = jnp.zeros_like(o_ref)


def expert_blocks(h2, ti_t, wt_t, block_expert, next_expert, rank_lo, sub_lo, sub_hi, n_used, w_gate_up, b_gate_up,
                  w_down, b_down):
    n, d = h2.shape
    e, _, f2 = w_gate_up.shape
    n_blocks = block_expert.shape[0]
    whole = lambda b, be, bn, rl, lo, hi, nu: (0, 0)
    by_expert = lambda b, be, bn, rl, lo, hi, nu: (be[b], 0, 0)
    once = pl.Buffered(1)
    grid_spec = pltpu.PrefetchScalarGridSpec(
        num_scalar_prefetch=6,
        grid=(n_blocks,),
        in_specs=[pl.BlockSpec((n, d), whole, pipeline_mode=once),
                  pl.BlockSpec((SUBLANES, n), whole, pipeline_mode=once),
                  pl.BlockSpec((SUBLANES, n), whole, pipeline_mode=once),
                  pl.BlockSpec(memory_space=pl.ANY),
                  pl.BlockSpec((1, 1, f2), by_expert),
                  pl.BlockSpec(memory_space=pl.ANY),
                  pl.BlockSpec((1, 1, d), by_expert)],
        out_specs=pl.BlockSpec((EXPERT_BLOCK, d), lambda b, be, bn, rl, lo, hi, nu: (b, 0)),
        scratch_shapes=[pltpu.VMEM((EXPERT_BLOCK, d), F32), pltpu.VMEM((EXPERT_BLOCK, 1), F32),
                        pltpu.VMEM((d, f2), BF16), pltpu.VMEM((f2 // 2, d), BF16),
                        pltpu.VMEM((d, f2), F32), pltpu.VMEM((f2 // 2, d), F32),
                        pltpu.SemaphoreType.DMA((2,))],
    )
    return pl.pallas_call(
        _expert_kernel,
        grid_spec=grid_spec,
        out_shape=jax.ShapeDtypeStruct((n_blocks * EXPERT_BLOCK, d), BF16),
        compiler_params=_params("arbitrary"),
        name="expert_blocks",
    )(block_expert, next_expert, rank_lo, sub_lo, sub_hi, n_used, h2, ti_t, wt_t, w_gate_up, b_gate_up.reshape(e, 1, f2), w_down,
      b_down.reshape(e, 1, d))


def _combine_kernel(coff_ref, cgran_ref, cexp_ref, cbase_ref, yb_hbm, ti_ref, x2_ref, gf_ref, g_ref, o_ref,
                    slab, sem, acc):
    c = pl.program_id(0)
    start = coff_ref[c]
    count = coff_ref[c + 1] - start
    n_groups = (count + COMBINE_SLOTS - 1) // COMBINE_SLOTS

    def item(g, s, first=start, n=count):
        return first + jnp.minimum(g * COMBINE_SLOTS + s, n - 1)

    def copies(g, buf, first=start, n=count):
        rows = [pl.multiple_of(cgran_ref[item(g, s, first, n)] * COMBINE_GRANULE, COMBINE_GRANULE)
                for s in range(COMBINE_SLOTS)]
        return [pltpu.make_async_copy(yb_hbm.at[pl.ds(r, COMBINE_GRANULE), :],
                                      slab.at[buf, pl.ds(s * COMBINE_GRANULE, COMBINE_GRANULE), :], sem.at[buf])
                for s, r in enumerate(rows)]

    @pl.when(c == 0)
    def _():
        for cp in copies(0, 0):
            cp.start()

    acc[...] = jnp.zeros_like(acc)
    chunk = x2_ref.shape[0]
    row = lax.broadcasted_iota(jnp.int32, (COMBINE_GRANULE, chunk), 0)
    ids = ti_ref[0:TOP_K, :]
    ranks = ti_ref[TOP_K:2 * TOP_K, :]

    def group(g, carry):
        buf = g % 2

        @pl.when(g + 1 < n_groups)
        def _():
            for cp in copies(g + 1, 1 - buf):
                cp.start()

        for cp in copies(g, buf):
            cp.wait()
        pieces = []
        for s in range(COMBINE_SLOTS):
            live = g * COMBINE_SLOTS + s < count
            local = ranks - cbase_ref[item(g, s)]
            inside = (ids == jnp.where(live, cexp_ref[item(g, s)], -1)) & (local >= 0) & (local < COMBINE_GRANULE)
            hit_row = jnp.sum(jnp.where(inside, local + 1, 0), axis=0, keepdims=True) - 1
            pieces.append(jnp.where(row == hit_row, 1.0, 0.0).astype(BF16))
        onehot_t = jnp.concatenate(pieces, axis=0)
        acc[...] += lax.dot_general(onehot_t, slab[buf], (((0,), (0,)), ((), ())),
                                    preferred_element_type=F32)
        return carry

    lax.fori_loop(0, n_groups, group, 0)

    @pl.when(c + 1 < pl.num_programs(0))
    def _():
        nxt = coff_ref[c + 1]
        for cp in copies(0, 0, nxt, coff_ref[c + 2] - nxt):
            cp.start()

    y = acc[...]
    yn = y * lax.rsqrt(jnp.mean(y * y, axis=-1, keepdims=True) + EPS) * g_ref[...]
    o_ref[...] = x2_ref[...] + gf_ref[0] * yn


def combine_residual(yb, ti_t, x2, chunk_off, chunk_items, mod, mod_row_of_chunk, g_post_f):
    n, d = x2.shape
    chunk_row = lambda c, off, gran, exp, base: (c, 0)
    grid_spec = pltpu.PrefetchScalarGridSpec(
        num_scalar_prefetch=4,
        grid=(n // MOE_CHUNK,),
        in_specs=[pl.BlockSpec(memory_space=pl.ANY),
                  pl.BlockSpec((SUBLANES, MOE_CHUNK), lambda c, off, gran, exp, base: (0, c)),
                  pl.BlockSpec((MOE_CHUNK, d), chunk_row),
                  pl.BlockSpec((1, 1, d), lambda c, off, gran, exp, base: (mod_row_of_chunk(c), 0, 5)),
                  pl.BlockSpec((1, d), lambda c, off, gran, exp, base: (0, 0))],
        out_specs=pl.BlockSpec((MOE_CHUNK, d), chunk_row),
        scratch_shapes=[pltpu.VMEM((2, COMBINE_SLOTS * COMBINE_GRANULE, d), BF16),
                        pltpu.SemaphoreType.DMA((2,)),
                        pltpu.VMEM((MOE_CHUNK, d), F32)],
    )
    return pl.pallas_call(
        _combine_kernel,
        grid_spec=grid_spec,
        out_shape=jax.ShapeDtypeStruct((n, d), F32),
        compiler_params=_params("arbitrary"),
        name="combine_residual",
    )(chunk_off, *chunk_items, yb, ti_t, x2, mod, g_post_f.reshape(1, d))


def _grid_pos_embedding(rows, d):
    quarter = d // 4
    omega = POS_BASE ** (-jnp.arange(quarter, dtype=F32) / quarter)

    def emb(p):
        ang = p[:, None] * omega[None, :]
        return jnp.concatenate([jnp.sin(ang), jnp.cos(ang)], axis=-1)

    row_emb = jnp.repeat(emb(jnp.arange(rows, dtype=F32)), GRID_W, axis=0)
    col_emb = jnp.tile(emb(jnp.arange(GRID_W, dtype=F32)), (rows, 1))
    return jnp.concatenate([row_emb, col_emb], axis=-1)


def _count_le(sorted_vals, queries):
    return jnp.sum(sorted_vals <= queries[:, None], axis=1).astype(jnp.int32)


def _moe_plan(cnt_after, n_tok, tm):
    assert tm == GATHER_SUB
    sub_after = cnt_after[:, 0, :N_EXPERTS].astype(jnp.int32)
    counts = sub_after[-1]
    padded = (counts + EXPERT_BLOCK - 1) // EXPERT_BLOCK * EXPERT_BLOCK
    padded_end = jnp.cumsum(padded)
    padded_start = padded_end - padded
    n_blocks = n_tok * TOP_K // EXPERT_BLOCK + N_EXPERTS
    blocks = jnp.arange(n_blocks, dtype=jnp.int32)
    block_expert = jnp.minimum(_count_le(padded_end[None, :], blocks * EXPERT_BLOCK), N_EXPERTS - 1)
    n_used = (padded_end[-1:] // EXPERT_BLOCK).astype(jnp.int32)
    of_block = (block_expert[:, None] == jnp.arange(N_EXPERTS, dtype=jnp.int32)[None, :]).astype(jnp.int32)
    r_lo = (blocks - jnp.sum(of_block * padded_start[None, :], axis=1) // EXPERT_BLOCK) * EXPERT_BLOCK
    r_hi = jnp.minimum(jnp.sum(of_block * counts[None, :], axis=1), r_lo + EXPERT_BLOCK) - 1
    through = jnp.sum(of_block[:, :, None] * sub_after.T[None, :, :], axis=1)
    after = jnp.sum(of_block * padded_end[None, :], axis=1) // EXPERT_BLOCK
    next_expert = jnp.sum((after[:, None] == blocks[None, :]).astype(jnp.int32) * block_expert[None, :], axis=1)
    next_expert = jnp.where(after < n_used[0], next_expert, -1).astype(jnp.int32)
    n_sub = sub_after.shape[0]
    sub_lo = jnp.minimum(_count_le(through, r_lo), n_sub - 1)
    sub_hi = jnp.minimum(_count_le(through, r_hi), n_sub - 1)
    nch = n_tok // MOE_CHUNK
    per_chunk = MOE_CHUNK // tm
    cb_after = sub_after[per_chunk - 1::per_chunk]
    cb_before = jnp.concatenate([jnp.zeros((1, N_EXPERTS), jnp.int32), cb_after[:-1]], axis=0)
    b_lo = ((padded_start[None, :] + cb_before) // COMBINE_GRANULE).reshape(-1)
    b_hi = ((padded_start[None, :] + cb_after - 1) // COMBINE_GRANULE).reshape(-1)
    n_it = jnp.where((cb_after > cb_before).reshape(-1), b_hi - b_lo + 1, 0)
    off_end = jnp.cumsum(n_it)
    off = off_end - n_it
    n_items = n_blocks * (EXPERT_BLOCK // COMBINE_GRANULE) + N_EXPERTS * (nch - 1)
    w = jnp.minimum(jnp.arange(n_items, dtype=jnp.int32), off_end[-1] - 1)
    chunk_end = off_end[N_EXPERTS - 1::N_EXPERTS]
    chunk_off = jnp.concatenate([jnp.zeros((1,), jnp.int32), chunk_end]).astype(jnp.int32)
    chunk_of = jnp.minimum(_count_le(chunk_end[None, :], w), nch - 1)
    chunk_expert = jnp.minimum(_count_le(off_end.reshape(nch, N_EXPERTS)[chunk_of], w), N_EXPERTS - 1)
    cell = chunk_of * N_EXPERTS + chunk_expert
    chunk_granules = (b_lo[cell] + w - off[cell]).astype(jnp.int32)
    chunk_base = chunk_granules * COMBINE_GRANULE - padded_start[chunk_expert]
    return block_expert, next_expert, r_lo, sub_lo, sub_hi, n_used, chunk_off, (chunk_granules, chunk_expert, chunk_base)


def kernel(x, c, ctx, c_ctx, w_mod, b_mod, g_pre_mix, g_post_mix, g_pre_ffn, g_post_ffn, w_in, conv_kv,
           conv_q, a_log, dt_bias, gdn_norm_g, w_proj_a, conf_dw, conf_dw_b, conf_ln_g, conf_ln_b,
           w_proj_b, w_out, w_router, b_router, w_gate_up, b_gate_up, w_down, b_down):
    batch, seq, d = x.shape
    ctx_len = ctx.shape[1]
    n_tok = batch * seq
    gw = HEADS * DV
    beta_off = HEADS * DK + gw
    state_cols = beta_off + 4 * HEADS
    q_off = state_cols

    mod_rows = 2 * SUBLANES
    cond = jnp.zeros((mod_rows, d), F32).at[:batch].set(c).at[batch].set(c_ctx)
    mod = modulation(cond, w_mod[0], b_mod[0]).reshape(mod_rows, 1, 6 * d)

    w_t = jnp.swapaxes(w_in[0], 0, 1)
    w_main = projection_weights(w_t, keep=beta_off, skip=state_cols - beta_off)
    w_ba = jnp.zeros((LANES, d), F32).at[:4 * HEADS].set(w_t[beta_off:state_cols]).astype(BF16)

    pos = _grid_pos_embedding(seq // GRID_W, d)
    x_flat = x.reshape(n_tok, d)
    tm = 1024
    tiles_per_seq = seq // tm
    proj, ba = input_projection(x_flat, pos, mod, lambda i: i // tiles_per_seq, g_pre_mix[0],
                                w_main, w_ba, tm=tm, tn=2048)
    proj_ctx, ba_ctx = input_projection(ctx.reshape(batch * ctx_len, d), None, mod, lambda i: batch,
                                        g_pre_mix[0], w_main[:beta_off], w_ba, tm=ctx_len, tn=1024)

    og = gated_deltanet(proj, ba, proj_ctx, ba_ctx, conv_kv[0], conv_q[0], a_log[0], dt_bias[0],
                        gdn_norm_g[0], batch=batch, seq=seq, ctx=ctx_len)
    uc = conformer_conv(proj, conf_dw[0], conf_dw_b[0], batch=batch, seq=seq)

    tm2 = 512
    w_r = jnp.zeros((d, LANES), F32).at[:, :N_EXPERTS].set(w_router[0]).astype(BF16)
    b_r = jnp.zeros((1, LANES), F32).at[0, :N_EXPERTS].set(b_router[0])
    x2, h2, top_i, top_w, cnt_after = mixer_output(
        og, uc, proj, x_flat, pos, mod, lambda i: i // (seq // tm2), g_post_mix[0], g_pre_ffn[0],
        conf_ln_g[0], conf_ln_b[0], w_proj_a[0].astype(BF16), w_proj_b[0].astype(BF16),
        w_out[0].astype(BF16), w_r, b_r, tm=tm2, sub=GATHER_SUB)

    block_expert, next_expert, rank_lo, sub_lo, sub_hi, n_used, chunk_off, chunk_items = _moe_plan(
        cnt_after, n_tok, GATHER_SUB)
    yb = expert_blocks(h2, top_i, top_w, block_expert, next_expert, rank_lo, sub_lo, sub_hi, n_used, w_gate_up[0],
                       b_gate_up[0], w_down[0], b_down[0])
    out = combine_residual(yb, top_i, x2, chunk_off, chunk_items, mod, lambda c: c // (seq // MOE_CHUNK),
                           g_post_ffn[0])
    return out.reshape(batch, seq, d)
```

```python
import functools

import jax
import jax.numpy as jnp
from jax import lax
from jax.experimental import pallas as pl
from jax.experimental.pallas import tpu as pltpu

F32 = jnp.float32
BF16 = jnp.bfloat16

D_MODEL = 1024
GRID_W = 64
HEADS = 8
DK = 128
DV = 128
SHORT_CONV = 5
CHUNK = 64
CONF_KERNEL = 31
N_EXPERTS = 32
TOP_K = 4
SWIGLU_LIMIT = 7.0
SWIGLU_ALPHA = 1.702
MOE_BLOCK = 128
EPS = 1e-6
POS_BASE = 10000.0

LANES = 128
SUBLANES = 8
VMEM_LIMIT = 60 * 1024 * 1024

COL_K, COL_V, COL_Q, COL_Z, COL_GLU_A, COL_GLU_G, COL_GATE_A, COL_GATE_B = range(8)


def _params(*sem):
    return pltpu.CompilerParams(dimension_semantics=sem, vmem_limit_bytes=VMEM_LIMIT)


def _mod_kernel(c_ref, w_ref, b_ref, o_ref):
    c = c_ref[...]
    s = c * jax.nn.sigmoid(c)
    o_ref[...] = jnp.dot(s, w_ref[...], preferred_element_type=F32,
                         precision=lax.Precision.HIGHEST) + b_ref[...]


def modulation(cond, w_mod, b_mod):
    r, d = cond.shape
    n = w_mod.shape[1]
    tn = 1024
    return pl.pallas_call(
        _mod_kernel,
        grid=(n // tn,),
        in_specs=[pl.BlockSpec((r, d), lambda j: (0, 0)),
                  pl.BlockSpec((d, tn), lambda j: (0, j)),
                  pl.BlockSpec((1, tn), lambda j: (0, j))],
        out_specs=pl.BlockSpec((r, tn), lambda j: (0, j)),
        out_shape=jax.ShapeDtypeStruct((r, n), F32),
        compiler_params=_params("arbitrary"),
        name="modulation",
    )(cond, w_mod, b_mod.reshape(1, n))


def _proj_weights_kernel(a_ref, b_ref, o_ref, *, first_shifted, shift):
    @pl.when(pl.program_id(0) < first_shifted)
    def _():
        o_ref[...] = a_ref[...].astype(o_ref.dtype)

    @pl.when(pl.program_id(0) >= first_shifted)
    def _():
        o_ref[...] = jnp.concatenate([a_ref[shift:, :], b_ref[...]], axis=0).astype(o_ref.dtype)


def projection_weights(w_t, *, keep, skip, tr=1024):
    total, d = w_t.shape
    n_out = total - skip
    assert keep % tr == 0 and n_out % tr == 0 and tr % skip == 0 and skip % SUBLANES == 0
    return pl.pallas_call(
        functools.partial(_proj_weights_kernel, first_shifted=keep // tr, shift=skip),
        grid=(n_out // tr,),
        in_specs=[pl.BlockSpec((tr, d), lambda j: (j, 0)),
                  pl.BlockSpec((skip, d), lambda j: ((j + 1) * (tr // skip), 0))],
        out_specs=pl.BlockSpec((tr, d), lambda j: (j, 0)),
        out_shape=jax.ShapeDtypeStruct((n_out, d), BF16),
        compiler_params=_params("arbitrary"),
        name="projection_weights",
    )(w_t, w_t)


_NT = (((1,), (1,)), ((), ()))


def _inproj_kernel(*refs, has_pos):
    if has_pos:
        x_ref, pos_ref, sh_ref, sc_ref, g_ref, w_ref, wba_ref, o_ref, ba_ref, h_scr = refs
    else:
        x_ref, sh_ref, sc_ref, g_ref, w_ref, wba_ref, o_ref, ba_ref, h_scr = refs

    @pl.when(pl.program_id(1) == 0)
    def _():
        x = x_ref[...]
        if has_pos:
            x = x + pos_ref[...]
        y = x * lax.rsqrt(jnp.mean(x * x, axis=-1, keepdims=True) + EPS) * g_ref[...]
        h = (y * (1.0 + sc_ref[0]) + sh_ref[0]).astype(BF16)
        h_scr[...] = h
        ba_ref[...] = lax.dot_general(h, wba_ref[...], _NT, preferred_element_type=F32)

    o_ref[...] = lax.dot_general(h_scr[...], w_ref[...], _NT, preferred_element_type=F32).astype(o_ref.dtype)


def input_projection(x, pos, mod, mod_row_of_tile, g_pre, w_main, w_ba, *, tm, tn):
    n, d = x.shape
    w = w_main.shape[0]
    has_pos = pos is not None
    in_specs = [pl.BlockSpec((tm, d), lambda i, j: (i, 0))]
    args = [x]
    if has_pos:
        pos_tiles = pos.shape[0] // tm
        in_specs.append(pl.BlockSpec((tm, d), lambda i, j: (i % pos_tiles, 0)))
        args.append(pos)
    in_specs += [
        pl.BlockSpec((1, 1, d), lambda i, j: (mod_row_of_tile(i), 0, 0)),
        pl.BlockSpec((1, 1, d), lambda i, j: (mod_row_of_tile(i), 0, 1)),
        pl.BlockSpec((1, d), lambda i, j: (0, 0)),
        pl.BlockSpec((tn, d), lambda i, j: (j, 0)),
        pl.BlockSpec((LANES, d), lambda i, j: (0, 0)),
    ]
    args += [mod, mod, g_pre.reshape(1, d), w_main, w_ba]
    return pl.pallas_call(
        functools.partial(_inproj_kernel, has_pos=has_pos),
        grid=(n // tm, w // tn),
        in_specs=in_specs,
        out_specs=[pl.BlockSpec((tm, tn), lambda i, j: (i, j)),
                   pl.BlockSpec((tm, LANES), lambda i, j: (i, 0))],
        out_shape=[jax.ShapeDtypeStruct((n, w), BF16), jax.ShapeDtypeStruct((n, LANES), F32)],
        scratch_shapes=[pltpu.VMEM((tm, d), BF16)],
        compiler_params=_params("arbitrary", "arbitrary"),
        name="input_projection_pos" if has_pos else "input_projection_ctx",
    )(*args)


def _silu(x):
    return x * jax.nn.sigmoid(x)


def _softplus(x):
    return jnp.maximum(x, 0.0) + jnp.log(1.0 + jnp.exp(-jnp.abs(x)))


def _short_conv(src, taps, pad_scr, rows):
    half = SHORT_CONV // 2
    pad_scr[0:SUBLANES, :] = jnp.zeros((SUBLANES, LANES), F32)
    pad_scr[SUBLANES:SUBLANES + rows, :] = src.astype(F32)
    pad_scr[SUBLANES + rows:2 * SUBLANES + rows, :] = jnp.zeros((SUBLANES, LANES), F32)
    acc = None
    for j in range(SHORT_CONV):
        off = SUBLANES + j - half
        term = pad_scr[off:off + rows, :] * taps[j:j + 1, :]
        acc = term if acc is None else acc + term
    return _silu(acc)


def _l2n(x):
    return x * lax.rsqrt(jnp.sum(x * x, axis=-1, keepdims=True) + EPS)


def _gdn_kernel(proj_hbm, ba_hbm, kc_ref, vc_ref, bac_ref,
                wk_ref, wv_ref, wq_ref, ab_ref, gn_ref, o_ref,
                k_ref, v_ref, q_ref, z_ref, ba_ref, in_sem,
                kf, vf, qf, gates, pad_scr, lhs, val, att, ktt, dec, osc, *, seq, ctx, unroll, group):
    total = ctx + seq
    n_batch, n_groups = pl.num_programs(0), pl.num_programs(1)
    gl = group * LANES

    def block(bb, hg, base):
        return proj_hbm.at[pl.ds(pl.multiple_of(bb * seq, seq), seq),
                           pl.ds(pl.multiple_of((base * n_groups + hg) * gl, gl), gl)]

    def input_copies(bb, hg):
        return [pltpu.make_async_copy(block(bb, hg, COL_K), k_ref, in_sem.at[0]),
                pltpu.make_async_copy(block(bb, hg, COL_V), v_ref, in_sem.at[1]),
                pltpu.make_async_copy(block(bb, hg, COL_Q), q_ref, in_sem.at[2]),
                pltpu.make_async_copy(ba_hbm.at[pl.ds(pl.multiple_of(bb * seq, seq), seq), :], ba_ref, in_sem.at[3])]

    b_now, g_now = pl.program_id(0), pl.program_id(1)
    z_copy = pltpu.make_async_copy(block(b_now, g_now, COL_Z), z_ref, in_sem.at[4])

    @pl.when((b_now == 0) & (g_now == 0))
    def _():
        for cp in input_copies(b_now, g_now):
            cp.start()

    for cp in input_copies(b_now, g_now):
        cp.wait()
    z_copy.start()
    nc = total // CHUNK
    ncc = ctx // CHUNK
    two = 2 * CHUNK
    lg_chunk = CHUNK.bit_length() - 1

    ri = lax.broadcasted_iota(jnp.int32, (two, two), 0)
    ci = lax.broadcasted_iota(jnp.int32, (two, two), 1)
    same_dir = (ri >> lg_chunk) == (ci >> lg_chunk)
    incl = same_dir & (((ri < CHUNK) & (ri >= ci)) | ((ri >= CHUNK) & (ri <= ci)))
    strict = incl & (ri != ci)
    eye = jnp.where(ri == ci, 1.0, 0.0)
    tri = jnp.where(incl, 1.0, 0.0).astype(BF16)
    level_masks = []
    for lg in range(lg_chunk):
        same_parent = (ri >> (lg + 1)) == (ci >> (lg + 1))
        level_masks.append(jnp.where(same_parent & ((ri >> lg) != (ci >> lg)), 1.0, 0.0))
    top_rows = lax.broadcasted_iota(jnp.int32, (two, LANES), 0) < CHUNK

    def bwd_chunk(t):
        return jnp.where(t < ncc, ncc - 1 - t, nc + ncc - 1 - t)

    def stacked(ref, rf, rb):
        return jnp.concatenate([ref[pl.ds(rf, CHUNK), :], ref[pl.ds(rb, CHUNK), :]], axis=0)

    def chunk_load(t):
        rf = pl.multiple_of(t * CHUNK, CHUNK)
        rb = pl.multiple_of(bwd_chunk(t) * CHUNK, CHUNK)
        beta = jnp.concatenate([gates[0, pl.ds(rf, CHUNK), :], gates[1, pl.ds(rb, CHUNK), :]], axis=0)
        g = jnp.concatenate([gates[2, pl.ds(rf, CHUNK), :], gates[3, pl.ds(rb, CHUNK), :]], axis=0)
        return stacked(kf, rf, rb), stacked(vf, rf, rb), stacked(qf, rf, rb), beta, g

    def chunk_triangle(k, q, beta, g):
        kb = k.astype(BF16)
        kq = jnp.concatenate([kb, q.astype(BF16)], axis=0)
        kk_qk = lax.dot_general(kq, kb, (((1,), (1,)), ((), ())), preferred_element_type=F32)
        g_hi = g.astype(BF16)
        r1 = g - g_hi.astype(F32)
        g_mid = r1.astype(BF16)
        g_lo = (r1 - g_mid.astype(F32)).astype(BF16)
        gc3 = jnp.dot(tri, jnp.concatenate([g_hi, g_mid, g_lo], axis=1), preferred_element_type=F32)
        gc = gc3[:, :LANES] + gc3[:, LANES:2 * LANES] + gc3[:, 2 * LANES:]
        gc_row = jnp.transpose(gc)
        decay = jnp.where(incl, jnp.exp(gc - gc_row), 0.0)
        a = jnp.where(strict, beta * kk_qk[:two] * decay, 0.0)
        return a, (kk_qk[two:] * decay).astype(BF16), gc

    def chunk_finish(hh, t, k, v, q, beta, gc, attn, tmat):
        eg = jnp.exp(gc)
        rhs = jnp.concatenate([v * beta, k * beta * eg], axis=1).astype(BF16)
        sol = jnp.dot(tmat.astype(BF16), rhs, preferred_element_type=F32)
        g_end = jnp.where(top_rows, jnp.broadcast_to(gc[CHUNK - 1:CHUNK, :], (two, LANES)),
                          jnp.broadcast_to(gc[CHUNK:CHUNK + 1, :], (two, LANES)))
        k_tail = k * jnp.exp(g_end - gc)
        qg = (q * eg).astype(BF16)
        kcum = sol[:, DV:].astype(BF16)
        r0 = pl.multiple_of(t * two, two)
        lhs[hh, 0, pl.ds(r0, two), :] = jnp.concatenate([kcum[:CHUNK], qg[:CHUNK]], axis=0)
        lhs[hh, 1, pl.ds(r0, two), :] = jnp.concatenate([kcum[CHUNK:], qg[CHUNK:]], axis=0)
        val[hh, pl.ds(r0, two), :] = sol[:, :DV].astype(BF16)
        att[hh, pl.ds(r0, two), :] = attn
        ktt[hh, pl.ds(r0, two), :] = jnp.transpose(k_tail).astype(BF16)
        d0 = pl.multiple_of(t * SUBLANES, SUBLANES)
        e_end = jnp.exp(g_end)
        dec[hh, 0, pl.ds(d0, SUBLANES), :] = e_end[:SUBLANES]
        dec[hh, 1, pl.ds(d0, SUBLANES), :] = e_end[CHUNK:CHUNK + SUBLANES]

    for hh in range(group):
        head = pl.program_id(1) * group + hh
        cols = slice(hh * LANES, (hh + 1) * LANES)

        kf[0:ctx, :] = _l2n(_short_conv(kc_ref[:, cols], wk_ref[:, cols], pad_scr, ctx))
        vf[0:ctx, :] = _short_conv(vc_ref[:, cols], wv_ref[:, cols], pad_scr, ctx)
        qf[0:ctx, :] = jnp.zeros((ctx, LANES), F32)
        kf[ctx:total, :] = _l2n(_short_conv(k_ref[:, cols], wk_ref[:, cols], pad_scr, seq))
        vf[ctx:total, :] = _short_conv(v_ref[:, cols], wv_ref[:, cols], pad_scr, seq)
        qf[ctx:total, :] = _l2n(_short_conv(q_ref[:, cols], wq_ref[:, cols], pad_scr, seq)) * (DK ** -0.5)

        def gate_cols(src_ref, lo, rows):
            x = src_ref[...]
            lane = lax.broadcasted_iota(jnp.int32, (rows, LANES), 1)
            beta = jax.nn.sigmoid(x)
            g = -jnp.exp(ab_ref[0:1, :]) * _softplus(x + ab_ref[1:2, :])
            for slot, (arr, base) in enumerate(((beta, 0), (beta, HEADS), (g, 2 * HEADS), (g, 3 * HEADS))):
                col = jnp.sum(jnp.where(lane == base + head, arr, 0.0), axis=-1, keepdims=True)
                gates[slot, lo:lo + rows, :] = jnp.broadcast_to(col, (rows, LANES))

        gate_cols(bac_ref, 0, ctx)
        gate_cols(ba_ref, ctx, seq)

        def chunk_body(i, carry, hh=hh):
            steps = [i * unroll + u for u in range(unroll)]
            loaded = [chunk_load(t) for t in steps]
            tris = [chunk_triangle(k, q, beta, g) for (k, v, q, beta, g) in loaded]
            tmats = [eye - a * level_masks[0] for (a, _, _) in tris]
            for m in level_masks[1:]:
                nxt = []
                for (a, _, _), tmat in zip(tris, tmats):
                    cs = (a * m).astype(BF16)
                    tb = tmat.astype(BF16)
                    tc = jnp.dot(tb, cs, preferred_element_type=F32)
                    nxt.append(tmat - jnp.dot(tc.astype(BF16), tb, preferred_element_type=F32))
                tmats = nxt
            for t, (k, v, q, beta, g), (a, attn, gc), tmat in zip(steps, loaded, tris, tmats):
                chunk_finish(hh, t, k, v, q, beta, gc, attn, tmat)
            return carry

        lax.fori_loop(0, nc // unroll, chunk_body, 0)

    wrap = g_now + 1 == n_groups

    @pl.when(jnp.logical_not(wrap & (b_now + 1 == n_batch)))
    def _():
        for cp in input_copies(jnp.where(wrap, b_now + 1, b_now), jnp.where(wrap, 0, g_now + 1)):
            cp.start()

    def scan_body(t, carry):
        rf = pl.multiple_of(t * CHUNK, CHUNK)
        rb = pl.multiple_of(bwd_chunk(t) * CHUNK, CHUNK)
        r0 = pl.multiple_of(t * two, two)
        d0 = pl.multiple_of(t * SUBLANES, SUBLANES)
        rs = [(jnp.dot(lhs[hh, 0, pl.ds(r0, two), :], carry[2 * hh].astype(BF16), preferred_element_type=F32),
               jnp.dot(lhs[hh, 1, pl.ds(r0, two), :], carry[2 * hh + 1].astype(BF16), preferred_element_type=F32))
              for hh in range(group)]
        v_news = [val[hh, pl.ds(r0, two), :].astype(F32) - jnp.concatenate([r_f[:CHUNK], r_b[:CHUNK]], axis=0)
                  for hh, (r_f, r_b) in enumerate(rs)]
        outs, new_states = [], []
        for hh, ((r_f, r_b), v_new) in enumerate(zip(rs, v_news)):
            zeros = jnp.zeros_like(v_new)
            v_bd = jnp.concatenate([jnp.where(top_rows, v_new, zeros), jnp.where(top_rows, zeros, v_new)],
                                   axis=1).astype(BF16)
            upd = jnp.dot(ktt[hh, pl.ds(r0, two), :], v_bd, preferred_element_type=F32)
            new_states.append(carry[2 * hh] * dec[hh, 0, pl.ds(d0, SUBLANES), :][0:1, :] + upd[:, :DV])
            new_states.append(carry[2 * hh + 1] * dec[hh, 1, pl.ds(d0, SUBLANES), :][0:1, :] + upd[:, DV:])
            outs.append(jnp.concatenate([r_f[CHUNK:], r_b[CHUNK:]], axis=0)
                        + jnp.dot(att[hh, pl.ds(r0, two), :], v_new.astype(BF16), preferred_element_type=F32))
        for hh, o in enumerate(outs):
            osc[hh, pl.ds(rf, CHUNK), :] += o[:CHUNK]
            osc[hh, pl.ds(rb, CHUNK), :] += o[CHUNK:]
        return tuple(new_states)

    osc[...] = jnp.zeros_like(osc)
    zero = jnp.zeros((DK, DV), F32)
    lax.fori_loop(0, nc, scan_body, (zero,) * (2 * group))

    z_copy.wait()
    for hh in range(group):
        cols = slice(hh * LANES, (hh + 1) * LANES)
        o = osc[hh, ctx:total, :]
        o = o * lax.rsqrt(jnp.mean(o * o, axis=-1, keepdims=True) + EPS) * gn_ref[...]
        o_ref[:, cols] = (o * _silu(z_ref[:, cols].astype(F32))).astype(o_ref.dtype)


def gated_deltanet(proj, ba, proj_ctx, ba_ctx, conv_kv, conv_q, a_log, dt_bias, gn, *, batch, seq, ctx,
                   unroll=18, group=4):
    total = seq + ctx
    nc = total // CHUNK
    gl = group * LANES
    ng = HEADS // group
    taps = jnp.zeros((SUBLANES, conv_kv.shape[1]), F32).at[:SHORT_CONV].set(conv_kv)
    taps_q = jnp.zeros((SUBLANES, conv_q.shape[1]), F32).at[:SHORT_CONV].set(conv_q)
    ab = jnp.zeros((SUBLANES, LANES), F32)
    ab = ab.at[0, 2 * HEADS:4 * HEADS].set(a_log.reshape(-1)).at[1, 2 * HEADS:4 * HEADS].set(dt_bias.reshape(-1))
    col = lambda base: (lambda b, h: (b, base * ng + h))
    return pl.pallas_call(
        functools.partial(_gdn_kernel, seq=seq, ctx=ctx, unroll=unroll, group=group),
        grid=(batch, ng),
        in_specs=[
            pl.BlockSpec(memory_space=pl.ANY),
            pl.BlockSpec(memory_space=pl.ANY),
            pl.BlockSpec((ctx, gl), col(COL_K)),
            pl.BlockSpec((ctx, gl), col(COL_V)),
            pl.BlockSpec((ctx, LANES), lambda b, h: (b, 0)),
            pl.BlockSpec((SUBLANES, gl), lambda b, h: (0, h)),
            pl.BlockSpec((SUBLANES, gl), lambda b, h: (0, ng + h)),
            pl.BlockSpec((SUBLANES, gl), lambda b, h: (0, h)),
            pl.BlockSpec((SUBLANES, LANES), lambda b, h: (0, 0)),
            pl.BlockSpec((1, LANES), lambda b, h: (0, 0)),
        ],
        out_specs=pl.BlockSpec((seq, gl), lambda b, h: (b, h)),
        out_shape=jax.ShapeDtypeStruct((batch * seq, HEADS * DV), BF16),
        scratch_shapes=[
            pltpu.VMEM((seq, gl), BF16),
            pltpu.VMEM((seq, gl), BF16),
            pltpu.VMEM((seq, gl), BF16),
            pltpu.VMEM((seq, gl), BF16),
            pltpu.VMEM((seq, LANES), F32),
            pltpu.SemaphoreType.DMA((5,)),
            pltpu.VMEM((total, LANES), F32),
            pltpu.VMEM((total, LANES), F32),
            pltpu.VMEM((total, LANES), F32),
            pltpu.VMEM((4, total, LANES), F32),
            pltpu.VMEM((seq + 2 * SUBLANES, LANES), F32),
            pltpu.VMEM((group, 2, nc * 2 * CHUNK, LANES), BF16),
            pltpu.VMEM((group, nc * 2 * CHUNK, LANES), BF16),
            pltpu.VMEM((group, nc * 2 * CHUNK, LANES), BF16),
            pltpu.VMEM((group, nc * 2 * CHUNK, LANES), BF16),
            pltpu.VMEM((group, 2, nc * SUBLANES, LANES), F32),
            pltpu.VMEM((group, total, LANES), F32),
        ],
        compiler_params=_params("arbitrary", "arbitrary"),
        name="gated_deltanet",
    )(proj, ba, proj_ctx, proj_ctx, ba_ctx, taps, taps, taps_q, ab, gn.reshape(1, DV))


CONF_ROWS = 512
CONF_HALO = 16


def _conf_kernel(a_ref, g_ref, w_ref, b_ref, o_ref, pad_scr, *, seq):
    cb = a_ref.shape[1]
    u = a_ref[...].astype(F32) * jax.nn.sigmoid(g_ref[...].astype(F32))
    pad_scr[0:CONF_HALO, :] = jnp.zeros((CONF_HALO, cb), F32)
    pad_scr[CONF_HALO:CONF_HALO + seq, :] = u
    pad_scr[CONF_HALO + seq:2 * CONF_HALO + seq, :] = jnp.zeros((CONF_HALO, cb), F32)
    win = CONF_ROWS + 2 * CONF_HALO
    first = CONF_HALO - CONF_KERNEL // 2

    def body(i, carry):
        r0 = pl.multiple_of(i * CONF_ROWS, CONF_ROWS)
        w = pad_scr[pl.ds(r0, win), :]
        acc = jnp.zeros((CONF_ROWS, cb), F32) + b_ref[...]
        for sub in range(SUBLANES):
            shifted = w if sub == 0 else pltpu.roll(w, win - sub, axis=0)
            for j in range(CONF_KERNEL):
                off = first + j
                if off % SUBLANES == sub:
                    base = off - sub
                    acc = acc + shifted[base:base + CONF_ROWS, :] * w_ref[j:j + 1, :]
        o_ref[pl.ds(r0, CONF_ROWS), :] = acc.astype(o_ref.dtype)
        return carry

    lax.fori_loop(0, seq // CONF_ROWS, body, 0)


def conformer_conv(proj, conf_dw, conf_dw_b, *, batch, seq, cb=128):
    c = conf_dw.shape[1]
    nb = c // cb
    taps = jnp.zeros((32, c), F32).at[:CONF_KERNEL].set(conf_dw)
    return pl.pallas_call(
        functools.partial(_conf_kernel, seq=seq),
        grid=(batch, nb),
        in_specs=[pl.BlockSpec((seq, cb), lambda b, j: (b, COL_GLU_A * nb + j)),
                  pl.BlockSpec((seq, cb), lambda b, j: (b, COL_GLU_G * nb + j)),
                  pl.BlockSpec((32, cb), lambda b, j: (0, j)),
                  pl.BlockSpec((1, cb), lambda b, j: (0, j))],
        out_specs=pl.BlockSpec((seq, cb), lambda b, j: (b, j)),
        out_shape=jax.ShapeDtypeStruct((batch * seq, c), BF16),
        scratch_shapes=[pltpu.VMEM((seq + 2 * CONF_HALO, cb), F32)],
        compiler_params=_params("arbitrary", "arbitrary"),
        name="conformer_conv",
    )(proj, proj, taps, conf_dw_b.reshape(1, c))


def _mixer_out_kernel(og_ref, uc_ref, ga_ref, gb_ref, x_ref, pos_ref, gm_ref, shf_ref, scf_ref,
                      gpost_ref, gpre_ref, lng_ref, lnb_ref, wa_ref, wb_ref, wo_ref, wr_ref, br_ref,
                      x2_ref, h2_ref, ti_ref, tw_ref, cnt_ref, cnt_scr, *, sub):
    parts = [slice(r, r + sub) for r in range(0, og_ref.shape[0], sub)]

    def rms(v):
        return v * lax.rsqrt(jnp.mean(v * v, axis=-1, keepdims=True) + EPS)

    def layer_norm_silu(uc):
        mu = jnp.mean(uc, axis=-1, keepdims=True)
        var = jnp.mean(jnp.square(uc - mu), axis=-1, keepdims=True)
        return _silu((uc - mu) * lax.rsqrt(var + EPS) * lng_ref[...] + lnb_ref[...]).astype(BF16)

    us = [layer_norm_silu(uc_ref[p, :].astype(F32)) for p in parts]
    yas = [jnp.dot(og_ref[p, :], wa_ref[...], preferred_element_type=F32) for p in parts]
    ybs = [jnp.dot(u, wb_ref[...], preferred_element_type=F32) for u in us]
    merged = [(jax.nn.sigmoid(ga_ref[p, :].astype(F32)) * ya
               + jax.nn.sigmoid(gb_ref[p, :].astype(F32)) * yb).astype(BF16) for p, ya, yb in zip(parts, yas, ybs)]
    ys = [jnp.dot(m, wo_ref[...], preferred_element_type=F32) for m in merged]
    x2s = [x_ref[p, :] + pos_ref[p, :] + gm_ref[0] * (rms(y) * gpost_ref[...]) for p, y in zip(parts, ys)]
    h2s = [(rms(x2) * gpre_ref[...] * (1.0 + scf_ref[0]) + shf_ref[0]).astype(BF16) for x2 in x2s]
    for p, x2, h2 in zip(parts, x2s, h2s):
        x2_ref[p, :] = x2
        h2_ref[p, :] = h2
    all_logits = [jnp.dot(h2, wr_ref[...], preferred_element_type=F32) + br_ref[...] for h2 in h2s]
    lane = lax.broadcasted_iota(jnp.int32, (sub, LANES), 1).astype(F32)
    ri = lax.broadcasted_iota(jnp.int32, (sub, sub), 0)
    ci = lax.broadcasted_iota(jnp.int32, (sub, sub), 1)
    before = jnp.where(ri > ci, 1.0, 0.0).astype(BF16)

    @pl.when(pl.program_id(0) == 0)
    def _():
        cnt_scr[...] = jnp.zeros_like(cnt_scr)

    routed = []
    for logits in all_logits:
        live = jnp.where(lane < N_EXPERTS, logits, -jnp.inf)
        top_v, top_i = [], []
        for _ in range(TOP_K):
            m = jnp.max(live, axis=-1, keepdims=True)
            idx = jnp.min(jnp.where(live == m, lane, float(LANES)), axis=-1, keepdims=True)
            top_v.append(m)
            top_i.append(idx)
            live = jnp.where(lane == idx, -jnp.inf, live)
        ex = [jnp.exp(v - top_v[0]) for v in top_v]
        denom = ex[0] + ex[1] + ex[2] + ex[3]
        picked = jnp.zeros((sub, LANES), F32)
        for k in range(TOP_K):
            picked = jnp.where(lane == top_i[k], 1.0, picked)
        routed.append((top_i, [e / denom for e in ex], picked,
                       jnp.dot(before, picked.astype(BF16), preferred_element_type=F32)))

    count = cnt_scr[...]
    for j, (p, (top_i, top_w, picked, within)) in enumerate(zip(parts, routed)):
        prefix = within + count[0:1, :]
        ti = jnp.zeros((sub, LANES), F32)
        tw = jnp.zeros((sub, LANES), F32)
        for k in range(TOP_K):
            rank = jnp.sum(jnp.where(lane == top_i[k], prefix, 0.0), axis=-1, keepdims=True)
            ti = jnp.where(lane == k, top_i[k], ti)
            ti = jnp.where(lane == TOP_K + k, rank, ti)
            tw = jnp.where(lane == k, top_w[k], tw)
        ti_ref[:, p] = jnp.transpose(ti)[:SUBLANES, :].astype(jnp.int32)
        tw_ref[:, p] = jnp.transpose(tw)[:SUBLANES, :]
        count = count + jnp.sum(picked, axis=0, keepdims=True)
        cnt_ref[j] = count
    cnt_scr[...] = count


def mixer_output(og, uc, proj, x, pos, mod, mod_row_of_tile, g_post, g_pre_f, ln_g, ln_b,
                 w_a, w_b, w_o, w_r, b_r, *, tm, sub):
    n, d = x.shape
    pos_tiles = pos.shape[0] // tm
    row = lambda i: (i, 0)
    fixed = lambda i: (0, 0)
    modspec = lambda k: pl.BlockSpec((1, 1, d), lambda i: (mod_row_of_tile(i), 0, k))
    vec = pl.BlockSpec((1, d), fixed)
    mat = pl.BlockSpec((d, d), fixed)
    return pl.pallas_call(
        functools.partial(_mixer_out_kernel, sub=sub),
        grid=(n // tm,),
        in_specs=[pl.BlockSpec((tm, d), row), pl.BlockSpec((tm, d), row),
                  pl.BlockSpec((tm, d), lambda i: (i, COL_GATE_A)),
                  pl.BlockSpec((tm, d), lambda i: (i, COL_GATE_B)),
                  pl.BlockSpec((tm, d), row),
                  pl.BlockSpec((tm, d), lambda i: (i % pos_tiles, 0)),
                  modspec(2), modspec(3), modspec(4),
                  vec, vec, vec, vec, mat, mat, mat,
                  pl.BlockSpec((d, LANES), fixed), pl.BlockSpec((1, LANES), fixed)],
        out_specs=[pl.BlockSpec((tm, d), row), pl.BlockSpec((tm, d), row),
                   pl.BlockSpec((SUBLANES, tm), lambda i: (0, i)), pl.BlockSpec((SUBLANES, tm), lambda i: (0, i)),
                   pl.BlockSpec((tm // sub, SUBLANES, LANES), lambda i: (i, 0, 0))],
        out_shape=[jax.ShapeDtypeStruct((n, d), F32), jax.ShapeDtypeStruct((n, d), BF16),
                   jax.ShapeDtypeStruct((SUBLANES, n), jnp.int32), jax.ShapeDtypeStruct((SUBLANES, n), F32),
                   jax.ShapeDtypeStruct((n // sub, SUBLANES, LANES), F32)],
        scratch_shapes=[pltpu.VMEM((SUBLANES, LANES), F32)],
        compiler_params=_params("arbitrary"),
        name="mixer_output",
    )(og, uc, proj, proj, x, pos, mod, mod, mod,
      g_post.reshape(1, d), g_pre_f.reshape(1, d), ln_g.reshape(1, d), ln_b.reshape(1, d),
      w_a, w_b, w_o, w_r, b_r)


EXPERT_BLOCK = 256
MOE_CHUNK = 512
GATHER_SUB = 256
GATHER_GROUP = 3
COMBINE_GRANULE = 32
COMBINE_SLOTS = 32


def _expert_kernel(bexp_ref, bnext_ref, rlo_ref, slo_ref, shi_ref, nused_ref,
                   h2_ref, ti_ref, wt_ref, wgu_hbm, bgu_ref, wd_hbm, bd_ref, o_ref,
                   acc, rw_acc, wgu_bf, wd_bf, wgu_f32, wd_f32, wsem):
    b = pl.program_id(0)
    expert = bexp_ref[b]

    def fetch(e):
        return (pltpu.make_async_copy(wgu_hbm.at[e], wgu_f32, wsem.at[0]),
                pltpu.make_async_copy(wd_hbm.at[e], wd_f32, wsem.at[1]))

    @pl.when(((b == 0) | (expert != bexp_ref[jnp.maximum(b - 1, 0)])) & (b < nused_ref[0]))
    def _():
        @pl.when(b == 0)
        def _():
            for cp in fetch(expert):
                cp.start()

        for cp in fetch(expert):
            cp.wait()
        wgu_bf[...] = wgu_f32[...].astype(BF16)
        wd_bf[...] = wd_f32[...].astype(BF16)

        @pl.when(bnext_ref[b] >= 0)
        def _():
            for cp in fetch(bnext_ref[b]):
                cp.start()

    @pl.when(b < nused_ref[0])
    def _():
        acc[...] = jnp.zeros_like(acc)
        rw_acc[...] = jnp.zeros_like(rw_acc)
        row = lax.broadcasted_iota(jnp.int32, (EXPERT_BLOCK, GATHER_SUB), 0)
        s_lo, s_hi = slo_ref[b], shi_ref[b]
        last_sub = h2_ref.shape[0] // GATHER_SUB - 1

        def select(s, live):
            t0 = pl.multiple_of(s * GATHER_SUB, GATHER_SUB)
            ids = ti_ref[0:TOP_K, pl.ds(t0, GATHER_SUB)]
            local = ti_ref[TOP_K:2 * TOP_K, pl.ds(t0, GATHER_SUB)] - rlo_ref[b]
            if live is not True:
                local = jnp.where(live, local, -1)
            inside = (ids == expert) & (local >= 0) & (local < EXPERT_BLOCK)
            hit_row = jnp.sum(jnp.where(inside, local + 1, 0), axis=0, keepdims=True) - 1
            hit_w = jnp.sum(jnp.where(inside, wt_ref[0:TOP_K, pl.ds(t0, GATHER_SUB)], 0.0), axis=0, keepdims=True)
            hit = row == hit_row
            return (jnp.where(hit, 1.0, 0.0).astype(BF16), jnp.where(hit, hit_w, 0.0),
                    h2_ref[pl.ds(t0, GATHER_SUB), :])

        def gather(i, carry):
            s0 = s_lo + GATHER_GROUP * i
            parts = [select(s0, True)] + [select(jnp.minimum(s0 + q, last_sub), s0 + q <= s_hi)
                                          for q in range(1, GATHER_GROUP)]
            acc[...] += jnp.dot(jnp.concatenate([p for p, _, _ in parts], axis=1),
                                jnp.concatenate([x for _, _, x in parts], axis=0), preferred_element_type=F32)
            rw_acc[...] += jnp.sum(functools.reduce(lambda a, c: a + c, [w for _, w, _ in parts]), axis=1,
                                   keepdims=True)
            return carry

        lax.fori_loop(0, (s_hi - s_lo + GATHER_GROUP) // GATHER_GROUP, gather, 0)
        gu = jnp.dot(acc[...].astype(BF16), wgu_bf[...], preferred_element_type=F32) + bgu_ref[0]
        dff = gu.shape[1] // 2
        gl = jnp.minimum(gu[:, :dff], SWIGLU_LIMIT)
        lin = jnp.clip(gu[:, dff:], -SWIGLU_LIMIT, SWIGLU_LIMIT)
        act = (gl * jax.nn.sigmoid(SWIGLU_ALPHA * gl) * (lin + 1.0)).astype(BF16)
        y = jnp.dot(act, wd_bf[...], preferred_element_type=F32) + bd_ref[0]
        o_ref[...] = (y * rw_acc[...]).astype(o_ref.dtype)

    @pl.when(b >= nused_ref[0])
    def _():
        o_ref[...] = jnp.zeros_like(o_ref)


def expert_blocks(h2, ti_t, wt_t, block_expert, next_expert, rank_lo, sub_lo, sub_hi, n_used, w_gate_up, b_gate_up,
                  w_down, b_down):
    n, d = h2.shape
    e, _, f2 = w_gate_up.shape
    n_blocks = block_expert.shape[0]
    whole = lambda b, be, bn, rl, lo, hi, nu: (0, 0)
    by_expert = lambda b, be, bn, rl, lo, hi, nu: (be[b], 0, 0)
    once = pl.Buffered(1)
    grid_spec = pltpu.PrefetchScalarGridSpec(
        num_scalar_prefetch=6,
        grid=(n_blocks,),
        in_specs=[pl.BlockSpec((n, d), whole, pipeline_mode=once),
                  pl.BlockSpec((SUBLANES, n), whole, pipeline_mode=once),
                  pl.BlockSpec((SUBLANES, n), whole, pipeline_mode=once),
                  pl.BlockSpec(memory_space=pl.ANY),
                  pl.BlockSpec((1, 1, f2), by_expert),
                  pl.BlockSpec(memory_space=pl.ANY),
                  pl.BlockSpec((1, 1, d), by_expert)],
        out_specs=pl.BlockSpec((EXPERT_BLOCK, d), lambda b, be, bn, rl, lo, hi, nu: (b, 0)),
        scratch_shapes=[pltpu.VMEM((EXPERT_BLOCK, d), F32), pltpu.VMEM((EXPERT_BLOCK, 1), F32),
                        pltpu.VMEM((d, f2), BF16), pltpu.VMEM((f2 // 2, d), BF16),
                        pltpu.VMEM((d, f2), F32), pltpu.VMEM((f2 // 2, d), F32),
                        pltpu.SemaphoreType.DMA((2,))],
    )
    return pl.pallas_call(
        _expert_kernel,
        grid_spec=grid_spec,
        out_shape=jax.ShapeDtypeStruct((n_blocks * EXPERT_BLOCK, d), BF16),
        compiler_params=_params("arbitrary"),
        name="expert_blocks",
    )(block_expert, next_expert, rank_lo, sub_lo, sub_hi, n_used, h2, ti_t, wt_t, w_gate_up, b_gate_up.reshape(e, 1, f2), w_down,
      b_down.reshape(e, 1, d))


def _combine_kernel(coff_ref, cgran_ref, cexp_ref, cbase_ref, yb_hbm, ti_ref, x2_ref, gf_ref, g_ref, o_ref,
                    slab, sem, acc):
    c = pl.program_id(0)
    start = coff_ref[c]
    count = coff_ref[c + 1] - start
    n_groups = (count + COMBINE_SLOTS - 1) // COMBINE_SLOTS

    def item(g, s, first=start, n=count):
        return first + jnp.minimum(g * COMBINE_SLOTS + s, n - 1)

    def copies(g, buf, first=start, n=count):
        rows = [pl.multiple_of(cgran_ref[item(g, s, first, n)] * COMBINE_GRANULE, COMBINE_GRANULE)
                for s in range(COMBINE_SLOTS)]
        return [pltpu.make_async_copy(yb_hbm.at[pl.ds(r, COMBINE_GRANULE), :],
                                      slab.at[buf, pl.ds(s * COMBINE_GRANULE, COMBINE_GRANULE), :], sem.at[buf])
                for s, r in enumerate(rows)]

    @pl.when(c == 0)
    def _():
        for cp in copies(0, 0):
            cp.start()

    acc[...] = jnp.zeros_like(acc)
    chunk = x2_ref.shape[0]
    row = lax.broadcasted_iota(jnp.int32, (COMBINE_GRANULE, chunk), 0)
    ids = ti_ref[0:TOP_K, :]
    ranks = ti_ref[TOP_K:2 * TOP_K, :]

    def group(g, carry):
        buf = g % 2

        @pl.when(g + 1 < n_groups)
        def _():
            for cp in copies(g + 1, 1 - buf):
                cp.start()

        for cp in copies(g, buf):
            cp.wait()
        pieces = []
        for s in range(COMBINE_SLOTS):
            live = g * COMBINE_SLOTS + s < count
            local = ranks - cbase_ref[item(g, s)]
            inside = (ids == jnp.where(live, cexp_ref[item(g, s)], -1)) & (local >= 0) & (local < COMBINE_GRANULE)
            hit_row = jnp.sum(jnp.where(inside, local + 1, 0), axis=0, keepdims=True) - 1
            pieces.append(jnp.where(row == hit_row, 1.0, 0.0).astype(BF16))
        onehot_t = jnp.concatenate(pieces, axis=0)
        acc[...] += lax.dot_general(onehot_t, slab[buf], (((0,), (0,)), ((), ())),
                                    preferred_element_type=F32)
        return carry

    lax.fori_loop(0, n_groups, group, 0)

    @pl.when(c + 1 < pl.num_programs(0))
    def _():
        nxt = coff_ref[c + 1]
        for cp in copies(0, 0, nxt, coff_ref[c + 2] - nxt):
            cp.start()

    y = acc[...]
    yn = y * lax.rsqrt(jnp.mean(y * y, axis=-1, keepdims=True) + EPS) * g_ref[...]
    o_ref[...] = x2_ref[...] + gf_ref[0] * yn


def combine_residual(yb, ti_t, x2, chunk_off, chunk_items, mod, mod_row_of_chunk, g_post_f):
    n, d = x2.shape
    chunk_row = lambda c, off, gran, exp, base: (c, 0)
    grid_spec = pltpu.PrefetchScalarGridSpec(
        num_scalar_prefetch=4,
        grid=(n // MOE_CHUNK,),
        in_specs=[pl.BlockSpec(memory_space=pl.ANY),
                  pl.BlockSpec((SUBLANES, MOE_CHUNK), lambda c, off, gran, exp, base: (0, c)),
                  pl.BlockSpec((MOE_CHUNK, d), chunk_row),
                  pl.BlockSpec((1, 1, d), lambda c, off, gran, exp, base: (mod_row_of_chunk(c), 0, 5)),
                  pl.BlockSpec((1, d), lambda c, off, gran, exp, base: (0, 0))],
        out_specs=pl.BlockSpec((MOE_CHUNK, d), chunk_row),
        scratch_shapes=[pltpu.VMEM((2, COMBINE_SLOTS * COMBINE_GRANULE, d), BF16),
                        pltpu.SemaphoreType.DMA((2,)),
                        pltpu.VMEM((MOE_CHUNK, d), F32)],
    )
    return pl.pallas_call(
        _combine_kernel,
        grid_spec=grid_spec,
        out_shape=jax.ShapeDtypeStruct((n, d), F32),
        compiler_params=_params("arbitrary"),
        name="combine_residual",
    )(chunk_off, *chunk_items, yb, ti_t, x2, mod, g_post_f.reshape(1, d))


def _grid_pos_embedding(rows, d):
    quarter = d // 4
    omega = POS_BASE ** (-jnp.arange(quarter, dtype=F32) / quarter)

    def emb(p):
        ang = p[:, None] * omega[None, :]
        return jnp.concatenate([jnp.sin(ang), jnp.cos(ang)], axis=-1)

    row_emb = jnp.repeat(emb(jnp.arange(rows, dtype=F32)), GRID_W, axis=0)
    col_emb = jnp.tile(emb(jnp.arange(GRID_W, dtype=F32)), (rows, 1))
    return jnp.concatenate([row_emb, col_emb], axis=-1)


def _count_le(sorted_vals, queries):
    return jnp.sum(sorted_vals <= queries[:, None], axis=1).astype(jnp.int32)


def _moe_plan(cnt_after, n_tok, tm):
    assert tm == GATHER_SUB
    sub_after = cnt_after[:, 0, :N_EXPERTS].astype(jnp.int32)
    counts = sub_after[-1]
    padded = (counts + EXPERT_BLOCK - 1) // EXPERT_BLOCK * EXPERT_BLOCK
    padded_end = jnp.cumsum(padded)
    padded_start = padded_end - padded
    n_blocks = n_tok * TOP_K // EXPERT_BLOCK + N_EXPERTS
    blocks = jnp.arange(n_blocks, dtype=jnp.int32)
    block_expert = jnp.minimum(_count_le(padded_end[None, :], blocks * EXPERT_BLOCK), N_EXPERTS - 1)
    n_used = (padded_end[-1:] // EXPERT_BLOCK).astype(jnp.int32)
    of_block = (block_expert[:, None] == jnp.arange(N_EXPERTS, dtype=jnp.int32)[None, :]).astype(jnp.int32)
    r_lo = (blocks - jnp.sum(of_block * padded_start[None, :], axis=1) // EXPERT_BLOCK) * EXPERT_BLOCK
    r_hi = jnp.minimum(jnp.sum(of_block * counts[None, :], axis=1), r_lo + EXPERT_BLOCK) - 1
    through = jnp.sum(of_block[:, :, None] * sub_after.T[None, :, :], axis=1)
    after = jnp.sum(of_block * padded_end[None, :], axis=1) // EXPERT_BLOCK
    next_expert = jnp.sum((after[:, None] == blocks[None, :]).astype(jnp.int32) * block_expert[None, :], axis=1)
    next_expert = jnp.where(after < n_used[0], next_expert, -1).astype(jnp.int32)
    n_sub = sub_after.shape[0]
    sub_lo = jnp.minimum(_count_le(through, r_lo), n_sub - 1)
    sub_hi = jnp.minimum(_count_le(through, r_hi), n_sub - 1)
    nch = n_tok // MOE_CHUNK
    per_chunk = MOE_CHUNK // tm
    cb_after = sub_after[per_chunk - 1::per_chunk]
    cb_before = jnp.concatenate([jnp.zeros((1, N_EXPERTS), jnp.int32), cb_after[:-1]], axis=0)
    b_lo = ((padded_start[None, :] + cb_before) // COMBINE_GRANULE).reshape(-1)
    b_hi = ((padded_start[None, :] + cb_after - 1) // COMBINE_GRANULE).reshape(-1)
    n_it = jnp.where((cb_after > cb_before).reshape(-1), b_hi - b_lo + 1, 0)
    off_end = jnp.cumsum(n_it)
    off = off_end - n_it
    n_items = n_blocks * (EXPERT_BLOCK // COMBINE_GRANULE) + N_EXPERTS * (nch - 1)
    w = jnp.minimum(jnp.arange(n_items, dtype=jnp.int32), off_end[-1] - 1)
    chunk_end = off_end[N_EXPERTS - 1::N_EXPERTS]
    chunk_off = jnp.concatenate([jnp.zeros((1,), jnp.int32), chunk_end]).astype(jnp.int32)
    chunk_of = jnp.minimum(_count_le(chunk_end[None, :], w), nch - 1)
    chunk_expert = jnp.minimum(_count_le(off_end.reshape(nch, N_EXPERTS)[chunk_of], w), N_EXPERTS - 1)
    cell = chunk_of * N_EXPERTS + chunk_expert
    chunk_granules = (b_lo[cell] + w - off[cell]).astype(jnp.int32)
    chunk_base = chunk_granules * COMBINE_GRANULE - padded_start[chunk_expert]
    return block_expert, next_expert, r_lo, sub_lo, sub_hi, n_used, chunk_off, (chunk_granules, chunk_expert, chunk_base)


def kernel(x, c, ctx, c_ctx, w_mod, b_mod, g_pre_mix, g_post_mix, g_pre_ffn, g_post_ffn, w_in, conv_kv,
           conv_q, a_log, dt_bias, gdn_norm_g, w_proj_a, conf_dw, conf_dw_b, conf_ln_g, conf_ln_b,
           w_proj_b, w_out, w_router, b_router, w_gate_up, b_gate_up, w_down, b_down):
    batch, seq, d = x.shape
    ctx_len = ctx.shape[1]
    n_tok = batch * seq
    gw = HEADS * DV
    beta_off = HEADS * DK + gw
    state_cols = beta_off + 4 * HEADS
    q_off = state_cols

    mod_rows = 2 * SUBLANES
    cond = jnp.zeros((mod_rows, d), F32).at[:batch].set(c).at[batch].set(c_ctx)
    mod = modulation(cond, w_mod[0], b_mod[0]).reshape(mod_rows, 1, 6 * d)

    w_t = jnp.swapaxes(w_in[0], 0, 1)
    w_main = projection_weights(w_t, keep=beta_off, skip=state_cols - beta_off)
    w_ba = jnp.zeros((LANES, d), F32).at[:4 * HEADS].set(w_t[beta_off:state_cols]).astype(BF16)

    pos = _grid_pos_embedding(seq // GRID_W, d)
    x_flat = x.reshape(n_tok, d)
    tm = 1024
    tiles_per_seq = seq // tm
    proj, ba = input_projection(x_flat, pos, mod, lambda i: i // tiles_per_seq, g_pre_mix[0],
                                w_main, w_ba, tm=tm, tn=4096)
    proj_ctx, ba_ctx = input_projection(ctx.reshape(batch * ctx_len, d), None, mod, lambda i: batch,
                                        g_pre_mix[0], w_main[:beta_off], w_ba, tm=ctx_len, tn=1024)

    og = gated_deltanet(proj, ba, proj_ctx, ba_ctx, conv_kv[0], conv_q[0], a_log[0], dt_bias[0],
                        gdn_norm_g[0], batch=batch, seq=seq, ctx=ctx_len)
    uc = conformer_conv(proj, conf_dw[0], conf_dw_b[0], batch=batch, seq=seq)

    tm2 = 512
    w_r = jnp.zeros((d, LANES), F32).at[:, :N_EXPERTS].set(w_router[0]).astype(BF16)
    b_r = jnp.zeros((1, LANES), F32).at[0, :N_EXPERTS].set(b_router[0])
    x2, h2, top_i, top_w, cnt_after = mixer_output(
        og, uc, proj, x_flat, pos, mod, lambda i: i // (seq // tm2), g_post_mix[0], g_pre_ffn[0],
        conf_ln_g[0], conf_ln_b[0], w_proj_a[0].astype(BF16), w_proj_b[0].astype(BF16),
        w_out[0].astype(BF16), w_r, b_r, tm=tm2, sub=GATHER_SUB)

    block_expert, next_expert, rank_lo, sub_lo, sub_hi, n_used, chunk_off, chunk_items = _moe_plan(
        cnt_after, n_tok, GATHER_SUB)
    yb = expert_blocks(h2, top_i, top_w, block_expert, next_expert, rank_lo, sub_lo, sub_hi, n_used, w_gate_up[0],
                       b_gate_up[0], w_down[0], b_down[0])
    out = combine_residual(yb, top_i, x2, chunk_off, chunk_items, mod, lambda c: c // (seq // MOE_CHUNK),
                           g_post_ffn[0])
    return out.reshape(batch, seq, d)
```

```python
import functools

import jax
import jax.numpy as jnp
from jax import lax
from jax.experimental import pallas as pl
from jax.experimental.pallas import tpu as pltpu

F32 = jnp.float32
BF16 = jnp.bfloat16

D_MODEL = 1024
GRID_W = 64
HEADS = 8
DK = 128
DV = 128
SHORT_CONV = 5
CHUNK = 64
CONF_KERNEL = 31
N_EXPERTS = 32
TOP_K = 4
SWIGLU_LIMIT = 7.0
SWIGLU_ALPHA = 1.702
MOE_BLOCK = 128
EPS = 1e-6
POS_BASE = 10000.0

LANES = 128
SUBLANES = 8
VMEM_LIMIT = 60 * 1024 * 1024

COL_K, COL_V, COL_Q, COL_Z, COL_GLU_A, COL_GLU_G, COL_GATE_A, COL_GATE_B = range(8)


def _params(*sem):
    return pltpu.CompilerParams(dimension_semantics=sem, vmem_limit_bytes=VMEM_LIMIT)


def _mod_kernel(c_ref, w_ref, b_ref, o_ref):
    c = c_ref[...]
    s = c * jax.nn.sigmoid(c)
    o_ref[...] = jnp.dot(s, w_ref[...], preferred_element_type=F32,
                         precision=lax.Precision.HIGHEST) + b_ref[...]


def modulation(cond, w_mod, b_mod):
    r, d = cond.shape
    n = w_mod.shape[1]
    tn = 1024
    return pl.pallas_call(
        _mod_kernel,
        grid=(n // tn,),
        in_specs=[pl.BlockSpec((r, d), lambda j: (0, 0)),
                  pl.BlockSpec((d, tn), lambda j: (0, j)),
                  pl.BlockSpec((1, tn), lambda j: (0, j))],
        out_specs=pl.BlockSpec((r, tn), lambda j: (0, j)),
        out_shape=jax.ShapeDtypeStruct((r, n), F32),
        compiler_params=_params("arbitrary"),
        name="modulation",
    )(cond, w_mod, b_mod.reshape(1, n))


def _proj_weights_kernel(a_ref, b_ref, o_ref, *, first_shifted, shift):
    @pl.when(pl.program_id(0) < first_shifted)
    def _():
        o_ref[...] = a_ref[...].astype(o_ref.dtype)

    @pl.when(pl.program_id(0) >= first_shifted)
    def _():
        o_ref[...] = jnp.concatenate([a_ref[shift:, :], b_ref[...]], axis=0).astype(o_ref.dtype)


def projection_weights(w_t, *, keep, skip, tr=1024):
    total, d = w_t.shape
    n_out = total - skip
    assert keep % tr == 0 and n_out % tr == 0 and tr % skip == 0 and skip % SUBLANES == 0
    return pl.pallas_call(
        functools.partial(_proj_weights_kernel, first_shifted=keep // tr, shift=skip),
        grid=(n_out // tr,),
        in_specs=[pl.BlockSpec((tr, d), lambda j: (j, 0)),
                  pl.BlockSpec((skip, d), lambda j: ((j + 1) * (tr // skip), 0))],
        out_specs=pl.BlockSpec((tr, d), lambda j: (j, 0)),
        out_shape=jax.ShapeDtypeStruct((n_out, d), BF16),
        compiler_params=_params("arbitrary"),
        name="projection_weights",
    )(w_t, w_t)


_NT = (((1,), (1,)), ((), ()))


def _inproj_kernel(*refs, has_pos):
    if has_pos:
        x_ref, pos_ref, sh_ref, sc_ref, g_ref, w_ref, wba_ref, o_ref, ba_ref, h_scr = refs
    else:
        x_ref, sh_ref, sc_ref, g_ref, w_ref, wba_ref, o_ref, ba_ref, h_scr = refs

    @pl.when(pl.program_id(1) == 0)
    def _():
        x = x_ref[...]
        if has_pos:
            x = x + pos_ref[...]
        y = x * lax.rsqrt(jnp.mean(x * x, axis=-1, keepdims=True) + EPS) * g_ref[...]
        h = (y * (1.0 + sc_ref[0]) + sh_ref[0]).astype(BF16)
        h_scr[...] = h
        ba_ref[...] = lax.dot_general(h, wba_ref[...], _NT, preferred_element_type=F32)

    o_ref[...] = lax.dot_general(h_scr[...], w_ref[...], _NT, preferred_element_type=F32).astype(o_ref.dtype)


def input_projection(x, pos, mod, mod_row_of_tile, g_pre, w_main, w_ba, *, tm, tn):
    n, d = x.shape
    w = w_main.shape[0]
    assert n % tm == 0 and w % tn == 0
    has_pos = pos is not None
    in_specs = [pl.BlockSpec((tm, d), lambda i, j: (i, 0))]
    args = [x]
    if has_pos:
        pos_tiles = pos.shape[0] // tm
        in_specs.append(pl.BlockSpec((tm, d), lambda i, j: (i % pos_tiles, 0)))
        args.append(pos)
    in_specs += [
        pl.BlockSpec((1, 1, d), lambda i, j: (mod_row_of_tile(i), 0, 0)),
        pl.BlockSpec((1, 1, d), lambda i, j: (mod_row_of_tile(i), 0, 1)),
        pl.BlockSpec((1, d), lambda i, j: (0, 0)),
        pl.BlockSpec((tn, d), lambda i, j: (j, 0)),
        pl.BlockSpec((LANES, d), lambda i, j: (0, 0)),
    ]
    args += [mod, mod, g_pre.reshape(1, d), w_main, w_ba]
    return pl.pallas_call(
        functools.partial(_inproj_kernel, has_pos=has_pos),
        grid=(n // tm, w // tn),
        in_specs=in_specs,
        out_specs=[pl.BlockSpec((tm, tn), lambda i, j: (i, j)),
                   pl.BlockSpec((tm, LANES), lambda i, j: (i, 0))],
        out_shape=[jax.ShapeDtypeStruct((n, w), BF16), jax.ShapeDtypeStruct((n, LANES), F32)],
        scratch_shapes=[pltpu.VMEM((tm, d), BF16)],
        compiler_params=_params("arbitrary", "arbitrary"),
        name="input_projection_pos" if has_pos else "input_projection_ctx",
    )(*args)


def _silu(x):
    return x * jax.nn.sigmoid(x)


def _softplus(x):
    return jnp.maximum(x, 0.0) + jnp.log(1.0 + jnp.exp(-jnp.abs(x)))


def _short_conv(src, taps, pad_scr, rows):
    half = SHORT_CONV // 2
    pad_scr[0:SUBLANES, :] = jnp.zeros((SUBLANES, LANES), F32)
    pad_scr[SUBLANES:SUBLANES + rows, :] = src.astype(F32)
    pad_scr[SUBLANES + rows:2 * SUBLANES + rows, :] = jnp.zeros((SUBLANES, LANES), F32)
    acc = None
    for j in range(SHORT_CONV):
        off = SUBLANES + j - half
        term = pad_scr[off:off + rows, :] * taps[j:j + 1, :]
        acc = term if acc is None else acc + term
    return _silu(acc)


def _l2n(x):
    return x * lax.rsqrt(jnp.sum(x * x, axis=-1, keepdims=True) + EPS)


def _gdn_kernel(proj_hbm, ba_hbm, kc_ref, vc_ref, bac_ref,
                wk_ref, wv_ref, wq_ref, ab_ref, gn_ref, o_ref,
                k_ref, v_ref, q_ref, z_ref, ba_ref, in_sem,
                kf, vf, qf, gates, pad_scr, lhs, val, att, ktt, dec, osc, *, seq, ctx, unroll, group):
    total = ctx + seq
    n_batch, n_groups = pl.num_programs(0), pl.num_programs(1)
    gl = group * LANES

    def block(bb, hg, base):
        return proj_hbm.at[pl.ds(pl.multiple_of(bb * seq, seq), seq),
                           pl.ds(pl.multiple_of((base * n_groups + hg) * gl, gl), gl)]

    def input_copies(bb, hg):
        return [pltpu.make_async_copy(block(bb, hg, COL_K), k_ref, in_sem.at[0]),
                pltpu.make_async_copy(block(bb, hg, COL_V), v_ref, in_sem.at[1]),
                pltpu.make_async_copy(block(bb, hg, COL_Q), q_ref, in_sem.at[2]),
                pltpu.make_async_copy(ba_hbm.at[pl.ds(pl.multiple_of(bb * seq, seq), seq), :], ba_ref, in_sem.at[3])]

    b_now, g_now = pl.program_id(0), pl.program_id(1)
    z_copy = pltpu.make_async_copy(block(b_now, g_now, COL_Z), z_ref, in_sem.at[4])

    @pl.when((b_now == 0) & (g_now == 0))
    def _():
        for cp in input_copies(b_now, g_now):
            cp.start()

    for cp in input_copies(b_now, g_now):
        cp.wait()
    z_copy.start()
    nc = total // CHUNK
    ncc = ctx // CHUNK
    two = 2 * CHUNK
    lg_chunk = CHUNK.bit_length() - 1

    ri = lax.broadcasted_iota(jnp.int32, (two, two), 0)
    ci = lax.broadcasted_iota(jnp.int32, (two, two), 1)
    same_dir = (ri >> lg_chunk) == (ci >> lg_chunk)
    incl = same_dir & (((ri < CHUNK) & (ri >= ci)) | ((ri >= CHUNK) & (ri <= ci)))
    strict = incl & (ri != ci)
    eye = jnp.where(ri == ci, 1.0, 0.0)
    tri = jnp.where(incl, 1.0, 0.0).astype(BF16)
    level_masks = []
    for lg in range(lg_chunk):
        same_parent = (ri >> (lg + 1)) == (ci >> (lg + 1))
        level_masks.append(jnp.where(same_parent & ((ri >> lg) != (ci >> lg)), 1.0, 0.0))
    top_rows = lax.broadcasted_iota(jnp.int32, (two, LANES), 0) < CHUNK

    def bwd_chunk(t):
        return jnp.where(t < ncc, ncc - 1 - t, nc + ncc - 1 - t)

    def stacked(ref, rf, rb):
        return jnp.concatenate([ref[pl.ds(rf, CHUNK), :], ref[pl.ds(rb, CHUNK), :]], axis=0)

    def chunk_load(t):
        rf = pl.multiple_of(t * CHUNK, CHUNK)
        rb = pl.multiple_of(bwd_chunk(t) * CHUNK, CHUNK)
        beta = jnp.concatenate([gates[0, pl.ds(rf, CHUNK), :], gates[1, pl.ds(rb, CHUNK), :]], axis=0)
        g = jnp.concatenate([gates[2, pl.ds(rf, CHUNK), :], gates[3, pl.ds(rb, CHUNK), :]], axis=0)
        return stacked(kf, rf, rb), stacked(vf, rf, rb), stacked(qf, rf, rb), beta, g

    def chunk_triangle(k, q, beta, g):
        kb = k.astype(BF16)
        kq = jnp.concatenate([kb, q.astype(BF16)], axis=0)
        kk_qk = lax.dot_general(kq, kb, (((1,), (1,)), ((), ())), preferred_element_type=F32)
        g_hi = g.astype(BF16)
        r1 = g - g_hi.astype(F32)
        g_mid = r1.astype(BF16)
        g_lo = (r1 - g_mid.astype(F32)).astype(BF16)
        gc3 = jnp.dot(tri, jnp.concatenate([g_hi, g_mid, g_lo], axis=1), preferred_element_type=F32)
        gc = gc3[:, :LANES] + gc3[:, LANES:2 * LANES] + gc3[:, 2 * LANES:]
        gc_row = jnp.transpose(gc)
        decay = jnp.where(incl, jnp.exp(gc - gc_row), 0.0)
        a = jnp.where(strict, beta * kk_qk[:two] * decay, 0.0)
        return a, (kk_qk[two:] * decay).astype(BF16), gc

    def chunk_finish(hh, t, k, v, q, beta, gc, attn, tmat):
        eg = jnp.exp(gc)
        rhs = jnp.concatenate([v * beta, k * beta * eg], axis=1).astype(BF16)
        sol = jnp.dot(tmat.astype(BF16), rhs, preferred_element_type=F32)
        g_end = jnp.where(top_rows, jnp.broadcast_to(gc[CHUNK - 1:CHUNK, :], (two, LANES)),
                          jnp.broadcast_to(gc[CHUNK:CHUNK + 1, :], (two, LANES)))
        k_tail = k * jnp.exp(g_end - gc)
        qg = (q * eg).astype(BF16)
        kcum = sol[:, DV:].astype(BF16)
        r0 = pl.multiple_of(t * two, two)
        lhs[hh, 0, pl.ds(r0, two), :] = jnp.concatenate([kcum[:CHUNK], qg[:CHUNK]], axis=0)
        lhs[hh, 1, pl.ds(r0, two), :] = jnp.concatenate([kcum[CHUNK:], qg[CHUNK:]], axis=0)
        val[hh, pl.ds(r0, two), :] = sol[:, :DV].astype(BF16)
        att[hh, pl.ds(r0, two), :] = attn
        ktt[hh, pl.ds(r0, two), :] = jnp.transpose(k_tail).astype(BF16)
        d0 = pl.multiple_of(t * SUBLANES, SUBLANES)
        e_end = jnp.exp(g_end)
        dec[hh, 0, pl.ds(d0, SUBLANES), :] = e_end[:SUBLANES]
        dec[hh, 1, pl.ds(d0, SUBLANES), :] = e_end[CHUNK:CHUNK + SUBLANES]

    for hh in range(group):
        head = pl.program_id(1) * group + hh
        cols = slice(hh * LANES, (hh + 1) * LANES)

        kf[0:ctx, :] = _l2n(_short_conv(kc_ref[:, cols], wk_ref[:, cols], pad_scr, ctx))
        vf[0:ctx, :] = _short_conv(vc_ref[:, cols], wv_ref[:, cols], pad_scr, ctx)
        qf[0:ctx, :] = jnp.zeros((ctx, LANES), F32)
        kf[ctx:total, :] = _l2n(_short_conv(k_ref[:, cols], wk_ref[:, cols], pad_scr, seq))
        vf[ctx:total, :] = _short_conv(v_ref[:, cols], wv_ref[:, cols], pad_scr, seq)
        qf[ctx:total, :] = _l2n(_short_conv(q_ref[:, cols], wq_ref[:, cols], pad_scr, seq)) * (DK ** -0.5)

        def gate_cols(src_ref, lo, rows):
            x = src_ref[...]
            lane = lax.broadcasted_iota(jnp.int32, (rows, LANES), 1)
            beta = jax.nn.sigmoid(x)
            g = -jnp.exp(ab_ref[0:1, :]) * _softplus(x + ab_ref[1:2, :])
            for slot, (arr, base) in enumerate(((beta, 0), (beta, HEADS), (g, 2 * HEADS), (g, 3 * HEADS))):
                col = jnp.sum(jnp.where(lane == base + head, arr, 0.0), axis=-1, keepdims=True)
                gates[slot, lo:lo + rows, :] = jnp.broadcast_to(col, (rows, LANES))

        gate_cols(bac_ref, 0, ctx)
        gate_cols(ba_ref, ctx, seq)

        def chunk_body(i, carry, hh=hh):
            steps = [i * unroll + u for u in range(unroll)]
            loaded = [chunk_load(t) for t in steps]
            tris = [chunk_triangle(k, q, beta, g) for (k, v, q, beta, g) in loaded]
            tmats = [eye - a * level_masks[0] for (a, _, _) in tris]
            for m in level_masks[1:]:
                nxt = []
                for (a, _, _), tmat in zip(tris, tmats):
                    cs = (a * m).astype(BF16)
                    tb = tmat.astype(BF16)
                    tc = jnp.dot(tb, cs, preferred_element_type=F32)
                    nxt.append(tmat - jnp.dot(tc.astype(BF16), tb, preferred_element_type=F32))
                tmats = nxt
            for t, (k, v, q, beta, g), (a, attn, gc), tmat in zip(steps, loaded, tris, tmats):
                chunk_finish(hh, t, k, v, q, beta, gc, attn, tmat)
            return carry

        lax.fori_loop(0, nc // unroll, chunk_body, 0)

    wrap = g_now + 1 == n_groups

    @pl.when(jnp.logical_not(wrap & (b_now + 1 == n_batch)))
    def _():
        for cp in input_copies(jnp.where(wrap, b_now + 1, b_now), jnp.where(wrap, 0, g_now + 1)):
            cp.start()

    def scan_body(t, carry):
        rf = pl.multiple_of(t * CHUNK, CHUNK)
        rb = pl.multiple_of(bwd_chunk(t) * CHUNK, CHUNK)
        r0 = pl.multiple_of(t * two, two)
        d0 = pl.multiple_of(t * SUBLANES, SUBLANES)
        rs = [(jnp.dot(lhs[hh, 0, pl.ds(r0, two), :], carry[2 * hh].astype(BF16), preferred_element_type=F32),
               jnp.dot(lhs[hh, 1, pl.ds(r0, two), :], carry[2 * hh + 1].astype(BF16), preferred_element_type=F32))
              for hh in range(group)]
        v_news = [val[hh, pl.ds(r0, two), :].astype(F32) - jnp.concatenate([r_f[:CHUNK], r_b[:CHUNK]], axis=0)
                  for hh, (r_f, r_b) in enumerate(rs)]
        outs, new_states = [], []
        for hh, ((r_f, r_b), v_new) in enumerate(zip(rs, v_news)):
            zeros = jnp.zeros_like(v_new)
            v_bd = jnp.concatenate([jnp.where(top_rows, v_new, zeros), jnp.where(top_rows, zeros, v_new)],
                                   axis=1).astype(BF16)
            upd = jnp.dot(ktt[hh, pl.ds(r0, two), :], v_bd, preferred_element_type=F32)
            new_states.append(carry[2 * hh] * dec[hh, 0, pl.ds(d0, SUBLANES), :][0:1, :] + upd[:, :DV])
            new_states.append(carry[2 * hh + 1] * dec[hh, 1, pl.ds(d0, SUBLANES), :][0:1, :] + upd[:, DV:])
            outs.append(jnp.concatenate([r_f[CHUNK:], r_b[CHUNK:]], axis=0)
                        + jnp.dot(att[hh, pl.ds(r0, two), :], v_new.astype(BF16), preferred_element_type=F32))
        for hh, o in enumerate(outs):
            osc[hh, pl.ds(rf, CHUNK), :] += o[:CHUNK]
            osc[hh, pl.ds(rb, CHUNK), :] += o[CHUNK:]
        return tuple(new_states)

    osc[...] = jnp.zeros_like(osc)
    zero = jnp.zeros((DK, DV), F32)
    lax.fori_loop(0, nc, scan_body, (zero,) * (2 * group))

    z_copy.wait()
    for hh in range(group):
        cols = slice(hh * LANES, (hh + 1) * LANES)
        o = osc[hh, ctx:total, :]
        o = o * lax.rsqrt(jnp.mean(o * o, axis=-1, keepdims=True) + EPS) * gn_ref[...]
        o_ref[:, cols] = (o * _silu(z_ref[:, cols].astype(F32))).astype(o_ref.dtype)


def gated_deltanet(proj, ba, proj_ctx, ba_ctx, conv_kv, conv_q, a_log, dt_bias, gn, *, batch, seq, ctx,
                   unroll=18, group=4):
    total = seq + ctx
    nc = total // CHUNK
    assert seq % CHUNK == 0 and ctx % CHUNK == 0 and nc % unroll == 0 and HEADS % group == 0
    gl = group * LANES
    ng = HEADS // group
    taps = jnp.zeros((SUBLANES, conv_kv.shape[1]), F32).at[:SHORT_CONV].set(conv_kv)
    taps_q = jnp.zeros((SUBLANES, conv_q.shape[1]), F32).at[:SHORT_CONV].set(conv_q)
    ab = jnp.zeros((SUBLANES, LANES), F32)
    ab = ab.at[0, 2 * HEADS:4 * HEADS].set(a_log.reshape(-1)).at[1, 2 * HEADS:4 * HEADS].set(dt_bias.reshape(-1))
    col = lambda base: (lambda b, h: (b, base * ng + h))
    return pl.pallas_call(
        functools.partial(_gdn_kernel, seq=seq, ctx=ctx, unroll=unroll, group=group),
        grid=(batch, ng),
        in_specs=[
            pl.BlockSpec(memory_space=pl.ANY),
            pl.BlockSpec(memory_space=pl.ANY),
            pl.BlockSpec((ctx, gl), col(COL_K)),
            pl.BlockSpec((ctx, gl), col(COL_V)),
            pl.BlockSpec((ctx, LANES), lambda b, h: (b, 0)),
            pl.BlockSpec((SUBLANES, gl), lambda b, h: (0, h)),
            pl.BlockSpec((SUBLANES, gl), lambda b, h: (0, ng + h)),
            pl.BlockSpec((SUBLANES, gl), lambda b, h: (0, h)),
            pl.BlockSpec((SUBLANES, LANES), lambda b, h: (0, 0)),
            pl.BlockSpec((1, LANES), lambda b, h: (0, 0)),
        ],
        out_specs=pl.BlockSpec((seq, gl), lambda b, h: (b, h)),
        out_shape=jax.ShapeDtypeStruct((batch * seq, HEADS * DV), BF16),
        scratch_shapes=[
            pltpu.VMEM((seq, gl), BF16),
            pltpu.VMEM((seq, gl), BF16),
            pltpu.VMEM((seq, gl), BF16),
            pltpu.VMEM((seq, gl), BF16),
            pltpu.VMEM((seq, LANES), F32),
            pltpu.SemaphoreType.DMA((5,)),
            pltpu.VMEM((total, LANES), F32),
            pltpu.VMEM((total, LANES), F32),
            pltpu.VMEM((total, LANES), F32),
            pltpu.VMEM((4, total, LANES), F32),
            pltpu.VMEM((seq + 2 * SUBLANES, LANES), F32),
            pltpu.VMEM((group, 2, nc * 2 * CHUNK, LANES), BF16),
            pltpu.VMEM((group, nc * 2 * CHUNK, LANES), BF16),
            pltpu.VMEM((group, nc * 2 * CHUNK, LANES), BF16),
            pltpu.VMEM((group, nc * 2 * CHUNK, LANES), BF16),
            pltpu.VMEM((group, 2, nc * SUBLANES, LANES), F32),
            pltpu.VMEM((group, total, LANES), F32),
        ],
        compiler_params=_params("arbitrary", "arbitrary"),
        name="gated_deltanet",
    )(proj, ba, proj_ctx, proj_ctx, ba_ctx, taps, taps, taps_q, ab, gn.reshape(1, DV))


CONF_ROWS = 512
CONF_HALO = 16


def _conf_kernel(a_ref, g_ref, w_ref, b_ref, o_ref, pad_scr, *, seq):
    cb = a_ref.shape[1]
    u = a_ref[...].astype(F32) * jax.nn.sigmoid(g_ref[...].astype(F32))
    pad_scr[0:CONF_HALO, :] = jnp.zeros((CONF_HALO, cb), F32)
    pad_scr[CONF_HALO:CONF_HALO + seq, :] = u
    pad_scr[CONF_HALO + seq:2 * CONF_HALO + seq, :] = jnp.zeros((CONF_HALO, cb), F32)
    win = CONF_ROWS + 2 * CONF_HALO
    first = CONF_HALO - CONF_KERNEL // 2

    def body(i, carry):
        r0 = pl.multiple_of(i * CONF_ROWS, CONF_ROWS)
        w = pad_scr[pl.ds(r0, win), :]
        acc = jnp.zeros((CONF_ROWS, cb), F32) + b_ref[...]
        for sub in range(SUBLANES):
            shifted = w if sub == 0 else pltpu.roll(w, win - sub, axis=0)
            for j in range(CONF_KERNEL):
                off = first + j
                if off % SUBLANES == sub:
                    base = off - sub
                    acc = acc + shifted[base:base + CONF_ROWS, :] * w_ref[j:j + 1, :]
        o_ref[pl.ds(r0, CONF_ROWS), :] = acc.astype(o_ref.dtype)
        return carry

    lax.fori_loop(0, seq // CONF_ROWS, body, 0)


def conformer_conv(proj, conf_dw, conf_dw_b, *, batch, seq, cb=128):
    c = conf_dw.shape[1]
    assert seq % CONF_ROWS == 0 and c % cb == 0
    nb = c // cb
    taps = jnp.zeros((32, c), F32).at[:CONF_KERNEL].set(conf_dw)
    return pl.pallas_call(
        functools.partial(_conf_kernel, seq=seq),
        grid=(batch, nb),
        in_specs=[pl.BlockSpec((seq, cb), lambda b, j: (b, COL_GLU_A * nb + j)),
                  pl.BlockSpec((seq, cb), lambda b, j: (b, COL_GLU_G * nb + j)),
                  pl.BlockSpec((32, cb), lambda b, j: (0, j)),
                  pl.BlockSpec((1, cb), lambda b, j: (0, j))],
        out_specs=pl.BlockSpec((seq, cb), lambda b, j: (b, j)),
        out_shape=jax.ShapeDtypeStruct((batch * seq, c), BF16),
        scratch_shapes=[pltpu.VMEM((seq + 2 * CONF_HALO, cb), F32)],
        compiler_params=_params("arbitrary", "arbitrary"),
        name="conformer_conv",
    )(proj, proj, taps, conf_dw_b.reshape(1, c))


def _mixer_out_kernel(og_ref, uc_ref, ga_ref, gb_ref, x_ref, pos_ref, gm_ref, shf_ref, scf_ref,
                      gpost_ref, gpre_ref, lng_ref, lnb_ref, wa_ref, wb_ref, wo_ref, wr_ref, br_ref,
                      x2_ref, h2_ref, ti_ref, tw_ref, cnt_ref, cnt_scr, *, sub):
    parts = [slice(r, r + sub) for r in range(0, og_ref.shape[0], sub)]

    def rms(v):
        return v * lax.rsqrt(jnp.mean(v * v, axis=-1, keepdims=True) + EPS)

    def layer_norm_silu(uc):
        mu = jnp.mean(uc, axis=-1, keepdims=True)
        var = jnp.mean(jnp.square(uc - mu), axis=-1, keepdims=True)
        return _silu((uc - mu) * lax.rsqrt(var + EPS) * lng_ref[...] + lnb_ref[...]).astype(BF16)

    us = [layer_norm_silu(uc_ref[p, :].astype(F32)) for p in parts]
    yas = [jnp.dot(og_ref[p, :], wa_ref[...], preferred_element_type=F32) for p in parts]
    ybs = [jnp.dot(u, wb_ref[...], preferred_element_type=F32) for u in us]
    merged = [(jax.nn.sigmoid(ga_ref[p, :].astype(F32)) * ya
               + jax.nn.sigmoid(gb_ref[p, :].astype(F32)) * yb).astype(BF16) for p, ya, yb in zip(parts, yas, ybs)]
    ys = [jnp.dot(m, wo_ref[...], preferred_element_type=F32) for m in merged]
    x2s = [x_ref[p, :] + pos_ref[p, :] + gm_ref[0] * (rms(y) * gpost_ref[...]) for p, y in zip(parts, ys)]
    h2s = [(rms(x2) * gpre_ref[...] * (1.0 + scf_ref[0]) + shf_ref[0]).astype(BF16) for x2 in x2s]
    for p, x2, h2 in zip(parts, x2s, h2s):
        x2_ref[p, :] = x2
        h2_ref[p, :] = h2
    all_logits = [jnp.dot(h2, wr_ref[...], preferred_element_type=F32) + br_ref[...] for h2 in h2s]
    lane = lax.broadcasted_iota(jnp.int32, (sub, LANES), 1).astype(F32)
    ri = lax.broadcasted_iota(jnp.int32, (sub, sub), 0)
    ci = lax.broadcasted_iota(jnp.int32, (sub, sub), 1)
    before = jnp.where(ri > ci, 1.0, 0.0).astype(BF16)

    @pl.when(pl.program_id(0) == 0)
    def _():
        cnt_scr[...] = jnp.zeros_like(cnt_scr)

    routed = []
    for logits in all_logits:
        live = jnp.where(lane < N_EXPERTS, logits, -jnp.inf)
        top_v, top_i = [], []
        for _ in range(TOP_K):
            m = jnp.max(live, axis=-1, keepdims=True)
            idx = jnp.min(jnp.where(live == m, lane, float(LANES)), axis=-1, keepdims=True)
            top_v.append(m)
            top_i.append(idx)
            live = jnp.where(lane == idx, -jnp.inf, live)
        ex = [jnp.exp(v - top_v[0]) for v in top_v]
        denom = ex[0] + ex[1] + ex[2] + ex[3]
        picked = jnp.zeros((sub, LANES), F32)
        for k in range(TOP_K):
            picked = jnp.where(lane == top_i[k], 1.0, picked)
        routed.append((top_i, [e / denom for e in ex], picked,
                       jnp.dot(before, picked.astype(BF16), preferred_element_type=F32)))

    count = cnt_scr[...]
    for j, (p, (top_i, top_w, picked, within)) in enumerate(zip(parts, routed)):
        prefix = within + count[0:1, :]
        ti = jnp.zeros((sub, LANES), F32)
        tw = jnp.zeros((sub, LANES), F32)
        for k in range(TOP_K):
            rank = jnp.sum(jnp.where(lane == top_i[k], prefix, 0.0), axis=-1, keepdims=True)
            ti = jnp.where(lane == k, top_i[k], ti)
            ti = jnp.where(lane == TOP_K + k, rank, ti)
            tw = jnp.where(lane == k, top_w[k], tw)
        ti_ref[:, p] = jnp.transpose(ti)[:SUBLANES, :].astype(jnp.int32)
        tw_ref[:, p] = jnp.transpose(tw)[:SUBLANES, :]
        count = count + jnp.sum(picked, axis=0, keepdims=True)
        cnt_ref[j] = count
    cnt_scr[...] = count


def mixer_output(og, uc, proj, x, pos, mod, mod_row_of_tile, g_post, g_pre_f, ln_g, ln_b,
                 w_a, w_b, w_o, w_r, b_r, *, tm, sub):
    n, d = x.shape
    assert n % tm == 0 and pos.shape[0] % tm == 0 and tm % sub == 0
    pos_tiles = pos.shape[0] // tm
    row = lambda i: (i, 0)
    fixed = lambda i: (0, 0)
    modspec = lambda k: pl.BlockSpec((1, 1, d), lambda i: (mod_row_of_tile(i), 0, k))
    vec = pl.BlockSpec((1, d), fixed)
    mat = pl.BlockSpec((d, d), fixed)
    return pl.pallas_call(
        functools.partial(_mixer_out_kernel, sub=sub),
        grid=(n // tm,),
        in_specs=[pl.BlockSpec((tm, d), row), pl.BlockSpec((tm, d), row),
                  pl.BlockSpec((tm, d), lambda i: (i, COL_GATE_A)),
                  pl.BlockSpec((tm, d), lambda i: (i, COL_GATE_B)),
                  pl.BlockSpec((tm, d), row),
                  pl.BlockSpec((tm, d), lambda i: (i % pos_tiles, 0)),
                  modspec(2), modspec(3), modspec(4),
                  vec, vec, vec, vec, mat, mat, mat,
                  pl.BlockSpec((d, LANES), fixed), pl.BlockSpec((1, LANES), fixed)],
        out_specs=[pl.BlockSpec((tm, d), row), pl.BlockSpec((tm, d), row),
                   pl.BlockSpec((SUBLANES, tm), lambda i: (0, i)), pl.BlockSpec((SUBLANES, tm), lambda i: (0, i)),
                   pl.BlockSpec((tm // sub, SUBLANES, LANES), lambda i: (i, 0, 0))],
        out_shape=[jax.ShapeDtypeStruct((n, d), F32), jax.ShapeDtypeStruct((n, d), BF16),
                   jax.ShapeDtypeStruct((SUBLANES, n), jnp.int32), jax.ShapeDtypeStruct((SUBLANES, n), F32),
                   jax.ShapeDtypeStruct((n // sub, SUBLANES, LANES), F32)],
        scratch_shapes=[pltpu.VMEM((SUBLANES, LANES), F32)],
        compiler_params=_params("arbitrary"),
        name="mixer_output",
    )(og, uc, proj, proj, x, pos, mod, mod, mod,
      g_post.reshape(1, d), g_pre_f.reshape(1, d), ln_g.reshape(1, d), ln_b.reshape(1, d),
      w_a, w_b, w_o, w_r, b_r)


EXPERT_BLOCK = 256
MOE_CHUNK = 512
GATHER_SUB = 256
GATHER_GROUP = 3
COMBINE_GRANULE = 32
COMBINE_SLOTS = 32


def _expert_kernel(bexp_ref, bnext_ref, rlo_ref, slo_ref, shi_ref, nused_ref,
                   h2_ref, ti_ref, wt_ref, wgu_hbm, bgu_ref, wd_hbm, bd_ref, o_ref,
                   acc, rw_acc, wgu_bf, wd_bf, wgu_f32, wd_f32, wsem):
    b = pl.program_id(0)
    expert = bexp_ref[b]

    def fetch(e):
        return (pltpu.make_async_copy(wgu_hbm.at[e], wgu_f32, wsem.at[0]),
                pltpu.make_async_copy(wd_hbm.at[e], wd_f32, wsem.at[1]))

    @pl.when(((b == 0) | (expert != bexp_ref[jnp.maximum(b - 1, 0)])) & (b < nused_ref[0]))
    def _():
        @pl.when(b == 0)
        def _():
            for cp in fetch(expert):
                cp.start()

        for cp in fetch(expert):
            cp.wait()
        wgu_bf[...] = wgu_f32[...].astype(BF16)
        wd_bf[...] = wd_f32[...].astype(BF16)

        @pl.when(bnext_ref[b] >= 0)
        def _():
            for cp in fetch(bnext_ref[b]):
                cp.start()

    @pl.when(b < nused_ref[0])
    def _():
        acc[...] = jnp.zeros_like(acc)
        rw_acc[...] = jnp.zeros_like(rw_acc)
        row = lax.broadcasted_iota(jnp.int32, (EXPERT_BLOCK, GATHER_SUB), 0)
        s_lo, s_hi = slo_ref[b], shi_ref[b]
        last_sub = h2_ref.shape[0] // GATHER_SUB - 1

        def select(s, live):
            t0 = pl.multiple_of(s * GATHER_SUB, GATHER_SUB)
            ids = ti_ref[0:TOP_K, pl.ds(t0, GATHER_SUB)]
            local = ti_ref[TOP_K:2 * TOP_K, pl.ds(t0, GATHER_SUB)] - rlo_ref[b]
            if live is not True:
                local = jnp.where(live, local, -1)
            inside = (ids == expert) & (local >= 0) & (local < EXPERT_BLOCK)
            hit_row = jnp.sum(jnp.where(inside, local + 1, 0), axis=0, keepdims=True) - 1
            hit_w = jnp.sum(jnp.where(inside, wt_ref[0:TOP_K, pl.ds(t0, GATHER_SUB)], 0.0), axis=0, keepdims=True)
            hit = row == hit_row
            return (jnp.where(hit, 1.0, 0.0).astype(BF16), jnp.where(hit, hit_w, 0.0),
                    h2_ref[pl.ds(t0, GATHER_SUB), :])

        def gather(i, carry):
            s0 = s_lo + GATHER_GROUP * i
            parts = [select(s0, True)] + [select(jnp.minimum(s0 + q, last_sub), s0 + q <= s_hi)
                                          for q in range(1, GATHER_GROUP)]
            acc[...] += jnp.dot(jnp.concatenate([p for p, _, _ in parts], axis=1),
                                jnp.concatenate([x for _, _, x in parts], axis=0), preferred_element_type=F32)
            rw_acc[...] += jnp.sum(functools.reduce(lambda a, c: a + c, [w for _, w, _ in parts]), axis=1,
                                   keepdims=True)
            return carry

        lax.fori_loop(0, (s_hi - s_lo + GATHER_GROUP) // GATHER_GROUP, gather, 0)
        gu = jnp.dot(acc[...].astype(BF16), wgu_bf[...], preferred_element_type=F32) + bgu_ref[0]
        dff = gu.shape[1] // 2
        gl = jnp.minimum(gu[:, :dff], SWIGLU_LIMIT)
        lin = jnp.clip(gu[:, dff:], -SWIGLU_LIMIT, SWIGLU_LIMIT)
        act = (gl * jax.nn.sigmoid(SWIGLU_ALPHA * gl) * (lin + 1.0)).astype(BF16)
        y = jnp.dot(act, wd_bf[...], preferred_element_type=F32) + bd_ref[0]
        o_ref[...] = (y * rw_acc[...]).astype(o_ref.dtype)

    @pl.when(b >= nused_ref[0])
    def _():
        o_ref[...] = jnp.zeros_like(o_ref)


def expert_blocks(h2, ti_t, wt_t, block_expert, next_expert, rank_lo, sub_lo, sub_hi, n_used, w_gate_up, b_gate_up,
                  w_down, b_down):
    n, d = h2.shape
    e, _, f2 = w_gate_up.shape
    n_blocks = block_expert.shape[0]
    whole = lambda b, be, bn, rl, lo, hi, nu: (0, 0)
    by_expert = lambda b, be, bn, rl, lo, hi, nu: (be[b], 0, 0)
    once = pl.Buffered(1)
    grid_spec = pltpu.PrefetchScalarGridSpec(
        num_scalar_prefetch=6,
        grid=(n_blocks,),
        in_specs=[pl.BlockSpec((n, d), whole, pipeline_mode=once),
                  pl.BlockSpec((SUBLANES, n), whole, pipeline_mode=once),
                  pl.BlockSpec((SUBLANES, n), whole, pipeline_mode=once),
                  pl.BlockSpec(memory_space=pl.ANY),
                  pl.BlockSpec((1, 1, f2), by_expert),
                  pl.BlockSpec(memory_space=pl.ANY),
                  pl.BlockSpec((1, 1, d), by_expert)],
        out_specs=pl.BlockSpec((EXPERT_BLOCK, d), lambda b, be, bn, rl, lo, hi, nu: (b, 0)),
        scratch_shapes=[pltpu.VMEM((EXPERT_BLOCK, d), F32), pltpu.VMEM((EXPERT_BLOCK, 1), F32),
                        pltpu.VMEM((d, f2), BF16), pltpu.VMEM((f2 // 2, d), BF16),
                        pltpu.VMEM((d, f2), F32), pltpu.VMEM((f2 // 2, d), F32),
                        pltpu.SemaphoreType.DMA((2,))],
    )
    return pl.pallas_call(
        _expert_kernel,
        grid_spec=grid_spec,
        out_shape=jax.ShapeDtypeStruct((n_blocks * EXPERT_BLOCK, d), BF16),
        compiler_params=_params("arbitrary"),
        name="expert_blocks",
    )(block_expert, next_expert, rank_lo, sub_lo, sub_hi, n_used, h2, ti_t, wt_t, w_gate_up, b_gate_up.reshape(e, 1, f2), w_down,
      b_down.reshape(e, 1, d))


def _combine_kernel(coff_ref, cgran_ref, cexp_ref, cbase_ref, yb_hbm, ti_ref, x2_ref, gf_ref, g_ref, o_ref,
                    slab, sem, acc):
    c = pl.program_id(0)
    start = coff_ref[c]
    count = coff_ref[c + 1] - start
    n_groups = (count + COMBINE_SLOTS - 1) // COMBINE_SLOTS

    def item(g, s, first=start, n=count):
        return first + jnp.minimum(g * COMBINE_SLOTS + s, n - 1)

    def copies(g, buf, first=start, n=count):
        rows = [pl.multiple_of(cgran_ref[item(g, s, first, n)] * COMBINE_GRANULE, COMBINE_GRANULE)
                for s in range(COMBINE_SLOTS)]
        return [pltpu.make_async_copy(yb_hbm.at[pl.ds(r, COMBINE_GRANULE), :],
                                      slab.at[buf, pl.ds(s * COMBINE_GRANULE, COMBINE_GRANULE), :], sem.at[buf])
                for s, r in enumerate(rows)]

    @pl.when(c == 0)
    def _():
        for cp in copies(0, 0):
            cp.start()

    acc[...] = jnp.zeros_like(acc)
    chunk = x2_ref.shape[0]
    row = lax.broadcasted_iota(jnp.int32, (COMBINE_GRANULE, chunk), 0)
    ids = ti_ref[0:TOP_K, :]
    ranks = ti_ref[TOP_K:2 * TOP_K, :]

    def group(g, carry):
        buf = g % 2

        @pl.when(g + 1 < n_groups)
        def _():
            for cp in copies(g + 1, 1 - buf):
                cp.start()

        for cp in copies(g, buf):
            cp.wait()
        pieces = []
        for s in range(COMBINE_SLOTS):
            live = g * COMBINE_SLOTS + s < count
            local = ranks - cbase_ref[item(g, s)]
            inside = (ids == jnp.where(live, cexp_ref[item(g, s)], -1)) & (local >= 0) & (local < COMBINE_GRANULE)
            hit_row = jnp.sum(jnp.where(inside, local + 1, 0), axis=0, keepdims=True) - 1
            pieces.append(jnp.where(row == hit_row, 1.0, 0.0).astype(BF16))
        onehot_t = jnp.concatenate(pieces, axis=0)
        acc[...] += lax.dot_general(onehot_t, slab[buf], (((0,), (0,)), ((), ())),
                                    preferred_element_type=F32)
        return carry

    lax.fori_loop(0, n_groups, group, 0)

    @pl.when(c + 1 < pl.num_programs(0))
    def _():
        nxt = coff_ref[c + 1]
        for cp in copies(0, 0, nxt, coff_ref[c + 2] - nxt):
            cp.start()

    y = acc[...]
    yn = y * lax.rsqrt(jnp.mean(y * y, axis=-1, keepdims=True) + EPS) * g_ref[...]
    o_ref[...] = x2_ref[...] + gf_ref[0] * yn


def combine_residual(yb, ti_t, x2, chunk_off, chunk_items, mod, mod_row_of_chunk, g_post_f):
    n, d = x2.shape
    chunk_row = lambda c, off, gran, exp, base: (c, 0)
    grid_spec = pltpu.PrefetchScalarGridSpec(
        num_scalar_prefetch=4,
        grid=(n // MOE_CHUNK,),
        in_specs=[pl.BlockSpec(memory_space=pl.ANY),
                  pl.BlockSpec((SUBLANES, MOE_CHUNK), lambda c, off, gran, exp, base: (0, c)),
                  pl.BlockSpec((MOE_CHUNK, d), chunk_row),
                  pl.BlockSpec((1, 1, d), lambda c, off, gran, exp, base: (mod_row_of_chunk(c), 0, 5)),
                  pl.BlockSpec((1, d), lambda c, off, gran, exp, base: (0, 0))],
        out_specs=pl.BlockSpec((MOE_CHUNK, d), chunk_row),
        scratch_shapes=[pltpu.VMEM((2, COMBINE_SLOTS * COMBINE_GRANULE, d), BF16),
                        pltpu.SemaphoreType.DMA((2,)),
                        pltpu.VMEM((MOE_CHUNK, d), F32)],
    )
    return pl.pallas_call(
        _combine_kernel,
        grid_spec=grid_spec,
        out_shape=jax.ShapeDtypeStruct((n, d), F32),
        compiler_params=_params("arbitrary"),
        name="combine_residual",
    )(chunk_off, *chunk_items, yb, ti_t, x2, mod, g_post_f.reshape(1, d))


def _grid_pos_embedding(rows, d):
    quarter = d // 4
    omega = POS_BASE ** (-jnp.arange(quarter, dtype=F32) / quarter)

    def emb(p):
        ang = p[:, None] * omega[None, :]
        return jnp.concatenate([jnp.sin(ang), jnp.cos(ang)], axis=-1)

    row_emb = jnp.repeat(emb(jnp.arange(rows, dtype=F32)), GRID_W, axis=0)
    col_emb = jnp.tile(emb(jnp.arange(GRID_W, dtype=F32)), (rows, 1))
    return jnp.concatenate([row_emb, col_emb], axis=-1)


def _count_le(sorted_vals, queries):
    return jnp.sum(sorted_vals <= queries[:, None], axis=1).astype(jnp.int32)


def _moe_plan(cnt_after, n_tok, tm):
    assert tm == GATHER_SUB and n_tok % MOE_CHUNK == 0 and MOE_CHUNK % tm == 0
    sub_after = cnt_after[:, 0, :N_EXPERTS].astype(jnp.int32)
    counts = sub_after[-1]
    padded = (counts + EXPERT_BLOCK - 1) // EXPERT_BLOCK * EXPERT_BLOCK
    padded_end = jnp.cumsum(padded)
    padded_start = padded_end - padded
    n_blocks = n_tok * TOP_K // EXPERT_BLOCK + N_EXPERTS
    blocks = jnp.arange(n_blocks, dtype=jnp.int32)
    block_expert = jnp.minimum(_count_le(padded_end[None, :], blocks * EXPERT_BLOCK), N_EXPERTS - 1)
    n_used = (padded_end[-1:] // EXPERT_BLOCK).astype(jnp.int32)
    of_block = (block_expert[:, None] == jnp.arange(N_EXPERTS, dtype=jnp.int32)[None, :]).astype(jnp.int32)
    r_lo = (blocks - jnp.sum(of_block * padded_start[None, :], axis=1) // EXPERT_BLOCK) * EXPERT_BLOCK
    r_hi = jnp.minimum(jnp.sum(of_block * counts[None, :], axis=1), r_lo + EXPERT_BLOCK) - 1
    through = jnp.sum(of_block[:, :, None] * sub_after.T[None, :, :], axis=1)
    after = jnp.sum(of_block * padded_end[None, :], axis=1) // EXPERT_BLOCK
    next_expert = jnp.sum((after[:, None] == blocks[None, :]).astype(jnp.int32) * block_expert[None, :], axis=1)
    next_expert = jnp.where(after < n_used[0], next_expert, -1).astype(jnp.int32)
    n_sub = sub_after.shape[0]
    sub_lo = jnp.minimum(_count_le(through, r_lo), n_sub - 1)
    sub_hi = jnp.minimum(_count_le(through, r_hi), n_sub - 1)
    nch = n_tok // MOE_CHUNK
    per_chunk = MOE_CHUNK // tm
    cb_after = sub_after[per_chunk - 1::per_chunk]
    cb_before = jnp.concatenate([jnp.zeros((1, N_EXPERTS), jnp.int32), cb_after[:-1]], axis=0)
    b_lo = ((padded_start[None, :] + cb_before) // COMBINE_GRANULE).reshape(-1)
    b_hi = ((padded_start[None, :] + cb_after - 1) // COMBINE_GRANULE).reshape(-1)
    n_it = jnp.where((cb_after > cb_before).reshape(-1), b_hi - b_lo + 1, 0)
    off_end = jnp.cumsum(n_it)
    off = off_end - n_it
    n_items = n_blocks * (EXPERT_BLOCK // COMBINE_GRANULE) + N_EXPERTS * (nch - 1)
    w = jnp.minimum(jnp.arange(n_items, dtype=jnp.int32), off_end[-1] - 1)
    chunk_end = off_end[N_EXPERTS - 1::N_EXPERTS]
    chunk_off = jnp.concatenate([jnp.zeros((1,), jnp.int32), chunk_end]).astype(jnp.int32)
    chunk_of = jnp.minimum(_count_le(chunk_end[None, :], w), nch - 1)
    chunk_expert = jnp.minimum(_count_le(off_end.reshape(nch, N_EXPERTS)[chunk_of], w), N_EXPERTS - 1)
    cell = chunk_of * N_EXPERTS + chunk_expert
    chunk_granules = (b_lo[cell] + w - off[cell]).astype(jnp.int32)
    chunk_base = chunk_granules * COMBINE_GRANULE - padded_start[chunk_expert]
    return block_expert, next_expert, r_lo, sub_lo, sub_hi, n_used, chunk_off, (chunk_granules, chunk_expert, chunk_base)


def kernel(x, c, ctx, c_ctx, w_mod, b_mod, g_pre_mix, g_post_mix, g_pre_ffn, g_post_ffn, w_in, conv_kv,
           conv_q, a_log, dt_bias, gdn_norm_g, w_proj_a, conf_dw, conf_dw_b, conf_ln_g, conf_ln_b,
           w_proj_b, w_out, w_router, b_router, w_gate_up, b_gate_up, w_down, b_down):
    batch, seq, d = x.shape
    ctx_len = ctx.shape[1]
    n_tok = batch * seq
    gw = HEADS * DV
    beta_off = HEADS * DK + gw
    state_cols = beta_off + 4 * HEADS
    q_off = state_cols

    mod_rows = 2 * SUBLANES
    cond = jnp.zeros((mod_rows, d), F32).at[:batch].set(c).at[batch].set(c_ctx)
    mod = modulation(cond, w_mod[0], b_mod[0]).reshape(mod_rows, 1, 6 * d)

    w_t = jnp.swapaxes(w_in[0], 0, 1)
    w_main = projection_weights(w_t, keep=beta_off, skip=state_cols - beta_off)
    w_ba = jnp.zeros((LANES, d), F32).at[:4 * HEADS].set(w_t[beta_off:state_cols]).astype(BF16)

    pos = _grid_pos_embedding(seq // GRID_W, d)
    x_flat = x.reshape(n_tok, d)
    tm = 1024
    tiles_per_seq = seq // tm
    proj, ba = input_projection(x_flat, pos, mod, lambda i: i // tiles_per_seq, g_pre_mix[0],
                                w_main, w_ba, tm=tm, tn=4096)
    proj_ctx, ba_ctx = input_projection(ctx.reshape(batch * ctx_len, d), None, mod, lambda i: batch,
                                        g_pre_mix[0], w_main[:beta_off], w_ba, tm=ctx_len, tn=1024)

    og = gated_deltanet(proj, ba, proj_ctx, ba_ctx, conv_kv[0], conv_q[0], a_log[0], dt_bias[0],
                        gdn_norm_g[0], batch=batch, seq=seq, ctx=ctx_len)
    uc = conformer_conv(proj, conf_dw[0], conf_dw_b[0], batch=batch, seq=seq)

    tm2 = 512
    w_r = jnp.zeros((d, LANES), F32).at[:, :N_EXPERTS].set(w_router[0]).astype(BF16)
    b_r = jnp.zeros((1, LANES), F32).at[0, :N_EXPERTS].set(b_router[0])
    x2, h2, top_i, top_w, cnt_after = mixer_output(
        og, uc, proj, x_flat, pos, mod, lambda i: i // (seq // tm2), g_post_mix[0], g_pre_ffn[0],
        conf_ln_g[0], conf_ln_b[0], w_proj_a[0].astype(BF16), w_proj_b[0].astype(BF16),
        w_out[0].astype(BF16), w_r, b_r, tm=tm2, sub=GATHER_SUB)

    block_expert, next_expert, rank_lo, sub_lo, sub_hi, n_used, chunk_off, chunk_items = _moe_plan(
        cnt_after, n_tok, GATHER_SUB)
    yb = expert_blocks(h2, top_i, top_w, block_expert, next_expert, rank_lo, sub_lo, sub_hi, n_used, w_gate_up[0],
                       b_gate_up[0], w_down[0], b_down[0])
    out = combine_residual(yb, top_i, x2, chunk_off, chunk_items, mod, lambda c: c // (seq // MOE_CHUNK),
                           g_post_ffn[0])
    return out.reshape(batch, seq, d)
```

```python
import functools

import jax
import jax.numpy as jnp
from jax import lax
from jax.experimental import pallas as pl
from jax.experimental.pallas import tpu as pltpu

F32 = jnp.float32
BF16 = jnp.bfloat16

D_MODEL = 1024
GRID_W = 64
HEADS = 8
DK = 128
DV = 128
SHORT_CONV = 5
CHUNK = 64
CONF_KERNEL = 31
N_EXPERTS = 32
TOP_K = 4
SWIGLU_LIMIT = 7.0
SWIGLU_ALPHA = 1.702
MOE_BLOCK = 128
EPS = 1e-6
POS_BASE = 10000.0

LANES = 128
SUBLANES = 8
VMEM_LIMIT = 60 * 1024 * 1024

COL_K, COL_V, COL_Q, COL_Z, COL_GLU_A, COL_GLU_G, COL_GATE_A, COL_GATE_B = range(8)


def _params(*sem):
    return pltpu.CompilerParams(dimension_semantics=sem, vmem_limit_bytes=VMEM_LIMIT)


def _mod_kernel(c_ref, w_ref, b_ref, o_ref):
    c = c_ref[...]
    s = c * jax.nn.sigmoid(c)
    o_ref[...] = jnp.dot(s, w_ref[...], preferred_element_type=F32,
                         precision=lax.Precision.HIGHEST) + b_ref[...]


def modulation(cond, w_mod, b_mod):
    r, d = cond.shape
    n = w_mod.shape[1]
    tn = 1024
    return pl.pallas_call(
        _mod_kernel,
        grid=(n // tn,),
        in_specs=[pl.BlockSpec((r, d), lambda j: (0, 0)),
                  pl.BlockSpec((d, tn), lambda j: (0, j)),
                  pl.BlockSpec((1, tn), lambda j: (0, j))],
        out_specs=pl.BlockSpec((r, tn), lambda j: (0, j)),
        out_shape=jax.ShapeDtypeStruct((r, n), F32),
        compiler_params=_params("arbitrary"),
        name="modulation",
    )(cond, w_mod, b_mod.reshape(1, n))


def _proj_weights_kernel(a_ref, b_ref, o_ref, *, first_shifted, shift):
    @pl.when(pl.program_id(0) < first_shifted)
    def _():
        o_ref[...] = a_ref[...].astype(o_ref.dtype)

    @pl.when(pl.program_id(0) >= first_shifted)
    def _():
        o_ref[...] = jnp.concatenate([a_ref[shift:, :], b_ref[...]], axis=0).astype(o_ref.dtype)


def projection_weights(w_t, *, keep, skip, tr=1024):
    total, d = w_t.shape
    n_out = total - skip
    assert keep % tr == 0 and n_out % tr == 0 and tr % skip == 0 and skip % SUBLANES == 0
    return pl.pallas_call(
        functools.partial(_proj_weights_kernel, first_shifted=keep // tr, shift=skip),
        grid=(n_out // tr,),
        in_specs=[pl.BlockSpec((tr, d), lambda j: (j, 0)),
                  pl.BlockSpec((skip, d), lambda j: ((j + 1) * (tr // skip), 0))],
        out_specs=pl.BlockSpec((tr, d), lambda j: (j, 0)),
        out_shape=jax.ShapeDtypeStruct((n_out, d), BF16),
        compiler_params=_params("arbitrary"),
        name="projection_weights",
    )(w_t, w_t)


_NT = (((1,), (1,)), ((), ()))


def _inproj_kernel(*refs, has_pos):
    if has_pos:
        x_ref, pos_ref, sh_ref, sc_ref, g_ref, w_ref, wba_ref, o_ref, ba_ref, h_scr = refs
    else:
        x_ref, sh_ref, sc_ref, g_ref, w_ref, wba_ref, o_ref, ba_ref, h_scr = refs

    @pl.when(pl.program_id(1) == 0)
    def _():
        x = x_ref[...]
        if has_pos:
            x = x + pos_ref[...]
        y = x * lax.rsqrt(jnp.mean(x * x, axis=-1, keepdims=True) + EPS) * g_ref[...]
        h = (y * (1.0 + sc_ref[0]) + sh_ref[0]).astype(BF16)
        h_scr[...] = h
        ba_ref[...] = lax.dot_general(h, wba_ref[...], _NT, preferred_element_type=F32)

    o_ref[...] = lax.dot_general(h_scr[...], w_ref[...], _NT, preferred_element_type=F32).astype(o_ref.dtype)


def input_projection(x, pos, mod, mod_row_of_tile, g_pre, w_main, w_ba, *, tm, tn):
    n, d = x.shape
    w = w_main.shape[0]
    assert n % tm == 0 and w % tn == 0
    has_pos = pos is not None
    in_specs = [pl.BlockSpec((tm, d), lambda i, j: (i, 0))]
    args = [x]
    if has_pos:
        pos_tiles = pos.shape[0] // tm
        in_specs.append(pl.BlockSpec((tm, d), lambda i, j: (i % pos_tiles, 0)))
        args.append(pos)
    in_specs += [
        pl.BlockSpec((1, 1, d), lambda i, j: (mod_row_of_tile(i), 0, 0)),
        pl.BlockSpec((1, 1, d), lambda i, j: (mod_row_of_tile(i), 0, 1)),
        pl.BlockSpec((1, d), lambda i, j: (0, 0)),
        pl.BlockSpec((tn, d), lambda i, j: (j, 0)),
        pl.BlockSpec((LANES, d), lambda i, j: (0, 0)),
    ]
    args += [mod, mod, g_pre.reshape(1, d), w_main, w_ba]
    return pl.pallas_call(
        functools.partial(_inproj_kernel, has_pos=has_pos),
        grid=(n // tm, w // tn),
        in_specs=in_specs,
        out_specs=[pl.BlockSpec((tm, tn), lambda i, j: (i, j)),
                   pl.BlockSpec((tm, LANES), lambda i, j: (i, 0))],
        out_shape=[jax.ShapeDtypeStruct((n, w), BF16), jax.ShapeDtypeStruct((n, LANES), F32)],
        scratch_shapes=[pltpu.VMEM((tm, d), BF16)],
        compiler_params=_params("arbitrary", "arbitrary"),
        name="input_projection_pos" if has_pos else "input_projection_ctx",
    )(*args)


def _silu(x):
    return x * jax.nn.sigmoid(x)


def _softplus(x):
    return jnp.maximum(x, 0.0) + jnp.log(1.0 + jnp.exp(-jnp.abs(x)))


def _short_conv(src, taps, pad_scr, rows):
    half = SHORT_CONV // 2
    pad_scr[0:SUBLANES, :] = jnp.zeros((SUBLANES, LANES), F32)
    pad_scr[SUBLANES:SUBLANES + rows, :] = src.astype(F32)
    pad_scr[SUBLANES + rows:2 * SUBLANES + rows, :] = jnp.zeros((SUBLANES, LANES), F32)
    acc = None
    for j in range(SHORT_CONV):
        off = SUBLANES + j - half
        term = pad_scr[off:off + rows, :] * taps[j:j + 1, :]
        acc = term if acc is None else acc + term
    return _silu(acc)


def _l2n(x):
    return x * lax.rsqrt(jnp.sum(x * x, axis=-1, keepdims=True) + EPS)


def _gdn_kernel(proj_hbm, ba_hbm, kc_ref, vc_ref, bac_ref,
                wk_ref, wv_ref, wq_ref, ab_ref, gn_ref, o_ref,
                k_ref, v_ref, q_ref, z_ref, ba_ref, in_sem,
                kf, vf, qf, gates, pad_scr, lhs, val, att, ktt, dec, osc, *, seq, ctx, unroll, group):
    total = ctx + seq
    n_batch, n_groups = pl.num_programs(0), pl.num_programs(1)
    gl = group * LANES

    def block(bb, hg, base):
        return proj_hbm.at[pl.ds(pl.multiple_of(bb * seq, seq), seq),
                           pl.ds(pl.multiple_of((base * n_groups + hg) * gl, gl), gl)]

    def input_copies(bb, hg):
        return [pltpu.make_async_copy(block(bb, hg, COL_K), k_ref, in_sem.at[0]),
                pltpu.make_async_copy(block(bb, hg, COL_V), v_ref, in_sem.at[1]),
                pltpu.make_async_copy(block(bb, hg, COL_Q), q_ref, in_sem.at[2]),
                pltpu.make_async_copy(ba_hbm.at[pl.ds(pl.multiple_of(bb * seq, seq), seq), :], ba_ref, in_sem.at[3])]

    b_now, g_now = pl.program_id(0), pl.program_id(1)
    z_copy = pltpu.make_async_copy(block(b_now, g_now, COL_Z), z_ref, in_sem.at[4])

    @pl.when((b_now == 0) & (g_now == 0))
    def _():
        for cp in input_copies(b_now, g_now):
            cp.start()

    for cp in input_copies(b_now, g_now):
        cp.wait()
    z_copy.start()
    nc = total // CHUNK
    ncc = ctx // CHUNK
    two = 2 * CHUNK
    lg_chunk = CHUNK.bit_length() - 1

    ri = lax.broadcasted_iota(jnp.int32, (two, two), 0)
    ci = lax.broadcasted_iota(jnp.int32, (two, two), 1)
    same_dir = (ri >> lg_chunk) == (ci >> lg_chunk)
    incl = same_dir & (((ri < CHUNK) & (ri >= ci)) | ((ri >= CHUNK) & (ri <= ci)))
    strict = incl & (ri != ci)
    eye = jnp.where(ri == ci, 1.0, 0.0)
    tri = jnp.where(incl, 1.0, 0.0).astype(BF16)
    level_masks = []
    for lg in range(lg_chunk):
        same_parent = (ri >> (lg + 1)) == (ci >> (lg + 1))
        level_masks.append(jnp.where(same_parent & ((ri >> lg) != (ci >> lg)), 1.0, 0.0))
    top_rows = lax.broadcasted_iota(jnp.int32, (two, LANES), 0) < CHUNK

    def bwd_chunk(t):
        return jnp.where(t < ncc, ncc - 1 - t, nc + ncc - 1 - t)

    def stacked(ref, rf, rb):
        return jnp.concatenate([ref[pl.ds(rf, CHUNK), :], ref[pl.ds(rb, CHUNK), :]], axis=0)

    def chunk_load(t):
        rf = pl.multiple_of(t * CHUNK, CHUNK)
        rb = pl.multiple_of(bwd_chunk(t) * CHUNK, CHUNK)
        beta = jnp.concatenate([gates[0, pl.ds(rf, CHUNK), :], gates[1, pl.ds(rb, CHUNK), :]], axis=0)
        g = jnp.concatenate([gates[2, pl.ds(rf, CHUNK), :], gates[3, pl.ds(rb, CHUNK), :]], axis=0)
        return stacked(kf, rf, rb), stacked(vf, rf, rb), stacked(qf, rf, rb), beta, g

    def chunk_triangle(k, q, beta, g):
        kb = k.astype(BF16)
        kq = jnp.concatenate([kb, q.astype(BF16)], axis=0)
        kk_qk = lax.dot_general(kq, kb, (((1,), (1,)), ((), ())), preferred_element_type=F32)
        g_hi = g.astype(BF16)
        r1 = g - g_hi.astype(F32)
        g_mid = r1.astype(BF16)
        g_lo = (r1 - g_mid.astype(F32)).astype(BF16)
        gc3 = jnp.dot(tri, jnp.concatenate([g_hi, g_mid, g_lo], axis=1), preferred_element_type=F32)
        gc = gc3[:, :LANES] + gc3[:, LANES:2 * LANES] + gc3[:, 2 * LANES:]
        gc_row = jnp.transpose(gc)
        decay = jnp.where(incl, jnp.exp(gc - gc_row), 0.0)
        a = jnp.where(strict, beta * kk_qk[:two] * decay, 0.0)
        return a, (kk_qk[two:] * decay).astype(BF16), gc

    def chunk_finish(hh, t, k, v, q, beta, gc, attn, tmat):
        eg = jnp.exp(gc)
        rhs = jnp.concatenate([v * beta, k * beta * eg], axis=1).astype(BF16)
        sol = jnp.dot(tmat.astype(BF16), rhs, preferred_element_type=F32)
        g_end = jnp.where(top_rows, jnp.broadcast_to(gc[CHUNK - 1:CHUNK, :], (two, LANES)),
                          jnp.broadcast_to(gc[CHUNK:CHUNK + 1, :], (two, LANES)))
        k_tail = k * jnp.exp(g_end - gc)
        qg = (q * eg).astype(BF16)
        kcum = sol[:, DV:].astype(BF16)
        r0 = pl.multiple_of(t * two, two)
        lhs[hh, 0, pl.ds(r0, two), :] = jnp.concatenate([kcum[:CHUNK], qg[:CHUNK]], axis=0)
        lhs[hh, 1, pl.ds(r0, two), :] = jnp.concatenate([kcum[CHUNK:], qg[CHUNK:]], axis=0)
        val[hh, pl.ds(r0, two), :] = sol[:, :DV].astype(BF16)
        att[hh, pl.ds(r0, two), :] = attn
        ktt[hh, pl.ds(r0, two), :] = jnp.transpose(k_tail).astype(BF16)
        d0 = pl.multiple_of(t * SUBLANES, SUBLANES)
        e_end = jnp.exp(g_end)
        dec[hh, 0, pl.ds(d0, SUBLANES), :] = e_end[:SUBLANES]
        dec[hh, 1, pl.ds(d0, SUBLANES), :] = e_end[CHUNK:CHUNK + SUBLANES]

    for hh in range(group):
        head = pl.program_id(1) * group + hh
        cols = slice(hh * LANES, (hh + 1) * LANES)

        kf[0:ctx, :] = _l2n(_short_conv(kc_ref[:, cols], wk_ref[:, cols], pad_scr, ctx))
        vf[0:ctx, :] = _short_conv(vc_ref[:, cols], wv_ref[:, cols], pad_scr, ctx)
        qf[0:ctx, :] = jnp.zeros((ctx, LANES), F32)
        kf[ctx:total, :] = _l2n(_short_conv(k_ref[:, cols], wk_ref[:, cols], pad_scr, seq))
        vf[ctx:total, :] = _short_conv(v_ref[:, cols], wv_ref[:, cols], pad_scr, seq)
        qf[ctx:total, :] = _l2n(_short_conv(q_ref[:, cols], wq_ref[:, cols], pad_scr, seq)) * (DK ** -0.5)

        def gate_cols(src_ref, lo, rows):
            x = src_ref[...]
            lane = lax.broadcasted_iota(jnp.int32, (rows, LANES), 1)
            beta = jax.nn.sigmoid(x)
            g = -jnp.exp(ab_ref[0:1, :]) * _softplus(x + ab_ref[1:2, :])
            for slot, (arr, base) in enumerate(((beta, 0), (beta, HEADS), (g, 2 * HEADS), (g, 3 * HEADS))):
                col = jnp.sum(jnp.where(lane == base + head, arr, 0.0), axis=-1, keepdims=True)
                gates[slot, lo:lo + rows, :] = jnp.broadcast_to(col, (rows, LANES))

        gate_cols(bac_ref, 0, ctx)
        gate_cols(ba_ref, ctx, seq)

        def chunk_body(i, carry, hh=hh):
            steps = [i * unroll + u for u in range(unroll)]
            loaded = [chunk_load(t) for t in steps]
            tris = [chunk_triangle(k, q, beta, g) for (k, v, q, beta, g) in loaded]
            tmats = [eye - a * level_masks[0] for (a, _, _) in tris]
            for m in level_masks[1:]:
                nxt = []
                for (a, _, _), tmat in zip(tris, tmats):
                    cs = (a * m).astype(BF16)
                    tb = tmat.astype(BF16)
                    tc = jnp.dot(tb, cs, preferred_element_type=F32)
                    nxt.append(tmat - jnp.dot(tc.astype(BF16), tb, preferred_element_type=F32))
                tmats = nxt
            for t, (k, v, q, beta, g), (a, attn, gc), tmat in zip(steps, loaded, tris, tmats):
                chunk_finish(hh, t, k, v, q, beta, gc, attn, tmat)
            return carry

        lax.fori_loop(0, nc // unroll, chunk_body, 0)

    wrap = g_now + 1 == n_groups

    @pl.when(jnp.logical_not(wrap & (b_now + 1 == n_batch)))
    def _():
        for cp in input_copies(jnp.where(wrap, b_now + 1, b_now), jnp.where(wrap, 0, g_now + 1)):
            cp.start()

    def scan_body(t, carry):
        rf = pl.multiple_of(t * CHUNK, CHUNK)
        rb = pl.multiple_of(bwd_chunk(t) * CHUNK, CHUNK)
        r0 = pl.multiple_of(t * two, two)
        d0 = pl.multiple_of(t * SUBLANES, SUBLANES)
        rs = [(jnp.dot(lhs[hh, 0, pl.ds(r0, two), :], carry[2 * hh].astype(BF16), preferred_element_type=F32),
               jnp.dot(lhs[hh, 1, pl.ds(r0, two), :], carry[2 * hh + 1].astype(BF16), preferred_element_type=F32))
              for hh in range(group)]
        v_news = [val[hh, pl.ds(r0, two), :].astype(F32) - jnp.concatenate([r_f[:CHUNK], r_b[:CHUNK]], axis=0)
                  for hh, (r_f, r_b) in enumerate(rs)]
        outs, new_states = [], []
        for hh, ((r_f, r_b), v_new) in enumerate(zip(rs, v_news)):
            zeros = jnp.zeros_like(v_new)
            v_bd = jnp.concatenate([jnp.where(top_rows, v_new, zeros), jnp.where(top_rows, zeros, v_new)],
                                   axis=1).astype(BF16)
            upd = jnp.dot(ktt[hh, pl.ds(r0, two), :], v_bd, preferred_element_type=F32)
            new_states.append(carry[2 * hh] * dec[hh, 0, pl.ds(d0, SUBLANES), :][0:1, :] + upd[:, :DV])
            new_states.append(carry[2 * hh + 1] * dec[hh, 1, pl.ds(d0, SUBLANES), :][0:1, :] + upd[:, DV:])
            outs.append(jnp.concatenate([r_f[CHUNK:], r_b[CHUNK:]], axis=0)
                        + jnp.dot(att[hh, pl.ds(r0, two), :], v_new.astype(BF16), preferred_element_type=F32))
        for hh, o in enumerate(outs):
            osc[hh, pl.ds(rf, CHUNK), :] += o[:CHUNK]
            osc[hh, pl.ds(rb, CHUNK), :] += o[CHUNK:]
        return tuple(new_states)

    osc[...] = jnp.zeros_like(osc)
    zero = jnp.zeros((DK, DV), F32)
    lax.fori_loop(0, nc, scan_body, (zero,) * (2 * group))

    z_copy.wait()
    for hh in range(group):
        cols = slice(hh * LANES, (hh + 1) * LANES)
        o = osc[hh, ctx:total, :]
        o = o * lax.rsqrt(jnp.mean(o * o, axis=-1, keepdims=True) + EPS) * gn_ref[...]
        o_ref[:, cols] = (o * _silu(z_ref[:, cols].astype(F32))).astype(o_ref.dtype)


def gated_deltanet(proj, ba, proj_ctx, ba_ctx, conv_kv, conv_q, a_log, dt_bias, gn, *, batch, seq, ctx,
                   unroll=18, group=4):
    total = seq + ctx
    nc = total // CHUNK
    assert seq % CHUNK == 0 and ctx % CHUNK == 0 and nc % unroll == 0 and HEADS % group == 0
    gl = group * LANES
    ng = HEADS // group
    taps = jnp.zeros((SUBLANES, conv_kv.shape[1]), F32).at[:SHORT_CONV].set(conv_kv)
    taps_q = jnp.zeros((SUBLANES, conv_q.shape[1]), F32).at[:SHORT_CONV].set(conv_q)
    ab = jnp.zeros((SUBLANES, LANES), F32)
    ab = ab.at[0, 2 * HEADS:4 * HEADS].set(a_log.reshape(-1)).at[1, 2 * HEADS:4 * HEADS].set(dt_bias.reshape(-1))
    col = lambda base: (lambda b, h: (b, base * ng + h))
    return pl.pallas_call(
        functools.partial(_gdn_kernel, seq=seq, ctx=ctx, unroll=unroll, group=group),
        grid=(batch, ng),
        in_specs=[
            pl.BlockSpec(memory_space=pl.ANY),
            pl.BlockSpec(memory_space=pl.ANY),
            pl.BlockSpec((ctx, gl), col(COL_K)),
            pl.BlockSpec((ctx, gl), col(COL_V)),
            pl.BlockSpec((ctx, LANES), lambda b, h: (b, 0)),
            pl.BlockSpec((SUBLANES, gl), lambda b, h: (0, h)),
            pl.BlockSpec((SUBLANES, gl), lambda b, h: (0, ng + h)),
            pl.BlockSpec((SUBLANES, gl), lambda b, h: (0, h)),
            pl.BlockSpec((SUBLANES, LANES), lambda b, h: (0, 0)),
            pl.BlockSpec((1, LANES), lambda b, h: (0, 0)),
        ],
        out_specs=pl.BlockSpec((seq, gl), lambda b, h: (b, h)),
        out_shape=jax.ShapeDtypeStruct((batch * seq, HEADS * DV), BF16),
        scratch_shapes=[
            pltpu.VMEM((seq, gl), BF16),
            pltpu.VMEM((seq, gl), BF16),
            pltpu.VMEM((seq, gl), BF16),
            pltpu.VMEM((seq, gl), BF16),
            pltpu.VMEM((seq, LANES), F32),
            pltpu.SemaphoreType.DMA((5,)),
            pltpu.VMEM((total, LANES), F32),
            pltpu.VMEM((total, LANES), F32),
            pltpu.VMEM((total, LANES), F32),
            pltpu.VMEM((4, total, LANES), F32),
            pltpu.VMEM((seq + 2 * SUBLANES, LANES), F32),
            pltpu.VMEM((group, 2, nc * 2 * CHUNK, LANES), BF16),
            pltpu.VMEM((group, nc * 2 * CHUNK, LANES), BF16),
            pltpu.VMEM((group, nc * 2 * CHUNK, LANES), BF16),
            pltpu.VMEM((group, nc * 2 * CHUNK, LANES), BF16),
            pltpu.VMEM((group, 2, nc * SUBLANES, LANES), F32),
            pltpu.VMEM((group, total, LANES), F32),
        ],
        compiler_params=_params("arbitrary", "arbitrary"),
        name="gated_deltanet",
    )(proj, ba, proj_ctx, proj_ctx, ba_ctx, taps, taps, taps_q, ab, gn.reshape(1, DV))


CONF_ROWS = 512
CONF_HALO = 16


def _conf_kernel(a_ref, g_ref, w_ref, b_ref, o_ref, pad_scr, *, seq):
    cb = a_ref.shape[1]
    u = a_ref[...].astype(F32) * jax.nn.sigmoid(g_ref[...].astype(F32))
    pad_scr[0:CONF_HALO, :] = jnp.zeros((CONF_HALO, cb), F32)
    pad_scr[CONF_HALO:CONF_HALO + seq, :] = u
    pad_scr[CONF_HALO + seq:2 * CONF_HALO + seq, :] = jnp.zeros((CONF_HALO, cb), F32)
    win = CONF_ROWS + 2 * CONF_HALO
    first = CONF_HALO - CONF_KERNEL // 2

    def body(i, carry):
        r0 = pl.multiple_of(i * CONF_ROWS, CONF_ROWS)
        w = pad_scr[pl.ds(r0, win), :]
        acc = jnp.zeros((CONF_ROWS, cb), F32) + b_ref[...]
        for sub in range(SUBLANES):
            shifted = w if sub == 0 else pltpu.roll(w, win - sub, axis=0)
            for j in range(CONF_KERNEL):
                off = first + j
                if off % SUBLANES == sub:
                    base = off - sub
                    acc = acc + shifted[base:base + CONF_ROWS, :] * w_ref[j:j + 1, :]
        o_ref[pl.ds(r0, CONF_ROWS), :] = acc.astype(o_ref.dtype)
        return carry

    lax.fori_loop(0, seq // CONF_ROWS, body, 0)


def conformer_conv(proj, conf_dw, conf_dw_b, *, batch, seq, cb=128):
    c = conf_dw.shape[1]
    assert seq % CONF_ROWS == 0 and c % cb == 0
    nb = c // cb
    taps = jnp.zeros((32, c), F32).at[:CONF_KERNEL].set(conf_dw)
    return pl.pallas_call(
        functools.partial(_conf_kernel, seq=seq),
        grid=(batch, nb),
        in_specs=[pl.BlockSpec((seq, cb), lambda b, j: (b, COL_GLU_A * nb + j)),
                  pl.BlockSpec((seq, cb), lambda b, j: (b, COL_GLU_G * nb + j)),
                  pl.BlockSpec((32, cb), lambda b, j: (0, j)),
                  pl.BlockSpec((1, cb), lambda b, j: (0, j))],
        out_specs=pl.BlockSpec((seq, cb), lambda b, j: (b, j)),
        out_shape=jax.ShapeDtypeStruct((batch * seq, c), BF16),
        scratch_shapes=[pltpu.VMEM((seq + 2 * CONF_HALO, cb), F32)],
        compiler_params=_params("arbitrary", "arbitrary"),
        name="conformer_conv",
    )(proj, proj, taps, conf_dw_b.reshape(1, c))


def _mixer_out_kernel(og_ref, uc_ref, ga_ref, gb_ref, x_ref, pos_ref, gm_ref, shf_ref, scf_ref,
                      gpost_ref, gpre_ref, lng_ref, lnb_ref, wa_ref, wb_ref, wo_ref, wr_ref, br_ref,
                      x2_ref, h2_ref, ti_ref, tw_ref, cnt_ref, cnt_scr, *, sub):
    parts = [slice(r, r + sub) for r in range(0, og_ref.shape[0], sub)]

    def rms(v):
        return v * lax.rsqrt(jnp.mean(v * v, axis=-1, keepdims=True) + EPS)

    def layer_norm_silu(uc):
        mu = jnp.mean(uc, axis=-1, keepdims=True)
        var = jnp.mean(jnp.square(uc - mu), axis=-1, keepdims=True)
        return _silu((uc - mu) * lax.rsqrt(var + EPS) * lng_ref[...] + lnb_ref[...]).astype(BF16)

    us = [layer_norm_silu(uc_ref[p, :].astype(F32)) for p in parts]
    yas = [jnp.dot(og_ref[p, :], wa_ref[...], preferred_element_type=F32) for p in parts]
    ybs = [jnp.dot(u, wb_ref[...], preferred_element_type=F32) for u in us]
    merged = [(jax.nn.sigmoid(ga_ref[p, :].astype(F32)) * ya
               + jax.nn.sigmoid(gb_ref[p, :].astype(F32)) * yb).astype(BF16) for p, ya, yb in zip(parts, yas, ybs)]
    ys = [jnp.dot(m, wo_ref[...], preferred_element_type=F32) for m in merged]
    x2s = [x_ref[p, :] + pos_ref[p, :] + gm_ref[0] * (rms(y) * gpost_ref[...]) for p, y in zip(parts, ys)]
    h2s = [(rms(x2) * gpre_ref[...] * (1.0 + scf_ref[0]) + shf_ref[0]).astype(BF16) for x2 in x2s]
    for p, x2, h2 in zip(parts, x2s, h2s):
        x2_ref[p, :] = x2
        h2_ref[p, :] = h2
    all_logits = [jnp.dot(h2, wr_ref[...], preferred_element_type=F32) + br_ref[...] for h2 in h2s]
    lane = lax.broadcasted_iota(jnp.int32, (sub, LANES), 1).astype(F32)
    ri = lax.broadcasted_iota(jnp.int32, (sub, sub), 0)
    ci = lax.broadcasted_iota(jnp.int32, (sub, sub), 1)
    before = jnp.where(ri > ci, 1.0, 0.0).astype(BF16)

    @pl.when(pl.program_id(0) == 0)
    def _():
        cnt_scr[...] = jnp.zeros_like(cnt_scr)

    routed = []
    for logits in all_logits:
        live = jnp.where(lane < N_EXPERTS, logits, -jnp.inf)
        top_v, top_i = [], []
        for _ in range(TOP_K):
            m = jnp.max(live, axis=-1, keepdims=True)
            idx = jnp.min(jnp.where(live == m, lane, float(LANES)), axis=-1, keepdims=True)
            top_v.append(m)
            top_i.append(idx)
            live = jnp.where(lane == idx, -jnp.inf, live)
        ex = [jnp.exp(v - top_v[0]) for v in top_v]
        denom = ex[0] + ex[1] + ex[2] + ex[3]
        picked = jnp.zeros((sub, LANES), F32)
        for k in range(TOP_K):
            picked = jnp.where(lane == top_i[k], 1.0, picked)
        routed.append((top_i, [e / denom for e in ex], picked,
                       jnp.dot(before, picked.astype(BF16), preferred_element_type=F32)))

    count = cnt_scr[...]
    for j, (p, (top_i, top_w, picked, within)) in enumerate(zip(parts, routed)):
        prefix = within + count[0:1, :]
        ti = jnp.zeros((sub, LANES), F32)
        tw = jnp.zeros((sub, LANES), F32)
        for k in range(TOP_K):
            rank = jnp.sum(jnp.where(lane == top_i[k], prefix, 0.0), axis=-1, keepdims=True)
            ti = jnp.where(lane == k, top_i[k], ti)
            ti = jnp.where(lane == TOP_K + k, rank, ti)
            tw = jnp.where(lane == k, top_w[k], tw)
        ti_ref[:, p] = jnp.transpose(ti)[:SUBLANES, :].astype(jnp.int32)
        tw_ref[:, p] = jnp.transpose(tw)[:SUBLANES, :]
        count = count + jnp.sum(picked, axis=0, keepdims=True)
        cnt_ref[j] = count
    cnt_scr[...] = count


def mixer_output(og, uc, proj, x, pos, mod, mod_row_of_tile, g_post, g_pre_f, ln_g, ln_b,
                 w_a, w_b, w_o, w_r, b_r, *, tm, sub):
    n, d = x.shape
    assert n % tm == 0 and pos.shape[0] % tm == 0 and tm % sub == 0
    pos_tiles = pos.shape[0] // tm
    row = lambda i: (i, 0)
    fixed = lambda i: (0, 0)
    modspec = lambda k: pl.BlockSpec((1, 1, d), lambda i: (mod_row_of_tile(i), 0, k))
    vec = pl.BlockSpec((1, d), fixed)
    mat = pl.BlockSpec((d, d), fixed)
    return pl.pallas_call(
        functools.partial(_mixer_out_kernel, sub=sub),
        grid=(n // tm,),
        in_specs=[pl.BlockSpec((tm, d), row), pl.BlockSpec((tm, d), row),
                  pl.BlockSpec((tm, d), lambda i: (i, COL_GATE_A)),
                  pl.BlockSpec((tm, d), lambda i: (i, COL_GATE_B)),
                  pl.BlockSpec((tm, d), row),
                  pl.BlockSpec((tm, d), lambda i: (i % pos_tiles, 0)),
                  modspec(2), modspec(3), modspec(4),
                  vec, vec, vec, vec, mat, mat, mat,
                  pl.BlockSpec((d, LANES), fixed), pl.BlockSpec((1, LANES), fixed)],
        out_specs=[pl.BlockSpec((tm, d), row), pl.BlockSpec((tm, d), row),
                   pl.BlockSpec((SUBLANES, tm), lambda i: (0, i)), pl.BlockSpec((SUBLANES, tm), lambda i: (0, i)),
                   pl.BlockSpec((tm // sub, SUBLANES, LANES), lambda i: (i, 0, 0))],
        out_shape=[jax.ShapeDtypeStruct((n, d), F32), jax.ShapeDtypeStruct((n, d), BF16),
                   jax.ShapeDtypeStruct((SUBLANES, n), jnp.int32), jax.ShapeDtypeStruct((SUBLANES, n), F32),
                   jax.ShapeDtypeStruct((n // sub, SUBLANES, LANES), F32)],
        scratch_shapes=[pltpu.VMEM((SUBLANES, LANES), F32)],
        compiler_params=_params("arbitrary"),
        name="mixer_output",
    )(og, uc, proj, proj, x, pos, mod, mod, mod,
      g_post.reshape(1, d), g_pre_f.reshape(1, d), ln_g.reshape(1, d), ln_b.reshape(1, d),
      w_a, w_b, w_o, w_r, b_r)


EXPERT_BLOCK = 256
MOE_CHUNK = 512
GATHER_SUB = 256
GATHER_GROUP = 3
COMBINE_GRANULE = 32
COMBINE_SLOTS = 32


def _expert_kernel(bexp_ref, bnext_ref, rlo_ref, slo_ref, shi_ref, nused_ref,
                   h2_ref, ti_ref, wt_ref, wgu_hbm, bgu_ref, wd_hbm, bd_ref, o_ref,
                   acc, rw_acc, wgu_bf, wd_bf, wgu_f32, wd_f32, wsem):
    b = pl.program_id(0)
    expert = bexp_ref[b]

    def fetch(e):
        return (pltpu.make_async_copy(wgu_hbm.at[e], wgu_f32, wsem.at[0]),
                pltpu.make_async_copy(wd_hbm.at[e], wd_f32, wsem.at[1]))

    @pl.when(((b == 0) | (expert != bexp_ref[jnp.maximum(b - 1, 0)])) & (b < nused_ref[0]))
    def _():
        @pl.when(b == 0)
        def _():
            for cp in fetch(expert):
                cp.start()

        for cp in fetch(expert):
            cp.wait()
        wgu_bf[...] = wgu_f32[...].astype(BF16)
        wd_bf[...] = wd_f32[...].astype(BF16)

        @pl.when(bnext_ref[b] >= 0)
        def _():
            for cp in fetch(bnext_ref[b]):
                cp.start()

    @pl.when(b < nused_ref[0])
    def _():
        acc[...] = jnp.zeros_like(acc)
        rw_acc[...] = jnp.zeros_like(rw_acc)
        row = lax.broadcasted_iota(jnp.int32, (EXPERT_BLOCK, GATHER_SUB), 0)
        s_lo, s_hi = slo_ref[b], shi_ref[b]
        last_sub = h2_ref.shape[0] // GATHER_SUB - 1

        def select(s, live):
            t0 = pl.multiple_of(s * GATHER_SUB, GATHER_SUB)
            ids = ti_ref[0:TOP_K, pl.ds(t0, GATHER_SUB)]
            local = ti_ref[TOP_K:2 * TOP_K, pl.ds(t0, GATHER_SUB)] - rlo_ref[b]
            if live is not True:
                local = jnp.where(live, local, -1)
            inside = (ids == expert) & (local >= 0) & (local < EXPERT_BLOCK)
            hit_row = jnp.sum(jnp.where(inside, local + 1, 0), axis=0, keepdims=True) - 1
            hit_w = jnp.sum(jnp.where(inside, wt_ref[0:TOP_K, pl.ds(t0, GATHER_SUB)], 0.0), axis=0, keepdims=True)
            hit = row == hit_row
            return (jnp.where(hit, 1.0, 0.0).astype(BF16), jnp.where(hit, hit_w, 0.0),
                    h2_ref[pl.ds(t0, GATHER_SUB), :])

        def gather(i, carry):
            s0 = s_lo + GATHER_GROUP * i
            parts = [select(s0, True)] + [select(jnp.minimum(s0 + q, last_sub), s0 + q <= s_hi)
                                          for q in range(1, GATHER_GROUP)]
            acc[...] += jnp.dot(jnp.concatenate([p for p, _, _ in parts], axis=1),
                                jnp.concatenate([x for _, _, x in parts], axis=0), preferred_element_type=F32)
            rw_acc[...] += jnp.sum(functools.reduce(lambda a, c: a + c, [w for _, w, _ in parts]), axis=1,
                                   keepdims=True)
            return carry

        lax.fori_loop(0, (s_hi - s_lo + GATHER_GROUP) // GATHER_GROUP, gather, 0)
        gu = jnp.dot(acc[...].astype(BF16), wgu_bf[...], preferred_element_type=F32) + bgu_ref[0]
        dff = gu.shape[1] // 2
        gl = jnp.minimum(gu[:, :dff], SWIGLU_LIMIT)
        lin = jnp.clip(gu[:, dff:], -SWIGLU_LIMIT, SWIGLU_LIMIT)
        act = (gl * jax.nn.sigmoid(SWIGLU_ALPHA * gl) * (lin + 1.0)).astype(BF16)
        y = jnp.dot(act, wd_bf[...], preferred_element_type=F32) + bd_ref[0]
        o_ref[...] = (y * rw_acc[...]).astype(o_ref.dtype)

    @pl.when(b >= nused_ref[0])
    def _():
        o_ref[...] = jnp.zeros_like(o_ref)


def expert_blocks(h2, ti_t, wt_t, block_expert, next_expert, rank_lo, sub_lo, sub_hi, n_used, w_gate_up, b_gate_up,
                  w_down, b_down):
    n, d = h2.shape
    e, _, f2 = w_gate_up.shape
    n_blocks = block_expert.shape[0]
    whole = lambda b, be, bn, rl, lo, hi, nu: (0, 0)
    by_expert = lambda b, be, bn, rl, lo, hi, nu: (be[b], 0, 0)
    once = pl.Buffered(1)
    grid_spec = pltpu.PrefetchScalarGridSpec(
        num_scalar_prefetch=6,
        grid=(n_blocks,),
        in_specs=[pl.BlockSpec((n, d), whole, pipeline_mode=once),
                  pl.BlockSpec((SUBLANES, n), whole, pipeline_mode=once),
                  pl.BlockSpec((SUBLANES, n), whole, pipeline_mode=once),
                  pl.BlockSpec(memory_space=pl.ANY),
                  pl.BlockSpec((1, 1, f2), by_expert),
                  pl.BlockSpec(memory_space=pl.ANY),
                  pl.BlockSpec((1, 1, d), by_expert)],
        out_specs=pl.BlockSpec((EXPERT_BLOCK, d), lambda b, be, bn, rl, lo, hi, nu: (b, 0)),
        scratch_shapes=[pltpu.VMEM((EXPERT_BLOCK, d), F32), pltpu.VMEM((EXPERT_BLOCK, 1), F32),
                        pltpu.VMEM((d, f2), BF16), pltpu.VMEM((f2 // 2, d), BF16),
                        pltpu.VMEM((d, f2), F32), pltpu.VMEM((f2 // 2, d), F32),
                        pltpu.SemaphoreType.DMA((2,))],
    )
    return pl.pallas_call(
        _expert_kernel,
        grid_spec=grid_spec,
        out_shape=jax.ShapeDtypeStruct((n_blocks * EXPERT_BLOCK, d), BF16),
        compiler_params=_params("arbitrary"),
        name="expert_blocks",
    )(block_expert, next_expert, rank_lo, sub_lo, sub_hi, n_used, h2, ti_t, wt_t, w_gate_up, b_gate_up.reshape(e, 1, f2), w_down,
      b_down.reshape(e, 1, d))


def _combine_kernel(coff_ref, cgran_ref, cexp_ref, cbase_ref, yb_hbm, ti_ref, x2_ref, gf_ref, g_ref, o_ref,
                    slab, sem, acc):
    c = pl.program_id(0)
    start = coff_ref[c]
    count = coff_ref[c + 1] - start
    n_groups = (count + COMBINE_SLOTS - 1) // COMBINE_SLOTS

    def item(g, s, first=start, n=count):
        return first + jnp.minimum(g * COMBINE_SLOTS + s, n - 1)

    def copies(g, buf, first=start, n=count):
        rows = [pl.multiple_of(cgran_ref[item(g, s, first, n)] * COMBINE_GRANULE, COMBINE_GRANULE)
                for s in range(COMBINE_SLOTS)]
        return [pltpu.make_async_copy(yb_hbm.at[pl.ds(r, COMBINE_GRANULE), :],
                                      slab.at[buf, pl.ds(s * COMBINE_GRANULE, COMBINE_GRANULE), :], sem.at[buf])
                for s, r in enumerate(rows)]

    @pl.when(c == 0)
    def _():
        for cp in copies(0, 0):
            cp.start()

    acc[...] = jnp.zeros_like(acc)
    chunk = x2_ref.shape[0]
    row = lax.broadcasted_iota(jnp.int32, (COMBINE_GRANULE, chunk), 0)
    ids = ti_ref[0:TOP_K, :]
    ranks = ti_ref[TOP_K:2 * TOP_K, :]

    def group(g, carry):
        buf = g % 2

        @pl.when(g + 1 < n_groups)
        def _():
            for cp in copies(g + 1, 1 - buf):
                cp.start()

        for cp in copies(g, buf):
            cp.wait()
        pieces = []
        for s in range(COMBINE_SLOTS):
            live = g * COMBINE_SLOTS + s < count
            local = ranks - cbase_ref[item(g, s)]
            inside = (ids == jnp.where(live, cexp_ref[item(g, s)], -1)) & (local >= 0) & (local < COMBINE_GRANULE)
            hit_row = jnp.sum(jnp.where(inside, local + 1, 0), axis=0, keepdims=True) - 1
            pieces.append(jnp.where(row == hit_row, 1.0, 0.0).astype(BF16))
        onehot_t = jnp.concatenate(pieces, axis=0)
        acc[...] += lax.dot_general(onehot_t, slab[buf], (((0,), (0,)), ((), ())),
                                    preferred_element_type=F32)
        return carry

    lax.fori_loop(0, n_groups, group, 0)

    @pl.when(c + 1 < pl.num_programs(0))
    def _():
        nxt = coff_ref[c + 1]
        for cp in copies(0, 0, nxt, coff_ref[c + 2] - nxt):
            cp.start()

    y = acc[...]
    yn = y * lax.rsqrt(jnp.mean(y * y, axis=-1, keepdims=True) + EPS) * g_ref[...]
    o_ref[...] = x2_ref[...] + gf_ref[0] * yn


def combine_residual(yb, ti_t, x2, chunk_off, chunk_items, mod, mod_row_of_chunk, g_post_f):
    n, d = x2.shape
    chunk_row = lambda c, off, gran, exp, base: (c, 0)
    grid_spec = pltpu.PrefetchScalarGridSpec(
        num_scalar_prefetch=4,
        grid=(n // MOE_CHUNK,),
        in_specs=[pl.BlockSpec(memory_space=pl.ANY),
                  pl.BlockSpec((SUBLANES, MOE_CHUNK), lambda c, off, gran, exp, base: (0, c)),
                  pl.BlockSpec((MOE_CHUNK, d), chunk_row),
                  pl.BlockSpec((1, 1, d), lambda c, off, gran, exp, base: (mod_row_of_chunk(c), 0, 5)),
                  pl.BlockSpec((1, d), lambda c, off, gran, exp, base: (0, 0))],
        out_specs=pl.BlockSpec((MOE_CHUNK, d), chunk_row),
        scratch_shapes=[pltpu.VMEM((2, COMBINE_SLOTS * COMBINE_GRANULE, d), BF16),
                        pltpu.SemaphoreType.DMA((2,)),
                        pltpu.VMEM((MOE_CHUNK, d), F32)],
    )
    return pl.pallas_call(
        _combine_kernel,
        grid_spec=grid_spec,
        out_shape=jax.ShapeDtypeStruct((n, d), F32),
        compiler_params=_params("arbitrary"),
        name="combine_residual",
    )(chunk_off, *chunk_items, yb, ti_t, x2, mod, g_post_f.reshape(1, d))


def _grid_pos_embedding(rows, d):
    quarter = d // 4
    omega = POS_BASE ** (-jnp.arange(quarter, dtype=F32) / quarter)

    def emb(p):
        ang = p[:, None] * omega[None, :]
        return jnp.concatenate([jnp.sin(ang), jnp.cos(ang)], axis=-1)

    row_emb = jnp.repeat(emb(jnp.arange(rows, dtype=F32)), GRID_W, axis=0)
    col_emb = jnp.tile(emb(jnp.arange(GRID_W, dtype=F32)), (rows, 1))
    return jnp.concatenate([row_emb, col_emb], axis=-1)


def _count_le(sorted_vals, queries):
    return jnp.sum(sorted_vals <= queries[:, None], axis=1).astype(jnp.int32)


def _moe_plan(cnt_after, n_tok, tm):
    assert tm == GATHER_SUB and n_tok % MOE_CHUNK == 0 and MOE_CHUNK % tm == 0
    sub_after = cnt_after[:, 0, :N_EXPERTS].astype(jnp.int32)
    counts = sub_after[-1]
    padded = (counts + EXPERT_BLOCK - 1) // EXPERT_BLOCK * EXPERT_BLOCK
    padded_end = jnp.cumsum(padded)
    padded_start = padded_end - padded
    n_blocks = n_tok * TOP_K // EXPERT_BLOCK + N_EXPERTS
    blocks = jnp.arange(n_blocks, dtype=jnp.int32)
    block_expert = jnp.minimum(_count_le(padded_end[None, :], blocks * EXPERT_BLOCK), N_EXPERTS - 1)
    n_used = (padded_end[-1:] // EXPERT_BLOCK).astype(jnp.int32)
    of_block = (block_expert[:, None] == jnp.arange(N_EXPERTS, dtype=jnp.int32)[None, :]).astype(jnp.int32)
    r_lo = (blocks - jnp.sum(of_block * padded_start[None, :], axis=1) // EXPERT_BLOCK) * EXPERT_BLOCK
    r_hi = jnp.minimum(jnp.sum(of_block * counts[None, :], axis=1), r_lo + EXPERT_BLOCK) - 1
    through = jnp.sum(of_block[:, :, None] * sub_after.T[None, :, :], axis=1)
    after = jnp.sum(of_block * padded_end[None, :], axis=1) // EXPERT_BLOCK
    next_expert = jnp.sum((after[:, None] == blocks[None, :]).astype(jnp.int32) * block_expert[None, :], axis=1)
    next_expert = jnp.where(after < n_used[0], next_expert, -1).astype(jnp.int32)
    n_sub = sub_after.shape[0]
    sub_lo = jnp.minimum(_count_le(through, r_lo), n_sub - 1)
    sub_hi = jnp.minimum(_count_le(through, r_hi), n_sub - 1)
    nch = n_tok // MOE_CHUNK
    per_chunk = MOE_CHUNK // tm
    cb_after = sub_after[per_chunk - 1::per_chunk]
    cb_before = jnp.concatenate([jnp.zeros((1, N_EXPERTS), jnp.int32), cb_after[:-1]], axis=0)
    b_lo = ((padded_start[None, :] + cb_before) // COMBINE_GRANULE).reshape(-1)
    b_hi = ((padded_start[None, :] + cb_after - 1) // COMBINE_GRANULE).reshape(-1)
    n_it = jnp.where((cb_after > cb_before).reshape(-1), b_hi - b_lo + 1, 0)
    off_end = jnp.cumsum(n_it)
    off = off_end - n_it
    n_items = n_blocks * (EXPERT_BLOCK // COMBINE_GRANULE) + N_EXPERTS * (nch - 1)
    w = jnp.minimum(jnp.arange(n_items, dtype=jnp.int32), off_end[-1] - 1)
    chunk_end = off_end[N_EXPERTS - 1::N_EXPERTS]
    chunk_off = jnp.concatenate([jnp.zeros((1,), jnp.int32), chunk_end]).astype(jnp.int32)
    chunk_of = jnp.minimum(_count_le(chunk_end[None, :], w), nch - 1)
    chunk_expert = jnp.minimum(_count_le(off_end.reshape(nch, N_EXPERTS)[chunk_of], w), N_EXPERTS - 1)
    cell = chunk_of * N_EXPERTS + chunk_expert
    chunk_granules = (b_lo[cell] + w - off[cell]).astype(jnp.int32)
    chunk_base = chunk_granules * COMBINE_GRANULE - padded_start[chunk_expert]
    return block_expert, next_expert, r_lo, sub_lo, sub_hi, n_used, chunk_off, (chunk_granules, chunk_expert, chunk_base)


def kernel(x, c, ctx, c_ctx, w_mod, b_mod, g_pre_mix, g_post_mix, g_pre_ffn, g_post_ffn, w_in, conv_kv,
           conv_q, a_log, dt_bias, gdn_norm_g, w_proj_a, conf_dw, conf_dw_b, conf_ln_g, conf_ln_b,
           w_proj_b, w_out, w_router, b_router, w_gate_up, b_gate_up, w_down, b_down):
    batch, seq, d = x.shape
    ctx_len = ctx.shape[1]
    n_tok = batch * seq
    gw = HEADS * DV
    beta_off = HEADS * DK + gw
    state_cols = beta_off + 4 * HEADS
    q_off = state_cols

    mod_rows = 2 * SUBLANES
    cond = jnp.zeros((mod_rows, d), F32).at[:batch].set(c).at[batch].set(c_ctx)
    mod = modulation(cond, w_mod[0], b_mod[0]).reshape(mod_rows, 1, 6 * d)

    w_t = jnp.swapaxes(w_in[0], 0, 1)
    w_main = projection_weights(w_t, keep=beta_off, skip=state_cols - beta_off)
    w_ba = jnp.zeros((LANES, d), F32).at[:4 * HEADS].set(w_t[beta_off:state_cols]).astype(BF16)

    pos = _grid_pos_embedding(seq // GRID_W, d)
    x_flat = x.reshape(n_tok, d)
    tm = 512
    tiles_per_seq = seq // tm
    proj, ba = input_projection(x_flat, pos, mod, lambda i: i // tiles_per_seq, g_pre_mix[0],
                                w_main, w_ba, tm=tm, tn=w_main.shape[0])
    proj_ctx, ba_ctx = input_projection(ctx.reshape(batch * ctx_len, d), None, mod, lambda i: batch,
                                        g_pre_mix[0], w_main[:beta_off], w_ba, tm=ctx_len, tn=1024)

    og = gated_deltanet(proj, ba, proj_ctx, ba_ctx, conv_kv[0], conv_q[0], a_log[0], dt_bias[0],
                        gdn_norm_g[0], batch=batch, seq=seq, ctx=ctx_len)
    uc = conformer_conv(proj, conf_dw[0], conf_dw_b[0], batch=batch, seq=seq)

    tm2 = 512
    w_r = jnp.zeros((d, LANES), F32).at[:, :N_EXPERTS].set(w_router[0]).astype(BF16)
    b_r = jnp.zeros((1, LANES), F32).at[0, :N_EXPERTS].set(b_router[0])
    x2, h2, top_i, top_w, cnt_after = mixer_output(
        og, uc, proj, x_flat, pos, mod, lambda i: i // (seq // tm2), g_post_mix[0], g_pre_ffn[0],
        conf_ln_g[0], conf_ln_b[0], w_proj_a[0].astype(BF16), w_proj_b[0].astype(BF16),
        w_out[0].astype(BF16), w_r, b_r, tm=tm2, sub=GATHER_SUB)

    block_expert, next_expert, rank_lo, sub_lo, sub_hi, n_used, chunk_off, chunk_items = _moe_plan(
        cnt_after, n_tok, GATHER_SUB)
    yb = expert_blocks(h2, top_i, top_w, block_expert, next_expert, rank_lo, sub_lo, sub_hi, n_used, w_gate_up[0],
                       b_gate_up[0], w_down[0], b_down[0])
    out = combine_residual(yb, top_i, x2, chunk_off, chunk_items, mod, lambda c: c // (seq // MOE_CHUNK),
                           g_post_ffn[0])
    return out.reshape(batch, seq, d)
```

```python
import functools

import jax
import jax.numpy as jnp
from jax import lax
from jax.experimental import pallas as pl
from jax.experimental.pallas import tpu as pltpu

F32 = jnp.float32
BF16 = jnp.bfloat16

D_MODEL = 1024
GRID_W = 64
HEADS = 8
DK = 128
DV = 128
SHORT_CONV = 5
CHUNK = 64
CONF_KERNEL = 31
N_EXPERTS = 32
TOP_K = 4
SWIGLU_LIMIT = 7.0
SWIGLU_ALPHA = 1.702
MOE_BLOCK = 128
EPS = 1e-6
POS_BASE = 10000.0

LANES = 128
SUBLANES = 8
VMEM_LIMIT = 60 * 1024 * 1024

COL_K, COL_V, COL_Q, COL_Z, COL_GLU_A, COL_GLU_G, COL_GATE_A, COL_GATE_B = range(8)


def _params(*sem):
    return pltpu.CompilerParams(dimension_semantics=sem, vmem_limit_bytes=VMEM_LIMIT)


def _mod_kernel(c_ref, w_ref, b_ref, o_ref):
    c = c_ref[...]
    s = c * jax.nn.sigmoid(c)
    o_ref[...] = jnp.dot(s, w_ref[...], preferred_element_type=F32,
                         precision=lax.Precision.HIGHEST) + b_ref[...]


def modulation(cond, w_mod, b_mod):
    r, d = cond.shape
    n = w_mod.shape[1]
    tn = 1024
    return pl.pallas_call(
        _mod_kernel,
        grid=(n // tn,),
        in_specs=[pl.BlockSpec((r, d), lambda j: (0, 0)),
                  pl.BlockSpec((d, tn), lambda j: (0, j)),
                  pl.BlockSpec((1, tn), lambda j: (0, j))],
        out_specs=pl.BlockSpec((r, tn), lambda j: (0, j)),
        out_shape=jax.ShapeDtypeStruct((r, n), F32),
        compiler_params=_params("arbitrary"),
        name="modulation",
    )(cond, w_mod, b_mod.reshape(1, n))


def _proj_weights_kernel(a_ref, b_ref, o_ref, *, first_shifted, shift):
    @pl.when(pl.program_id(0) < first_shifted)
    def _():
        o_ref[...] = a_ref[...].astype(o_ref.dtype)

    @pl.when(pl.program_id(0) >= first_shifted)
    def _():
        o_ref[...] = jnp.concatenate([a_ref[shift:, :], b_ref[...]], axis=0).astype(o_ref.dtype)


def projection_weights(w_t, *, keep, skip, tr=1024):
    total, d = w_t.shape
    n_out = total - skip
    assert keep % tr == 0 and n_out % tr == 0 and tr % skip == 0 and skip % SUBLANES == 0
    return pl.pallas_call(
        functools.partial(_proj_weights_kernel, first_shifted=keep // tr, shift=skip),
        grid=(n_out // tr,),
        in_specs=[pl.BlockSpec((tr, d), lambda j: (j, 0)),
                  pl.BlockSpec((skip, d), lambda j: ((j + 1) * (tr // skip), 0))],
        out_specs=pl.BlockSpec((tr, d), lambda j: (j, 0)),
        out_shape=jax.ShapeDtypeStruct((n_out, d), BF16),
        compiler_params=_params("arbitrary"),
        name="projection_weights",
    )(w_t, w_t)


_NT = (((1,), (1,)), ((), ()))


def _inproj_kernel(*refs, has_pos):
    if has_pos:
        x_ref, pos_ref, sh_ref, sc_ref, g_ref, w_ref, wba_ref, o_ref, ba_ref, h_scr = refs
    else:
        x_ref, sh_ref, sc_ref, g_ref, w_ref, wba_ref, o_ref, ba_ref, h_scr = refs

    @pl.when(pl.program_id(1) == 0)
    def _():
        x = x_ref[...]
        if has_pos:
            x = x + pos_ref[...]
        y = x * lax.rsqrt(jnp.mean(x * x, axis=-1, keepdims=True) + EPS) * g_ref[...]
        h = (y * (1.0 + sc_ref[0]) + sh_ref[0]).astype(BF16)
        h_scr[...] = h
        ba_ref[...] = lax.dot_general(h, wba_ref[...], _NT, preferred_element_type=F32)

    o_ref[...] = lax.dot_general(h_scr[...], w_ref[...], _NT, preferred_element_type=F32).astype(o_ref.dtype)


def input_projection(x, pos, mod, mod_row_of_tile, g_pre, w_main, w_ba, *, tm, tn):
    n, d = x.shape
    w = w_main.shape[0]
    assert n % tm == 0 and w % tn == 0
    has_pos = pos is not None
    in_specs = [pl.BlockSpec((tm, d), lambda i, j: (i, 0))]
    args = [x]
    if has_pos:
        pos_tiles = pos.shape[0] // tm
        in_specs.append(pl.BlockSpec((tm, d), lambda i, j: (i % pos_tiles, 0)))
        args.append(pos)
    in_specs += [
        pl.BlockSpec((1, 1, d), lambda i, j: (mod_row_of_tile(i), 0, 0)),
        pl.BlockSpec((1, 1, d), lambda i, j: (mod_row_of_tile(i), 0, 1)),
        pl.BlockSpec((1, d), lambda i, j: (0, 0)),
        pl.BlockSpec((tn, d), lambda i, j: (j, 0)),
        pl.BlockSpec((LANES, d), lambda i, j: (0, 0)),
    ]
    args += [mod, mod, g_pre.reshape(1, d), w_main, w_ba]
    return pl.pallas_call(
        functools.partial(_inproj_kernel, has_pos=has_pos),
        grid=(n // tm, w // tn),
        in_specs=in_specs,
        out_specs=[pl.BlockSpec((tm, tn), lambda i, j: (i, j)),
                   pl.BlockSpec((tm, LANES), lambda i, j: (i, 0))],
        out_shape=[jax.ShapeDtypeStruct((n, w), BF16), jax.ShapeDtypeStruct((n, LANES), F32)],
        scratch_shapes=[pltpu.VMEM((tm, d), BF16)],
        compiler_params=_params("arbitrary", "arbitrary"),
        name="input_projection_pos" if has_pos else "input_projection_ctx",
    )(*args)


def _silu(x):
    return x * jax.nn.sigmoid(x)


def _softplus(x):
    return jnp.maximum(x, 0.0) + jnp.log(1.0 + jnp.exp(-jnp.abs(x)))


def _short_conv(src, taps, pad_scr, rows):
    half = SHORT_CONV // 2
    pad_scr[0:SUBLANES, :] = jnp.zeros((SUBLANES, LANES), F32)
    pad_scr[SUBLANES:SUBLANES + rows, :] = src.astype(F32)
    pad_scr[SUBLANES + rows:2 * SUBLANES + rows, :] = jnp.zeros((SUBLANES, LANES), F32)
    acc = None
    for j in range(SHORT_CONV):
        off = SUBLANES + j - half
        term = pad_scr[off:off + rows, :] * taps[j:j + 1, :]
        acc = term if acc is None else acc + term
    return _silu(acc)


def _l2n(x):
    return x * lax.rsqrt(jnp.sum(x * x, axis=-1, keepdims=True) + EPS)


def _gdn_kernel(proj_hbm, ba_hbm, kc_ref, vc_ref, bac_ref,
                wk_ref, wv_ref, wq_ref, ab_ref, gn_ref, o_ref,
                k_ref, v_ref, q_ref, z_ref, ba_ref, in_sem,
                kf, vf, qf, gates, pad_scr, lhs, val, att, ktt, dec, osc, *, seq, ctx, unroll, group):
    total = ctx + seq
    n_batch, n_groups = pl.num_programs(0), pl.num_programs(1)
    gl = group * LANES

    def block(bb, hg, base):
        return proj_hbm.at[pl.ds(pl.multiple_of(bb * seq, seq), seq),
                           pl.ds(pl.multiple_of((base * n_groups + hg) * gl, gl), gl)]

    def input_copies(bb, hg):
        return [pltpu.make_async_copy(block(bb, hg, COL_K), k_ref, in_sem.at[0]),
                pltpu.make_async_copy(block(bb, hg, COL_V), v_ref, in_sem.at[1]),
                pltpu.make_async_copy(block(bb, hg, COL_Q), q_ref, in_sem.at[2]),
                pltpu.make_async_copy(ba_hbm.at[pl.ds(pl.multiple_of(bb * seq, seq), seq), :], ba_ref, in_sem.at[3])]

    b_now, g_now = pl.program_id(0), pl.program_id(1)
    z_copy = pltpu.make_async_copy(block(b_now, g_now, COL_Z), z_ref, in_sem.at[4])

    @pl.when((b_now == 0) & (g_now == 0))
    def _():
        for cp in input_copies(b_now, g_now):
            cp.start()

    for cp in input_copies(b_now, g_now):
        cp.wait()
    z_copy.start()

    def gate_values(x):
        lane = lax.broadcasted_iota(jnp.int32, x.shape, 1)
        g = -jnp.exp(ab_ref[0:1, :]) * _softplus(x + ab_ref[1:2, :])
        return jnp.where(lane < 2 * HEADS, jax.nn.sigmoid(x), g)

    ba_ref[...] = gate_values(ba_ref[...])
    nc = total // CHUNK
    ncc = ctx // CHUNK
    two = 2 * CHUNK
    lg_chunk = CHUNK.bit_length() - 1

    ri = lax.broadcasted_iota(jnp.int32, (two, two), 0)
    ci = lax.broadcasted_iota(jnp.int32, (two, two), 1)
    same_dir = (ri >> lg_chunk) == (ci >> lg_chunk)
    incl = same_dir & (((ri < CHUNK) & (ri >= ci)) | ((ri >= CHUNK) & (ri <= ci)))
    strict = incl & (ri != ci)
    eye = jnp.where(ri == ci, 1.0, 0.0)
    tri = jnp.where(incl, 1.0, 0.0).astype(BF16)
    level_masks = []
    for lg in range(lg_chunk):
        same_parent = (ri >> (lg + 1)) == (ci >> (lg + 1))
        level_masks.append(jnp.where(same_parent & ((ri >> lg) != (ci >> lg)), 1.0, 0.0))
    top_rows = lax.broadcasted_iota(jnp.int32, (two, LANES), 0) < CHUNK

    def bwd_chunk(t):
        return jnp.where(t < ncc, ncc - 1 - t, nc + ncc - 1 - t)

    def stacked(ref, rf, rb):
        return jnp.concatenate([ref[pl.ds(rf, CHUNK), :], ref[pl.ds(rb, CHUNK), :]], axis=0)

    def chunk_load(t):
        rf = pl.multiple_of(t * CHUNK, CHUNK)
        rb = pl.multiple_of(bwd_chunk(t) * CHUNK, CHUNK)
        beta = jnp.concatenate([gates[0, pl.ds(rf, CHUNK), :], gates[1, pl.ds(rb, CHUNK), :]], axis=0)
        g = jnp.concatenate([gates[2, pl.ds(rf, CHUNK), :], gates[3, pl.ds(rb, CHUNK), :]], axis=0)
        return stacked(kf, rf, rb), stacked(vf, rf, rb), stacked(qf, rf, rb), beta, g

    def chunk_triangle(k, q, beta, g):
        kb = k.astype(BF16)
        kq = jnp.concatenate([kb, q.astype(BF16)], axis=0)
        kk_qk = lax.dot_general(kq, kb, (((1,), (1,)), ((), ())), preferred_element_type=F32)
        g_hi = g.astype(BF16)
        r1 = g - g_hi.astype(F32)
        g_mid = r1.astype(BF16)
        g_lo = (r1 - g_mid.astype(F32)).astype(BF16)
        gc3 = jnp.dot(tri, jnp.concatenate([g_hi, g_mid, g_lo], axis=1), preferred_element_type=F32)
        gc = gc3[:, :LANES] + gc3[:, LANES:2 * LANES] + gc3[:, 2 * LANES:]
        gc_row = jnp.transpose(gc)
        decay = jnp.where(incl, jnp.exp(gc - gc_row), 0.0)
        a = jnp.where(strict, beta * kk_qk[:two] * decay, 0.0)
        return a, (kk_qk[two:] * decay).astype(BF16), gc

    def chunk_finish(hh, t, k, v, q, beta, gc, attn, tmat):
        eg = jnp.exp(gc)
        rhs = jnp.concatenate([v * beta, k * beta * eg], axis=1).astype(BF16)
        sol = jnp.dot(tmat.astype(BF16), rhs, preferred_element_type=F32)
        g_end = jnp.where(top_rows, jnp.broadcast_to(gc[CHUNK - 1:CHUNK, :], (two, LANES)),
                          jnp.broadcast_to(gc[CHUNK:CHUNK + 1, :], (two, LANES)))
        k_tail = k * jnp.exp(g_end - gc)
        qg = (q * eg).astype(BF16)
        kcum = sol[:, DV:].astype(BF16)
        r0 = pl.multiple_of(t * two, two)
        lhs[hh, 0, pl.ds(r0, two), :] = jnp.concatenate([kcum[:CHUNK], qg[:CHUNK]], axis=0)
        lhs[hh, 1, pl.ds(r0, two), :] = jnp.concatenate([kcum[CHUNK:], qg[CHUNK:]], axis=0)
        val[hh, pl.ds(r0, two), :] = sol[:, :DV].astype(BF16)
        att[hh, pl.ds(r0, two), :] = attn
        ktt[hh, pl.ds(r0, two), :] = jnp.transpose(k_tail).astype(BF16)
        d0 = pl.multiple_of(t * SUBLANES, SUBLANES)
        e_end = jnp.exp(g_end)
        dec[hh, 0, pl.ds(d0, SUBLANES), :] = e_end[:SUBLANES]
        dec[hh, 1, pl.ds(d0, SUBLANES), :] = e_end[CHUNK:CHUNK + SUBLANES]

    for hh in range(group):
        head = pl.program_id(1) * group + hh
        cols = slice(hh * LANES, (hh + 1) * LANES)

        kf[0:ctx, :] = _l2n(_short_conv(kc_ref[:, cols], wk_ref[:, cols], pad_scr, ctx))
        vf[0:ctx, :] = _short_conv(vc_ref[:, cols], wv_ref[:, cols], pad_scr, ctx)
        qf[0:ctx, :] = jnp.zeros((ctx, LANES), F32)
        kf[ctx:total, :] = _l2n(_short_conv(k_ref[:, cols], wk_ref[:, cols], pad_scr, seq))
        vf[ctx:total, :] = _short_conv(v_ref[:, cols], wv_ref[:, cols], pad_scr, seq)
        qf[ctx:total, :] = _l2n(_short_conv(q_ref[:, cols], wq_ref[:, cols], pad_scr, seq)) * (DK ** -0.5)

        def gate_cols(vals, lo, rows):
            lane = lax.broadcasted_iota(jnp.int32, (rows, LANES), 1)
            for slot in range(4):
                col = jnp.sum(jnp.where(lane == slot * HEADS + head, vals, 0.0), axis=-1, keepdims=True)
                gates[slot, lo:lo + rows, :] = jnp.broadcast_to(col, (rows, LANES))

        gate_cols(gate_values(bac_ref[...]), 0, ctx)
        gate_cols(ba_ref[...], ctx, seq)

        def chunk_body(i, carry, hh=hh):
            steps = [i * unroll + u for u in range(unroll)]
            loaded = [chunk_load(t) for t in steps]
            tris = [chunk_triangle(k, q, beta, g) for (k, v, q, beta, g) in loaded]
            tmats = [eye - a * level_masks[0] for (a, _, _) in tris]
            for m in level_masks[1:]:
                nxt = []
                for (a, _, _), tmat in zip(tris, tmats):
                    cs = (a * m).astype(BF16)
                    tb = tmat.astype(BF16)
                    tc = jnp.dot(tb, cs, preferred_element_type=F32)
                    nxt.append(tmat - jnp.dot(tc.astype(BF16), tb, preferred_element_type=F32))
                tmats = nxt
            for t, (k, v, q, beta, g), (a, attn, gc), tmat in zip(steps, loaded, tris, tmats):
                chunk_finish(hh, t, k, v, q, beta, gc, attn, tmat)
            return carry

        lax.fori_loop(0, nc // unroll, chunk_body, 0)

    wrap = g_now + 1 == n_groups

    @pl.when(jnp.logical_not(wrap & (b_now + 1 == n_batch)))
    def _():
        for cp in input_copies(jnp.where(wrap, b_now + 1, b_now), jnp.where(wrap, 0, g_now + 1)):
            cp.start()

    def scan_body(t, carry):
        rf = pl.multiple_of(t * CHUNK, CHUNK)
        rb = pl.multiple_of(bwd_chunk(t) * CHUNK, CHUNK)
        r0 = pl.multiple_of(t * two, two)
        d0 = pl.multiple_of(t * SUBLANES, SUBLANES)
        rs = [(jnp.dot(lhs[hh, 0, pl.ds(r0, two), :], carry[2 * hh].astype(BF16), preferred_element_type=F32),
               jnp.dot(lhs[hh, 1, pl.ds(r0, two), :], carry[2 * hh + 1].astype(BF16), preferred_element_type=F32))
              for hh in range(group)]
        v_news = [val[hh, pl.ds(r0, two), :].astype(F32) - jnp.concatenate([r_f[:CHUNK], r_b[:CHUNK]], axis=0)
                  for hh, (r_f, r_b) in enumerate(rs)]
        outs, new_states = [], []
        for hh, ((r_f, r_b), v_new) in enumerate(zip(rs, v_news)):
            zeros = jnp.zeros_like(v_new)
            v_bd = jnp.concatenate([jnp.where(top_rows, v_new, zeros), jnp.where(top_rows, zeros, v_new)],
                                   axis=1).astype(BF16)
            upd = jnp.dot(ktt[hh, pl.ds(r0, two), :], v_bd, preferred_element_type=F32)
            new_states.append(carry[2 * hh] * dec[hh, 0, pl.ds(d0, SUBLANES), :][0:1, :] + upd[:, :DV])
            new_states.append(carry[2 * hh + 1] * dec[hh, 1, pl.ds(d0, SUBLANES), :][0:1, :] + upd[:, DV:])
            outs.append(jnp.concatenate([r_f[CHUNK:], r_b[CHUNK:]], axis=0)
                        + jnp.dot(att[hh, pl.ds(r0, two), :], v_new.astype(BF16), preferred_element_type=F32))
        for hh, o in enumerate(outs):
            osc[hh, pl.ds(rf, CHUNK), :] += o[:CHUNK]
            osc[hh, pl.ds(rb, CHUNK), :] += o[CHUNK:]
        return tuple(new_states)

    osc[...] = jnp.zeros_like(osc)
    zero = jnp.zeros((DK, DV), F32)
    lax.fori_loop(0, nc, scan_body, (zero,) * (2 * group))

    z_copy.wait()
    for hh in range(group):
        cols = slice(hh * LANES, (hh + 1) * LANES)
        o = osc[hh, ctx:total, :]
        o = o * lax.rsqrt(jnp.mean(o * o, axis=-1, keepdims=True) + EPS) * gn_ref[...]
        o_ref[:, cols] = (o * _silu(z_ref[:, cols].astype(F32))).astype(o_ref.dtype)


def gated_deltanet(proj, ba, proj_ctx, ba_ctx, conv_kv, conv_q, a_log, dt_bias, gn, *, batch, seq, ctx,
                   unroll=18, group=4):
    total = seq + ctx
    nc = total // CHUNK
    assert seq % CHUNK == 0 and ctx % CHUNK == 0 and nc % unroll == 0 and HEADS % group == 0
    gl = group * LANES
    ng = HEADS // group
    taps = jnp.zeros((SUBLANES, conv_kv.shape[1]), F32).at[:SHORT_CONV].set(conv_kv)
    taps_q = jnp.zeros((SUBLANES, conv_q.shape[1]), F32).at[:SHORT_CONV].set(conv_q)
    ab = jnp.zeros((SUBLANES, LANES), F32)
    ab = ab.at[0, 2 * HEADS:4 * HEADS].set(a_log.reshape(-1)).at[1, 2 * HEADS:4 * HEADS].set(dt_bias.reshape(-1))
    col = lambda base: (lambda b, h: (b, base * ng + h))
    return pl.pallas_call(
        functools.partial(_gdn_kernel, seq=seq, ctx=ctx, unroll=unroll, group=group),
        grid=(batch, ng),
        in_specs=[
            pl.BlockSpec(memory_space=pl.ANY),
            pl.BlockSpec(memory_space=pl.ANY),
            pl.BlockSpec((ctx, gl), col(COL_K)),
            pl.BlockSpec((ctx, gl), col(COL_V)),
            pl.BlockSpec((ctx, LANES), lambda b, h: (b, 0)),
            pl.BlockSpec((SUBLANES, gl), lambda b, h: (0, h)),
            pl.BlockSpec((SUBLANES, gl), lambda b, h: (0, ng + h)),
            pl.BlockSpec((SUBLANES, gl), lambda b, h: (0, h)),
            pl.BlockSpec((SUBLANES, LANES), lambda b, h: (0, 0)),
            pl.BlockSpec((1, LANES), lambda b, h: (0, 0)),
        ],
        out_specs=pl.BlockSpec((seq, gl), lambda b, h: (b, h)),
        out_shape=jax.ShapeDtypeStruct((batch * seq, HEADS * DV), BF16),
        scratch_shapes=[
            pltpu.VMEM((seq, gl), BF16),
            pltpu.VMEM((seq, gl), BF16),
            pltpu.VMEM((seq, gl), BF16),
            pltpu.VMEM((seq, gl), BF16),
            pltpu.VMEM((seq, LANES), F32),
            pltpu.SemaphoreType.DMA((5,)),
            pltpu.VMEM((total, LANES), F32),
            pltpu.VMEM((total, LANES), F32),
            pltpu.VMEM((total, LANES), F32),
            pltpu.VMEM((4, total, LANES), F32),
            pltpu.VMEM((seq + 2 * SUBLANES, LANES), F32),
            pltpu.VMEM((group, 2, nc * 2 * CHUNK, LANES), BF16),
            pltpu.VMEM((group, nc * 2 * CHUNK, LANES), BF16),
            pltpu.VMEM((group, nc * 2 * CHUNK, LANES), BF16),
            pltpu.VMEM((group, nc * 2 * CHUNK, LANES), BF16),
            pltpu.VMEM((group, 2, nc * SUBLANES, LANES), F32),
            pltpu.VMEM((group, total, LANES), F32),
        ],
        compiler_params=_params("arbitrary", "arbitrary"),
        name="gated_deltanet",
    )(proj, ba, proj_ctx, proj_ctx, ba_ctx, taps, taps, taps_q, ab, gn.reshape(1, DV))


CONF_ROWS = 512
CONF_HALO = 16


def _conf_kernel(a_ref, g_ref, w_ref, b_ref, o_ref, pad_scr, *, seq):
    cb = a_ref.shape[1]
    u = a_ref[...].astype(F32) * jax.nn.sigmoid(g_ref[...].astype(F32))
    pad_scr[0:CONF_HALO, :] = jnp.zeros((CONF_HALO, cb), F32)
    pad_scr[CONF_HALO:CONF_HALO + seq, :] = u
    pad_scr[CONF_HALO + seq:2 * CONF_HALO + seq, :] = jnp.zeros((CONF_HALO, cb), F32)
    win = CONF_ROWS + 2 * CONF_HALO
    first = CONF_HALO - CONF_KERNEL // 2

    def body(i, carry):
        r0 = pl.multiple_of(i * CONF_ROWS, CONF_ROWS)
        w = pad_scr[pl.ds(r0, win), :]
        acc = jnp.zeros((CONF_ROWS, cb), F32) + b_ref[...]
        for sub in range(SUBLANES):
            shifted = w if sub == 0 else pltpu.roll(w, win - sub, axis=0)
            for j in range(CONF_KERNEL):
                off = first + j
                if off % SUBLANES == sub:
                    base = off - sub
                    acc = acc + shifted[base:base + CONF_ROWS, :] * w_ref[j:j + 1, :]
        o_ref[pl.ds(r0, CONF_ROWS), :] = acc.astype(o_ref.dtype)
        return carry

    lax.fori_loop(0, seq // CONF_ROWS, body, 0)


def conformer_conv(proj, conf_dw, conf_dw_b, *, batch, seq, cb=128):
    c = conf_dw.shape[1]
    assert seq % CONF_ROWS == 0 and c % cb == 0
    nb = c // cb
    taps = jnp.zeros((32, c), F32).at[:CONF_KERNEL].set(conf_dw)
    return pl.pallas_call(
        functools.partial(_conf_kernel, seq=seq),
        grid=(batch, nb),
        in_specs=[pl.BlockSpec((seq, cb), lambda b, j: (b, COL_GLU_A * nb + j)),
                  pl.BlockSpec((seq, cb), lambda b, j: (b, COL_GLU_G * nb + j)),
                  pl.BlockSpec((32, cb), lambda b, j: (0, j)),
                  pl.BlockSpec((1, cb), lambda b, j: (0, j))],
        out_specs=pl.BlockSpec((seq, cb), lambda b, j: (b, j)),
        out_shape=jax.ShapeDtypeStruct((batch * seq, c), BF16),
        scratch_shapes=[pltpu.VMEM((seq + 2 * CONF_HALO, cb), F32)],
        compiler_params=_params("arbitrary", "arbitrary"),
        name="conformer_conv",
    )(proj, proj, taps, conf_dw_b.reshape(1, c))


def _mixer_out_kernel(og_ref, uc_ref, ga_ref, gb_ref, x_ref, pos_ref, gm_ref, shf_ref, scf_ref,
                      gpost_ref, gpre_ref, lng_ref, lnb_ref, wa_ref, wb_ref, wo_ref, wr_ref, br_ref,
                      x2_ref, h2_ref, ti_ref, tw_ref, cnt_ref, cnt_scr, *, sub):
    parts = [slice(r, r + sub) for r in range(0, og_ref.shape[0], sub)]

    def rms(v):
        return v * lax.rsqrt(jnp.mean(v * v, axis=-1, keepdims=True) + EPS)

    def layer_norm_silu(uc):
        mu = jnp.mean(uc, axis=-1, keepdims=True)
        var = jnp.mean(jnp.square(uc - mu), axis=-1, keepdims=True)
        return _silu((uc - mu) * lax.rsqrt(var + EPS) * lng_ref[...] + lnb_ref[...]).astype(BF16)

    us = [layer_norm_silu(uc_ref[p, :].astype(F32)) for p in parts]
    yas = [jnp.dot(og_ref[p, :], wa_ref[...], preferred_element_type=F32) for p in parts]
    ybs = [jnp.dot(u, wb_ref[...], preferred_element_type=F32) for u in us]
    merged = [(jax.nn.sigmoid(ga_ref[p, :].astype(F32)) * ya
               + jax.nn.sigmoid(gb_ref[p, :].astype(F32)) * yb).astype(BF16) for p, ya, yb in zip(parts, yas, ybs)]
    ys = [jnp.dot(m, wo_ref[...], preferred_element_type=F32) for m in merged]
    x2s = [x_ref[p, :] + pos_ref[p, :] + gm_ref[0] * (rms(y) * gpost_ref[...]) for p, y in zip(parts, ys)]
    h2s = [(rms(x2) * gpre_ref[...] * (1.0 + scf_ref[0]) + shf_ref[0]).astype(BF16) for x2 in x2s]
    for p, x2, h2 in zip(parts, x2s, h2s):
        x2_ref[p, :] = x2
        h2_ref[p, :] = h2
    all_logits = [jnp.dot(h2, wr_ref[...], preferred_element_type=F32) + br_ref[...] for h2 in h2s]
    lane = lax.broadcasted_iota(jnp.int32, (sub, LANES), 1).astype(F32)
    ri = lax.broadcasted_iota(jnp.int32, (sub, sub), 0)
    ci = lax.broadcasted_iota(jnp.int32, (sub, sub), 1)
    before = jnp.where(ri > ci, 1.0, 0.0).astype(BF16)

    @pl.when(pl.program_id(0) == 0)
    def _():
        cnt_scr[...] = jnp.zeros_like(cnt_scr)

    routed = []
    for logits in all_logits:
        live = jnp.where(lane < N_EXPERTS, logits, -jnp.inf)
        top_v, top_i = [], []
        for _ in range(TOP_K):
            m = jnp.max(live, axis=-1, keepdims=True)
            idx = jnp.min(jnp.where(live == m, lane, float(LANES)), axis=-1, keepdims=True)
            top_v.append(m)
            top_i.append(idx)
            live = jnp.where(lane == idx, -jnp.inf, live)
        ex = [jnp.exp(v - top_v[0]) for v in top_v]
        denom = ex[0] + ex[1] + ex[2] + ex[3]
        picked = jnp.zeros((sub, LANES), F32)
        for k in range(TOP_K):
            picked = jnp.where(lane == top_i[k], 1.0, picked)
        routed.append((top_i, [e / denom for e in ex], picked,
                       jnp.dot(before, picked.astype(BF16), preferred_element_type=F32)))

    count = cnt_scr[...]
    for j, (p, (top_i, top_w, picked, within)) in enumerate(zip(parts, routed)):
        prefix = within + count[0:1, :]
        ti = jnp.zeros((sub, LANES), F32)
        tw = jnp.zeros((sub, LANES), F32)
        for k in range(TOP_K):
            rank = jnp.sum(jnp.where(lane == top_i[k], prefix, 0.0), axis=-1, keepdims=True)
            ti = jnp.where(lane == k, top_i[k], ti)
            ti = jnp.where(lane == TOP_K + k, rank, ti)
            tw = jnp.where(lane == k, top_w[k], tw)
        ti_ref[:, p] = jnp.transpose(ti)[:SUBLANES, :].astype(jnp.int32)
        tw_ref[:, p] = jnp.transpose(tw)[:SUBLANES, :]
        count = count + jnp.sum(picked, axis=0, keepdims=True)
        cnt_ref[j] = count
    cnt_scr[...] = count


def mixer_output(og, uc, proj, x, pos, mod, mod_row_of_tile, g_post, g_pre_f, ln_g, ln_b,
                 w_a, w_b, w_o, w_r, b_r, *, tm, sub):
    n, d = x.shape
    assert n % tm == 0 and pos.shape[0] % tm == 0 and tm % sub == 0
    pos_tiles = pos.shape[0] // tm
    row = lambda i: (i, 0)
    fixed = lambda i: (0, 0)
    modspec = lambda k: pl.BlockSpec((1, 1, d), lambda i: (mod_row_of_tile(i), 0, k))
    vec = pl.BlockSpec((1, d), fixed)
    mat = pl.BlockSpec((d, d), fixed)
    return pl.pallas_call(
        functools.partial(_mixer_out_kernel, sub=sub),
        grid=(n // tm,),
        in_specs=[pl.BlockSpec((tm, d), row), pl.BlockSpec((tm, d), row),
                  pl.BlockSpec((tm, d), lambda i: (i, COL_GATE_A)),
                  pl.BlockSpec((tm, d), lambda i: (i, COL_GATE_B)),
                  pl.BlockSpec((tm, d), row),
                  pl.BlockSpec((tm, d), lambda i: (i % pos_tiles, 0)),
                  modspec(2), modspec(3), modspec(4),
                  vec, vec, vec, vec, mat, mat, mat,
                  pl.BlockSpec((d, LANES), fixed), pl.BlockSpec((1, LANES), fixed)],
        out_specs=[pl.BlockSpec((tm, d), row), pl.BlockSpec((tm, d), row),
                   pl.BlockSpec((SUBLANES, tm), lambda i: (0, i)), pl.BlockSpec((SUBLANES, tm), lambda i: (0, i)),
                   pl.BlockSpec((tm // sub, SUBLANES, LANES), lambda i: (i, 0, 0))],
        out_shape=[jax.ShapeDtypeStruct((n, d), F32), jax.ShapeDtypeStruct((n, d), BF16),
                   jax.ShapeDtypeStruct((SUBLANES, n), jnp.int32), jax.ShapeDtypeStruct((SUBLANES, n), F32),
                   jax.ShapeDtypeStruct((n // sub, SUBLANES, LANES), F32)],
        scratch_shapes=[pltpu.VMEM((SUBLANES, LANES), F32)],
        compiler_params=_params("arbitrary"),
        name="mixer_output",
    )(og, uc, proj, proj, x, pos, mod, mod, mod,
      g_post.reshape(1, d), g_pre_f.reshape(1, d), ln_g.reshape(1, d), ln_b.reshape(1, d),
      w_a, w_b, w_o, w_r, b_r)


EXPERT_BLOCK = 256
MOE_CHUNK = 512
GATHER_SUB = 256
GATHER_GROUP = 3
COMBINE_GRANULE = 32
COMBINE_SLOTS = 32


def _expert_kernel(bexp_ref, bnext_ref, rlo_ref, slo_ref, shi_ref, nused_ref,
                   h2_ref, ti_ref, wt_ref, wgu_hbm, bgu_ref, wd_hbm, bd_ref, o_ref,
                   acc, rw_acc, wgu_bf, wd_bf, wgu_f32, wd_f32, wsem):
    b = pl.program_id(0)
    expert = bexp_ref[b]

    def fetch(e):
        return (pltpu.make_async_copy(wgu_hbm.at[e], wgu_f32, wsem.at[0]),
                pltpu.make_async_copy(wd_hbm.at[e], wd_f32, wsem.at[1]))

    @pl.when(((b == 0) | (expert != bexp_ref[jnp.maximum(b - 1, 0)])) & (b < nused_ref[0]))
    def _():
        @pl.when(b == 0)
        def _():
            for cp in fetch(expert):
                cp.start()

        for cp in fetch(expert):
            cp.wait()
        wgu_bf[...] = wgu_f32[...].astype(BF16)
        wd_bf[...] = wd_f32[...].astype(BF16)

        @pl.when(bnext_ref[b] >= 0)
        def _():
            for cp in fetch(bnext_ref[b]):
                cp.start()

    @pl.when(b < nused_ref[0])
    def _():
        acc[...] = jnp.zeros_like(acc)
        rw_acc[...] = jnp.zeros_like(rw_acc)
        row = lax.broadcasted_iota(jnp.int32, (EXPERT_BLOCK, GATHER_SUB), 0)
        s_lo, s_hi = slo_ref[b], shi_ref[b]
        last_sub = h2_ref.shape[0] // GATHER_SUB - 1

        def select(s, live):
            t0 = pl.multiple_of(s * GATHER_SUB, GATHER_SUB)
            ids = ti_ref[0:TOP_K, pl.ds(t0, GATHER_SUB)]
            local = ti_ref[TOP_K:2 * TOP_K, pl.ds(t0, GATHER_SUB)] - rlo_ref[b]
            if live is not True:
                local = jnp.where(live, local, -1)
            inside = (ids == expert) & (local >= 0) & (local < EXPERT_BLOCK)
            hit_row = jnp.sum(jnp.where(inside, local + 1, 0), axis=0, keepdims=True) - 1
            hit_w = jnp.sum(jnp.where(inside, wt_ref[0:TOP_K, pl.ds(t0, GATHER_SUB)], 0.0), axis=0, keepdims=True)
            hit = row == hit_row
            return (jnp.where(hit, 1.0, 0.0).astype(BF16), jnp.where(hit, hit_w, 0.0),
                    h2_ref[pl.ds(t0, GATHER_SUB), :])

        def gather(i, carry):
            s0 = s_lo + GATHER_GROUP * i
            parts = [select(s0, True)] + [select(jnp.minimum(s0 + q, last_sub), s0 + q <= s_hi)
                                          for q in range(1, GATHER_GROUP)]
            acc[...] += jnp.dot(jnp.concatenate([p for p, _, _ in parts], axis=1),
                                jnp.concatenate([x for _, _, x in parts], axis=0), preferred_element_type=F32)
            rw_acc[...] += jnp.sum(functools.reduce(lambda a, c: a + c, [w for _, w, _ in parts]), axis=1,
                                   keepdims=True)
            return carry

        lax.fori_loop(0, (s_hi - s_lo + GATHER_GROUP) // GATHER_GROUP, gather, 0)
        gu = jnp.dot(acc[...].astype(BF16), wgu_bf[...], preferred_element_type=F32) + bgu_ref[0]
        dff = gu.shape[1] // 2
        gl = jnp.minimum(gu[:, :dff], SWIGLU_LIMIT)
        lin = jnp.clip(gu[:, dff:], -SWIGLU_LIMIT, SWIGLU_LIMIT)
        act = (gl * jax.nn.sigmoid(SWIGLU_ALPHA * gl) * (lin + 1.0)).astype(BF16)
        y = jnp.dot(act, wd_bf[...], preferred_element_type=F32) + bd_ref[0]
        o_ref[...] = (y * rw_acc[...]).astype(o_ref.dtype)

    @pl.when(b >= nused_ref[0])
    def _():
        o_ref[...] = jnp.zeros_like(o_ref)


def expert_blocks(h2, ti_t, wt_t, block_expert, next_expert, rank_lo, sub_lo, sub_hi, n_used, w_gate_up, b_gate_up,
                  w_down, b_down):
    n, d = h2.shape
    e, _, f2 = w_gate_up.shape
    n_blocks = block_expert.shape[0]
    whole = lambda b, be, bn, rl, lo, hi, nu: (0, 0)
    by_expert = lambda b, be, bn, rl, lo, hi, nu: (be[b], 0, 0)
    once = pl.Buffered(1)
    grid_spec = pltpu.PrefetchScalarGridSpec(
        num_scalar_prefetch=6,
        grid=(n_blocks,),
        in_specs=[pl.BlockSpec((n, d), whole, pipeline_mode=once),
                  pl.BlockSpec((SUBLANES, n), whole, pipeline_mode=once),
                  pl.BlockSpec((SUBLANES, n), whole, pipeline_mode=once),
                  pl.BlockSpec(memory_space=pl.ANY),
                  pl.BlockSpec((1, 1, f2), by_expert),
                  pl.BlockSpec(memory_space=pl.ANY),
                  pl.BlockSpec((1, 1, d), by_expert)],
        out_specs=pl.BlockSpec((EXPERT_BLOCK, d), lambda b, be, bn, rl, lo, hi, nu: (b, 0)),
        scratch_shapes=[pltpu.VMEM((EXPERT_BLOCK, d), F32), pltpu.VMEM((EXPERT_BLOCK, 1), F32),
                        pltpu.VMEM((d, f2), BF16), pltpu.VMEM((f2 // 2, d), BF16),
                        pltpu.VMEM((d, f2), F32), pltpu.VMEM((f2 // 2, d), F32),
                        pltpu.SemaphoreType.DMA((2,))],
    )
    return pl.pallas_call(
        _expert_kernel,
        grid_spec=grid_spec,
        out_shape=jax.ShapeDtypeStruct((n_blocks * EXPERT_BLOCK, d), BF16),
        compiler_params=_params("arbitrary"),
        name="expert_blocks",
    )(block_expert, next_expert, rank_lo, sub_lo, sub_hi, n_used, h2, ti_t, wt_t, w_gate_up, b_gate_up.reshape(e, 1, f2), w_down,
      b_down.reshape(e, 1, d))


def _combine_kernel(coff_ref, cgran_ref, cexp_ref, cbase_ref, yb_hbm, ti_ref, x2_ref, gf_ref, g_ref, o_ref,
                    slab, sem, acc):
    c = pl.program_id(0)
    start = coff_ref[c]
    count = coff_ref[c + 1] - start
    n_groups = (count + COMBINE_SLOTS - 1) // COMBINE_SLOTS

    def item(g, s, first=start, n=count):
        return first + jnp.minimum(g * COMBINE_SLOTS + s, n - 1)

    def copies(g, buf, first=start, n=count):
        rows = [pl.multiple_of(cgran_ref[item(g, s, first, n)] * COMBINE_GRANULE, COMBINE_GRANULE)
                for s in range(COMBINE_SLOTS)]
        return [pltpu.make_async_copy(yb_hbm.at[pl.ds(r, COMBINE_GRANULE), :],
                                      slab.at[buf, pl.ds(s * COMBINE_GRANULE, COMBINE_GRANULE), :], sem.at[buf])
                for s, r in enumerate(rows)]

    @pl.when(c == 0)
    def _():
        for cp in copies(0, 0):
            cp.start()

    acc[...] = jnp.zeros_like(acc)
    chunk = x2_ref.shape[0]
    row = lax.broadcasted_iota(jnp.int32, (COMBINE_GRANULE, chunk), 0)
    ids = ti_ref[0:TOP_K, :]
    ranks = ti_ref[TOP_K:2 * TOP_K, :]

    def group(g, carry):
        buf = g % 2

        @pl.when(g + 1 < n_groups)
        def _():
            for cp in copies(g + 1, 1 - buf):
                cp.start()

        for cp in copies(g, buf):
            cp.wait()
        pieces = []
        for s in range(COMBINE_SLOTS):
            live = g * COMBINE_SLOTS + s < count
            local = ranks - cbase_ref[item(g, s)]
            inside = (ids == jnp.where(live, cexp_ref[item(g, s)], -1)) & (local >= 0) & (local < COMBINE_GRANULE)
            hit_row = jnp.sum(jnp.where(inside, local + 1, 0), axis=0, keepdims=True) - 1
            pieces.append(jnp.where(row == hit_row, 1.0, 0.0).astype(BF16))
        onehot_t = jnp.concatenate(pieces, axis=0)
        acc[...] += lax.dot_general(onehot_t, slab[buf], (((0,), (0,)), ((), ())),
                                    preferred_element_type=F32)
        return carry

    lax.fori_loop(0, n_groups, group, 0)

    @pl.when(c + 1 < pl.num_programs(0))
    def _():
        nxt = coff_ref[c + 1]
        for cp in copies(0, 0, nxt, coff_ref[c + 2] - nxt):
            cp.start()

    y = acc[...]
    yn = y * lax.rsqrt(jnp.mean(y * y, axis=-1, keepdims=True) + EPS) * g_ref[...]
    o_ref[...] = x2_ref[...] + gf_ref[0] * yn


def combine_residual(yb, ti_t, x2, chunk_off, chunk_items, mod, mod_row_of_chunk, g_post_f):
    n, d = x2.shape
    chunk_row = lambda c, off, gran, exp, base: (c, 0)
    grid_spec = pltpu.PrefetchScalarGridSpec(
        num_scalar_prefetch=4,
        grid=(n // MOE_CHUNK,),
        in_specs=[pl.BlockSpec(memory_space=pl.ANY),
                  pl.BlockSpec((SUBLANES, MOE_CHUNK), lambda c, off, gran, exp, base: (0, c)),
                  pl.BlockSpec((MOE_CHUNK, d), chunk_row),
                  pl.BlockSpec((1, 1, d), lambda c, off, gran, exp, base: (mod_row_of_chunk(c), 0, 5)),
                  pl.BlockSpec((1, d), lambda c, off, gran, exp, base: (0, 0))],
        out_specs=pl.BlockSpec((MOE_CHUNK, d), chunk_row),
        scratch_shapes=[pltpu.VMEM((2, COMBINE_SLOTS * COMBINE_GRANULE, d), BF16),
                        pltpu.SemaphoreType.DMA((2,)),
                        pltpu.VMEM((MOE_CHUNK, d), F32)],
    )
    return pl.pallas_call(
        _combine_kernel,
        grid_spec=grid_spec,
        out_shape=jax.ShapeDtypeStruct((n, d), F32),
        compiler_params=_params("arbitrary"),
        name="combine_residual",
    )(chunk_off, *chunk_items, yb, ti_t, x2, mod, g_post_f.reshape(1, d))


def _grid_pos_embedding(rows, d):
    quarter = d // 4
    omega = POS_BASE ** (-jnp.arange(quarter, dtype=F32) / quarter)

    def emb(p):
        ang = p[:, None] * omega[None, :]
        return jnp.concatenate([jnp.sin(ang), jnp.cos(ang)], axis=-1)

    row_emb = jnp.repeat(emb(jnp.arange(rows, dtype=F32)), GRID_W, axis=0)
    col_emb = jnp.tile(emb(jnp.arange(GRID_W, dtype=F32)), (rows, 1))
    return jnp.concatenate([row_emb, col_emb], axis=-1)


def _count_le(sorted_vals, queries):
    return jnp.sum(sorted_vals <= queries[:, None], axis=1).astype(jnp.int32)


def _moe_plan(cnt_after, n_tok, tm):
    assert tm == GATHER_SUB and n_tok % MOE_CHUNK == 0 and MOE_CHUNK % tm == 0
    sub_after = cnt_after[:, 0, :N_EXPERTS].astype(jnp.int32)
    counts = sub_after[-1]
    padded = (counts + EXPERT_BLOCK - 1) // EXPERT_BLOCK * EXPERT_BLOCK
    padded_end = jnp.cumsum(padded)
    padded_start = padded_end - padded
    n_blocks = n_tok * TOP_K // EXPERT_BLOCK + N_EXPERTS
    blocks = jnp.arange(n_blocks, dtype=jnp.int32)
    block_expert = jnp.minimum(_count_le(padded_end[None, :], blocks * EXPERT_BLOCK), N_EXPERTS - 1)
    n_used = (padded_end[-1:] // EXPERT_BLOCK).astype(jnp.int32)
    of_block = (block_expert[:, None] == jnp.arange(N_EXPERTS, dtype=jnp.int32)[None, :]).astype(jnp.int32)
    r_lo = (blocks - jnp.sum(of_block * padded_start[None, :], axis=1) // EXPERT_BLOCK) * EXPERT_BLOCK
    r_hi = jnp.minimum(jnp.sum(of_block * counts[None, :], axis=1), r_lo + EXPERT_BLOCK) - 1
    through = jnp.sum(of_block[:, :, None] * sub_after.T[None, :, :], axis=1)
    after = jnp.sum(of_block * padded_end[None, :], axis=1) // EXPERT_BLOCK
    next_expert = jnp.sum((after[:, None] == blocks[None, :]).astype(jnp.int32) * block_expert[None, :], axis=1)
    next_expert = jnp.where(after < n_used[0], next_expert, -1).astype(jnp.int32)
    n_sub = sub_after.shape[0]
    sub_lo = jnp.minimum(_count_le(through, r_lo), n_sub - 1)
    sub_hi = jnp.minimum(_count_le(through, r_hi), n_sub - 1)
    nch = n_tok // MOE_CHUNK
    per_chunk = MOE_CHUNK // tm
    cb_after = sub_after[per_chunk - 1::per_chunk]
    cb_before = jnp.concatenate([jnp.zeros((1, N_EXPERTS), jnp.int32), cb_after[:-1]], axis=0)
    b_lo = ((padded_start[None, :] + cb_before) // COMBINE_GRANULE).reshape(-1)
    b_hi = ((padded_start[None, :] + cb_after - 1) // COMBINE_GRANULE).reshape(-1)
    n_it = jnp.where((cb_after > cb_before).reshape(-1), b_hi - b_lo + 1, 0)
    off_end = jnp.cumsum(n_it)
    off = off_end - n_it
    n_items = n_blocks * (EXPERT_BLOCK // COMBINE_GRANULE) + N_EXPERTS * (nch - 1)
    w = jnp.minimum(jnp.arange(n_items, dtype=jnp.int32), off_end[-1] - 1)
    chunk_end = off_end[N_EXPERTS - 1::N_EXPERTS]
    chunk_off = jnp.concatenate([jnp.zeros((1,), jnp.int32), chunk_end]).astype(jnp.int32)
    chunk_of = jnp.minimum(_count_le(chunk_end[None, :], w), nch - 1)
    chunk_expert = jnp.minimum(_count_le(off_end.reshape(nch, N_EXPERTS)[chunk_of], w), N_EXPERTS - 1)
    cell = chunk_of * N_EXPERTS + chunk_expert
    chunk_granules = (b_lo[cell] + w - off[cell]).astype(jnp.int32)
    chunk_base = chunk_granules * COMBINE_GRANULE - padded_start[chunk_expert]
    return block_expert, next_expert, r_lo, sub_lo, sub_hi, n_used, chunk_off, (chunk_granules, chunk_expert, chunk_base)


def kernel(x, c, ctx, c_ctx, w_mod, b_mod, g_pre_mix, g_post_mix, g_pre_ffn, g_post_ffn, w_in, conv_kv,
           conv_q, a_log, dt_bias, gdn_norm_g, w_proj_a, conf_dw, conf_dw_b, conf_ln_g, conf_ln_b,
           w_proj_b, w_out, w_router, b_router, w_gate_up, b_gate_up, w_down, b_down):
    batch, seq, d = x.shape
    ctx_len = ctx.shape[1]
    n_tok = batch * seq
    gw = HEADS * DV
    beta_off = HEADS * DK + gw
    state_cols = beta_off + 4 * HEADS
    q_off = state_cols

    mod_rows = 2 * SUBLANES
    cond = jnp.zeros((mod_rows, d), F32).at[:batch].set(c).at[batch].set(c_ctx)
    mod = modulation(cond, w_mod[0], b_mod[0]).reshape(mod_rows, 1, 6 * d)

    w_t = jnp.swapaxes(w_in[0], 0, 1)
    w_main = projection_weights(w_t, keep=beta_off, skip=state_cols - beta_off)
    w_ba = jnp.zeros((LANES, d), F32).at[:4 * HEADS].set(w_t[beta_off:state_cols]).astype(BF16)

    pos = _grid_pos_embedding(seq // GRID_W, d)
    x_flat = x.reshape(n_tok, d)
    tm = 1024
    tiles_per_seq = seq // tm
    proj, ba = input_projection(x_flat, pos, mod, lambda i: i // tiles_per_seq, g_pre_mix[0],
                                w_main, w_ba, tm=tm, tn=4096)
    proj_ctx, ba_ctx = input_projection(ctx.reshape(batch * ctx_len, d), None, mod, lambda i: batch,
                                        g_pre_mix[0], w_main[:beta_off], w_ba, tm=ctx_len, tn=1024)

    og = gated_deltanet(proj, ba, proj_ctx, ba_ctx, conv_kv[0], conv_q[0], a_log[0], dt_bias[0],
                        gdn_norm_g[0], batch=batch, seq=seq, ctx=ctx_len)
    uc = conformer_conv(proj, conf_dw[0], conf_dw_b[0], batch=batch, seq=seq)

    tm2 = 512
    w_r = jnp.zeros((d, LANES), F32).at[:, :N_EXPERTS].set(w_router[0]).astype(BF16)
    b_r = jnp.zeros((1, LANES), F32).at[0, :N_EXPERTS].set(b_router[0])
    x2, h2, top_i, top_w, cnt_after = mixer_output(
        og, uc, proj, x_flat, pos, mod, lambda i: i // (seq // tm2), g_post_mix[0], g_pre_ffn[0],
        conf_ln_g[0], conf_ln_b[0], w_proj_a[0].astype(BF16), w_proj_b[0].astype(BF16),
        w_out[0].astype(BF16), w_r, b_r, tm=tm2, sub=GATHER_SUB)

    block_expert, next_expert, rank_lo, sub_lo, sub_hi, n_used, chunk_off, chunk_items = _moe_plan(
        cnt_after, n_tok, GATHER_SUB)
    yb = expert_blocks(h2, top_i, top_w, block_expert, next_expert, rank_lo, sub_lo, sub_hi, n_used, w_gate_up[0],
                       b_gate_up[0], w_down[0], b_down[0])
    out = combine_residual(yb, top_i, x2, chunk_off, chunk_items, mod, lambda c: c // (seq // MOE_CHUNK),
                           g_post_ffn[0])
    return out.reshape(batch, seq, d)
```
